```python
import jax, jax.numpy as jnp
from jax import lax
import numpy as np

D_MODEL = 1024
BATCH = 32
SEQ = 2048
DEPTH = 1

CHUNK = 64
CONV_WIDTH = D_MODEL // 2
CONV_GROUPS = 8
CONV_K = 3
ATTN_HEADS = 8
HEAD_DIM = 64
ATTN_WIDTH = ATTN_HEADS * HEAD_DIM
N_BRANCH = 2
Q_BLOCK = 128
FFN_HIDDEN = 2816
FFN_CONV_K = 3
EPS = 1e-6
IN_SPLITS = (CONV_WIDTH, CONV_WIDTH, CONV_WIDTH, ATTN_WIDTH, ATTN_WIDTH, ATTN_WIDTH, ATTN_HEADS, N_BRANCH * D_MODEL)
IN_WIDTH = sum(IN_SPLITS)

kernel_name = "hybrid_gated_conv_fox_convffn"


def rms_norm(x, g):
    x32 = x.astype(jnp.float32)
    y = x32 * lax.rsqrt(jnp.mean(x32 * x32, axis=-1, keepdims=True) + EPS)
    return y.astype(x.dtype) * g


def causal_dwconv(x, w):
    K = w.shape[0]
    S = x.shape[1]
    xp = jnp.pad(x, ((0, 0), (K - 1, 0), (0, 0)))
    y = xp[:, K - 1:K - 1 + S, :] * w[K - 1]
    for k in range(K - 1):
        y = y + xp[:, k:k + S, :] * w[k]
    return y


def forgetting_attention(q, k, v, log_f):
    B, S, H, hd = q.shape
    scale = 1.0 / np.sqrt(hd).astype(np.float32)
    F = jnp.cumsum(log_f, axis=1).transpose(0, 2, 1)
    qh = q.transpose(0, 2, 1, 3).astype(jnp.float32) * scale
    kh = k.transpose(0, 2, 1, 3).astype(jnp.float32)
    vh = v.transpose(0, 2, 1, 3).astype(jnp.float32)
    nb = S // Q_BLOCK
    q_blocks = qh.reshape(B, H, nb, Q_BLOCK, hd).transpose(2, 0, 1, 3, 4)
    fq_blocks = F.reshape(B, H, nb, Q_BLOCK).transpose(2, 0, 1, 3)
    k_pos = jnp.arange(S)

    def one_block(args):
        q_blk, fq_blk, i = args
        q_pos = i * Q_BLOCK + jnp.arange(Q_BLOCK)
        s = jnp.einsum('bhqd,bhkd->bhqk', q_blk, kh) + fq_blk[..., None] - F[:, :, None, :]
        s = jnp.where(k_pos[None, :] <= q_pos[:, None], s, -jnp.inf)
        p = jax.nn.softmax(s, axis=-1)
        return jnp.einsum('bhqk,bhkd->bhqd', p, vh)

    o = lax.map(one_block, (q_blocks, fq_blocks, jnp.arange(nb)))
    o = o.transpose(1, 0, 3, 2, 4).reshape(B, S, H * hd)
    return o.astype(q.dtype)


def _fwd_setup_inputs(seed: int = 0) -> dict:
    key = jax.random.key(seed)
    ks = jax.random.split(key, 16)
    f32 = jnp.float32
    L = DEPTH
    x = jax.random.normal(ks[0], (BATCH, SEQ, D_MODEL), f32)
    norm_mix_g = 1.0 + 0.02 * jax.random.normal(ks[1], (L, D_MODEL), f32)
    w_in = jax.random.normal(ks[2], (L, D_MODEL, IN_WIDTH), f32) * D_MODEL ** -0.5
    b_f = 2.0 + 0.5 * jax.random.normal(ks[3], (L, ATTN_HEADS), f32)
    b_gate = 0.02 * jax.random.normal(ks[4], (L, N_BRANCH * D_MODEL), f32)
    conv_mix_w = jax.random.normal(ks[5], (L, CONV_K, CONV_WIDTH), f32) * CONV_K ** -0.5
    w_out_conv = jax.random.normal(ks[6], (L, CONV_WIDTH, D_MODEL), f32) * CONV_WIDTH ** -0.5
    w_out_attn = jax.random.normal(ks[7], (L, ATTN_WIDTH, D_MODEL), f32) * ATTN_WIDTH ** -0.5
    w_o = jax.random.normal(ks[8], (L, D_MODEL, D_MODEL), f32) * D_MODEL ** -0.5
    norm_ffn_g = 1.0 + 0.02 * jax.random.normal(ks[9], (L, D_MODEL), f32)
    w_up = jax.random.normal(ks[10], (L, D_MODEL, 2 * FFN_HIDDEN), f32) * D_MODEL ** -0.5
    conv_ffn_w = jax.random.normal(ks[11], (L, FFN_CONV_K, 2 * FFN_HIDDEN), f32) * FFN_CONV_K ** -0.5
    w_down = jax.random.normal(ks[12], (L, FFN_HIDDEN, D_MODEL), f32) * FFN_HIDDEN ** -0.5
    norm_f_g = 1.0 + 0.02 * jax.random.normal(ks[13], (D_MODEL,), f32)
    return {"x": x, "norm_mix_g": norm_mix_g, "w_in": w_in, "b_f": b_f, "b_gate": b_gate,
            "conv_mix_w": conv_mix_w, "w_out_conv": w_out_conv, "w_out_attn": w_out_attn,
            "w_o": w_o, "norm_ffn_g": norm_ffn_g, "w_up": w_up, "conv_ffn_w": conv_ffn_w,
            "w_down": w_down, "norm_f_g": norm_f_g}


def _fwd_reference(x, norm_mix_g, w_in, b_f, b_gate, conv_mix_w, w_out_conv, w_out_attn,
              w_o, norm_ffn_g, w_up, conv_ffn_w, w_down, norm_f_g):
    B, S, _ = x.shape
    split_idx = list(np.cumsum(IN_SPLITS)[:-1])
    for l in range(DEPTH):
        h = rms_norm(x, norm_mix_g[l])
        proj = h @ w_in[l]
        cb, cc, cin, q, k, v, f_logit, g_logit = jnp.split(proj, split_idx, axis=-1)
        u = causal_dwconv(cc * cin, conv_mix_w[l])
        y_conv = (cb * u) @ w_out_conv[l]
        log_f = jax.nn.log_sigmoid((f_logit + b_f[l]).astype(jnp.float32))
        o = forgetting_attention(q.reshape(B, S, ATTN_HEADS, HEAD_DIM),
                                 k.reshape(B, S, ATTN_HEADS, HEAD_DIM),
                                 v.reshape(B, S, ATTN_HEADS, HEAD_DIM), log_f)
        y_attn = o @ w_out_attn[l]
        gates = jax.nn.sigmoid(g_logit + b_gate[l])
        g_conv, g_attn = jnp.split(gates, 2, axis=-1)
        x = x + (g_conv * y_conv + g_attn * y_attn) @ w_o[l]
        h = rms_norm(x, norm_ffn_g[l])
        up = causal_dwconv(h @ w_up[l], conv_ffn_w[l])
        a, b = jnp.split(up, 2, axis=-1)
        x = x + (jax.nn.silu(a) * b) @ w_down[l]
    return rms_norm(x, norm_f_g)


import jax as _jax
import jax.numpy as _jnp

TWIN_FORMAT = 'train_step'
FWD_PARAMS = ['x', 'norm_mix_g', 'w_in', 'b_f', 'b_gate', 'conv_mix_w', 'w_out_conv', 'w_out_attn', 'w_o', 'norm_ffn_g', 'w_up', 'conv_ffn_w', 'w_down', 'norm_f_g']
TWIN_WEIGHTS = ['norm_mix_g', 'w_in', 'b_f', 'b_gate', 'conv_mix_w', 'w_out_conv', 'w_out_attn', 'w_o', 'norm_ffn_g', 'w_up', 'conv_ffn_w', 'w_down', 'norm_f_g']
TWIN_DIFF_INPUT = 'x'
TWIN_INPUTS = ['x', 'norm_mix_g', 'w_in', 'b_f', 'b_gate', 'conv_mix_w', 'w_out_conv', 'w_out_attn', 'w_o', 'norm_ffn_g', 'w_up', 'conv_ffn_w', 'w_down', 'norm_f_g', 'loss_target', 'm_norm_mix_g', 'm_w_in', 'm_b_f', 'm_b_gate', 'm_conv_mix_w', 'm_w_out_conv', 'm_w_out_attn', 'm_w_o', 'm_norm_ffn_g', 'm_w_up', 'm_conv_ffn_w', 'm_w_down', 'm_norm_f_g', 'v_norm_mix_g', 'v_w_in', 'v_b_f', 'v_b_gate', 'v_conv_mix_w', 'v_w_out_conv', 'v_w_out_attn', 'v_w_o', 'v_norm_ffn_g', 'v_w_up', 'v_conv_ffn_w', 'v_w_down', 'v_norm_f_g']
TWIN_OUTPUTS = ['loss', 'grad_x', 'grad_norm_mix_g', 'grad_w_in', 'grad_b_f', 'grad_b_gate', 'grad_conv_mix_w', 'grad_w_out_conv', 'grad_w_out_attn', 'grad_w_o', 'grad_norm_ffn_g', 'grad_w_up', 'grad_conv_ffn_w', 'grad_w_down', 'grad_norm_f_g', 'delta_norm_mix_g', 'delta_w_in', 'delta_b_f', 'delta_b_gate', 'delta_conv_mix_w', 'delta_w_out_conv', 'delta_w_out_attn', 'delta_w_o', 'delta_norm_ffn_g', 'delta_w_up', 'delta_conv_ffn_w', 'delta_w_down', 'delta_norm_f_g', 'new_m_norm_mix_g', 'new_m_w_in', 'new_m_b_f', 'new_m_b_gate', 'new_m_conv_mix_w', 'new_m_w_out_conv', 'new_m_w_out_attn', 'new_m_w_o', 'new_m_norm_ffn_g', 'new_m_w_up', 'new_m_conv_ffn_w', 'new_m_w_down', 'new_m_norm_f_g', 'new_v_norm_mix_g', 'new_v_w_in', 'new_v_b_f', 'new_v_b_gate', 'new_v_conv_mix_w', 'new_v_w_out_conv', 'new_v_w_out_attn', 'new_v_w_o', 'new_v_norm_ffn_g', 'new_v_w_up', 'new_v_conv_ffn_w', 'new_v_w_down', 'new_v_norm_f_g']
TWIN_LEAF_KINDS = {'loss': 'loss', 'grad_x': 'grad_x', 'grad_norm_mix_g': 'grad_w', 'grad_w_in': 'grad_w', 'grad_b_f': 'grad_w', 'grad_b_gate': 'grad_w', 'grad_conv_mix_w': 'grad_w', 'grad_w_out_conv': 'grad_w', 'grad_w_out_attn': 'grad_w', 'grad_w_o': 'grad_w', 'grad_norm_ffn_g': 'grad_w', 'grad_w_up': 'grad_w', 'grad_conv_ffn_w': 'grad_w', 'grad_w_down': 'grad_w', 'grad_norm_f_g': 'grad_w', 'delta_norm_mix_g': 'delta_w', 'delta_w_in': 'delta_w', 'delta_b_f': 'delta_w', 'delta_b_gate': 'delta_w', 'delta_conv_mix_w': 'delta_w', 'delta_w_out_conv': 'delta_w', 'delta_w_out_attn': 'delta_w', 'delta_w_o': 'delta_w', 'delta_norm_ffn_g': 'delta_w', 'delta_w_up': 'delta_w', 'delta_conv_ffn_w': 'delta_w', 'delta_w_down': 'delta_w', 'delta_norm_f_g': 'delta_w', 'new_m_norm_mix_g': 'new_m', 'new_m_w_in': 'new_m', 'new_m_b_f': 'new_m', 'new_m_b_gate': 'new_m', 'new_m_conv_mix_w': 'new_m', 'new_m_w_out_conv': 'new_m', 'new_m_w_out_attn': 'new_m', 'new_m_w_o': 'new_m', 'new_m_norm_ffn_g': 'new_m', 'new_m_w_up': 'new_m', 'new_m_conv_ffn_w': 'new_m', 'new_m_w_down': 'new_m', 'new_m_norm_f_g': 'new_m', 'new_v_norm_mix_g': 'new_v', 'new_v_w_in': 'new_v', 'new_v_b_f': 'new_v', 'new_v_b_gate': 'new_v', 'new_v_conv_mix_w': 'new_v', 'new_v_w_out_conv': 'new_v', 'new_v_w_out_attn': 'new_v', 'new_v_w_o': 'new_v', 'new_v_norm_ffn_g': 'new_v', 'new_v_w_up': 'new_v', 'new_v_conv_ffn_w': 'new_v', 'new_v_w_down': 'new_v', 'new_v_norm_f_g': 'new_v'}


def _forward(args):
    return _fwd_reference(*[args[k] for k in FWD_PARAMS])


def _output_shape():
    out = _jax.eval_shape(lambda: _forward(_fwd_setup_inputs(0)))
    return out.shape, out.dtype

N_MICROBATCH = 1
ADAM_LR = 0.001
ADAM_B1 = 0.9
ADAM_B2 = 0.999
ADAM_EPS = 1e-08
ADAM_WD = 0.01
ADAM_STEP = 10
PER_EXAMPLE_BATCH_AXIS = {'x': 0, 'loss_target': 0}
SHARED_INPUTS = []
_WEIGHT_DTYPES = {'norm_mix_g': _jnp.float32, 'w_in': _jnp.float32, 'b_f': _jnp.float32, 'b_gate': _jnp.float32, 'conv_mix_w': _jnp.float32, 'w_out_conv': _jnp.float32, 'w_out_attn': _jnp.float32, 'w_o': _jnp.float32, 'norm_ffn_g': _jnp.float32, 'w_up': _jnp.float32, 'conv_ffn_w': _jnp.float32, 'w_down': _jnp.float32, 'norm_f_g': _jnp.float32}
MOMENT_SCALE = {'norm_mix_g': 2.511110e-01, 'w_in': 1.107635e-01, 'b_f': 4.746007e-01, 'b_gate': 4.399731e-02, 'conv_mix_w': 1.813746e-01, 'w_out_conv': 1.347128e-01, 'w_out_attn': 5.526267e-02, 'w_o': 1.426192e-01, 'norm_ffn_g': 1.734522e-01, 'w_up': 7.282455e-02, 'conv_ffn_w': 7.176429e-02, 'w_down': 1.188763e-01, 'norm_f_g': 6.402243e+01}


def _to_microbatches(a, axis):
    t = _jnp.moveaxis(a, axis, 0)
    t = t.reshape((N_MICROBATCH, t.shape[0] // N_MICROBATCH) + t.shape[1:])
    return _jnp.moveaxis(t, 1, axis + 1)


def setup_inputs(seed: int = 0) -> dict:
    inp = _fwd_setup_inputs(seed)
    key = _jax.random.fold_in(_jax.random.key(seed), 7919)
    shape, _ = _output_shape()
    out = dict(inp)
    out["loss_target"] = _jax.random.normal(_jax.random.fold_in(key, 0), shape, _jnp.float32)
    for i, name in enumerate(TWIN_WEIGHTS):
        w = inp[name].astype(_jnp.float32)
        if MOMENT_SCALE is None:
            s = _jnp.sqrt(_jnp.mean(_jnp.square(w)) + 1e-30)
        else:
            s = MOMENT_SCALE[name]
        km, kv = _jax.random.split(_jax.random.fold_in(key, i + 1))
        out[name] = w
        out["m_" + name] = s * _jax.random.normal(km, w.shape, _jnp.float32)
        out["v_" + name] = (s * s) * _jax.random.uniform(kv, w.shape, _jnp.float32, 0.5, 1.5)
    if N_MICROBATCH > 1:
        for name, axis in PER_EXAMPLE_BATCH_AXIS.items():
            out[name] = _to_microbatches(out[name], axis)
    return {'x': out['x'], 'norm_mix_g': out['norm_mix_g'], 'w_in': out['w_in'], 'b_f': out['b_f'], 'b_gate': out['b_gate'], 'conv_mix_w': out['conv_mix_w'], 'w_out_conv': out['w_out_conv'], 'w_out_attn': out['w_out_attn'], 'w_o': out['w_o'], 'norm_ffn_g': out['norm_ffn_g'], 'w_up': out['w_up'], 'conv_ffn_w': out['conv_ffn_w'], 'w_down': out['w_down'], 'norm_f_g': out['norm_f_g'], 'loss_target': out['loss_target'], 'm_norm_mix_g': out['m_norm_mix_g'], 'm_w_in': out['m_w_in'], 'm_b_f': out['m_b_f'], 'm_b_gate': out['m_b_gate'], 'm_conv_mix_w': out['m_conv_mix_w'], 'm_w_out_conv': out['m_w_out_conv'], 'm_w_out_attn': out['m_w_out_attn'], 'm_w_o': out['m_w_o'], 'm_norm_ffn_g': out['m_norm_ffn_g'], 'm_w_up': out['m_w_up'], 'm_conv_ffn_w': out['m_conv_ffn_w'], 'm_w_down': out['m_w_down'], 'm_norm_f_g': out['m_norm_f_g'], 'v_norm_mix_g': out['v_norm_mix_g'], 'v_w_in': out['v_w_in'], 'v_b_f': out['v_b_f'], 'v_b_gate': out['v_b_gate'], 'v_conv_mix_w': out['v_conv_mix_w'], 'v_w_out_conv': out['v_w_out_conv'], 'v_w_out_attn': out['v_w_out_attn'], 'v_w_o': out['v_w_o'], 'v_norm_ffn_g': out['v_norm_ffn_g'], 'v_w_up': out['v_w_up'], 'v_conv_ffn_w': out['v_conv_ffn_w'], 'v_w_down': out['v_w_down'], 'v_norm_f_g': out['v_norm_f_g']}


def _loss(weights, diff, rest, loss_target):
    with _jax.named_scope("forward"):
        args = {**rest, TWIN_DIFF_INPUT: diff, **{k: w.astype(_WEIGHT_DTYPES[k]) for k, w in weights.items()}}
        y = _forward(args)
    with _jax.named_scope("loss_head"):
        err = _jnp.square(y.astype(_jnp.float32) - loss_target)
        return 0.5 * _jnp.sum(_jnp.mean(err, axis=-1)) if err.ndim else 0.5 * err


def _adamw(w, g, m, v):
    m = ADAM_B1 * m + (1.0 - ADAM_B1) * g
    v = ADAM_B2 * v + (1.0 - ADAM_B2) * _jnp.square(g)
    m_hat = m / (1.0 - ADAM_B1 ** ADAM_STEP)
    v_hat = v / (1.0 - ADAM_B2 ** ADAM_STEP)
    delta = -ADAM_LR * (m_hat / (_jnp.sqrt(v_hat) + ADAM_EPS) + ADAM_WD * w)
    return delta, m, v


def reference(x, norm_mix_g, w_in, b_f, b_gate, conv_mix_w, w_out_conv, w_out_attn, w_o, norm_ffn_g, w_up, conv_ffn_w, w_down, norm_f_g, loss_target, m_norm_mix_g, m_w_in, m_b_f, m_b_gate, m_conv_mix_w, m_w_out_conv, m_w_out_attn, m_w_o, m_norm_ffn_g, m_w_up, m_conv_ffn_w, m_w_down, m_norm_f_g, v_norm_mix_g, v_w_in, v_b_f, v_b_gate, v_conv_mix_w, v_w_out_conv, v_w_out_attn, v_w_o, v_norm_ffn_g, v_w_up, v_conv_ffn_w, v_w_down, v_norm_f_g):
    given = dict(x=x, norm_mix_g=norm_mix_g, w_in=w_in, b_f=b_f, b_gate=b_gate, conv_mix_w=conv_mix_w, w_out_conv=w_out_conv, w_out_attn=w_out_attn, w_o=w_o, norm_ffn_g=norm_ffn_g, w_up=w_up, conv_ffn_w=conv_ffn_w, w_down=w_down, norm_f_g=norm_f_g, loss_target=loss_target, m_norm_mix_g=m_norm_mix_g, m_w_in=m_w_in, m_b_f=m_b_f, m_b_gate=m_b_gate, m_conv_mix_w=m_conv_mix_w, m_w_out_conv=m_w_out_conv, m_w_out_attn=m_w_out_attn, m_w_o=m_w_o, m_norm_ffn_g=m_norm_ffn_g, m_w_up=m_w_up, m_conv_ffn_w=m_conv_ffn_w, m_w_down=m_w_down, m_norm_f_g=m_norm_f_g, v_norm_mix_g=v_norm_mix_g, v_w_in=v_w_in, v_b_f=v_b_f, v_b_gate=v_b_gate, v_conv_mix_w=v_conv_mix_w, v_w_out_conv=v_w_out_conv, v_w_out_attn=v_w_out_attn, v_w_o=v_w_o, v_norm_ffn_g=v_norm_ffn_g, v_w_up=v_w_up, v_conv_ffn_w=v_conv_ffn_w, v_w_down=v_w_down, v_norm_f_g=v_norm_f_g)
    weights = {n: given[n] for n in TWIN_WEIGHTS}
    shared = {n: given[n] for n in SHARED_INPUTS}
    per_example = {n: given[n] for n in ['x']}
    grad_fn = _jax.value_and_grad(_loss, argnums=(0, 1))

    def one_microbatch(ex, loss_target):
        ex = dict(ex)
        diff = ex.pop(TWIN_DIFF_INPUT)
        return grad_fn(weights, diff, {**shared, **ex}, loss_target)

    if N_MICROBATCH == 1:
        loss, (grad_w, grad_x) = one_microbatch(per_example, given["loss_target"])
    else:
        def body(carry, xs):
            loss_sum, grad_sum = carry
            l_k, (gw_k, gx_k) = one_microbatch(xs[0], xs[1])
            with _jax.named_scope("update"):
                return (loss_sum + l_k, _jax.tree.map(_jnp.add, grad_sum, gw_k)), gx_k

        init = (_jnp.zeros((), _jnp.float32), _jax.tree.map(_jnp.zeros_like, weights))
        (loss, grad_w), grad_x = _jax.lax.scan(body, init, (per_example, given["loss_target"]))
    with _jax.named_scope("update"):
        delta_w, new_m, new_v = {}, {}, {}
        for n in TWIN_WEIGHTS:
            delta_w[n], new_m[n], new_v[n] = _adamw(weights[n], grad_w[n], given["m_" + n], given["v_" + n])
    return (loss, grad_x, *[grad_w[n] for n in TWIN_WEIGHTS], *[delta_w[n] for n in TWIN_WEIGHTS],
            *[new_m[n] for n in TWIN_WEIGHTS], *[new_v[n] for n in TWIN_WEIGHTS])
```

```python
import functools

import jax
import jax.numpy as jnp
from jax import lax
from jax.experimental import pallas as pl
from jax.experimental.pallas import tpu as pltpu

F32, BF16 = jnp.float32, jnp.bfloat16
EPS = 1e-6
ADAM_LR, ADAM_B1, ADAM_B2, ADAM_EPS, ADAM_WD, ADAM_STEP = 0.001, 0.9, 0.999, 1e-08, 0.01, 10
N_DEV = 8
LANES = 128
V7X_VMEM_LIMIT = 56 * 1024 * 1024
MESH = pl.DeviceIdType.MESH
SDS = jax.ShapeDtypeStruct


def _tile(n, target, mult=LANES):
    best = None
    for t in range(mult, min(n, target) + 1, mult):
        if n % t == 0:
            best = t
    return best if best is not None else n


def _resident(shape):
    return pl.BlockSpec(shape, lambda *_: (0,) * len(shape), pipeline_mode=pl.Buffered(1))


def _cparams(n_axes=1):
    return pltpu.CompilerParams(dimension_semantics=("arbitrary",) * n_axes, vmem_limit_bytes=V7X_VMEM_LIMIT)


def _dot(a, b):
    return jnp.dot(a, b, preferred_element_type=F32)


def _dot_tn(a, b):
    return lax.dot_general(a, b, (((0,), (0,)), ((), ())), preferred_element_type=F32)


def _dot_nt(a, b):
    return lax.dot_general(a, b, (((1,), (1,)), ((), ())), preferred_element_type=F32)


def _sigmoid(x):
    return 1.0 / (1.0 + jnp.exp(-x))


def _rms(x):
    return lax.rsqrt(jnp.mean(x * x, axis=-1, keepdims=True) + EPS)


def _taps_back(z, r6, r7):
    tm = z.shape[0]
    row = lax.broadcasted_iota(jnp.int32, (tm, 1), 0)
    z1 = jnp.where(row == 0, r7, pltpu.roll(z, 1, 0))
    z2 = jnp.where(row == 0, r6, jnp.where(row == 1, r7, pltpu.roll(z, 2, 0)))
    return z1, z2


def _taps_ahead(d, h0, h1):
    tm = d.shape[0]
    row = lax.broadcasted_iota(jnp.int32, (tm, 1), 0)
    d1 = jnp.where(row == tm - 1, h0, pltpu.roll(d, tm - 1, 0))
    d2 = jnp.where(row == tm - 2, h0, jnp.where(row == tm - 1, h1, pltpu.roll(d, tm - 2, 0)))
    return d1, d2


def _tri_dot(tri, x):
    hi = x.astype(BF16)
    r = x - hi.astype(F32)
    mid = r.astype(BF16)
    lo = (r - mid.astype(F32)).astype(BF16)
    return (_dot(tri, lo) + _dot(tri, mid)) + _dot(tri, hi)


def _place():
    return lax.axis_index("x"), lax.axis_index("y"), lax.axis_index("c")


def _all_gather(x, name):
    def body(x_ref, out_ref, send_sems, recv_sems, local_sem):
        px, py, pc = _place()
        me, sibling = (px, py, pc), (px, py, 1 - pc)
        chips = [(1 - px, py), (px, 1 - py), (1 - px, 1 - py)]

        def slot(qx, qy, qc):
            return out_ref.at[4 * qx + 2 * qy + qc]

        def copy(k, block, to, src=None):
            return pltpu.make_async_remote_copy(
                src_ref=slot(*block) if src is None else src, dst_ref=slot(*block),
                send_sem=send_sems.at[k], recv_sem=recv_sems.at[k], device_id=to, device_id_type=MESH)

        mine = pltpu.make_async_copy(x_ref, slot(*me), local_sem)
        mine.start()
        first = [copy(0, me, sibling, src=x_ref)]
        first += [copy(1 + j, me, (*chip, pc), src=x_ref) for j, chip in enumerate(chips)]
        for cp in first:
            cp.start()
        passed = [copy(4 + j, (*chip, pc), sibling) for j, chip in enumerate(chips)]
        for j, chip in enumerate(chips):
            copy(1 + j, (*chip, pc), me).wait_recv()
            passed[j].start()
        copy(0, sibling, me).wait_recv()
        for j, chip in enumerate(chips):
            copy(4 + j, (*chip, 1 - pc), me).wait_recv()
        for cp in first + passed:
            cp.wait_send()
        mine.wait()

    return pl.pallas_call(
        body, name=name,
        out_shape=SDS((N_DEV,) + x.shape, x.dtype),
        in_specs=[pl.BlockSpec(memory_space=pl.ANY)],
        out_specs=pl.BlockSpec(memory_space=pl.ANY),
        scratch_shapes=[pltpu.SemaphoreType.DMA((7,)), pltpu.SemaphoreType.DMA((7,)), pltpu.SemaphoreType.DMA],
    )(x)


def _sibling_exchange(g, name):
    def body(g_ref, out_ref, send_sem, recv_sem):
        px, py, pc = _place()
        cp = pltpu.make_async_remote_copy(
            src_ref=g_ref.at[1 - pc], dst_ref=out_ref, send_sem=send_sem, recv_sem=recv_sem,
            device_id=(px, py, 1 - pc), device_id_type=MESH)
        cp.start()
        cp.wait()

    return pl.pallas_call(
        body, name=name,
        out_shape=SDS(g.shape[1:], g.dtype),
        in_specs=[pl.BlockSpec(memory_space=pl.ANY)],
        out_specs=pl.BlockSpec(memory_space=pl.ANY),
        scratch_shapes=[pltpu.SemaphoreType.DMA, pltpu.SemaphoreType.DMA],
    )(g)


def _chip_exchange(p, name):
    def body(p_ref, out_ref, send_sems, recv_sems):
        px, py, pc = _place()
        chips = [(1 - px, py), (px, 1 - py), (1 - px, 1 - py)]
        copies = [
            pltpu.make_async_remote_copy(
                src_ref=p_ref.at[2 * qx + qy], dst_ref=out_ref.at[k], send_sem=send_sems.at[k], recv_sem=recv_sems.at[k],
                device_id=(qx, qy, pc), device_id_type=MESH)
            for k, (qx, qy) in enumerate(chips)]
        for cp in copies:
            cp.start()
        for cp in copies:
            cp.wait()

    return pl.pallas_call(
        body, name=name,
        out_shape=SDS((3,) + p.shape[1:], p.dtype),
        in_specs=[pl.BlockSpec(memory_space=pl.ANY)],
        out_specs=pl.BlockSpec(memory_space=pl.ANY),
        scratch_shapes=[pltpu.SemaphoreType.DMA((3,)), pltpu.SemaphoreType.DMA((3,))],
    )(p)


def _pair_sum(g, recv, ids, name):
    _, _, R, L = g.shape
    tr = _tile(R, 512, 8)

    def body(ids_ref, g_ref, r_ref, own_ref, pb_ref):
        s = g_ref[...] + r_ref[...]
        pb_ref[...] = s.astype(BF16)

        @pl.when(pl.program_id(1) == ids_ref[1])
        def _():
            own_ref[...] = s

    return pl.pallas_call(
        body, name=name,
        grid_spec=pltpu.PrefetchScalarGridSpec(
            num_scalar_prefetch=1, grid=(R // tr, 4),
            in_specs=[pl.BlockSpec((None, None, tr, L), lambda r, j, ids: (ids[0], j, r, 0)),
                      pl.BlockSpec((None, tr, L), lambda r, j, ids: (j, r, 0))],
            out_specs=[pl.BlockSpec((tr, L), lambda r, j, ids: (r, 0)),
                       pl.BlockSpec((None, tr, L), lambda r, j, ids: (j, r, 0))]),
        out_shape=[SDS((R, L), F32), SDS((4, R, L), BF16)],
        compiler_params=_cparams(2),
    )(ids, g, recv)


def _final_sum(own, recv, name):
    R, L = own.shape
    tr = _tile(R, 512, 8)

    def body(o_ref, r_ref, out_ref):
        out_ref[...] = ((o_ref[...] + r_ref[0].astype(F32)) + r_ref[1].astype(F32)) + r_ref[2].astype(F32)

    return pl.pallas_call(
        body, name=name, grid=(R // tr,),
        in_specs=[pl.BlockSpec((tr, L), lambda r: (r, 0)), pl.BlockSpec((3, tr, L), lambda r: (0, r, 0))],
        out_specs=pl.BlockSpec((tr, L), lambda r: (r, 0)),
        out_shape=SDS((R, L), F32),
        compiler_params=_cparams(1),
    )(own, recv)


def _sum8(a, name):
    def body(a_ref, out_ref):
        s = a_ref[0]
        for d in range(1, N_DEV):
            s = s + a_ref[d]
        out_ref[...] = s

    return pl.pallas_call(body, name=name, out_shape=SDS(a.shape[1:], F32))(a)


def _inproj_fwd(x, g1, wm, wf, tm):
    T, D = x.shape
    NM, NF = wm.shape[1], wf.shape[1]
    ch = _tile(NM, 1024)

    def body(x_ref, g_ref, wm_ref, wf_ref, pm_ref, fl_ref, h_ref):
        xv = x_ref[...]
        h = (xv * _rms(xv) * g_ref[...]).astype(BF16)
        h_ref[...] = h
        for c in range(0, NM, ch):
            pm_ref[:, c:c + ch] = _dot(h, wm_ref[:, c:c + ch]).astype(BF16)
        fl_ref[...] = _dot(h, wf_ref[...])

    return pl.pallas_call(
        body, name="inproj_fwd", grid=(T // tm,),
        in_specs=[pl.BlockSpec((tm, D), lambda i: (i, 0)), _resident((1, D)), _resident((D, NM)), _resident((D, NF))],
        out_specs=[pl.BlockSpec((tm, NM), lambda i: (i, 0)), pl.BlockSpec((tm, NF), lambda i: (i, 0)),
                   pl.BlockSpec((tm, D), lambda i: (i, 0))],
        out_shape=[SDS((T, NM), BF16), SDS((T, NF), F32), SDS((T, D), BF16)],
        compiler_params=_cparams(1),
    )(x, g1, wm, wf)


def _log_sigmoid(x):
    return jnp.minimum(x, 0.0) - jnp.log(1.0 + jnp.exp(-jnp.abs(x)))


def _forget_cumsum(fl, bf, S):
    T, NF = fl.shape
    ch = _tile(S, 256, 8)

    def body(fl_ref, bf_ref, f_ref):
        row = lax.broadcasted_iota(jnp.int32, (ch, ch), 0)
        col = lax.broadcasted_iota(jnp.int32, (ch, ch), 1)
        tri = (col <= row).astype(BF16)
        carry = jnp.zeros((1, NF), F32)
        for c in range(0, S, ch):
            lf = _log_sigmoid(fl_ref[c:c + ch, :] + bf_ref[...])
            f_ref[c:c + ch, :] = _tri_dot(tri, lf) + carry
            carry = carry + jnp.sum(lf, axis=0, keepdims=True)

    return pl.pallas_call(
        body, name="forget_cumsum", grid=(T // S,),
        in_specs=[pl.BlockSpec((S, NF), lambda b: (b, 0)), _resident((1, NF))],
        out_specs=pl.BlockSpec((S, NF), lambda b: (b, 0)),
        out_shape=SDS((T, NF), F32),
        compiler_params=_cparams(1),
    )(fl, bf)


def _attn_fwd(q, k, v, fcol, frow, tq):
    BH, S, hd = q.shape
    scale = float(hd) ** -0.5
    nq = S // tq

    def body(q_ref, k_ref, v_ref, fc_ref, fr_ref, o_ref, lse_ref):
        for i in range(nq):
            rs, kend = slice(i * tq, (i + 1) * tq), (i + 1) * tq
            qs = q_ref[rs, :] * scale
            s = _dot_nt(qs, k_ref[0:kend, :]) + fc_ref[rs, :] - fr_ref[:, 0:kend]
            row = i * tq + lax.broadcasted_iota(jnp.int32, (tq, kend), 0)
            col = lax.broadcasted_iota(jnp.int32, (tq, kend), 1)
            s = jnp.where(col <= row, s, -1e30)
            m = jnp.max(s, axis=1, keepdims=True)
            p = jnp.exp(s - m)
            l = jnp.sum(p, axis=1, keepdims=True)
            o_ref[rs, :] = (_dot(p.astype(BF16), v_ref[0:kend, :]) / l).astype(BF16)
            lse_ref[rs, :] = m + jnp.log(l)

    seq = lambda w: pl.BlockSpec((None, S, w), lambda b: (b, 0, 0))
    return pl.pallas_call(
        body, name="attn_fwd", grid=(BH,),
        in_specs=[seq(hd), seq(hd), seq(hd), seq(1), pl.BlockSpec((None, 1, S), lambda b: (b, 0, 0))],
        out_specs=[seq(hd), seq(1)],
        out_shape=[SDS((BH, S, hd), BF16), SDS((BH, S, 1), F32)],
        compiler_params=_cparams(1),
    )(q, k, v, fcol, frow)


def _merge_fwd(pm, o, x, wc, bg, woc, woa, wo, S, tm):
    T, D = x.shape
    CW, AW = woc.shape[0], woa.shape[0]
    tps = S // tm

    def body(cb_ref, cc_ref, cin_ref, gc_ref, ga_ref, o_ref, x_ref, wc_ref, bg_ref, woc_ref, woa_ref, wo_ref,
             x2_ref, u_ref, m_ref, ycin_ref, yc_ref, ya_ref, tail_ref):
        @pl.when(pl.program_id(0) % tps == 0)
        def _():
            tail_ref[...] = jnp.zeros_like(tail_ref)

        z = cc_ref[...].astype(F32) * cin_ref[...].astype(F32)
        z1, z2 = _taps_back(z, tail_ref[6:7, :], tail_ref[7:8, :])
        tail_ref[...] = z[tm - 8:tm, :]
        u = (z * wc_ref[2:3, :] + z2 * wc_ref[0:1, :]) + z1 * wc_ref[1:2, :]
        u_ref[...] = u.astype(BF16)
        ycin = (cb_ref[...].astype(F32) * u).astype(BF16)
        ycin_ref[...] = ycin
        yc = _dot(ycin, woc_ref[...])
        ya = _dot(o_ref[...], woa_ref[...])
        yc_ref[...] = yc.astype(BF16)
        ya_ref[...] = ya.astype(BF16)
        gc = _sigmoid(gc_ref[...].astype(F32) + bg_ref[:, 0:D])
        ga = _sigmoid(ga_ref[...].astype(F32) + bg_ref[:, D:2 * D])
        m = (gc * yc + ga * ya).astype(BF16)
        m_ref[...] = m
        x2_ref[...] = x_ref[...] + _dot(m, wo_ref[...])

    g_off = (3 * CW + 3 * AW) // D
    tok = lambda w, j=0: pl.BlockSpec((tm, w), lambda i: (i, j))
    return pl.pallas_call(
        body, name="merge_fwd", grid=(T // tm,),
        in_specs=[tok(CW, 0), tok(CW, 1), tok(CW, 2), tok(D, g_off), tok(D, g_off + 1), tok(AW), tok(D),
                  _resident((8, CW)), _resident((1, 2 * D)), _resident((CW, D)), _resident((AW, D)), _resident((D, D))],
        out_specs=[tok(D), tok(CW), tok(D), tok(CW), tok(D), tok(D)],
        out_shape=[SDS((T, D), F32), SDS((T, CW), BF16), SDS((T, D), BF16), SDS((T, CW), BF16),
                   SDS((T, D), BF16), SDS((T, D), BF16)],
        scratch_shapes=[pltpu.VMEM((8, CW), F32)],
        compiler_params=_cparams(1),
    )(pm, pm, pm, pm, pm, o, x, wc, bg, woc, woa, wo)


def _ffn_fwd(x2, g2, wup, wf, wdn, gf, tgt, S, tm):
    T, D = x2.shape
    FH = wdn.shape[0]
    ch = _tile(FH, 1408)
    tps = S // tm

    def body(x2_ref, g2_ref, wup_ref, wf_ref, wdn_ref, gf_ref, tgt_ref,
             upp_ref, h2_ref, dx3_ref, loss_ref, dgf_ref, tail_ref):
        i = pl.program_id(0)

        @pl.when(i % tps == 0)
        def _():
            tail_ref[...] = jnp.zeros_like(tail_ref)

        @pl.when(i == 0)
        def _():
            loss_ref[...] = jnp.zeros_like(loss_ref)
            dgf_ref[...] = jnp.zeros_like(dgf_ref)

        x2v = x2_ref[...]
        h2 = (x2v * _rms(x2v) * g2_ref[...]).astype(BF16)
        h2_ref[...] = h2
        x3 = x2v
        for c in range(0, FH, ch):
            gated = []
            for cols in (slice(c, c + ch), slice(FH + c, FH + c + ch)):
                upp = _dot(h2, wup_ref[:, cols])
                upp_ref[:, cols] = upp.astype(BF16)
                p1, p2 = _taps_back(upp, tail_ref[6:7, cols], tail_ref[7:8, cols])
                tail_ref[:, cols] = upp[tm - 8:tm, :]
                gated.append((upp * wf_ref[2:3, cols] + p2 * wf_ref[0:1, cols]) + p1 * wf_ref[1:2, cols])
            a, b = gated
            act = (a * _sigmoid(a) * b).astype(BF16)
            x3 = x3 + _dot(act, wdn_ref[c:c + ch, :])
        r3 = _rms(x3)
        xn3 = x3 * r3
        e = xn3 * gf_ref[...] - tgt_ref[...]
        loss_ref[...] += 0.5 * jnp.sum(jnp.mean(e * e, axis=-1, keepdims=True), axis=0, keepdims=True)
        dy = e / D
        dgf_ref[0:1, :] += jnp.sum(dy * xn3, axis=0, keepdims=True)
        dxn = dy * gf_ref[...]
        dx3_ref[...] = r3 * (dxn - xn3 * jnp.mean(dxn * xn3, axis=-1, keepdims=True))

    tok = lambda w: pl.BlockSpec((tm, w), lambda i: (i, 0))
    return pl.pallas_call(
        body, name="ffn_fwd", grid=(T // tm,),
        in_specs=[tok(D), _resident((1, D)), _resident((D, 2 * FH)), _resident((8, 2 * FH)), _resident((FH, D)),
                  _resident((1, D)), tok(D)],
        out_specs=[tok(2 * FH), tok(D), tok(D), pl.BlockSpec((8, LANES), lambda i: (0, 0)),
                   pl.BlockSpec((8, D), lambda i: (0, 0))],
        out_shape=[SDS((T, 2 * FH), BF16), SDS((T, D), BF16), SDS((T, D), F32), SDS((8, LANES), F32), SDS((8, D), F32)],
        scratch_shapes=[pltpu.VMEM((8, 2 * FH), F32)],
        compiler_params=_cparams(1),
    )(x2, g2, wup, wf, wdn, gf, tgt)


def _ffn_bwd(dx3, x2, g2, upp, wf, wdn_t, wup_t, S, tm):
    T, D = x2.shape
    FH = wdn_t.shape[1]
    ch = _tile(FH, 1408)
    n, tps = T // tm, S // tm
    hb = tm // 16

    def body(dx3_ref, x2_ref, g2_ref, upp_ref, halo_ref, wf_ref, wdnt_ref, wupt_ref,
             dx2_ref, dupp_ref, act_ref, dwf_ref, dg2_ref, head_ref, prev_ref):
        i = pl.program_id(0)
        t = n - 1 - i

        @pl.when(i == 0)
        def _():
            dwf_ref[...] = jnp.zeros_like(dwf_ref)
            dg2_ref[...] = jnp.zeros_like(dg2_ref)

        @pl.when(t % tps == tps - 1)
        def _():
            head_ref[...] = jnp.zeros_like(head_ref)

        keep = jnp.where(t % tps == 0, 0.0, 1.0)
        prev_ref[...] = halo_ref[...].astype(F32) * keep
        dx3v = dx3_ref[...]
        dx3b = dx3v.astype(BF16)
        x2v = x2_ref[...]
        r2 = _rms(x2v)
        xn2 = x2v * r2
        dh2 = jnp.zeros((tm, D), F32)
        for c in range(0, FH, ch):
            ca, cb = slice(c, c + ch), slice(FH + c, FH + c + ch)
            pre, post = [], []
            for cols in (ca, cb):
                u0 = upp_ref[:, cols].astype(F32)
                u1, u2 = _taps_back(u0, prev_ref[14:15, cols], prev_ref[15:16, cols])
                pre.append(u0)
                post.append((u0 * wf_ref[2:3, cols] + u2 * wf_ref[0:1, cols]) + u1 * wf_ref[1:2, cols])
            a, b = post
            sig = _sigmoid(a)
            sl = a * sig
            act_ref[:, ca] = (sl * b).astype(BF16)
            dact = _dot(dx3b, wdnt_ref[:, ca])
            grads = (dact * b * (sig * (1.0 + a * (1.0 - sig))), dact * sl)
            for cols, u0, d in zip((ca, cb), pre, grads):
                d1, d2 = _taps_ahead(d, head_ref[0:1, cols], head_ref[1:2, cols])
                head_ref[:, cols] = d[0:8, :]
                dwf_ref[2:3, cols] += jnp.sum(u0 * d, axis=0, keepdims=True)
                dwf_ref[1:2, cols] += jnp.sum(u0 * d1, axis=0, keepdims=True)
                dwf_ref[0:1, cols] += jnp.sum(u0 * d2, axis=0, keepdims=True)
                dpre = ((d * wf_ref[2:3, cols] + d1 * wf_ref[1:2, cols]) + d2 * wf_ref[0:1, cols]).astype(BF16)
                dupp_ref[:, cols] = dpre
                dh2 = dh2 + _dot(dpre, wupt_ref[cols, :])
        dg2_ref[0:1, :] += jnp.sum(dh2 * xn2, axis=0, keepdims=True)
        dxn = dh2 * g2_ref[...]
        dx2_ref[...] = dx3v + r2 * (dxn - xn2 * jnp.mean(dxn * xn2, axis=-1, keepdims=True))

    tok = lambda w: pl.BlockSpec((tm, w), lambda i: (n - 1 - i, 0))
    halo = pl.BlockSpec((16, 2 * FH), lambda i: (jnp.maximum((n - 1 - i) * hb - 1, 0), 0))
    acc = lambda w: pl.BlockSpec((8, w), lambda i: (0, 0))
    return pl.pallas_call(
        body, name="ffn_bwd", grid=(n,),
        in_specs=[tok(D), tok(D), _resident((1, D)), tok(2 * FH), halo, _resident((8, 2 * FH)),
                  _resident((D, FH)), _resident((2 * FH, D))],
        out_specs=[tok(D), tok(2 * FH), tok(FH), acc(2 * FH), acc(D)],
        out_shape=[SDS((T, D), F32), SDS((T, 2 * FH), BF16), SDS((T, FH), BF16), SDS((8, 2 * FH), F32), SDS((8, D), F32)],
        scratch_shapes=[pltpu.VMEM((8, 2 * FH), F32), pltpu.VMEM((16, 2 * FH), F32)],
        compiler_params=_cparams(1),
    )(dx3, x2, g2, upp, upp, wf, wdn_t, wup_t)


def _merge_bwd(dx2, pm, u, yc, ya, wc, bg, wo_t, woc_t, woa_t, S, tm):
    T, D = dx2.shape
    CW, AW = woc_t.shape[1], woa_t.shape[1]
    n, tps = T // tm, S // tm

    def body(dx2_ref, cb_ref, cc_ref, cin_ref, gc_ref, ga_ref, u_ref, yc_ref, ya_ref, wc_ref, bg_ref,
             wot_ref, woct_ref, woat_ref,
             dconv_ref, dgl_ref, do_ref, dyc_ref, dya_ref, dwc_ref, dbg_ref, head_ref):
        i = pl.program_id(0)
        t = n - 1 - i

        @pl.when(i == 0)
        def _():
            dwc_ref[...] = jnp.zeros_like(dwc_ref)
            dbg_ref[...] = jnp.zeros_like(dbg_ref)

        @pl.when(t % tps == tps - 1)
        def _():
            head_ref[...] = jnp.zeros_like(head_ref)

        dm = _dot(dx2_ref[...].astype(BF16), wot_ref[...])
        outs = []
        for g_ref, y_ref, cols in ((gc_ref, yc_ref, slice(0, D)), (ga_ref, ya_ref, slice(D, 2 * D))):
            g = _sigmoid(g_ref[...].astype(F32) + bg_ref[:, cols])
            dgl = dm * y_ref[...].astype(F32) * g * (1.0 - g)
            dgl_ref[:, cols] = dgl.astype(BF16)
            dbg_ref[0:1, cols] += jnp.sum(dgl, axis=0, keepdims=True)
            outs.append((dm * g).astype(BF16))
        dyc, dya = outs
        dyc_ref[...] = dyc
        dya_ref[...] = dya
        do_ref[...] = _dot(dya, woat_ref[...]).astype(BF16)
        dycin = _dot(dyc, woct_ref[...])
        cc, cin = cc_ref[...].astype(F32), cin_ref[...].astype(F32)
        z = cc * cin
        du = dycin * cb_ref[...].astype(F32)
        du1, du2 = _taps_ahead(du, head_ref[0:1, :], head_ref[1:2, :])
        head_ref[...] = du[0:8, :]
        dwc_ref[2:3, :] += jnp.sum(z * du, axis=0, keepdims=True)
        dwc_ref[1:2, :] += jnp.sum(z * du1, axis=0, keepdims=True)
        dwc_ref[0:1, :] += jnp.sum(z * du2, axis=0, keepdims=True)
        dz = (du * wc_ref[2:3, :] + du1 * wc_ref[1:2, :]) + du2 * wc_ref[0:1, :]
        dconv_ref[:, 0:CW] = (dycin * u_ref[...].astype(F32)).astype(BF16)
        dconv_ref[:, CW:2 * CW] = (dz * cin).astype(BF16)
        dconv_ref[:, 2 * CW:3 * CW] = (dz * cc).astype(BF16)

    g_off = (3 * CW + 3 * AW) // D
    tok = lambda w, j=0: pl.BlockSpec((tm, w), lambda i: (n - 1 - i, j))
    acc = lambda w: pl.BlockSpec((8, w), lambda i: (0, 0))
    return pl.pallas_call(
        body, name="merge_bwd", grid=(n,),
        in_specs=[tok(D), tok(CW, 0), tok(CW, 1), tok(CW, 2), tok(D, g_off), tok(D, g_off + 1), tok(CW), tok(D), tok(D),
                  _resident((8, CW)), _resident((1, 2 * D)), _resident((D, D)), _resident((D, CW)), _resident((D, AW))],
        out_specs=[tok(3 * CW), tok(2 * D), tok(AW), tok(D), tok(D), acc(CW), acc(2 * D)],
        out_shape=[SDS((T, 3 * CW), BF16), SDS((T, 2 * D), BF16), SDS((T, AW), BF16), SDS((T, D), BF16), SDS((T, D), BF16),
                   SDS((8, CW), F32), SDS((8, 2 * D), F32)],
        scratch_shapes=[pltpu.VMEM((8, CW), F32)],
        compiler_params=_cparams(1),
    )(dx2, pm, pm, pm, pm, pm, u, yc, ya, wc, bg, wo_t, woc_t, woa_t)


def _attn_bwd(q, k, v, do, fcol, frow, lse, tq):
    BH, S, hd = q.shape
    scale = float(hd) ** -0.5
    nq = S // tq

    def body(q_ref, k_ref, v_ref, do_ref, fc_ref, fr_ref, lse_ref, dq_ref, dk_ref, dv_ref, dfr_ref, dk_acc, dv_acc):
        dk_acc[...] = jnp.zeros_like(dk_acc)
        dv_acc[...] = jnp.zeros_like(dv_acc)
        dfr_ref[...] = jnp.zeros_like(dfr_ref)
        for i in range(nq):
            rs, kend = slice(i * tq, (i + 1) * tq), (i + 1) * tq
            qs = q_ref[rs, :] * scale
            doi = do_ref[rs, :]
            kk, vv = k_ref[0:kend, :], v_ref[0:kend, :]
            s = _dot_nt(qs, kk) + fc_ref[rs, :] - fr_ref[:, 0:kend]
            row = i * tq + lax.broadcasted_iota(jnp.int32, (tq, kend), 0)
            col = lax.broadcasted_iota(jnp.int32, (tq, kend), 1)
            p = jnp.where(col <= row, jnp.exp(s - lse_ref[rs, :]), 0.0)
            dp = _dot_nt(doi, vv)
            ds = p * (dp - jnp.sum(p * dp, axis=1, keepdims=True))
            pb, dsb = p.astype(BF16), ds.astype(BF16)
            dq_ref[rs, :] = (_dot(dsb, kk) * scale).astype(BF16)
            dv_acc[0:kend, :] += _dot_tn(pb, doi)
            dk_acc[0:kend, :] += _dot_tn(dsb, qs)
            dfr_ref[:, 0:kend] -= jnp.sum(ds, axis=0, keepdims=True)
        dk_ref[...] = dk_acc[...].astype(BF16)
        dv_ref[...] = dv_acc[...].astype(BF16)

    seq = lambda w: pl.BlockSpec((None, S, w), lambda b: (b, 0, 0))
    rowspec = pl.BlockSpec((None, 1, S), lambda b: (b, 0, 0))
    return pl.pallas_call(
        body, name="attn_bwd", grid=(BH,),
        in_specs=[seq(hd), seq(hd), seq(hd), seq(hd), seq(1), rowspec, seq(1)],
        out_specs=[seq(hd), seq(hd), seq(hd), rowspec],
        out_shape=[SDS((BH, S, hd), BF16), SDS((BH, S, hd), BF16), SDS((BH, S, hd), BF16), SDS((BH, 1, S), F32)],
        scratch_shapes=[pltpu.VMEM((S, hd), F32), pltpu.VMEM((S, hd), F32)],
        compiler_params=_cparams(1),
    )(q, k, v, do, fcol, frow, lse)


def _forget_bwd(dfc, fl, bf, S):
    T, NF = fl.shape
    ch = _tile(S, 256, 8)

    def body(df_ref, fl_ref, bf_ref, dfl_ref, dbf_ref):
        @pl.when(pl.program_id(0) == 0)
        def _():
            dbf_ref[...] = jnp.zeros_like(dbf_ref)

        row = lax.broadcasted_iota(jnp.int32, (ch, ch), 0)
        col = lax.broadcasted_iota(jnp.int32, (ch, ch), 1)
        tri = (col >= row).astype(BF16)
        carry = jnp.zeros((1, NF), F32)
        for c in range(S - ch, -1, -ch):
            d = df_ref[c:c + ch, :]
            dlf = _tri_dot(tri, d) + carry
            carry = carry + jnp.sum(d, axis=0, keepdims=True)
            dfl = dlf * _sigmoid(-(fl_ref[c:c + ch, :] + bf_ref[...]))
            dfl_ref[c:c + ch, :] = dfl.astype(BF16)
            dbf_ref[0:1, :] += jnp.sum(dfl, axis=0, keepdims=True)

    return pl.pallas_call(
        body, name="forget_bwd", grid=(T // S,),
        in_specs=[pl.BlockSpec((S, NF), lambda b: (b, 0)), pl.BlockSpec((S, NF), lambda b: (b, 0)), _resident((1, NF))],
        out_specs=[pl.BlockSpec((S, NF), lambda b: (b, 0)), pl.BlockSpec((8, NF), lambda b: (0, 0))],
        out_shape=[SDS((T, NF), BF16), SDS((8, NF), F32)],
        compiler_params=_cparams(1),
    )(dfc, fl, bf)


def _inproj_bwd(dparts, wparts_t, x, g1, dx2, tm):
    T, D = x.shape
    npart = len(dparts)

    def body(*refs):
        d_refs, w_refs = refs[:npart], refs[npart:2 * npart]
        x_ref, g_ref, dx2_ref, dx_ref, dg_ref = refs[2 * npart:]

        @pl.when(pl.program_id(0) == 0)
        def _():
            dg_ref[...] = jnp.zeros_like(dg_ref)

        dh = _dot(d_refs[0][...], w_refs[0][...])
        for d_ref, w_ref in zip(d_refs[1:], w_refs[1:]):
            dh = dh + _dot(d_ref[...], w_ref[...])
        xv = x_ref[...]
        r = _rms(xv)
        xn = xv * r
        dg_ref[0:1, :] += jnp.sum(dh * xn, axis=0, keepdims=True)
        dxn = dh * g_ref[...]
        dx_ref[...] = dx2_ref[...] + r * (dxn - xn * jnp.mean(dxn * xn, axis=-1, keepdims=True))

    tok = lambda w: pl.BlockSpec((tm, w), lambda i: (i, 0))
    return pl.pallas_call(
        body, name="inproj_bwd", grid=(T // tm,),
        in_specs=[tok(d.shape[1]) for d in dparts] + [_resident(w.shape) for w in wparts_t]
                 + [tok(D), _resident((1, D)), tok(D)],
        out_specs=[tok(D), pl.BlockSpec((8, D), lambda i: (0, 0))],
        out_shape=[SDS((T, D), F32), SDS((8, D), F32)],
        compiler_params=_cparams(1),
    )(*dparts, *wparts_t, x, g1, dx2)


def _wgrad(a, b, name):
    T, M = a.shape
    N = b.shape[1]
    tn = _tile(N, 1536 if M <= 1024 else 512)
    tk = _tile(T, 1024, 16)

    def body(a_ref, b_ref, o_ref):
        @pl.when(pl.program_id(1) == 0)
        def _():
            o_ref[...] = jnp.zeros_like(o_ref)

        o_ref[...] += _dot_tn(a_ref[...], b_ref[...].astype(BF16))

    return pl.pallas_call(
        body, name=name, grid=(N // tn, T // tk),
        in_specs=[pl.BlockSpec((tk, M), lambda j, k: (k, 0)), pl.BlockSpec((tk, tn), lambda j, k: (k, j))],
        out_specs=pl.BlockSpec((M, tn), lambda j, k: (0, j)),
        out_shape=SDS((M, N), F32),
        compiler_params=_cparams(2),
    )(a, b)


def _adamw(w, g, m, v, name):
    shape = w.shape
    C = shape[-1]
    w2, g2, m2, v2 = (a.reshape(-1, C) for a in (w, g, m, v))
    R = w2.shape[0]
    tr = R if R <= 512 else _tile(R, 256, 8)

    def body(w_ref, g_ref, m_ref, v_ref, d_ref, nm_ref, nv_ref):
        gv = g_ref[...]
        mv = ADAM_B1 * m_ref[...] + (1.0 - ADAM_B1) * gv
        vv = ADAM_B2 * v_ref[...] + (1.0 - ADAM_B2) * (gv * gv)
        m_hat = mv / (1.0 - ADAM_B1 ** ADAM_STEP)
        v_hat = vv / (1.0 - ADAM_B2 ** ADAM_STEP)
        d_ref[...] = -ADAM_LR * (m_hat / (jnp.sqrt(v_hat) + ADAM_EPS) + ADAM_WD * w_ref[...])
        nm_ref[...] = mv
        nv_ref[...] = vv

    spec = pl.BlockSpec((tr, C), lambda r: (r, 0))
    outs = pl.pallas_call(
        body, name=name, grid=(R // tr,),
        in_specs=[spec] * 4, out_specs=[spec] * 3, out_shape=[SDS((R, C), F32)] * 3,
        compiler_params=_cparams(1),
    )(w2, g2, m2, v2)
    return tuple(o.reshape(shape) for o in outs)


def _rows(a, L):
    lead = a.shape[0]
    flat = a.reshape(lead, -1)
    n = flat.shape[1]
    r = -(-n // L)
    return jnp.pad(flat, ((0, 0), (0, r * L - n))).reshape(lead, r, L)


def _unrows(p, shape):
    n = 1
    for s in shape:
        n *= s
    return p.reshape(-1)[:n].reshape(shape)


def _col_blocks(a):
    R, C = a.shape
    return a.reshape(R, N_DEV, C // N_DEV).transpose(1, 0, 2)


def _from_col_blocks(a):
    n, R, c = a.shape
    return a.transpose(1, 0, 2).reshape(R, n * c)


def kernel(x, norm_mix_g, w_in, b_f, b_gate, conv_mix_w, w_out_conv, w_out_attn, w_o, norm_ffn_g, w_up, conv_ffn_w, w_down, norm_f_g, loss_target, m_norm_mix_g, m_w_in, m_b_f, m_b_gate, m_conv_mix_w, m_w_out_conv, m_w_out_attn, m_w_o, m_norm_ffn_g, m_w_up, m_conv_ffn_w, m_w_down, m_norm_f_g, v_norm_mix_g, v_w_in, v_b_f, v_b_gate, v_conv_mix_w, v_w_out_conv, v_w_out_attn, v_w_o, v_norm_ffn_g, v_w_up, v_conv_ffn_w, v_w_down, v_norm_f_g):
    Bl, S, D = x.shape
    T = Bl * S
    H = b_f.shape[-1]
    CW = N_DEV * conv_mix_w.shape[-1]
    AW = w_out_attn.shape[1]
    hd = AW // H
    FH = N_DEV * w_down.shape[1]
    NIN = N_DEV * w_in.shape[-1]
    NM = 3 * CW + 3 * AW + 2 * D
    assert NIN == NM + H and w_out_conv.shape[1] == CW and (3 * CW + 3 * AW) % D == 0
    L = D
    tm_big = min(512, S // 2)
    tm_ffn = min(256, S // 2)
    tq = min(256, S // 2)

    bits = lambda a: lax.bitcast_convert_type(a, BF16)
    shards = [w_in[0].astype(BF16), w_up[0].astype(BF16), w_down[0].astype(BF16), w_o[0].astype(BF16),
              w_out_conv[0].astype(BF16), w_out_attn[0].astype(BF16), bits(conv_ffn_w[0]), bits(conv_mix_w[0])]
    packed = [_rows(s[None], L)[0] for s in shards]
    counts = [p.shape[0] for p in packed]
    wrows = sum(counts)
    wpack = jnp.concatenate(packed + [jnp.zeros((-wrows % 16, L), BF16)], axis=0)
    gathered = _all_gather(wpack, "weights_all_gather")
    offs = [sum(counts[:i]) for i in range(len(counts))]

    def part(i, shape):
        blk = gathered[:, offs[i]:offs[i] + counts[i], :]
        return jax.vmap(lambda p: _unrows(p, shape))(blk)

    W_in = _from_col_blocks(part(0, w_in.shape[1:]))
    W_up = _from_col_blocks(part(1, w_up.shape[1:]))
    W_dn = part(2, w_down.shape[1:]).reshape(FH, D)
    W_o = part(3, w_o.shape[1:]).reshape(D, D)
    W_oc = _from_col_blocks(part(4, w_out_conv.shape[1:]))
    W_oa = _from_col_blocks(part(5, w_out_attn.shape[1:]))
    wf_full = _from_col_blocks(lax.bitcast_convert_type(part(6, conv_ffn_w.shape[1:] + (2,)), F32))
    wc_full = _from_col_blocks(lax.bitcast_convert_type(part(7, conv_mix_w.shape[1:] + (2,)), F32))

    nqkv = 3 * CW + 3 * AW
    W_main = jnp.concatenate([W_in[:, :nqkv], W_in[:, nqkv + H:]], axis=1)
    W_f = jnp.pad(W_in[:, nqkv:nqkv + H], ((0, 0), (0, LANES - H)))
    wf8 = jnp.pad(wf_full, ((0, 5), (0, 0)))
    wc8 = jnp.pad(wc_full, ((0, 5), (0, 0)))
    bf128 = jnp.pad(b_f, ((0, 0), (0, LANES - H)))

    x2d = x.reshape(T, D)
    tgt = loss_target.reshape(T, D)
    pm, fl, h1 = _inproj_fwd(x2d, norm_mix_g, W_main, W_f, tm_big)
    fcum = _forget_cumsum(fl, bf128, S)
    f_bhs = fcum[:, :H].reshape(Bl, S, H).transpose(0, 2, 1).reshape(Bl * H, S)
    fcol, frow = f_bhs[:, :, None], f_bhs[:, None, :]
    qkv = pm[:, 3 * CW:nqkv].reshape(Bl, S, 3, H, hd).transpose(2, 0, 3, 1, 4).reshape(3, Bl * H, S, hd)
    o_h, lse = _attn_fwd(qkv[0], qkv[1], qkv[2], fcol, frow, tq)
    o = o_h.reshape(Bl, H, S, hd).transpose(0, 2, 1, 3).reshape(T, AW)
    x2, u, m, ycin, yc, ya = _merge_fwd(pm, o, x2d, wc8, b_gate, W_oc, W_oa, W_o, S, tm_big)
    upp, h2, dx3, loss8, dgf8 = _ffn_fwd(x2, norm_ffn_g, W_up, wf8, W_dn, norm_f_g.reshape(1, D), tgt, S, tm_ffn)

    dx2, dupp, act, dwf8, dg2_8 = _ffn_bwd(dx3, x2, norm_ffn_g, upp, wf8, W_dn.T, W_up.T, S, tm_ffn)
    dW_dn = _wgrad(act, dx3, "wgrad_down")
    dW_up = _wgrad(h2, dupp, "wgrad_up")
    dconv, dgl, do, dyc, dya, dwc8, dbg8 = _merge_bwd(dx2, pm, u, yc, ya, wc8, b_gate, W_o.T, W_oc.T, W_oa.T, S, tm_big)
    dW_o = _wgrad(m, dx2, "wgrad_o")
    dW_oc = _wgrad(ycin, dyc, "wgrad_out_conv")
    dW_oa = _wgrad(o, dya, "wgrad_out_attn")
    do_h = do.reshape(Bl, S, H, hd).transpose(0, 2, 1, 3).reshape(Bl * H, S, hd)
    dq, dk, dv, dfr = _attn_bwd(qkv[0], qkv[1], qkv[2], do_h, fcol, frow, lse, tq)
    dqkv = jnp.stack([dq, dk, dv]).reshape(3, Bl, H, S, hd).transpose(1, 3, 0, 2, 4).reshape(T, 3 * AW)
    dfc = jnp.pad(dfr.reshape(Bl, H, S).transpose(0, 2, 1).reshape(T, H), ((0, 0), (0, LANES - H)))
    dfl, dbf8 = _forget_bwd(dfc, fl, bf128, S)
    W_in_t = W_in.T
    wparts_t = [W_in_t[:3 * CW], W_in_t[3 * CW:nqkv], W_in_t[nqkv + H:], jnp.pad(W_in_t[nqkv:nqkv + H], ((0, LANES - H), (0, 0)))]
    dparts = [dconv, dqkv, dgl, dfl]
    grad_x, dg1_8 = _inproj_bwd(dparts, wparts_t, x2d, norm_mix_g, dx2, tm_big)
    dW_in_parts = [_wgrad(h1, d, "wgrad_in_%d" % i) for i, d in enumerate(dparts)]
    dW_in = jnp.concatenate([dW_in_parts[0], dW_in_parts[1], dW_in_parts[3][:, :H], dW_in_parts[2]], axis=1)

    gblocks = [_rows(_col_blocks(dW_in), L), _rows(_col_blocks(dW_up), L), dW_dn.reshape(N_DEV, -1, L),
               dW_o.reshape(N_DEV, -1, L), _rows(_col_blocks(dW_oc), L), _rows(_col_blocks(dW_oa), L),
               _rows(_col_blocks(dwf8[:3]), L), _rows(_col_blocks(dwc8[:3]), L)]
    gcounts = [g.shape[1] for g in gblocks]
    grows = sum(gcounts)
    gpack = jnp.concatenate(gblocks + [jnp.zeros((N_DEV, -grows % 8, L), F32)], axis=1)
    R = gpack.shape[1]
    gpack = gpack.reshape(4, 2, R, L).transpose(1, 0, 2, 3)
    px, py, pc = _place()
    ids = jnp.stack([pc, 2 * px + py]).astype(jnp.int32)
    from_sibling = _sibling_exchange(gpack, "grads_sibling_exchange")
    own, chip_sums = _pair_sum(gpack, from_sibling, ids, "grads_pair_sum")
    from_chips = _chip_exchange(chip_sums, "grads_chip_exchange")
    reduced = _final_sum(own, from_chips, "grads_final_sum")
    goffs = [sum(gcounts[:i]) for i in range(len(gcounts))]
    gshapes = [w_in.shape[1:], w_up.shape[1:], w_down.shape[1:], w_o.shape[1:], w_out_conv.shape[1:], w_out_attn.shape[1:],
               conv_ffn_w.shape[1:], conv_mix_w.shape[1:]]
    g_in, g_up, g_dn, g_o, g_oc, g_oa, g_wf, g_wc = [
        _unrows(reduced[goffs[i]:goffs[i] + gcounts[i]], gshapes[i])[None] for i in range(len(gcounts))]

    small = [dg1_8[0:1], dg2_8[0:1], dgf8[0:1], dbg8[0:1, :D], dbg8[0:1, D:], jnp.pad(dbf8[0:1], ((0, 0), (0, L - LANES)))]
    spack = jnp.concatenate(small + [jnp.zeros((2, L), F32)], axis=0)
    ssum = _sum8(_all_gather(spack, "small_all_gather"), "small_sum")
    g_g1, g_g2, g_gf = ssum[0:1], ssum[1:2], ssum[2]
    g_bg = jnp.concatenate([ssum[3:4], ssum[4:5]], axis=1)
    g_bf = ssum[5:6, :H]

    loss = lax.psum(loss8[0, 0], ("x", "y", "c"))

    names = ["norm_mix_g", "w_in", "b_f", "b_gate", "conv_mix_w", "w_out_conv", "w_out_attn", "w_o", "norm_ffn_g", "w_up",
             "conv_ffn_w", "w_down", "norm_f_g"]
    weights = [norm_mix_g, w_in, b_f, b_gate, conv_mix_w, w_out_conv, w_out_attn, w_o, norm_ffn_g, w_up, conv_ffn_w, w_down, norm_f_g]
    grads = [g_g1, g_in, g_bf, g_bg, g_wc, g_oc, g_oa, g_o, g_g2, g_up, g_wf, g_dn, g_gf]
    ms = [m_norm_mix_g, m_w_in, m_b_f, m_b_gate, m_conv_mix_w, m_w_out_conv, m_w_out_attn, m_w_o, m_norm_ffn_g, m_w_up,
          m_conv_ffn_w, m_w_down, m_norm_f_g]
    vs = [v_norm_mix_g, v_w_in, v_b_f, v_b_gate, v_conv_mix_w, v_w_out_conv, v_w_out_attn, v_w_o, v_norm_ffn_g, v_w_up,
          v_conv_ffn_w, v_w_down, v_norm_f_g]
    grads = [g.reshape(w.shape) for g, w in zip(grads, weights)]
    steps = [_adamw(w, g, mm, vv, "adamw_" + nm) for nm, w, g, mm, vv in zip(names, weights, grads, ms, vs)]
    deltas, new_ms, new_vs = zip(*steps)
    return (loss, grad_x.reshape(Bl, S, D), *grads, *deltas, *new_ms, *new_vs)
```

```python
import numpy as np

import jax
import jax.numpy as jnp
from jax import lax
from jax.experimental import pallas as pl
from jax.experimental.pallas import tpu as pltpu

F32, BF16 = jnp.float32, jnp.bfloat16
EPS = 1e-6
ADAM_LR, ADAM_B1, ADAM_B2, ADAM_EPS, ADAM_WD, ADAM_STEP = 0.001, 0.9, 0.999, 1e-08, 0.01, 10
N_DEV = 8
LANES = 128
V7X_VMEM_LIMIT = 56 * 1024 * 1024
MESH = pl.DeviceIdType.MESH
SDS = jax.ShapeDtypeStruct
ANY = pl.BlockSpec(memory_space=pl.ANY)


def _tile(n, target, mult=LANES):
    best = None
    for t in range(mult, min(n, target) + 1, mult):
        if n % t == 0:
            best = t
    return best if best is not None else n


def _resident(shape):
    return pl.BlockSpec(shape, lambda *_: (0,) * len(shape), pipeline_mode=pl.Buffered(1))


def _cparams(n_axes=1):
    return pltpu.CompilerParams(dimension_semantics=("arbitrary",) * n_axes, vmem_limit_bytes=V7X_VMEM_LIMIT)


def _dot(a, b):
    return jnp.dot(a, b, preferred_element_type=F32)


def _dot_tn(a, b):
    return lax.dot_general(a, b, (((0,), (0,)), ((), ())), preferred_element_type=F32)


def _dot_nt(a, b):
    return lax.dot_general(a, b, (((1,), (1,)), ((), ())), preferred_element_type=F32)


def _sigmoid(x):
    return 1.0 / (1.0 + jnp.exp(-x))


def _rms(x):
    return lax.rsqrt(jnp.mean(x * x, axis=-1, keepdims=True) + EPS)


def _taps_back(z, r6, r7):
    tm = z.shape[0]
    row = lax.broadcasted_iota(jnp.int32, (tm, 1), 0)
    z1 = jnp.where(row == 0, r7, pltpu.roll(z, 1, 0))
    z2 = jnp.where(row == 0, r6, jnp.where(row == 1, r7, pltpu.roll(z, 2, 0)))
    return z1, z2


def _taps_ahead(d, h0, h1):
    tm = d.shape[0]
    row = lax.broadcasted_iota(jnp.int32, (tm, 1), 0)
    d1 = jnp.where(row == tm - 1, h0, pltpu.roll(d, tm - 1, 0))
    d2 = jnp.where(row == tm - 2, h0, jnp.where(row == tm - 1, h1, pltpu.roll(d, tm - 2, 0)))
    return d1, d2


def _tri_dot(tri, x):
    hi = x.astype(BF16)
    r = x - hi.astype(F32)
    mid = r.astype(BF16)
    lo = (r - mid.astype(F32)).astype(BF16)
    return (_dot(tri, lo) + _dot(tri, mid)) + _dot(tri, hi)


def _lane_pick(block, lane):
    lanes = lax.broadcasted_iota(jnp.int32, (1, block.shape[1]), 1)
    return jnp.sum(jnp.where(lanes == lane, block, 0.0), axis=1, keepdims=True)


def _place():
    return lax.axis_index("x"), lax.axis_index("y"), lax.axis_index("c")


def _all_gather(xs, name):
    n = len(xs)

    def body(*refs):
        x_refs, out_refs = refs[:n], refs[n:2 * n]
        send_sems, recv_sems, local_sems = refs[2 * n:]
        px, py, pc = _place()
        me, sibling = (px, py, pc), (px, py, 1 - pc)
        chips = [(1 - px, py), (px, 1 - py), (1 - px, 1 - py)]

        def slot(a, qx, qy, qc):
            return out_refs[a].at[4 * qx + 2 * qy + qc]

        def copy(a, k, block, to, src=None):
            return pltpu.make_async_remote_copy(
                src_ref=slot(a, *block) if src is None else src, dst_ref=slot(a, *block),
                send_sem=send_sems.at[7 * a + k], recv_sem=recv_sems.at[7 * a + k], device_id=to, device_id_type=MESH)

        mine = [pltpu.make_async_copy(x_refs[a], slot(a, *me), local_sems.at[a]) for a in range(n)]
        for cp in mine:
            cp.start()
        first = []
        for a in range(n):
            first.append(copy(a, 0, me, sibling, src=x_refs[a]))
            first += [copy(a, 1 + j, me, (*chip, pc), src=x_refs[a]) for j, chip in enumerate(chips)]
        for cp in first:
            cp.start()
        passed = []
        for j, chip in enumerate(chips):
            for a in range(n):
                copy(a, 1 + j, (*chip, pc), me).wait_recv()
                passed.append(copy(a, 4 + j, (*chip, pc), sibling))
                passed[-1].start()
        for a in range(n):
            copy(a, 0, sibling, me).wait_recv()
        for j, chip in enumerate(chips):
            for a in range(n):
                copy(a, 4 + j, (*chip, 1 - pc), me).wait_recv()
        for cp in first + passed:
            cp.wait_send()
        for cp in mine:
            cp.wait()

    return pl.pallas_call(
        body, name=name,
        out_shape=[SDS((N_DEV,) + x.shape, x.dtype) for x in xs],
        in_specs=[ANY] * n, out_specs=[ANY] * n,
        scratch_shapes=[pltpu.SemaphoreType.DMA((7 * n,)), pltpu.SemaphoreType.DMA((7 * n,)), pltpu.SemaphoreType.DMA((n,))],
    )(*xs)


def _sibling_exchange(win_buf, win_starts, win_rows, blocked, name):
    nb = len(blocked)

    def body(*refs):
        win_ref, blk_refs = refs[0], refs[1:1 + nb]
        own_ref, rwin_ref = refs[1 + nb], refs[2 + nb]
        rblk_refs = refs[3 + nb:3 + 2 * nb]
        send_sems, recv_sems, local_sems = refs[3 + 2 * nb:]
        px, py, pc = _place()
        sibling = (px, py, 1 - pc)
        copies, local = [], []
        for j in range(4):
            mine = jnp.where(pc == 0, win_starts[2 * j], win_starts[2 * j + 1])
            theirs = jnp.where(pc == 0, win_starts[2 * j + 1], win_starts[2 * j])
            local.append(pltpu.make_async_copy(
                win_ref.at[pl.ds(pl.multiple_of(mine, 8), win_rows)], own_ref.at[j], local_sems.at[j]))
            copies.append(pltpu.make_async_remote_copy(
                src_ref=win_ref.at[pl.ds(pl.multiple_of(theirs, 8), win_rows)], dst_ref=rwin_ref.at[j],
                send_sem=send_sems.at[j], recv_sem=recv_sems.at[j], device_id=sibling, device_id_type=MESH))
            for a in range(nb):
                k = 4 + 4 * a + j
                copies.append(pltpu.make_async_remote_copy(
                    src_ref=blk_refs[a].at[2 * j + 1 - pc], dst_ref=rblk_refs[a].at[j],
                    send_sem=send_sems.at[k], recv_sem=recv_sems.at[k], device_id=sibling, device_id_type=MESH))
        for cp in local + copies:
            cp.start()
        for cp in copies + local:
            cp.wait()

    C = win_buf.shape[1]
    nsem = 4 + 4 * nb
    return pl.pallas_call(
        body, name=name,
        out_shape=[SDS((4, win_rows, C), F32), SDS((4, win_rows, C), F32)] + [SDS((4,) + b.shape[1:], F32) for b in blocked],
        in_specs=[ANY] * (1 + nb), out_specs=[ANY] * (2 + nb),
        scratch_shapes=[pltpu.SemaphoreType.DMA((nsem,)), pltpu.SemaphoreType.DMA((nsem,)), pltpu.SemaphoreType.DMA((4,))],
    )(win_buf, *blocked)


def _chip_exchange(ps, name):
    n = len(ps)

    def body(*refs):
        p_refs, out_refs = refs[:n], refs[n:2 * n]
        send_sems, recv_sems = refs[2 * n:]
        px, py, pc = _place()
        chips = [(1 - px, py), (px, 1 - py), (1 - px, 1 - py)]
        copies = [
            pltpu.make_async_remote_copy(
                src_ref=p_refs[a].at[2 * qx + qy], dst_ref=out_refs[a].at[k],
                send_sem=send_sems.at[3 * a + k], recv_sem=recv_sems.at[3 * a + k],
                device_id=(qx, qy, pc), device_id_type=MESH)
            for k, (qx, qy) in enumerate(chips) for a in range(n)]
        for cp in copies:
            cp.start()
        for cp in copies:
            cp.wait()

    return pl.pallas_call(
        body, name=name,
        out_shape=[SDS((3,) + p.shape[1:], p.dtype) for p in ps],
        in_specs=[ANY] * n, out_specs=[ANY] * n,
        scratch_shapes=[pltpu.SemaphoreType.DMA((3 * n,)), pltpu.SemaphoreType.DMA((3 * n,))],
    )(*ps)


def _pair_sum(own, recv, ids, name):
    _, R, C = recv.shape
    tr = _tile(R, 512, 8)
    by_device = own.shape[0] == N_DEV

    def body(ids_ref, g_ref, r_ref, own_ref, pb_ref):
        s = g_ref[...] + r_ref[...]
        pb_ref[...] = s.astype(BF16)

        @pl.when(pl.program_id(1) == ids_ref[1])
        def _():
            own_ref[...] = s

    own_map = (lambda r, j, ids: (2 * j + ids[0], r, 0)) if by_device else (lambda r, j, ids: (j, r, 0))
    return pl.pallas_call(
        body, name=name,
        grid_spec=pltpu.PrefetchScalarGridSpec(
            num_scalar_prefetch=1, grid=(R // tr, 4),
            in_specs=[pl.BlockSpec((None, tr, C), own_map), pl.BlockSpec((None, tr, C), lambda r, j, ids: (j, r, 0))],
            out_specs=[pl.BlockSpec((tr, C), lambda r, j, ids: (r, 0)),
                       pl.BlockSpec((None, tr, C), lambda r, j, ids: (j, r, 0))]),
        out_shape=[SDS((R, C), F32), SDS((4, R, C), BF16)],
        compiler_params=_cparams(2),
    )(ids, own, recv)


def _final_sum(own, recv, name):
    R, C = own.shape
    tr = _tile(R, 512, 8)

    def body(o_ref, r_ref, out_ref):
        out_ref[...] = ((o_ref[...] + r_ref[0].astype(F32)) + r_ref[1].astype(F32)) + r_ref[2].astype(F32)

    return pl.pallas_call(
        body, name=name, grid=(R // tr,),
        in_specs=[pl.BlockSpec((tr, C), lambda r: (r, 0)), pl.BlockSpec((3, tr, C), lambda r: (0, r, 0))],
        out_specs=pl.BlockSpec((tr, C), lambda r: (r, 0)),
        out_shape=SDS((R, C), F32),
        compiler_params=_cparams(1),
    )(own, recv)


def _sum8(a, name):
    def body(a_ref, out_ref):
        s = a_ref[0]
        for d in range(1, N_DEV):
            s = s + a_ref[d]
        out_ref[...] = s

    return pl.pallas_call(body, name=name, out_shape=SDS(a.shape[1:], F32))(a)


def _inproj_fwd(x, g1, w_t, NM, tm):
    T, D = x.shape
    NF = w_t.shape[0] - NM
    ch = _tile(NM, 1024)

    def body(x_ref, g_ref, w_ref, pm_ref, fl_ref, h_ref):
        xv = x_ref[...]
        h = (xv * _rms(xv) * g_ref[...]).astype(BF16)
        h_ref[...] = h
        for c in range(0, NM, ch):
            pm_ref[:, c:c + ch] = _dot_nt(h, w_ref[c:c + ch, :]).astype(BF16)
        fl_ref[...] = _dot_nt(h, w_ref[NM:NM + NF, :])

    return pl.pallas_call(
        body, name="inproj_fwd", grid=(T // tm,),
        in_specs=[pl.BlockSpec((tm, D), lambda i: (i, 0)), _resident((1, D)), _resident(w_t.shape)],
        out_specs=[pl.BlockSpec((tm, NM), lambda i: (i, 0)), pl.BlockSpec((tm, NF), lambda i: (i, 0)),
                   pl.BlockSpec((tm, D), lambda i: (i, 0))],
        out_shape=[SDS((T, NM), BF16), SDS((T, NF), F32), SDS((T, D), BF16)],
        compiler_params=_cparams(1),
    )(x, g1, w_t)


def _log_sigmoid(x):
    return jnp.minimum(x, 0.0) - jnp.log(1.0 + jnp.exp(-jnp.abs(x)))


def _forget_cumsum(fl, bf, S):
    T, NF = fl.shape
    ch = _tile(S, 256, 8)

    def body(fl_ref, bf_ref, f_ref):
        row = lax.broadcasted_iota(jnp.int32, (ch, ch), 0)
        col = lax.broadcasted_iota(jnp.int32, (ch, ch), 1)
        tri = (col <= row).astype(BF16)
        carry = jnp.zeros((1, NF), F32)
        for c in range(0, S, ch):
            lf = _log_sigmoid(fl_ref[c:c + ch, :] + bf_ref[...])
            f_ref[c:c + ch, :] = _tri_dot(tri, lf) + carry
            carry = carry + jnp.sum(lf, axis=0, keepdims=True)

    return pl.pallas_call(
        body, name="forget_cumsum", grid=(T // S,),
        in_specs=[pl.BlockSpec((S, NF), lambda b: (b, 0)), _resident((1, NF))],
        out_specs=pl.BlockSpec((S, NF), lambda b: (b, 0)),
        out_shape=SDS((T, NF), F32),
        compiler_params=_cparams(1),
    )(fl, bf)


def _head_mask(e, hd):
    lanes = lax.broadcasted_iota(jnp.int32, (1, LANES), 1)
    return (lanes >= e * hd) & (lanes < (e + 1) * hd)


def _attn_specs(S, q_off, AW):
    ng = AW // LANES
    return [pl.BlockSpec((S, LANES), lambda b, g, o=q_off + w * ng: (b, o + g)) for w in range(3)]


def _attn_fwd(pm, fcum, frow, S, q_off, AW, hd, tq):
    T = pm.shape[0]
    scale = float(hd) ** -0.5
    nq, hpg = S // tq, LANES // hd

    def body(q_ref, k_ref, v_ref, fc_ref, fr_ref, o_ref, lse_ref):
        g = pl.program_id(1)

        @pl.when(g == 0)
        def _():
            lse_ref[...] = jnp.zeros_like(lse_ref)

        lanes = lax.broadcasted_iota(jnp.int32, (1, LANES), 1)
        for i in range(nq):
            rs, kend = slice(i * tq, (i + 1) * tq), (i + 1) * tq
            row = i * tq + lax.broadcasted_iota(jnp.int32, (tq, kend), 0)
            col = lax.broadcasted_iota(jnp.int32, (tq, kend), 1)
            o_tile = jnp.zeros((tq, LANES), F32)
            lse_tile = lse_ref[rs, :]
            for e in range(hpg):
                mask = _head_mask(e, hd)
                qs = jnp.where(mask, q_ref[rs, :], 0) * scale
                s = _dot_nt(qs, k_ref[0:kend, :]) + _lane_pick(fc_ref[rs, :], g * hpg + e) - fr_ref[e:e + 1, 0:kend]
                s = jnp.where(col <= row, s, -1e30)
                m = jnp.max(s, axis=1, keepdims=True)
                p = jnp.exp(s - m)
                l = jnp.sum(p, axis=1, keepdims=True)
                o_tile = jnp.where(mask, _dot(p.astype(BF16), v_ref[0:kend, :]) / l, o_tile)
                lse_tile = jnp.where(lanes == g * hpg + e, m + jnp.log(l), lse_tile)
            o_ref[rs, :] = o_tile.astype(BF16)
            lse_ref[rs, :] = lse_tile

    ng = AW // LANES
    full = pl.BlockSpec((S, LANES), lambda b, g: (b, 0))
    return pl.pallas_call(
        body, name="attn_fwd", grid=(T // S, ng),
        in_specs=_attn_specs(S, q_off, AW) + [full, pl.BlockSpec((None, None, 8, S), lambda b, g: (b, g, 0, 0))],
        out_specs=[pl.BlockSpec((S, LANES), lambda b, g: (b, g)), full],
        out_shape=[SDS((T, AW), BF16), SDS((T, LANES), F32)],
        compiler_params=_cparams(2),
    )(pm, pm, pm, fcum, frow)


def _merge_fwd(pm, o, x, wc, bg, woc_t, woa_t, wo, S, tm):
    T, D = x.shape
    CW, AW = woc_t.shape[1], woa_t.shape[1]
    tps = S // tm

    def body(cb_ref, cc_ref, cin_ref, gc_ref, ga_ref, o_ref, x_ref, wc_ref, bg_ref, woct_ref, woat_ref, wo_ref,
             x2_ref, u_ref, m_ref, ycin_ref, yc_ref, ya_ref, tail_ref):
        @pl.when(pl.program_id(0) % tps == 0)
        def _():
            tail_ref[...] = jnp.zeros_like(tail_ref)

        z = cc_ref[...].astype(F32) * cin_ref[...].astype(F32)
        z1, z2 = _taps_back(z, tail_ref[6:7, :], tail_ref[7:8, :])
        tail_ref[...] = z[tm - 8:tm, :]
        u = (z * wc_ref[2:3, :] + z2 * wc_ref[0:1, :]) + z1 * wc_ref[1:2, :]
        u_ref[...] = u.astype(BF16)
        ycin = (cb_ref[...].astype(F32) * u).astype(BF16)
        ycin_ref[...] = ycin
        yc = _dot_nt(ycin, woct_ref[...])
        ya = _dot_nt(o_ref[...], woat_ref[...])
        yc_ref[...] = yc.astype(BF16)
        ya_ref[...] = ya.astype(BF16)
        gc = _sigmoid(gc_ref[...].astype(F32) + bg_ref[:, 0:D])
        ga = _sigmoid(ga_ref[...].astype(F32) + bg_ref[:, D:2 * D])
        m = (gc * yc + ga * ya).astype(BF16)
        m_ref[...] = m
        x2_ref[...] = x_ref[...] + _dot(m, wo_ref[...])

    g_off = (3 * CW + 3 * AW) // D
    tok = lambda w, j=0: pl.BlockSpec((tm, w), lambda i: (i, j))
    return pl.pallas_call(
        body, name="merge_fwd", grid=(T // tm,),
        in_specs=[tok(CW, 0), tok(CW, 1), tok(CW, 2), tok(D, g_off), tok(D, g_off + 1), tok(AW), tok(D),
                  _resident((8, CW)), _resident((1, 2 * D)), _resident((D, CW)), _resident((D, AW)), _resident((D, D))],
        out_specs=[tok(D), tok(CW), tok(D), tok(CW), tok(D), tok(D)],
        out_shape=[SDS((T, D), F32), SDS((T, CW), BF16), SDS((T, D), BF16), SDS((T, CW), BF16),
                   SDS((T, D), BF16), SDS((T, D), BF16)],
        scratch_shapes=[pltpu.VMEM((8, CW), F32)],
        compiler_params=_cparams(1),
    )(pm, pm, pm, pm, pm, o, x, wc, bg, woc_t, woa_t, wo)


def _ffn_fwd(x2, g2, wup_t, wf, wdn, gf, tgt, S, tm):
    T, D = x2.shape
    FH = wdn.shape[0]
    ch = _tile(FH, 1408)
    tps = S // tm

    def body(x2_ref, g2_ref, wupt_ref, wf_ref, wdn_ref, gf_ref, tgt_ref,
             upp_ref, h2_ref, dx3_ref, loss_ref, dgf_ref, tail_ref):
        i = pl.program_id(0)

        @pl.when(i % tps == 0)
        def _():
            tail_ref[...] = jnp.zeros_like(tail_ref)

        @pl.when(i == 0)
        def _():
            loss_ref[...] = jnp.zeros_like(loss_ref)
            dgf_ref[...] = jnp.zeros_like(dgf_ref)

        x2v = x2_ref[...]
        h2 = (x2v * _rms(x2v) * g2_ref[...]).astype(BF16)
        h2_ref[...] = h2
        x3 = x2v
        for c in range(0, FH, ch):
            gated = []
            for cols in (slice(c, c + ch), slice(FH + c, FH + c + ch)):
                upp = _dot_nt(h2, wupt_ref[cols, :])
                upp_ref[:, cols] = upp.astype(BF16)
                p1, p2 = _taps_back(upp, tail_ref[6:7, cols], tail_ref[7:8, cols])
                tail_ref[:, cols] = upp[tm - 8:tm, :]
                gated.append((upp * wf_ref[2:3, cols] + p2 * wf_ref[0:1, cols]) + p1 * wf_ref[1:2, cols])
            a, b = gated
            act = (a * _sigmoid(a) * b).astype(BF16)
            x3 = x3 + _dot(act, wdn_ref[c:c + ch, :])
        r3 = _rms(x3)
        xn3 = x3 * r3
        e = xn3 * gf_ref[...] - tgt_ref[...]
        loss_ref[...] += 0.5 * jnp.sum(jnp.mean(e * e, axis=-1, keepdims=True), axis=0, keepdims=True)
        dy = e / D
        dgf_ref[0:1, :] += jnp.sum(dy * xn3, axis=0, keepdims=True)
        dxn = dy * gf_ref[...]
        dx3_ref[...] = r3 * (dxn - xn3 * jnp.mean(dxn * xn3, axis=-1, keepdims=True))

    tok = lambda w: pl.BlockSpec((tm, w), lambda i: (i, 0))
    return pl.pallas_call(
        body, name="ffn_fwd", grid=(T // tm,),
        in_specs=[tok(D), _resident((1, D)), _resident((2 * FH, D)), _resident((8, 2 * FH)), _resident((FH, D)),
                  _resident((1, D)), tok(D)],
        out_specs=[tok(2 * FH), tok(D), tok(D), pl.BlockSpec((8, LANES), lambda i: (0, 0)),
                   pl.BlockSpec((8, D), lambda i: (0, 0))],
        out_shape=[SDS((T, 2 * FH), BF16), SDS((T, D), BF16), SDS((T, D), F32), SDS((8, LANES), F32), SDS((8, D), F32)],
        scratch_shapes=[pltpu.VMEM((8, 2 * FH), F32)],
        compiler_params=_cparams(1),
    )(x2, g2, wup_t, wf, wdn, gf, tgt)


def _ffn_bwd(dx3, x2, g2, upp, wf, wdn, wup_t, S, tm):
    T, D = x2.shape
    FH = wdn.shape[0]
    ch = _tile(FH, 1408)
    n, tps = T // tm, S // tm
    hb = tm // 16

    def body(dx3_ref, x2_ref, g2_ref, upp_ref, halo_ref, wf_ref, wdn_ref, wupt_ref,
             dx2_ref, dupp_ref, act_ref, dwf_ref, dg2_ref, head_ref, prev_ref):
        i = pl.program_id(0)
        t = n - 1 - i

        @pl.when(i == 0)
        def _():
            dwf_ref[...] = jnp.zeros_like(dwf_ref)
            dg2_ref[...] = jnp.zeros_like(dg2_ref)

        @pl.when(t % tps == tps - 1)
        def _():
            head_ref[...] = jnp.zeros_like(head_ref)

        keep = jnp.where(t % tps == 0, 0.0, 1.0)
        prev_ref[...] = halo_ref[...].astype(F32) * keep
        dx3v = dx3_ref[...]
        dx3b = dx3v.astype(BF16)
        x2v = x2_ref[...]
        r2 = _rms(x2v)
        xn2 = x2v * r2
        dh2 = jnp.zeros((tm, D), F32)
        for c in range(0, FH, ch):
            ca, cb = slice(c, c + ch), slice(FH + c, FH + c + ch)
            pre, post = [], []
            for cols in (ca, cb):
                u0 = upp_ref[:, cols].astype(F32)
                u1, u2 = _taps_back(u0, prev_ref[14:15, cols], prev_ref[15:16, cols])
                pre.append(u0)
                post.append((u0 * wf_ref[2:3, cols] + u2 * wf_ref[0:1, cols]) + u1 * wf_ref[1:2, cols])
            a, b = post
            sig = _sigmoid(a)
            sl = a * sig
            act_ref[:, ca] = (sl * b).astype(BF16)
            dact = _dot_nt(dx3b, wdn_ref[ca, :])
            grads = (dact * b * (sig * (1.0 + a * (1.0 - sig))), dact * sl)
            for cols, u0, d in zip((ca, cb), pre, grads):
                d1, d2 = _taps_ahead(d, head_ref[0:1, cols], head_ref[1:2, cols])
                head_ref[:, cols] = d[0:8, :]
                dwf_ref[2:3, cols] += jnp.sum(u0 * d, axis=0, keepdims=True)
                dwf_ref[1:2, cols] += jnp.sum(u0 * d1, axis=0, keepdims=True)
                dwf_ref[0:1, cols] += jnp.sum(u0 * d2, axis=0, keepdims=True)
                dpre = ((d * wf_ref[2:3, cols] + d1 * wf_ref[1:2, cols]) + d2 * wf_ref[0:1, cols]).astype(BF16)
                dupp_ref[:, cols] = dpre
                dh2 = dh2 + _dot(dpre, wupt_ref[cols, :])
        dg2_ref[0:1, :] += jnp.sum(dh2 * xn2, axis=0, keepdims=True)
        dxn = dh2 * g2_ref[...]
        dx2_ref[...] = dx3v + r2 * (dxn - xn2 * jnp.mean(dxn * xn2, axis=-1, keepdims=True))

    tok = lambda w: pl.BlockSpec((tm, w), lambda i: (n - 1 - i, 0))
    halo = pl.BlockSpec((16, 2 * FH), lambda i: (jnp.maximum((n - 1 - i) * hb - 1, 0), 0))
    acc = lambda w: pl.BlockSpec((8, w), lambda i: (0, 0))
    return pl.pallas_call(
        body, name="ffn_bwd", grid=(n,),
        in_specs=[tok(D), tok(D), _resident((1, D)), tok(2 * FH), halo, _resident((8, 2 * FH)),
                  _resident((FH, D)), _resident((2 * FH, D))],
        out_specs=[tok(D), tok(2 * FH), tok(FH), acc(2 * FH), acc(D)],
        out_shape=[SDS((T, D), F32), SDS((T, 2 * FH), BF16), SDS((T, FH), BF16), SDS((8, 2 * FH), F32), SDS((8, D), F32)],
        scratch_shapes=[pltpu.VMEM((8, 2 * FH), F32), pltpu.VMEM((16, 2 * FH), F32)],
        compiler_params=_cparams(1),
    )(dx3, x2, g2, upp, upp, wf, wdn, wup_t)


def _merge_bwd(dx2, pm, u, yc, ya, wc, bg, wo, woc_t, woa_t, S, tm):
    T, D = dx2.shape
    CW, AW = woc_t.shape[1], woa_t.shape[1]
    n, tps = T // tm, S // tm

    def body(dx2_ref, cb_ref, cc_ref, cin_ref, gc_ref, ga_ref, u_ref, yc_ref, ya_ref, wc_ref, bg_ref,
             wo_ref, woct_ref, woat_ref,
             dconv_ref, dgl_ref, do_ref, dyc_ref, dya_ref, dwc_ref, dbg_ref, head_ref):
        i = pl.program_id(0)
        t = n - 1 - i

        @pl.when(i == 0)
        def _():
            dwc_ref[...] = jnp.zeros_like(dwc_ref)
            dbg_ref[...] = jnp.zeros_like(dbg_ref)

        @pl.when(t % tps == tps - 1)
        def _():
            head_ref[...] = jnp.zeros_like(head_ref)

        dm = _dot_nt(dx2_ref[...].astype(BF16), wo_ref[...])
        outs = []
        for g_ref, y_ref, cols in ((gc_ref, yc_ref, slice(0, D)), (ga_ref, ya_ref, slice(D, 2 * D))):
            g = _sigmoid(g_ref[...].astype(F32) + bg_ref[:, cols])
            dgl = dm * y_ref[...].astype(F32) * g * (1.0 - g)
            dgl_ref[:, cols] = dgl.astype(BF16)
            dbg_ref[0:1, cols] += jnp.sum(dgl, axis=0, keepdims=True)
            outs.append((dm * g).astype(BF16))
        dyc, dya = outs
        dyc_ref[...] = dyc
        dya_ref[...] = dya
        do_ref[...] = _dot(dya, woat_ref[...]).astype(BF16)
        dycin = _dot(dyc, woct_ref[...])
        cc, cin = cc_ref[...].astype(F32), cin_ref[...].astype(F32)
        z = cc * cin
        du = dycin * cb_ref[...].astype(F32)
        du1, du2 = _taps_ahead(du, head_ref[0:1, :], head_ref[1:2, :])
        head_ref[...] = du[0:8, :]
        dwc_ref[2:3, :] += jnp.sum(z * du, axis=0, keepdims=True)
        dwc_ref[1:2, :] += jnp.sum(z * du1, axis=0, keepdims=True)
        dwc_ref[0:1, :] += jnp.sum(z * du2, axis=0, keepdims=True)
        dz = (du * wc_ref[2:3, :] + du1 * wc_ref[1:2, :]) + du2 * wc_ref[0:1, :]
        dconv_ref[:, 0:CW] = (dycin * u_ref[...].astype(F32)).astype(BF16)
        dconv_ref[:, CW:2 * CW] = (dz * cin).astype(BF16)
        dconv_ref[:, 2 * CW:3 * CW] = (dz * cc).astype(BF16)

    g_off = (3 * CW + 3 * AW) // D
    tok = lambda w, j=0: pl.BlockSpec((tm, w), lambda i: (n - 1 - i, j))
    acc = lambda w: pl.BlockSpec((8, w), lambda i: (0, 0))
    return pl.pallas_call(
        body, name="merge_bwd", grid=(n,),
        in_specs=[tok(D), tok(CW, 0), tok(CW, 1), tok(CW, 2), tok(D, g_off), tok(D, g_off + 1), tok(CW), tok(D), tok(D),
                  _resident((8, CW)), _resident((1, 2 * D)), _resident((D, D)), _resident((D, CW)), _resident((D, AW))],
        out_specs=[tok(3 * CW), tok(2 * D), tok(AW), tok(D), tok(D), acc(CW), acc(2 * D)],
        out_shape=[SDS((T, 3 * CW), BF16), SDS((T, 2 * D), BF16), SDS((T, AW), BF16), SDS((T, D), BF16), SDS((T, D), BF16),
                   SDS((8, CW), F32), SDS((8, 2 * D), F32)],
        scratch_shapes=[pltpu.VMEM((8, CW), F32)],
        compiler_params=_cparams(1),
    )(dx2, pm, pm, pm, pm, pm, u, yc, ya, wc, bg, wo, woc_t, woa_t)


def _attn_bwd(pm, do, fcum, frow, lse, S, q_off, AW, hd, tq):
    T = pm.shape[0]
    scale = float(hd) ** -0.5
    nq, hpg, ng = S // tq, LANES // hd, AW // LANES

    def body(q_ref, k_ref, v_ref, do_ref, fc_ref, fr_ref, lse_ref, dq_ref, dk_ref, dv_ref, dfr_ref, dk_acc, dv_acc):
        g = pl.program_id(1)
        dk_acc[...] = jnp.zeros_like(dk_acc)
        dv_acc[...] = jnp.zeros_like(dv_acc)
        dfr_ref[...] = jnp.zeros_like(dfr_ref)
        for i in range(nq):
            rs, kend = slice(i * tq, (i + 1) * tq), (i + 1) * tq
            row = i * tq + lax.broadcasted_iota(jnp.int32, (tq, kend), 0)
            col = lax.broadcasted_iota(jnp.int32, (tq, kend), 1)
            kk, vv = k_ref[0:kend, :], v_ref[0:kend, :]
            dq_tile = jnp.zeros((tq, LANES), F32)
            for e in range(hpg):
                mask = _head_mask(e, hd)
                qs = jnp.where(mask, q_ref[rs, :], 0) * scale
                doi = jnp.where(mask, do_ref[rs, :], 0)
                s = _dot_nt(qs, kk) + _lane_pick(fc_ref[rs, :], g * hpg + e) - fr_ref[e:e + 1, 0:kend]
                p = jnp.where(col <= row, jnp.exp(s - _lane_pick(lse_ref[rs, :], g * hpg + e)), 0.0)
                dp = _dot_nt(doi, vv)
                ds = p * (dp - jnp.sum(p * dp, axis=1, keepdims=True))
                pb, dsb = p.astype(BF16), ds.astype(BF16)
                dq_tile = jnp.where(mask, _dot(dsb, kk) * scale, dq_tile)
                dv_acc[0:kend, :] += _dot_tn(pb, doi)
                dk_acc[0:kend, :] += _dot_tn(dsb, qs)
                dfr_ref[e:e + 1, 0:kend] -= jnp.sum(ds, axis=0, keepdims=True)
            dq_ref[rs, :] = dq_tile.astype(BF16)
        dk_ref[...] = dk_acc[...].astype(BF16)
        dv_ref[...] = dv_acc[...].astype(BF16)

    full = pl.BlockSpec((S, LANES), lambda b, g: (b, 0))
    grp = pl.BlockSpec((S, LANES), lambda b, g: (b, g))
    rows = pl.BlockSpec((None, None, 8, S), lambda b, g: (b, g, 0, 0))
    return pl.pallas_call(
        body, name="attn_bwd", grid=(T // S, ng),
        in_specs=_attn_specs(S, q_off, AW) + [grp, full, rows, full],
        out_specs=[grp, grp, grp, rows],
        out_shape=[SDS((T, AW), BF16), SDS((T, AW), BF16), SDS((T, AW), BF16), SDS((T // S, ng, 8, S), F32)],
        scratch_shapes=[pltpu.VMEM((S, LANES), F32), pltpu.VMEM((S, LANES), F32)],
        compiler_params=_cparams(2),
    )(pm, pm, pm, do, fcum, frow, lse)


def _forget_bwd(dfc, fl, bf, S):
    T, NF = fl.shape
    ch = _tile(S, 256, 8)

    def body(df_ref, fl_ref, bf_ref, dfl_ref, dbf_ref):
        @pl.when(pl.program_id(0) == 0)
        def _():
            dbf_ref[...] = jnp.zeros_like(dbf_ref)

        row = lax.broadcasted_iota(jnp.int32, (ch, ch), 0)
        col = lax.broadcasted_iota(jnp.int32, (ch, ch), 1)
        tri = (col >= row).astype(BF16)
        carry = jnp.zeros((1, NF), F32)
        for c in range(S - ch, -1, -ch):
            d = df_ref[c:c + ch, :]
            dlf = _tri_dot(tri, d) + carry
            carry = carry + jnp.sum(d, axis=0, keepdims=True)
            dfl = dlf * _sigmoid(-(fl_ref[c:c + ch, :] + bf_ref[...]))
            dfl_ref[c:c + ch, :] = dfl.astype(BF16)
            dbf_ref[0:1, :] += jnp.sum(dfl, axis=0, keepdims=True)

    return pl.pallas_call(
        body, name="forget_bwd", grid=(T // S,),
        in_specs=[pl.BlockSpec((S, NF), lambda b: (b, 0)), pl.BlockSpec((S, NF), lambda b: (b, 0)), _resident((1, NF))],
        out_specs=[pl.BlockSpec((S, NF), lambda b: (b, 0)), pl.BlockSpec((8, NF), lambda b: (0, 0))],
        out_shape=[SDS((T, NF), BF16), SDS((8, NF), F32)],
        compiler_params=_cparams(1),
    )(dfc, fl, bf)


def _inproj_bwd(dparts, offs, w_t, x, g1, dx2, tm):
    T, D = x.shape
    npart = len(dparts)

    def body(*refs):
        d_refs = refs[:npart]
        w_ref, x_ref, g_ref, dx2_ref, dx_ref, dg_ref = refs[npart:]

        @pl.when(pl.program_id(0) == 0)
        def _():
            dg_ref[...] = jnp.zeros_like(dg_ref)

        dh = None
        for d_ref, off in zip(d_refs, offs):
            term = _dot(d_ref[...], w_ref[off:off + d_ref.shape[1], :])
            dh = term if dh is None else dh + term
        xv = x_ref[...]
        r = _rms(xv)
        xn = xv * r
        dg_ref[0:1, :] += jnp.sum(dh * xn, axis=0, keepdims=True)
        dxn = dh * g_ref[...]
        dx_ref[...] = dx2_ref[...] + r * (dxn - xn * jnp.mean(dxn * xn, axis=-1, keepdims=True))

    tok = lambda w: pl.BlockSpec((tm, w), lambda i: (i, 0))
    return pl.pallas_call(
        body, name="inproj_bwd", grid=(T // tm,),
        in_specs=[tok(d.shape[1]) for d in dparts] + [_resident(w_t.shape), tok(D), _resident((1, D)), tok(D)],
        out_specs=[tok(D), pl.BlockSpec((8, D), lambda i: (0, 0))],
        out_shape=[SDS((T, D), F32), SDS((8, D), F32)],
        compiler_params=_cparams(1),
    )(*dparts, w_t, x, g1, dx2)


def _wgrad(b, a, name, into=None, row_off=0, total_rows=None):
    T, N = b.shape
    M = a.shape[1]
    tn = _tile(N, 1408 if M <= 1024 else 512)
    while row_off % tn:
        tn = _tile(N, tn - LANES)
    tk = _tile(T, 1024, 16)
    blk0 = row_off // tn

    def body(b_ref, a_ref, *rest):
        o_ref = rest[-1]

        @pl.when(pl.program_id(1) == 0)
        def _():
            o_ref[...] = jnp.zeros_like(o_ref)

        o_ref[...] += _dot_tn(b_ref[...].astype(BF16), a_ref[...].astype(BF16))

    in_specs = [pl.BlockSpec((tk, tn), lambda j, k: (k, j)), pl.BlockSpec((tk, M), lambda j, k: (k, 0))]
    args = (b, a)
    kwargs = {}
    if into is not None:
        in_specs.append(ANY)
        args += (into,)
        kwargs["input_output_aliases"] = {2: 0}
        total_rows = into.shape[0]
    return pl.pallas_call(
        body, name=name, grid=(N // tn, T // tk),
        in_specs=in_specs,
        out_specs=pl.BlockSpec((tn, M), lambda j, k: (blk0 + j, 0)),
        out_shape=SDS((N if total_rows is None else total_rows, M), F32),
        compiler_params=_cparams(2), **kwargs,
    )(*args)


def _adamw(w, g, m, v, name):
    shape = w.shape
    C = shape[-1]
    w2, g2, m2, v2 = (a.reshape(-1, C) for a in (w, g, m, v))
    R = w2.shape[0]
    tr = R if R <= 512 else _tile(R, 256, 8)

    def body(w_ref, g_ref, m_ref, v_ref, d_ref, nm_ref, nv_ref):
        gv = g_ref[...]
        mv = ADAM_B1 * m_ref[...] + (1.0 - ADAM_B1) * gv
        vv = ADAM_B2 * v_ref[...] + (1.0 - ADAM_B2) * (gv * gv)
        m_hat = mv / (1.0 - ADAM_B1 ** ADAM_STEP)
        v_hat = vv / (1.0 - ADAM_B2 ** ADAM_STEP)
        d_ref[...] = -ADAM_LR * (m_hat / (jnp.sqrt(v_hat) + ADAM_EPS) + ADAM_WD * w_ref[...])
        nm_ref[...] = mv
        nv_ref[...] = vv

    spec = pl.BlockSpec((tr, C), lambda r: (r, 0))
    outs = pl.pallas_call(
        body, name=name, grid=(R // tr,),
        in_specs=[spec] * 4, out_specs=[spec] * 3, out_shape=[SDS((R, C), F32)] * 3,
        compiler_params=_cparams(1),
    )(w2, g2, m2, v2)
    return tuple(o.reshape(shape) for o in outs)


def _rows(a, L):
    lead = a.shape[0]
    flat = a.reshape(lead, -1)
    n = flat.shape[1]
    r = -(-n // L)
    return jnp.pad(flat, ((0, 0), (0, r * L - n))).reshape(lead, r, L)


def _unrows(p, shape):
    return p.reshape(-1)[:int(np.prod(shape))].reshape(shape)


def _col_blocks(a):
    R, C = a.shape
    return a.reshape(R, N_DEV, C // N_DEV).transpose(1, 0, 2)


def _from_col_blocks(a):
    n, R, c = a.shape
    return a.transpose(1, 0, 2).reshape(R, n * c)


def _in_windows(n_loc, nqkv, H, NM):
    win = (n_loc + 7 + 7) // 8 * 8
    starts, index = [], np.zeros((N_DEV, n_loc), np.int32)
    for d in range(N_DEV):
        rows = np.arange(n_loc * d, n_loc * (d + 1))
        is_f = (rows >= nqkv) & (rows < nqkv + H)
        kept = np.where(rows < nqkv, rows, rows - H)
        lo = int(kept[~is_f].min())
        start = lo // 8 * 8
        assert int(kept[~is_f].max()) - start < win and start + win <= NM + LANES
        starts.append(start)
        index[d] = np.where(is_f, win + rows - nqkv, kept - start)
    return starts, win, index


def kernel(x, norm_mix_g, w_in, b_f, b_gate, conv_mix_w, w_out_conv, w_out_attn, w_o, norm_ffn_g, w_up, conv_ffn_w, w_down, norm_f_g, loss_target, m_norm_mix_g, m_w_in, m_b_f, m_b_gate, m_conv_mix_w, m_w_out_conv, m_w_out_attn, m_w_o, m_norm_ffn_g, m_w_up, m_conv_ffn_w, m_w_down, m_norm_f_g, v_norm_mix_g, v_w_in, v_b_f, v_b_gate, v_conv_mix_w, v_w_out_conv, v_w_out_attn, v_w_o, v_norm_ffn_g, v_w_up, v_conv_ffn_w, v_w_down, v_norm_f_g):
    Bl, S, D = x.shape
    T = Bl * S
    H = b_f.shape[-1]
    CW = N_DEV * conv_mix_w.shape[-1]
    AW = w_out_attn.shape[1]
    hd = AW // H
    FH = N_DEV * w_down.shape[1]
    n_loc = w_in.shape[-1]
    NIN = N_DEV * n_loc
    NM = 3 * CW + 3 * AW + 2 * D
    nqkv = 3 * CW + 3 * AW
    assert NIN == NM + H and w_out_conv.shape[1] == CW and nqkv % D == 0 and AW % LANES == 0 and LANES % hd == 0
    hpg, ng = LANES // hd, AW // LANES
    assert hpg <= 8
    tm_big = min(512, S // 2)
    tm_ffn = min(256, S // 2)
    tq = min(256, S // 2)
    px, py, pc = _place()
    me = 4 * px + 2 * py + pc

    bits = lambda a: lax.bitcast_convert_type(a, BF16)
    taps = jnp.concatenate([_rows(bits(conv_ffn_w[0])[None], D)[0], _rows(bits(conv_mix_w[0])[None], D)[0]], axis=0)
    n_ffn_rows = -(-conv_ffn_w[0].size * 2 // D)
    shards = [w_in[0].T.astype(BF16), w_up[0].T.astype(BF16), w_down[0].astype(BF16), w_o[0].astype(BF16),
              w_out_conv[0].T.astype(BF16), w_out_attn[0].T.astype(BF16), taps]
    g_in, g_up, g_dn, g_o, g_oc, g_oa, g_taps = _all_gather(shards, "weights_all_gather")
    W_in_rows = g_in.reshape(NIN, D)
    W_in_t = jnp.concatenate([W_in_rows[:nqkv], W_in_rows[nqkv + H:], W_in_rows[nqkv:nqkv + H],
                              jnp.zeros((LANES - H, D), BF16)], axis=0)
    W_up_t = g_up.reshape(2 * FH, D)
    W_dn = g_dn.reshape(FH, D)
    W_o = g_o.reshape(D, D)
    W_oc_t = g_oc.reshape(D, CW)
    W_oa_t = g_oa.reshape(D, AW)
    tap_bits = g_taps.reshape(N_DEV, -1)
    n_ffn, n_mix = conv_ffn_w[0].size * 2, conv_mix_w[0].size * 2
    wf_full = _from_col_blocks(lax.bitcast_convert_type(
        tap_bits[:, :n_ffn].reshape((N_DEV,) + conv_ffn_w.shape[1:] + (2,)), F32))
    wc_full = _from_col_blocks(lax.bitcast_convert_type(
        tap_bits[:, n_ffn_rows * D:n_ffn_rows * D + n_mix].reshape((N_DEV,) + conv_mix_w.shape[1:] + (2,)), F32))
    wf8 = jnp.pad(wf_full, ((0, 5), (0, 0)))
    wc8 = jnp.pad(wc_full, ((0, 5), (0, 0)))
    bf128 = jnp.pad(b_f, ((0, 0), (0, LANES - H)))

    x2d = x.reshape(T, D)
    tgt = loss_target.reshape(T, D)
    pm, fl, h1 = _inproj_fwd(x2d, norm_mix_g, W_in_t, NM, tm_big)
    fcum = _forget_cumsum(fl, bf128, S)
    frow = jnp.pad(fcum[:, :H].reshape(Bl, S, ng, hpg).transpose(0, 2, 3, 1), ((0, 0), (0, 0), (0, 8 - hpg), (0, 0)))
    q_off = 3 * CW // LANES
    o, lse = _attn_fwd(pm, fcum, frow, S, q_off, AW, hd, tq)
    x2, u, m, ycin, yc, ya = _merge_fwd(pm, o, x2d, wc8, b_gate, W_oc_t, W_oa_t, W_o, S, tm_big)
    upp, h2, dx3, loss8, dgf8 = _ffn_fwd(x2, norm_ffn_g, W_up_t, wf8, W_dn, norm_f_g.reshape(1, D), tgt, S, tm_ffn)

    dx2, dupp, act, dwf8, dg2_8 = _ffn_bwd(dx3, x2, norm_ffn_g, upp, wf8, W_dn, W_up_t, S, tm_ffn)
    dW_dn = _wgrad(act, dx3, "wgrad_down")
    dW_up_t = _wgrad(dupp, h2, "wgrad_up")
    dconv, dgl, do, dyc, dya, dwc8, dbg8 = _merge_bwd(dx2, pm, u, yc, ya, wc8, b_gate, W_o, W_oc_t, W_oa_t, S, tm_big)
    dW_o = _wgrad(m, dx2, "wgrad_o")
    dW_oc_t = _wgrad(dyc, ycin, "wgrad_out_conv")
    dW_oa_t = _wgrad(dya, o, "wgrad_out_attn")
    dq, dk, dv, dfr = _attn_bwd(pm, do, fcum, frow, lse, S, q_off, AW, hd, tq)
    dfc = jnp.pad(dfr[:, :, :hpg, :].transpose(0, 3, 1, 2).reshape(T, H), ((0, 0), (0, LANES - H)))
    dfl, dbf8 = _forget_bwd(dfc, fl, bf128, S)
    dparts = [dconv, dq, dk, dv, dgl, dfl]
    offs = [0, 3 * CW, 3 * CW + AW, 3 * CW + 2 * AW, nqkv, NM]
    grad_x, dg1_8 = _inproj_bwd(dparts, offs, W_in_t, x2d, norm_mix_g, dx2, tm_big)
    dW_in_t = _wgrad(dparts[0], h1, "wgrad_in_0", total_rows=NM + LANES)
    for k in range(1, len(dparts)):
        dW_in_t = _wgrad(dparts[k], h1, "wgrad_in_%d" % k, into=dW_in_t, row_off=offs[k])

    starts, win, index = _in_windows(n_loc, nqkv, H, NM)
    small = jnp.concatenate([dW_o.reshape(N_DEV, -1, D), _rows(dW_oc_t.reshape(N_DEV, -1), D), _rows(dW_oa_t.reshape(N_DEV, -1), D),
                             _rows(_col_blocks(dwf8[:3]), D), _rows(_col_blocks(dwc8[:3]), D)], axis=1)
    small = jnp.pad(small, ((0, 0), (0, -small.shape[1] % 8), (0, 0)))
    blocked = [dW_up_t.reshape(N_DEV, -1, D), dW_dn.reshape(N_DEV, -1, D), small]
    ids = jnp.stack([pc, 2 * px + py]).astype(jnp.int32)
    own_win, sib_win, *sib_blocked = _sibling_exchange(dW_in_t, starts, win, blocked, "grads_sibling_exchange")
    sums = [_pair_sum(own_win, sib_win, ids, "grads_pair_sum_in")]
    sums += [_pair_sum(b, r, ids, "grads_pair_sum_%d" % a) for a, (b, r) in enumerate(zip(blocked, sib_blocked))]
    from_chips = _chip_exchange([s[1] for s in sums], "grads_chip_exchange")
    red_win, red_up, red_dn, red_small = [
        _final_sum(s[0], r, "grads_final_sum_%d" % a) for a, (s, r) in enumerate(zip(sums, from_chips))]

    f_rows = dW_in_t[NM:NM + 8]
    smalls = [dg1_8[0:1], dg2_8[0:1], dgf8[0:1], dbg8[0:1, :D], dbg8[0:1, D:], jnp.pad(dbf8[0:1], ((0, 0), (0, D - LANES))),
              jnp.zeros((2, D), F32), f_rows]
    spack = jnp.concatenate(smalls, axis=0)
    ssum = _sum8(_all_gather([spack], "small_all_gather")[0], "small_sum")
    g_g1, g_g2, g_gf = ssum[0:1], ssum[1:2], ssum[2]
    g_bg = jnp.concatenate([ssum[3:4], ssum[4:5]], axis=1)
    g_bf = ssum[5:6, :H]

    ext = jnp.concatenate([red_win, ssum[8:8 + H]], axis=0)
    my_index = lax.dynamic_index_in_dim(jnp.asarray(index), me, axis=0, keepdims=False)
    g_w_in = jnp.take(ext, my_index, axis=0).T
    g_w_up = red_up.T
    r_o = dW_o.shape[0] // N_DEV
    r_oc = _rows(dW_oc_t.reshape(N_DEV, -1), D).shape[1]
    r_wf = _rows(_col_blocks(dwf8[:3]), D).shape[1]
    r_wc = _rows(_col_blocks(dwc8[:3]), D).shape[1]
    o0, o1, o2, o3 = r_o, r_o + r_oc, r_o + 2 * r_oc, r_o + 2 * r_oc + r_wf
    g_w_o = red_small[:o0]
    g_w_oc = _unrows(red_small[o0:o1], (D // N_DEV, CW)).T
    g_w_oa = _unrows(red_small[o1:o2], (D // N_DEV, AW)).T
    g_wf = _unrows(red_small[o2:o3], conv_ffn_w.shape[1:])
    g_wc = _unrows(red_small[o3:o3 + r_wc], conv_mix_w.shape[1:])

    loss = lax.psum(loss8[0, 0], ("x", "y", "c"))

    names = ["norm_mix_g", "w_in", "b_f", "b_gate", "conv_mix_w", "w_out_conv", "w_out_attn", "w_o", "norm_ffn_g", "w_up",
             "conv_ffn_w", "w_down", "norm_f_g"]
    weights = [norm_mix_g, w_in, b_f, b_gate, conv_mix_w, w_out_conv, w_out_attn, w_o, norm_ffn_g, w_up, conv_ffn_w, w_down, norm_f_g]
    grads = [g_g1, g_w_in, g_bf, g_bg, g_wc, g_w_oc, g_w_oa, g_w_o, g_g2, g_w_up, g_wf, red_dn, g_gf]
    ms = [m_norm_mix_g, m_w_in, m_b_f, m_b_gate, m_conv_mix_w, m_w_out_conv, m_w_out_attn, m_w_o, m_norm_ffn_g, m_w_up,
          m_conv_ffn_w, m_w_down, m_norm_f_g]
    vs = [v_norm_mix_g, v_w_in, v_b_f, v_b_gate, v_conv_mix_w, v_w_out_conv, v_w_out_attn, v_w_o, v_norm_ffn_g, v_w_up,
          v_conv_ffn_w, v_w_down, v_norm_f_g]
    grads = [g.reshape(w.shape) for g, w in zip(grads, weights)]
    steps = [_adamw(w, g, mm, vv, "adamw_" + nm) for nm, w, g, mm, vv in zip(names, weights, grads, ms, vs)]
    deltas, new_ms, new_vs = zip(*steps)
    return (loss, grad_x.reshape(Bl, S, D), *grads, *deltas, *new_ms, *new_vs)
```

```python
import numpy as np

import jax
import jax.numpy as jnp
from jax import lax
from jax.experimental import pallas as pl
from jax.experimental.pallas import tpu as pltpu

F32, BF16 = jnp.float32, jnp.bfloat16
EPS = 1e-6
ADAM_LR, ADAM_B1, ADAM_B2, ADAM_EPS, ADAM_WD, ADAM_STEP = 0.001, 0.9, 0.999, 1e-08, 0.01, 10
N_DEV = 8
LANES = 128
V7X_VMEM_LIMIT = 56 * 1024 * 1024
MESH = pl.DeviceIdType.MESH
SDS = jax.ShapeDtypeStruct
ANY = pl.BlockSpec(memory_space=pl.ANY)


def _tile(n, target, mult=LANES):
    best = None
    for t in range(mult, min(n, target) + 1, mult):
        if n % t == 0:
            best = t
    return best if best is not None else n


def _resident(shape):
    return pl.BlockSpec(shape, lambda *_: (0,) * len(shape), pipeline_mode=pl.Buffered(1))


def _cparams(n_axes=1):
    return pltpu.CompilerParams(dimension_semantics=("arbitrary",) * n_axes, vmem_limit_bytes=V7X_VMEM_LIMIT)


def _dot(a, b):
    return jnp.dot(a, b, preferred_element_type=F32)


def _dot_tn(a, b):
    return lax.dot_general(a, b, (((0,), (0,)), ((), ())), preferred_element_type=F32)


def _dot_nt(a, b):
    return lax.dot_general(a, b, (((1,), (1,)), ((), ())), preferred_element_type=F32)


def _sigmoid(x):
    return 1.0 / (1.0 + jnp.exp(-x))


def _rms(x):
    return lax.rsqrt(jnp.mean(x * x, axis=-1, keepdims=True) + EPS)


def _taps_back(z, r6, r7):
    tm = z.shape[0]
    row = lax.broadcasted_iota(jnp.int32, (tm, 1), 0)
    z1 = jnp.where(row == 0, r7, pltpu.roll(z, 1, 0))
    z2 = jnp.where(row == 0, r6, jnp.where(row == 1, r7, pltpu.roll(z, 2, 0)))
    return z1, z2


def _taps_ahead(d, h0, h1):
    tm = d.shape[0]
    row = lax.broadcasted_iota(jnp.int32, (tm, 1), 0)
    d1 = jnp.where(row == tm - 1, h0, pltpu.roll(d, tm - 1, 0))
    d2 = jnp.where(row == tm - 2, h0, jnp.where(row == tm - 1, h1, pltpu.roll(d, tm - 2, 0)))
    return d1, d2


def _tri_dot(tri, x):
    hi = x.astype(BF16)
    r = x - hi.astype(F32)
    mid = r.astype(BF16)
    lo = (r - mid.astype(F32)).astype(BF16)
    return (_dot(tri, lo) + _dot(tri, mid)) + _dot(tri, hi)


def _lane_pick(block, lane):
    lanes = lax.broadcasted_iota(jnp.int32, (1, block.shape[1]), 1)
    return jnp.sum(jnp.where(lanes == lane, block, 0.0), axis=1, keepdims=True)


def _place():
    return lax.axis_index("x"), lax.axis_index("y"), lax.axis_index("c")


def _all_gather(xs, name):
    n = len(xs)

    def body(*refs):
        start, forward, finish = _gather_phases(refs[:n], refs[n:2 * n], *refs[2 * n:])
        start()
        forward()
        finish()

    return pl.pallas_call(
        body, name=name,
        out_shape=_gather_shapes(xs), in_specs=[ANY] * n, out_specs=[ANY] * n, scratch_shapes=_gather_sems(n),
    )(*xs)


def _gather_shapes(xs):
    return [SDS((N_DEV,) + x.shape, x.dtype) for x in xs]


def _gather_sems(n):
    return [pltpu.SemaphoreType.DMA((7 * n,)), pltpu.SemaphoreType.DMA((7 * n,)), pltpu.SemaphoreType.DMA((n,))]


def _gather_phases(x_refs, out_refs, send_sems, recv_sems, local_sems):
    n = len(x_refs)

    def parts():
        px, py, pc = _place()
        me, sibling = (px, py, pc), (px, py, 1 - pc)
        chips = [(1 - px, py), (px, 1 - py), (1 - px, 1 - py)]

        def slot(a, qx, qy, qc):
            return out_refs[a].at[4 * qx + 2 * qy + qc]

        def copy(a, k, block, to, src=None):
            return pltpu.make_async_remote_copy(
                src_ref=slot(a, *block) if src is None else src, dst_ref=slot(a, *block),
                send_sem=send_sems.at[7 * a + k], recv_sem=recv_sems.at[7 * a + k], device_id=to, device_id_type=MESH)

        def mine():
            return [pltpu.make_async_copy(x_refs[a], slot(a, *me), local_sems.at[a]) for a in range(n)]

        def first():
            out = []
            for a in range(n):
                out.append(copy(a, 0, me, sibling, src=x_refs[a]))
                out += [copy(a, 1 + j, me, (*chip, pc), src=x_refs[a]) for j, chip in enumerate(chips)]
            return out

        def landed():
            return [copy(a, 1 + j, (*chip, pc), me) for j, chip in enumerate(chips) for a in range(n)]

        def passed():
            return [copy(a, 4 + j, (*chip, pc), sibling) for j, chip in enumerate(chips) for a in range(n)]

        def late():
            out = [copy(a, 0, sibling, me) for a in range(n)]
            return out + [copy(a, 4 + j, (*chip, 1 - pc), me) for j, chip in enumerate(chips) for a in range(n)]

        return mine, first, landed, passed, late

    def start():
        mine, first, _, _, _ = parts()
        for cp in mine() + first():
            cp.start()

    def forward():
        _, _, landed, passed, _ = parts()
        for got, cp in zip(landed(), passed()):
            got.wait_recv()
            cp.start()

    def finish():
        mine, first, _, passed, late = parts()
        for cp in late():
            cp.wait_recv()
        for cp in first() + passed():
            cp.wait_send()
        for cp in mine():
            cp.wait()

    return start, forward, finish


def _sibling_exchange(win_buf, win_starts, win_rows, blocked, name):
    nb = len(blocked)

    def body(*refs):
        win_ref, blk_refs = refs[0], refs[1:1 + nb]
        rwin_ref, rblk_refs = refs[1 + nb], refs[2 + nb:2 + 2 * nb]
        send_sems, recv_sems, wsend_sems, wrecv_sems = refs[2 + 2 * nb:]
        px, py, pc = _place()
        copies = _sibling_copies(blk_refs, rblk_refs, send_sems, recv_sems)
        for j in range(4):
            theirs = jnp.where(pc == 0, win_starts[2 * j + 1], win_starts[2 * j])
            copies.append(pltpu.make_async_remote_copy(
                src_ref=win_ref.at[pl.ds(pl.multiple_of(theirs, 8), win_rows)], dst_ref=rwin_ref.at[j],
                send_sem=wsend_sems.at[j], recv_sem=wrecv_sems.at[j], device_id=(px, py, 1 - pc), device_id_type=MESH))
        for cp in copies:
            cp.start()
        for cp in copies:
            cp.wait()

    C = win_buf.shape[1]
    return pl.pallas_call(
        body, name=name,
        out_shape=[SDS((4, win_rows, C), F32)] + _sibling_shapes(blocked),
        in_specs=[ANY] * (1 + nb), out_specs=[ANY] * (1 + nb),
        scratch_shapes=_exchange_sems(nb) + _exchange_sems(4),
    )(win_buf, *blocked)


def _sibling_shapes(blocked):
    return [SDS((4,) + b.shape[2:], F32) for b in blocked]


def _exchange_sems(n):
    return [pltpu.SemaphoreType.DMA((n,)), pltpu.SemaphoreType.DMA((n,))]


def _sibling_copies(blk_refs, out_refs, send_sems, recv_sems):
    px, py, pc = _place()
    return [pltpu.make_async_remote_copy(
        src_ref=b.at[:, 1 - pc], dst_ref=o, send_sem=send_sems.at[a], recv_sem=recv_sems.at[a],
        device_id=(px, py, 1 - pc), device_id_type=MESH) for a, (b, o) in enumerate(zip(blk_refs, out_refs))]


def _chip_shapes(ps):
    return [SDS((3,) + p.shape[1:], p.dtype) for p in ps]


def _chip_copies(p_refs, out_refs, send_sems, recv_sems):
    px, py, pc = _place()
    n = len(p_refs)
    chips = [(1 - px, py), (px, 1 - py), (1 - px, 1 - py)]
    return [pltpu.make_async_remote_copy(
        src_ref=p_refs[a].at[2 * qx + qy], dst_ref=out_refs[a].at[k],
        send_sem=send_sems.at[3 * a + k], recv_sem=recv_sems.at[3 * a + k],
        device_id=(qx, qy, pc), device_id_type=MESH) for k, (qx, qy) in enumerate(chips) for a in range(n)]


def _chip_exchange(ps, name):
    n = len(ps)

    def body(*refs):
        copies = _chip_copies(refs[:n], refs[n:2 * n], *refs[2 * n:])
        for cp in copies:
            cp.start()
        for cp in copies:
            cp.wait()

    return pl.pallas_call(
        body, name=name,
        out_shape=_chip_shapes(ps), in_specs=[ANY] * n, out_specs=[ANY] * n, scratch_shapes=_exchange_sems(3 * n),
    )(*ps)


def _pair_sum(own, recv, ids, name, win_rows=None):
    _, R, C = recv.shape
    tr = _tile(R, 512, 8)

    def body(ids_ref, g_ref, r_ref, own_ref, pb_ref):
        s = g_ref[...] + r_ref[...]
        pb_ref[...] = s.astype(BF16)

        @pl.when(pl.program_id(1) == ids_ref[1])
        def _():
            own_ref[...] = s

    if win_rows is None:
        own_spec = pl.BlockSpec((None, None, tr, C), lambda r, j, ids: (j, ids[0], r, 0))
    else:
        own_spec = pl.BlockSpec((pl.Element(tr), pl.Element(C)), lambda r, j, ids: (pl.multiple_of(ids[2 + j] + r * tr, 8), 0))
    return pl.pallas_call(
        body, name=name,
        grid_spec=pltpu.PrefetchScalarGridSpec(
            num_scalar_prefetch=1, grid=(R // tr, 4),
            in_specs=[own_spec, pl.BlockSpec((None, tr, C), lambda r, j, ids: (j, r, 0))],
            out_specs=[pl.BlockSpec((tr, C), lambda r, j, ids: (r, 0)),
                       pl.BlockSpec((None, tr, C), lambda r, j, ids: (j, r, 0))]),
        out_shape=[SDS((R, C), F32), SDS((4, R, C), BF16)],
        compiler_params=_cparams(2),
    )(ids, own, recv)


def _final_sum(own, recv, name):
    R, C = own.shape
    tr = _tile(R, 512, 8)

    def body(o_ref, r_ref, out_ref):
        out_ref[...] = ((o_ref[...] + r_ref[0].astype(F32)) + r_ref[1].astype(F32)) + r_ref[2].astype(F32)

    return pl.pallas_call(
        body, name=name, grid=(R // tr,),
        in_specs=[pl.BlockSpec((tr, C), lambda r: (r, 0)), pl.BlockSpec((3, tr, C), lambda r: (0, r, 0))],
        out_specs=pl.BlockSpec((tr, C), lambda r: (r, 0)),
        out_shape=SDS((R, C), F32),
        compiler_params=_cparams(1),
    )(own, recv)


def _sum8(a, name):
    def body(a_ref, out_ref):
        s = a_ref[0]
        for d in range(1, N_DEV):
            s = s + a_ref[d]
        out_ref[...] = s

    return pl.pallas_call(body, name=name, out_shape=SDS(a.shape[1:], F32))(a)


def _inproj_fwd(x, g1, w_t, NM, tm):
    T, D = x.shape
    NF = w_t.shape[0] - NM
    ch = _tile(NM, 1024)

    def body(x_ref, g_ref, w_ref, pm_ref, fl_ref, h_ref):
        xv = x_ref[...]
        h = (xv * _rms(xv) * g_ref[...]).astype(BF16)
        h_ref[...] = h
        for c in range(0, NM, ch):
            pm_ref[:, c:c + ch] = _dot_nt(h, w_ref[c:c + ch, :]).astype(BF16)
        fl_ref[...] = _dot_nt(h, w_ref[NM:NM + NF, :])

    return pl.pallas_call(
        body, name="inproj_fwd", grid=(T // tm,),
        in_specs=[pl.BlockSpec((tm, D), lambda i: (i, 0)), _resident((1, D)), _resident(w_t.shape)],
        out_specs=[pl.BlockSpec((tm, NM), lambda i: (i, 0)), pl.BlockSpec((tm, NF), lambda i: (i, 0)),
                   pl.BlockSpec((tm, D), lambda i: (i, 0))],
        out_shape=[SDS((T, NM), BF16), SDS((T, NF), F32), SDS((T, D), BF16)],
        compiler_params=_cparams(1),
    )(x, g1, w_t)


def _log_sigmoid(x):
    return jnp.minimum(x, 0.0) - jnp.log(1.0 + jnp.exp(-jnp.abs(x)))


def _forget_cumsum(fl, bf, S):
    T, NF = fl.shape
    ch = _tile(S, 256, 8)

    def body(fl_ref, bf_ref, f_ref):
        row = lax.broadcasted_iota(jnp.int32, (ch, ch), 0)
        col = lax.broadcasted_iota(jnp.int32, (ch, ch), 1)
        tri = (col <= row).astype(BF16)
        carry = jnp.zeros((1, NF), F32)
        for c in range(0, S, ch):
            lf = _log_sigmoid(fl_ref[c:c + ch, :] + bf_ref[...])
            f_ref[c:c + ch, :] = _tri_dot(tri, lf) + carry
            carry = carry + jnp.sum(lf, axis=0, keepdims=True)

    return pl.pallas_call(
        body, name="forget_cumsum", grid=(T // S,),
        in_specs=[pl.BlockSpec((S, NF), lambda b: (b, 0)), _resident((1, NF))],
        out_specs=pl.BlockSpec((S, NF), lambda b: (b, 0)),
        out_shape=SDS((T, NF), F32),
        compiler_params=_cparams(1),
    )(fl, bf)


def _head_mask(e, hd):
    lanes = lax.broadcasted_iota(jnp.int32, (1, LANES), 1)
    return (lanes >= e * hd) & (lanes < (e + 1) * hd)


def _attn_specs(S, q_off, AW):
    ng = AW // LANES
    return [pl.BlockSpec((S, LANES), lambda b, g, o=q_off + w * ng: (b, o + g)) for w in range(3)]


def _attn_fwd(pm, fcum, frow, S, q_off, AW, hd, tq, gather):
    T = pm.shape[0]
    scale = float(hd) ** -0.5
    nq, hpg = S // tq, LANES // hd
    ng, nx = AW // LANES, len(gather)
    steps = (T // S) * ng

    def body(q_ref, k_ref, v_ref, fc_ref, fr_ref, *rest):
        x_refs, (o_ref, lse_ref), out_refs, sems = rest[:nx], rest[nx:nx + 2], rest[nx + 2:2 * nx + 2], rest[2 * nx + 2:]
        g = pl.program_id(1)
        step = pl.program_id(0) * ng + g
        start, forward, finish = _gather_phases(x_refs, out_refs, *sems)
        pl.when(step == 0)(start)

        @pl.when(g == 0)
        def _():
            lse_ref[...] = jnp.zeros_like(lse_ref)

        lanes = lax.broadcasted_iota(jnp.int32, (1, LANES), 1)
        for i in range(nq):
            rs, kend = slice(i * tq, (i + 1) * tq), (i + 1) * tq
            row = i * tq + lax.broadcasted_iota(jnp.int32, (tq, kend), 0)
            col = lax.broadcasted_iota(jnp.int32, (tq, kend), 1)
            o_tile = jnp.zeros((tq, LANES), F32)
            lse_tile = lse_ref[rs, :]
            for e in range(hpg):
                mask = _head_mask(e, hd)
                qs = jnp.where(mask, q_ref[rs, :], 0) * scale
                s = _dot_nt(qs, k_ref[0:kend, :]) + _lane_pick(fc_ref[rs, :], g * hpg + e) - fr_ref[e:e + 1, 0:kend]
                s = jnp.where(col <= row, s, -1e30)
                m = jnp.max(s, axis=1, keepdims=True)
                p = jnp.exp(s - m)
                l = jnp.sum(p, axis=1, keepdims=True)
                o_tile = jnp.where(mask, _dot(p.astype(BF16), v_ref[0:kend, :]) / l, o_tile)
                lse_tile = jnp.where(lanes == g * hpg + e, m + jnp.log(l), lse_tile)
            o_ref[rs, :] = o_tile.astype(BF16)
            lse_ref[rs, :] = lse_tile
        pl.when(step == (steps * 5) // 8)(forward)
        pl.when(step == steps - 1)(finish)

    full = pl.BlockSpec((S, LANES), lambda b, g: (b, 0))
    return pl.pallas_call(
        body, name="attn_fwd", grid=(T // S, ng),
        in_specs=_attn_specs(S, q_off, AW) + [full, pl.BlockSpec((None, None, 8, S), lambda b, g: (b, g, 0, 0))] + [ANY] * nx,
        out_specs=[pl.BlockSpec((S, LANES), lambda b, g: (b, g)), full] + [ANY] * nx,
        out_shape=[SDS((T, AW), BF16), SDS((T, LANES), F32)] + _gather_shapes(gather),
        scratch_shapes=_gather_sems(nx),
        compiler_params=_cparams(2),
    )(pm, pm, pm, fcum, frow, *gather)


def _merge_fwd(pm, o, x, wc, bg, woc_t, woa_t, wo, S, tm):
    T, D = x.shape
    CW, AW = woc_t.shape[1], woa_t.shape[1]
    tps = S // tm

    def body(cb_ref, cc_ref, cin_ref, gc_ref, ga_ref, o_ref, x_ref, wc_ref, bg_ref, woct_ref, woat_ref, wo_ref,
             x2_ref, u_ref, m_ref, ycin_ref, yc_ref, ya_ref, tail_ref):
        @pl.when(pl.program_id(0) % tps == 0)
        def _():
            tail_ref[...] = jnp.zeros_like(tail_ref)

        z = cc_ref[...].astype(F32) * cin_ref[...].astype(F32)
        z1, z2 = _taps_back(z, tail_ref[6:7, :], tail_ref[7:8, :])
        tail_ref[...] = z[tm - 8:tm, :]
        u = (z * wc_ref[2:3, :] + z2 * wc_ref[0:1, :]) + z1 * wc_ref[1:2, :]
        u_ref[...] = u.astype(BF16)
        ycin = (cb_ref[...].astype(F32) * u).astype(BF16)
        ycin_ref[...] = ycin
        yc = _dot_nt(ycin, woct_ref[...])
        ya = _dot_nt(o_ref[...], woat_ref[...])
        yc_ref[...] = yc.astype(BF16)
        ya_ref[...] = ya.astype(BF16)
        gc = _sigmoid(gc_ref[...].astype(F32) + bg_ref[:, 0:D])
        ga = _sigmoid(ga_ref[...].astype(F32) + bg_ref[:, D:2 * D])
        m = (gc * yc + ga * ya).astype(BF16)
        m_ref[...] = m
        x2_ref[...] = x_ref[...] + _dot(m, wo_ref[...])

    g_off = (3 * CW + 3 * AW) // D
    tok = lambda w, j=0: pl.BlockSpec((tm, w), lambda i: (i, j))
    return pl.pallas_call(
        body, name="merge_fwd", grid=(T // tm,),
        in_specs=[tok(CW, 0), tok(CW, 1), tok(CW, 2), tok(D, g_off), tok(D, g_off + 1), tok(AW), tok(D),
                  _resident((8, CW)), _resident((1, 2 * D)), _resident((D, CW)), _resident((D, AW)), _resident((D, D))],
        out_specs=[tok(D), tok(CW), tok(D), tok(CW), tok(D), tok(D)],
        out_shape=[SDS((T, D), F32), SDS((T, CW), BF16), SDS((T, D), BF16), SDS((T, CW), BF16),
                   SDS((T, D), BF16), SDS((T, D), BF16)],
        scratch_shapes=[pltpu.VMEM((8, CW), F32)],
        compiler_params=_cparams(1),
    )(pm, pm, pm, pm, pm, o, x, wc, bg, woc_t, woa_t, wo)


def _ffn_fwd(x2, g2, wup_t, wf, wdn, gf, tgt, S, tm):
    T, D = x2.shape
    FH = wdn.shape[0]
    ch = _tile(FH, 1408)
    tps = S // tm

    def body(x2_ref, g2_ref, wupt_ref, wf_ref, wdn_ref, gf_ref, tgt_ref,
             upp_ref, h2_ref, dx3_ref, loss_ref, dgf_ref, tail_ref):
        i = pl.program_id(0)

        @pl.when(i % tps == 0)
        def _():
            tail_ref[...] = jnp.zeros_like(tail_ref)

        @pl.when(i == 0)
        def _():
            loss_ref[...] = jnp.zeros_like(loss_ref)
            dgf_ref[...] = jnp.zeros_like(dgf_ref)

        x2v = x2_ref[...]
        h2 = (x2v * _rms(x2v) * g2_ref[...]).astype(BF16)
        h2_ref[...] = h2
        x3 = x2v
        for c in range(0, FH, ch):
            gated = []
            for cols in (slice(c, c + ch), slice(FH + c, FH + c + ch)):
                upp = _dot_nt(h2, wupt_ref[cols, :])
                upp_ref[:, cols] = upp.astype(BF16)
                p1, p2 = _taps_back(upp, tail_ref[6:7, cols], tail_ref[7:8, cols])
                tail_ref[:, cols] = upp[tm - 8:tm, :]
                gated.append((upp * wf_ref[2:3, cols] + p2 * wf_ref[0:1, cols]) + p1 * wf_ref[1:2, cols])
            a, b = gated
            act = (a * _sigmoid(a) * b).astype(BF16)
            x3 = x3 + _dot(act, wdn_ref[c:c + ch, :])
        r3 = _rms(x3)
        xn3 = x3 * r3
        e = xn3 * gf_ref[...] - tgt_ref[...]
        loss_ref[...] += 0.5 * jnp.sum(jnp.mean(e * e, axis=-1, keepdims=True), axis=0, keepdims=True)
        dy = e / D
        dgf_ref[0:1, :] += jnp.sum(dy * xn3, axis=0, keepdims=True)
        dxn = dy * gf_ref[...]
        dx3_ref[...] = r3 * (dxn - xn3 * jnp.mean(dxn * xn3, axis=-1, keepdims=True))

    tok = lambda w: pl.BlockSpec((tm, w), lambda i: (i, 0))
    return pl.pallas_call(
        body, name="ffn_fwd", grid=(T // tm,),
        in_specs=[tok(D), _resident((1, D)), _resident((2 * FH, D)), _resident((8, 2 * FH)), _resident((FH, D)),
                  _resident((1, D)), tok(D)],
        out_specs=[tok(2 * FH), tok(D), tok(D), pl.BlockSpec((8, LANES), lambda i: (0, 0)),
                   pl.BlockSpec((8, D), lambda i: (0, 0))],
        out_shape=[SDS((T, 2 * FH), BF16), SDS((T, D), BF16), SDS((T, D), F32), SDS((8, LANES), F32), SDS((8, D), F32)],
        scratch_shapes=[pltpu.VMEM((8, 2 * FH), F32)],
        compiler_params=_cparams(1),
    )(x2, g2, wup_t, wf, wdn, gf, tgt)


def _ffn_bwd(dx3, x2, g2, upp, wf, wdn, wup_t, S, tm):
    T, D = x2.shape
    FH = wdn.shape[0]
    ch = _tile(FH, 1408)
    n, tps = T // tm, S // tm
    hb = tm // 16

    def body(dx3_ref, x2_ref, g2_ref, upp_ref, halo_ref, wf_ref, wdn_ref, wupt_ref,
             dx2_ref, dupp_ref, act_ref, dwf_ref, dg2_ref, head_ref, prev_ref):
        i = pl.program_id(0)
        t = n - 1 - i

        @pl.when(i == 0)
        def _():
            dwf_ref[...] = jnp.zeros_like(dwf_ref)
            dg2_ref[...] = jnp.zeros_like(dg2_ref)

        @pl.when(t % tps == tps - 1)
        def _():
            head_ref[...] = jnp.zeros_like(head_ref)

        keep = jnp.where(t % tps == 0, 0.0, 1.0)
        prev_ref[...] = halo_ref[...].astype(F32) * keep
        dx3v = dx3_ref[...]
        dx3b = dx3v.astype(BF16)
        x2v = x2_ref[...]
        r2 = _rms(x2v)
        xn2 = x2v * r2
        dh2 = jnp.zeros((tm, D), F32)
        for c in range(0, FH, ch):
            ca, cb = slice(c, c + ch), slice(FH + c, FH + c + ch)
            pre, post = [], []
            for cols in (ca, cb):
                u0 = upp_ref[:, cols].astype(F32)
                u1, u2 = _taps_back(u0, prev_ref[14:15, cols], prev_ref[15:16, cols])
                pre.append(u0)
                post.append((u0 * wf_ref[2:3, cols] + u2 * wf_ref[0:1, cols]) + u1 * wf_ref[1:2, cols])
            a, b = post
            sig = _sigmoid(a)
            sl = a * sig
            act_ref[:, ca] = (sl * b).astype(BF16)
            dact = _dot_nt(dx3b, wdn_ref[ca, :])
            grads = (dact * b * (sig * (1.0 + a * (1.0 - sig))), dact * sl)
            for cols, u0, d in zip((ca, cb), pre, grads):
                d1, d2 = _taps_ahead(d, head_ref[0:1, cols], head_ref[1:2, cols])
                head_ref[:, cols] = d[0:8, :]
                dwf_ref[2:3, cols] += jnp.sum(u0 * d, axis=0, keepdims=True)
                dwf_ref[1:2, cols] += jnp.sum(u0 * d1, axis=0, keepdims=True)
                dwf_ref[0:1, cols] += jnp.sum(u0 * d2, axis=0, keepdims=True)
                dpre = ((d * wf_ref[2:3, cols] + d1 * wf_ref[1:2, cols]) + d2 * wf_ref[0:1, cols]).astype(BF16)
                dupp_ref[:, cols] = dpre
                dh2 = dh2 + _dot(dpre, wupt_ref[cols, :])
        dg2_ref[0:1, :] += jnp.sum(dh2 * xn2, axis=0, keepdims=True)
        dxn = dh2 * g2_ref[...]
        dx2_ref[...] = dx3v + r2 * (dxn - xn2 * jnp.mean(dxn * xn2, axis=-1, keepdims=True))

    tok = lambda w: pl.BlockSpec((tm, w), lambda i: (n - 1 - i, 0))
    halo = pl.BlockSpec((16, 2 * FH), lambda i: (jnp.maximum((n - 1 - i) * hb - 1, 0), 0))
    acc = lambda w: pl.BlockSpec((8, w), lambda i: (0, 0))
    return pl.pallas_call(
        body, name="ffn_bwd", grid=(n,),
        in_specs=[tok(D), tok(D), _resident((1, D)), tok(2 * FH), halo, _resident((8, 2 * FH)),
                  _resident((FH, D)), _resident((2 * FH, D))],
        out_specs=[tok(D), tok(2 * FH), tok(FH), acc(2 * FH), acc(D)],
        out_shape=[SDS((T, D), F32), SDS((T, 2 * FH), BF16), SDS((T, FH), BF16), SDS((8, 2 * FH), F32), SDS((8, D), F32)],
        scratch_shapes=[pltpu.VMEM((8, 2 * FH), F32), pltpu.VMEM((16, 2 * FH), F32)],
        compiler_params=_cparams(1),
    )(dx3, x2, g2, upp, upp, wf, wdn, wup_t)


def _merge_bwd(dx2, pm, u, yc, ya, wc, bg, wo, woc_t, woa_t, S, tm, exchange):
    T, D = dx2.shape
    CW, AW = woc_t.shape[1], woa_t.shape[1]
    n, tps = T // tm, S // tm
    nx = len(exchange)

    def body(dx2_ref, cb_ref, cc_ref, cin_ref, gc_ref, ga_ref, u_ref, yc_ref, ya_ref, wc_ref, bg_ref,
             wo_ref, woct_ref, woat_ref, *rest):
        x_refs = rest[:nx]
        dconv_ref, dgl_ref, do_ref, dyc_ref, dya_ref, dwc_ref, dbg_ref = rest[nx:nx + 7]
        got_refs, head_ref, sems = rest[nx + 7:2 * nx + 7], rest[2 * nx + 7], rest[2 * nx + 8:]
        i = pl.program_id(0)
        t = n - 1 - i

        @pl.when(i == 0)
        def _():
            for cp in _sibling_copies(x_refs, got_refs, *sems):
                cp.start()

        @pl.when(i == 0)
        def _():
            dwc_ref[...] = jnp.zeros_like(dwc_ref)
            dbg_ref[...] = jnp.zeros_like(dbg_ref)

        @pl.when(t % tps == tps - 1)
        def _():
            head_ref[...] = jnp.zeros_like(head_ref)

        dm = _dot_nt(dx2_ref[...].astype(BF16), wo_ref[...])
        outs = []
        for g_ref, y_ref, cols in ((gc_ref, yc_ref, slice(0, D)), (ga_ref, ya_ref, slice(D, 2 * D))):
            g = _sigmoid(g_ref[...].astype(F32) + bg_ref[:, cols])
            dgl = dm * y_ref[...].astype(F32) * g * (1.0 - g)
            dgl_ref[:, cols] = dgl.astype(BF16)
            dbg_ref[0:1, cols] += jnp.sum(dgl, axis=0, keepdims=True)
            outs.append((dm * g).astype(BF16))
        dyc, dya = outs
        dyc_ref[...] = dyc
        dya_ref[...] = dya
        do_ref[...] = _dot(dya, woat_ref[...]).astype(BF16)
        dycin = _dot(dyc, woct_ref[...])
        cc, cin = cc_ref[...].astype(F32), cin_ref[...].astype(F32)
        z = cc * cin
        du = dycin * cb_ref[...].astype(F32)
        du1, du2 = _taps_ahead(du, head_ref[0:1, :], head_ref[1:2, :])
        head_ref[...] = du[0:8, :]
        dwc_ref[2:3, :] += jnp.sum(z * du, axis=0, keepdims=True)
        dwc_ref[1:2, :] += jnp.sum(z * du1, axis=0, keepdims=True)
        dwc_ref[0:1, :] += jnp.sum(z * du2, axis=0, keepdims=True)
        dz = (du * wc_ref[2:3, :] + du1 * wc_ref[1:2, :]) + du2 * wc_ref[0:1, :]
        dconv_ref[:, 0:CW] = (dycin * u_ref[...].astype(F32)).astype(BF16)
        dconv_ref[:, CW:2 * CW] = (dz * cin).astype(BF16)
        dconv_ref[:, 2 * CW:3 * CW] = (dz * cc).astype(BF16)

        @pl.when(i == n - 1)
        def _():
            for cp in _sibling_copies(x_refs, got_refs, *sems):
                cp.wait()

    g_off = (3 * CW + 3 * AW) // D
    tok = lambda w, j=0: pl.BlockSpec((tm, w), lambda i: (n - 1 - i, j))
    acc = lambda w: pl.BlockSpec((8, w), lambda i: (0, 0))
    return pl.pallas_call(
        body, name="merge_bwd", grid=(n,),
        in_specs=[tok(D), tok(CW, 0), tok(CW, 1), tok(CW, 2), tok(D, g_off), tok(D, g_off + 1), tok(CW), tok(D), tok(D),
                  _resident((8, CW)), _resident((1, 2 * D)), _resident((D, D)), _resident((D, CW)), _resident((D, AW))]
                 + [ANY] * nx,
        out_specs=[tok(3 * CW), tok(2 * D), tok(AW), tok(D), tok(D), acc(CW), acc(2 * D)] + [ANY] * nx,
        out_shape=[SDS((T, 3 * CW), BF16), SDS((T, 2 * D), BF16), SDS((T, AW), BF16), SDS((T, D), BF16), SDS((T, D), BF16),
                   SDS((8, CW), F32), SDS((8, 2 * D), F32)] + _sibling_shapes(exchange),
        scratch_shapes=[pltpu.VMEM((8, CW), F32)] + _exchange_sems(nx),
        compiler_params=_cparams(1),
    )(dx2, pm, pm, pm, pm, pm, u, yc, ya, wc, bg, wo, woc_t, woa_t, *exchange)


def _attn_bwd(pm, do, fcum, frow, lse, S, q_off, AW, hd, tq, exchange):
    T = pm.shape[0]
    scale = float(hd) ** -0.5
    nq, hpg, ng = S // tq, LANES // hd, AW // LANES
    nx = len(exchange)
    steps = (T // S) * ng

    def body(q_ref, k_ref, v_ref, do_ref, fc_ref, fr_ref, lse_ref, *rest):
        x_refs, (dq_ref, dk_ref, dv_ref, dfr_ref) = rest[:nx], rest[nx:nx + 4]
        got_refs, (dk_acc, dv_acc), sems = rest[nx + 4:2 * nx + 4], rest[2 * nx + 4:2 * nx + 6], rest[2 * nx + 6:]
        g = pl.program_id(1)
        step = pl.program_id(0) * ng + g

        @pl.when(step == 0)
        def _():
            for cp in _chip_copies(x_refs, got_refs, *sems):
                cp.start()

        dk_acc[...] = jnp.zeros_like(dk_acc)
        dv_acc[...] = jnp.zeros_like(dv_acc)
        dfr_ref[...] = jnp.zeros_like(dfr_ref)
        for i in range(nq):
            rs, kend = slice(i * tq, (i + 1) * tq), (i + 1) * tq
            row = i * tq + lax.broadcasted_iota(jnp.int32, (tq, kend), 0)
            col = lax.broadcasted_iota(jnp.int32, (tq, kend), 1)
            kk, vv = k_ref[0:kend, :], v_ref[0:kend, :]
            dq_tile = jnp.zeros((tq, LANES), F32)
            for e in range(hpg):
                mask = _head_mask(e, hd)
                qs = jnp.where(mask, q_ref[rs, :], 0) * scale
                doi = jnp.where(mask, do_ref[rs, :], 0)
                s = _dot_nt(qs, kk) + _lane_pick(fc_ref[rs, :], g * hpg + e) - fr_ref[e:e + 1, 0:kend]
                p = jnp.where(col <= row, jnp.exp(s - _lane_pick(lse_ref[rs, :], g * hpg + e)), 0.0)
                dp = _dot_nt(doi, vv)
                ds = p * (dp - jnp.sum(p * dp, axis=1, keepdims=True))
                pb, dsb = p.astype(BF16), ds.astype(BF16)
                dq_tile = jnp.where(mask, _dot(dsb, kk) * scale, dq_tile)
                dv_acc[0:kend, :] += _dot_tn(pb, doi)
                dk_acc[0:kend, :] += _dot_tn(dsb, qs)
                dfr_ref[e:e + 1, 0:kend] -= jnp.sum(ds, axis=0, keepdims=True)
            dq_ref[rs, :] = dq_tile.astype(BF16)
        dk_ref[...] = dk_acc[...].astype(BF16)
        dv_ref[...] = dv_acc[...].astype(BF16)

        @pl.when(step == steps - 1)
        def _():
            for cp in _chip_copies(x_refs, got_refs, *sems):
                cp.wait()

    full = pl.BlockSpec((S, LANES), lambda b, g: (b, 0))
    grp = pl.BlockSpec((S, LANES), lambda b, g: (b, g))
    rows = pl.BlockSpec((None, None, 8, S), lambda b, g: (b, g, 0, 0))
    return pl.pallas_call(
        body, name="attn_bwd", grid=(T // S, ng),
        in_specs=_attn_specs(S, q_off, AW) + [grp, full, rows, full] + [ANY] * nx,
        out_specs=[grp, grp, grp, rows] + [ANY] * nx,
        out_shape=[SDS((T, AW), BF16), SDS((T, AW), BF16), SDS((T, AW), BF16), SDS((T // S, ng, 8, S), F32)]
                  + _chip_shapes(exchange),
        scratch_shapes=[pltpu.VMEM((S, LANES), F32), pltpu.VMEM((S, LANES), F32)] + _exchange_sems(3 * nx),
        compiler_params=_cparams(2),
    )(pm, pm, pm, do, fcum, frow, lse, *exchange)


def _forget_bwd(dfc, fl, bf, S):
    T, NF = fl.shape
    ch = _tile(S, 256, 8)

    def body(df_ref, fl_ref, bf_ref, dfl_ref, dbf_ref):
        @pl.when(pl.program_id(0) == 0)
        def _():
            dbf_ref[...] = jnp.zeros_like(dbf_ref)

        row = lax.broadcasted_iota(jnp.int32, (ch, ch), 0)
        col = lax.broadcasted_iota(jnp.int32, (ch, ch), 1)
        tri = (col >= row).astype(BF16)
        carry = jnp.zeros((1, NF), F32)
        for c in range(S - ch, -1, -ch):
            d = df_ref[c:c + ch, :]
            dlf = _tri_dot(tri, d) + carry
            carry = carry + jnp.sum(d, axis=0, keepdims=True)
            dfl = dlf * _sigmoid(-(fl_ref[c:c + ch, :] + bf_ref[...]))
            dfl_ref[c:c + ch, :] = dfl.astype(BF16)
            dbf_ref[0:1, :] += jnp.sum(dfl, axis=0, keepdims=True)

    return pl.pallas_call(
        body, name="forget_bwd", grid=(T // S,),
        in_specs=[pl.BlockSpec((S, NF), lambda b: (b, 0)), pl.BlockSpec((S, NF), lambda b: (b, 0)), _resident((1, NF))],
        out_specs=[pl.BlockSpec((S, NF), lambda b: (b, 0)), pl.BlockSpec((8, NF), lambda b: (0, 0))],
        out_shape=[SDS((T, NF), BF16), SDS((8, NF), F32)],
        compiler_params=_cparams(1),
    )(dfc, fl, bf)


def _inproj_bwd(dparts, offs, w_t, x, g1, dx2, tm):
    T, D = x.shape
    npart = len(dparts)

    def body(*refs):
        d_refs = refs[:npart]
        w_ref, x_ref, g_ref, dx2_ref, dx_ref, dg_ref = refs[npart:]

        @pl.when(pl.program_id(0) == 0)
        def _():
            dg_ref[...] = jnp.zeros_like(dg_ref)

        dh = None
        for d_ref, off in zip(d_refs, offs):
            term = _dot(d_ref[...], w_ref[off:off + d_ref.shape[1], :])
            dh = term if dh is None else dh + term
        xv = x_ref[...]
        r = _rms(xv)
        xn = xv * r
        dg_ref[0:1, :] += jnp.sum(dh * xn, axis=0, keepdims=True)
        dxn = dh * g_ref[...]
        dx_ref[...] = dx2_ref[...] + r * (dxn - xn * jnp.mean(dxn * xn, axis=-1, keepdims=True))

    tok = lambda w: pl.BlockSpec((tm, w), lambda i: (i, 0))
    return pl.pallas_call(
        body, name="inproj_bwd", grid=(T // tm,),
        in_specs=[tok(d.shape[1]) for d in dparts] + [_resident(w_t.shape), tok(D), _resident((1, D)), tok(D)],
        out_specs=[tok(D), pl.BlockSpec((8, D), lambda i: (0, 0))],
        out_shape=[SDS((T, D), F32), SDS((8, D), F32)],
        compiler_params=_cparams(1),
    )(*dparts, w_t, x, g1, dx2)


def _wgrad(b, a, name, into=None, row_off=0, total_rows=None):
    T, N = b.shape
    M = a.shape[1]
    tn = _tile(N, 1408 if M <= 1024 else 512)
    while row_off % tn:
        tn = _tile(N, tn - LANES)
    tk = _tile(T, 1024, 16)
    blk0 = row_off // tn

    def body(b_ref, a_ref, *rest):
        o_ref = rest[-1]

        @pl.when(pl.program_id(1) == 0)
        def _():
            o_ref[...] = jnp.zeros_like(o_ref)

        o_ref[...] += _dot_tn(b_ref[...].astype(BF16), a_ref[...].astype(BF16))

    in_specs = [pl.BlockSpec((tk, tn), lambda j, k: (k, j)), pl.BlockSpec((tk, M), lambda j, k: (k, 0))]
    args = (b, a)
    kwargs = {}
    if into is not None:
        in_specs.append(ANY)
        args += (into,)
        kwargs["input_output_aliases"] = {2: 0}
        total_rows = into.shape[0]
    return pl.pallas_call(
        body, name=name, grid=(N // tn, T // tk),
        in_specs=in_specs,
        out_specs=pl.BlockSpec((tn, M), lambda j, k: (blk0 + j, 0)),
        out_shape=SDS((N if total_rows is None else total_rows, M), F32),
        compiler_params=_cparams(2), **kwargs,
    )(*args)


def _adamw(w, g, m, v, name):
    shape = w.shape
    C = shape[-1]
    w2, g2, m2, v2 = (a.reshape(-1, C) for a in (w, g, m, v))
    R = w2.shape[0]
    tr = R if R <= 512 else _tile(R, 256, 8)

    def body(w_ref, g_ref, m_ref, v_ref, d_ref, nm_ref, nv_ref):
        gv = g_ref[...]
        mv = ADAM_B1 * m_ref[...] + (1.0 - ADAM_B1) * gv
        vv = ADAM_B2 * v_ref[...] + (1.0 - ADAM_B2) * (gv * gv)
        m_hat = mv / (1.0 - ADAM_B1 ** ADAM_STEP)
        v_hat = vv / (1.0 - ADAM_B2 ** ADAM_STEP)
        d_ref[...] = -ADAM_LR * (m_hat / (jnp.sqrt(v_hat) + ADAM_EPS) + ADAM_WD * w_ref[...])
        nm_ref[...] = mv
        nv_ref[...] = vv

    spec = pl.BlockSpec((tr, C), lambda r: (r, 0))
    outs = pl.pallas_call(
        body, name=name, grid=(R // tr,),
        in_specs=[spec] * 4, out_specs=[spec] * 3, out_shape=[SDS((R, C), F32)] * 3,
        compiler_params=_cparams(1),
    )(w2, g2, m2, v2)
    return tuple(o.reshape(shape) for o in outs)


def _rows(a, L):
    lead = a.shape[0]
    flat = a.reshape(lead, -1)
    n = flat.shape[1]
    r = -(-n // L)
    return jnp.pad(flat, ((0, 0), (0, r * L - n))).reshape(lead, r, L)


def _unrows(p, shape):
    return p.reshape(-1)[:int(np.prod(shape))].reshape(shape)


def _col_blocks(a):
    R, C = a.shape
    return a.reshape(R, N_DEV, C // N_DEV).transpose(1, 0, 2)


def _from_col_blocks(a):
    n, R, c = a.shape
    return a.transpose(1, 0, 2).reshape(R, n * c)


def _in_windows(n_loc, nqkv, H, NM):
    win = (n_loc + 7 + 7) // 8 * 8
    starts, index = [], np.zeros((N_DEV, n_loc), np.int32)
    for d in range(N_DEV):
        rows = np.arange(n_loc * d, n_loc * (d + 1))
        is_f = (rows >= nqkv) & (rows < nqkv + H)
        kept = np.where(rows < nqkv, rows, rows - H)
        lo = int(kept[~is_f].min())
        start = lo // 8 * 8
        assert int(kept[~is_f].max()) - start < win and start + win <= NM + LANES
        starts.append(start)
        index[d] = np.where(is_f, win + rows - nqkv, kept - start)
    return starts, win, index


def kernel(x, norm_mix_g, w_in, b_f, b_gate, conv_mix_w, w_out_conv, w_out_attn, w_o, norm_ffn_g, w_up, conv_ffn_w, w_down, norm_f_g, loss_target, m_norm_mix_g, m_w_in, m_b_f, m_b_gate, m_conv_mix_w, m_w_out_conv, m_w_out_attn, m_w_o, m_norm_ffn_g, m_w_up, m_conv_ffn_w, m_w_down, m_norm_f_g, v_norm_mix_g, v_w_in, v_b_f, v_b_gate, v_conv_mix_w, v_w_out_conv, v_w_out_attn, v_w_o, v_norm_ffn_g, v_w_up, v_conv_ffn_w, v_w_down, v_norm_f_g):
    Bl, S, D = x.shape
    T = Bl * S
    H = b_f.shape[-1]
    CW = N_DEV * conv_mix_w.shape[-1]
    AW = w_out_attn.shape[1]
    hd = AW // H
    FH = N_DEV * w_down.shape[1]
    n_loc = w_in.shape[-1]
    NIN = N_DEV * n_loc
    NM = 3 * CW + 3 * AW + 2 * D
    nqkv = 3 * CW + 3 * AW
    assert NIN == NM + H and w_out_conv.shape[1] == CW and nqkv % D == 0 and AW % LANES == 0 and LANES % hd == 0
    hpg, ng = LANES // hd, AW // LANES
    assert hpg <= 8
    tm_big = min(512, S // 2)
    tm_ffn = min(256, S // 2)
    tq = min(256, S // 2)
    px, py, pc = _place()
    me = 4 * px + 2 * py + pc

    bits = lambda a: lax.bitcast_convert_type(a, BF16)
    taps = jnp.concatenate([_rows(bits(conv_ffn_w[0])[None], D)[0], _rows(bits(conv_mix_w[0])[None], D)[0]], axis=0)
    n_ffn_rows = -(-conv_ffn_w[0].size * 2 // D)
    early = [w_in[0].T.astype(BF16), w_o[0].astype(BF16), w_out_conv[0].T.astype(BF16), w_out_attn[0].T.astype(BF16), taps]
    late = [w_up[0].T.astype(BF16), w_down[0].astype(BF16)]
    g_in, g_o, g_oc, g_oa, g_taps = _all_gather(early, "weights_all_gather")
    W_in_rows = g_in.reshape(NIN, D)
    W_in_t = jnp.concatenate([W_in_rows[:nqkv], W_in_rows[nqkv + H:], W_in_rows[nqkv:nqkv + H],
                              jnp.zeros((LANES - H, D), BF16)], axis=0)
    W_o = g_o.reshape(D, D)
    W_oc_t = g_oc.reshape(D, CW)
    W_oa_t = g_oa.reshape(D, AW)
    tap_bits = g_taps.reshape(N_DEV, -1)
    n_ffn, n_mix = conv_ffn_w[0].size * 2, conv_mix_w[0].size * 2
    wf_full = _from_col_blocks(lax.bitcast_convert_type(
        tap_bits[:, :n_ffn].reshape((N_DEV,) + conv_ffn_w.shape[1:] + (2,)), F32))
    wc_full = _from_col_blocks(lax.bitcast_convert_type(
        tap_bits[:, n_ffn_rows * D:n_ffn_rows * D + n_mix].reshape((N_DEV,) + conv_mix_w.shape[1:] + (2,)), F32))
    wf8 = jnp.pad(wf_full, ((0, 5), (0, 0)))
    wc8 = jnp.pad(wc_full, ((0, 5), (0, 0)))
    bf128 = jnp.pad(b_f, ((0, 0), (0, LANES - H)))

    x2d = x.reshape(T, D)
    tgt = loss_target.reshape(T, D)
    pm, fl, h1 = _inproj_fwd(x2d, norm_mix_g, W_in_t, NM, tm_big)
    fcum = _forget_cumsum(fl, bf128, S)
    frow = jnp.pad(fcum[:, :H].reshape(Bl, S, ng, hpg).transpose(0, 2, 3, 1), ((0, 0), (0, 0), (0, 8 - hpg), (0, 0)))
    q_off = 3 * CW // LANES
    o, lse, g_up, g_dn = _attn_fwd(pm, fcum, frow, S, q_off, AW, hd, tq, late)
    W_up_t = g_up.reshape(2 * FH, D)
    W_dn = g_dn.reshape(FH, D)
    x2, u, m, ycin, yc, ya = _merge_fwd(pm, o, x2d, wc8, b_gate, W_oc_t, W_oa_t, W_o, S, tm_big)
    upp, h2, dx3, loss8, dgf8 = _ffn_fwd(x2, norm_ffn_g, W_up_t, wf8, W_dn, norm_f_g.reshape(1, D), tgt, S, tm_ffn)

    dx2, dupp, act, dwf8, dg2_8 = _ffn_bwd(dx3, x2, norm_ffn_g, upp, wf8, W_dn, W_up_t, S, tm_ffn)
    dW_dn = _wgrad(act, dx3, "wgrad_down")
    dW_up_t = _wgrad(dupp, h2, "wgrad_up")
    ids = jnp.stack([pc, 2 * px + py]).astype(jnp.int32)
    big = [dW_up_t.reshape(4, 2, -1, D), dW_dn.reshape(4, 2, -1, D)]
    dconv, dgl, do, dyc, dya, dwc8, dbg8, *sib_big = _merge_bwd(
        dx2, pm, u, yc, ya, wc8, b_gate, W_o, W_oc_t, W_oa_t, S, tm_big, big)
    big_sums = [_pair_sum(b, r, ids, "grads_pair_sum_%d" % a) for a, (b, r) in enumerate(zip(big, sib_big))]
    dW_o = _wgrad(m, dx2, "wgrad_o")
    dW_oc_t = _wgrad(dyc, ycin, "wgrad_out_conv")
    dW_oa_t = _wgrad(dya, o, "wgrad_out_attn")
    dq, dk, dv, dfr, *chips_big = _attn_bwd(pm, do, fcum, frow, lse, S, q_off, AW, hd, tq, [s[1] for s in big_sums])
    dfc = jnp.pad(dfr[:, :, :hpg, :].transpose(0, 3, 1, 2).reshape(T, H), ((0, 0), (0, LANES - H)))
    dfl, dbf8 = _forget_bwd(dfc, fl, bf128, S)
    dparts = [dconv, dq, dk, dv, dgl, dfl]
    offs = [0, 3 * CW, 3 * CW + AW, 3 * CW + 2 * AW, nqkv, NM]
    grad_x, dg1_8 = _inproj_bwd(dparts, offs, W_in_t, x2d, norm_mix_g, dx2, tm_big)
    dW_in_t = _wgrad(dparts[0], h1, "wgrad_in_0", total_rows=NM + LANES)
    for k in range(1, len(dparts)):
        dW_in_t = _wgrad(dparts[k], h1, "wgrad_in_%d" % k, into=dW_in_t, row_off=offs[k])

    starts, win, index = _in_windows(n_loc, nqkv, H, NM)
    small = jnp.concatenate([dW_o.reshape(N_DEV, -1, D), _rows(dW_oc_t.reshape(N_DEV, -1), D), _rows(dW_oa_t.reshape(N_DEV, -1), D),
                             _rows(_col_blocks(dwf8[:3]), D), _rows(_col_blocks(dwc8[:3]), D)], axis=1)
    small = jnp.pad(small, ((0, 0), (0, -small.shape[1] % 8), (0, 0)))
    small = small.reshape(4, 2, -1, D)
    my_starts = [jnp.where(pc == 0, starts[2 * j], starts[2 * j + 1]) for j in range(4)]
    win_ids = jnp.stack([pc, 2 * px + py] + my_starts).astype(jnp.int32)
    sib_win, sib_small = _sibling_exchange(dW_in_t, starts, win, [small], "grads_sibling_exchange")
    sums = [_pair_sum(dW_in_t, sib_win, win_ids, "grads_pair_sum_in", win_rows=win),
            _pair_sum(small, sib_small, ids, "grads_pair_sum_small")]
    from_chips = _chip_exchange([s[1] for s in sums], "grads_chip_exchange")
    red_win, red_small, red_up, red_dn = [
        _final_sum(s[0], r, "grads_final_sum_%d" % a)
        for a, (s, r) in enumerate(zip(sums + big_sums, list(from_chips) + list(chips_big)))]

    f_rows = dW_in_t[NM:NM + 8]
    smalls = [dg1_8[0:1], dg2_8[0:1], dgf8[0:1], dbg8[0:1, :D], dbg8[0:1, D:], jnp.pad(dbf8[0:1], ((0, 0), (0, D - LANES))),
              jnp.zeros((2, D), F32), f_rows]
    spack = jnp.concatenate(smalls, axis=0)
    ssum = _sum8(_all_gather([spack], "small_all_gather")[0], "small_sum")
    g_g1, g_g2, g_gf = ssum[0:1], ssum[1:2], ssum[2]
    g_bg = jnp.concatenate([ssum[3:4], ssum[4:5]], axis=1)
    g_bf = ssum[5:6, :H]

    ext = jnp.concatenate([red_win, ssum[8:8 + H]], axis=0)
    my_index = lax.dynamic_index_in_dim(jnp.asarray(index), me, axis=0, keepdims=False)
    g_w_in = jnp.take(ext, my_index, axis=0).T
    g_w_up = red_up.T
    r_o = dW_o.shape[0] // N_DEV
    r_oc = _rows(dW_oc_t.reshape(N_DEV, -1), D).shape[1]
    r_wf = _rows(_col_blocks(dwf8[:3]), D).shape[1]
    r_wc = _rows(_col_blocks(dwc8[:3]), D).shape[1]
    o0, o1, o2, o3 = r_o, r_o + r_oc, r_o + 2 * r_oc, r_o + 2 * r_oc + r_wf
    g_w_o = red_small[:o0]
    g_w_oc = _unrows(red_small[o0:o1], (D // N_DEV, CW)).T
    g_w_oa = _unrows(red_small[o1:o2], (D // N_DEV, AW)).T
    g_wf = _unrows(red_small[o2:o3], conv_ffn_w.shape[1:])
    g_wc = _unrows(red_small[o3:o3 + r_wc], conv_mix_w.shape[1:])

    loss = lax.psum(loss8[0, 0], ("x", "y", "c"))

    names = ["norm_mix_g", "w_in", "b_f", "b_gate", "conv_mix_w", "w_out_conv", "w_out_attn", "w_o", "norm_ffn_g", "w_up",
             "conv_ffn_w", "w_down", "norm_f_g"]
    weights = [norm_mix_g, w_in, b_f, b_gate, conv_mix_w, w_out_conv, w_out_attn, w_o, norm_ffn_g, w_up, conv_ffn_w, w_down, norm_f_g]
    grads = [g_g1, g_w_in, g_bf, g_bg, g_wc, g_w_oc, g_w_oa, g_w_o, g_g2, g_w_up, g_wf, red_dn, g_gf]
    ms = [m_norm_mix_g, m_w_in, m_b_f, m_b_gate, m_conv_mix_w, m_w_out_conv, m_w_out_attn, m_w_o, m_norm_ffn_g, m_w_up,
          m_conv_ffn_w, m_w_down, m_norm_f_g]
    vs = [v_norm_mix_g, v_w_in, v_b_f, v_b_gate, v_conv_mix_w, v_w_out_conv, v_w_out_attn, v_w_o, v_norm_ffn_g, v_w_up,
          v_conv_ffn_w, v_w_down, v_norm_f_g]
    grads = [g.reshape(w.shape) for g, w in zip(grads, weights)]
    steps = [_adamw(w, g, mm, vv, "adamw_" + nm) for nm, w, g, mm, vv in zip(names, weights, grads, ms, vs)]
    deltas, new_ms, new_vs = zip(*steps)
    return (loss, grad_x.reshape(Bl, S, D), *grads, *deltas, *new_ms, *new_vs)
```

```python
import numpy as np

import jax
import jax.numpy as jnp
from jax import lax
from jax.experimental import pallas as pl
from jax.experimental.pallas import tpu as pltpu

F32, BF16 = jnp.float32, jnp.bfloat16
EPS = 1e-6
ADAM_LR, ADAM_B1, ADAM_B2, ADAM_EPS, ADAM_WD, ADAM_STEP = 0.001, 0.9, 0.999, 1e-08, 0.01, 10
N_DEV = 8
LANES = 128
V7X_VMEM_LIMIT = 56 * 1024 * 1024
MESH = pl.DeviceIdType.MESH
SDS = jax.ShapeDtypeStruct
ANY = pl.BlockSpec(memory_space=pl.ANY)


def _tile(n, target, mult=LANES):
    best = None
    for t in range(mult, min(n, target) + 1, mult):
        if n % t == 0:
            best = t
    return best if best is not None else n


def _resident(shape):
    return pl.BlockSpec(shape, lambda *_: (0,) * len(shape), pipeline_mode=pl.Buffered(1))


def _cparams(n_axes=1):
    return pltpu.CompilerParams(dimension_semantics=("arbitrary",) * n_axes, vmem_limit_bytes=V7X_VMEM_LIMIT)


def _dot(a, b):
    return jnp.dot(a, b, preferred_element_type=F32)


def _dot_tn(a, b):
    return lax.dot_general(a, b, (((0,), (0,)), ((), ())), preferred_element_type=F32)


def _dot_nt(a, b):
    return lax.dot_general(a, b, (((1,), (1,)), ((), ())), preferred_element_type=F32)


def _sigmoid(x):
    return 0.5 * jnp.tanh(0.5 * x) + 0.5


def _rms(x):
    return lax.rsqrt(jnp.mean(x * x, axis=-1, keepdims=True) + EPS)


def _taps_back(z, r6, r7):
    row = lax.broadcasted_iota(jnp.int32, (8, 1), 0)
    z1, z2 = pltpu.roll(z, 1, 0), pltpu.roll(z, 2, 0)
    z1 = jnp.concatenate([jnp.where(row == 0, r7, z1[0:8]), z1[8:]], axis=0)
    z2 = jnp.concatenate([jnp.where(row == 0, r6, jnp.where(row == 1, r7, z2[0:8])), z2[8:]], axis=0)
    return z1, z2


def _taps_ahead(d, h0, h1):
    tm = d.shape[0]
    row = lax.broadcasted_iota(jnp.int32, (8, 1), 0)
    d1, d2 = pltpu.roll(d, tm - 1, 0), pltpu.roll(d, tm - 2, 0)
    d1 = jnp.concatenate([d1[:tm - 8], jnp.where(row == 7, h0, d1[tm - 8:])], axis=0)
    d2 = jnp.concatenate([d2[:tm - 8], jnp.where(row == 6, h0, jnp.where(row == 7, h1, d2[tm - 8:]))], axis=0)
    return d1, d2


def _tri_dot(tri, x):
    hi = x.astype(BF16)
    r = x - hi.astype(F32)
    mid = r.astype(BF16)
    lo = (r - mid.astype(F32)).astype(BF16)
    return (_dot(tri, lo) + _dot(tri, mid)) + _dot(tri, hi)


def _lane_pick(block, lane):
    lanes = lax.broadcasted_iota(jnp.int32, (1, block.shape[1]), 1)
    return jnp.sum(jnp.where(lanes == lane, block, 0.0), axis=1, keepdims=True)


def _place():
    return lax.axis_index("x"), lax.axis_index("y"), lax.axis_index("c")


def _all_gather(xs, name):
    n = len(xs)

    def body(*refs):
        start, forward, finish = _gather_phases(refs[:n], refs[n:2 * n], *refs[2 * n:])
        start()
        forward()
        finish()

    return pl.pallas_call(
        body, name=name,
        out_shape=_gather_shapes(xs), in_specs=[ANY] * n, out_specs=[ANY] * n, scratch_shapes=_gather_sems(n),
    )(*xs)


def _gather_shapes(xs):
    return [SDS((N_DEV,) + x.shape, x.dtype) for x in xs]


def _gather_sems(n):
    return [pltpu.SemaphoreType.DMA((7 * n,)), pltpu.SemaphoreType.DMA((7 * n,)), pltpu.SemaphoreType.DMA((n,))]


def _gather_phases(x_refs, out_refs, send_sems, recv_sems, local_sems):
    n = len(x_refs)

    def parts():
        px, py, pc = _place()
        me, sibling = (px, py, pc), (px, py, 1 - pc)
        chips = [(1 - px, py), (px, 1 - py), (1 - px, 1 - py)]

        def slot(a, qx, qy, qc):
            return out_refs[a].at[4 * qx + 2 * qy + qc]

        def copy(a, k, block, to, src=None):
            return pltpu.make_async_remote_copy(
                src_ref=slot(a, *block) if src is None else src, dst_ref=slot(a, *block),
                send_sem=send_sems.at[7 * a + k], recv_sem=recv_sems.at[7 * a + k], device_id=to, device_id_type=MESH)

        def mine():
            return [pltpu.make_async_copy(x_refs[a], slot(a, *me), local_sems.at[a]) for a in range(n)]

        def first():
            out = []
            for a in range(n):
                out.append(copy(a, 0, me, sibling, src=x_refs[a]))
                out += [copy(a, 1 + j, me, (*chip, pc), src=x_refs[a]) for j, chip in enumerate(chips)]
            return out

        def landed():
            return [copy(a, 1 + j, (*chip, pc), me) for j, chip in enumerate(chips) for a in range(n)]

        def passed():
            return [copy(a, 4 + j, (*chip, pc), sibling) for j, chip in enumerate(chips) for a in range(n)]

        def late():
            out = [copy(a, 0, sibling, me) for a in range(n)]
            return out + [copy(a, 4 + j, (*chip, 1 - pc), me) for j, chip in enumerate(chips) for a in range(n)]

        return mine, first, landed, passed, late

    def start():
        mine, first, _, _, _ = parts()
        for cp in mine() + first():
            cp.start()

    def forward():
        _, _, landed, passed, _ = parts()
        for got, cp in zip(landed(), passed()):
            got.wait_recv()
            cp.start()

    def finish():
        mine, first, _, passed, late = parts()
        for cp in late():
            cp.wait_recv()
        for cp in first() + passed():
            cp.wait_send()
        for cp in mine():
            cp.wait()

    return start, forward, finish


def _sibling_exchange(win_buf, win_starts, win_rows, blocked, name):
    nb = len(blocked)

    def body(*refs):
        win_ref, blk_refs = refs[0], refs[1:1 + nb]
        rwin_ref, rblk_refs = refs[1 + nb], refs[2 + nb:2 + 2 * nb]
        send_sems, recv_sems, wsend_sems, wrecv_sems = refs[2 + 2 * nb:]
        px, py, pc = _place()
        copies = _sibling_copies(blk_refs, rblk_refs, send_sems, recv_sems)
        for j in range(4):
            theirs = jnp.where(pc == 0, win_starts[2 * j + 1], win_starts[2 * j])
            copies.append(pltpu.make_async_remote_copy(
                src_ref=win_ref.at[pl.ds(pl.multiple_of(theirs, 8), win_rows)], dst_ref=rwin_ref.at[j],
                send_sem=wsend_sems.at[j], recv_sem=wrecv_sems.at[j], device_id=(px, py, 1 - pc), device_id_type=MESH))
        for cp in copies:
            cp.start()
        for cp in copies:
            cp.wait()

    C = win_buf.shape[1]
    return pl.pallas_call(
        body, name=name,
        out_shape=[SDS((4, win_rows, C), F32)] + _sibling_shapes(blocked),
        in_specs=[ANY] * (1 + nb), out_specs=[ANY] * (1 + nb),
        scratch_shapes=_exchange_sems(nb) + _exchange_sems(4),
    )(win_buf, *blocked)


def _sibling_shapes(blocked):
    return [SDS((4,) + b.shape[2:], F32) for b in blocked]


def _exchange_sems(n):
    return [pltpu.SemaphoreType.DMA((n,)), pltpu.SemaphoreType.DMA((n,))]


def _sibling_copies(blk_refs, out_refs, send_sems, recv_sems):
    px, py, pc = _place()
    return [pltpu.make_async_remote_copy(
        src_ref=b.at[:, 1 - pc], dst_ref=o, send_sem=send_sems.at[a], recv_sem=recv_sems.at[a],
        device_id=(px, py, 1 - pc), device_id_type=MESH) for a, (b, o) in enumerate(zip(blk_refs, out_refs))]


def _chip_shapes(ps):
    return [SDS((3,) + p.shape[1:], p.dtype) for p in ps]


def _chip_copies(p_refs, out_refs, send_sems, recv_sems):
    px, py, pc = _place()
    n = len(p_refs)
    chips = [(1 - px, py), (px, 1 - py), (1 - px, 1 - py)]
    return [pltpu.make_async_remote_copy(
        src_ref=p_refs[a].at[2 * qx + qy], dst_ref=out_refs[a].at[k],
        send_sem=send_sems.at[3 * a + k], recv_sem=recv_sems.at[3 * a + k],
        device_id=(qx, qy, pc), device_id_type=MESH) for k, (qx, qy) in enumerate(chips) for a in range(n)]


def _pair_sum(own, recv, ids, name, win_rows=None):
    _, R, C = recv.shape
    tr = _tile(R, 512, 8)

    def body(ids_ref, g_ref, r_ref, own_ref, pb_ref):
        s = g_ref[...] + r_ref[...]
        pb_ref[...] = s.astype(BF16)

        @pl.when(pl.program_id(1) == ids_ref[1])
        def _():
            own_ref[...] = s

    if win_rows is None:
        own_spec = pl.BlockSpec((None, None, tr, C), lambda r, j, ids: (j, ids[0], r, 0))
    else:
        own_spec = pl.BlockSpec((pl.Element(tr), pl.Element(C)), lambda r, j, ids: (pl.multiple_of(ids[2 + j] + r * tr, 8), 0))
    return pl.pallas_call(
        body, name=name,
        grid_spec=pltpu.PrefetchScalarGridSpec(
            num_scalar_prefetch=1, grid=(R // tr, 4),
            in_specs=[own_spec, pl.BlockSpec((None, tr, C), lambda r, j, ids: (j, r, 0))],
            out_specs=[pl.BlockSpec((tr, C), lambda r, j, ids: (r, 0)),
                       pl.BlockSpec((None, tr, C), lambda r, j, ids: (j, r, 0))]),
        out_shape=[SDS((R, C), F32), SDS((4, R, C), BF16)],
        compiler_params=_cparams(2),
    )(ids, own, recv)


def _final_sum(own, recv, name):
    R, C = own.shape
    tr = _tile(R, 512, 8)

    def body(o_ref, r_ref, out_ref):
        out_ref[...] = ((o_ref[...] + r_ref[0].astype(F32)) + r_ref[1].astype(F32)) + r_ref[2].astype(F32)

    return pl.pallas_call(
        body, name=name, grid=(R // tr,),
        in_specs=[pl.BlockSpec((tr, C), lambda r: (r, 0)), pl.BlockSpec((3, tr, C), lambda r: (0, r, 0))],
        out_specs=pl.BlockSpec((tr, C), lambda r: (r, 0)),
        out_shape=SDS((R, C), F32),
        compiler_params=_cparams(1),
    )(own, recv)


def _sum8(a, name):
    def body(a_ref, out_ref):
        s = a_ref[0]
        for d in range(1, N_DEV):
            s = s + a_ref[d]
        out_ref[...] = s

    return pl.pallas_call(body, name=name, out_shape=SDS(a.shape[1:], F32))(a)


def _inproj_fwd(x, g1, w_t, NM, tm):
    T, D = x.shape
    NF = w_t.shape[0] - NM
    ch = _tile(NM, 1024)

    def body(x_ref, g_ref, w_ref, pm_ref, fl_ref, h_ref):
        xv = x_ref[...]
        h = (xv * _rms(xv) * g_ref[...]).astype(BF16)
        h_ref[...] = h
        for c in range(0, NM, ch):
            pm_ref[:, c:c + ch] = _dot_nt(h, w_ref[c:c + ch, :]).astype(BF16)
        fl_ref[...] = _dot_nt(h, w_ref[NM:NM + NF, :])

    return pl.pallas_call(
        body, name="inproj_fwd", grid=(T // tm,),
        in_specs=[pl.BlockSpec((tm, D), lambda i: (i, 0)), _resident((1, D)), _resident(w_t.shape)],
        out_specs=[pl.BlockSpec((tm, NM), lambda i: (i, 0)), pl.BlockSpec((tm, NF), lambda i: (i, 0)),
                   pl.BlockSpec((tm, D), lambda i: (i, 0))],
        out_shape=[SDS((T, NM), BF16), SDS((T, NF), F32), SDS((T, D), BF16)],
        compiler_params=_cparams(1),
    )(x, g1, w_t)


def _log_sigmoid(x):
    return jnp.minimum(x, 0.0) - jnp.log(1.0 + jnp.exp(-jnp.abs(x)))


def _forget_cumsum(fl, bf, S):
    T, NF = fl.shape
    ch = _tile(S, 256, 8)

    def body(fl_ref, bf_ref, f_ref):
        row = lax.broadcasted_iota(jnp.int32, (ch, ch), 0)
        col = lax.broadcasted_iota(jnp.int32, (ch, ch), 1)
        tri = (col <= row).astype(BF16)
        carry = jnp.zeros((1, NF), F32)
        for c in range(0, S, ch):
            lf = _log_sigmoid(fl_ref[c:c + ch, :] + bf_ref[...])
            f_ref[c:c + ch, :] = _tri_dot(tri, lf) + carry
            carry = carry + jnp.sum(lf, axis=0, keepdims=True)

    return pl.pallas_call(
        body, name="forget_cumsum", grid=(T // S,),
        in_specs=[pl.BlockSpec((S, NF), lambda b: (b, 0)), _resident((1, NF))],
        out_specs=pl.BlockSpec((S, NF), lambda b: (b, 0)),
        out_shape=SDS((T, NF), F32),
        compiler_params=_cparams(1),
    )(fl, bf)


def _head_mask(e, hd):
    lanes = lax.broadcasted_iota(jnp.int32, (1, LANES), 1)
    return (lanes >= e * hd) & (lanes < (e + 1) * hd)


def _attn_specs(S, q_off, AW):
    ng = AW // LANES
    return [pl.BlockSpec((S, LANES), lambda b, g, o=q_off + w * ng: (b, o + g)) for w in range(3)]


def _attn_fwd(pm, fcum, frow, S, q_off, AW, hd, tq, gather):
    T = pm.shape[0]
    scale = float(hd) ** -0.5
    nq, hpg = S // tq, LANES // hd
    ng, nx = AW // LANES, len(gather)
    steps = (T // S) * ng

    def body(q_ref, k_ref, v_ref, fc_ref, fr_ref, *rest):
        x_refs, (o_ref, lse_ref), out_refs, sems = rest[:nx], rest[nx:nx + 2], rest[nx + 2:2 * nx + 2], rest[2 * nx + 2:]
        g = pl.program_id(1)
        step = pl.program_id(0) * ng + g
        start, forward, finish = _gather_phases(x_refs, out_refs, *sems)
        pl.when(step == 0)(start)

        @pl.when(g == 0)
        def _():
            lse_ref[...] = jnp.zeros_like(lse_ref)

        lanes = lax.broadcasted_iota(jnp.int32, (1, LANES), 1)
        for i in range(nq):
            rs, kend = slice(i * tq, (i + 1) * tq), (i + 1) * tq
            row = i * tq + lax.broadcasted_iota(jnp.int32, (tq, kend), 0)
            col = lax.broadcasted_iota(jnp.int32, (tq, kend), 1)
            o_tile = jnp.zeros((tq, LANES), F32)
            lse_tile = lse_ref[rs, :]
            for e in range(hpg):
                mask = _head_mask(e, hd)
                qs = jnp.where(mask, q_ref[rs, :], 0) * scale
                s = _dot_nt(qs, k_ref[0:kend, :]) + _lane_pick(fc_ref[rs, :], g * hpg + e) - fr_ref[e:e + 1, 0:kend]
                s = jnp.where(col <= row, s, -1e30)
                m = jnp.max(s, axis=1, keepdims=True)
                p = jnp.exp(s - m)
                l = jnp.sum(p, axis=1, keepdims=True)
                o_tile = jnp.where(mask, _dot(p.astype(BF16), v_ref[0:kend, :]) / l, o_tile)
                lse_tile = jnp.where(lanes == g * hpg + e, m + jnp.log(l), lse_tile)
            o_ref[rs, :] = o_tile.astype(BF16)
            lse_ref[rs, :] = lse_tile
        pl.when(step == (steps * 5) // 8)(forward)
        pl.when(step == steps - 1)(finish)

    full = pl.BlockSpec((S, LANES), lambda b, g: (b, 0))
    return pl.pallas_call(
        body, name="attn_fwd", grid=(T // S, ng),
        in_specs=_attn_specs(S, q_off, AW) + [full, pl.BlockSpec((None, None, 8, S), lambda b, g: (b, g, 0, 0))] + [ANY] * nx,
        out_specs=[pl.BlockSpec((S, LANES), lambda b, g: (b, g)), full] + [ANY] * nx,
        out_shape=[SDS((T, AW), BF16), SDS((T, LANES), F32)] + _gather_shapes(gather),
        scratch_shapes=_gather_sems(nx),
        compiler_params=_cparams(2),
    )(pm, pm, pm, fcum, frow, *gather)


def _merge_fwd(pm, o, x, wc, bg, woc_t, woa_t, wo, S, tm):
    T, D = x.shape
    CW, AW = woc_t.shape[1], woa_t.shape[1]
    tps = S // tm

    def body(cb_ref, cc_ref, cin_ref, gc_ref, ga_ref, o_ref, x_ref, wc_ref, bg_ref, woct_ref, woat_ref, wo_ref,
             x2_ref, u_ref, m_ref, ycin_ref, yc_ref, ya_ref, tail_ref):
        @pl.when(pl.program_id(0) % tps == 0)
        def _():
            tail_ref[...] = jnp.zeros_like(tail_ref)

        z = cc_ref[...].astype(F32) * cin_ref[...].astype(F32)
        z1, z2 = _taps_back(z, tail_ref[6:7, :], tail_ref[7:8, :])
        tail_ref[...] = z[tm - 8:tm, :]
        u = (z * wc_ref[2:3, :] + z2 * wc_ref[0:1, :]) + z1 * wc_ref[1:2, :]
        u_ref[...] = u.astype(BF16)
        ycin = (cb_ref[...].astype(F32) * u).astype(BF16)
        ycin_ref[...] = ycin
        yc = _dot_nt(ycin, woct_ref[...])
        ya = _dot_nt(o_ref[...], woat_ref[...])
        yc_ref[...] = yc.astype(BF16)
        ya_ref[...] = ya.astype(BF16)
        gc = _sigmoid(gc_ref[...].astype(F32) + bg_ref[:, 0:D])
        ga = _sigmoid(ga_ref[...].astype(F32) + bg_ref[:, D:2 * D])
        m = (gc * yc + ga * ya).astype(BF16)
        m_ref[...] = m
        x2_ref[...] = x_ref[...] + _dot(m, wo_ref[...])

    g_off = (3 * CW + 3 * AW) // D
    tok = lambda w, j=0: pl.BlockSpec((tm, w), lambda i: (i, j))
    return pl.pallas_call(
        body, name="merge_fwd", grid=(T // tm,),
        in_specs=[tok(CW, 0), tok(CW, 1), tok(CW, 2), tok(D, g_off), tok(D, g_off + 1), tok(AW), tok(D),
                  _resident((8, CW)), _resident((1, 2 * D)), _resident((D, CW)), _resident((D, AW)), _resident((D, D))],
        out_specs=[tok(D), tok(CW), tok(D), tok(CW), tok(D), tok(D)],
        out_shape=[SDS((T, D), F32), SDS((T, CW), BF16), SDS((T, D), BF16), SDS((T, CW), BF16),
                   SDS((T, D), BF16), SDS((T, D), BF16)],
        scratch_shapes=[pltpu.VMEM((8, CW), F32)],
        compiler_params=_cparams(1),
    )(pm, pm, pm, pm, pm, o, x, wc, bg, woc_t, woa_t, wo)


def _ffn_fwd(x2, g2, wup_t, wf, wdn, gf, tgt, S, tm):
    T, D = x2.shape
    FH = wdn.shape[0]
    ch = _tile(FH, 1408)
    tps = S // tm

    def body(x2_ref, g2_ref, wupt_ref, wf_ref, wdn_ref, gf_ref, tgt_ref,
             upp_ref, up_ref, h2_ref, dx3_ref, loss_ref, dgf_ref, tail_ref):
        i = pl.program_id(0)

        @pl.when(i % tps == 0)
        def _():
            tail_ref[...] = jnp.zeros_like(tail_ref)

        @pl.when(i == 0)
        def _():
            loss_ref[...] = jnp.zeros_like(loss_ref)
            dgf_ref[...] = jnp.zeros_like(dgf_ref)

        x2v = x2_ref[...]
        h2 = (x2v * _rms(x2v) * g2_ref[...]).astype(BF16)
        h2_ref[...] = h2
        x3 = x2v
        for c in range(0, FH, ch):
            gated = []
            for cols in (slice(c, c + ch), slice(FH + c, FH + c + ch)):
                upp = _dot_nt(h2, wupt_ref[cols, :])
                upp_ref[:, cols] = upp.astype(BF16)
                p1, p2 = _taps_back(upp, tail_ref[6:7, cols], tail_ref[7:8, cols])
                tail_ref[:, cols] = upp[tm - 8:tm, :]
                up = (upp * wf_ref[2:3, cols] + p2 * wf_ref[0:1, cols]) + p1 * wf_ref[1:2, cols]
                up_ref[:, cols] = up.astype(BF16)
                gated.append(up)
            a, b = gated
            act = (a * _sigmoid(a) * b).astype(BF16)
            x3 = x3 + _dot(act, wdn_ref[c:c + ch, :])
        r3 = _rms(x3)
        xn3 = x3 * r3
        e = xn3 * gf_ref[...] - tgt_ref[...]
        loss_ref[...] += 0.5 * jnp.sum(jnp.mean(e * e, axis=-1, keepdims=True), axis=0, keepdims=True)
        dy = e / D
        dgf_ref[0:1, :] += jnp.sum(dy * xn3, axis=0, keepdims=True)
        dxn = dy * gf_ref[...]
        dx3_ref[...] = r3 * (dxn - xn3 * jnp.mean(dxn * xn3, axis=-1, keepdims=True))

    tok = lambda w: pl.BlockSpec((tm, w), lambda i: (i, 0))
    return pl.pallas_call(
        body, name="ffn_fwd", grid=(T // tm,),
        in_specs=[tok(D), _resident((1, D)), _resident((2 * FH, D)), _resident((8, 2 * FH)), _resident((FH, D)),
                  _resident((1, D)), tok(D)],
        out_specs=[tok(2 * FH), tok(2 * FH), tok(D), tok(D), pl.BlockSpec((8, LANES), lambda i: (0, 0)),
                   pl.BlockSpec((8, D), lambda i: (0, 0))],
        out_shape=[SDS((T, 2 * FH), BF16), SDS((T, 2 * FH), BF16), SDS((T, D), BF16), SDS((T, D), F32),
                   SDS((8, LANES), F32), SDS((8, D), F32)],
        scratch_shapes=[pltpu.VMEM((8, 2 * FH), F32)],
        compiler_params=_cparams(1),
    )(x2, g2, wup_t, wf, wdn, gf, tgt)


def _ffn_bwd(dx3, x2, g2, upp, up, wf, wdn, wup_t, S, tm):
    T, D = x2.shape
    FH = wdn.shape[0]
    ch = _tile(FH, 1408)
    n, tps = T // tm, S // tm

    def body(dx3_ref, x2_ref, g2_ref, upp_ref, up_ref, wf_ref, wdn_ref, wupt_ref,
             dx2_ref, dupp_ref, act_ref, dwf_ref, dg2_ref, head_ref):
        i = pl.program_id(0)
        t = n - 1 - i

        @pl.when(i == 0)
        def _():
            dwf_ref[...] = jnp.zeros_like(dwf_ref)
            dg2_ref[...] = jnp.zeros_like(dg2_ref)

        @pl.when(t % tps == tps - 1)
        def _():
            head_ref[...] = jnp.zeros_like(head_ref)

        dx3v = dx3_ref[...]
        dx3b = dx3v.astype(BF16)
        x2v = x2_ref[...]
        r2 = _rms(x2v)
        xn2 = x2v * r2
        dh2 = jnp.zeros((tm, D), F32)
        for c in range(0, FH, ch):
            ca, cb = slice(c, c + ch), slice(FH + c, FH + c + ch)
            a, b = up_ref[:, ca].astype(F32), up_ref[:, cb].astype(F32)
            sig = _sigmoid(a)
            sl = a * sig
            act_ref[:, ca] = (sl * b).astype(BF16)
            dact = _dot_nt(dx3b, wdn_ref[ca, :])
            grads = (dact * b * (sig * (1.0 + a * (1.0 - sig))), dact * sl)
            for cols, d in zip((ca, cb), grads):
                u0 = upp_ref[:, cols].astype(F32)
                d1, d2 = _taps_ahead(d, head_ref[0:1, cols], head_ref[1:2, cols])
                head_ref[:, cols] = d[0:8, :]
                dwf_ref[2:3, cols] += jnp.sum(u0 * d, axis=0, keepdims=True)
                dwf_ref[1:2, cols] += jnp.sum(u0 * d1, axis=0, keepdims=True)
                dwf_ref[0:1, cols] += jnp.sum(u0 * d2, axis=0, keepdims=True)
                dpre = ((d * wf_ref[2:3, cols] + d1 * wf_ref[1:2, cols]) + d2 * wf_ref[0:1, cols]).astype(BF16)
                dupp_ref[:, cols] = dpre
                dh2 = dh2 + _dot(dpre, wupt_ref[cols, :])
        dg2_ref[0:1, :] += jnp.sum(dh2 * xn2, axis=0, keepdims=True)
        dxn = dh2 * g2_ref[...]
        dx2_ref[...] = dx3v + r2 * (dxn - xn2 * jnp.mean(dxn * xn2, axis=-1, keepdims=True))

    tok = lambda w: pl.BlockSpec((tm, w), lambda i: (n - 1 - i, 0))
    acc = lambda w: pl.BlockSpec((8, w), lambda i: (0, 0))
    return pl.pallas_call(
        body, name="ffn_bwd", grid=(n,),
        in_specs=[tok(D), tok(D), _resident((1, D)), tok(2 * FH), tok(2 * FH), _resident((8, 2 * FH)),
                  _resident((FH, D)), _resident((2 * FH, D))],
        out_specs=[tok(D), tok(2 * FH), tok(FH), acc(2 * FH), acc(D)],
        out_shape=[SDS((T, D), F32), SDS((T, 2 * FH), BF16), SDS((T, FH), BF16), SDS((8, 2 * FH), F32), SDS((8, D), F32)],
        scratch_shapes=[pltpu.VMEM((8, 2 * FH), F32)],
        compiler_params=_cparams(1),
    )(dx3, x2, g2, upp, up, wf, wdn, wup_t)


def _merge_bwd(dx2, pm, u, yc, ya, wc, bg, wo, woc_t, woa_t, S, tm, exchange):
    T, D = dx2.shape
    CW, AW = woc_t.shape[1], woa_t.shape[1]
    n, tps = T // tm, S // tm
    nx = len(exchange)

    def body(dx2_ref, cb_ref, cc_ref, cin_ref, gc_ref, ga_ref, u_ref, yc_ref, ya_ref, wc_ref, bg_ref,
             wo_ref, woct_ref, woat_ref, *rest):
        x_refs = rest[:nx]
        dconv_ref, dgl_ref, do_ref, dyc_ref, dya_ref, dwc_ref, dbg_ref = rest[nx:nx + 7]
        got_refs, head_ref, sems = rest[nx + 7:2 * nx + 7], rest[2 * nx + 7], rest[2 * nx + 8:]
        i = pl.program_id(0)
        t = n - 1 - i

        @pl.when(i == 0)
        def _():
            for cp in _sibling_copies(x_refs, got_refs, *sems):
                cp.start()

        @pl.when(i == 0)
        def _():
            dwc_ref[...] = jnp.zeros_like(dwc_ref)
            dbg_ref[...] = jnp.zeros_like(dbg_ref)

        @pl.when(t % tps == tps - 1)
        def _():
            head_ref[...] = jnp.zeros_like(head_ref)

        dm = _dot_nt(dx2_ref[...].astype(BF16), wo_ref[...])
        outs = []
        for g_ref, y_ref, cols in ((gc_ref, yc_ref, slice(0, D)), (ga_ref, ya_ref, slice(D, 2 * D))):
            g = _sigmoid(g_ref[...].astype(F32) + bg_ref[:, cols])
            dgl = dm * y_ref[...].astype(F32) * g * (1.0 - g)
            dgl_ref[:, cols] = dgl.astype(BF16)
            dbg_ref[0:1, cols] += jnp.sum(dgl, axis=0, keepdims=True)
            outs.append((dm * g).astype(BF16))
        dyc, dya = outs
        dyc_ref[...] = dyc
        dya_ref[...] = dya
        do_ref[...] = _dot(dya, woat_ref[...]).astype(BF16)
        dycin = _dot(dyc, woct_ref[...])
        cc, cin = cc_ref[...].astype(F32), cin_ref[...].astype(F32)
        z = cc * cin
        du = dycin * cb_ref[...].astype(F32)
        du1, du2 = _taps_ahead(du, head_ref[0:1, :], head_ref[1:2, :])
        head_ref[...] = du[0:8, :]
        dwc_ref[2:3, :] += jnp.sum(z * du, axis=0, keepdims=True)
        dwc_ref[1:2, :] += jnp.sum(z * du1, axis=0, keepdims=True)
        dwc_ref[0:1, :] += jnp.sum(z * du2, axis=0, keepdims=True)
        dz = (du * wc_ref[2:3, :] + du1 * wc_ref[1:2, :]) + du2 * wc_ref[0:1, :]
        dconv_ref[:, 0:CW] = (dycin * u_ref[...].astype(F32)).astype(BF16)
        dconv_ref[:, CW:2 * CW] = (dz * cin).astype(BF16)
        dconv_ref[:, 2 * CW:3 * CW] = (dz * cc).astype(BF16)

        @pl.when(i == n - 1)
        def _():
            for cp in _sibling_copies(x_refs, got_refs, *sems):
                cp.wait()

    g_off = (3 * CW + 3 * AW) // D
    tok = lambda w, j=0: pl.BlockSpec((tm, w), lambda i: (n - 1 - i, j))
    acc = lambda w: pl.BlockSpec((8, w), lambda i: (0, 0))
    return pl.pallas_call(
        body, name="merge_bwd", grid=(n,),
        in_specs=[tok(D), tok(CW, 0), tok(CW, 1), tok(CW, 2), tok(D, g_off), tok(D, g_off + 1), tok(CW), tok(D), tok(D),
                  _resident((8, CW)), _resident((1, 2 * D)), _resident((D, D)), _resident((D, CW)), _resident((D, AW))]
                 + [ANY] * nx,
        out_specs=[tok(3 * CW), tok(2 * D), tok(AW), tok(D), tok(D), acc(CW), acc(2 * D)] + [ANY] * nx,
        out_shape=[SDS((T, 3 * CW), BF16), SDS((T, 2 * D), BF16), SDS((T, AW), BF16), SDS((T, D), BF16), SDS((T, D), BF16),
                   SDS((8, CW), F32), SDS((8, 2 * D), F32)] + _sibling_shapes(exchange),
        scratch_shapes=[pltpu.VMEM((8, CW), F32)] + _exchange_sems(nx),
        compiler_params=_cparams(1),
    )(dx2, pm, pm, pm, pm, pm, u, yc, ya, wc, bg, wo, woc_t, woa_t, *exchange)


def _attn_bwd(pm, do, fcum, frow, lse, S, q_off, AW, hd, tq, exchange):
    T = pm.shape[0]
    scale = float(hd) ** -0.5
    nq, hpg, ng = S // tq, LANES // hd, AW // LANES
    nx = len(exchange)
    steps = (T // S) * ng

    def body(q_ref, k_ref, v_ref, do_ref, fc_ref, fr_ref, lse_ref, *rest):
        x_refs, (dq_ref, dk_ref, dv_ref, dfr_ref) = rest[:nx], rest[nx:nx + 4]
        got_refs, (dk_acc, dv_acc), sems = rest[nx + 4:2 * nx + 4], rest[2 * nx + 4:2 * nx + 6], rest[2 * nx + 6:]
        g = pl.program_id(1)
        step = pl.program_id(0) * ng + g

        @pl.when(step == 0)
        def _():
            for cp in _chip_copies(x_refs, got_refs, *sems):
                cp.start()

        dk_acc[...] = jnp.zeros_like(dk_acc)
        dv_acc[...] = jnp.zeros_like(dv_acc)
        dfr_ref[...] = jnp.zeros_like(dfr_ref)
        for i in range(nq):
            rs, kend = slice(i * tq, (i + 1) * tq), (i + 1) * tq
            row = i * tq + lax.broadcasted_iota(jnp.int32, (tq, kend), 0)
            col = lax.broadcasted_iota(jnp.int32, (tq, kend), 1)
            kk, vv = k_ref[0:kend, :], v_ref[0:kend, :]
            dq_tile = jnp.zeros((tq, LANES), F32)
            for e in range(hpg):
                mask = _head_mask(e, hd)
                qs = jnp.where(mask, q_ref[rs, :], 0) * scale
                doi = jnp.where(mask, do_ref[rs, :], 0)
                s = _dot_nt(qs, kk) + _lane_pick(fc_ref[rs, :], g * hpg + e) - fr_ref[e:e + 1, 0:kend]
                p = jnp.where(col <= row, jnp.exp(s - _lane_pick(lse_ref[rs, :], g * hpg + e)), 0.0)
                dp = _dot_nt(doi, vv)
                ds = p * (dp - jnp.sum(p * dp, axis=1, keepdims=True))
                pb, dsb = p.astype(BF16), ds.astype(BF16)
                dq_tile = jnp.where(mask, _dot(dsb, kk) * scale, dq_tile)
                dv_acc[0:kend, :] += _dot_tn(pb, doi)
                dk_acc[0:kend, :] += _dot_tn(dsb, qs)
                dfr_ref[e:e + 1, 0:kend] -= jnp.sum(ds, axis=0, keepdims=True)
            dq_ref[rs, :] = dq_tile.astype(BF16)
        dk_ref[...] = dk_acc[...].astype(BF16)
        dv_ref[...] = dv_acc[...].astype(BF16)

        @pl.when(step == steps - 1)
        def _():
            for cp in _chip_copies(x_refs, got_refs, *sems):
                cp.wait()

    full = pl.BlockSpec((S, LANES), lambda b, g: (b, 0))
    grp = pl.BlockSpec((S, LANES), lambda b, g: (b, g))
    rows = pl.BlockSpec((None, None, 8, S), lambda b, g: (b, g, 0, 0))
    return pl.pallas_call(
        body, name="attn_bwd", grid=(T // S, ng),
        in_specs=_attn_specs(S, q_off, AW) + [grp, full, rows, full] + [ANY] * nx,
        out_specs=[grp, grp, grp, rows] + [ANY] * nx,
        out_shape=[SDS((T, AW), BF16), SDS((T, AW), BF16), SDS((T, AW), BF16), SDS((T // S, ng, 8, S), F32)]
                  + _chip_shapes(exchange),
        scratch_shapes=[pltpu.VMEM((S, LANES), F32), pltpu.VMEM((S, LANES), F32)] + _exchange_sems(3 * nx),
        compiler_params=_cparams(2),
    )(pm, pm, pm, do, fcum, frow, lse, *exchange)


def _forget_bwd(dfc, fl, bf, S):
    T, NF = fl.shape
    ch = _tile(S, 256, 8)

    def body(df_ref, fl_ref, bf_ref, dfl_ref, dbf_ref):
        @pl.when(pl.program_id(0) == 0)
        def _():
            dbf_ref[...] = jnp.zeros_like(dbf_ref)

        row = lax.broadcasted_iota(jnp.int32, (ch, ch), 0)
        col = lax.broadcasted_iota(jnp.int32, (ch, ch), 1)
        tri = (col >= row).astype(BF16)
        carry = jnp.zeros((1, NF), F32)
        for c in range(S - ch, -1, -ch):
            d = df_ref[c:c + ch, :]
            dlf = _tri_dot(tri, d) + carry
            carry = carry + jnp.sum(d, axis=0, keepdims=True)
            dfl = dlf * _sigmoid(-(fl_ref[c:c + ch, :] + bf_ref[...]))
            dfl_ref[c:c + ch, :] = dfl.astype(BF16)
            dbf_ref[0:1, :] += jnp.sum(dfl, axis=0, keepdims=True)

    return pl.pallas_call(
        body, name="forget_bwd", grid=(T // S,),
        in_specs=[pl.BlockSpec((S, NF), lambda b: (b, 0)), pl.BlockSpec((S, NF), lambda b: (b, 0)), _resident((1, NF))],
        out_specs=[pl.BlockSpec((S, NF), lambda b: (b, 0)), pl.BlockSpec((8, NF), lambda b: (0, 0))],
        out_shape=[SDS((T, NF), BF16), SDS((8, NF), F32)],
        compiler_params=_cparams(1),
    )(dfc, fl, bf)


def _inproj_bwd(dparts, offs, w_t, x, g1, dx2, tm, exchange):
    T, D = x.shape
    npart, nx = len(dparts), len(exchange)
    n = T // tm

    def body(*refs):
        d_refs = refs[:npart]
        w_ref, x_ref, g_ref, dx2_ref = refs[npart:npart + 4]
        x_refs = refs[npart + 4:npart + 4 + nx]
        dx_ref, dg_ref = refs[npart + 4 + nx:npart + 6 + nx]
        got_refs, sems = refs[npart + 6 + nx:npart + 6 + 2 * nx], refs[npart + 6 + 2 * nx:]

        @pl.when(pl.program_id(0) == 0)
        def _():
            for cp in _chip_copies(x_refs, got_refs, *sems):
                cp.start()

        @pl.when(pl.program_id(0) == 0)
        def _():
            dg_ref[...] = jnp.zeros_like(dg_ref)

        dh = None
        for d_ref, off in zip(d_refs, offs):
            term = _dot(d_ref[...], w_ref[off:off + d_ref.shape[1], :])
            dh = term if dh is None else dh + term
        xv = x_ref[...]
        r = _rms(xv)
        xn = xv * r
        dg_ref[0:1, :] += jnp.sum(dh * xn, axis=0, keepdims=True)
        dxn = dh * g_ref[...]
        dx_ref[...] = dx2_ref[...] + r * (dxn - xn * jnp.mean(dxn * xn, axis=-1, keepdims=True))

        @pl.when(pl.program_id(0) == n - 1)
        def _():
            for cp in _chip_copies(x_refs, got_refs, *sems):
                cp.wait()

    tok = lambda w: pl.BlockSpec((tm, w), lambda i: (i, 0))
    return pl.pallas_call(
        body, name="inproj_bwd", grid=(n,),
        in_specs=[tok(d.shape[1]) for d in dparts] + [_resident(w_t.shape), tok(D), _resident((1, D)), tok(D)] + [ANY] * nx,
        out_specs=[tok(D), pl.BlockSpec((8, D), lambda i: (0, 0))] + [ANY] * nx,
        out_shape=[SDS((T, D), F32), SDS((8, D), F32)] + _chip_shapes(exchange),
        scratch_shapes=_exchange_sems(3 * nx),
        compiler_params=_cparams(1),
    )(*dparts, w_t, x, g1, dx2, *exchange)


def _wgrad(b, a, name, into=None, row_off=0, total_rows=None):
    T, N = b.shape
    M = a.shape[1]
    tn = _tile(N, 1408 if M <= 1024 else 512)
    while row_off % tn:
        tn = _tile(N, tn - LANES)
    tk = _tile(T, 1024, 16)
    blk0 = row_off // tn

    def body(b_ref, a_ref, *rest):
        o_ref = rest[-1]

        @pl.when(pl.program_id(1) == 0)
        def _():
            o_ref[...] = jnp.zeros_like(o_ref)

        o_ref[...] += _dot_tn(b_ref[...].astype(BF16), a_ref[...].astype(BF16))

    in_specs = [pl.BlockSpec((tk, tn), lambda j, k: (k, j)), pl.BlockSpec((tk, M), lambda j, k: (k, 0))]
    args = (b, a)
    kwargs = {}
    if into is not None:
        in_specs.append(ANY)
        args += (into,)
        kwargs["input_output_aliases"] = {2: 0}
        total_rows = into.shape[0]
    return pl.pallas_call(
        body, name=name, grid=(N // tn, T // tk),
        in_specs=in_specs,
        out_specs=pl.BlockSpec((tn, M), lambda j, k: (blk0 + j, 0)),
        out_shape=SDS((N if total_rows is None else total_rows, M), F32),
        compiler_params=_cparams(2), **kwargs,
    )(*args)


def _adamw(w, g, m, v, name):
    shape = w.shape
    C = shape[-1]
    w2, g2, m2, v2 = (a.reshape(-1, C) for a in (w, g, m, v))
    R = w2.shape[0]
    tr = R if R <= 512 else _tile(R, 256, 8)

    def body(w_ref, g_ref, m_ref, v_ref, d_ref, nm_ref, nv_ref):
        gv = g_ref[...]
        mv = ADAM_B1 * m_ref[...] + (1.0 - ADAM_B1) * gv
        vv = ADAM_B2 * v_ref[...] + (1.0 - ADAM_B2) * (gv * gv)
        m_hat = mv / (1.0 - ADAM_B1 ** ADAM_STEP)
        v_hat = vv / (1.0 - ADAM_B2 ** ADAM_STEP)
        d_ref[...] = -ADAM_LR * (m_hat / (jnp.sqrt(v_hat) + ADAM_EPS) + ADAM_WD * w_ref[...])
        nm_ref[...] = mv
        nv_ref[...] = vv

    spec = pl.BlockSpec((tr, C), lambda r: (r, 0))
    outs = pl.pallas_call(
        body, name=name, grid=(R // tr,),
        in_specs=[spec] * 4, out_specs=[spec] * 3, out_shape=[SDS((R, C), F32)] * 3,
        compiler_params=_cparams(1),
    )(w2, g2, m2, v2)
    return tuple(o.reshape(shape) for o in outs)


def _rows(a, L):
    lead = a.shape[0]
    flat = a.reshape(lead, -1)
    n = flat.shape[1]
    r = -(-n // L)
    return jnp.pad(flat, ((0, 0), (0, r * L - n))).reshape(lead, r, L)


def _unrows(p, shape):
    return p.reshape(-1)[:int(np.prod(shape))].reshape(shape)


def _col_blocks(a):
    R, C = a.shape
    return a.reshape(R, N_DEV, C // N_DEV).transpose(1, 0, 2)


def _from_col_blocks(a):
    n, R, c = a.shape
    return a.transpose(1, 0, 2).reshape(R, n * c)


def _in_windows(n_loc, nqkv, H, NM):
    win = (n_loc + 7 + 7) // 8 * 8
    starts, index = [], np.zeros((N_DEV, n_loc), np.int32)
    for d in range(N_DEV):
        rows = np.arange(n_loc * d, n_loc * (d + 1))
        is_f = (rows >= nqkv) & (rows < nqkv + H)
        kept = np.where(rows < nqkv, rows, rows - H)
        lo = int(kept[~is_f].min())
        start = lo // 8 * 8
        assert int(kept[~is_f].max()) - start < win and start + win <= NM + LANES
        starts.append(start)
        index[d] = np.where(is_f, win + rows - nqkv, kept - start)
    return starts, win, index


def kernel(x, norm_mix_g, w_in, b_f, b_gate, conv_mix_w, w_out_conv, w_out_attn, w_o, norm_ffn_g, w_up, conv_ffn_w, w_down, norm_f_g, loss_target, m_norm_mix_g, m_w_in, m_b_f, m_b_gate, m_conv_mix_w, m_w_out_conv, m_w_out_attn, m_w_o, m_norm_ffn_g, m_w_up, m_conv_ffn_w, m_w_down, m_norm_f_g, v_norm_mix_g, v_w_in, v_b_f, v_b_gate, v_conv_mix_w, v_w_out_conv, v_w_out_attn, v_w_o, v_norm_ffn_g, v_w_up, v_conv_ffn_w, v_w_down, v_norm_f_g):
    Bl, S, D = x.shape
    T = Bl * S
    H = b_f.shape[-1]
    CW = N_DEV * conv_mix_w.shape[-1]
    AW = w_out_attn.shape[1]
    hd = AW // H
    FH = N_DEV * w_down.shape[1]
    n_loc = w_in.shape[-1]
    NIN = N_DEV * n_loc
    NM = 3 * CW + 3 * AW + 2 * D
    nqkv = 3 * CW + 3 * AW
    assert NIN == NM + H and w_out_conv.shape[1] == CW and nqkv % D == 0 and AW % LANES == 0 and LANES % hd == 0
    hpg, ng = LANES // hd, AW // LANES
    assert hpg <= 8
    tm_big = min(512, S // 2)
    tm_ffn = min(256, S // 2)
    tq = min(256, S // 2)
    px, py, pc = _place()
    me = 4 * px + 2 * py + pc

    bits = lambda a: lax.bitcast_convert_type(a, BF16)
    taps = jnp.concatenate([_rows(bits(conv_ffn_w[0])[None], D)[0], _rows(bits(conv_mix_w[0])[None], D)[0]], axis=0)
    n_ffn_rows = -(-conv_ffn_w[0].size * 2 // D)
    late = [w_up[0].T.astype(BF16), w_down[0].astype(BF16), w_o[0].astype(BF16), w_out_conv[0].T.astype(BF16),
            w_out_attn[0].T.astype(BF16), taps]
    g_in, = _all_gather([w_in[0].T.astype(BF16)], "weights_all_gather")
    W_in_rows = g_in.reshape(NIN, D)
    W_in_t = jnp.concatenate([W_in_rows[:nqkv], W_in_rows[nqkv + H:], W_in_rows[nqkv:nqkv + H],
                              jnp.zeros((LANES - H, D), BF16)], axis=0)
    bf128 = jnp.pad(b_f, ((0, 0), (0, LANES - H)))

    x2d = x.reshape(T, D)
    tgt = loss_target.reshape(T, D)
    pm, fl, h1 = _inproj_fwd(x2d, norm_mix_g, W_in_t, NM, tm_big)
    fcum = _forget_cumsum(fl, bf128, S)
    frow = jnp.pad(fcum[:, :H].reshape(Bl, S, ng, hpg).transpose(0, 2, 3, 1), ((0, 0), (0, 0), (0, 8 - hpg), (0, 0)))
    q_off = 3 * CW // LANES
    o, lse, g_up, g_dn, g_o, g_oc, g_oa, g_taps = _attn_fwd(pm, fcum, frow, S, q_off, AW, hd, tq, late)
    W_up_t = g_up.reshape(2 * FH, D)
    W_dn = g_dn.reshape(FH, D)
    W_o = g_o.reshape(D, D)
    W_oc_t = g_oc.reshape(D, CW)
    W_oa_t = g_oa.reshape(D, AW)
    tap_bits = g_taps.reshape(N_DEV, -1)
    n_ffn, n_mix = conv_ffn_w[0].size * 2, conv_mix_w[0].size * 2
    wf_full = _from_col_blocks(lax.bitcast_convert_type(
        tap_bits[:, :n_ffn].reshape((N_DEV,) + conv_ffn_w.shape[1:] + (2,)), F32))
    wc_full = _from_col_blocks(lax.bitcast_convert_type(
        tap_bits[:, n_ffn_rows * D:n_ffn_rows * D + n_mix].reshape((N_DEV,) + conv_mix_w.shape[1:] + (2,)), F32))
    wf8 = jnp.pad(wf_full, ((0, 5), (0, 0)))
    wc8 = jnp.pad(wc_full, ((0, 5), (0, 0)))
    x2, u, m, ycin, yc, ya = _merge_fwd(pm, o, x2d, wc8, b_gate, W_oc_t, W_oa_t, W_o, S, tm_big)
    upp, up, h2, dx3, loss8, dgf8 = _ffn_fwd(x2, norm_ffn_g, W_up_t, wf8, W_dn, norm_f_g.reshape(1, D), tgt, S, tm_ffn)

    dx2, dupp, act, dwf8, dg2_8 = _ffn_bwd(dx3, x2, norm_ffn_g, upp, up, wf8, W_dn, W_up_t, S, tm_ffn)
    dW_dn = _wgrad(act, dx3, "wgrad_down")
    dW_up_t = _wgrad(dupp, h2, "wgrad_up")
    ids = jnp.stack([pc, 2 * px + py]).astype(jnp.int32)
    big = [dW_up_t.reshape(4, 2, -1, D), dW_dn.reshape(4, 2, -1, D)]
    dconv, dgl, do, dyc, dya, dwc8, dbg8, *sib_big = _merge_bwd(
        dx2, pm, u, yc, ya, wc8, b_gate, W_o, W_oc_t, W_oa_t, S, tm_big, big)
    big_sums = [_pair_sum(b, r, ids, "grads_pair_sum_%d" % a) for a, (b, r) in enumerate(zip(big, sib_big))]
    dW_o = _wgrad(m, dx2, "wgrad_o")
    dW_oc_t = _wgrad(dyc, ycin, "wgrad_out_conv")
    dW_oa_t = _wgrad(dya, o, "wgrad_out_attn")
    dq, dk, dv, dfr, *chips_big = _attn_bwd(pm, do, fcum, frow, lse, S, q_off, AW, hd, tq, [s[1] for s in big_sums])
    dfc = jnp.pad(dfr[:, :, :hpg, :].transpose(0, 3, 1, 2).reshape(T, H), ((0, 0), (0, LANES - H)))
    dfl, dbf8 = _forget_bwd(dfc, fl, bf128, S)
    dparts = [dconv, dq, dk, dv, dgl, dfl]
    offs = [0, 3 * CW, 3 * CW + AW, 3 * CW + 2 * AW, nqkv, NM]
    dW_in_t = _wgrad(dparts[0], h1, "wgrad_in_0", total_rows=NM + LANES)
    for k in range(1, len(dparts)):
        dW_in_t = _wgrad(dparts[k], h1, "wgrad_in_%d" % k, into=dW_in_t, row_off=offs[k])

    starts, win, index = _in_windows(n_loc, nqkv, H, NM)
    small = jnp.concatenate([dW_o.reshape(N_DEV, -1, D), _rows(dW_oc_t.reshape(N_DEV, -1), D), _rows(dW_oa_t.reshape(N_DEV, -1), D),
                             _rows(_col_blocks(dwf8[:3]), D), _rows(_col_blocks(dwc8[:3]), D)], axis=1)
    small = jnp.pad(small, ((0, 0), (0, -small.shape[1] % 8), (0, 0)))
    small = small.reshape(4, 2, -1, D)
    my_starts = [jnp.where(pc == 0, starts[2 * j], starts[2 * j + 1]) for j in range(4)]
    win_ids = jnp.stack([pc, 2 * px + py] + my_starts).astype(jnp.int32)
    sib_win, sib_small = _sibling_exchange(dW_in_t, starts, win, [small], "grads_sibling_exchange")
    sums = [_pair_sum(dW_in_t, sib_win, win_ids, "grads_pair_sum_in", win_rows=win),
            _pair_sum(small, sib_small, ids, "grads_pair_sum_small")]
    grad_x, dg1_8, *from_chips = _inproj_bwd(dparts, offs, W_in_t, x2d, norm_mix_g, dx2, tm_big, [s[1] for s in sums])
    red_win, red_small, red_up, red_dn = [
        _final_sum(s[0], r, "grads_final_sum_%d" % a)
        for a, (s, r) in enumerate(zip(sums + big_sums, list(from_chips) + list(chips_big)))]

    f_rows = dW_in_t[NM:NM + 8]
    smalls = [dg1_8[0:1], dg2_8[0:1], dgf8[0:1], dbg8[0:1, :D], dbg8[0:1, D:], jnp.pad(dbf8[0:1], ((0, 0), (0, D - LANES))),
              jnp.pad(loss8[0:1], ((0, 0), (0, D - LANES))), jnp.zeros((1, D), F32), f_rows]
    spack = jnp.concatenate(smalls, axis=0)
    ssum = _sum8(_all_gather([spack], "small_all_gather")[0], "small_sum")
    g_g1, g_g2, g_gf = ssum[0:1], ssum[1:2], ssum[2]
    g_bg = jnp.concatenate([ssum[3:4], ssum[4:5]], axis=1)
    g_bf = ssum[5:6, :H]

    ext = jnp.concatenate([red_win, ssum[8:8 + H]], axis=0)
    my_index = lax.dynamic_index_in_dim(jnp.asarray(index), me, axis=0, keepdims=False)
    g_w_in = jnp.take(ext, my_index, axis=0).T
    g_w_up = red_up.T
    r_o = dW_o.shape[0] // N_DEV
    r_oc = _rows(dW_oc_t.reshape(N_DEV, -1), D).shape[1]
    r_wf = _rows(_col_blocks(dwf8[:3]), D).shape[1]
    r_wc = _rows(_col_blocks(dwc8[:3]), D).shape[1]
    o0, o1, o2, o3 = r_o, r_o + r_oc, r_o + 2 * r_oc, r_o + 2 * r_oc + r_wf
    g_w_o = red_small[:o0]
    g_w_oc = _unrows(red_small[o0:o1], (D // N_DEV, CW)).T
    g_w_oa = _unrows(red_small[o1:o2], (D // N_DEV, AW)).T
    g_wf = _unrows(red_small[o2:o3], conv_ffn_w.shape[1:])
    g_wc = _unrows(red_small[o3:o3 + r_wc], conv_mix_w.shape[1:])

    loss = ssum[6, 0]

    names = ["norm_mix_g", "w_in", "b_f", "b_gate", "conv_mix_w", "w_out_conv", "w_out_attn", "w_o", "norm_ffn_g", "w_up",
             "conv_ffn_w", "w_down", "norm_f_g"]
    weights = [norm_mix_g, w_in, b_f, b_gate, conv_mix_w, w_out_conv, w_out_attn, w_o, norm_ffn_g, w_up, conv_ffn_w, w_down, norm_f_g]
    grads = [g_g1, g_w_in, g_bf, g_bg, g_wc, g_w_oc, g_w_oa, g_w_o, g_g2, g_w_up, g_wf, red_dn, g_gf]
    ms = [m_norm_mix_g, m_w_in, m_b_f, m_b_gate, m_conv_mix_w, m_w_out_conv, m_w_out_attn, m_w_o, m_norm_ffn_g, m_w_up,
          m_conv_ffn_w, m_w_down, m_norm_f_g]
    vs = [v_norm_mix_g, v_w_in, v_b_f, v_b_gate, v_conv_mix_w, v_w_out_conv, v_w_out_attn, v_w_o, v_norm_ffn_g, v_w_up,
          v_conv_ffn_w, v_w_down, v_norm_f_g]
    grads = [g.reshape(w.shape) for g, w in zip(grads, weights)]
    steps = [_adamw(w, g, mm, vv, "adamw_" + nm) for nm, w, g, mm, vv in zip(names, weights, grads, ms, vs)]
    deltas, new_ms, new_vs = zip(*steps)
    return (loss, grad_x.reshape(Bl, S, D), *grads, *deltas, *new_ms, *new_vs)
```

```python
import numpy as np

import jax
import jax.numpy as jnp
from jax import lax
from jax.experimental import pallas as pl
from jax.experimental.pallas import tpu as pltpu

F32, BF16 = jnp.float32, jnp.bfloat16
EPS = 1e-6
ADAM_LR, ADAM_B1, ADAM_B2, ADAM_EPS, ADAM_WD, ADAM_STEP = 0.001, 0.9, 0.999, 1e-08, 0.01, 10
N_DEV = 8
LANES = 128
V7X_VMEM_LIMIT = 56 * 1024 * 1024
FFN_CHUNK = 2816
MESH = pl.DeviceIdType.MESH
SDS = jax.ShapeDtypeStruct
ANY = pl.BlockSpec(memory_space=pl.ANY)


def _tile(n, target, mult=LANES):
    best = None
    for t in range(mult, min(n, target) + 1, mult):
        if n % t == 0:
            best = t
    return best if best is not None else n


def _resident(shape):
    return pl.BlockSpec(shape, lambda *_: (0,) * len(shape), pipeline_mode=pl.Buffered(1))


def _cparams(n_axes=1):
    return pltpu.CompilerParams(dimension_semantics=("arbitrary",) * n_axes, vmem_limit_bytes=V7X_VMEM_LIMIT)


def _dot(a, b):
    return jnp.dot(a, b, preferred_element_type=F32)


def _dot_tn(a, b):
    return lax.dot_general(a, b, (((0,), (0,)), ((), ())), preferred_element_type=F32)


def _dot_nt(a, b):
    return lax.dot_general(a, b, (((1,), (1,)), ((), ())), preferred_element_type=F32)


def _sigmoid(x):
    return 0.5 * jnp.tanh(0.5 * x) + 0.5


def _rms(x):
    return lax.rsqrt(jnp.mean(x * x, axis=-1, keepdims=True) + EPS)


def _taps_back(z, r6, r7):
    row = lax.broadcasted_iota(jnp.int32, (8, 1), 0)
    z1, z2 = pltpu.roll(z, 1, 0), pltpu.roll(z, 2, 0)
    z1 = jnp.concatenate([jnp.where(row == 0, r7, z1[0:8]), z1[8:]], axis=0)
    z2 = jnp.concatenate([jnp.where(row == 0, r6, jnp.where(row == 1, r7, z2[0:8])), z2[8:]], axis=0)
    return z1, z2


def _taps_ahead(d, h0, h1):
    tm = d.shape[0]
    row = lax.broadcasted_iota(jnp.int32, (8, 1), 0)
    d1, d2 = pltpu.roll(d, tm - 1, 0), pltpu.roll(d, tm - 2, 0)
    d1 = jnp.concatenate([d1[:tm - 8], jnp.where(row == 7, h0, d1[tm - 8:])], axis=0)
    d2 = jnp.concatenate([d2[:tm - 8], jnp.where(row == 6, h0, jnp.where(row == 7, h1, d2[tm - 8:]))], axis=0)
    return d1, d2


def _tri_dot(tri, x):
    hi = x.astype(BF16)
    r = x - hi.astype(F32)
    mid = r.astype(BF16)
    lo = (r - mid.astype(F32)).astype(BF16)
    return (_dot(tri, lo) + _dot(tri, mid)) + _dot(tri, hi)


def _lane_pick(block, lane):
    lanes = lax.broadcasted_iota(jnp.int32, (1, block.shape[1]), 1)
    return jnp.sum(jnp.where(lanes == lane, block, 0.0), axis=1, keepdims=True)


def _place():
    return lax.axis_index("x"), lax.axis_index("y"), lax.axis_index("c")


def _all_gather(xs, name):
    n = len(xs)

    def body(*refs):
        start, forward, finish = _gather_phases(refs[:n], refs[n:2 * n], *refs[2 * n:])
        start()
        forward()
        finish()

    return pl.pallas_call(
        body, name=name,
        out_shape=_gather_shapes(xs), in_specs=[ANY] * n, out_specs=[ANY] * n, scratch_shapes=_gather_sems(n),
    )(*xs)


def _gather_shapes(xs):
    return [SDS((N_DEV,) + x.shape, x.dtype) for x in xs]


def _gather_sems(n):
    return [pltpu.SemaphoreType.DMA((7 * n,)), pltpu.SemaphoreType.DMA((7 * n,)), pltpu.SemaphoreType.DMA((n,))]


def _gather_phases(x_refs, out_refs, send_sems, recv_sems, local_sems):
    n = len(x_refs)

    def parts():
        px, py, pc = _place()
        me, sibling = (px, py, pc), (px, py, 1 - pc)
        chips = [(1 - px, py), (px, 1 - py), (1 - px, 1 - py)]

        def slot(a, qx, qy, qc):
            return out_refs[a].at[4 * qx + 2 * qy + qc]

        def copy(a, k, block, to, src=None):
            return pltpu.make_async_remote_copy(
                src_ref=slot(a, *block) if src is None else src, dst_ref=slot(a, *block),
                send_sem=send_sems.at[7 * a + k], recv_sem=recv_sems.at[7 * a + k], device_id=to, device_id_type=MESH)

        def mine():
            return [pltpu.make_async_copy(x_refs[a], slot(a, *me), local_sems.at[a]) for a in range(n)]

        def first():
            out = []
            for a in range(n):
                out.append(copy(a, 0, me, sibling, src=x_refs[a]))
                out += [copy(a, 1 + j, me, (*chip, pc), src=x_refs[a]) for j, chip in enumerate(chips)]
            return out

        def landed():
            return [copy(a, 1 + j, (*chip, pc), me) for j, chip in enumerate(chips) for a in range(n)]

        def passed():
            return [copy(a, 4 + j, (*chip, pc), sibling) for j, chip in enumerate(chips) for a in range(n)]

        def late():
            out = [copy(a, 0, sibling, me) for a in range(n)]
            return out + [copy(a, 4 + j, (*chip, 1 - pc), me) for j, chip in enumerate(chips) for a in range(n)]

        return mine, first, landed, passed, late

    def start():
        mine, first, _, _, _ = parts()
        for cp in mine() + first():
            cp.start()

    def forward():
        _, _, landed, passed, _ = parts()
        for got, cp in zip(landed(), passed()):
            got.wait_recv()
            cp.start()

    def finish():
        mine, first, _, passed, late = parts()
        for cp in late():
            cp.wait_recv()
        for cp in first() + passed():
            cp.wait_send()
        for cp in mine():
            cp.wait()

    return start, forward, finish


def _sibling_exchange(win_buf, win_starts, win_rows, blocked, name):
    nb = len(blocked)

    def body(*refs):
        win_ref, blk_refs = refs[0], refs[1:1 + nb]
        rwin_ref, rblk_refs = refs[1 + nb], refs[2 + nb:2 + 2 * nb]
        send_sems, recv_sems, wsend_sems, wrecv_sems = refs[2 + 2 * nb:]
        px, py, pc = _place()
        copies = _sibling_copies(blk_refs, rblk_refs, send_sems, recv_sems)
        for j in range(4):
            theirs = jnp.where(pc == 0, win_starts[2 * j + 1], win_starts[2 * j])
            copies.append(pltpu.make_async_remote_copy(
                src_ref=win_ref.at[pl.ds(pl.multiple_of(theirs, 8), win_rows)], dst_ref=rwin_ref.at[j],
                send_sem=wsend_sems.at[j], recv_sem=wrecv_sems.at[j], device_id=(px, py, 1 - pc), device_id_type=MESH))
        for cp in copies:
            cp.start()
        for cp in copies:
            cp.wait()

    C = win_buf.shape[1]
    return pl.pallas_call(
        body, name=name,
        out_shape=[SDS((4, win_rows, C), F32)] + _sibling_shapes(blocked),
        in_specs=[ANY] * (1 + nb), out_specs=[ANY] * (1 + nb),
        scratch_shapes=_exchange_sems(nb) + _exchange_sems(4),
    )(win_buf, *blocked)


def _sibling_shapes(blocked):
    return [SDS((4,) + b.shape[2:], F32) for b in blocked]


def _exchange_sems(n):
    return [pltpu.SemaphoreType.DMA((n,)), pltpu.SemaphoreType.DMA((n,))]


def _sibling_copies(blk_refs, out_refs, send_sems, recv_sems):
    px, py, pc = _place()
    return [pltpu.make_async_remote_copy(
        src_ref=b.at[:, 1 - pc], dst_ref=o, send_sem=send_sems.at[a], recv_sem=recv_sems.at[a],
        device_id=(px, py, 1 - pc), device_id_type=MESH) for a, (b, o) in enumerate(zip(blk_refs, out_refs))]


def _chip_shapes(ps):
    return [SDS((3,) + p.shape[1:], p.dtype) for p in ps]


def _chip_copies(p_refs, out_refs, send_sems, recv_sems):
    px, py, pc = _place()
    n = len(p_refs)
    chips = [(1 - px, py), (px, 1 - py), (1 - px, 1 - py)]
    return [pltpu.make_async_remote_copy(
        src_ref=p_refs[a].at[2 * qx + qy], dst_ref=out_refs[a].at[k],
        send_sem=send_sems.at[3 * a + k], recv_sem=recv_sems.at[3 * a + k],
        device_id=(qx, qy, pc), device_id_type=MESH) for k, (qx, qy) in enumerate(chips) for a in range(n)]


def _pair_sum(own, recv, ids, name, win_rows=None):
    _, R, C = recv.shape
    tr = _tile(R, 512, 8)

    def body(ids_ref, g_ref, r_ref, own_ref, pb_ref):
        s = g_ref[...] + r_ref[...]
        pb_ref[...] = s.astype(BF16)

        @pl.when(pl.program_id(1) == ids_ref[1])
        def _():
            own_ref[...] = s

    if win_rows is None:
        own_spec = pl.BlockSpec((None, None, tr, C), lambda r, j, ids: (j, ids[0], r, 0))
    else:
        own_spec = pl.BlockSpec((pl.Element(tr), pl.Element(C)), lambda r, j, ids: (pl.multiple_of(ids[2 + j] + r * tr, 8), 0))
    return pl.pallas_call(
        body, name=name,
        grid_spec=pltpu.PrefetchScalarGridSpec(
            num_scalar_prefetch=1, grid=(R // tr, 4),
            in_specs=[own_spec, pl.BlockSpec((None, tr, C), lambda r, j, ids: (j, r, 0))],
            out_specs=[pl.BlockSpec((tr, C), lambda r, j, ids: (r, 0)),
                       pl.BlockSpec((None, tr, C), lambda r, j, ids: (j, r, 0))]),
        out_shape=[SDS((R, C), F32), SDS((4, R, C), BF16)],
        compiler_params=_cparams(2),
    )(ids, own, recv)


def _final_sum(own, recv, name):
    R, C = own.shape
    tr = _tile(R, 512, 8)

    def body(o_ref, r_ref, out_ref):
        out_ref[...] = ((o_ref[...] + r_ref[0].astype(F32)) + r_ref[1].astype(F32)) + r_ref[2].astype(F32)

    return pl.pallas_call(
        body, name=name, grid=(R // tr,),
        in_specs=[pl.BlockSpec((tr, C), lambda r: (r, 0)), pl.BlockSpec((3, tr, C), lambda r: (0, r, 0))],
        out_specs=pl.BlockSpec((tr, C), lambda r: (r, 0)),
        out_shape=SDS((R, C), F32),
        compiler_params=_cparams(1),
    )(own, recv)


def _sum8(a, name):
    def body(a_ref, out_ref):
        s = a_ref[0]
        for d in range(1, N_DEV):
            s = s + a_ref[d]
        out_ref[...] = s

    return pl.pallas_call(body, name=name, out_shape=SDS(a.shape[1:], F32))(a)


def _inproj_fwd(x, g1, w_t, NM, tm):
    T, D = x.shape
    NF = w_t.shape[0] - NM
    ch = _tile(NM, 1024)

    def body(x_ref, g_ref, w_ref, pm_ref, fl_ref, h_ref):
        xv = x_ref[...]
        h = (xv * _rms(xv) * g_ref[...]).astype(BF16)
        h_ref[...] = h
        for c in range(0, NM, ch):
            pm_ref[:, c:c + ch] = _dot_nt(h, w_ref[c:c + ch, :]).astype(BF16)
        fl_ref[...] = _dot_nt(h, w_ref[NM:NM + NF, :])

    return pl.pallas_call(
        body, name="inproj_fwd", grid=(T // tm,),
        in_specs=[pl.BlockSpec((tm, D), lambda i: (i, 0)), _resident((1, D)), _resident(w_t.shape)],
        out_specs=[pl.BlockSpec((tm, NM), lambda i: (i, 0)), pl.BlockSpec((tm, NF), lambda i: (i, 0)),
                   pl.BlockSpec((tm, D), lambda i: (i, 0))],
        out_shape=[SDS((T, NM), BF16), SDS((T, NF), F32), SDS((T, D), BF16)],
        compiler_params=_cparams(1),
    )(x, g1, w_t)


def _log_sigmoid(x):
    return jnp.minimum(x, 0.0) - jnp.log(1.0 + jnp.exp(-jnp.abs(x)))


def _forget_cumsum(fl, bf, S):
    T, NF = fl.shape
    ch = _tile(S, 256, 8)

    def body(fl_ref, bf_ref, f_ref):
        row = lax.broadcasted_iota(jnp.int32, (ch, ch), 0)
        col = lax.broadcasted_iota(jnp.int32, (ch, ch), 1)
        tri = (col <= row).astype(BF16)
        carry = jnp.zeros((1, NF), F32)
        for c in range(0, S, ch):
            lf = _log_sigmoid(fl_ref[c:c + ch, :] + bf_ref[...])
            f_ref[c:c + ch, :] = _tri_dot(tri, lf) + carry
            carry = carry + jnp.sum(lf, axis=0, keepdims=True)

    return pl.pallas_call(
        body, name="forget_cumsum", grid=(T // S,),
        in_specs=[pl.BlockSpec((S, NF), lambda b: (b, 0)), _resident((1, NF))],
        out_specs=pl.BlockSpec((S, NF), lambda b: (b, 0)),
        out_shape=SDS((T, NF), F32),
        compiler_params=_cparams(1),
    )(fl, bf)


def _head_mask(e, hd):
    lanes = lax.broadcasted_iota(jnp.int32, (1, LANES), 1)
    return (lanes >= e * hd) & (lanes < (e + 1) * hd)


def _attn_specs(S, q_off, AW):
    ng = AW // LANES
    return [pl.BlockSpec((S, LANES), lambda b, g, o=q_off + w * ng: (b, o + g)) for w in range(3)]


def _attn_fwd(pm, fcum, frow, S, q_off, AW, hd, tq, gather):
    T = pm.shape[0]
    scale = float(hd) ** -0.5
    nq, hpg = S // tq, LANES // hd
    ng, nx = AW // LANES, len(gather)
    steps = (T // S) * ng

    def body(q_ref, k_ref, v_ref, fc_ref, fr_ref, *rest):
        x_refs, (o_ref, lse_ref), out_refs, sems = rest[:nx], rest[nx:nx + 2], rest[nx + 2:2 * nx + 2], rest[2 * nx + 2:]
        g = pl.program_id(1)
        step = pl.program_id(0) * ng + g
        start, forward, finish = _gather_phases(x_refs, out_refs, *sems)
        pl.when(step == 0)(start)

        @pl.when(g == 0)
        def _():
            lse_ref[...] = jnp.zeros_like(lse_ref)

        lanes = lax.broadcasted_iota(jnp.int32, (1, LANES), 1)
        for i in range(nq):
            rs, kend = slice(i * tq, (i + 1) * tq), (i + 1) * tq
            row = i * tq + lax.broadcasted_iota(jnp.int32, (tq, kend), 0)
            col = lax.broadcasted_iota(jnp.int32, (tq, kend), 1)
            o_tile = jnp.zeros((tq, LANES), F32)
            lse_tile = lse_ref[rs, :]
            for e in range(hpg):
                mask = _head_mask(e, hd)
                qs = jnp.where(mask, q_ref[rs, :], 0) * scale
                s = _dot_nt(qs, k_ref[0:kend, :]) + _lane_pick(fc_ref[rs, :], g * hpg + e) - fr_ref[e:e + 1, 0:kend]
                s = jnp.where(col <= row, s, -1e30)
                m = jnp.max(s, axis=1, keepdims=True)
                p = jnp.exp(s - m)
                l = jnp.sum(p, axis=1, keepdims=True)
                o_tile = jnp.where(mask, _dot(p.astype(BF16), v_ref[0:kend, :]) / l, o_tile)
                lse_tile = jnp.where(lanes == g * hpg + e, m + jnp.log(l), lse_tile)
            o_ref[rs, :] = o_tile.astype(BF16)
            lse_ref[rs, :] = lse_tile
        pl.when(step == (steps * 5) // 8)(forward)
        pl.when(step == steps - 1)(finish)

    full = pl.BlockSpec((S, LANES), lambda b, g: (b, 0))
    return pl.pallas_call(
        body, name="attn_fwd", grid=(T // S, ng),
        in_specs=_attn_specs(S, q_off, AW) + [full, pl.BlockSpec((None, None, 8, S), lambda b, g: (b, g, 0, 0))] + [ANY] * nx,
        out_specs=[pl.BlockSpec((S, LANES), lambda b, g: (b, g)), full] + [ANY] * nx,
        out_shape=[SDS((T, AW), BF16), SDS((T, LANES), F32)] + _gather_shapes(gather),
        scratch_shapes=_gather_sems(nx),
        compiler_params=_cparams(2),
    )(pm, pm, pm, fcum, frow, *gather)


def _merge_fwd(pm, o, x, wc, bg, woc_t, woa_t, wo, S, tm):
    T, D = x.shape
    CW, AW = woc_t.shape[1], woa_t.shape[1]
    tps = S // tm

    def body(cb_ref, cc_ref, cin_ref, gc_ref, ga_ref, o_ref, x_ref, wc_ref, bg_ref, woct_ref, woat_ref, wo_ref,
             x2_ref, u_ref, m_ref, ycin_ref, yc_ref, ya_ref, tail_ref):
        @pl.when(pl.program_id(0) % tps == 0)
        def _():
            tail_ref[...] = jnp.zeros_like(tail_ref)

        z = cc_ref[...].astype(F32) * cin_ref[...].astype(F32)
        z1, z2 = _taps_back(z, tail_ref[6:7, :], tail_ref[7:8, :])
        tail_ref[...] = z[tm - 8:tm, :]
        u = (z * wc_ref[2:3, :] + z2 * wc_ref[0:1, :]) + z1 * wc_ref[1:2, :]
        u_ref[...] = u.astype(BF16)
        ycin = (cb_ref[...].astype(F32) * u).astype(BF16)
        ycin_ref[...] = ycin
        yc = _dot_nt(ycin, woct_ref[...])
        ya = _dot_nt(o_ref[...], woat_ref[...])
        yc_ref[...] = yc.astype(BF16)
        ya_ref[...] = ya.astype(BF16)
        gc = _sigmoid(gc_ref[...].astype(F32) + bg_ref[:, 0:D])
        ga = _sigmoid(ga_ref[...].astype(F32) + bg_ref[:, D:2 * D])
        m = (gc * yc + ga * ya).astype(BF16)
        m_ref[...] = m
        x2_ref[...] = x_ref[...] + _dot(m, wo_ref[...])

    g_off = (3 * CW + 3 * AW) // D
    tok = lambda w, j=0: pl.BlockSpec((tm, w), lambda i: (i, j))
    return pl.pallas_call(
        body, name="merge_fwd", grid=(T // tm,),
        in_specs=[tok(CW, 0), tok(CW, 1), tok(CW, 2), tok(D, g_off), tok(D, g_off + 1), tok(AW), tok(D),
                  _resident((8, CW)), _resident((1, 2 * D)), _resident((D, CW)), _resident((D, AW)), _resident((D, D))],
        out_specs=[tok(D), tok(CW), tok(D), tok(CW), tok(D), tok(D)],
        out_shape=[SDS((T, D), F32), SDS((T, CW), BF16), SDS((T, D), BF16), SDS((T, CW), BF16),
                   SDS((T, D), BF16), SDS((T, D), BF16)],
        scratch_shapes=[pltpu.VMEM((8, CW), F32)],
        compiler_params=_cparams(1),
    )(pm, pm, pm, pm, pm, o, x, wc, bg, woc_t, woa_t, wo)


def _ffn_fwd(x2, g2, wup_t, wf, wdn, gf, tgt, S, tm):
    T, D = x2.shape
    FH = wdn.shape[0]
    ch = _tile(FH, FFN_CHUNK)
    tps = S // tm

    def body(x2_ref, g2_ref, wupt_ref, wf_ref, wdn_ref, gf_ref, tgt_ref,
             upp_ref, up_ref, h2_ref, dx3_ref, loss_ref, dgf_ref, tail_ref):
        i = pl.program_id(0)

        @pl.when(i % tps == 0)
        def _():
            tail_ref[...] = jnp.zeros_like(tail_ref)

        @pl.when(i == 0)
        def _():
            loss_ref[...] = jnp.zeros_like(loss_ref)
            dgf_ref[...] = jnp.zeros_like(dgf_ref)

        x2v = x2_ref[...]
        h2 = (x2v * _rms(x2v) * g2_ref[...]).astype(BF16)
        h2_ref[...] = h2
        x3 = x2v
        for c in range(0, FH, ch):
            gated = []
            for cols in (slice(c, c + ch), slice(FH + c, FH + c + ch)):
                upp = _dot_nt(h2, wupt_ref[cols, :])
                upp_ref[:, cols] = upp.astype(BF16)
                p1, p2 = _taps_back(upp, tail_ref[6:7, cols], tail_ref[7:8, cols])
                tail_ref[:, cols] = upp[tm - 8:tm, :]
                up = (upp * wf_ref[2:3, cols] + p2 * wf_ref[0:1, cols]) + p1 * wf_ref[1:2, cols]
                up_ref[:, cols] = up.astype(BF16)
                gated.append(up)
            a, b = gated
            act = (a * _sigmoid(a) * b).astype(BF16)
            x3 = x3 + _dot(act, wdn_ref[c:c + ch, :])
        r3 = _rms(x3)
        xn3 = x3 * r3
        e = xn3 * gf_ref[...] - tgt_ref[...]
        loss_ref[...] += 0.5 * jnp.sum(jnp.mean(e * e, axis=-1, keepdims=True), axis=0, keepdims=True)
        dy = e / D
        dgf_ref[0:1, :] += jnp.sum(dy * xn3, axis=0, keepdims=True)
        dxn = dy * gf_ref[...]
        dx3_ref[...] = r3 * (dxn - xn3 * jnp.mean(dxn * xn3, axis=-1, keepdims=True))

    tok = lambda w: pl.BlockSpec((tm, w), lambda i: (i, 0))
    return pl.pallas_call(
        body, name="ffn_fwd", grid=(T // tm,),
        in_specs=[tok(D), _resident((1, D)), _resident((2 * FH, D)), _resident((8, 2 * FH)), _resident((FH, D)),
                  _resident((1, D)), tok(D)],
        out_specs=[tok(2 * FH), tok(2 * FH), tok(D), tok(D), pl.BlockSpec((8, LANES), lambda i: (0, 0)),
                   pl.BlockSpec((8, D), lambda i: (0, 0))],
        out_shape=[SDS((T, 2 * FH), BF16), SDS((T, 2 * FH), BF16), SDS((T, D), BF16), SDS((T, D), F32),
                   SDS((8, LANES), F32), SDS((8, D), F32)],
        scratch_shapes=[pltpu.VMEM((8, 2 * FH), F32)],
        compiler_params=_cparams(1),
    )(x2, g2, wup_t, wf, wdn, gf, tgt)


def _ffn_bwd(dx3, x2, g2, upp, up, wf, wdn, wup_t, S, tm):
    T, D = x2.shape
    FH = wdn.shape[0]
    ch = _tile(FH, FFN_CHUNK)
    n, tps = T // tm, S // tm

    def body(dx3_ref, x2_ref, g2_ref, upp_ref, up_ref, wf_ref, wdn_ref, wupt_ref,
             dx2_ref, dupp_ref, act_ref, dwf_ref, dg2_ref, head_ref):
        i = pl.program_id(0)
        t = n - 1 - i

        @pl.when(i == 0)
        def _():
            dwf_ref[...] = jnp.zeros_like(dwf_ref)
            dg2_ref[...] = jnp.zeros_like(dg2_ref)

        @pl.when(t % tps == tps - 1)
        def _():
            head_ref[...] = jnp.zeros_like(head_ref)

        dx3v = dx3_ref[...]
        dx3b = dx3v.astype(BF16)
        x2v = x2_ref[...]
        r2 = _rms(x2v)
        xn2 = x2v * r2
        dh2 = jnp.zeros((tm, D), F32)
        for c in range(0, FH, ch):
            ca, cb = slice(c, c + ch), slice(FH + c, FH + c + ch)
            a, b = up_ref[:, ca].astype(F32), up_ref[:, cb].astype(F32)
            sig = _sigmoid(a)
            sl = a * sig
            act_ref[:, ca] = (sl * b).astype(BF16)
            dact = _dot_nt(dx3b, wdn_ref[ca, :])
            grads = (dact * b * (sig * (1.0 + a * (1.0 - sig))), dact * sl)
            for cols, d in zip((ca, cb), grads):
                u0 = upp_ref[:, cols].astype(F32)
                d1, d2 = _taps_ahead(d, head_ref[0:1, cols], head_ref[1:2, cols])
                head_ref[:, cols] = d[0:8, :]
                dwf_ref[2:3, cols] += jnp.sum(u0 * d, axis=0, keepdims=True)
                dwf_ref[1:2, cols] += jnp.sum(u0 * d1, axis=0, keepdims=True)
                dwf_ref[0:1, cols] += jnp.sum(u0 * d2, axis=0, keepdims=True)
                dpre = ((d * wf_ref[2:3, cols] + d1 * wf_ref[1:2, cols]) + d2 * wf_ref[0:1, cols]).astype(BF16)
                dupp_ref[:, cols] = dpre
                dh2 = dh2 + _dot(dpre, wupt_ref[cols, :])
        dg2_ref[0:1, :] += jnp.sum(dh2 * xn2, axis=0, keepdims=True)
        dxn = dh2 * g2_ref[...]
        dx2_ref[...] = dx3v + r2 * (dxn - xn2 * jnp.mean(dxn * xn2, axis=-1, keepdims=True))

    tok = lambda w: pl.BlockSpec((tm, w), lambda i: (n - 1 - i, 0))
    acc = lambda w: pl.BlockSpec((8, w), lambda i: (0, 0))
    return pl.pallas_call(
        body, name="ffn_bwd", grid=(n,),
        in_specs=[tok(D), tok(D), _resident((1, D)), tok(2 * FH), tok(2 * FH), _resident((8, 2 * FH)),
                  _resident((FH, D)), _resident((2 * FH, D))],
        out_specs=[tok(D), tok(2 * FH), tok(FH), acc(2 * FH), acc(D)],
        out_shape=[SDS((T, D), F32), SDS((T, 2 * FH), BF16), SDS((T, FH), BF16), SDS((8, 2 * FH), F32), SDS((8, D), F32)],
        scratch_shapes=[pltpu.VMEM((8, 2 * FH), F32)],
        compiler_params=_cparams(1),
    )(dx3, x2, g2, upp, up, wf, wdn, wup_t)


def _merge_bwd(dx2, pm, u, yc, ya, wc, bg, wo, woc_t, woa_t, S, tm, exchange):
    T, D = dx2.shape
    CW, AW = woc_t.shape[1], woa_t.shape[1]
    n, tps = T // tm, S // tm
    nx = len(exchange)

    def body(dx2_ref, cb_ref, cc_ref, cin_ref, gc_ref, ga_ref, u_ref, yc_ref, ya_ref, wc_ref, bg_ref,
             wo_ref, woct_ref, woat_ref, *rest):
        x_refs = rest[:nx]
        dconv_ref, dgl_ref, do_ref, dyc_ref, dya_ref, dwc_ref, dbg_ref = rest[nx:nx + 7]
        got_refs, head_ref, sems = rest[nx + 7:2 * nx + 7], rest[2 * nx + 7], rest[2 * nx + 8:]
        i = pl.program_id(0)
        t = n - 1 - i

        @pl.when(i == 0)
        def _():
            for cp in _sibling_copies(x_refs, got_refs, *sems):
                cp.start()

        @pl.when(i == 0)
        def _():
            dwc_ref[...] = jnp.zeros_like(dwc_ref)
            dbg_ref[...] = jnp.zeros_like(dbg_ref)

        @pl.when(t % tps == tps - 1)
        def _():
            head_ref[...] = jnp.zeros_like(head_ref)

        dm = _dot_nt(dx2_ref[...].astype(BF16), wo_ref[...])
        outs = []
        for g_ref, y_ref, cols in ((gc_ref, yc_ref, slice(0, D)), (ga_ref, ya_ref, slice(D, 2 * D))):
            g = _sigmoid(g_ref[...].astype(F32) + bg_ref[:, cols])
            dgl = dm * y_ref[...].astype(F32) * g * (1.0 - g)
            dgl_ref[:, cols] = dgl.astype(BF16)
            dbg_ref[0:1, cols] += jnp.sum(dgl, axis=0, keepdims=True)
            outs.append((dm * g).astype(BF16))
        dyc, dya = outs
        dyc_ref[...] = dyc
        dya_ref[...] = dya
        do_ref[...] = _dot(dya, woat_ref[...]).astype(BF16)
        dycin = _dot(dyc, woct_ref[...])
        cc, cin = cc_ref[...].astype(F32), cin_ref[...].astype(F32)
        z = cc * cin
        du = dycin * cb_ref[...].astype(F32)
        du1, du2 = _taps_ahead(du, head_ref[0:1, :], head_ref[1:2, :])
        head_ref[...] = du[0:8, :]
        dwc_ref[2:3, :] += jnp.sum(z * du, axis=0, keepdims=True)
        dwc_ref[1:2, :] += jnp.sum(z * du1, axis=0, keepdims=True)
        dwc_ref[0:1, :] += jnp.sum(z * du2, axis=0, keepdims=True)
        dz = (du * wc_ref[2:3, :] + du1 * wc_ref[1:2, :]) + du2 * wc_ref[0:1, :]
        dconv_ref[:, 0:CW] = (dycin * u_ref[...].astype(F32)).astype(BF16)
        dconv_ref[:, CW:2 * CW] = (dz * cin).astype(BF16)
        dconv_ref[:, 2 * CW:3 * CW] = (dz * cc).astype(BF16)

        @pl.when(i == n - 1)
        def _():
            for cp in _sibling_copies(x_refs, got_refs, *sems):
                cp.wait()

    g_off = (3 * CW + 3 * AW) // D
    tok = lambda w, j=0: pl.BlockSpec((tm, w), lambda i: (n - 1 - i, j))
    acc = lambda w: pl.BlockSpec((8, w), lambda i: (0, 0))
    return pl.pallas_call(
        body, name="merge_bwd", grid=(n,),
        in_specs=[tok(D), tok(CW, 0), tok(CW, 1), tok(CW, 2), tok(D, g_off), tok(D, g_off + 1), tok(CW), tok(D), tok(D),
                  _resident((8, CW)), _resident((1, 2 * D)), _resident((D, D)), _resident((D, CW)), _resident((D, AW))]
                 + [ANY] * nx,
        out_specs=[tok(3 * CW), tok(2 * D), tok(AW), tok(D), tok(D), acc(CW), acc(2 * D)] + [ANY] * nx,
        out_shape=[SDS((T, 3 * CW), BF16), SDS((T, 2 * D), BF16), SDS((T, AW), BF16), SDS((T, D), BF16), SDS((T, D), BF16),
                   SDS((8, CW), F32), SDS((8, 2 * D), F32)] + _sibling_shapes(exchange),
        scratch_shapes=[pltpu.VMEM((8, CW), F32)] + _exchange_sems(nx),
        compiler_params=_cparams(1),
    )(dx2, pm, pm, pm, pm, pm, u, yc, ya, wc, bg, wo, woc_t, woa_t, *exchange)


def _attn_bwd(pm, do, fcum, frow, lse, S, q_off, AW, hd, tq, exchange):
    T = pm.shape[0]
    scale = float(hd) ** -0.5
    nq, hpg, ng = S // tq, LANES // hd, AW // LANES
    nx = len(exchange)
    steps = (T // S) * ng

    def body(q_ref, k_ref, v_ref, do_ref, fc_ref, fr_ref, lse_ref, *rest):
        x_refs, (dq_ref, dk_ref, dv_ref, dfr_ref) = rest[:nx], rest[nx:nx + 4]
        got_refs, (dk_acc, dv_acc), sems = rest[nx + 4:2 * nx + 4], rest[2 * nx + 4:2 * nx + 6], rest[2 * nx + 6:]
        g = pl.program_id(1)
        step = pl.program_id(0) * ng + g

        @pl.when(step == 0)
        def _():
            for cp in _chip_copies(x_refs, got_refs, *sems):
                cp.start()

        dk_acc[...] = jnp.zeros_like(dk_acc)
        dv_acc[...] = jnp.zeros_like(dv_acc)
        dfr_ref[...] = jnp.zeros_like(dfr_ref)
        for i in range(nq):
            rs, kend = slice(i * tq, (i + 1) * tq), (i + 1) * tq
            row = i * tq + lax.broadcasted_iota(jnp.int32, (tq, kend), 0)
            col = lax.broadcasted_iota(jnp.int32, (tq, kend), 1)
            kk, vv = k_ref[0:kend, :], v_ref[0:kend, :]
            dq_tile = jnp.zeros((tq, LANES), F32)
            for e in range(hpg):
                mask = _head_mask(e, hd)
                qs = jnp.where(mask, q_ref[rs, :], 0) * scale
                doi = jnp.where(mask, do_ref[rs, :], 0)
                s = _dot_nt(qs, kk) + _lane_pick(fc_ref[rs, :], g * hpg + e) - fr_ref[e:e + 1, 0:kend]
                p = jnp.where(col <= row, jnp.exp(s - _lane_pick(lse_ref[rs, :], g * hpg + e)), 0.0)
                dp = _dot_nt(doi, vv)
                ds = p * (dp - jnp.sum(p * dp, axis=1, keepdims=True))
                pb, dsb = p.astype(BF16), ds.astype(BF16)
                dq_tile = jnp.where(mask, _dot(dsb, kk) * scale, dq_tile)
                dv_acc[0:kend, :] += _dot_tn(pb, doi)
                dk_acc[0:kend, :] += _dot_tn(dsb, qs)
                dfr_ref[e:e + 1, 0:kend] -= jnp.sum(ds, axis=0, keepdims=True)
            dq_ref[rs, :] = dq_tile.astype(BF16)
        dk_ref[...] = dk_acc[...].astype(BF16)
        dv_ref[...] = dv_acc[...].astype(BF16)

        @pl.when(step == steps - 1)
        def _():
            for cp in _chip_copies(x_refs, got_refs, *sems):
                cp.wait()

    full = pl.BlockSpec((S, LANES), lambda b, g: (b, 0))
    grp = pl.BlockSpec((S, LANES), lambda b, g: (b, g))
    rows = pl.BlockSpec((None, None, 8, S), lambda b, g: (b, g, 0, 0))
    return pl.pallas_call(
        body, name="attn_bwd", grid=(T // S, ng),
        in_specs=_attn_specs(S, q_off, AW) + [grp, full, rows, full] + [ANY] * nx,
        out_specs=[grp, grp, grp, rows] + [ANY] * nx,
        out_shape=[SDS((T, AW), BF16), SDS((T, AW), BF16), SDS((T, AW), BF16), SDS((T // S, ng, 8, S), F32)]
                  + _chip_shapes(exchange),
        scratch_shapes=[pltpu.VMEM((S, LANES), F32), pltpu.VMEM((S, LANES), F32)] + _exchange_sems(3 * nx),
        compiler_params=_cparams(2),
    )(pm, pm, pm, do, fcum, frow, lse, *exchange)


def _forget_bwd(dfc, fl, bf, S):
    T, NF = fl.shape
    ch = _tile(S, 256, 8)

    def body(df_ref, fl_ref, bf_ref, dfl_ref, dbf_ref):
        @pl.when(pl.program_id(0) == 0)
        def _():
            dbf_ref[...] = jnp.zeros_like(dbf_ref)

        row = lax.broadcasted_iota(jnp.int32, (ch, ch), 0)
        col = lax.broadcasted_iota(jnp.int32, (ch, ch), 1)
        tri = (col >= row).astype(BF16)
        carry = jnp.zeros((1, NF), F32)
        for c in range(S - ch, -1, -ch):
            d = df_ref[c:c + ch, :]
            dlf = _tri_dot(tri, d) + carry
            carry = carry + jnp.sum(d, axis=0, keepdims=True)
            dfl = dlf * _sigmoid(-(fl_ref[c:c + ch, :] + bf_ref[...]))
            dfl_ref[c:c + ch, :] = dfl.astype(BF16)
            dbf_ref[0:1, :] += jnp.sum(dfl, axis=0, keepdims=True)

    return pl.pallas_call(
        body, name="forget_bwd", grid=(T // S,),
        in_specs=[pl.BlockSpec((S, NF), lambda b: (b, 0)), pl.BlockSpec((S, NF), lambda b: (b, 0)), _resident((1, NF))],
        out_specs=[pl.BlockSpec((S, NF), lambda b: (b, 0)), pl.BlockSpec((8, NF), lambda b: (0, 0))],
        out_shape=[SDS((T, NF), BF16), SDS((8, NF), F32)],
        compiler_params=_cparams(1),
    )(dfc, fl, bf)


def _inproj_bwd(dparts, offs, w_t, x, g1, dx2, tm, exchange):
    T, D = x.shape
    npart, nx = len(dparts), len(exchange)
    n = T // tm

    def body(*refs):
        d_refs = refs[:npart]
        w_ref, x_ref, g_ref, dx2_ref = refs[npart:npart + 4]
        x_refs = refs[npart + 4:npart + 4 + nx]
        dx_ref, dg_ref = refs[npart + 4 + nx:npart + 6 + nx]
        got_refs, sems = refs[npart + 6 + nx:npart + 6 + 2 * nx], refs[npart + 6 + 2 * nx:]

        @pl.when(pl.program_id(0) == 0)
        def _():
            for cp in _chip_copies(x_refs, got_refs, *sems):
                cp.start()

        @pl.when(pl.program_id(0) == 0)
        def _():
            dg_ref[...] = jnp.zeros_like(dg_ref)

        dh = None
        for d_ref, off in zip(d_refs, offs):
            term = _dot(d_ref[...], w_ref[off:off + d_ref.shape[1], :])
            dh = term if dh is None else dh + term
        xv = x_ref[...]
        r = _rms(xv)
        xn = xv * r
        dg_ref[0:1, :] += jnp.sum(dh * xn, axis=0, keepdims=True)
        dxn = dh * g_ref[...]
        dx_ref[...] = dx2_ref[...] + r * (dxn - xn * jnp.mean(dxn * xn, axis=-1, keepdims=True))

        @pl.when(pl.program_id(0) == n - 1)
        def _():
            for cp in _chip_copies(x_refs, got_refs, *sems):
                cp.wait()

    tok = lambda w: pl.BlockSpec((tm, w), lambda i: (i, 0))
    return pl.pallas_call(
        body, name="inproj_bwd", grid=(n,),
        in_specs=[tok(d.shape[1]) for d in dparts] + [_resident(w_t.shape), tok(D), _resident((1, D)), tok(D)] + [ANY] * nx,
        out_specs=[tok(D), pl.BlockSpec((8, D), lambda i: (0, 0))] + [ANY] * nx,
        out_shape=[SDS((T, D), F32), SDS((8, D), F32)] + _chip_shapes(exchange),
        scratch_shapes=_exchange_sems(3 * nx),
        compiler_params=_cparams(1),
    )(*dparts, w_t, x, g1, dx2, *exchange)


def _wgrad(b, a, name, into=None, row_off=0, total_rows=None):
    T, N = b.shape
    M = a.shape[1]
    tn = _tile(N, 1408 if M <= 1024 else 512)
    while row_off % tn:
        tn = _tile(N, tn - LANES)
    tk = _tile(T, 1024, 16)
    blk0 = row_off // tn

    def body(b_ref, a_ref, *rest):
        o_ref = rest[-1]

        @pl.when(pl.program_id(1) == 0)
        def _():
            o_ref[...] = jnp.zeros_like(o_ref)

        o_ref[...] += _dot_tn(b_ref[...].astype(BF16), a_ref[...].astype(BF16))

    in_specs = [pl.BlockSpec((tk, tn), lambda j, k: (k, j)), pl.BlockSpec((tk, M), lambda j, k: (k, 0))]
    args = (b, a)
    kwargs = {}
    if into is not None:
        in_specs.append(ANY)
        args += (into,)
        kwargs["input_output_aliases"] = {2: 0}
        total_rows = into.shape[0]
    return pl.pallas_call(
        body, name=name, grid=(N // tn, T // tk),
        in_specs=in_specs,
        out_specs=pl.BlockSpec((tn, M), lambda j, k: (blk0 + j, 0)),
        out_shape=SDS((N if total_rows is None else total_rows, M), F32),
        compiler_params=_cparams(2), **kwargs,
    )(*args)


def _adamw(w, g, m, v, name):
    shape = w.shape
    C = shape[-1]
    w2, g2, m2, v2 = (a.reshape(-1, C) for a in (w, g, m, v))
    R = w2.shape[0]
    tr = R if R <= 512 else _tile(R, 256, 8)
    if tr < 64:
        tr = R

    def body(w_ref, g_ref, m_ref, v_ref, d_ref, nm_ref, nv_ref):
        gv = g_ref[...]
        mv = ADAM_B1 * m_ref[...] + (1.0 - ADAM_B1) * gv
        vv = ADAM_B2 * v_ref[...] + (1.0 - ADAM_B2) * (gv * gv)
        m_hat = mv / (1.0 - ADAM_B1 ** ADAM_STEP)
        v_hat = vv / (1.0 - ADAM_B2 ** ADAM_STEP)
        d_ref[...] = -ADAM_LR * (m_hat / (jnp.sqrt(v_hat) + ADAM_EPS) + ADAM_WD * w_ref[...])
        nm_ref[...] = mv
        nv_ref[...] = vv

    spec = pl.BlockSpec((tr, C), lambda r: (r, 0))
    outs = pl.pallas_call(
        body, name=name, grid=(R // tr,),
        in_specs=[spec] * 4, out_specs=[spec] * 3, out_shape=[SDS((R, C), F32)] * 3,
        compiler_params=_cparams(1),
    )(w2, g2, m2, v2)
    return tuple(o.reshape(shape) for o in outs)


def _rows(a, L):
    lead = a.shape[0]
    flat = a.reshape(lead, -1)
    n = flat.shape[1]
    r = -(-n // L)
    return jnp.pad(flat, ((0, 0), (0, r * L - n))).reshape(lead, r, L)


def _unrows(p, shape):
    return p.reshape(-1)[:int(np.prod(shape))].reshape(shape)


def _col_blocks(a):
    R, C = a.shape
    return a.reshape(R, N_DEV, C // N_DEV).transpose(1, 0, 2)


def _from_col_blocks(a):
    n, R, c = a.shape
    return a.transpose(1, 0, 2).reshape(R, n * c)


def _in_windows(n_loc, nqkv, H, NM):
    win = (n_loc + 7 + 7) // 8 * 8
    starts, index = [], np.zeros((N_DEV, n_loc), np.int32)
    for d in range(N_DEV):
        rows = np.arange(n_loc * d, n_loc * (d + 1))
        is_f = (rows >= nqkv) & (rows < nqkv + H)
        kept = np.where(rows < nqkv, rows, rows - H)
        lo = int(kept[~is_f].min())
        start = lo // 8 * 8
        assert int(kept[~is_f].max()) - start < win and start + win <= NM + LANES
        starts.append(start)
        index[d] = np.where(is_f, win + rows - nqkv, kept - start)
    return starts, win, index


def kernel(x, norm_mix_g, w_in, b_f, b_gate, conv_mix_w, w_out_conv, w_out_attn, w_o, norm_ffn_g, w_up, conv_ffn_w, w_down, norm_f_g, loss_target, m_norm_mix_g, m_w_in, m_b_f, m_b_gate, m_conv_mix_w, m_w_out_conv, m_w_out_attn, m_w_o, m_norm_ffn_g, m_w_up, m_conv_ffn_w, m_w_down, m_norm_f_g, v_norm_mix_g, v_w_in, v_b_f, v_b_gate, v_conv_mix_w, v_w_out_conv, v_w_out_attn, v_w_o, v_norm_ffn_g, v_w_up, v_conv_ffn_w, v_w_down, v_norm_f_g):
    Bl, S, D = x.shape
    T = Bl * S
    H = b_f.shape[-1]
    CW = N_DEV * conv_mix_w.shape[-1]
    AW = w_out_attn.shape[1]
    hd = AW // H
    FH = N_DEV * w_down.shape[1]
    n_loc = w_in.shape[-1]
    NIN = N_DEV * n_loc
    NM = 3 * CW + 3 * AW + 2 * D
    nqkv = 3 * CW + 3 * AW
    assert NIN == NM + H and w_out_conv.shape[1] == CW and nqkv % D == 0 and AW % LANES == 0 and LANES % hd == 0
    hpg, ng = LANES // hd, AW // LANES
    assert hpg <= 8
    tm_big = min(512, S // 2)
    tm_ffn = min(256, S // 2)
    tq = min(256, S // 2)
    px, py, pc = _place()
    me = 4 * px + 2 * py + pc

    bits = lambda a: lax.bitcast_convert_type(a, BF16)
    taps = jnp.concatenate([_rows(bits(conv_ffn_w[0])[None], D)[0], _rows(bits(conv_mix_w[0])[None], D)[0]], axis=0)
    n_ffn_rows = -(-conv_ffn_w[0].size * 2 // D)
    late = [w_up[0].T.astype(BF16), w_down[0].astype(BF16), w_o[0].astype(BF16), w_out_conv[0].T.astype(BF16),
            w_out_attn[0].T.astype(BF16), taps]
    n_pad = -(-n_loc // 16) * 16
    g_in, = _all_gather([jnp.pad(w_in[0].T.astype(BF16), ((0, n_pad - n_loc), (0, 0)))], "weights_all_gather")
    W_in_pad = g_in.reshape(N_DEV * n_pad, D)

    def w_in_rows(lo, hi):
        out = []
        while lo < hi:
            d = lo // n_loc
            stop = min(hi, n_loc * (d + 1))
            out.append(W_in_pad[n_pad * d + lo - n_loc * d:n_pad * d + stop - n_loc * d])
            lo = stop
        return out

    W_in_t = jnp.concatenate(w_in_rows(0, nqkv) + w_in_rows(nqkv + H, NIN) + w_in_rows(nqkv, nqkv + H)
                             + [jnp.zeros((LANES - H, D), BF16)], axis=0)
    bf128 = jnp.pad(b_f, ((0, 0), (0, LANES - H)))

    x2d = x.reshape(T, D)
    tgt = loss_target.reshape(T, D)
    pm, fl, h1 = _inproj_fwd(x2d, norm_mix_g, W_in_t, NM, tm_big)
    fcum = _forget_cumsum(fl, bf128, S)
    frow = jnp.pad(fcum[:, :H].reshape(Bl, S, ng, hpg).transpose(0, 2, 3, 1), ((0, 0), (0, 0), (0, 8 - hpg), (0, 0)))
    q_off = 3 * CW // LANES
    o, lse, g_up, g_dn, g_o, g_oc, g_oa, g_taps = _attn_fwd(pm, fcum, frow, S, q_off, AW, hd, tq, late)
    W_up_t = g_up.reshape(2 * FH, D)
    W_dn = g_dn.reshape(FH, D)
    W_o = g_o.reshape(D, D)
    W_oc_t = g_oc.reshape(D, CW)
    W_oa_t = g_oa.reshape(D, AW)
    tap_bits = g_taps.reshape(N_DEV, -1)
    n_ffn, n_mix = conv_ffn_w[0].size * 2, conv_mix_w[0].size * 2
    wf_full = _from_col_blocks(lax.bitcast_convert_type(
        tap_bits[:, :n_ffn].reshape((N_DEV,) + conv_ffn_w.shape[1:] + (2,)), F32))
    wc_full = _from_col_blocks(lax.bitcast_convert_type(
        tap_bits[:, n_ffn_rows * D:n_ffn_rows * D + n_mix].reshape((N_DEV,) + conv_mix_w.shape[1:] + (2,)), F32))
    wf8 = jnp.pad(wf_full, ((0, 5), (0, 0)))
    wc8 = jnp.pad(wc_full, ((0, 5), (0, 0)))
    x2, u, m, ycin, yc, ya = _merge_fwd(pm, o, x2d, wc8, b_gate, W_oc_t, W_oa_t, W_o, S, tm_big)
    upp, up, h2, dx3, loss8, dgf8 = _ffn_fwd(x2, norm_ffn_g, W_up_t, wf8, W_dn, norm_f_g.reshape(1, D), tgt, S, tm_ffn)

    dx2, dupp, act, dwf8, dg2_8 = _ffn_bwd(dx3, x2, norm_ffn_g, upp, up, wf8, W_dn, W_up_t, S, tm_ffn)
    dW_dn = _wgrad(act, dx3, "wgrad_down")
    dW_up_t = _wgrad(dupp, h2, "wgrad_up")
    ids = jnp.stack([pc, 2 * px + py]).astype(jnp.int32)
    big = [dW_up_t.reshape(4, 2, -1, D), dW_dn.reshape(4, 2, -1, D)]
    dconv, dgl, do, dyc, dya, dwc8, dbg8, *sib_big = _merge_bwd(
        dx2, pm, u, yc, ya, wc8, b_gate, W_o, W_oc_t, W_oa_t, S, tm_big, big)
    big_sums = [_pair_sum(b, r, ids, "grads_pair_sum_%d" % a) for a, (b, r) in enumerate(zip(big, sib_big))]
    dW_o = _wgrad(m, dx2, "wgrad_o")
    dW_oc_t = _wgrad(dyc, ycin, "wgrad_out_conv")
    dW_oa_t = _wgrad(dya, o, "wgrad_out_attn")
    dq, dk, dv, dfr, *chips_big = _attn_bwd(pm, do, fcum, frow, lse, S, q_off, AW, hd, tq, [s[1] for s in big_sums])
    dfc = jnp.pad(dfr[:, :, :hpg, :].transpose(0, 3, 1, 2).reshape(T, H), ((0, 0), (0, LANES - H)))
    dfl, dbf8 = _forget_bwd(dfc, fl, bf128, S)
    dparts = [dconv, dq, dk, dv, dgl, dfl]
    offs = [0, 3 * CW, 3 * CW + AW, 3 * CW + 2 * AW, nqkv, NM]
    dW_in_t = _wgrad(dparts[0], h1, "wgrad_in_0", total_rows=NM + LANES)
    for k in range(1, len(dparts)):
        dW_in_t = _wgrad(dparts[k], h1, "wgrad_in_%d" % k, into=dW_in_t, row_off=offs[k])

    starts, win, index = _in_windows(n_loc, nqkv, H, NM)
    small = jnp.concatenate([dW_o.reshape(N_DEV, -1, D), _rows(dW_oc_t.reshape(N_DEV, -1), D), _rows(dW_oa_t.reshape(N_DEV, -1), D),
                             _rows(_col_blocks(dwf8[:3]), D), _rows(_col_blocks(dwc8[:3]), D)], axis=1)
    small = jnp.pad(small, ((0, 0), (0, -small.shape[1] % 8), (0, 0)))
    small = small.reshape(4, 2, -1, D)
    my_starts = [jnp.where(pc == 0, starts[2 * j], starts[2 * j + 1]) for j in range(4)]
    win_ids = jnp.stack([pc, 2 * px + py] + my_starts).astype(jnp.int32)
    sib_win, sib_small = _sibling_exchange(dW_in_t, starts, win, [small], "grads_sibling_exchange")
    sums = [_pair_sum(dW_in_t, sib_win, win_ids, "grads_pair_sum_in", win_rows=win),
            _pair_sum(small, sib_small, ids, "grads_pair_sum_small")]
    grad_x, dg1_8, *from_chips = _inproj_bwd(dparts, offs, W_in_t, x2d, norm_mix_g, dx2, tm_big, [s[1] for s in sums])
    red_win, red_small, red_up, red_dn = [
        _final_sum(s[0], r, "grads_final_sum_%d" % a)
        for a, (s, r) in enumerate(zip(sums + big_sums, list(from_chips) + list(chips_big)))]

    f_rows = dW_in_t[NM:NM + 8]
    smalls = [dg1_8[0:1], dg2_8[0:1], dgf8[0:1], dbg8[0:1, :D], dbg8[0:1, D:], jnp.pad(dbf8[0:1], ((0, 0), (0, D - LANES))),
              jnp.pad(loss8[0:1], ((0, 0), (0, D - LANES))), jnp.zeros((1, D), F32), f_rows]
    spack = jnp.concatenate(smalls, axis=0)
    ssum = _sum8(_all_gather([spack], "small_all_gather")[0], "small_sum")
    g_g1, g_g2, g_gf = ssum[0:1], ssum[1:2], ssum[2]
    g_bg = jnp.concatenate([ssum[3:4], ssum[4:5]], axis=1)
    g_bf = ssum[5:6, :H]

    ext = jnp.concatenate([red_win, ssum[8:8 + H]], axis=0)
    my_index = lax.dynamic_index_in_dim(jnp.asarray(index), me, axis=0, keepdims=False)
    g_w_in_t = jnp.take(ext, my_index, axis=0)
    r_o = dW_o.shape[0] // N_DEV
    r_oc = _rows(dW_oc_t.reshape(N_DEV, -1), D).shape[1]
    r_wf = _rows(_col_blocks(dwf8[:3]), D).shape[1]
    r_wc = _rows(_col_blocks(dwc8[:3]), D).shape[1]
    o0, o1, o2, o3 = r_o, r_o + r_oc, r_o + 2 * r_oc, r_o + 2 * r_oc + r_wf
    g_w_o = red_small[:o0]
    g_w_oc = _unrows(red_small[o0:o1], (D // N_DEV, CW)).T
    g_w_oa = _unrows(red_small[o1:o2], (D // N_DEV, AW)).T
    g_wf = _unrows(red_small[o2:o3], conv_ffn_w.shape[1:])
    g_wc = _unrows(red_small[o3:o3 + r_wc], conv_mix_w.shape[1:])

    loss = ssum[6, 0]

    names = ["norm_mix_g", "w_in", "b_f", "b_gate", "conv_mix_w", "w_out_conv", "w_out_attn", "w_o", "norm_ffn_g", "w_up",
             "conv_ffn_w", "w_down", "norm_f_g"]
    weights = [norm_mix_g, w_in, b_f, b_gate, conv_mix_w, w_out_conv, w_out_attn, w_o, norm_ffn_g, w_up, conv_ffn_w, w_down, norm_f_g]
    grads = [g_g1, g_w_in_t, g_bf, g_bg, g_wc, g_w_oc, g_w_oa, g_w_o, g_g2, red_up, g_wf, red_dn, g_gf]
    ms = [m_norm_mix_g, m_w_in, m_b_f, m_b_gate, m_conv_mix_w, m_w_out_conv, m_w_out_attn, m_w_o, m_norm_ffn_g, m_w_up,
          m_conv_ffn_w, m_w_down, m_norm_f_g]
    vs = [v_norm_mix_g, v_w_in, v_b_f, v_b_gate, v_conv_mix_w, v_w_out_conv, v_w_out_attn, v_w_o, v_norm_ffn_g, v_w_up,
          v_conv_ffn_w, v_w_down, v_norm_f_g]
    to_view = {"w_in": lambda a: a[0].T.reshape(-1, LANES), "w_up": lambda a: a[0].T}
    from_view = {"w_in": lambda a: a.reshape(n_loc, D).T[None], "w_up": lambda a: a.T[None]}
    out_grads, steps = [], []
    for nm, w, g, mm, vv in zip(names, weights, grads, ms, vs):
        if nm in to_view:
            wv, mv, vw = (to_view[nm](a) for a in (w, mm, vv))
            gv = g.reshape(wv.shape)
            steps.append(tuple(from_view[nm](o) for o in _adamw(wv, gv, mv, vw, "adamw_" + nm)))
            out_grads.append(from_view[nm](gv))
        else:
            gv = g.reshape(w.shape)
            steps.append(_adamw(w, gv, mm, vv, "adamw_" + nm))
            out_grads.append(gv)
    deltas, new_ms, new_vs = zip(*steps)
    return (loss, grad_x.reshape(Bl, S, D), *out_grads, *deltas, *new_ms, *new_vs)
```

```python
import numpy as np

import jax
import jax.numpy as jnp
from jax import lax
from jax.experimental import pallas as pl
from jax.experimental.pallas import tpu as pltpu

F32, BF16 = jnp.float32, jnp.bfloat16
EPS = 1e-6
ADAM_LR, ADAM_B1, ADAM_B2, ADAM_EPS, ADAM_WD, ADAM_STEP = 0.001, 0.9, 0.999, 1e-08, 0.01, 10
N_DEV = 8
LANES = 128
V7X_VMEM_LIMIT = 56 * 1024 * 1024
FFN_CHUNK = 2816
MESH = pl.DeviceIdType.MESH
SDS = jax.ShapeDtypeStruct
ANY = pl.BlockSpec(memory_space=pl.ANY)


def _tile(n, target, mult=LANES):
    best = None
    for t in range(mult, min(n, target) + 1, mult):
        if n % t == 0:
            best = t
    return best if best is not None else n


def _resident(shape):
    return pl.BlockSpec(shape, lambda *_: (0,) * len(shape), pipeline_mode=pl.Buffered(1))


def _cparams(n_axes=1):
    return pltpu.CompilerParams(dimension_semantics=("arbitrary",) * n_axes, vmem_limit_bytes=V7X_VMEM_LIMIT)


def _dot(a, b):
    return jnp.dot(a, b, preferred_element_type=F32)


def _dot_tn(a, b):
    return lax.dot_general(a, b, (((0,), (0,)), ((), ())), preferred_element_type=F32)


def _dot_nt(a, b):
    return lax.dot_general(a, b, (((1,), (1,)), ((), ())), preferred_element_type=F32)


def _sigmoid(x):
    return 0.5 * jnp.tanh(0.5 * x) + 0.5


def _rms(x):
    return lax.rsqrt(jnp.mean(x * x, axis=-1, keepdims=True) + EPS)


def _taps_back(z, r6, r7):
    row = lax.broadcasted_iota(jnp.int32, (8, 1), 0)
    z1, z2 = pltpu.roll(z, 1, 0), pltpu.roll(z, 2, 0)
    z1 = jnp.concatenate([jnp.where(row == 0, r7, z1[0:8]), z1[8:]], axis=0)
    z2 = jnp.concatenate([jnp.where(row == 0, r6, jnp.where(row == 1, r7, z2[0:8])), z2[8:]], axis=0)
    return z1, z2


def _taps_ahead(d, h0, h1):
    tm = d.shape[0]
    row = lax.broadcasted_iota(jnp.int32, (8, 1), 0)
    d1, d2 = pltpu.roll(d, tm - 1, 0), pltpu.roll(d, tm - 2, 0)
    d1 = jnp.concatenate([d1[:tm - 8], jnp.where(row == 7, h0, d1[tm - 8:])], axis=0)
    d2 = jnp.concatenate([d2[:tm - 8], jnp.where(row == 6, h0, jnp.where(row == 7, h1, d2[tm - 8:]))], axis=0)
    return d1, d2


def _tri_dot(tri, x):
    hi = x.astype(BF16)
    r = x - hi.astype(F32)
    mid = r.astype(BF16)
    lo = (r - mid.astype(F32)).astype(BF16)
    return (_dot(tri, lo) + _dot(tri, mid)) + _dot(tri, hi)


def _lane_pick(block, lane):
    lanes = lax.broadcasted_iota(jnp.int32, (1, block.shape[1]), 1)
    return jnp.sum(jnp.where(lanes == lane, block, 0.0), axis=1, keepdims=True)


def _place():
    return lax.axis_index("x"), lax.axis_index("y"), lax.axis_index("c")


def _all_gather(xs, name):
    n = len(xs)

    def body(*refs):
        start, forward, finish = _gather_phases(refs[:n], refs[n:2 * n], *refs[2 * n:])
        start()
        forward()
        finish()

    return pl.pallas_call(
        body, name=name,
        out_shape=_gather_shapes(xs), in_specs=[ANY] * n, out_specs=[ANY] * n, scratch_shapes=_gather_sems(n),
    )(*xs)


def _gather_shapes(xs):
    return [SDS((N_DEV,) + x.shape, x.dtype) for x in xs]


def _gather_sems(n):
    return [pltpu.SemaphoreType.DMA((7 * n,)), pltpu.SemaphoreType.DMA((7 * n,)), pltpu.SemaphoreType.DMA((n,))]


def _gather_phases(x_refs, out_refs, send_sems, recv_sems, local_sems):
    n = len(x_refs)

    def parts():
        px, py, pc = _place()
        me, sibling = (px, py, pc), (px, py, 1 - pc)
        chips = [(1 - px, py), (px, 1 - py), (1 - px, 1 - py)]

        def slot(a, qx, qy, qc):
            return out_refs[a].at[4 * qx + 2 * qy + qc]

        def copy(a, k, block, to, src=None):
            return pltpu.make_async_remote_copy(
                src_ref=slot(a, *block) if src is None else src, dst_ref=slot(a, *block),
                send_sem=send_sems.at[7 * a + k], recv_sem=recv_sems.at[7 * a + k], device_id=to, device_id_type=MESH)

        def mine():
            return [pltpu.make_async_copy(x_refs[a], slot(a, *me), local_sems.at[a]) for a in range(n)]

        def first():
            out = []
            for a in range(n):
                out.append(copy(a, 0, me, sibling, src=x_refs[a]))
                out += [copy(a, 1 + j, me, (*chip, pc), src=x_refs[a]) for j, chip in enumerate(chips)]
            return out

        def landed():
            return [copy(a, 1 + j, (*chip, pc), me) for j, chip in enumerate(chips) for a in range(n)]

        def passed():
            return [copy(a, 4 + j, (*chip, pc), sibling) for j, chip in enumerate(chips) for a in range(n)]

        def late():
            out = [copy(a, 0, sibling, me) for a in range(n)]
            return out + [copy(a, 4 + j, (*chip, 1 - pc), me) for j, chip in enumerate(chips) for a in range(n)]

        return mine, first, landed, passed, late

    def start():
        mine, first, _, _, _ = parts()
        for cp in mine() + first():
            cp.start()

    def forward():
        _, _, landed, passed, _ = parts()
        for got, cp in zip(landed(), passed()):
            got.wait_recv()
            cp.start()

    def finish():
        mine, first, _, passed, late = parts()
        for cp in late():
            cp.wait_recv()
        for cp in first() + passed():
            cp.wait_send()
        for cp in mine():
            cp.wait()

    return start, forward, finish


def _sibling_exchange(win_buf, win_starts, win_rows, blocked, name):
    nb = len(blocked)

    def body(*refs):
        win_ref, blk_refs = refs[0], refs[1:1 + nb]
        rwin_ref, rblk_refs = refs[1 + nb], refs[2 + nb:2 + 2 * nb]
        send_sems, recv_sems, wsend_sems, wrecv_sems = refs[2 + 2 * nb:]
        px, py, pc = _place()
        copies = _sibling_copies(blk_refs, rblk_refs, send_sems, recv_sems)
        for j in range(4):
            theirs = jnp.where(pc == 0, win_starts[2 * j + 1], win_starts[2 * j])
            copies.append(pltpu.make_async_remote_copy(
                src_ref=win_ref.at[pl.ds(pl.multiple_of(theirs, 8), win_rows)], dst_ref=rwin_ref.at[j],
                send_sem=wsend_sems.at[j], recv_sem=wrecv_sems.at[j], device_id=(px, py, 1 - pc), device_id_type=MESH))
        for cp in copies:
            cp.start()
        for cp in copies:
            cp.wait()

    C = win_buf.shape[1]
    return pl.pallas_call(
        body, name=name,
        out_shape=[SDS((4, win_rows, C), F32)] + _sibling_shapes(blocked),
        in_specs=[ANY] * (1 + nb), out_specs=[ANY] * (1 + nb),
        scratch_shapes=_exchange_sems(nb) + _exchange_sems(4),
    )(win_buf, *blocked)


def _sibling_shapes(blocked):
    return [SDS((4,) + b.shape[2:], F32) for b in blocked]


def _exchange_sems(n):
    return [pltpu.SemaphoreType.DMA((n,)), pltpu.SemaphoreType.DMA((n,))]


def _sibling_copies(blk_refs, out_refs, send_sems, recv_sems):
    px, py, pc = _place()
    return [pltpu.make_async_remote_copy(
        src_ref=b.at[:, 1 - pc], dst_ref=o, send_sem=send_sems.at[a], recv_sem=recv_sems.at[a],
        device_id=(px, py, 1 - pc), device_id_type=MESH) for a, (b, o) in enumerate(zip(blk_refs, out_refs))]


def _chip_shapes(ps):
    return [SDS((3,) + p.shape[1:], p.dtype) for p in ps]


def _chip_copies(p_refs, out_refs, send_sems, recv_sems):
    px, py, pc = _place()
    n = len(p_refs)
    chips = [(1 - px, py), (px, 1 - py), (1 - px, 1 - py)]
    return [pltpu.make_async_remote_copy(
        src_ref=p_refs[a].at[2 * qx + qy], dst_ref=out_refs[a].at[k],
        send_sem=send_sems.at[3 * a + k], recv_sem=recv_sems.at[3 * a + k],
        device_id=(qx, qy, pc), device_id_type=MESH) for k, (qx, qy) in enumerate(chips) for a in range(n)]


def _pair_sum(own, recv, ids, name, win_rows=None):
    _, R, C = recv.shape
    tr = _tile(R, 512, 8)

    def body(ids_ref, g_ref, r_ref, own_ref, pb_ref):
        s = g_ref[...] + r_ref[...]
        pb_ref[...] = s.astype(BF16)

        @pl.when(pl.program_id(1) == ids_ref[1])
        def _():
            own_ref[...] = s

    if win_rows is None:
        own_spec = pl.BlockSpec((None, None, tr, C), lambda r, j, ids: (j, ids[0], r, 0))
    else:
        own_spec = pl.BlockSpec((pl.Element(tr), pl.Element(C)), lambda r, j, ids: (pl.multiple_of(ids[2 + j] + r * tr, 8), 0))
    return pl.pallas_call(
        body, name=name,
        grid_spec=pltpu.PrefetchScalarGridSpec(
            num_scalar_prefetch=1, grid=(R // tr, 4),
            in_specs=[own_spec, pl.BlockSpec((None, tr, C), lambda r, j, ids: (j, r, 0))],
            out_specs=[pl.BlockSpec((tr, C), lambda r, j, ids: (r, 0)),
                       pl.BlockSpec((None, tr, C), lambda r, j, ids: (j, r, 0))]),
        out_shape=[SDS((R, C), F32), SDS((4, R, C), BF16)],
        compiler_params=_cparams(2),
    )(ids, own, recv)


def _final_sum(own, recv, name):
    R, C = own.shape
    tr = _tile(R, 512, 8)

    def body(o_ref, r_ref, out_ref):
        out_ref[...] = ((o_ref[...] + r_ref[0].astype(F32)) + r_ref[1].astype(F32)) + r_ref[2].astype(F32)

    return pl.pallas_call(
        body, name=name, grid=(R // tr,),
        in_specs=[pl.BlockSpec((tr, C), lambda r: (r, 0)), pl.BlockSpec((3, tr, C), lambda r: (0, r, 0))],
        out_specs=pl.BlockSpec((tr, C), lambda r: (r, 0)),
        out_shape=SDS((R, C), F32),
        compiler_params=_cparams(1),
    )(own, recv)


def _sum8(a, name):
    def body(a_ref, out_ref):
        s = a_ref[0]
        for d in range(1, N_DEV):
            s = s + a_ref[d]
        out_ref[...] = s

    return pl.pallas_call(body, name=name, out_shape=SDS(a.shape[1:], F32))(a)


def _inproj_fwd(x, g1, w_t, NM, tm):
    T, D = x.shape
    NF = w_t.shape[0] - NM
    ch = _tile(NM, 1024)

    def body(x_ref, g_ref, w_ref, pm_ref, fl_ref, h_ref):
        xv = x_ref[...]
        h = (xv * _rms(xv) * g_ref[...]).astype(BF16)
        h_ref[...] = h
        for c in range(0, NM, ch):
            pm_ref[:, c:c + ch] = _dot_nt(h, w_ref[c:c + ch, :]).astype(BF16)
        fl_ref[...] = _dot_nt(h, w_ref[NM:NM + NF, :])

    return pl.pallas_call(
        body, name="inproj_fwd", grid=(T // tm,),
        in_specs=[pl.BlockSpec((tm, D), lambda i: (i, 0)), _resident((1, D)), _resident(w_t.shape)],
        out_specs=[pl.BlockSpec((tm, NM), lambda i: (i, 0)), pl.BlockSpec((tm, NF), lambda i: (i, 0)),
                   pl.BlockSpec((tm, D), lambda i: (i, 0))],
        out_shape=[SDS((T, NM), BF16), SDS((T, NF), F32), SDS((T, D), BF16)],
        compiler_params=_cparams(1),
    )(x, g1, w_t)


def _log_sigmoid(x):
    return jnp.minimum(x, 0.0) - jnp.log(1.0 + jnp.exp(-jnp.abs(x)))


def _forget_cumsum(fl, bf, S):
    T, NF = fl.shape
    ch = _tile(S, 256, 8)

    def body(fl_ref, bf_ref, f_ref):
        row = lax.broadcasted_iota(jnp.int32, (ch, ch), 0)
        col = lax.broadcasted_iota(jnp.int32, (ch, ch), 1)
        tri = (col <= row).astype(BF16)
        carry = jnp.zeros((1, NF), F32)
        for c in range(0, S, ch):
            lf = _log_sigmoid(fl_ref[c:c + ch, :] + bf_ref[...])
            f_ref[c:c + ch, :] = _tri_dot(tri, lf) + carry
            carry = carry + jnp.sum(lf, axis=0, keepdims=True)

    return pl.pallas_call(
        body, name="forget_cumsum", grid=(T // S,),
        in_specs=[pl.BlockSpec((S, NF), lambda b: (b, 0)), _resident((1, NF))],
        out_specs=pl.BlockSpec((S, NF), lambda b: (b, 0)),
        out_shape=SDS((T, NF), F32),
        compiler_params=_cparams(1),
    )(fl, bf)


def _head_mask(e, hd):
    lanes = lax.broadcasted_iota(jnp.int32, (1, LANES), 1)
    return (lanes >= e * hd) & (lanes < (e + 1) * hd)


def _attn_specs(S, q_off, AW):
    ng = AW // LANES
    return [pl.BlockSpec((S, LANES), lambda b, g, o=q_off + w * ng: (b, o + g)) for w in range(3)]


def _attn_fwd(pm, fcum, frow, S, q_off, AW, hd, tq, gather):
    T = pm.shape[0]
    scale = float(hd) ** -0.5
    nq, hpg = S // tq, LANES // hd
    ng, nx = AW // LANES, len(gather)
    steps = (T // S) * ng

    def body(q_ref, k_ref, v_ref, fc_ref, fr_ref, *rest):
        x_refs, (o_ref, lse_ref), out_refs, sems = rest[:nx], rest[nx:nx + 2], rest[nx + 2:2 * nx + 2], rest[2 * nx + 2:]
        g = pl.program_id(1)
        step = pl.program_id(0) * ng + g
        start, forward, finish = _gather_phases(x_refs, out_refs, *sems)
        pl.when(step == 0)(start)

        @pl.when(g == 0)
        def _():
            lse_ref[...] = jnp.zeros_like(lse_ref)

        lanes = lax.broadcasted_iota(jnp.int32, (1, LANES), 1)
        for i in range(nq):
            rs, kend = slice(i * tq, (i + 1) * tq), (i + 1) * tq
            row = i * tq + lax.broadcasted_iota(jnp.int32, (tq, kend), 0)
            col = lax.broadcasted_iota(jnp.int32, (tq, kend), 1)
            o_tile = jnp.zeros((tq, LANES), F32)
            lse_tile = lse_ref[rs, :]
            for e in range(hpg):
                mask = _head_mask(e, hd)
                qs = jnp.where(mask, q_ref[rs, :], 0) * scale
                s = _dot_nt(qs, k_ref[0:kend, :]) + _lane_pick(fc_ref[rs, :], g * hpg + e) - fr_ref[e:e + 1, 0:kend]
                s = jnp.where(col <= row, s, -1e30)
                m = jnp.max(s, axis=1, keepdims=True)
                p = jnp.exp(s - m)
                l = jnp.sum(p, axis=1, keepdims=True)
                o_tile = jnp.where(mask, _dot(p.astype(BF16), v_ref[0:kend, :]) / l, o_tile)
                lse_tile = jnp.where(lanes == g * hpg + e, m + jnp.log(l), lse_tile)
            o_ref[rs, :] = o_tile.astype(BF16)
            lse_ref[rs, :] = lse_tile
        pl.when(step == (steps * 5) // 8)(forward)
        pl.when(step == steps - 1)(finish)

    full = pl.BlockSpec((S, LANES), lambda b, g: (b, 0))
    return pl.pallas_call(
        body, name="attn_fwd", grid=(T // S, ng),
        in_specs=_attn_specs(S, q_off, AW) + [full, pl.BlockSpec((None, None, 8, S), lambda b, g: (b, g, 0, 0))] + [ANY] * nx,
        out_specs=[pl.BlockSpec((S, LANES), lambda b, g: (b, g)), full] + [ANY] * nx,
        out_shape=[SDS((T, AW), BF16), SDS((T, LANES), F32)] + _gather_shapes(gather),
        scratch_shapes=_gather_sems(nx),
        compiler_params=_cparams(2),
    )(pm, pm, pm, fcum, frow, *gather)


def _merge_fwd(pm, o, x, wc, bg, woc_t, woa_t, wo, S, tm):
    T, D = x.shape
    CW, AW = woc_t.shape[1], woa_t.shape[1]
    tps = S // tm

    def body(cb_ref, cc_ref, cin_ref, gc_ref, ga_ref, o_ref, x_ref, wc_ref, bg_ref, woct_ref, woat_ref, wo_ref,
             x2_ref, u_ref, m_ref, ycin_ref, yc_ref, ya_ref, tail_ref):
        @pl.when(pl.program_id(0) % tps == 0)
        def _():
            tail_ref[...] = jnp.zeros_like(tail_ref)

        z = cc_ref[...].astype(F32) * cin_ref[...].astype(F32)
        z1, z2 = _taps_back(z, tail_ref[6:7, :], tail_ref[7:8, :])
        tail_ref[...] = z[tm - 8:tm, :]
        u = (z * wc_ref[2:3, :] + z2 * wc_ref[0:1, :]) + z1 * wc_ref[1:2, :]
        u_ref[...] = u.astype(BF16)
        ycin = (cb_ref[...].astype(F32) * u).astype(BF16)
        ycin_ref[...] = ycin
        yc = _dot_nt(ycin, woct_ref[...])
        ya = _dot_nt(o_ref[...], woat_ref[...])
        yc_ref[...] = yc.astype(BF16)
        ya_ref[...] = ya.astype(BF16)
        gc = _sigmoid(gc_ref[...].astype(F32) + bg_ref[:, 0:D])
        ga = _sigmoid(ga_ref[...].astype(F32) + bg_ref[:, D:2 * D])
        m = (gc * yc + ga * ya).astype(BF16)
        m_ref[...] = m
        x2_ref[...] = x_ref[...] + _dot(m, wo_ref[...])

    g_off = (3 * CW + 3 * AW) // D
    tok = lambda w, j=0: pl.BlockSpec((tm, w), lambda i: (i, j))
    return pl.pallas_call(
        body, name="merge_fwd", grid=(T // tm,),
        in_specs=[tok(CW, 0), tok(CW, 1), tok(CW, 2), tok(D, g_off), tok(D, g_off + 1), tok(AW), tok(D),
                  _resident((8, CW)), _resident((1, 2 * D)), _resident((D, CW)), _resident((D, AW)), _resident((D, D))],
        out_specs=[tok(D), tok(CW), tok(D), tok(CW), tok(D), tok(D)],
        out_shape=[SDS((T, D), F32), SDS((T, CW), BF16), SDS((T, D), BF16), SDS((T, CW), BF16),
                   SDS((T, D), BF16), SDS((T, D), BF16)],
        scratch_shapes=[pltpu.VMEM((8, CW), F32)],
        compiler_params=_cparams(1),
    )(pm, pm, pm, pm, pm, o, x, wc, bg, woc_t, woa_t, wo)


def _ffn_fwd(x2, g2, wup_t, wf, wdn, gf, tgt, S, tm):
    T, D = x2.shape
    FH = wdn.shape[0]
    ch = _tile(FH, FFN_CHUNK)
    tps = S // tm

    def body(x2_ref, g2_ref, wupt_ref, wf_ref, wdn_ref, gf_ref, tgt_ref,
             upp_ref, up_ref, h2_ref, dx3_ref, loss_ref, dgf_ref, tail_ref):
        i = pl.program_id(0)

        @pl.when(i % tps == 0)
        def _():
            tail_ref[...] = jnp.zeros_like(tail_ref)

        @pl.when(i == 0)
        def _():
            loss_ref[...] = jnp.zeros_like(loss_ref)
            dgf_ref[...] = jnp.zeros_like(dgf_ref)

        x2v = x2_ref[...]
        h2 = (x2v * _rms(x2v) * g2_ref[...]).astype(BF16)
        h2_ref[...] = h2
        x3 = x2v
        for c in range(0, FH, ch):
            gated = []
            for cols in (slice(c, c + ch), slice(FH + c, FH + c + ch)):
                upp = _dot_nt(h2, wupt_ref[cols, :])
                upp_ref[:, cols] = upp.astype(BF16)
                p1, p2 = _taps_back(upp, tail_ref[6:7, cols], tail_ref[7:8, cols])
                tail_ref[:, cols] = upp[tm - 8:tm, :]
                up = (upp * wf_ref[2:3, cols] + p2 * wf_ref[0:1, cols]) + p1 * wf_ref[1:2, cols]
                up_ref[:, cols] = up.astype(BF16)
                gated.append(up)
            a, b = gated
            act = (a * _sigmoid(a) * b).astype(BF16)
            x3 = x3 + _dot(act, wdn_ref[c:c + ch, :])
        r3 = _rms(x3)
        xn3 = x3 * r3
        e = xn3 * gf_ref[...] - tgt_ref[...]
        loss_ref[...] += 0.5 * jnp.sum(jnp.mean(e * e, axis=-1, keepdims=True), axis=0, keepdims=True)
        dy = e / D
        dgf_ref[0:1, :] += jnp.sum(dy * xn3, axis=0, keepdims=True)
        dxn = dy * gf_ref[...]
        dx3_ref[...] = r3 * (dxn - xn3 * jnp.mean(dxn * xn3, axis=-1, keepdims=True))

    tok = lambda w: pl.BlockSpec((tm, w), lambda i: (i, 0))
    return pl.pallas_call(
        body, name="ffn_fwd", grid=(T // tm,),
        in_specs=[tok(D), _resident((1, D)), _resident((2 * FH, D)), _resident((8, 2 * FH)), _resident((FH, D)),
                  _resident((1, D)), tok(D)],
        out_specs=[tok(2 * FH), tok(2 * FH), tok(D), tok(D), pl.BlockSpec((8, LANES), lambda i: (0, 0)),
                   pl.BlockSpec((8, D), lambda i: (0, 0))],
        out_shape=[SDS((T, 2 * FH), BF16), SDS((T, 2 * FH), BF16), SDS((T, D), BF16), SDS((T, D), F32),
                   SDS((8, LANES), F32), SDS((8, D), F32)],
        scratch_shapes=[pltpu.VMEM((8, 2 * FH), F32)],
        compiler_params=_cparams(1),
    )(x2, g2, wup_t, wf, wdn, gf, tgt)


def _ffn_bwd(dx3, x2, g2, upp, up, wf, wdn, wup_t, S, tm):
    T, D = x2.shape
    FH = wdn.shape[0]
    ch = _tile(FH, FFN_CHUNK)
    n, tps = T // tm, S // tm

    def body(dx3_ref, x2_ref, g2_ref, upp_ref, up_ref, wf_ref, wdn_ref, wupt_ref,
             dx2_ref, dupp_ref, act_ref, dwf_ref, dg2_ref, head_ref):
        i = pl.program_id(0)
        t = n - 1 - i

        @pl.when(i == 0)
        def _():
            dwf_ref[...] = jnp.zeros_like(dwf_ref)
            dg2_ref[...] = jnp.zeros_like(dg2_ref)

        @pl.when(t % tps == tps - 1)
        def _():
            head_ref[...] = jnp.zeros_like(head_ref)

        dx3v = dx3_ref[...]
        dx3b = dx3v.astype(BF16)
        x2v = x2_ref[...]
        r2 = _rms(x2v)
        xn2 = x2v * r2
        dh2 = jnp.zeros((tm, D), F32)
        for c in range(0, FH, ch):
            ca, cb = slice(c, c + ch), slice(FH + c, FH + c + ch)
            a, b = up_ref[:, ca].astype(F32), up_ref[:, cb].astype(F32)
            sig = _sigmoid(a)
            sl = a * sig
            act_ref[:, ca] = (sl * b).astype(BF16)
            dact = _dot_nt(dx3b, wdn_ref[ca, :])
            grads = (dact * b * (sig * (1.0 + a * (1.0 - sig))), dact * sl)
            for cols, d in zip((ca, cb), grads):
                u0 = upp_ref[:, cols].astype(F32)
                d1, d2 = _taps_ahead(d, head_ref[0:1, cols], head_ref[1:2, cols])
                head_ref[:, cols] = d[0:8, :]
                dwf_ref[2:3, cols] += jnp.sum(u0 * d, axis=0, keepdims=True)
                dwf_ref[1:2, cols] += jnp.sum(u0 * d1, axis=0, keepdims=True)
                dwf_ref[0:1, cols] += jnp.sum(u0 * d2, axis=0, keepdims=True)
                dpre = ((d * wf_ref[2:3, cols] + d1 * wf_ref[1:2, cols]) + d2 * wf_ref[0:1, cols]).astype(BF16)
                dupp_ref[:, cols] = dpre
                dh2 = dh2 + _dot(dpre, wupt_ref[cols, :])
        dg2_ref[0:1, :] += jnp.sum(dh2 * xn2, axis=0, keepdims=True)
        dxn = dh2 * g2_ref[...]
        dx2_ref[...] = dx3v + r2 * (dxn - xn2 * jnp.mean(dxn * xn2, axis=-1, keepdims=True))

    tok = lambda w: pl.BlockSpec((tm, w), lambda i: (n - 1 - i, 0))
    acc = lambda w: pl.BlockSpec((8, w), lambda i: (0, 0))
    return pl.pallas_call(
        body, name="ffn_bwd", grid=(n,),
        in_specs=[tok(D), tok(D), _resident((1, D)), tok(2 * FH), tok(2 * FH), _resident((8, 2 * FH)),
                  _resident((FH, D)), _resident((2 * FH, D))],
        out_specs=[tok(D), tok(2 * FH), tok(FH), acc(2 * FH), acc(D)],
        out_shape=[SDS((T, D), F32), SDS((T, 2 * FH), BF16), SDS((T, FH), BF16), SDS((8, 2 * FH), F32), SDS((8, D), F32)],
        scratch_shapes=[pltpu.VMEM((8, 2 * FH), F32)],
        compiler_params=_cparams(1),
    )(dx3, x2, g2, upp, up, wf, wdn, wup_t)


def _merge_bwd(dx2, pm, u, yc, ya, m, ycin, o, wc, bg, wo, woc_t, woa_t, S, tm, exchange):
    T, D = dx2.shape
    CW, AW = woc_t.shape[1], woa_t.shape[1]
    n, tps = T // tm, S // tm
    nx = len(exchange)

    def body(dx2_ref, cb_ref, cc_ref, cin_ref, gc_ref, ga_ref, u_ref, yc_ref, ya_ref, m_ref, ycin_ref, o_ref,
             wc_ref, bg_ref, wo_ref, woct_ref, woat_ref, *rest):
        x_refs = rest[:nx]
        dconv_ref, dgl_ref, do_ref, dwo_ref, dwoc_ref, dwoa_ref, dwc_ref, dbg_ref = rest[nx:nx + 8]
        got_refs, head_ref, sems = rest[nx + 8:2 * nx + 8], rest[2 * nx + 8], rest[2 * nx + 9:]
        i = pl.program_id(0)
        t = n - 1 - i

        @pl.when(i == 0)
        def _():
            for cp in _sibling_copies(x_refs, got_refs, *sems):
                cp.start()

        @pl.when(i == 0)
        def _():
            for ref in (dwc_ref, dbg_ref, dwo_ref, dwoc_ref, dwoa_ref):
                ref[...] = jnp.zeros_like(ref)

        @pl.when(t % tps == tps - 1)
        def _():
            head_ref[...] = jnp.zeros_like(head_ref)

        dx2b = dx2_ref[...].astype(BF16)
        dwo_ref[...] += _dot_tn(m_ref[...], dx2b)
        dm = _dot_nt(dx2b, wo_ref[...])
        outs = []
        for g_ref, y_ref, cols in ((gc_ref, yc_ref, slice(0, D)), (ga_ref, ya_ref, slice(D, 2 * D))):
            g = _sigmoid(g_ref[...].astype(F32) + bg_ref[:, cols])
            dgl = dm * y_ref[...].astype(F32) * g * (1.0 - g)
            dgl_ref[:, cols] = dgl.astype(BF16)
            dbg_ref[0:1, cols] += jnp.sum(dgl, axis=0, keepdims=True)
            outs.append((dm * g).astype(BF16))
        dyc, dya = outs
        dwoc_ref[...] += _dot_tn(dyc, ycin_ref[...])
        dwoa_ref[...] += _dot_tn(dya, o_ref[...])
        do_ref[...] = _dot(dya, woat_ref[...]).astype(BF16)
        dycin = _dot(dyc, woct_ref[...])
        cc, cin = cc_ref[...].astype(F32), cin_ref[...].astype(F32)
        z = cc * cin
        du = dycin * cb_ref[...].astype(F32)
        du1, du2 = _taps_ahead(du, head_ref[0:1, :], head_ref[1:2, :])
        head_ref[...] = du[0:8, :]
        dwc_ref[2:3, :] += jnp.sum(z * du, axis=0, keepdims=True)
        dwc_ref[1:2, :] += jnp.sum(z * du1, axis=0, keepdims=True)
        dwc_ref[0:1, :] += jnp.sum(z * du2, axis=0, keepdims=True)
        dz = (du * wc_ref[2:3, :] + du1 * wc_ref[1:2, :]) + du2 * wc_ref[0:1, :]
        dconv_ref[:, 0:CW] = (dycin * u_ref[...].astype(F32)).astype(BF16)
        dconv_ref[:, CW:2 * CW] = (dz * cin).astype(BF16)
        dconv_ref[:, 2 * CW:3 * CW] = (dz * cc).astype(BF16)

        @pl.when(i == n - 1)
        def _():
            for cp in _sibling_copies(x_refs, got_refs, *sems):
                cp.wait()

    g_off = (3 * CW + 3 * AW) // D
    tok = lambda w, j=0: pl.BlockSpec((tm, w), lambda i: (n - 1 - i, j))
    acc = lambda w: pl.BlockSpec((8, w), lambda i: (0, 0))
    whole = lambda r, w: pl.BlockSpec((r, w), lambda i: (0, 0))
    return pl.pallas_call(
        body, name="merge_bwd", grid=(n,),
        in_specs=[tok(D), tok(CW, 0), tok(CW, 1), tok(CW, 2), tok(D, g_off), tok(D, g_off + 1), tok(CW), tok(D), tok(D),
                  tok(D), tok(CW), tok(AW),
                  _resident((8, CW)), _resident((1, 2 * D)), _resident((D, D)), _resident((D, CW)), _resident((D, AW))]
                 + [ANY] * nx,
        out_specs=[tok(3 * CW), tok(2 * D), tok(AW), whole(D, D), whole(D, CW), whole(D, AW), acc(CW), acc(2 * D)]
                  + [ANY] * nx,
        out_shape=[SDS((T, 3 * CW), BF16), SDS((T, 2 * D), BF16), SDS((T, AW), BF16), SDS((D, D), F32), SDS((D, CW), F32),
                   SDS((D, AW), F32), SDS((8, CW), F32), SDS((8, 2 * D), F32)] + _sibling_shapes(exchange),
        scratch_shapes=[pltpu.VMEM((8, CW), F32)] + _exchange_sems(nx),
        compiler_params=_cparams(1),
    )(dx2, pm, pm, pm, pm, pm, u, yc, ya, m, ycin, o, wc, bg, wo, woc_t, woa_t, *exchange)


def _attn_bwd(pm, do, fcum, frow, lse, S, q_off, AW, hd, tq, exchange):
    T = pm.shape[0]
    scale = float(hd) ** -0.5
    nq, hpg, ng = S // tq, LANES // hd, AW // LANES
    nx = len(exchange)
    steps = (T // S) * ng

    def body(q_ref, k_ref, v_ref, do_ref, fc_ref, fr_ref, lse_ref, *rest):
        x_refs, (dq_ref, dk_ref, dv_ref, dfr_ref) = rest[:nx], rest[nx:nx + 4]
        got_refs, (dk_acc, dv_acc), sems = rest[nx + 4:2 * nx + 4], rest[2 * nx + 4:2 * nx + 6], rest[2 * nx + 6:]
        g = pl.program_id(1)
        step = pl.program_id(0) * ng + g

        @pl.when(step == 0)
        def _():
            for cp in _chip_copies(x_refs, got_refs, *sems):
                cp.start()

        dk_acc[...] = jnp.zeros_like(dk_acc)
        dv_acc[...] = jnp.zeros_like(dv_acc)
        dfr_ref[...] = jnp.zeros_like(dfr_ref)
        for i in range(nq):
            rs, kend = slice(i * tq, (i + 1) * tq), (i + 1) * tq
            row = i * tq + lax.broadcasted_iota(jnp.int32, (tq, kend), 0)
            col = lax.broadcasted_iota(jnp.int32, (tq, kend), 1)
            kk, vv = k_ref[0:kend, :], v_ref[0:kend, :]
            dq_tile = jnp.zeros((tq, LANES), F32)
            for e in range(hpg):
                mask = _head_mask(e, hd)
                qs = jnp.where(mask, q_ref[rs, :], 0) * scale
                doi = jnp.where(mask, do_ref[rs, :], 0)
                s = _dot_nt(qs, kk) + _lane_pick(fc_ref[rs, :], g * hpg + e) - fr_ref[e:e + 1, 0:kend]
                p = jnp.where(col <= row, jnp.exp(s - _lane_pick(lse_ref[rs, :], g * hpg + e)), 0.0)
                dp = _dot_nt(doi, vv)
                ds = p * (dp - jnp.sum(p * dp, axis=1, keepdims=True))
                pb, dsb = p.astype(BF16), ds.astype(BF16)
                dq_tile = jnp.where(mask, _dot(dsb, kk) * scale, dq_tile)
                dv_acc[0:kend, :] += _dot_tn(pb, doi)
                dk_acc[0:kend, :] += _dot_tn(dsb, qs)
                dfr_ref[e:e + 1, 0:kend] -= jnp.sum(ds, axis=0, keepdims=True)
            dq_ref[rs, :] = dq_tile.astype(BF16)
        dk_ref[...] = dk_acc[...].astype(BF16)
        dv_ref[...] = dv_acc[...].astype(BF16)

        @pl.when(step == steps - 1)
        def _():
            for cp in _chip_copies(x_refs, got_refs, *sems):
                cp.wait()

    full = pl.BlockSpec((S, LANES), lambda b, g: (b, 0))
    grp = pl.BlockSpec((S, LANES), lambda b, g: (b, g))
    rows = pl.BlockSpec((None, None, 8, S), lambda b, g: (b, g, 0, 0))
    return pl.pallas_call(
        body, name="attn_bwd", grid=(T // S, ng),
        in_specs=_attn_specs(S, q_off, AW) + [grp, full, rows, full] + [ANY] * nx,
        out_specs=[grp, grp, grp, rows] + [ANY] * nx,
        out_shape=[SDS((T, AW), BF16), SDS((T, AW), BF16), SDS((T, AW), BF16), SDS((T // S, ng, 8, S), F32)]
                  + _chip_shapes(exchange),
        scratch_shapes=[pltpu.VMEM((S, LANES), F32), pltpu.VMEM((S, LANES), F32)] + _exchange_sems(3 * nx),
        compiler_params=_cparams(2),
    )(pm, pm, pm, do, fcum, frow, lse, *exchange)


def _forget_bwd(dfc, fl, bf, S):
    T, NF = fl.shape
    ch = _tile(S, 256, 8)

    def body(df_ref, fl_ref, bf_ref, dfl_ref, dbf_ref):
        @pl.when(pl.program_id(0) == 0)
        def _():
            dbf_ref[...] = jnp.zeros_like(dbf_ref)

        row = lax.broadcasted_iota(jnp.int32, (ch, ch), 0)
        col = lax.broadcasted_iota(jnp.int32, (ch, ch), 1)
        tri = (col >= row).astype(BF16)
        carry = jnp.zeros((1, NF), F32)
        for c in range(S - ch, -1, -ch):
            d = df_ref[c:c + ch, :]
            dlf = _tri_dot(tri, d) + carry
            carry = carry + jnp.sum(d, axis=0, keepdims=True)
            dfl = dlf * _sigmoid(-(fl_ref[c:c + ch, :] + bf_ref[...]))
            dfl_ref[c:c + ch, :] = dfl.astype(BF16)
            dbf_ref[0:1, :] += jnp.sum(dfl, axis=0, keepdims=True)

    return pl.pallas_call(
        body, name="forget_bwd", grid=(T // S,),
        in_specs=[pl.BlockSpec((S, NF), lambda b: (b, 0)), pl.BlockSpec((S, NF), lambda b: (b, 0)), _resident((1, NF))],
        out_specs=[pl.BlockSpec((S, NF), lambda b: (b, 0)), pl.BlockSpec((8, NF), lambda b: (0, 0))],
        out_shape=[SDS((T, NF), BF16), SDS((8, NF), F32)],
        compiler_params=_cparams(1),
    )(dfc, fl, bf)


def _inproj_bwd(dparts, offs, w_t, x, g1, dx2, tm, exchange):
    T, D = x.shape
    npart, nx = len(dparts), len(exchange)
    n = T // tm

    def body(*refs):
        d_refs = refs[:npart]
        w_ref, x_ref, g_ref, dx2_ref = refs[npart:npart + 4]
        x_refs = refs[npart + 4:npart + 4 + nx]
        dx_ref, dg_ref = refs[npart + 4 + nx:npart + 6 + nx]
        got_refs, sems = refs[npart + 6 + nx:npart + 6 + 2 * nx], refs[npart + 6 + 2 * nx:]

        @pl.when(pl.program_id(0) == 0)
        def _():
            for cp in _chip_copies(x_refs, got_refs, *sems):
                cp.start()

        @pl.when(pl.program_id(0) == 0)
        def _():
            dg_ref[...] = jnp.zeros_like(dg_ref)

        dh = None
        for d_ref, off in zip(d_refs, offs):
            term = _dot(d_ref[...], w_ref[off:off + d_ref.shape[1], :])
            dh = term if dh is None else dh + term
        xv = x_ref[...]
        r = _rms(xv)
        xn = xv * r
        dg_ref[0:1, :] += jnp.sum(dh * xn, axis=0, keepdims=True)
        dxn = dh * g_ref[...]
        dx_ref[...] = dx2_ref[...] + r * (dxn - xn * jnp.mean(dxn * xn, axis=-1, keepdims=True))

        @pl.when(pl.program_id(0) == n - 1)
        def _():
            for cp in _chip_copies(x_refs, got_refs, *sems):
                cp.wait()

    tok = lambda w: pl.BlockSpec((tm, w), lambda i: (i, 0))
    return pl.pallas_call(
        body, name="inproj_bwd", grid=(n,),
        in_specs=[tok(d.shape[1]) for d in dparts] + [_resident(w_t.shape), tok(D), _resident((1, D)), tok(D)] + [ANY] * nx,
        out_specs=[tok(D), pl.BlockSpec((8, D), lambda i: (0, 0))] + [ANY] * nx,
        out_shape=[SDS((T, D), F32), SDS((8, D), F32)] + _chip_shapes(exchange),
        scratch_shapes=_exchange_sems(3 * nx),
        compiler_params=_cparams(1),
    )(*dparts, w_t, x, g1, dx2, *exchange)


def _wgrad(b, a, name, into=None, row_off=0, total_rows=None):
    T, N = b.shape
    M = a.shape[1]
    tn = _tile(N, 1408 if M <= 1024 else 512)
    while row_off % tn:
        tn = _tile(N, tn - LANES)
    tk = _tile(T, 1024, 16)
    blk0 = row_off // tn

    def body(b_ref, a_ref, *rest):
        o_ref = rest[-1]

        @pl.when(pl.program_id(1) == 0)
        def _():
            o_ref[...] = jnp.zeros_like(o_ref)

        o_ref[...] += _dot_tn(b_ref[...].astype(BF16), a_ref[...].astype(BF16))

    in_specs = [pl.BlockSpec((tk, tn), lambda j, k: (k, j)), pl.BlockSpec((tk, M), lambda j, k: (k, 0))]
    args = (b, a)
    kwargs = {}
    if into is not None:
        in_specs.append(ANY)
        args += (into,)
        kwargs["input_output_aliases"] = {2: 0}
        total_rows = into.shape[0]
    return pl.pallas_call(
        body, name=name, grid=(N // tn, T // tk),
        in_specs=in_specs,
        out_specs=pl.BlockSpec((tn, M), lambda j, k: (blk0 + j, 0)),
        out_shape=SDS((N if total_rows is None else total_rows, M), F32),
        compiler_params=_cparams(2), **kwargs,
    )(*args)


def _adamw(w, g, m, v, name):
    shape = w.shape
    C = shape[-1]
    w2, g2, m2, v2 = (a.reshape(-1, C) for a in (w, g, m, v))
    R = w2.shape[0]
    tr = R if R <= 512 else _tile(R, 256, 8)
    if tr < 64:
        tr = R

    def body(w_ref, g_ref, m_ref, v_ref, d_ref, nm_ref, nv_ref):
        gv = g_ref[...]
        mv = ADAM_B1 * m_ref[...] + (1.0 - ADAM_B1) * gv
        vv = ADAM_B2 * v_ref[...] + (1.0 - ADAM_B2) * (gv * gv)
        m_hat = mv / (1.0 - ADAM_B1 ** ADAM_STEP)
        v_hat = vv / (1.0 - ADAM_B2 ** ADAM_STEP)
        d_ref[...] = -ADAM_LR * (m_hat / (jnp.sqrt(v_hat) + ADAM_EPS) + ADAM_WD * w_ref[...])
        nm_ref[...] = mv
        nv_ref[...] = vv

    spec = pl.BlockSpec((tr, C), lambda r: (r, 0))
    outs = pl.pallas_call(
        body, name=name, grid=(R // tr,),
        in_specs=[spec] * 4, out_specs=[spec] * 3, out_shape=[SDS((R, C), F32)] * 3,
        compiler_params=_cparams(1),
    )(w2, g2, m2, v2)
    return tuple(o.reshape(shape) for o in outs)


def _rows(a, L):
    lead = a.shape[0]
    flat = a.reshape(lead, -1)
    n = flat.shape[1]
    r = -(-n // L)
    return jnp.pad(flat, ((0, 0), (0, r * L - n))).reshape(lead, r, L)


def _unrows(p, shape):
    return p.reshape(-1)[:int(np.prod(shape))].reshape(shape)


def _col_blocks(a):
    R, C = a.shape
    return a.reshape(R, N_DEV, C // N_DEV).transpose(1, 0, 2)


def _from_col_blocks(a):
    n, R, c = a.shape
    return a.transpose(1, 0, 2).reshape(R, n * c)


def _in_windows(n_loc, nqkv, H, NM):
    win = (n_loc + 7 + 7) // 8 * 8
    starts, index = [], np.zeros((N_DEV, n_loc), np.int32)
    for d in range(N_DEV):
        rows = np.arange(n_loc * d, n_loc * (d + 1))
        is_f = (rows >= nqkv) & (rows < nqkv + H)
        kept = np.where(rows < nqkv, rows, rows - H)
        lo = int(kept[~is_f].min())
        start = lo // 8 * 8
        assert int(kept[~is_f].max()) - start < win and start + win <= NM + LANES
        starts.append(start)
        index[d] = np.where(is_f, win + rows - nqkv, kept - start)
    return starts, win, index


def kernel(x, norm_mix_g, w_in, b_f, b_gate, conv_mix_w, w_out_conv, w_out_attn, w_o, norm_ffn_g, w_up, conv_ffn_w, w_down, norm_f_g, loss_target, m_norm_mix_g, m_w_in, m_b_f, m_b_gate, m_conv_mix_w, m_w_out_conv, m_w_out_attn, m_w_o, m_norm_ffn_g, m_w_up, m_conv_ffn_w, m_w_down, m_norm_f_g, v_norm_mix_g, v_w_in, v_b_f, v_b_gate, v_conv_mix_w, v_w_out_conv, v_w_out_attn, v_w_o, v_norm_ffn_g, v_w_up, v_conv_ffn_w, v_w_down, v_norm_f_g):
    Bl, S, D = x.shape
    T = Bl * S
    H = b_f.shape[-1]
    CW = N_DEV * conv_mix_w.shape[-1]
    AW = w_out_attn.shape[1]
    hd = AW // H
    FH = N_DEV * w_down.shape[1]
    n_loc = w_in.shape[-1]
    NIN = N_DEV * n_loc
    NM = 3 * CW + 3 * AW + 2 * D
    nqkv = 3 * CW + 3 * AW
    assert NIN == NM + H and w_out_conv.shape[1] == CW and nqkv % D == 0 and AW % LANES == 0 and LANES % hd == 0
    hpg, ng = LANES // hd, AW // LANES
    assert hpg <= 8
    tm_big = min(512, S // 2)
    tm_ffn = min(256, S // 2)
    tq = min(256, S // 2)
    px, py, pc = _place()
    me = 4 * px + 2 * py + pc

    bits = lambda a: lax.bitcast_convert_type(a, BF16)
    taps = jnp.concatenate([_rows(bits(conv_ffn_w[0])[None], D)[0], _rows(bits(conv_mix_w[0])[None], D)[0]], axis=0)
    n_ffn_rows = -(-conv_ffn_w[0].size * 2 // D)
    late = [w_up[0].T.astype(BF16), w_down[0].astype(BF16), w_o[0].astype(BF16), w_out_conv[0].T.astype(BF16),
            w_out_attn[0].T.astype(BF16), taps]
    g_in, = _all_gather([w_in[0].T.astype(BF16)], "weights_all_gather")
    W_in_rows = g_in.reshape(NIN, D)
    W_in_t = jnp.concatenate([W_in_rows[:nqkv], W_in_rows[nqkv + H:], W_in_rows[nqkv:nqkv + H],
                              jnp.zeros((LANES - H, D), BF16)], axis=0)
    bf128 = jnp.pad(b_f, ((0, 0), (0, LANES - H)))

    x2d = x.reshape(T, D)
    tgt = loss_target.reshape(T, D)
    pm, fl, h1 = _inproj_fwd(x2d, norm_mix_g, W_in_t, NM, tm_big)
    fcum = _forget_cumsum(fl, bf128, S)
    frow = jnp.pad(fcum[:, :H].reshape(Bl, S, ng, hpg).transpose(0, 2, 3, 1), ((0, 0), (0, 0), (0, 8 - hpg), (0, 0)))
    q_off = 3 * CW // LANES
    o, lse, g_up, g_dn, g_o, g_oc, g_oa, g_taps = _attn_fwd(pm, fcum, frow, S, q_off, AW, hd, tq, late)
    W_up_t = g_up.reshape(2 * FH, D)
    W_dn = g_dn.reshape(FH, D)
    W_o = g_o.reshape(D, D)
    W_oc_t = g_oc.reshape(D, CW)
    W_oa_t = g_oa.reshape(D, AW)
    tap_bits = g_taps.reshape(N_DEV, -1)
    n_ffn, n_mix = conv_ffn_w[0].size * 2, conv_mix_w[0].size * 2
    wf_full = _from_col_blocks(lax.bitcast_convert_type(
        tap_bits[:, :n_ffn].reshape((N_DEV,) + conv_ffn_w.shape[1:] + (2,)), F32))
    wc_full = _from_col_blocks(lax.bitcast_convert_type(
        tap_bits[:, n_ffn_rows * D:n_ffn_rows * D + n_mix].reshape((N_DEV,) + conv_mix_w.shape[1:] + (2,)), F32))
    wf8 = jnp.pad(wf_full, ((0, 5), (0, 0)))
    wc8 = jnp.pad(wc_full, ((0, 5), (0, 0)))
    x2, u, m, ycin, yc, ya = _merge_fwd(pm, o, x2d, wc8, b_gate, W_oc_t, W_oa_t, W_o, S, tm_big)
    upp, up, h2, dx3, loss8, dgf8 = _ffn_fwd(x2, norm_ffn_g, W_up_t, wf8, W_dn, norm_f_g.reshape(1, D), tgt, S, tm_ffn)

    dx2, dupp, act, dwf8, dg2_8 = _ffn_bwd(dx3, x2, norm_ffn_g, upp, up, wf8, W_dn, W_up_t, S, tm_ffn)
    dW_dn = _wgrad(act, dx3, "wgrad_down")
    dW_up_t = _wgrad(dupp, h2, "wgrad_up")
    ids = jnp.stack([pc, 2 * px + py]).astype(jnp.int32)
    big = [dW_up_t.reshape(4, 2, -1, D), dW_dn.reshape(4, 2, -1, D)]
    dconv, dgl, do, dW_o, dW_oc_t, dW_oa_t, dwc8, dbg8, *sib_big = _merge_bwd(
        dx2, pm, u, yc, ya, m, ycin, o, wc8, b_gate, W_o, W_oc_t, W_oa_t, S, tm_big, big)
    big_sums = [_pair_sum(b, r, ids, "grads_pair_sum_%d" % a) for a, (b, r) in enumerate(zip(big, sib_big))]
    dq, dk, dv, dfr, *chips_big = _attn_bwd(pm, do, fcum, frow, lse, S, q_off, AW, hd, tq, [s[1] for s in big_sums])
    dfc = jnp.pad(dfr[:, :, :hpg, :].transpose(0, 3, 1, 2).reshape(T, H), ((0, 0), (0, LANES - H)))
    dfl, dbf8 = _forget_bwd(dfc, fl, bf128, S)
    dparts = [dconv, dq, dk, dv, dgl, dfl]
    offs = [0, 3 * CW, 3 * CW + AW, 3 * CW + 2 * AW, nqkv, NM]
    dW_in_t = _wgrad(dparts[0], h1, "wgrad_in_0", total_rows=NM + LANES)
    for k in range(1, len(dparts)):
        dW_in_t = _wgrad(dparts[k], h1, "wgrad_in_%d" % k, into=dW_in_t, row_off=offs[k])

    starts, win, index = _in_windows(n_loc, nqkv, H, NM)
    small = jnp.concatenate([dW_o.reshape(N_DEV, -1, D), _rows(dW_oc_t.reshape(N_DEV, -1), D), _rows(dW_oa_t.reshape(N_DEV, -1), D),
                             _rows(_col_blocks(dwf8[:3]), D), _rows(_col_blocks(dwc8[:3]), D)], axis=1)
    small = jnp.pad(small, ((0, 0), (0, -small.shape[1] % 8), (0, 0)))
    small = small.reshape(4, 2, -1, D)
    my_starts = [jnp.where(pc == 0, starts[2 * j], starts[2 * j + 1]) for j in range(4)]
    win_ids = jnp.stack([pc, 2 * px + py] + my_starts).astype(jnp.int32)
    sib_win, sib_small = _sibling_exchange(dW_in_t, starts, win, [small], "grads_sibling_exchange")
    sums = [_pair_sum(dW_in_t, sib_win, win_ids, "grads_pair_sum_in", win_rows=win),
            _pair_sum(small, sib_small, ids, "grads_pair_sum_small")]
    grad_x, dg1_8, *from_chips = _inproj_bwd(dparts, offs, W_in_t, x2d, norm_mix_g, dx2, tm_big, [s[1] for s in sums])
    red_win, red_small, red_up, red_dn = [
        _final_sum(s[0], r, "grads_final_sum_%d" % a)
        for a, (s, r) in enumerate(zip(sums + big_sums, list(from_chips) + list(chips_big)))]

    f_rows = dW_in_t[NM:NM + 8]
    smalls = [dg1_8[0:1], dg2_8[0:1], dgf8[0:1], dbg8[0:1, :D], dbg8[0:1, D:], jnp.pad(dbf8[0:1], ((0, 0), (0, D - LANES))),
              jnp.pad(loss8[0:1], ((0, 0), (0, D - LANES))), jnp.zeros((1, D), F32), f_rows]
    spack = jnp.concatenate(smalls, axis=0)
    ssum = _sum8(_all_gather([spack], "small_all_gather")[0], "small_sum")
    g_g1, g_g2, g_gf = ssum[0:1], ssum[1:2], ssum[2]
    g_bg = jnp.concatenate([ssum[3:4], ssum[4:5]], axis=1)
    g_bf = ssum[5:6, :H]

    ext = jnp.concatenate([red_win, ssum[8:8 + H]], axis=0)
    my_index = lax.dynamic_index_in_dim(jnp.asarray(index), me, axis=0, keepdims=False)
    g_w_in_t = jnp.take(ext, my_index, axis=0)
    r_o = dW_o.shape[0] // N_DEV
    r_oc = _rows(dW_oc_t.reshape(N_DEV, -1), D).shape[1]
    r_wf = _rows(_col_blocks(dwf8[:3]), D).shape[1]
    r_wc = _rows(_col_blocks(dwc8[:3]), D).shape[1]
    o0, o1, o2, o3 = r_o, r_o + r_oc, r_o + 2 * r_oc, r_o + 2 * r_oc + r_wf
    g_w_o = red_small[:o0]
    g_w_oc = _unrows(red_small[o0:o1], (D // N_DEV, CW)).T
    g_w_oa = _unrows(red_small[o1:o2], (D // N_DEV, AW)).T
    g_wf = _unrows(red_small[o2:o3], conv_ffn_w.shape[1:])
    g_wc = _unrows(red_small[o3:o3 + r_wc], conv_mix_w.shape[1:])

    loss = ssum[6, 0]

    names = ["norm_mix_g", "w_in", "b_f", "b_gate", "conv_mix_w", "w_out_conv", "w_out_attn", "w_o", "norm_ffn_g", "w_up",
             "conv_ffn_w", "w_down", "norm_f_g"]
    weights = [norm_mix_g, w_in, b_f, b_gate, conv_mix_w, w_out_conv, w_out_attn, w_o, norm_ffn_g, w_up, conv_ffn_w, w_down, norm_f_g]
    grads = [g_g1, g_w_in_t, g_bf, g_bg, g_wc, g_w_oc, g_w_oa, g_w_o, g_g2, red_up, g_wf, red_dn, g_gf]
    ms = [m_norm_mix_g, m_w_in, m_b_f, m_b_gate, m_conv_mix_w, m_w_out_conv, m_w_out_attn, m_w_o, m_norm_ffn_g, m_w_up,
          m_conv_ffn_w, m_w_down, m_norm_f_g]
    vs = [v_norm_mix_g, v_w_in, v_b_f, v_b_gate, v_conv_mix_w, v_w_out_conv, v_w_out_attn, v_w_o, v_norm_ffn_g, v_w_up,
          v_conv_ffn_w, v_w_down, v_norm_f_g]
    to_view = {"w_in": lambda a: a[0].T.reshape(-1, LANES), "w_up": lambda a: a[0].T}
    from_view = {"w_in": lambda a: a.reshape(n_loc, D).T[None], "w_up": lambda a: a.T[None]}
    out_grads, steps = [], []
    for nm, w, g, mm, vv in zip(names, weights, grads, ms, vs):
        if nm in to_view:
            wv, mv, vw = (to_view[nm](a) for a in (w, mm, vv))
            gv = g.reshape(wv.shape)
            steps.append(tuple(from_view[nm](o) for o in _adamw(wv, gv, mv, vw, "adamw_" + nm)))
            out_grads.append(from_view[nm](gv))
        else:
            gv = g.reshape(w.shape)
            steps.append(_adamw(w, gv, mm, vv, "adamw_" + nm))
            out_grads.append(gv)
    deltas, new_ms, new_vs = zip(*steps)
    return (loss, grad_x.reshape(Bl, S, D), *out_grads, *deltas, *new_ms, *new_vs)
```

```python
import numpy as np

import jax
import jax.numpy as jnp
from jax import lax
from jax.experimental import pallas as pl
from jax.experimental.pallas import tpu as pltpu

F32, BF16 = jnp.float32, jnp.bfloat16
EPS = 1e-6
ADAM_LR, ADAM_B1, ADAM_B2, ADAM_EPS, ADAM_WD, ADAM_STEP = 0.001, 0.9, 0.999, 1e-08, 0.01, 10
N_DEV = 8
LANES = 128
V7X_VMEM_LIMIT = 56 * 1024 * 1024
FFN_CHUNK = 2816
MESH = pl.DeviceIdType.MESH
SDS = jax.ShapeDtypeStruct
ANY = pl.BlockSpec(memory_space=pl.ANY)


def _tile(n, target, mult=LANES):
    best = None
    for t in range(mult, min(n, target) + 1, mult):
        if n % t == 0:
            best = t
    return best if best is not None else n


def _resident(shape):
    return pl.BlockSpec(shape, lambda *_: (0,) * len(shape), pipeline_mode=pl.Buffered(1))


def _cparams(n_axes=1):
    return pltpu.CompilerParams(dimension_semantics=("arbitrary",) * n_axes, vmem_limit_bytes=V7X_VMEM_LIMIT)


def _dot(a, b):
    return jnp.dot(a, b, preferred_element_type=F32)


def _dot_tn(a, b):
    return lax.dot_general(a, b, (((0,), (0,)), ((), ())), preferred_element_type=F32)


def _dot_nt(a, b):
    return lax.dot_general(a, b, (((1,), (1,)), ((), ())), preferred_element_type=F32)


def _sigmoid(x):
    return 0.5 * jnp.tanh(0.5 * x) + 0.5


def _rms(x):
    return lax.rsqrt(jnp.mean(x * x, axis=-1, keepdims=True) + EPS)


def _taps_back(z, r6, r7):
    row = lax.broadcasted_iota(jnp.int32, (8, 1), 0)
    z1, z2 = pltpu.roll(z, 1, 0), pltpu.roll(z, 2, 0)
    z1 = jnp.concatenate([jnp.where(row == 0, r7, z1[0:8]), z1[8:]], axis=0)
    z2 = jnp.concatenate([jnp.where(row == 0, r6, jnp.where(row == 1, r7, z2[0:8])), z2[8:]], axis=0)
    return z1, z2


def _taps_ahead(d, h0, h1):
    tm = d.shape[0]
    row = lax.broadcasted_iota(jnp.int32, (8, 1), 0)
    d1, d2 = pltpu.roll(d, tm - 1, 0), pltpu.roll(d, tm - 2, 0)
    d1 = jnp.concatenate([d1[:tm - 8], jnp.where(row == 7, h0, d1[tm - 8:])], axis=0)
    d2 = jnp.concatenate([d2[:tm - 8], jnp.where(row == 6, h0, jnp.where(row == 7, h1, d2[tm - 8:]))], axis=0)
    return d1, d2


def _tri_dot(tri, x):
    hi = x.astype(BF16)
    r = x - hi.astype(F32)
    mid = r.astype(BF16)
    lo = (r - mid.astype(F32)).astype(BF16)
    return (_dot(tri, lo) + _dot(tri, mid)) + _dot(tri, hi)


def _lane_pick(block, lane):
    lanes = lax.broadcasted_iota(jnp.int32, (1, block.shape[1]), 1)
    return jnp.sum(jnp.where(lanes == lane, block, 0.0), axis=1, keepdims=True)


def _place():
    return lax.axis_index("x"), lax.axis_index("y"), lax.axis_index("c")


def _all_gather(xs, name):
    n = len(xs)

    def body(*refs):
        start, forward, finish = _gather_phases(refs[:n], refs[n:2 * n], *refs[2 * n:])
        start()
        forward()
        finish()

    return pl.pallas_call(
        body, name=name,
        out_shape=_gather_shapes(xs), in_specs=[ANY] * n, out_specs=[ANY] * n, scratch_shapes=_gather_sems(n),
    )(*xs)


def _gather_shapes(xs):
    return [SDS((N_DEV,) + x.shape, x.dtype) for x in xs]


def _gather_sems(n):
    return [pltpu.SemaphoreType.DMA((7 * n,)), pltpu.SemaphoreType.DMA((7 * n,)), pltpu.SemaphoreType.DMA((n,))]


def _gather_phases(x_refs, out_refs, send_sems, recv_sems, local_sems):
    n = len(x_refs)

    def parts():
        px, py, pc = _place()
        me, sibling = (px, py, pc), (px, py, 1 - pc)
        chips = [(1 - px, py), (px, 1 - py), (1 - px, 1 - py)]

        def slot(a, qx, qy, qc):
            return out_refs[a].at[4 * qx + 2 * qy + qc]

        def copy(a, k, block, to, src=None):
            return pltpu.make_async_remote_copy(
                src_ref=slot(a, *block) if src is None else src, dst_ref=slot(a, *block),
                send_sem=send_sems.at[7 * a + k], recv_sem=recv_sems.at[7 * a + k], device_id=to, device_id_type=MESH)

        def mine():
            return [pltpu.make_async_copy(x_refs[a], slot(a, *me), local_sems.at[a]) for a in range(n)]

        def first():
            out = []
            for a in range(n):
                out.append(copy(a, 0, me, sibling, src=x_refs[a]))
                out += [copy(a, 1 + j, me, (*chip, pc), src=x_refs[a]) for j, chip in enumerate(chips)]
            return out

        def landed():
            return [copy(a, 1 + j, (*chip, pc), me) for j, chip in enumerate(chips) for a in range(n)]

        def passed():
            return [copy(a, 4 + j, (*chip, pc), sibling) for j, chip in enumerate(chips) for a in range(n)]

        def late():
            out = [copy(a, 0, sibling, me) for a in range(n)]
            return out + [copy(a, 4 + j, (*chip, 1 - pc), me) for j, chip in enumerate(chips) for a in range(n)]

        return mine, first, landed, passed, late

    def start():
        mine, first, _, _, _ = parts()
        for cp in mine() + first():
            cp.start()

    def forward():
        _, _, landed, passed, _ = parts()
        for got, cp in zip(landed(), passed()):
            got.wait_recv()
            cp.start()

    def finish():
        mine, first, _, passed, late = parts()
        for cp in late():
            cp.wait_recv()
        for cp in first() + passed():
            cp.wait_send()
        for cp in mine():
            cp.wait()

    return start, forward, finish


def _sibling_exchange(win_buf, win_starts, win_rows, blocked, name):
    nb = len(blocked)

    def body(*refs):
        win_ref, blk_refs = refs[0], refs[1:1 + nb]
        rwin_ref, rblk_refs = refs[1 + nb], refs[2 + nb:2 + 2 * nb]
        send_sems, recv_sems, wsend_sems, wrecv_sems = refs[2 + 2 * nb:]
        px, py, pc = _place()
        copies = _sibling_copies(blk_refs, rblk_refs, send_sems, recv_sems)
        for j in range(4):
            theirs = jnp.where(pc == 0, win_starts[2 * j + 1], win_starts[2 * j])
            copies.append(pltpu.make_async_remote_copy(
                src_ref=win_ref.at[pl.ds(pl.multiple_of(theirs, 8), win_rows)], dst_ref=rwin_ref.at[j],
                send_sem=wsend_sems.at[j], recv_sem=wrecv_sems.at[j], device_id=(px, py, 1 - pc), device_id_type=MESH))
        for cp in copies:
            cp.start()
        for cp in copies:
            cp.wait()

    C = win_buf.shape[1]
    return pl.pallas_call(
        body, name=name,
        out_shape=[SDS((4, win_rows, C), F32)] + _sibling_shapes(blocked),
        in_specs=[ANY] * (1 + nb), out_specs=[ANY] * (1 + nb),
        scratch_shapes=_exchange_sems(nb) + _exchange_sems(4),
    )(win_buf, *blocked)


def _sibling_shapes(blocked):
    return [SDS((4,) + b.shape[2:], F32) for b in blocked]


def _exchange_sems(n):
    return [pltpu.SemaphoreType.DMA((n,)), pltpu.SemaphoreType.DMA((n,))]


def _sibling_copies(blk_refs, out_refs, send_sems, recv_sems):
    px, py, pc = _place()
    return [pltpu.make_async_remote_copy(
        src_ref=b.at[:, 1 - pc], dst_ref=o, send_sem=send_sems.at[a], recv_sem=recv_sems.at[a],
        device_id=(px, py, 1 - pc), device_id_type=MESH) for a, (b, o) in enumerate(zip(blk_refs, out_refs))]


def _chip_shapes(ps):
    return [SDS((3,) + p.shape[1:], p.dtype) for p in ps]


def _chip_copies(p_refs, out_refs, send_sems, recv_sems):
    px, py, pc = _place()
    n = len(p_refs)
    chips = [(1 - px, py), (px, 1 - py), (1 - px, 1 - py)]
    return [pltpu.make_async_remote_copy(
        src_ref=p_refs[a].at[2 * qx + qy], dst_ref=out_refs[a].at[k],
        send_sem=send_sems.at[3 * a + k], recv_sem=recv_sems.at[3 * a + k],
        device_id=(qx, qy, pc), device_id_type=MESH) for k, (qx, qy) in enumerate(chips) for a in range(n)]


def _pair_sum(own, recv, ids, name, win_rows=None):
    _, R, C = recv.shape
    tr = _tile(R, 512, 8)

    def body(ids_ref, g_ref, r_ref, own_ref, pb_ref):
        s = g_ref[...] + r_ref[...]
        pb_ref[...] = s.astype(BF16)

        @pl.when(pl.program_id(1) == ids_ref[1])
        def _():
            own_ref[...] = s

    if win_rows is None:
        own_spec = pl.BlockSpec((None, None, tr, C), lambda r, j, ids: (j, ids[0], r, 0))
    else:
        own_spec = pl.BlockSpec((pl.Element(tr), pl.Element(C)), lambda r, j, ids: (pl.multiple_of(ids[2 + j] + r * tr, 8), 0))
    return pl.pallas_call(
        body, name=name,
        grid_spec=pltpu.PrefetchScalarGridSpec(
            num_scalar_prefetch=1, grid=(R // tr, 4),
            in_specs=[own_spec, pl.BlockSpec((None, tr, C), lambda r, j, ids: (j, r, 0))],
            out_specs=[pl.BlockSpec((tr, C), lambda r, j, ids: (r, 0)),
                       pl.BlockSpec((None, tr, C), lambda r, j, ids: (j, r, 0))]),
        out_shape=[SDS((R, C), F32), SDS((4, R, C), BF16)],
        compiler_params=_cparams(2),
    )(ids, own, recv)


def _final_sum(own, recv, name):
    R, C = own.shape
    tr = _tile(R, 512, 8)

    def body(o_ref, r_ref, out_ref):
        out_ref[...] = ((o_ref[...] + r_ref[0].astype(F32)) + r_ref[1].astype(F32)) + r_ref[2].astype(F32)

    return pl.pallas_call(
        body, name=name, grid=(R // tr,),
        in_specs=[pl.BlockSpec((tr, C), lambda r: (r, 0)), pl.BlockSpec((3, tr, C), lambda r: (0, r, 0))],
        out_specs=pl.BlockSpec((tr, C), lambda r: (r, 0)),
        out_shape=SDS((R, C), F32),
        compiler_params=_cparams(1),
    )(own, recv)


def _sum8(a, name):
    def body(a_ref, out_ref):
        s = a_ref[0]
        for d in range(1, N_DEV):
            s = s + a_ref[d]
        out_ref[...] = s

    return pl.pallas_call(body, name=name, out_shape=SDS(a.shape[1:], F32))(a)


def _inproj_fwd(x, g1, w_t, NM, tm):
    T, D = x.shape
    NF = w_t.shape[0] - NM
    ch = _tile(NM, 1024)

    def body(x_ref, g_ref, w_ref, pm_ref, fl_ref, h_ref):
        xv = x_ref[...]
        h = (xv * _rms(xv) * g_ref[...]).astype(BF16)
        h_ref[...] = h
        for c in range(0, NM, ch):
            pm_ref[:, c:c + ch] = _dot_nt(h, w_ref[c:c + ch, :]).astype(BF16)
        fl_ref[...] = _dot_nt(h, w_ref[NM:NM + NF, :])

    return pl.pallas_call(
        body, name="inproj_fwd", grid=(T // tm,),
        in_specs=[pl.BlockSpec((tm, D), lambda i: (i, 0)), _resident((1, D)), _resident(w_t.shape)],
        out_specs=[pl.BlockSpec((tm, NM), lambda i: (i, 0)), pl.BlockSpec((tm, NF), lambda i: (i, 0)),
                   pl.BlockSpec((tm, D), lambda i: (i, 0))],
        out_shape=[SDS((T, NM), BF16), SDS((T, NF), F32), SDS((T, D), BF16)],
        compiler_params=_cparams(1),
    )(x, g1, w_t)


def _log_sigmoid(x):
    return jnp.minimum(x, 0.0) - jnp.log(1.0 + jnp.exp(-jnp.abs(x)))


def _forget_cumsum(fl, bf, S):
    T, NF = fl.shape
    ch = _tile(S, 256, 8)

    def body(fl_ref, bf_ref, f_ref):
        row = lax.broadcasted_iota(jnp.int32, (ch, ch), 0)
        col = lax.broadcasted_iota(jnp.int32, (ch, ch), 1)
        tri = (col <= row).astype(BF16)
        carry = jnp.zeros((1, NF), F32)
        for c in range(0, S, ch):
            lf = _log_sigmoid(fl_ref[c:c + ch, :] + bf_ref[...])
            f_ref[c:c + ch, :] = _tri_dot(tri, lf) + carry
            carry = carry + jnp.sum(lf, axis=0, keepdims=True)

    return pl.pallas_call(
        body, name="forget_cumsum", grid=(T // S,),
        in_specs=[pl.BlockSpec((S, NF), lambda b: (b, 0)), _resident((1, NF))],
        out_specs=pl.BlockSpec((S, NF), lambda b: (b, 0)),
        out_shape=SDS((T, NF), F32),
        compiler_params=_cparams(1),
    )(fl, bf)


def _head_mask(e, hd):
    lanes = lax.broadcasted_iota(jnp.int32, (1, LANES), 1)
    return (lanes >= e * hd) & (lanes < (e + 1) * hd)


def _attn_specs(S, q_off, AW):
    ng = AW // LANES
    return [pl.BlockSpec((S, LANES), lambda b, g, o=q_off + w * ng: (b, o + g)) for w in range(3)]


def _attn_fwd(pm, fcum, frow, S, q_off, AW, hd, tq, gather):
    T = pm.shape[0]
    scale = float(hd) ** -0.5
    nq, hpg = S // tq, LANES // hd
    ng, nx = AW // LANES, len(gather)
    steps = (T // S) * ng

    def body(q_ref, k_ref, v_ref, fc_ref, fr_ref, *rest):
        x_refs, (o_ref, lse_ref), out_refs, sems = rest[:nx], rest[nx:nx + 2], rest[nx + 2:2 * nx + 2], rest[2 * nx + 2:]
        g = pl.program_id(1)
        step = pl.program_id(0) * ng + g
        start, forward, finish = _gather_phases(x_refs, out_refs, *sems)
        pl.when(step == 0)(start)

        @pl.when(g == 0)
        def _():
            lse_ref[...] = jnp.zeros_like(lse_ref)

        lanes = lax.broadcasted_iota(jnp.int32, (1, LANES), 1)
        for i in range(nq):
            rs, kend = slice(i * tq, (i + 1) * tq), (i + 1) * tq
            row = i * tq + lax.broadcasted_iota(jnp.int32, (tq, kend), 0)
            col = lax.broadcasted_iota(jnp.int32, (tq, kend), 1)
            o_tile = jnp.zeros((tq, LANES), F32)
            lse_tile = lse_ref[rs, :]
            for e in range(hpg):
                mask = _head_mask(e, hd)
                qs = jnp.where(mask, q_ref[rs, :], 0) * scale
                s = _dot_nt(qs, k_ref[0:kend, :]) + _lane_pick(fc_ref[rs, :], g * hpg + e) - fr_ref[e:e + 1, 0:kend]
                s = jnp.where(col <= row, s, -1e30)
                m = jnp.max(s, axis=1, keepdims=True)
                p = jnp.exp(s - m)
                l = jnp.sum(p, axis=1, keepdims=True)
                o_tile = jnp.where(mask, _dot(p.astype(BF16), v_ref[0:kend, :]) / l, o_tile)
                lse_tile = jnp.where(lanes == g * hpg + e, m + jnp.log(l), lse_tile)
            o_ref[rs, :] = o_tile.astype(BF16)
            lse_ref[rs, :] = lse_tile
        pl.when(step == (steps * 5) // 8)(forward)
        pl.when(step == steps - 1)(finish)

    full = pl.BlockSpec((S, LANES), lambda b, g: (b, 0))
    return pl.pallas_call(
        body, name="attn_fwd", grid=(T // S, ng),
        in_specs=_attn_specs(S, q_off, AW) + [full, pl.BlockSpec((None, None, 8, S), lambda b, g: (b, g, 0, 0))] + [ANY] * nx,
        out_specs=[pl.BlockSpec((S, LANES), lambda b, g: (b, g)), full] + [ANY] * nx,
        out_shape=[SDS((T, AW), BF16), SDS((T, LANES), F32)] + _gather_shapes(gather),
        scratch_shapes=_gather_sems(nx),
        compiler_params=_cparams(2),
    )(pm, pm, pm, fcum, frow, *gather)


def _merge_fwd(pm, o, x, wc, bg, woc_t, woa_t, wo, S, tm):
    T, D = x.shape
    CW, AW = woc_t.shape[1], woa_t.shape[1]
    tps = S // tm

    def body(cb_ref, cc_ref, cin_ref, gc_ref, ga_ref, o_ref, x_ref, wc_ref, bg_ref, woct_ref, woat_ref, wo_ref,
             x2_ref, u_ref, m_ref, ycin_ref, yc_ref, ya_ref, tail_ref):
        @pl.when(pl.program_id(0) % tps == 0)
        def _():
            tail_ref[...] = jnp.zeros_like(tail_ref)

        z = cc_ref[...].astype(F32) * cin_ref[...].astype(F32)
        z1, z2 = _taps_back(z, tail_ref[6:7, :], tail_ref[7:8, :])
        tail_ref[...] = z[tm - 8:tm, :]
        u = (z * wc_ref[2:3, :] + z2 * wc_ref[0:1, :]) + z1 * wc_ref[1:2, :]
        u_ref[...] = u.astype(BF16)
        ycin = (cb_ref[...].astype(F32) * u).astype(BF16)
        ycin_ref[...] = ycin
        yc = _dot_nt(ycin, woct_ref[...])
        ya = _dot_nt(o_ref[...], woat_ref[...])
        yc_ref[...] = yc.astype(BF16)
        ya_ref[...] = ya.astype(BF16)
        gc = _sigmoid(gc_ref[...].astype(F32) + bg_ref[:, 0:D])
        ga = _sigmoid(ga_ref[...].astype(F32) + bg_ref[:, D:2 * D])
        m = (gc * yc + ga * ya).astype(BF16)
        m_ref[...] = m
        x2_ref[...] = x_ref[...] + _dot(m, wo_ref[...])

    g_off = (3 * CW + 3 * AW) // D
    tok = lambda w, j=0: pl.BlockSpec((tm, w), lambda i: (i, j))
    return pl.pallas_call(
        body, name="merge_fwd", grid=(T // tm,),
        in_specs=[tok(CW, 0), tok(CW, 1), tok(CW, 2), tok(D, g_off), tok(D, g_off + 1), tok(AW), tok(D),
                  _resident((8, CW)), _resident((1, 2 * D)), _resident((D, CW)), _resident((D, AW)), _resident((D, D))],
        out_specs=[tok(D), tok(CW), tok(D), tok(CW), tok(D), tok(D)],
        out_shape=[SDS((T, D), F32), SDS((T, CW), BF16), SDS((T, D), BF16), SDS((T, CW), BF16),
                   SDS((T, D), BF16), SDS((T, D), BF16)],
        scratch_shapes=[pltpu.VMEM((8, CW), F32)],
        compiler_params=_cparams(1),
    )(pm, pm, pm, pm, pm, o, x, wc, bg, woc_t, woa_t, wo)


def _ffn_fwd(x2, g2, wup_t, wf, wdn, gf, tgt, S, tm):
    T, D = x2.shape
    FH = wdn.shape[0]
    ch = _tile(FH, FFN_CHUNK)
    tps = S // tm

    def body(x2_ref, g2_ref, wupt_ref, wf_ref, wdn_ref, gf_ref, tgt_ref,
             upp_ref, up_ref, act_ref, h2_ref, dx3_ref, loss_ref, dgf_ref, tail_ref):
        i = pl.program_id(0)

        @pl.when(i % tps == 0)
        def _():
            tail_ref[...] = jnp.zeros_like(tail_ref)

        @pl.when(i == 0)
        def _():
            loss_ref[...] = jnp.zeros_like(loss_ref)
            dgf_ref[...] = jnp.zeros_like(dgf_ref)

        x2v = x2_ref[...]
        h2 = (x2v * _rms(x2v) * g2_ref[...]).astype(BF16)
        h2_ref[...] = h2
        x3 = x2v
        for c in range(0, FH, ch):
            gated = []
            for cols in (slice(c, c + ch), slice(FH + c, FH + c + ch)):
                upp = _dot_nt(h2, wupt_ref[cols, :])
                upp_ref[:, cols] = upp.astype(BF16)
                p1, p2 = _taps_back(upp, tail_ref[6:7, cols], tail_ref[7:8, cols])
                tail_ref[:, cols] = upp[tm - 8:tm, :]
                up = (upp * wf_ref[2:3, cols] + p2 * wf_ref[0:1, cols]) + p1 * wf_ref[1:2, cols]
                up_ref[:, cols] = up.astype(BF16)
                gated.append(up)
            a, b = gated
            act = (a * _sigmoid(a) * b).astype(BF16)
            act_ref[:, c:c + ch] = act
            x3 = x3 + _dot(act, wdn_ref[c:c + ch, :])
        r3 = _rms(x3)
        xn3 = x3 * r3
        e = xn3 * gf_ref[...] - tgt_ref[...]
        loss_ref[...] += 0.5 * jnp.sum(jnp.mean(e * e, axis=-1, keepdims=True), axis=0, keepdims=True)
        dy = e / D
        dgf_ref[0:1, :] += jnp.sum(dy * xn3, axis=0, keepdims=True)
        dxn = dy * gf_ref[...]
        dx3_ref[...] = r3 * (dxn - xn3 * jnp.mean(dxn * xn3, axis=-1, keepdims=True))

    tok = lambda w: pl.BlockSpec((tm, w), lambda i: (i, 0))
    return pl.pallas_call(
        body, name="ffn_fwd", grid=(T // tm,),
        in_specs=[tok(D), _resident((1, D)), _resident((2 * FH, D)), _resident((8, 2 * FH)), _resident((FH, D)),
                  _resident((1, D)), tok(D)],
        out_specs=[tok(2 * FH), tok(2 * FH), tok(FH), tok(D), tok(D), pl.BlockSpec((8, LANES), lambda i: (0, 0)),
                   pl.BlockSpec((8, D), lambda i: (0, 0))],
        out_shape=[SDS((T, 2 * FH), BF16), SDS((T, 2 * FH), BF16), SDS((T, FH), BF16), SDS((T, D), BF16), SDS((T, D), F32),
                   SDS((8, LANES), F32), SDS((8, D), F32)],
        scratch_shapes=[pltpu.VMEM((8, 2 * FH), F32)],
        compiler_params=_cparams(1),
    )(x2, g2, wup_t, wf, wdn, gf, tgt)


def _ffn_bwd(dx3, x2, g2, upp, up, wf, wdn, wup_t, S, tm):
    T, D = x2.shape
    FH = wdn.shape[0]
    ch = _tile(FH, FFN_CHUNK)
    n, tps = T // tm, S // tm

    def body(dx3_ref, x2_ref, g2_ref, upp_ref, up_ref, wf_ref, wdn_ref, wupt_ref,
             dx2_ref, dupp_ref, dwf_ref, dg2_ref, head_ref):
        i = pl.program_id(0)
        t = n - 1 - i

        @pl.when(i == 0)
        def _():
            dwf_ref[...] = jnp.zeros_like(dwf_ref)
            dg2_ref[...] = jnp.zeros_like(dg2_ref)

        @pl.when(t % tps == tps - 1)
        def _():
            head_ref[...] = jnp.zeros_like(head_ref)

        dx3v = dx3_ref[...]
        dx3b = dx3v.astype(BF16)
        x2v = x2_ref[...]
        r2 = _rms(x2v)
        xn2 = x2v * r2
        dh2 = jnp.zeros((tm, D), F32)
        for c in range(0, FH, ch):
            ca, cb = slice(c, c + ch), slice(FH + c, FH + c + ch)
            a, b = up_ref[:, ca].astype(F32), up_ref[:, cb].astype(F32)
            sig = _sigmoid(a)
            sl = a * sig
            dact = _dot_nt(dx3b, wdn_ref[ca, :])
            grads = (dact * b * (sig * (1.0 + a * (1.0 - sig))), dact * sl)
            for cols, d in zip((ca, cb), grads):
                u0 = upp_ref[:, cols].astype(F32)
                d1, d2 = _taps_ahead(d, head_ref[0:1, cols], head_ref[1:2, cols])
                head_ref[:, cols] = d[0:8, :]
                dwf_ref[2:3, cols] += jnp.sum(u0 * d, axis=0, keepdims=True)
                dwf_ref[1:2, cols] += jnp.sum(u0 * d1, axis=0, keepdims=True)
                dwf_ref[0:1, cols] += jnp.sum(u0 * d2, axis=0, keepdims=True)
                dpre = ((d * wf_ref[2:3, cols] + d1 * wf_ref[1:2, cols]) + d2 * wf_ref[0:1, cols]).astype(BF16)
                dupp_ref[:, cols] = dpre
                dh2 = dh2 + _dot(dpre, wupt_ref[cols, :])
        dg2_ref[0:1, :] += jnp.sum(dh2 * xn2, axis=0, keepdims=True)
        dxn = dh2 * g2_ref[...]
        dx2_ref[...] = dx3v + r2 * (dxn - xn2 * jnp.mean(dxn * xn2, axis=-1, keepdims=True))

    tok = lambda w: pl.BlockSpec((tm, w), lambda i: (n - 1 - i, 0))
    acc = lambda w: pl.BlockSpec((8, w), lambda i: (0, 0))
    return pl.pallas_call(
        body, name="ffn_bwd", grid=(n,),
        in_specs=[tok(D), tok(D), _resident((1, D)), tok(2 * FH), tok(2 * FH), _resident((8, 2 * FH)),
                  _resident((FH, D)), _resident((2 * FH, D))],
        out_specs=[tok(D), tok(2 * FH), acc(2 * FH), acc(D)],
        out_shape=[SDS((T, D), F32), SDS((T, 2 * FH), BF16), SDS((8, 2 * FH), F32), SDS((8, D), F32)],
        scratch_shapes=[pltpu.VMEM((8, 2 * FH), F32)],
        compiler_params=_cparams(1),
    )(dx3, x2, g2, upp, up, wf, wdn, wup_t)


def _merge_bwd(dx2, pm, u, yc, ya, m, ycin, o, wc, bg, wo, woc_t, woa_t, S, tm, exchange):
    T, D = dx2.shape
    CW, AW = woc_t.shape[1], woa_t.shape[1]
    n, tps = T // tm, S // tm
    nx = len(exchange)
    rb = D // N_DEV
    assert CW + AW == D

    def body(dx2_ref, cb_ref, cc_ref, cin_ref, gc_ref, ga_ref, u_ref, yc_ref, ya_ref, m_ref, ycin_ref, o_ref,
             wc_ref, bg_ref, wo_ref, woct_ref, woat_ref, *rest):
        x_refs = rest[:nx]
        dconv_ref, dgl_ref, do_ref, dws_ref, dwc_ref, dbg_ref = rest[nx:nx + 6]
        got_refs, head_ref, sems = rest[nx + 6:2 * nx + 6], rest[2 * nx + 6], rest[2 * nx + 7:]

        def add_blocks(rows, cols, grad):
            for d in range(N_DEV):
                dws_ref[d, rows, cols] += grad[d * rb:(d + 1) * rb, :]

        i = pl.program_id(0)
        t = n - 1 - i

        @pl.when(i == 0)
        def _():
            for cp in _sibling_copies(x_refs, got_refs, *sems):
                cp.start()

        @pl.when(i == 0)
        def _():
            for ref in (dwc_ref, dbg_ref, dws_ref):
                ref[...] = jnp.zeros_like(ref)

        @pl.when(t % tps == tps - 1)
        def _():
            head_ref[...] = jnp.zeros_like(head_ref)

        dx2b = dx2_ref[...].astype(BF16)
        add_blocks(slice(0, rb), slice(0, D), _dot_tn(m_ref[...], dx2b))
        dm = _dot_nt(dx2b, wo_ref[...])
        outs = []
        for g_ref, y_ref, cols in ((gc_ref, yc_ref, slice(0, D)), (ga_ref, ya_ref, slice(D, 2 * D))):
            g = _sigmoid(g_ref[...].astype(F32) + bg_ref[:, cols])
            dgl = dm * y_ref[...].astype(F32) * g * (1.0 - g)
            dgl_ref[:, cols] = dgl.astype(BF16)
            dbg_ref[0:1, cols] += jnp.sum(dgl, axis=0, keepdims=True)
            outs.append((dm * g).astype(BF16))
        dyc, dya = outs
        add_blocks(slice(rb, 2 * rb), slice(0, CW), _dot_tn(dyc, ycin_ref[...]))
        add_blocks(slice(rb, 2 * rb), slice(CW, CW + AW), _dot_tn(dya, o_ref[...]))
        do_ref[...] = _dot(dya, woat_ref[...]).astype(BF16)
        dycin = _dot(dyc, woct_ref[...])
        cc, cin = cc_ref[...].astype(F32), cin_ref[...].astype(F32)
        z = cc * cin
        du = dycin * cb_ref[...].astype(F32)
        du1, du2 = _taps_ahead(du, head_ref[0:1, :], head_ref[1:2, :])
        head_ref[...] = du[0:8, :]
        dwc_ref[2:3, :] += jnp.sum(z * du, axis=0, keepdims=True)
        dwc_ref[1:2, :] += jnp.sum(z * du1, axis=0, keepdims=True)
        dwc_ref[0:1, :] += jnp.sum(z * du2, axis=0, keepdims=True)
        dz = (du * wc_ref[2:3, :] + du1 * wc_ref[1:2, :]) + du2 * wc_ref[0:1, :]
        dconv_ref[:, 0:CW] = (dycin * u_ref[...].astype(F32)).astype(BF16)
        dconv_ref[:, CW:2 * CW] = (dz * cin).astype(BF16)
        dconv_ref[:, 2 * CW:3 * CW] = (dz * cc).astype(BF16)

        @pl.when(i == n - 1)
        def _():
            for cp in _sibling_copies(x_refs, got_refs, *sems):
                cp.wait()

    g_off = (3 * CW + 3 * AW) // D
    tok = lambda w, j=0: pl.BlockSpec((tm, w), lambda i: (n - 1 - i, j))
    acc = lambda w: pl.BlockSpec((8, w), lambda i: (0, 0))
    return pl.pallas_call(
        body, name="merge_bwd", grid=(n,),
        in_specs=[tok(D), tok(CW, 0), tok(CW, 1), tok(CW, 2), tok(D, g_off), tok(D, g_off + 1), tok(CW), tok(D), tok(D),
                  tok(D), tok(CW), tok(AW),
                  _resident((8, CW)), _resident((1, 2 * D)), _resident((D, D)), _resident((D, CW)), _resident((D, AW))]
                 + [ANY] * nx,
        out_specs=[tok(3 * CW), tok(2 * D), tok(AW), pl.BlockSpec((N_DEV, 2 * rb, D), lambda i: (0, 0, 0)), acc(CW),
                   acc(2 * D)] + [ANY] * nx,
        out_shape=[SDS((T, 3 * CW), BF16), SDS((T, 2 * D), BF16), SDS((T, AW), BF16), SDS((N_DEV, 2 * rb, D), F32),
                   SDS((8, CW), F32), SDS((8, 2 * D), F32)] + _sibling_shapes(exchange),
        scratch_shapes=[pltpu.VMEM((8, CW), F32)] + _exchange_sems(nx),
        compiler_params=_cparams(1),
    )(dx2, pm, pm, pm, pm, pm, u, yc, ya, m, ycin, o, wc, bg, wo, woc_t, woa_t, *exchange)


def _attn_bwd(pm, do, fcum, frow, lse, S, q_off, AW, hd, tq, exchange):
    T = pm.shape[0]
    scale = float(hd) ** -0.5
    nq, hpg, ng = S // tq, LANES // hd, AW // LANES
    nx = len(exchange)
    steps = (T // S) * ng

    def body(q_ref, k_ref, v_ref, do_ref, fc_ref, fr_ref, lse_ref, *rest):
        x_refs, (dq_ref, dk_ref, dv_ref, dfr_ref) = rest[:nx], rest[nx:nx + 4]
        got_refs, (dk_acc, dv_acc), sems = rest[nx + 4:2 * nx + 4], rest[2 * nx + 4:2 * nx + 6], rest[2 * nx + 6:]
        g = pl.program_id(1)
        step = pl.program_id(0) * ng + g

        @pl.when(step == 0)
        def _():
            for cp in _chip_copies(x_refs, got_refs, *sems):
                cp.start()

        dk_acc[...] = jnp.zeros_like(dk_acc)
        dv_acc[...] = jnp.zeros_like(dv_acc)
        dfr_ref[...] = jnp.zeros_like(dfr_ref)
        for i in range(nq):
            rs, kend = slice(i * tq, (i + 1) * tq), (i + 1) * tq
            row = i * tq + lax.broadcasted_iota(jnp.int32, (tq, kend), 0)
            col = lax.broadcasted_iota(jnp.int32, (tq, kend), 1)
            kk, vv = k_ref[0:kend, :], v_ref[0:kend, :]
            dq_tile = jnp.zeros((tq, LANES), F32)
            for e in range(hpg):
                mask = _head_mask(e, hd)
                qs = jnp.where(mask, q_ref[rs, :], 0) * scale
                doi = jnp.where(mask, do_ref[rs, :], 0)
                s = _dot_nt(qs, kk) + _lane_pick(fc_ref[rs, :], g * hpg + e) - fr_ref[e:e + 1, 0:kend]
                p = jnp.where(col <= row, jnp.exp(s - _lane_pick(lse_ref[rs, :], g * hpg + e)), 0.0)
                dp = _dot_nt(doi, vv)
                ds = p * (dp - jnp.sum(p * dp, axis=1, keepdims=True))
                pb, dsb = p.astype(BF16), ds.astype(BF16)
                dq_tile = jnp.where(mask, _dot(dsb, kk) * scale, dq_tile)
                dv_acc[0:kend, :] += _dot_tn(pb, doi)
                dk_acc[0:kend, :] += _dot_tn(dsb, qs)
                dfr_ref[e:e + 1, 0:kend] -= jnp.sum(ds, axis=0, keepdims=True)
            dq_ref[rs, :] = dq_tile.astype(BF16)
        dk_ref[...] = dk_acc[...].astype(BF16)
        dv_ref[...] = dv_acc[...].astype(BF16)

        @pl.when(step == steps - 1)
        def _():
            for cp in _chip_copies(x_refs, got_refs, *sems):
                cp.wait()

    full = pl.BlockSpec((S, LANES), lambda b, g: (b, 0))
    grp = pl.BlockSpec((S, LANES), lambda b, g: (b, g))
    rows = pl.BlockSpec((None, None, 8, S), lambda b, g: (b, g, 0, 0))
    return pl.pallas_call(
        body, name="attn_bwd", grid=(T // S, ng),
        in_specs=_attn_specs(S, q_off, AW) + [grp, full, rows, full] + [ANY] * nx,
        out_specs=[grp, grp, grp, rows] + [ANY] * nx,
        out_shape=[SDS((T, AW), BF16), SDS((T, AW), BF16), SDS((T, AW), BF16), SDS((T // S, ng, 8, S), F32)]
                  + _chip_shapes(exchange),
        scratch_shapes=[pltpu.VMEM((S, LANES), F32), pltpu.VMEM((S, LANES), F32)] + _exchange_sems(3 * nx),
        compiler_params=_cparams(2),
    )(pm, pm, pm, do, fcum, frow, lse, *exchange)


def _forget_bwd(dfc, fl, bf, S):
    T, NF = fl.shape
    ch = _tile(S, 256, 8)

    def body(df_ref, fl_ref, bf_ref, dfl_ref, dbf_ref):
        @pl.when(pl.program_id(0) == 0)
        def _():
            dbf_ref[...] = jnp.zeros_like(dbf_ref)

        row = lax.broadcasted_iota(jnp.int32, (ch, ch), 0)
        col = lax.broadcasted_iota(jnp.int32, (ch, ch), 1)
        tri = (col >= row).astype(BF16)
        carry = jnp.zeros((1, NF), F32)
        for c in range(S - ch, -1, -ch):
            d = df_ref[c:c + ch, :]
            dlf = _tri_dot(tri, d) + carry
            carry = carry + jnp.sum(d, axis=0, keepdims=True)
            dfl = dlf * _sigmoid(-(fl_ref[c:c + ch, :] + bf_ref[...]))
            dfl_ref[c:c + ch, :] = dfl.astype(BF16)
            dbf_ref[0:1, :] += jnp.sum(dfl, axis=0, keepdims=True)

    return pl.pallas_call(
        body, name="forget_bwd", grid=(T // S,),
        in_specs=[pl.BlockSpec((S, NF), lambda b: (b, 0)), pl.BlockSpec((S, NF), lambda b: (b, 0)), _resident((1, NF))],
        out_specs=[pl.BlockSpec((S, NF), lambda b: (b, 0)), pl.BlockSpec((8, NF), lambda b: (0, 0))],
        out_shape=[SDS((T, NF), BF16), SDS((8, NF), F32)],
        compiler_params=_cparams(1),
    )(dfc, fl, bf)


def _inproj_bwd(dparts, offs, w_t, x, g1, dx2, tm, exchange):
    T, D = x.shape
    npart, nx = len(dparts), len(exchange)
    n = T // tm

    def body(*refs):
        d_refs = refs[:npart]
        w_ref, x_ref, g_ref, dx2_ref = refs[npart:npart + 4]
        x_refs = refs[npart + 4:npart + 4 + nx]
        dx_ref, dg_ref = refs[npart + 4 + nx:npart + 6 + nx]
        got_refs, sems = refs[npart + 6 + nx:npart + 6 + 2 * nx], refs[npart + 6 + 2 * nx:]

        @pl.when(pl.program_id(0) == 0)
        def _():
            for cp in _chip_copies(x_refs, got_refs, *sems):
                cp.start()

        @pl.when(pl.program_id(0) == 0)
        def _():
            dg_ref[...] = jnp.zeros_like(dg_ref)

        dh = None
        for d_ref, off in zip(d_refs, offs):
            term = _dot(d_ref[...], w_ref[off:off + d_ref.shape[1], :])
            dh = term if dh is None else dh + term
        xv = x_ref[...]
        r = _rms(xv)
        xn = xv * r
        dg_ref[0:1, :] += jnp.sum(dh * xn, axis=0, keepdims=True)
        dxn = dh * g_ref[...]
        dx_ref[...] = dx2_ref[...] + r * (dxn - xn * jnp.mean(dxn * xn, axis=-1, keepdims=True))

        @pl.when(pl.program_id(0) == n - 1)
        def _():
            for cp in _chip_copies(x_refs, got_refs, *sems):
                cp.wait()

    tok = lambda w: pl.BlockSpec((tm, w), lambda i: (i, 0))
    return pl.pallas_call(
        body, name="inproj_bwd", grid=(n,),
        in_specs=[tok(d.shape[1]) for d in dparts] + [_resident(w_t.shape), tok(D), _resident((1, D)), tok(D)] + [ANY] * nx,
        out_specs=[tok(D), pl.BlockSpec((8, D), lambda i: (0, 0))] + [ANY] * nx,
        out_shape=[SDS((T, D), F32), SDS((8, D), F32)] + _chip_shapes(exchange),
        scratch_shapes=_exchange_sems(3 * nx),
        compiler_params=_cparams(1),
    )(*dparts, w_t, x, g1, dx2, *exchange)


def _wgrad(b, a, name, into=None, row_off=0, total_rows=None):
    T, N = b.shape
    M = a.shape[1]
    tn = _tile(N, 1408 if M <= 1024 else 512)
    while row_off % tn:
        tn = _tile(N, tn - LANES)
    tk = _tile(T, 1024, 16)
    blk0 = row_off // tn

    def body(b_ref, a_ref, *rest):
        o_ref = rest[-1]

        @pl.when(pl.program_id(1) == 0)
        def _():
            o_ref[...] = jnp.zeros_like(o_ref)

        o_ref[...] += _dot_tn(b_ref[...].astype(BF16), a_ref[...].astype(BF16))

    in_specs = [pl.BlockSpec((tk, tn), lambda j, k: (k, j)), pl.BlockSpec((tk, M), lambda j, k: (k, 0))]
    args = (b, a)
    kwargs = {}
    if into is not None:
        in_specs.append(ANY)
        args += (into,)
        kwargs["input_output_aliases"] = {2: 0}
        total_rows = into.shape[0]
    return pl.pallas_call(
        body, name=name, grid=(N // tn, T // tk),
        in_specs=in_specs,
        out_specs=pl.BlockSpec((tn, M), lambda j, k: (blk0 + j, 0)),
        out_shape=SDS((N if total_rows is None else total_rows, M), F32),
        compiler_params=_cparams(2), **kwargs,
    )(*args)


def _adamw(w, g, m, v, name):
    shape = w.shape
    C = shape[-1]
    w2, g2, m2, v2 = (a.reshape(-1, C) for a in (w, g, m, v))
    R = w2.shape[0]
    tr = R if R <= 512 else _tile(R, 256, 8)
    if tr < 64:
        tr = R

    def body(w_ref, g_ref, m_ref, v_ref, d_ref, nm_ref, nv_ref):
        gv = g_ref[...]
        mv = ADAM_B1 * m_ref[...] + (1.0 - ADAM_B1) * gv
        vv = ADAM_B2 * v_ref[...] + (1.0 - ADAM_B2) * (gv * gv)
        m_hat = mv / (1.0 - ADAM_B1 ** ADAM_STEP)
        v_hat = vv / (1.0 - ADAM_B2 ** ADAM_STEP)
        d_ref[...] = -ADAM_LR * (m_hat / (jnp.sqrt(v_hat) + ADAM_EPS) + ADAM_WD * w_ref[...])
        nm_ref[...] = mv
        nv_ref[...] = vv

    spec = pl.BlockSpec((tr, C), lambda r: (r, 0))
    outs = pl.pallas_call(
        body, name=name, grid=(R // tr,),
        in_specs=[spec] * 4, out_specs=[spec] * 3, out_shape=[SDS((R, C), F32)] * 3,
        compiler_params=_cparams(1),
    )(w2, g2, m2, v2)
    return tuple(o.reshape(shape) for o in outs)


def _rows(a, L):
    lead = a.shape[0]
    flat = a.reshape(lead, -1)
    n = flat.shape[1]
    r = -(-n // L)
    return jnp.pad(flat, ((0, 0), (0, r * L - n))).reshape(lead, r, L)


def _unrows(p, shape):
    return p.reshape(-1)[:int(np.prod(shape))].reshape(shape)


def _from_col_blocks(a):
    n, R, c = a.shape
    return a.transpose(1, 0, 2).reshape(R, n * c)


def _in_windows(n_loc, nqkv, H, NM):
    win = (n_loc + 7 + 7) // 8 * 8
    starts, index = [], np.zeros((N_DEV, n_loc), np.int32)
    for d in range(N_DEV):
        rows = np.arange(n_loc * d, n_loc * (d + 1))
        is_f = (rows >= nqkv) & (rows < nqkv + H)
        kept = np.where(rows < nqkv, rows, rows - H)
        lo = int(kept[~is_f].min())
        start = lo // 8 * 8
        assert int(kept[~is_f].max()) - start < win and start + win <= NM + LANES
        starts.append(start)
        index[d] = np.where(is_f, win + rows - nqkv, kept - start)
    return starts, win, index


def kernel(x, norm_mix_g, w_in, b_f, b_gate, conv_mix_w, w_out_conv, w_out_attn, w_o, norm_ffn_g, w_up, conv_ffn_w, w_down, norm_f_g, loss_target, m_norm_mix_g, m_w_in, m_b_f, m_b_gate, m_conv_mix_w, m_w_out_conv, m_w_out_attn, m_w_o, m_norm_ffn_g, m_w_up, m_conv_ffn_w, m_w_down, m_norm_f_g, v_norm_mix_g, v_w_in, v_b_f, v_b_gate, v_conv_mix_w, v_w_out_conv, v_w_out_attn, v_w_o, v_norm_ffn_g, v_w_up, v_conv_ffn_w, v_w_down, v_norm_f_g):
    Bl, S, D = x.shape
    T = Bl * S
    H = b_f.shape[-1]
    CW = N_DEV * conv_mix_w.shape[-1]
    AW = w_out_attn.shape[1]
    hd = AW // H
    FH = N_DEV * w_down.shape[1]
    n_loc = w_in.shape[-1]
    NIN = N_DEV * n_loc
    NM = 3 * CW + 3 * AW + 2 * D
    nqkv = 3 * CW + 3 * AW
    assert NIN == NM + H and w_out_conv.shape[1] == CW and nqkv % D == 0 and AW % LANES == 0 and LANES % hd == 0
    hpg, ng = LANES // hd, AW // LANES
    assert hpg <= 8
    tm_big = min(512, S // 2)
    tm_ffn = min(256, S // 2)
    tq = min(256, S // 2)
    px, py, pc = _place()
    me = 4 * px + 2 * py + pc

    bits = lambda a: lax.bitcast_convert_type(a, BF16)
    taps = jnp.concatenate([_rows(bits(conv_ffn_w[0])[None], D)[0], _rows(bits(conv_mix_w[0])[None], D)[0]], axis=0)
    n_ffn_rows = -(-conv_ffn_w[0].size * 2 // D)
    late = [w_up[0].T.astype(BF16), w_down[0].astype(BF16), w_o[0].astype(BF16), w_out_conv[0].T.astype(BF16),
            w_out_attn[0].T.astype(BF16), taps]
    g_in, = _all_gather([w_in[0].T.astype(BF16)], "weights_all_gather")
    W_in_rows = g_in.reshape(NIN, D)
    W_in_t = jnp.concatenate([W_in_rows[:nqkv], W_in_rows[nqkv + H:], W_in_rows[nqkv:nqkv + H],
                              jnp.zeros((LANES - H, D), BF16)], axis=0)
    bf128 = jnp.pad(b_f, ((0, 0), (0, LANES - H)))

    x2d = x.reshape(T, D)
    tgt = loss_target.reshape(T, D)
    pm, fl, h1 = _inproj_fwd(x2d, norm_mix_g, W_in_t, NM, tm_big)
    fcum = _forget_cumsum(fl, bf128, S)
    frow = jnp.pad(fcum[:, :H].reshape(Bl, S, ng, hpg).transpose(0, 2, 3, 1), ((0, 0), (0, 0), (0, 8 - hpg), (0, 0)))
    q_off = 3 * CW // LANES
    o, lse, g_up, g_dn, g_o, g_oc, g_oa, g_taps = _attn_fwd(pm, fcum, frow, S, q_off, AW, hd, tq, late)
    W_up_t = g_up.reshape(2 * FH, D)
    W_dn = g_dn.reshape(FH, D)
    W_o = g_o.reshape(D, D)
    W_oc_t = g_oc.reshape(D, CW)
    W_oa_t = g_oa.reshape(D, AW)
    tap_bits = g_taps.reshape(N_DEV, -1)
    n_ffn, n_mix = conv_ffn_w[0].size * 2, conv_mix_w[0].size * 2
    wf_full = _from_col_blocks(lax.bitcast_convert_type(
        tap_bits[:, :n_ffn].reshape((N_DEV,) + conv_ffn_w.shape[1:] + (2,)), F32))
    wc_full = _from_col_blocks(lax.bitcast_convert_type(
        tap_bits[:, n_ffn_rows * D:n_ffn_rows * D + n_mix].reshape((N_DEV,) + conv_mix_w.shape[1:] + (2,)), F32))
    wf8 = jnp.pad(wf_full, ((0, 5), (0, 0)))
    wc8 = jnp.pad(wc_full, ((0, 5), (0, 0)))
    x2, u, m, ycin, yc, ya = _merge_fwd(pm, o, x2d, wc8, b_gate, W_oc_t, W_oa_t, W_o, S, tm_big)
    upp, up, act, h2, dx3, loss8, dgf8 = _ffn_fwd(x2, norm_ffn_g, W_up_t, wf8, W_dn, norm_f_g.reshape(1, D), tgt, S, tm_ffn)

    dx2, dupp, dwf8, dg2_8 = _ffn_bwd(dx3, x2, norm_ffn_g, upp, up, wf8, W_dn, W_up_t, S, tm_ffn)
    dW_dn = _wgrad(act, dx3, "wgrad_down")
    dW_up_t = _wgrad(dupp, h2, "wgrad_up")
    ids = jnp.stack([pc, 2 * px + py]).astype(jnp.int32)
    big = [dW_up_t.reshape(4, 2, -1, D), dW_dn.reshape(4, 2, -1, D)]
    dconv, dgl, do, dW_small, dwc8, dbg8, *sib_big = _merge_bwd(
        dx2, pm, u, yc, ya, m, ycin, o, wc8, b_gate, W_o, W_oc_t, W_oa_t, S, tm_big, big)
    big_sums = [_pair_sum(b, r, ids, "grads_pair_sum_%d" % a) for a, (b, r) in enumerate(zip(big, sib_big))]
    dq, dk, dv, dfr, *chips_big = _attn_bwd(pm, do, fcum, frow, lse, S, q_off, AW, hd, tq, [s[1] for s in big_sums])
    dfc = jnp.pad(dfr[:, :, :hpg, :].transpose(0, 3, 1, 2).reshape(T, H), ((0, 0), (0, LANES - H)))
    dfl, dbf8 = _forget_bwd(dfc, fl, bf128, S)
    dparts = [dconv, dq, dk, dv, dgl, dfl]
    offs = [0, 3 * CW, 3 * CW + AW, 3 * CW + 2 * AW, nqkv, NM]
    dW_in_t = _wgrad(dparts[0], h1, "wgrad_in_0", total_rows=NM + LANES)
    for k in range(1, len(dparts)):
        dW_in_t = _wgrad(dparts[k], h1, "wgrad_in_%d" % k, into=dW_in_t, row_off=offs[k])

    starts, win, index = _in_windows(n_loc, nqkv, H, NM)
    small = dW_small.reshape(4, 2, -1, D)
    my_starts = [jnp.where(pc == 0, starts[2 * j], starts[2 * j + 1]) for j in range(4)]
    win_ids = jnp.stack([pc, 2 * px + py] + my_starts).astype(jnp.int32)
    sib_win, sib_small = _sibling_exchange(dW_in_t, starts, win, [small], "grads_sibling_exchange")
    sums = [_pair_sum(dW_in_t, sib_win, win_ids, "grads_pair_sum_in", win_rows=win),
            _pair_sum(small, sib_small, ids, "grads_pair_sum_small")]
    grad_x, dg1_8, *from_chips = _inproj_bwd(dparts, offs, W_in_t, x2d, norm_mix_g, dx2, tm_big, [s[1] for s in sums])
    red_win, red_small, red_up, red_dn = [
        _final_sum(s[0], r, "grads_final_sum_%d" % a)
        for a, (s, r) in enumerate(zip(sums + big_sums, list(from_chips) + list(chips_big)))]

    f_rows = dW_in_t[NM:NM + 8]
    wf_rows = _rows(dwf8[:3].reshape(1, -1), D)[0]
    wc_rows = _rows(dwc8[:3].reshape(1, -1), D)[0]
    smalls = [dg1_8[0:1], dg2_8[0:1], dgf8[0:1], dbg8[0:1, :D], dbg8[0:1, D:], jnp.pad(dbf8[0:1], ((0, 0), (0, D - LANES))),
              jnp.pad(loss8[0:1], ((0, 0), (0, D - LANES))), jnp.zeros((1, D), F32), f_rows, wf_rows, wc_rows]
    spack = jnp.concatenate(smalls, axis=0)
    spack = jnp.pad(spack, ((0, -spack.shape[0] % 8), (0, 0)))
    ssum = _sum8(_all_gather([spack], "small_all_gather")[0], "small_sum")
    g_g1, g_g2, g_gf = ssum[0:1], ssum[1:2], ssum[2]
    g_bg = jnp.concatenate([ssum[3:4], ssum[4:5]], axis=1)
    g_bf = ssum[5:6, :H]
    r0 = 16
    r1 = r0 + wf_rows.shape[0]
    wf_sum = _unrows(ssum[r0:r1], (3, 2 * FH))
    wc_sum = _unrows(ssum[r1:r1 + wc_rows.shape[0]], (3, CW))
    g_wf = lax.dynamic_slice_in_dim(wf_sum, me * conv_ffn_w.shape[-1], conv_ffn_w.shape[-1], axis=1)
    g_wc = lax.dynamic_slice_in_dim(wc_sum, me * conv_mix_w.shape[-1], conv_mix_w.shape[-1], axis=1)

    ext = jnp.concatenate([red_win, ssum[8:8 + H]], axis=0)
    my_index = lax.dynamic_index_in_dim(jnp.asarray(index), me, axis=0, keepdims=False)
    g_w_in_t = jnp.take(ext, my_index, axis=0)
    rb = D // N_DEV
    g_w_o = red_small[:rb]
    g_w_oc = red_small[rb:, :CW].T
    g_w_oa = red_small[rb:, CW:].T

    loss = ssum[6, 0]

    names = ["norm_mix_g", "w_in", "b_f", "b_gate", "conv_mix_w", "w_out_conv", "w_out_attn", "w_o", "norm_ffn_g", "w_up",
             "conv_ffn_w", "w_down", "norm_f_g"]
    weights = [norm_mix_g, w_in, b_f, b_gate, conv_mix_w, w_out_conv, w_out_attn, w_o, norm_ffn_g, w_up, conv_ffn_w, w_down, norm_f_g]
    grads = [g_g1, g_w_in_t, g_bf, g_bg, g_wc, g_w_oc, g_w_oa, g_w_o, g_g2, red_up, g_wf, red_dn, g_gf]
    ms = [m_norm_mix_g, m_w_in, m_b_f, m_b_gate, m_conv_mix_w, m_w_out_conv, m_w_out_attn, m_w_o, m_norm_ffn_g, m_w_up,
          m_conv_ffn_w, m_w_down, m_norm_f_g]
    vs = [v_norm_mix_g, v_w_in, v_b_f, v_b_gate, v_conv_mix_w, v_w_out_conv, v_w_out_attn, v_w_o, v_norm_ffn_g, v_w_up,
          v_conv_ffn_w, v_w_down, v_norm_f_g]
    to_view = {"w_in": lambda a: a[0].T.reshape(-1, LANES), "w_up": lambda a: a[0].T}
    from_view = {"w_in": lambda a: a.reshape(n_loc, D).T[None], "w_up": lambda a: a.T[None]}
    out_grads, steps = [], []
    for nm, w, g, mm, vv in zip(names, weights, grads, ms, vs):
        if nm in to_view:
            wv, mv, vw = (to_view[nm](a) for a in (w, mm, vv))
            gv = g.reshape(wv.shape)
            steps.append(tuple(from_view[nm](o) for o in _adamw(wv, gv, mv, vw, "adamw_" + nm)))
            out_grads.append(from_view[nm](gv))
        else:
            gv = g.reshape(w.shape)
            steps.append(_adamw(w, gv, mm, vv, "adamw_" + nm))
            out_grads.append(gv)
    deltas, new_ms, new_vs = zip(*steps)
    return (loss, grad_x.reshape(Bl, S, D), *out_grads, *deltas, *new_ms, *new_vs)
```

```python
import numpy as np

import jax
import jax.numpy as jnp
from jax import lax
from jax.experimental import pallas as pl
from jax.experimental.pallas import tpu as pltpu

F32, BF16 = jnp.float32, jnp.bfloat16
EPS = 1e-6
ADAM_LR, ADAM_B1, ADAM_B2, ADAM_EPS, ADAM_WD, ADAM_STEP = 0.001, 0.9, 0.999, 1e-08, 0.01, 10
N_DEV = 8
LANES = 128
V7X_VMEM_LIMIT = 56 * 1024 * 1024
FFN_CHUNK = 2816
WGRAD_VMEM_BUDGET = 40 * 1024 * 1024
MESH = pl.DeviceIdType.MESH
SDS = jax.ShapeDtypeStruct
ANY = pl.BlockSpec(memory_space=pl.ANY)


def _tile(n, target, mult=LANES):
    best = None
    for t in range(mult, min(n, target) + 1, mult):
        if n % t == 0:
            best = t
    return best if best is not None else n


def _resident(shape):
    return pl.BlockSpec(shape, lambda *_: (0,) * len(shape), pipeline_mode=pl.Buffered(1))


def _cparams(n_axes=1):
    return pltpu.CompilerParams(dimension_semantics=("arbitrary",) * n_axes, vmem_limit_bytes=V7X_VMEM_LIMIT)


def _dot(a, b):
    return jnp.dot(a, b, preferred_element_type=F32)


def _dot_tn(a, b):
    return lax.dot_general(a, b, (((0,), (0,)), ((), ())), preferred_element_type=F32)


def _dot_nt(a, b):
    return lax.dot_general(a, b, (((1,), (1,)), ((), ())), preferred_element_type=F32)


def _sigmoid(x):
    return 0.5 * jnp.tanh(0.5 * x) + 0.5


def _rms(x):
    return lax.rsqrt(jnp.mean(x * x, axis=-1, keepdims=True) + EPS)


def _taps_back(z, r6, r7):
    row = lax.broadcasted_iota(jnp.int32, (8, 1), 0)
    z1, z2 = pltpu.roll(z, 1, 0), pltpu.roll(z, 2, 0)
    z1 = jnp.concatenate([jnp.where(row == 0, r7, z1[0:8]), z1[8:]], axis=0)
    z2 = jnp.concatenate([jnp.where(row == 0, r6, jnp.where(row == 1, r7, z2[0:8])), z2[8:]], axis=0)
    return z1, z2


def _taps_ahead(d, h0, h1):
    tm = d.shape[0]
    row = lax.broadcasted_iota(jnp.int32, (8, 1), 0)
    d1, d2 = pltpu.roll(d, tm - 1, 0), pltpu.roll(d, tm - 2, 0)
    d1 = jnp.concatenate([d1[:tm - 8], jnp.where(row == 7, h0, d1[tm - 8:])], axis=0)
    d2 = jnp.concatenate([d2[:tm - 8], jnp.where(row == 6, h0, jnp.where(row == 7, h1, d2[tm - 8:]))], axis=0)
    return d1, d2


def _tri_dot(tri, x):
    hi = x.astype(BF16)
    r = x - hi.astype(F32)
    mid = r.astype(BF16)
    lo = (r - mid.astype(F32)).astype(BF16)
    return (_dot(tri, lo) + _dot(tri, mid)) + _dot(tri, hi)


def _lane_pick(block, lane):
    lanes = lax.broadcasted_iota(jnp.int32, (1, block.shape[1]), 1)
    return jnp.sum(jnp.where(lanes == lane, block, 0.0), axis=1, keepdims=True)


def _place():
    return lax.axis_index("x"), lax.axis_index("y"), lax.axis_index("c")


def _all_gather(xs, name):
    n = len(xs)

    def body(*refs):
        start, forward, finish = _gather_phases(refs[:n], refs[n:2 * n], *refs[2 * n:])
        start()
        forward()
        finish()

    return pl.pallas_call(
        body, name=name,
        out_shape=_gather_shapes(xs), in_specs=[ANY] * n, out_specs=[ANY] * n, scratch_shapes=_gather_sems(n),
    )(*xs)


def _gather_shapes(xs):
    return [SDS((N_DEV,) + x.shape, x.dtype) for x in xs]


def _gather_sems(n):
    return [pltpu.SemaphoreType.DMA((7 * n,)), pltpu.SemaphoreType.DMA((7 * n,)), pltpu.SemaphoreType.DMA((n,))]


def _gather_phases(x_refs, out_refs, send_sems, recv_sems, local_sems):
    n = len(x_refs)

    def parts():
        px, py, pc = _place()
        me, sibling = (px, py, pc), (px, py, 1 - pc)
        chips = [(1 - px, py), (px, 1 - py), (1 - px, 1 - py)]

        def slot(a, qx, qy, qc):
            return out_refs[a].at[4 * qx + 2 * qy + qc]

        def copy(a, k, block, to, src=None):
            return pltpu.make_async_remote_copy(
                src_ref=slot(a, *block) if src is None else src, dst_ref=slot(a, *block),
                send_sem=send_sems.at[7 * a + k], recv_sem=recv_sems.at[7 * a + k], device_id=to, device_id_type=MESH)

        def mine():
            return [pltpu.make_async_copy(x_refs[a], slot(a, *me), local_sems.at[a]) for a in range(n)]

        def first():
            out = []
            for a in range(n):
                out.append(copy(a, 0, me, sibling, src=x_refs[a]))
                out += [copy(a, 1 + j, me, (*chip, pc), src=x_refs[a]) for j, chip in enumerate(chips)]
            return out

        def landed():
            return [copy(a, 1 + j, (*chip, pc), me) for j, chip in enumerate(chips) for a in range(n)]

        def passed():
            return [copy(a, 4 + j, (*chip, pc), sibling) for j, chip in enumerate(chips) for a in range(n)]

        def late():
            out = [copy(a, 0, sibling, me) for a in range(n)]
            return out + [copy(a, 4 + j, (*chip, 1 - pc), me) for j, chip in enumerate(chips) for a in range(n)]

        return mine, first, landed, passed, late

    def start():
        mine, first, _, _, _ = parts()
        for cp in mine() + first():
            cp.start()

    def forward():
        _, _, landed, passed, _ = parts()
        for got, cp in zip(landed(), passed()):
            got.wait_recv()
            cp.start()

    def finish():
        mine, first, _, passed, late = parts()
        for cp in late():
            cp.wait_recv()
        for cp in first() + passed():
            cp.wait_send()
        for cp in mine():
            cp.wait()

    return start, forward, finish


def _sibling_exchange(win_buf, win_starts, win_rows, blocked, name):
    nb = len(blocked)

    def body(*refs):
        win_ref, blk_refs = refs[0], refs[1:1 + nb]
        rwin_ref, rblk_refs = refs[1 + nb], refs[2 + nb:2 + 2 * nb]
        send_sems, recv_sems, wsend_sems, wrecv_sems = refs[2 + 2 * nb:]
        px, py, pc = _place()
        copies = _sibling_copies(blk_refs, rblk_refs, send_sems, recv_sems)
        for j in range(4):
            theirs = jnp.where(pc == 0, win_starts[2 * j + 1], win_starts[2 * j])
            copies.append(pltpu.make_async_remote_copy(
                src_ref=win_ref.at[pl.ds(pl.multiple_of(theirs, 8), win_rows)], dst_ref=rwin_ref.at[j],
                send_sem=wsend_sems.at[j], recv_sem=wrecv_sems.at[j], device_id=(px, py, 1 - pc), device_id_type=MESH))
        for cp in copies:
            cp.start()
        for cp in copies:
            cp.wait()

    C = win_buf.shape[1]
    return pl.pallas_call(
        body, name=name,
        out_shape=[SDS((4, win_rows, C), F32)] + _sibling_shapes(blocked),
        in_specs=[ANY] * (1 + nb), out_specs=[ANY] * (1 + nb),
        scratch_shapes=_exchange_sems(nb) + _exchange_sems(4),
    )(win_buf, *blocked)


def _sibling_shapes(blocked):
    return [SDS((4,) + b.shape[2:], F32) for b in blocked]


def _exchange_sems(n):
    return [pltpu.SemaphoreType.DMA((n,)), pltpu.SemaphoreType.DMA((n,))]


def _sibling_copies(blk_refs, out_refs, send_sems, recv_sems):
    px, py, pc = _place()
    return [pltpu.make_async_remote_copy(
        src_ref=b.at[:, 1 - pc], dst_ref=o, send_sem=send_sems.at[a], recv_sem=recv_sems.at[a],
        device_id=(px, py, 1 - pc), device_id_type=MESH) for a, (b, o) in enumerate(zip(blk_refs, out_refs))]


def _chip_shapes(ps):
    return [SDS((3,) + p.shape[1:], p.dtype) for p in ps]


def _chip_copies(p_refs, out_refs, send_sems, recv_sems):
    px, py, pc = _place()
    n = len(p_refs)
    chips = [(1 - px, py), (px, 1 - py), (1 - px, 1 - py)]
    return [pltpu.make_async_remote_copy(
        src_ref=p_refs[a].at[2 * qx + qy], dst_ref=out_refs[a].at[k],
        send_sem=send_sems.at[3 * a + k], recv_sem=recv_sems.at[3 * a + k],
        device_id=(qx, qy, pc), device_id_type=MESH) for k, (qx, qy) in enumerate(chips) for a in range(n)]


def _pair_sum(own, recv, ids, name, win_rows=None):
    _, R, C = recv.shape
    tr = _tile(R, 512, 8)

    def body(ids_ref, g_ref, r_ref, own_ref, pb_ref):
        s = g_ref[...] + r_ref[...]
        pb_ref[...] = s.astype(BF16)

        @pl.when(pl.program_id(1) == ids_ref[1])
        def _():
            own_ref[...] = s

    if win_rows is None:
        own_spec = pl.BlockSpec((None, None, tr, C), lambda r, j, ids: (j, ids[0], r, 0))
    else:
        own_spec = pl.BlockSpec((pl.Element(tr), pl.Element(C)), lambda r, j, ids: (pl.multiple_of(ids[2 + j] + r * tr, 8), 0))
    return pl.pallas_call(
        body, name=name,
        grid_spec=pltpu.PrefetchScalarGridSpec(
            num_scalar_prefetch=1, grid=(R // tr, 4),
            in_specs=[own_spec, pl.BlockSpec((None, tr, C), lambda r, j, ids: (j, r, 0))],
            out_specs=[pl.BlockSpec((tr, C), lambda r, j, ids: (r, 0)),
                       pl.BlockSpec((None, tr, C), lambda r, j, ids: (j, r, 0))]),
        out_shape=[SDS((R, C), F32), SDS((4, R, C), BF16)],
        compiler_params=_cparams(2),
    )(ids, own, recv)


def _final_sum(own, recv, name):
    R, C = own.shape
    tr = _tile(R, 512, 8)

    def body(o_ref, r_ref, out_ref):
        out_ref[...] = ((o_ref[...] + r_ref[0].astype(F32)) + r_ref[1].astype(F32)) + r_ref[2].astype(F32)

    return pl.pallas_call(
        body, name=name, grid=(R // tr,),
        in_specs=[pl.BlockSpec((tr, C), lambda r: (r, 0)), pl.BlockSpec((3, tr, C), lambda r: (0, r, 0))],
        out_specs=pl.BlockSpec((tr, C), lambda r: (r, 0)),
        out_shape=SDS((R, C), F32),
        compiler_params=_cparams(1),
    )(own, recv)


def _sum8(a, name):
    def body(a_ref, out_ref):
        s = a_ref[0]
        for d in range(1, N_DEV):
            s = s + a_ref[d]
        out_ref[...] = s

    return pl.pallas_call(body, name=name, out_shape=SDS(a.shape[1:], F32))(a)


def _inproj_fwd(x, g1, w_t, NM, tm):
    T, D = x.shape
    NF = w_t.shape[0] - NM
    ch = _tile(NM, 1024)

    def body(x_ref, g_ref, w_ref, pm_ref, fl_ref, h_ref):
        xv = x_ref[...]
        h = (xv * _rms(xv) * g_ref[...]).astype(BF16)
        h_ref[...] = h
        for c in range(0, NM, ch):
            pm_ref[:, c:c + ch] = _dot_nt(h, w_ref[c:c + ch, :]).astype(BF16)
        fl_ref[...] = _dot_nt(h, w_ref[NM:NM + NF, :])

    return pl.pallas_call(
        body, name="inproj_fwd", grid=(T // tm,),
        in_specs=[pl.BlockSpec((tm, D), lambda i: (i, 0)), _resident((1, D)), _resident(w_t.shape)],
        out_specs=[pl.BlockSpec((tm, NM), lambda i: (i, 0)), pl.BlockSpec((tm, NF), lambda i: (i, 0)),
                   pl.BlockSpec((tm, D), lambda i: (i, 0))],
        out_shape=[SDS((T, NM), BF16), SDS((T, NF), F32), SDS((T, D), BF16)],
        compiler_params=_cparams(1),
    )(x, g1, w_t)


def _log_sigmoid(x):
    return jnp.minimum(x, 0.0) - jnp.log(1.0 + jnp.exp(-jnp.abs(x)))


def _forget_cumsum(fl, bf, S):
    T, NF = fl.shape
    ch = _tile(S, 256, 8)

    def body(fl_ref, bf_ref, f_ref):
        row = lax.broadcasted_iota(jnp.int32, (ch, ch), 0)
        col = lax.broadcasted_iota(jnp.int32, (ch, ch), 1)
        tri = (col <= row).astype(BF16)
        carry = jnp.zeros((1, NF), F32)
        for c in range(0, S, ch):
            lf = _log_sigmoid(fl_ref[c:c + ch, :] + bf_ref[...])
            f_ref[c:c + ch, :] = _tri_dot(tri, lf) + carry
            carry = carry + jnp.sum(lf, axis=0, keepdims=True)

    return pl.pallas_call(
        body, name="forget_cumsum", grid=(T // S,),
        in_specs=[pl.BlockSpec((S, NF), lambda b: (b, 0)), _resident((1, NF))],
        out_specs=pl.BlockSpec((S, NF), lambda b: (b, 0)),
        out_shape=SDS((T, NF), F32),
        compiler_params=_cparams(1),
    )(fl, bf)


def _head_mask(e, hd):
    lanes = lax.broadcasted_iota(jnp.int32, (1, LANES), 1)
    return (lanes >= e * hd) & (lanes < (e + 1) * hd)


def _attn_specs(S, q_off, AW):
    ng = AW // LANES
    return [pl.BlockSpec((S, LANES), lambda b, g, o=q_off + w * ng: (b, o + g)) for w in range(3)]


def _attn_fwd(pm, fcum, frow, S, q_off, AW, hd, tq, gather):
    T = pm.shape[0]
    scale = float(hd) ** -0.5
    nq, hpg = S // tq, LANES // hd
    ng, nx = AW // LANES, len(gather)
    steps = (T // S) * ng

    def body(q_ref, k_ref, v_ref, fc_ref, fr_ref, *rest):
        x_refs, (o_ref, lse_ref), out_refs, sems = rest[:nx], rest[nx:nx + 2], rest[nx + 2:2 * nx + 2], rest[2 * nx + 2:]
        g = pl.program_id(1)
        step = pl.program_id(0) * ng + g
        start, forward, finish = _gather_phases(x_refs, out_refs, *sems)
        pl.when(step == 0)(start)

        @pl.when(g == 0)
        def _():
            lse_ref[...] = jnp.zeros_like(lse_ref)

        lanes = lax.broadcasted_iota(jnp.int32, (1, LANES), 1)
        for i in range(nq):
            rs, kend = slice(i * tq, (i + 1) * tq), (i + 1) * tq
            row = i * tq + lax.broadcasted_iota(jnp.int32, (tq, kend), 0)
            col = lax.broadcasted_iota(jnp.int32, (tq, kend), 1)
            o_tile = jnp.zeros((tq, LANES), F32)
            lse_tile = lse_ref[rs, :]
            for e in range(hpg):
                mask = _head_mask(e, hd)
                qs = jnp.where(mask, q_ref[rs, :], 0) * scale
                s = _dot_nt(qs, k_ref[0:kend, :]) + _lane_pick(fc_ref[rs, :], g * hpg + e) - fr_ref[e:e + 1, 0:kend]
                s = jnp.where(col <= row, s, -1e30)
                m = jnp.max(s, axis=1, keepdims=True)
                p = jnp.exp(s - m)
                l = jnp.sum(p, axis=1, keepdims=True)
                o_tile = jnp.where(mask, _dot(p.astype(BF16), v_ref[0:kend, :]) / l, o_tile)
                lse_tile = jnp.where(lanes == g * hpg + e, m + jnp.log(l), lse_tile)
            o_ref[rs, :] = o_tile.astype(BF16)
            lse_ref[rs, :] = lse_tile
        pl.when(step == (steps * 5) // 8)(forward)
        pl.when(step == steps - 1)(finish)

    full = pl.BlockSpec((S, LANES), lambda b, g: (b, 0))
    return pl.pallas_call(
        body, name="attn_fwd", grid=(T // S, ng),
        in_specs=_attn_specs(S, q_off, AW) + [full, pl.BlockSpec((None, None, 8, S), lambda b, g: (b, g, 0, 0))] + [ANY] * nx,
        out_specs=[pl.BlockSpec((S, LANES), lambda b, g: (b, g)), full] + [ANY] * nx,
        out_shape=[SDS((T, AW), BF16), SDS((T, LANES), F32)] + _gather_shapes(gather),
        scratch_shapes=_gather_sems(nx),
        compiler_params=_cparams(2),
    )(pm, pm, pm, fcum, frow, *gather)


def _merge_fwd(pm, o, x, wc, bg, woc_t, woa_t, wo, S, tm):
    T, D = x.shape
    CW, AW = woc_t.shape[1], woa_t.shape[1]
    tps = S // tm

    def body(cb_ref, cc_ref, cin_ref, gc_ref, ga_ref, o_ref, x_ref, wc_ref, bg_ref, woct_ref, woat_ref, wo_ref,
             x2_ref, u_ref, m_ref, ycin_ref, yc_ref, ya_ref, tail_ref):
        @pl.when(pl.program_id(0) % tps == 0)
        def _():
            tail_ref[...] = jnp.zeros_like(tail_ref)

        z = cc_ref[...].astype(F32) * cin_ref[...].astype(F32)
        z1, z2 = _taps_back(z, tail_ref[6:7, :], tail_ref[7:8, :])
        tail_ref[...] = z[tm - 8:tm, :]
        u = (z * wc_ref[2:3, :] + z2 * wc_ref[0:1, :]) + z1 * wc_ref[1:2, :]
        u_ref[...] = u.astype(BF16)
        ycin = (cb_ref[...].astype(F32) * u).astype(BF16)
        ycin_ref[...] = ycin
        yc = _dot_nt(ycin, woct_ref[...])
        ya = _dot_nt(o_ref[...], woat_ref[...])
        yc_ref[...] = yc.astype(BF16)
        ya_ref[...] = ya.astype(BF16)
        gc = _sigmoid(gc_ref[...].astype(F32) + bg_ref[:, 0:D])
        ga = _sigmoid(ga_ref[...].astype(F32) + bg_ref[:, D:2 * D])
        m = (gc * yc + ga * ya).astype(BF16)
        m_ref[...] = m
        x2_ref[...] = x_ref[...] + _dot(m, wo_ref[...])

    g_off = (3 * CW + 3 * AW) // D
    tok = lambda w, j=0: pl.BlockSpec((tm, w), lambda i: (i, j))
    return pl.pallas_call(
        body, name="merge_fwd", grid=(T // tm,),
        in_specs=[tok(CW, 0), tok(CW, 1), tok(CW, 2), tok(D, g_off), tok(D, g_off + 1), tok(AW), tok(D),
                  _resident((8, CW)), _resident((1, 2 * D)), _resident((D, CW)), _resident((D, AW)), _resident((D, D))],
        out_specs=[tok(D), tok(CW), tok(D), tok(CW), tok(D), tok(D)],
        out_shape=[SDS((T, D), F32), SDS((T, CW), BF16), SDS((T, D), BF16), SDS((T, CW), BF16),
                   SDS((T, D), BF16), SDS((T, D), BF16)],
        scratch_shapes=[pltpu.VMEM((8, CW), F32)],
        compiler_params=_cparams(1),
    )(pm, pm, pm, pm, pm, o, x, wc, bg, woc_t, woa_t, wo)


def _ffn_fwd(x2, g2, wup_t, wf, wdn, gf, tgt, S, tm):
    T, D = x2.shape
    FH = wdn.shape[0]
    ch = _tile(FH, FFN_CHUNK)
    tps = S // tm

    def body(x2_ref, g2_ref, wupt_ref, wf_ref, wdn_ref, gf_ref, tgt_ref,
             upp_ref, up_ref, act_ref, h2_ref, dx3_ref, loss_ref, dgf_ref, tail_ref):
        i = pl.program_id(0)

        @pl.when(i % tps == 0)
        def _():
            tail_ref[...] = jnp.zeros_like(tail_ref)

        @pl.when(i == 0)
        def _():
            loss_ref[...] = jnp.zeros_like(loss_ref)
            dgf_ref[...] = jnp.zeros_like(dgf_ref)

        x2v = x2_ref[...]
        h2 = (x2v * _rms(x2v) * g2_ref[...]).astype(BF16)
        h2_ref[...] = h2
        x3 = x2v
        for c in range(0, FH, ch):
            gated = []
            for cols in (slice(c, c + ch), slice(FH + c, FH + c + ch)):
                upp = _dot_nt(h2, wupt_ref[cols, :])
                upp_ref[:, cols] = upp.astype(BF16)
                p1, p2 = _taps_back(upp, tail_ref[6:7, cols], tail_ref[7:8, cols])
                tail_ref[:, cols] = upp[tm - 8:tm, :]
                up = (upp * wf_ref[2:3, cols] + p2 * wf_ref[0:1, cols]) + p1 * wf_ref[1:2, cols]
                up_ref[:, cols] = up.astype(BF16)
                gated.append(up)
            a, b = gated
            act = (a * _sigmoid(a) * b).astype(BF16)
            act_ref[:, c:c + ch] = act
            x3 = x3 + _dot(act, wdn_ref[c:c + ch, :])
        r3 = _rms(x3)
        xn3 = x3 * r3
        e = xn3 * gf_ref[...] - tgt_ref[...]
        loss_ref[...] += 0.5 * jnp.sum(jnp.mean(e * e, axis=-1, keepdims=True), axis=0, keepdims=True)
        dy = e / D
        dgf_ref[0:1, :] += jnp.sum(dy * xn3, axis=0, keepdims=True)
        dxn = dy * gf_ref[...]
        dx3_ref[...] = r3 * (dxn - xn3 * jnp.mean(dxn * xn3, axis=-1, keepdims=True))

    tok = lambda w: pl.BlockSpec((tm, w), lambda i: (i, 0))
    return pl.pallas_call(
        body, name="ffn_fwd", grid=(T // tm,),
        in_specs=[tok(D), _resident((1, D)), _resident((2 * FH, D)), _resident((8, 2 * FH)), _resident((FH, D)),
                  _resident((1, D)), tok(D)],
        out_specs=[tok(2 * FH), tok(2 * FH), tok(FH), tok(D), tok(D), pl.BlockSpec((8, LANES), lambda i: (0, 0)),
                   pl.BlockSpec((8, D), lambda i: (0, 0))],
        out_shape=[SDS((T, 2 * FH), BF16), SDS((T, 2 * FH), BF16), SDS((T, FH), BF16), SDS((T, D), BF16), SDS((T, D), F32),
                   SDS((8, LANES), F32), SDS((8, D), F32)],
        scratch_shapes=[pltpu.VMEM((8, 2 * FH), F32)],
        compiler_params=_cparams(1),
    )(x2, g2, wup_t, wf, wdn, gf, tgt)


def _ffn_bwd(dx3, x2, g2, upp, up, wf, wdn, wup_t, S, tm):
    T, D = x2.shape
    FH = wdn.shape[0]
    ch = _tile(FH, FFN_CHUNK)
    n, tps = T // tm, S // tm

    def body(dx3_ref, x2_ref, g2_ref, upp_ref, up_ref, wf_ref, wdn_ref, wupt_ref,
             dx2_ref, dupp_ref, dwf_ref, dg2_ref, head_ref):
        i = pl.program_id(0)
        t = n - 1 - i

        @pl.when(i == 0)
        def _():
            dwf_ref[...] = jnp.zeros_like(dwf_ref)
            dg2_ref[...] = jnp.zeros_like(dg2_ref)

        @pl.when(t % tps == tps - 1)
        def _():
            head_ref[...] = jnp.zeros_like(head_ref)

        dx3v = dx3_ref[...]
        dx3b = dx3v.astype(BF16)
        x2v = x2_ref[...]
        r2 = _rms(x2v)
        xn2 = x2v * r2
        dh2 = jnp.zeros((tm, D), F32)
        for c in range(0, FH, ch):
            ca, cb = slice(c, c + ch), slice(FH + c, FH + c + ch)
            a, b = up_ref[:, ca].astype(F32), up_ref[:, cb].astype(F32)
            sig = _sigmoid(a)
            sl = a * sig
            dact = _dot_nt(dx3b, wdn_ref[ca, :])
            grads = (dact * b * (sig * (1.0 + a * (1.0 - sig))), dact * sl)
            for cols, d in zip((ca, cb), grads):
                u0 = upp_ref[:, cols].astype(F32)
                d1, d2 = _taps_ahead(d, head_ref[0:1, cols], head_ref[1:2, cols])
                head_ref[:, cols] = d[0:8, :]
                dwf_ref[2:3, cols] += jnp.sum(u0 * d, axis=0, keepdims=True)
                dwf_ref[1:2, cols] += jnp.sum(u0 * d1, axis=0, keepdims=True)
                dwf_ref[0:1, cols] += jnp.sum(u0 * d2, axis=0, keepdims=True)
                dpre = ((d * wf_ref[2:3, cols] + d1 * wf_ref[1:2, cols]) + d2 * wf_ref[0:1, cols]).astype(BF16)
                dupp_ref[:, cols] = dpre
                dh2 = dh2 + _dot(dpre, wupt_ref[cols, :])
        dg2_ref[0:1, :] += jnp.sum(dh2 * xn2, axis=0, keepdims=True)
        dxn = dh2 * g2_ref[...]
        dx2_ref[...] = dx3v + r2 * (dxn - xn2 * jnp.mean(dxn * xn2, axis=-1, keepdims=True))

    tok = lambda w: pl.BlockSpec((tm, w), lambda i: (n - 1 - i, 0))
    acc = lambda w: pl.BlockSpec((8, w), lambda i: (0, 0))
    return pl.pallas_call(
        body, name="ffn_bwd", grid=(n,),
        in_specs=[tok(D), tok(D), _resident((1, D)), tok(2 * FH), tok(2 * FH), _resident((8, 2 * FH)),
                  _resident((FH, D)), _resident((2 * FH, D))],
        out_specs=[tok(D), tok(2 * FH), acc(2 * FH), acc(D)],
        out_shape=[SDS((T, D), F32), SDS((T, 2 * FH), BF16), SDS((8, 2 * FH), F32), SDS((8, D), F32)],
        scratch_shapes=[pltpu.VMEM((8, 2 * FH), F32)],
        compiler_params=_cparams(1),
    )(dx3, x2, g2, upp, up, wf, wdn, wup_t)


def _merge_bwd(dx2, pm, u, yc, ya, m, ycin, o, wc, bg, wo, woc_t, woa_t, S, tm, exchange):
    T, D = dx2.shape
    CW, AW = woc_t.shape[1], woa_t.shape[1]
    n, tps = T // tm, S // tm
    nx = len(exchange)
    rb = D // N_DEV
    assert CW + AW == D

    def body(dx2_ref, cb_ref, cc_ref, cin_ref, gc_ref, ga_ref, u_ref, yc_ref, ya_ref, m_ref, ycin_ref, o_ref,
             wc_ref, bg_ref, wo_ref, woct_ref, woat_ref, *rest):
        x_refs = rest[:nx]
        dconv_ref, dgl_ref, do_ref, dws_ref, dwc_ref, dbg_ref = rest[nx:nx + 6]
        got_refs, head_ref, sems = rest[nx + 6:2 * nx + 6], rest[2 * nx + 6], rest[2 * nx + 7:]

        def add_blocks(rows, cols, grad):
            for d in range(N_DEV):
                dws_ref[d, rows, cols] += grad[d * rb:(d + 1) * rb, :]

        i = pl.program_id(0)
        t = n - 1 - i

        @pl.when(i == 0)
        def _():
            for cp in _sibling_copies(x_refs, got_refs, *sems):
                cp.start()

        @pl.when(i == 0)
        def _():
            for ref in (dwc_ref, dbg_ref, dws_ref):
                ref[...] = jnp.zeros_like(ref)

        @pl.when(t % tps == tps - 1)
        def _():
            head_ref[...] = jnp.zeros_like(head_ref)

        dx2b = dx2_ref[...].astype(BF16)
        add_blocks(slice(0, rb), slice(0, D), _dot_tn(m_ref[...], dx2b))
        dm = _dot_nt(dx2b, wo_ref[...])
        outs = []
        for g_ref, y_ref, cols in ((gc_ref, yc_ref, slice(0, D)), (ga_ref, ya_ref, slice(D, 2 * D))):
            g = _sigmoid(g_ref[...].astype(F32) + bg_ref[:, cols])
            dgl = dm * y_ref[...].astype(F32) * g * (1.0 - g)
            dgl_ref[:, cols] = dgl.astype(BF16)
            dbg_ref[0:1, cols] += jnp.sum(dgl, axis=0, keepdims=True)
            outs.append((dm * g).astype(BF16))
        dyc, dya = outs
        add_blocks(slice(rb, 2 * rb), slice(0, CW), _dot_tn(dyc, ycin_ref[...]))
        add_blocks(slice(rb, 2 * rb), slice(CW, CW + AW), _dot_tn(dya, o_ref[...]))
        do_ref[...] = _dot(dya, woat_ref[...]).astype(BF16)
        dycin = _dot(dyc, woct_ref[...])
        cc, cin = cc_ref[...].astype(F32), cin_ref[...].astype(F32)
        z = cc * cin
        du = dycin * cb_ref[...].astype(F32)
        du1, du2 = _taps_ahead(du, head_ref[0:1, :], head_ref[1:2, :])
        head_ref[...] = du[0:8, :]
        dwc_ref[2:3, :] += jnp.sum(z * du, axis=0, keepdims=True)
        dwc_ref[1:2, :] += jnp.sum(z * du1, axis=0, keepdims=True)
        dwc_ref[0:1, :] += jnp.sum(z * du2, axis=0, keepdims=True)
        dz = (du * wc_ref[2:3, :] + du1 * wc_ref[1:2, :]) + du2 * wc_ref[0:1, :]
        dconv_ref[:, 0:CW] = (dycin * u_ref[...].astype(F32)).astype(BF16)
        dconv_ref[:, CW:2 * CW] = (dz * cin).astype(BF16)
        dconv_ref[:, 2 * CW:3 * CW] = (dz * cc).astype(BF16)

        @pl.when(i == n - 1)
        def _():
            for cp in _sibling_copies(x_refs, got_refs, *sems):
                cp.wait()

    g_off = (3 * CW + 3 * AW) // D
    tok = lambda w, j=0: pl.BlockSpec((tm, w), lambda i: (n - 1 - i, j))
    acc = lambda w: pl.BlockSpec((8, w), lambda i: (0, 0))
    return pl.pallas_call(
        body, name="merge_bwd", grid=(n,),
        in_specs=[tok(D), tok(CW, 0), tok(CW, 1), tok(CW, 2), tok(D, g_off), tok(D, g_off + 1), tok(CW), tok(D), tok(D),
                  tok(D), tok(CW), tok(AW),
                  _resident((8, CW)), _resident((1, 2 * D)), _resident((D, D)), _resident((D, CW)), _resident((D, AW))]
                 + [ANY] * nx,
        out_specs=[tok(3 * CW), tok(2 * D), tok(AW), pl.BlockSpec((N_DEV, 2 * rb, D), lambda i: (0, 0, 0)), acc(CW),
                   acc(2 * D)] + [ANY] * nx,
        out_shape=[SDS((T, 3 * CW), BF16), SDS((T, 2 * D), BF16), SDS((T, AW), BF16), SDS((N_DEV, 2 * rb, D), F32),
                   SDS((8, CW), F32), SDS((8, 2 * D), F32)] + _sibling_shapes(exchange),
        scratch_shapes=[pltpu.VMEM((8, CW), F32)] + _exchange_sems(nx),
        compiler_params=_cparams(1),
    )(dx2, pm, pm, pm, pm, pm, u, yc, ya, m, ycin, o, wc, bg, wo, woc_t, woa_t, *exchange)


def _attn_bwd(pm, do, fcum, frow, lse, S, q_off, AW, hd, tq, exchange):
    T = pm.shape[0]
    scale = float(hd) ** -0.5
    nq, hpg, ng = S // tq, LANES // hd, AW // LANES
    nx = len(exchange)
    steps = (T // S) * ng

    def body(q_ref, k_ref, v_ref, do_ref, fc_ref, fr_ref, lse_ref, *rest):
        x_refs, (dq_ref, dk_ref, dv_ref, dfr_ref) = rest[:nx], rest[nx:nx + 4]
        got_refs, (dk_acc, dv_acc), sems = rest[nx + 4:2 * nx + 4], rest[2 * nx + 4:2 * nx + 6], rest[2 * nx + 6:]
        g = pl.program_id(1)
        step = pl.program_id(0) * ng + g

        @pl.when(step == 0)
        def _():
            for cp in _chip_copies(x_refs, got_refs, *sems):
                cp.start()

        dk_acc[...] = jnp.zeros_like(dk_acc)
        dv_acc[...] = jnp.zeros_like(dv_acc)
        dfr_ref[...] = jnp.zeros_like(dfr_ref)
        for i in range(nq):
            rs, kend = slice(i * tq, (i + 1) * tq), (i + 1) * tq
            row = i * tq + lax.broadcasted_iota(jnp.int32, (tq, kend), 0)
            col = lax.broadcasted_iota(jnp.int32, (tq, kend), 1)
            kk, vv = k_ref[0:kend, :], v_ref[0:kend, :]
            dq_tile = jnp.zeros((tq, LANES), F32)
            for e in range(hpg):
                mask = _head_mask(e, hd)
                qs = jnp.where(mask, q_ref[rs, :], 0) * scale
                doi = jnp.where(mask, do_ref[rs, :], 0)
                s = _dot_nt(qs, kk) + _lane_pick(fc_ref[rs, :], g * hpg + e) - fr_ref[e:e + 1, 0:kend]
                p = jnp.where(col <= row, jnp.exp(s - _lane_pick(lse_ref[rs, :], g * hpg + e)), 0.0)
                dp = _dot_nt(doi, vv)
                ds = p * (dp - jnp.sum(p * dp, axis=1, keepdims=True))
                pb, dsb = p.astype(BF16), ds.astype(BF16)
                dq_tile = jnp.where(mask, _dot(dsb, kk) * scale, dq_tile)
                dv_acc[0:kend, :] += _dot_tn(pb, doi)
                dk_acc[0:kend, :] += _dot_tn(dsb, qs)
                dfr_ref[e:e + 1, 0:kend] -= jnp.sum(ds, axis=0, keepdims=True)
            dq_ref[rs, :] = dq_tile.astype(BF16)
        dk_ref[...] = dk_acc[...].astype(BF16)
        dv_ref[...] = dv_acc[...].astype(BF16)

        @pl.when(step == steps - 1)
        def _():
            for cp in _chip_copies(x_refs, got_refs, *sems):
                cp.wait()

    full = pl.BlockSpec((S, LANES), lambda b, g: (b, 0))
    grp = pl.BlockSpec((S, LANES), lambda b, g: (b, g))
    rows = pl.BlockSpec((None, None, 8, S), lambda b, g: (b, g, 0, 0))
    return pl.pallas_call(
        body, name="attn_bwd", grid=(T // S, ng),
        in_specs=_attn_specs(S, q_off, AW) + [grp, full, rows, full] + [ANY] * nx,
        out_specs=[grp, grp, grp, rows] + [ANY] * nx,
        out_shape=[SDS((T, AW), BF16), SDS((T, AW), BF16), SDS((T, AW), BF16), SDS((T // S, ng, 8, S), F32)]
                  + _chip_shapes(exchange),
        scratch_shapes=[pltpu.VMEM((S, LANES), F32), pltpu.VMEM((S, LANES), F32)] + _exchange_sems(3 * nx),
        compiler_params=_cparams(2),
    )(pm, pm, pm, do, fcum, frow, lse, *exchange)


def _forget_bwd(dfc, fl, bf, S):
    T, NF = fl.shape
    ch = _tile(S, 256, 8)

    def body(df_ref, fl_ref, bf_ref, dfl_ref, dbf_ref):
        @pl.when(pl.program_id(0) == 0)
        def _():
            dbf_ref[...] = jnp.zeros_like(dbf_ref)

        row = lax.broadcasted_iota(jnp.int32, (ch, ch), 0)
        col = lax.broadcasted_iota(jnp.int32, (ch, ch), 1)
        tri = (col >= row).astype(BF16)
        carry = jnp.zeros((1, NF), F32)
        for c in range(S - ch, -1, -ch):
            d = df_ref[c:c + ch, :]
            dlf = _tri_dot(tri, d) + carry
            carry = carry + jnp.sum(d, axis=0, keepdims=True)
            dfl = dlf * _sigmoid(-(fl_ref[c:c + ch, :] + bf_ref[...]))
            dfl_ref[c:c + ch, :] = dfl.astype(BF16)
            dbf_ref[0:1, :] += jnp.sum(dfl, axis=0, keepdims=True)

    return pl.pallas_call(
        body, name="forget_bwd", grid=(T // S,),
        in_specs=[pl.BlockSpec((S, NF), lambda b: (b, 0)), pl.BlockSpec((S, NF), lambda b: (b, 0)), _resident((1, NF))],
        out_specs=[pl.BlockSpec((S, NF), lambda b: (b, 0)), pl.BlockSpec((8, NF), lambda b: (0, 0))],
        out_shape=[SDS((T, NF), BF16), SDS((8, NF), F32)],
        compiler_params=_cparams(1),
    )(dfc, fl, bf)


def _inproj_bwd(dparts, offs, w_t, x, g1, dx2, tm, exchange):
    T, D = x.shape
    npart, nx = len(dparts), len(exchange)
    n = T // tm

    def body(*refs):
        d_refs = refs[:npart]
        w_ref, x_ref, g_ref, dx2_ref = refs[npart:npart + 4]
        x_refs = refs[npart + 4:npart + 4 + nx]
        dx_ref, dg_ref = refs[npart + 4 + nx:npart + 6 + nx]
        got_refs, sems = refs[npart + 6 + nx:npart + 6 + 2 * nx], refs[npart + 6 + 2 * nx:]

        @pl.when(pl.program_id(0) == 0)
        def _():
            for cp in _chip_copies(x_refs, got_refs, *sems):
                cp.start()

        @pl.when(pl.program_id(0) == 0)
        def _():
            dg_ref[...] = jnp.zeros_like(dg_ref)

        dh = None
        for d_ref, off in zip(d_refs, offs):
            term = _dot(d_ref[...], w_ref[off:off + d_ref.shape[1], :])
            dh = term if dh is None else dh + term
        xv = x_ref[...]
        r = _rms(xv)
        xn = xv * r
        dg_ref[0:1, :] += jnp.sum(dh * xn, axis=0, keepdims=True)
        dxn = dh * g_ref[...]
        dx_ref[...] = dx2_ref[...] + r * (dxn - xn * jnp.mean(dxn * xn, axis=-1, keepdims=True))

        @pl.when(pl.program_id(0) == n - 1)
        def _():
            for cp in _chip_copies(x_refs, got_refs, *sems):
                cp.wait()

    tok = lambda w: pl.BlockSpec((tm, w), lambda i: (i, 0))
    return pl.pallas_call(
        body, name="inproj_bwd", grid=(n,),
        in_specs=[tok(d.shape[1]) for d in dparts] + [_resident(w_t.shape), tok(D), _resident((1, D)), tok(D)] + [ANY] * nx,
        out_specs=[tok(D), pl.BlockSpec((8, D), lambda i: (0, 0))] + [ANY] * nx,
        out_shape=[SDS((T, D), F32), SDS((8, D), F32)] + _chip_shapes(exchange),
        scratch_shapes=_exchange_sems(3 * nx),
        compiler_params=_cparams(1),
    )(*dparts, w_t, x, g1, dx2, *exchange)


def _wgrad(b, a, name, into=None, row_off=0, total_rows=None):
    T, N = b.shape
    M = a.shape[1]
    tn = _tile(N, 1408 if M <= 1024 else 512)
    while row_off % tn:
        tn = _tile(N, tn - LANES)
    tk = _tile(T, 512, 16)
    per_row = 2 * (tn * b.dtype.itemsize + M * a.dtype.itemsize)
    while T % (2 * tk) == 0 and 2 * tk * per_row + 2 * tn * M * 4 <= WGRAD_VMEM_BUDGET:
        tk *= 2
    blk0 = row_off // tn

    def body(b_ref, a_ref, *rest):
        o_ref = rest[-1]

        @pl.when(pl.program_id(1) == 0)
        def _():
            o_ref[...] = jnp.zeros_like(o_ref)

        o_ref[...] += _dot_tn(b_ref[...].astype(BF16), a_ref[...].astype(BF16))

    in_specs = [pl.BlockSpec((tk, tn), lambda j, k: (k, j)), pl.BlockSpec((tk, M), lambda j, k: (k, 0))]
    args = (b, a)
    kwargs = {}
    if into is not None:
        in_specs.append(ANY)
        args += (into,)
        kwargs["input_output_aliases"] = {2: 0}
        total_rows = into.shape[0]
    return pl.pallas_call(
        body, name=name, grid=(N // tn, T // tk),
        in_specs=in_specs,
        out_specs=pl.BlockSpec((tn, M), lambda j, k: (blk0 + j, 0)),
        out_shape=SDS((N if total_rows is None else total_rows, M), F32),
        compiler_params=_cparams(2), **kwargs,
    )(*args)


def _adamw(w, g, m, v, name):
    shape = w.shape
    C = shape[-1]
    w2, g2, m2, v2 = (a.reshape(-1, C) for a in (w, g, m, v))
    R = w2.shape[0]
    tr = R if R <= 512 else _tile(R, 256, 8)
    if tr < 64:
        tr = R

    def body(w_ref, g_ref, m_ref, v_ref, d_ref, nm_ref, nv_ref):
        gv = g_ref[...]
        mv = ADAM_B1 * m_ref[...] + (1.0 - ADAM_B1) * gv
        vv = ADAM_B2 * v_ref[...] + (1.0 - ADAM_B2) * (gv * gv)
        m_hat = mv / (1.0 - ADAM_B1 ** ADAM_STEP)
        v_hat = vv / (1.0 - ADAM_B2 ** ADAM_STEP)
        d_ref[...] = -ADAM_LR * (m_hat / (jnp.sqrt(v_hat) + ADAM_EPS) + ADAM_WD * w_ref[...])
        nm_ref[...] = mv
        nv_ref[...] = vv

    spec = pl.BlockSpec((tr, C), lambda r: (r, 0))
    outs = pl.pallas_call(
        body, name=name, grid=(R // tr,),
        in_specs=[spec] * 4, out_specs=[spec] * 3, out_shape=[SDS((R, C), F32)] * 3,
        compiler_params=_cparams(1),
    )(w2, g2, m2, v2)
    return tuple(o.reshape(shape) for o in outs)


def _rows(a, L):
    lead = a.shape[0]
    flat = a.reshape(lead, -1)
    n = flat.shape[1]
    r = -(-n // L)
    return jnp.pad(flat, ((0, 0), (0, r * L - n))).reshape(lead, r, L)


def _unrows(p, shape):
    return p.reshape(-1)[:int(np.prod(shape))].reshape(shape)


def _from_col_blocks(a):
    n, R, c = a.shape
    return a.transpose(1, 0, 2).reshape(R, n * c)


def _in_windows(n_loc, nqkv, H, NM):
    win = (n_loc + 7 + 7) // 8 * 8
    starts, index = [], np.zeros((N_DEV, n_loc), np.int32)
    for d in range(N_DEV):
        rows = np.arange(n_loc * d, n_loc * (d + 1))
        is_f = (rows >= nqkv) & (rows < nqkv + H)
        kept = np.where(rows < nqkv, rows, rows - H)
        lo = int(kept[~is_f].min())
        start = lo // 8 * 8
        assert int(kept[~is_f].max()) - start < win and start + win <= NM + LANES
        starts.append(start)
        index[d] = np.where(is_f, win + rows - nqkv, kept - start)
    return starts, win, index


def kernel(x, norm_mix_g, w_in, b_f, b_gate, conv_mix_w, w_out_conv, w_out_attn, w_o, norm_ffn_g, w_up, conv_ffn_w, w_down, norm_f_g, loss_target, m_norm_mix_g, m_w_in, m_b_f, m_b_gate, m_conv_mix_w, m_w_out_conv, m_w_out_attn, m_w_o, m_norm_ffn_g, m_w_up, m_conv_ffn_w, m_w_down, m_norm_f_g, v_norm_mix_g, v_w_in, v_b_f, v_b_gate, v_conv_mix_w, v_w_out_conv, v_w_out_attn, v_w_o, v_norm_ffn_g, v_w_up, v_conv_ffn_w, v_w_down, v_norm_f_g):
    Bl, S, D = x.shape
    T = Bl * S
    H = b_f.shape[-1]
    CW = N_DEV * conv_mix_w.shape[-1]
    AW = w_out_attn.shape[1]
    hd = AW // H
    FH = N_DEV * w_down.shape[1]
    n_loc = w_in.shape[-1]
    NIN = N_DEV * n_loc
    NM = 3 * CW + 3 * AW + 2 * D
    nqkv = 3 * CW + 3 * AW
    assert NIN == NM + H and w_out_conv.shape[1] == CW and nqkv % D == 0 and AW % LANES == 0 and LANES % hd == 0
    hpg, ng = LANES // hd, AW // LANES
    assert hpg <= 8
    tm_big = min(512, S // 2)
    tm_ffn = min(256, S // 2)
    tq_fwd = min(512, S // 2)
    tq_bwd = min(256, S // 2)
    px, py, pc = _place()
    me = 4 * px + 2 * py + pc

    bits = lambda a: lax.bitcast_convert_type(a, BF16)
    taps = jnp.concatenate([_rows(bits(conv_ffn_w[0])[None], D)[0], _rows(bits(conv_mix_w[0])[None], D)[0]], axis=0)
    n_ffn_rows = -(-conv_ffn_w[0].size * 2 // D)
    late = [w_up[0].T.astype(BF16), w_down[0].astype(BF16), w_o[0].astype(BF16), w_out_conv[0].T.astype(BF16),
            w_out_attn[0].T.astype(BF16), taps]
    g_in, = _all_gather([w_in[0].T.astype(BF16)], "weights_all_gather")
    W_in_rows = g_in.reshape(NIN, D)
    W_in_t = jnp.concatenate([W_in_rows[:nqkv], W_in_rows[nqkv + H:], W_in_rows[nqkv:nqkv + H],
                              jnp.zeros((LANES - H, D), BF16)], axis=0)
    bf128 = jnp.pad(b_f, ((0, 0), (0, LANES - H)))

    x2d = x.reshape(T, D)
    tgt = loss_target.reshape(T, D)
    pm, fl, h1 = _inproj_fwd(x2d, norm_mix_g, W_in_t, NM, tm_big)
    fcum = _forget_cumsum(fl, bf128, S)
    frow = jnp.pad(fcum[:, :H].reshape(Bl, S, ng, hpg).transpose(0, 2, 3, 1), ((0, 0), (0, 0), (0, 8 - hpg), (0, 0)))
    q_off = 3 * CW // LANES
    o, lse, g_up, g_dn, g_o, g_oc, g_oa, g_taps = _attn_fwd(pm, fcum, frow, S, q_off, AW, hd, tq_fwd, late)
    W_up_t = g_up.reshape(2 * FH, D)
    W_dn = g_dn.reshape(FH, D)
    W_o = g_o.reshape(D, D)
    W_oc_t = g_oc.reshape(D, CW)
    W_oa_t = g_oa.reshape(D, AW)
    tap_bits = g_taps.reshape(N_DEV, -1)
    n_ffn, n_mix = conv_ffn_w[0].size * 2, conv_mix_w[0].size * 2
    wf_full = _from_col_blocks(lax.bitcast_convert_type(
        tap_bits[:, :n_ffn].reshape((N_DEV,) + conv_ffn_w.shape[1:] + (2,)), F32))
    wc_full = _from_col_blocks(lax.bitcast_convert_type(
        tap_bits[:, n_ffn_rows * D:n_ffn_rows * D + n_mix].reshape((N_DEV,) + conv_mix_w.shape[1:] + (2,)), F32))
    wf8 = jnp.pad(wf_full, ((0, 5), (0, 0)))
    wc8 = jnp.pad(wc_full, ((0, 5), (0, 0)))
    x2, u, m, ycin, yc, ya = _merge_fwd(pm, o, x2d, wc8, b_gate, W_oc_t, W_oa_t, W_o, S, tm_big)
    upp, up, act, h2, dx3, loss8, dgf8 = _ffn_fwd(x2, norm_ffn_g, W_up_t, wf8, W_dn, norm_f_g.reshape(1, D), tgt, S, tm_ffn)

    dx2, dupp, dwf8, dg2_8 = _ffn_bwd(dx3, x2, norm_ffn_g, upp, up, wf8, W_dn, W_up_t, S, tm_ffn)
    dW_dn = _wgrad(act, dx3, "wgrad_down")
    dW_up_t = _wgrad(dupp, h2, "wgrad_up")
    ids = jnp.stack([pc, 2 * px + py]).astype(jnp.int32)
    big = [dW_up_t.reshape(4, 2, -1, D), dW_dn.reshape(4, 2, -1, D)]
    dconv, dgl, do, dW_small, dwc8, dbg8, *sib_big = _merge_bwd(
        dx2, pm, u, yc, ya, m, ycin, o, wc8, b_gate, W_o, W_oc_t, W_oa_t, S, tm_big, big)
    big_sums = [_pair_sum(b, r, ids, "grads_pair_sum_%d" % a) for a, (b, r) in enumerate(zip(big, sib_big))]
    dq, dk, dv, dfr, *chips_big = _attn_bwd(pm, do, fcum, frow, lse, S, q_off, AW, hd, tq_bwd, [s[1] for s in big_sums])
    dfc = jnp.pad(dfr[:, :, :hpg, :].transpose(0, 3, 1, 2).reshape(T, H), ((0, 0), (0, LANES - H)))
    dfl, dbf8 = _forget_bwd(dfc, fl, bf128, S)
    dparts = [dconv, dq, dk, dv, dgl, dfl]
    offs = [0, 3 * CW, 3 * CW + AW, 3 * CW + 2 * AW, nqkv, NM]
    dW_in_t = _wgrad(dparts[0], h1, "wgrad_in_0", total_rows=NM + LANES)
    for k in range(1, len(dparts)):
        dW_in_t = _wgrad(dparts[k], h1, "wgrad_in_%d" % k, into=dW_in_t, row_off=offs[k])

    starts, win, index = _in_windows(n_loc, nqkv, H, NM)
    small = dW_small.reshape(4, 2, -1, D)
    my_starts = [jnp.where(pc == 0, starts[2 * j], starts[2 * j + 1]) for j in range(4)]
    win_ids = jnp.stack([pc, 2 * px + py] + my_starts).astype(jnp.int32)
    sib_win, sib_small = _sibling_exchange(dW_in_t, starts, win, [small], "grads_sibling_exchange")
    sums = [_pair_sum(dW_in_t, sib_win, win_ids, "grads_pair_sum_in", win_rows=win),
            _pair_sum(small, sib_small, ids, "grads_pair_sum_small")]
    grad_x, dg1_8, *from_chips = _inproj_bwd(dparts, offs, W_in_t, x2d, norm_mix_g, dx2, tm_big, [s[1] for s in sums])
    red_win, red_small, red_up, red_dn = [
        _final_sum(s[0], r, "grads_final_sum_%d" % a)
        for a, (s, r) in enumerate(zip(sums + big_sums, list(from_chips) + list(chips_big)))]

    f_rows = dW_in_t[NM:NM + 8]
    wf_rows = _rows(dwf8[:3].reshape(1, -1), D)[0]
    wc_rows = _rows(dwc8[:3].reshape(1, -1), D)[0]
    smalls = [dg1_8[0:1], dg2_8[0:1], dgf8[0:1], dbg8[0:1, :D], dbg8[0:1, D:], jnp.pad(dbf8[0:1], ((0, 0), (0, D - LANES))),
              jnp.pad(loss8[0:1], ((0, 0), (0, D - LANES))), jnp.zeros((1, D), F32), f_rows, wf_rows, wc_rows]
    spack = jnp.concatenate(smalls, axis=0)
    spack = jnp.pad(spack, ((0, -spack.shape[0] % 8), (0, 0)))
    ssum = _sum8(_all_gather([spack], "small_all_gather")[0], "small_sum")
    g_g1, g_g2, g_gf = ssum[0:1], ssum[1:2], ssum[2]
    g_bg = jnp.concatenate([ssum[3:4], ssum[4:5]], axis=1)
    g_bf = ssum[5:6, :H]
    r0 = 16
    r1 = r0 + wf_rows.shape[0]
    wf_sum = _unrows(ssum[r0:r1], (3, 2 * FH))
    wc_sum = _unrows(ssum[r1:r1 + wc_rows.shape[0]], (3, CW))
    g_wf = lax.dynamic_slice_in_dim(wf_sum, me * conv_ffn_w.shape[-1], conv_ffn_w.shape[-1], axis=1)
    g_wc = lax.dynamic_slice_in_dim(wc_sum, me * conv_mix_w.shape[-1], conv_mix_w.shape[-1], axis=1)

    ext = jnp.concatenate([red_win, ssum[8:8 + H]], axis=0)
    my_index = lax.dynamic_index_in_dim(jnp.asarray(index), me, axis=0, keepdims=False)
    g_w_in_t = jnp.take(ext, my_index, axis=0)
    rb = D // N_DEV
    g_w_o = red_small[:rb]
    g_w_oc = red_small[rb:, :CW].T
    g_w_oa = red_small[rb:, CW:].T

    loss = ssum[6, 0]

    names = ["norm_mix_g", "w_in", "b_f", "b_gate", "conv_mix_w", "w_out_conv", "w_out_attn", "w_o", "norm_ffn_g", "w_up",
             "conv_ffn_w", "w_down", "norm_f_g"]
    weights = [norm_mix_g, w_in, b_f, b_gate, conv_mix_w, w_out_conv, w_out_attn, w_o, norm_ffn_g, w_up, conv_ffn_w, w_down, norm_f_g]
    grads = [g_g1, g_w_in_t, g_bf, g_bg, g_wc, g_w_oc, g_w_oa, g_w_o, g_g2, red_up, g_wf, red_dn, g_gf]
    ms = [m_norm_mix_g, m_w_in, m_b_f, m_b_gate, m_conv_mix_w, m_w_out_conv, m_w_out_attn, m_w_o, m_norm_ffn_g, m_w_up,
          m_conv_ffn_w, m_w_down, m_norm_f_g]
    vs = [v_norm_mix_g, v_w_in, v_b_f, v_b_gate, v_conv_mix_w, v_w_out_conv, v_w_out_attn, v_w_o, v_norm_ffn_g, v_w_up,
          v_conv_ffn_w, v_w_down, v_norm_f_g]
    to_view = {"w_in": lambda a: a[0].T.reshape(-1, LANES), "w_up": lambda a: a[0].T}
    from_view = {"w_in": lambda a: a.reshape(n_loc, D).T[None], "w_up": lambda a: a.T[None]}
    out_grads, steps = [], []
    for nm, w, g, mm, vv in zip(names, weights, grads, ms, vs):
        if nm in to_view:
            wv, mv, vw = (to_view[nm](a) for a in (w, mm, vv))
            gv = g.reshape(wv.shape)
            steps.append(tuple(from_view[nm](o) for o in _adamw(wv, gv, mv, vw, "adamw_" + nm)))
            out_grads.append(from_view[nm](gv))
        else:
            gv = g.reshape(w.shape)
            steps.append(_adamw(w, gv, mm, vv, "adamw_" + nm))
            out_grads.append(gv)
    deltas, new_ms, new_vs = zip(*steps)
    return (loss, grad_x.reshape(Bl, S, D), *out_grads, *deltas, *new_ms, *new_vs)
```

```python
import numpy as np

import jax
import jax.numpy as jnp
from jax import lax
from jax.experimental import pallas as pl
from jax.experimental.pallas import tpu as pltpu

F32, BF16 = jnp.float32, jnp.bfloat16
EPS = 1e-6
ADAM_LR, ADAM_B1, ADAM_B2, ADAM_EPS, ADAM_WD, ADAM_STEP = 0.001, 0.9, 0.999, 1e-08, 0.01, 10
N_DEV = 8
LANES = 128
V7X_VMEM_LIMIT = 56 * 1024 * 1024
FFN_CHUNK = 2816
WGRAD_VMEM_BUDGET = 40 * 1024 * 1024
MESH = pl.DeviceIdType.MESH
SDS = jax.ShapeDtypeStruct
ANY = pl.BlockSpec(memory_space=pl.ANY)


def _tile(n, target, mult=LANES):
    best = None
    for t in range(mult, min(n, target) + 1, mult):
        if n % t == 0:
            best = t
    return best if best is not None else n


def _resident(shape):
    return pl.BlockSpec(shape, lambda *_: (0,) * len(shape), pipeline_mode=pl.Buffered(1))


def _cparams(n_axes=1):
    return pltpu.CompilerParams(dimension_semantics=("arbitrary",) * n_axes, vmem_limit_bytes=V7X_VMEM_LIMIT)


def _dot(a, b):
    return jnp.dot(a, b, preferred_element_type=F32)


def _dot_tn(a, b):
    return lax.dot_general(a, b, (((0,), (0,)), ((), ())), preferred_element_type=F32)


def _dot_nt(a, b):
    return lax.dot_general(a, b, (((1,), (1,)), ((), ())), preferred_element_type=F32)


def _sigmoid(x):
    return 0.5 * jnp.tanh(0.5 * x) + 0.5


def _rms(x):
    return lax.rsqrt(jnp.mean(x * x, axis=-1, keepdims=True) + EPS)


def _taps_back(z, r6, r7):
    row = lax.broadcasted_iota(jnp.int32, (8, 1), 0)
    z1, z2 = pltpu.roll(z, 1, 0), pltpu.roll(z, 2, 0)
    z1 = jnp.concatenate([jnp.where(row == 0, r7, z1[0:8]), z1[8:]], axis=0)
    z2 = jnp.concatenate([jnp.where(row == 0, r6, jnp.where(row == 1, r7, z2[0:8])), z2[8:]], axis=0)
    return z1, z2


def _taps_ahead(d, h0, h1):
    tm = d.shape[0]
    row = lax.broadcasted_iota(jnp.int32, (8, 1), 0)
    d1, d2 = pltpu.roll(d, tm - 1, 0), pltpu.roll(d, tm - 2, 0)
    d1 = jnp.concatenate([d1[:tm - 8], jnp.where(row == 7, h0, d1[tm - 8:])], axis=0)
    d2 = jnp.concatenate([d2[:tm - 8], jnp.where(row == 6, h0, jnp.where(row == 7, h1, d2[tm - 8:]))], axis=0)
    return d1, d2


def _tri_dot(tri, x):
    hi = x.astype(BF16)
    r = x - hi.astype(F32)
    mid = r.astype(BF16)
    lo = (r - mid.astype(F32)).astype(BF16)
    return (_dot(tri, lo) + _dot(tri, mid)) + _dot(tri, hi)


def _lane_pick(block, lane):
    lanes = lax.broadcasted_iota(jnp.int32, (1, block.shape[1]), 1)
    return jnp.sum(jnp.where(lanes == lane, block, 0.0), axis=1, keepdims=True)


def _place():
    return lax.axis_index("x"), lax.axis_index("y"), lax.axis_index("c")


def _all_gather(xs, name):
    n = len(xs)

    def body(*refs):
        start, forward, finish = _gather_phases(refs[:n], refs[n:2 * n], *refs[2 * n:])
        start()
        forward()
        finish()

    return pl.pallas_call(
        body, name=name,
        out_shape=_gather_shapes(xs), in_specs=[ANY] * n, out_specs=[ANY] * n, scratch_shapes=_gather_sems(n),
    )(*xs)


def _gather_shapes(xs):
    return [SDS((N_DEV,) + x.shape, x.dtype) for x in xs]


def _gather_sems(n):
    return [pltpu.SemaphoreType.DMA((7 * n,)), pltpu.SemaphoreType.DMA((7 * n,)), pltpu.SemaphoreType.DMA((n,))]


def _gather_phases(x_refs, out_refs, send_sems, recv_sems, local_sems):
    n = len(x_refs)

    def parts():
        px, py, pc = _place()
        me, sibling = (px, py, pc), (px, py, 1 - pc)
        chips = [(1 - px, py), (px, 1 - py), (1 - px, 1 - py)]

        def slot(a, qx, qy, qc):
            return out_refs[a].at[4 * qx + 2 * qy + qc]

        def copy(a, k, block, to, src=None):
            return pltpu.make_async_remote_copy(
                src_ref=slot(a, *block) if src is None else src, dst_ref=slot(a, *block),
                send_sem=send_sems.at[7 * a + k], recv_sem=recv_sems.at[7 * a + k], device_id=to, device_id_type=MESH)

        def mine():
            return [pltpu.make_async_copy(x_refs[a], slot(a, *me), local_sems.at[a]) for a in range(n)]

        def first():
            out = []
            for a in range(n):
                out.append(copy(a, 0, me, sibling, src=x_refs[a]))
                out += [copy(a, 1 + j, me, (*chip, pc), src=x_refs[a]) for j, chip in enumerate(chips)]
            return out

        def landed():
            return [copy(a, 1 + j, (*chip, pc), me) for j, chip in enumerate(chips) for a in range(n)]

        def passed():
            return [copy(a, 4 + j, (*chip, pc), sibling) for j, chip in enumerate(chips) for a in range(n)]

        def late():
            out = [copy(a, 0, sibling, me) for a in range(n)]
            return out + [copy(a, 4 + j, (*chip, 1 - pc), me) for j, chip in enumerate(chips) for a in range(n)]

        return mine, first, landed, passed, late

    def start():
        mine, first, _, _, _ = parts()
        for cp in mine() + first():
            cp.start()

    def forward():
        _, _, landed, passed, _ = parts()
        for got, cp in zip(landed(), passed()):
            got.wait_recv()
            cp.start()

    def finish():
        mine, first, _, passed, late = parts()
        for cp in late():
            cp.wait_recv()
        for cp in first() + passed():
            cp.wait_send()
        for cp in mine():
            cp.wait()

    return start, forward, finish


def _sibling_exchange(win_buf, win_starts, win_rows, blocked, name):
    nb = len(blocked)

    def body(*refs):
        win_ref, blk_refs = refs[0], refs[1:1 + nb]
        rwin_ref, rblk_refs = refs[1 + nb], refs[2 + nb:2 + 2 * nb]
        send_sems, recv_sems, wsend_sems, wrecv_sems = refs[2 + 2 * nb:]
        px, py, pc = _place()
        copies = _sibling_copies(blk_refs, rblk_refs, send_sems, recv_sems)
        for j in range(4):
            theirs = jnp.where(pc == 0, win_starts[2 * j + 1], win_starts[2 * j])
            copies.append(pltpu.make_async_remote_copy(
                src_ref=win_ref.at[pl.ds(pl.multiple_of(theirs, 8), win_rows)], dst_ref=rwin_ref.at[j],
                send_sem=wsend_sems.at[j], recv_sem=wrecv_sems.at[j], device_id=(px, py, 1 - pc), device_id_type=MESH))
        for cp in copies:
            cp.start()
        for cp in copies:
            cp.wait()

    C = win_buf.shape[1]
    return pl.pallas_call(
        body, name=name,
        out_shape=[SDS((4, win_rows, C), F32)] + _sibling_shapes(blocked),
        in_specs=[ANY] * (1 + nb), out_specs=[ANY] * (1 + nb),
        scratch_shapes=_exchange_sems(nb) + _exchange_sems(4),
    )(win_buf, *blocked)


def _sibling_shapes(blocked):
    return [SDS((4,) + b.shape[2:], F32) for b in blocked]


def _exchange_sems(n):
    return [pltpu.SemaphoreType.DMA((n,)), pltpu.SemaphoreType.DMA((n,))]


def _sibling_copies(blk_refs, out_refs, send_sems, recv_sems):
    px, py, pc = _place()
    return [pltpu.make_async_remote_copy(
        src_ref=b.at[:, 1 - pc], dst_ref=o, send_sem=send_sems.at[a], recv_sem=recv_sems.at[a],
        device_id=(px, py, 1 - pc), device_id_type=MESH) for a, (b, o) in enumerate(zip(blk_refs, out_refs))]


def _chip_shapes(ps):
    return [SDS((3,) + p.shape[1:], p.dtype) for p in ps]


def _chip_copies(p_refs, out_refs, send_sems, recv_sems):
    px, py, pc = _place()
    n = len(p_refs)
    chips = [(1 - px, py), (px, 1 - py), (1 - px, 1 - py)]
    return [pltpu.make_async_remote_copy(
        src_ref=p_refs[a].at[2 * qx + qy], dst_ref=out_refs[a].at[k],
        send_sem=send_sems.at[3 * a + k], recv_sem=recv_sems.at[3 * a + k],
        device_id=(qx, qy, pc), device_id_type=MESH) for k, (qx, qy) in enumerate(chips) for a in range(n)]


def _pair_sum(own, recv, ids, name, win_rows=None):
    _, R, C = recv.shape
    tr = _tile(R, 512, 8)

    def body(ids_ref, g_ref, r_ref, own_ref, pb_ref):
        s = g_ref[...] + r_ref[...]
        pb_ref[...] = s.astype(BF16)

        @pl.when(pl.program_id(1) == ids_ref[1])
        def _():
            own_ref[...] = s

    if win_rows is None:
        own_spec = pl.BlockSpec((None, None, tr, C), lambda r, j, ids: (j, ids[0], r, 0))
    else:
        own_spec = pl.BlockSpec((pl.Element(tr), pl.Element(C)), lambda r, j, ids: (pl.multiple_of(ids[2 + j] + r * tr, 8), 0))
    return pl.pallas_call(
        body, name=name,
        grid_spec=pltpu.PrefetchScalarGridSpec(
            num_scalar_prefetch=1, grid=(R // tr, 4),
            in_specs=[own_spec, pl.BlockSpec((None, tr, C), lambda r, j, ids: (j, r, 0))],
            out_specs=[pl.BlockSpec((tr, C), lambda r, j, ids: (r, 0)),
                       pl.BlockSpec((None, tr, C), lambda r, j, ids: (j, r, 0))]),
        out_shape=[SDS((R, C), F32), SDS((4, R, C), BF16)],
        compiler_params=_cparams(2),
    )(ids, own, recv)


def _final_sum(own, recv, name):
    R, C = own.shape
    tr = _tile(R, 512, 8)

    def body(o_ref, r_ref, out_ref):
        out_ref[...] = ((o_ref[...] + r_ref[0].astype(F32)) + r_ref[1].astype(F32)) + r_ref[2].astype(F32)

    return pl.pallas_call(
        body, name=name, grid=(R // tr,),
        in_specs=[pl.BlockSpec((tr, C), lambda r: (r, 0)), pl.BlockSpec((3, tr, C), lambda r: (0, r, 0))],
        out_specs=pl.BlockSpec((tr, C), lambda r: (r, 0)),
        out_shape=SDS((R, C), F32),
        compiler_params=_cparams(1),
    )(own, recv)


def _sum8(a, name):
    def body(a_ref, out_ref):
        s = a_ref[0]
        for d in range(1, N_DEV):
            s = s + a_ref[d]
        out_ref[...] = s

    return pl.pallas_call(body, name=name, out_shape=SDS(a.shape[1:], F32))(a)


def _inproj_fwd(x, g1, w_t, NM, tm):
    T, D = x.shape
    NF = w_t.shape[0] - NM
    ch = _tile(NM, 1024)

    def body(x_ref, g_ref, w_ref, pm_ref, fl_ref, h_ref):
        xv = x_ref[...]
        h = (xv * _rms(xv) * g_ref[...]).astype(BF16)
        h_ref[...] = h
        for c in range(0, NM, ch):
            pm_ref[:, c:c + ch] = _dot_nt(h, w_ref[c:c + ch, :]).astype(BF16)
        fl_ref[...] = _dot_nt(h, w_ref[NM:NM + NF, :])

    return pl.pallas_call(
        body, name="inproj_fwd", grid=(T // tm,),
        in_specs=[pl.BlockSpec((tm, D), lambda i: (i, 0)), _resident((1, D)), _resident(w_t.shape)],
        out_specs=[pl.BlockSpec((tm, NM), lambda i: (i, 0)), pl.BlockSpec((tm, NF), lambda i: (i, 0)),
                   pl.BlockSpec((tm, D), lambda i: (i, 0))],
        out_shape=[SDS((T, NM), BF16), SDS((T, NF), F32), SDS((T, D), BF16)],
        compiler_params=_cparams(1),
    )(x, g1, w_t)


def _log_sigmoid(x):
    return jnp.minimum(x, 0.0) - jnp.log(1.0 + jnp.exp(-jnp.abs(x)))


def _forget_cumsum(fl, bf, S):
    T, NF = fl.shape
    ch = _tile(S, 256, 8)

    def body(fl_ref, bf_ref, f_ref):
        row = lax.broadcasted_iota(jnp.int32, (ch, ch), 0)
        col = lax.broadcasted_iota(jnp.int32, (ch, ch), 1)
        tri = (col <= row).astype(BF16)
        carry = jnp.zeros((1, NF), F32)
        for c in range(0, S, ch):
            lf = _log_sigmoid(fl_ref[c:c + ch, :] + bf_ref[...])
            f_ref[c:c + ch, :] = _tri_dot(tri, lf) + carry
            carry = carry + jnp.sum(lf, axis=0, keepdims=True)

    return pl.pallas_call(
        body, name="forget_cumsum", grid=(T // S,),
        in_specs=[pl.BlockSpec((S, NF), lambda b: (b, 0)), _resident((1, NF))],
        out_specs=pl.BlockSpec((S, NF), lambda b: (b, 0)),
        out_shape=SDS((T, NF), F32),
        compiler_params=_cparams(1),
    )(fl, bf)


def _head_mask(e, hd):
    lanes = lax.broadcasted_iota(jnp.int32, (1, LANES), 1)
    return (lanes >= e * hd) & (lanes < (e + 1) * hd)


def _attn_specs(S, q_off, AW):
    ng = AW // LANES
    return [pl.BlockSpec((S, LANES), lambda b, g, o=q_off + w * ng: (b, o + g)) for w in range(3)]


def _attn_fwd(pm, fcum, frow, S, q_off, AW, hd, tq, gather):
    T = pm.shape[0]
    scale = float(hd) ** -0.5
    nq, hpg = S // tq, LANES // hd
    ng, nx = AW // LANES, len(gather)
    steps = (T // S) * ng

    def body(q_ref, k_ref, v_ref, fc_ref, fr_ref, *rest):
        x_refs, (o_ref, lse_ref), out_refs, sems = rest[:nx], rest[nx:nx + 2], rest[nx + 2:2 * nx + 2], rest[2 * nx + 2:]
        g = pl.program_id(1)
        step = pl.program_id(0) * ng + g
        start, forward, finish = _gather_phases(x_refs, out_refs, *sems)
        pl.when(step == 0)(start)

        @pl.when(g == 0)
        def _():
            lse_ref[...] = jnp.zeros_like(lse_ref)

        lanes = lax.broadcasted_iota(jnp.int32, (1, LANES), 1)
        for i in range(nq):
            rs, kend = slice(i * tq, (i + 1) * tq), (i + 1) * tq
            row = i * tq + lax.broadcasted_iota(jnp.int32, (tq, kend), 0)
            col = lax.broadcasted_iota(jnp.int32, (tq, kend), 1)
            o_tile = jnp.zeros((tq, LANES), F32)
            lse_tile = lse_ref[rs, :]
            for e in range(hpg):
                mask = _head_mask(e, hd)
                qs = jnp.where(mask, q_ref[rs, :], 0) * scale
                s = _dot_nt(qs, k_ref[0:kend, :]) + _lane_pick(fc_ref[rs, :], g * hpg + e) - fr_ref[e:e + 1, 0:kend]
                s = jnp.where(col <= row, s, -1e30)
                m = jnp.max(s, axis=1, keepdims=True)
                p = jnp.exp(s - m)
                l = jnp.sum(p, axis=1, keepdims=True)
                o_tile = jnp.where(mask, _dot(p.astype(BF16), v_ref[0:kend, :]) / l, o_tile)
                lse_tile = jnp.where(lanes == g * hpg + e, m + jnp.log(l), lse_tile)
            o_ref[rs, :] = o_tile.astype(BF16)
            lse_ref[rs, :] = lse_tile
        pl.when(step == (steps * 5) // 8)(forward)
        pl.when(step == steps - 1)(finish)

    full = pl.BlockSpec((S, LANES), lambda b, g: (b, 0))
    return pl.pallas_call(
        body, name="attn_fwd", grid=(T // S, ng),
        in_specs=_attn_specs(S, q_off, AW) + [full, pl.BlockSpec((None, None, 8, S), lambda b, g: (b, g, 0, 0))] + [ANY] * nx,
        out_specs=[pl.BlockSpec((S, LANES), lambda b, g: (b, g)), full] + [ANY] * nx,
        out_shape=[SDS((T, AW), BF16), SDS((T, LANES), F32)] + _gather_shapes(gather),
        scratch_shapes=_gather_sems(nx),
        compiler_params=_cparams(2),
    )(pm, pm, pm, fcum, frow, *gather)


def _merge_fwd(pm, o, x, wc, bg, woc_t, woa_t, wo, S, tm):
    T, D = x.shape
    CW, AW = woc_t.shape[1], woa_t.shape[1]
    tps = S // tm

    def body(cb_ref, cc_ref, cin_ref, gc_ref, ga_ref, o_ref, x_ref, wc_ref, bg_ref, woct_ref, woat_ref, wo_ref,
             x2_ref, u_ref, m_ref, ycin_ref, yc_ref, ya_ref, tail_ref):
        @pl.when(pl.program_id(0) % tps == 0)
        def _():
            tail_ref[...] = jnp.zeros_like(tail_ref)

        z = cc_ref[...].astype(F32) * cin_ref[...].astype(F32)
        z1, z2 = _taps_back(z, tail_ref[6:7, :], tail_ref[7:8, :])
        tail_ref[...] = z[tm - 8:tm, :]
        u = (z * wc_ref[2:3, :] + z2 * wc_ref[0:1, :]) + z1 * wc_ref[1:2, :]
        u_ref[...] = u.astype(BF16)
        ycin = (cb_ref[...].astype(F32) * u).astype(BF16)
        ycin_ref[...] = ycin
        yc = _dot_nt(ycin, woct_ref[...])
        ya = _dot_nt(o_ref[...], woat_ref[...])
        yc_ref[...] = yc.astype(BF16)
        ya_ref[...] = ya.astype(BF16)
        gc = _sigmoid(gc_ref[...].astype(F32) + bg_ref[:, 0:D])
        ga = _sigmoid(ga_ref[...].astype(F32) + bg_ref[:, D:2 * D])
        m = (gc * yc + ga * ya).astype(BF16)
        m_ref[...] = m
        x2_ref[...] = x_ref[...] + _dot(m, wo_ref[...])

    g_off = (3 * CW + 3 * AW) // D
    tok = lambda w, j=0: pl.BlockSpec((tm, w), lambda i: (i, j))
    return pl.pallas_call(
        body, name="merge_fwd", grid=(T // tm,),
        in_specs=[tok(CW, 0), tok(CW, 1), tok(CW, 2), tok(D, g_off), tok(D, g_off + 1), tok(AW), tok(D),
                  _resident((8, CW)), _resident((1, 2 * D)), _resident((D, CW)), _resident((D, AW)), _resident((D, D))],
        out_specs=[tok(D), tok(CW), tok(D), tok(CW), tok(D), tok(D)],
        out_shape=[SDS((T, D), F32), SDS((T, CW), BF16), SDS((T, D), BF16), SDS((T, CW), BF16),
                   SDS((T, D), BF16), SDS((T, D), BF16)],
        scratch_shapes=[pltpu.VMEM((8, CW), F32)],
        compiler_params=_cparams(1),
    )(pm, pm, pm, pm, pm, o, x, wc, bg, woc_t, woa_t, wo)


def _ffn_fwd(x2, g2, wup_t, wf, wdn, gf, tgt, S, tm):
    T, D = x2.shape
    FH = wdn.shape[0]
    ch = _tile(FH, FFN_CHUNK)
    tps = S // tm

    def body(x2_ref, g2_ref, wupt_ref, wf_ref, wdn_ref, gf_ref, tgt_ref,
             upp_ref, up_ref, act_ref, h2_ref, dx3_ref, loss_ref, dgf_ref, tail_ref):
        i = pl.program_id(0)

        @pl.when(i % tps == 0)
        def _():
            tail_ref[...] = jnp.zeros_like(tail_ref)

        @pl.when(i == 0)
        def _():
            loss_ref[...] = jnp.zeros_like(loss_ref)
            dgf_ref[...] = jnp.zeros_like(dgf_ref)

        x2v = x2_ref[...]
        h2 = (x2v * _rms(x2v) * g2_ref[...]).astype(BF16)
        h2_ref[...] = h2
        x3 = x2v
        for c in range(0, FH, ch):
            gated = []
            for cols in (slice(c, c + ch), slice(FH + c, FH + c + ch)):
                upp = _dot_nt(h2, wupt_ref[cols, :])
                upp_ref[:, cols] = upp.astype(BF16)
                p1, p2 = _taps_back(upp, tail_ref[6:7, cols], tail_ref[7:8, cols])
                tail_ref[:, cols] = upp[tm - 8:tm, :]
                up = (upp * wf_ref[2:3, cols] + p2 * wf_ref[0:1, cols]) + p1 * wf_ref[1:2, cols]
                up_ref[:, cols] = up.astype(BF16)
                gated.append(up)
            a, b = gated
            act = (a * _sigmoid(a) * b).astype(BF16)
            act_ref[:, c:c + ch] = act
            x3 = x3 + _dot(act, wdn_ref[c:c + ch, :])
        r3 = _rms(x3)
        xn3 = x3 * r3
        e = xn3 * gf_ref[...] - tgt_ref[...]
        loss_ref[...] += 0.5 * jnp.sum(jnp.mean(e * e, axis=-1, keepdims=True), axis=0, keepdims=True)
        dy = e / D
        dgf_ref[0:1, :] += jnp.sum(dy * xn3, axis=0, keepdims=True)
        dxn = dy * gf_ref[...]
        dx3_ref[...] = r3 * (dxn - xn3 * jnp.mean(dxn * xn3, axis=-1, keepdims=True))

    tok = lambda w: pl.BlockSpec((tm, w), lambda i: (i, 0))
    return pl.pallas_call(
        body, name="ffn_fwd", grid=(T // tm,),
        in_specs=[tok(D), _resident((1, D)), _resident((2 * FH, D)), _resident((8, 2 * FH)), _resident((FH, D)),
                  _resident((1, D)), tok(D)],
        out_specs=[tok(2 * FH), tok(2 * FH), tok(FH), tok(D), tok(D), pl.BlockSpec((8, LANES), lambda i: (0, 0)),
                   pl.BlockSpec((8, D), lambda i: (0, 0))],
        out_shape=[SDS((T, 2 * FH), BF16), SDS((T, 2 * FH), BF16), SDS((T, FH), BF16), SDS((T, D), BF16), SDS((T, D), F32),
                   SDS((8, LANES), F32), SDS((8, D), F32)],
        scratch_shapes=[pltpu.VMEM((8, 2 * FH), F32)],
        compiler_params=_cparams(1),
    )(x2, g2, wup_t, wf, wdn, gf, tgt)


def _ffn_bwd(dx3, x2, g2, upp, up, wf, wdn, wup_t, S, tm):
    T, D = x2.shape
    FH = wdn.shape[0]
    ch = _tile(FH, FFN_CHUNK)
    n, tps = T // tm, S // tm

    def body(dx3_ref, x2_ref, g2_ref, upp_ref, up_ref, wf_ref, wdn_ref, wupt_ref,
             dx2_ref, dupp_ref, dwf_ref, dg2_ref, head_ref):
        i = pl.program_id(0)
        t = n - 1 - i

        @pl.when(i == 0)
        def _():
            dwf_ref[...] = jnp.zeros_like(dwf_ref)
            dg2_ref[...] = jnp.zeros_like(dg2_ref)

        @pl.when(t % tps == tps - 1)
        def _():
            head_ref[...] = jnp.zeros_like(head_ref)

        dx3v = dx3_ref[...]
        dx3b = dx3v.astype(BF16)
        x2v = x2_ref[...]
        r2 = _rms(x2v)
        xn2 = x2v * r2
        dh2 = jnp.zeros((tm, D), F32)
        for c in range(0, FH, ch):
            ca, cb = slice(c, c + ch), slice(FH + c, FH + c + ch)
            a, b = up_ref[:, ca].astype(F32), up_ref[:, cb].astype(F32)
            sig = _sigmoid(a)
            sl = a * sig
            dact = _dot_nt(dx3b, wdn_ref[ca, :])
            grads = (dact * b * (sig * (1.0 + a * (1.0 - sig))), dact * sl)
            for cols, d in zip((ca, cb), grads):
                u0 = upp_ref[:, cols].astype(F32)
                d1, d2 = _taps_ahead(d, head_ref[0:1, cols], head_ref[1:2, cols])
                head_ref[:, cols] = d[0:8, :]
                dwf_ref[2:3, cols] += jnp.sum(u0 * d, axis=0, keepdims=True)
                dwf_ref[1:2, cols] += jnp.sum(u0 * d1, axis=0, keepdims=True)
                dwf_ref[0:1, cols] += jnp.sum(u0 * d2, axis=0, keepdims=True)
                dpre = ((d * wf_ref[2:3, cols] + d1 * wf_ref[1:2, cols]) + d2 * wf_ref[0:1, cols]).astype(BF16)
                dupp_ref[:, cols] = dpre
                dh2 = dh2 + _dot(dpre, wupt_ref[cols, :])
        dg2_ref[0:1, :] += jnp.sum(dh2 * xn2, axis=0, keepdims=True)
        dxn = dh2 * g2_ref[...]
        dx2_ref[...] = dx3v + r2 * (dxn - xn2 * jnp.mean(dxn * xn2, axis=-1, keepdims=True))

    tok = lambda w: pl.BlockSpec((tm, w), lambda i: (n - 1 - i, 0))
    acc = lambda w: pl.BlockSpec((8, w), lambda i: (0, 0))
    return pl.pallas_call(
        body, name="ffn_bwd", grid=(n,),
        in_specs=[tok(D), tok(D), _resident((1, D)), tok(2 * FH), tok(2 * FH), _resident((8, 2 * FH)),
                  _resident((FH, D)), _resident((2 * FH, D))],
        out_specs=[tok(D), tok(2 * FH), acc(2 * FH), acc(D)],
        out_shape=[SDS((T, D), F32), SDS((T, 2 * FH), BF16), SDS((8, 2 * FH), F32), SDS((8, D), F32)],
        scratch_shapes=[pltpu.VMEM((8, 2 * FH), F32)],
        compiler_params=_cparams(1),
    )(dx3, x2, g2, upp, up, wf, wdn, wup_t)


def _merge_bwd(dx2, pm, u, yc, ya, m, ycin, o, wc, bg, wo, woc_t, woa_t, S, tm, exchange):
    T, D = dx2.shape
    CW, AW = woc_t.shape[1], woa_t.shape[1]
    n, tps = T // tm, S // tm
    nx = len(exchange)
    rb = D // N_DEV
    assert CW + AW == D

    def body(dx2_ref, cb_ref, cc_ref, cin_ref, gc_ref, ga_ref, u_ref, yc_ref, ya_ref, m_ref, ycin_ref, o_ref,
             wc_ref, bg_ref, wo_ref, woct_ref, woat_ref, *rest):
        x_refs = rest[:nx]
        dconv_ref, dgl_ref, do_ref, dws_ref, dwc_ref, dbg_ref = rest[nx:nx + 6]
        got_refs, head_ref, sems = rest[nx + 6:2 * nx + 6], rest[2 * nx + 6], rest[2 * nx + 7:]

        def add_blocks(rows, cols, grad):
            for d in range(N_DEV):
                dws_ref[d, rows, cols] += grad[d * rb:(d + 1) * rb, :]

        i = pl.program_id(0)
        t = n - 1 - i

        @pl.when(i == 0)
        def _():
            for cp in _sibling_copies(x_refs, got_refs, *sems):
                cp.start()

        @pl.when(i == 0)
        def _():
            for ref in (dwc_ref, dbg_ref, dws_ref):
                ref[...] = jnp.zeros_like(ref)

        @pl.when(t % tps == tps - 1)
        def _():
            head_ref[...] = jnp.zeros_like(head_ref)

        dx2b = dx2_ref[...].astype(BF16)
        add_blocks(slice(0, rb), slice(0, D), _dot_tn(m_ref[...], dx2b))
        dm = _dot_nt(dx2b, wo_ref[...])
        outs = []
        for g_ref, y_ref, cols in ((gc_ref, yc_ref, slice(0, D)), (ga_ref, ya_ref, slice(D, 2 * D))):
            g = _sigmoid(g_ref[...].astype(F32) + bg_ref[:, cols])
            dgl = dm * y_ref[...].astype(F32) * g * (1.0 - g)
            dgl_ref[:, cols] = dgl.astype(BF16)
            dbg_ref[0:1, cols] += jnp.sum(dgl, axis=0, keepdims=True)
            outs.append((dm * g).astype(BF16))
        dyc, dya = outs
        add_blocks(slice(rb, 2 * rb), slice(0, CW), _dot_tn(dyc, ycin_ref[...]))
        add_blocks(slice(rb, 2 * rb), slice(CW, CW + AW), _dot_tn(dya, o_ref[...]))
        do_ref[...] = _dot(dya, woat_ref[...]).astype(BF16)
        dycin = _dot(dyc, woct_ref[...])
        cc, cin = cc_ref[...].astype(F32), cin_ref[...].astype(F32)
        z = cc * cin
        du = dycin * cb_ref[...].astype(F32)
        du1, du2 = _taps_ahead(du, head_ref[0:1, :], head_ref[1:2, :])
        head_ref[...] = du[0:8, :]
        dwc_ref[2:3, :] += jnp.sum(z * du, axis=0, keepdims=True)
        dwc_ref[1:2, :] += jnp.sum(z * du1, axis=0, keepdims=True)
        dwc_ref[0:1, :] += jnp.sum(z * du2, axis=0, keepdims=True)
        dz = (du * wc_ref[2:3, :] + du1 * wc_ref[1:2, :]) + du2 * wc_ref[0:1, :]
        dconv_ref[:, 0:CW] = (dycin * u_ref[...].astype(F32)).astype(BF16)
        dconv_ref[:, CW:2 * CW] = (dz * cin).astype(BF16)
        dconv_ref[:, 2 * CW:3 * CW] = (dz * cc).astype(BF16)

        @pl.when(i == n - 1)
        def _():
            for cp in _sibling_copies(x_refs, got_refs, *sems):
                cp.wait()

    g_off = (3 * CW + 3 * AW) // D
    tok = lambda w, j=0: pl.BlockSpec((tm, w), lambda i: (n - 1 - i, j))
    acc = lambda w: pl.BlockSpec((8, w), lambda i: (0, 0))
    return pl.pallas_call(
        body, name="merge_bwd", grid=(n,),
        in_specs=[tok(D), tok(CW, 0), tok(CW, 1), tok(CW, 2), tok(D, g_off), tok(D, g_off + 1), tok(CW), tok(D), tok(D),
                  tok(D), tok(CW), tok(AW),
                  _resident((8, CW)), _resident((1, 2 * D)), _resident((D, D)), _resident((D, CW)), _resident((D, AW))]
                 + [ANY] * nx,
        out_specs=[tok(3 * CW), tok(2 * D), tok(AW), pl.BlockSpec((N_DEV, 2 * rb, D), lambda i: (0, 0, 0)), acc(CW),
                   acc(2 * D)] + [ANY] * nx,
        out_shape=[SDS((T, 3 * CW), BF16), SDS((T, 2 * D), BF16), SDS((T, AW), BF16), SDS((N_DEV, 2 * rb, D), F32),
                   SDS((8, CW), F32), SDS((8, 2 * D), F32)] + _sibling_shapes(exchange),
        scratch_shapes=[pltpu.VMEM((8, CW), F32)] + _exchange_sems(nx),
        compiler_params=_cparams(1),
    )(dx2, pm, pm, pm, pm, pm, u, yc, ya, m, ycin, o, wc, bg, wo, woc_t, woa_t, *exchange)


def _attn_bwd(pm, do, fcum, frow, lse, S, q_off, AW, hd, tq, exchange):
    T = pm.shape[0]
    scale = float(hd) ** -0.5
    nq, hpg, ng = S // tq, LANES // hd, AW // LANES
    nx = len(exchange)
    steps = (T // S) * ng

    def body(q_ref, k_ref, v_ref, do_ref, fc_ref, fr_ref, lse_ref, *rest):
        x_refs, (dq_ref, dk_ref, dv_ref, dfr_ref) = rest[:nx], rest[nx:nx + 4]
        got_refs, (dk_acc, dv_acc), sems = rest[nx + 4:2 * nx + 4], rest[2 * nx + 4:2 * nx + 6], rest[2 * nx + 6:]
        g = pl.program_id(1)
        step = pl.program_id(0) * ng + g

        @pl.when(step == 0)
        def _():
            for cp in _chip_copies(x_refs, got_refs, *sems):
                cp.start()

        dk_acc[...] = jnp.zeros_like(dk_acc)
        dv_acc[...] = jnp.zeros_like(dv_acc)
        dfr_ref[...] = jnp.zeros_like(dfr_ref)
        for i in range(nq):
            rs, kend = slice(i * tq, (i + 1) * tq), (i + 1) * tq
            row = i * tq + lax.broadcasted_iota(jnp.int32, (tq, kend), 0)
            col = lax.broadcasted_iota(jnp.int32, (tq, kend), 1)
            kk, vv = k_ref[0:kend, :], v_ref[0:kend, :]
            dq_tile = jnp.zeros((tq, LANES), F32)
            for e in range(hpg):
                mask = _head_mask(e, hd)
                qs = jnp.where(mask, q_ref[rs, :], 0) * scale
                doi = jnp.where(mask, do_ref[rs, :], 0)
                s = _dot_nt(qs, kk) + _lane_pick(fc_ref[rs, :], g * hpg + e) - fr_ref[e:e + 1, 0:kend]
                p = jnp.where(col <= row, jnp.exp(s - _lane_pick(lse_ref[rs, :], g * hpg + e)), 0.0)
                dp = _dot_nt(doi, vv)
                ds = p * (dp - jnp.sum(p * dp, axis=1, keepdims=True))
                pb, dsb = p.astype(BF16), ds.astype(BF16)
                dq_tile = jnp.where(mask, _dot(dsb, kk) * scale, dq_tile)
                dv_acc[0:kend, :] += _dot_tn(pb, doi)
                dk_acc[0:kend, :] += _dot_tn(dsb, qs)
                dfr_ref[e:e + 1, 0:kend] -= jnp.sum(ds, axis=0, keepdims=True)
            dq_ref[rs, :] = dq_tile.astype(BF16)
        dk_ref[...] = dk_acc[...].astype(BF16)
        dv_ref[...] = dv_acc[...].astype(BF16)

        @pl.when(step == steps - 1)
        def _():
            for cp in _chip_copies(x_refs, got_refs, *sems):
                cp.wait()

    full = pl.BlockSpec((S, LANES), lambda b, g: (b, 0))
    grp = pl.BlockSpec((S, LANES), lambda b, g: (b, g))
    rows = pl.BlockSpec((None, None, 8, S), lambda b, g: (b, g, 0, 0))
    return pl.pallas_call(
        body, name="attn_bwd", grid=(T // S, ng),
        in_specs=_attn_specs(S, q_off, AW) + [grp, full, rows, full] + [ANY] * nx,
        out_specs=[grp, grp, grp, rows] + [ANY] * nx,
        out_shape=[SDS((T, AW), BF16), SDS((T, AW), BF16), SDS((T, AW), BF16), SDS((T // S, ng, 8, S), F32)]
                  + _chip_shapes(exchange),
        scratch_shapes=[pltpu.VMEM((S, LANES), F32), pltpu.VMEM((S, LANES), F32)] + _exchange_sems(3 * nx),
        compiler_params=_cparams(2),
    )(pm, pm, pm, do, fcum, frow, lse, *exchange)


def _forget_bwd(dfc, fl, bf, S):
    T, NF = fl.shape
    ch = _tile(S, 256, 8)

    def body(df_ref, fl_ref, bf_ref, dfl_ref, dbf_ref):
        @pl.when(pl.program_id(0) == 0)
        def _():
            dbf_ref[...] = jnp.zeros_like(dbf_ref)

        row = lax.broadcasted_iota(jnp.int32, (ch, ch), 0)
        col = lax.broadcasted_iota(jnp.int32, (ch, ch), 1)
        tri = (col >= row).astype(BF16)
        carry = jnp.zeros((1, NF), F32)
        for c in range(S - ch, -1, -ch):
            d = df_ref[c:c + ch, :]
            dlf = _tri_dot(tri, d) + carry
            carry = carry + jnp.sum(d, axis=0, keepdims=True)
            dfl = dlf * _sigmoid(-(fl_ref[c:c + ch, :] + bf_ref[...]))
            dfl_ref[c:c + ch, :] = dfl.astype(BF16)
            dbf_ref[0:1, :] += jnp.sum(dfl, axis=0, keepdims=True)

    return pl.pallas_call(
        body, name="forget_bwd", grid=(T // S,),
        in_specs=[pl.BlockSpec((S, NF), lambda b: (b, 0)), pl.BlockSpec((S, NF), lambda b: (b, 0)), _resident((1, NF))],
        out_specs=[pl.BlockSpec((S, NF), lambda b: (b, 0)), pl.BlockSpec((8, NF), lambda b: (0, 0))],
        out_shape=[SDS((T, NF), BF16), SDS((8, NF), F32)],
        compiler_params=_cparams(1),
    )(dfc, fl, bf)


def _inproj_bwd(dparts, offs, w_t, x, g1, dx2, tm, exchange):
    T, D = x.shape
    npart, nx = len(dparts), len(exchange)
    n = T // tm

    def body(*refs):
        d_refs = refs[:npart]
        w_ref, x_ref, g_ref, dx2_ref = refs[npart:npart + 4]
        x_refs = refs[npart + 4:npart + 4 + nx]
        dx_ref, dg_ref = refs[npart + 4 + nx:npart + 6 + nx]
        got_refs, sems = refs[npart + 6 + nx:npart + 6 + 2 * nx], refs[npart + 6 + 2 * nx:]

        @pl.when(pl.program_id(0) == 0)
        def _():
            for cp in _chip_copies(x_refs, got_refs, *sems):
                cp.start()

        @pl.when(pl.program_id(0) == 0)
        def _():
            dg_ref[...] = jnp.zeros_like(dg_ref)

        dh = None
        for d_ref, off in zip(d_refs, offs):
            term = _dot(d_ref[...], w_ref[off:off + d_ref.shape[1], :])
            dh = term if dh is None else dh + term
        xv = x_ref[...]
        r = _rms(xv)
        xn = xv * r
        dg_ref[0:1, :] += jnp.sum(dh * xn, axis=0, keepdims=True)
        dxn = dh * g_ref[...]
        dx_ref[...] = dx2_ref[...] + r * (dxn - xn * jnp.mean(dxn * xn, axis=-1, keepdims=True))

        @pl.when(pl.program_id(0) == n - 1)
        def _():
            for cp in _chip_copies(x_refs, got_refs, *sems):
                cp.wait()

    tok = lambda w: pl.BlockSpec((tm, w), lambda i: (i, 0))
    return pl.pallas_call(
        body, name="inproj_bwd", grid=(n,),
        in_specs=[tok(d.shape[1]) for d in dparts] + [_resident(w_t.shape), tok(D), _resident((1, D)), tok(D)] + [ANY] * nx,
        out_specs=[tok(D), pl.BlockSpec((8, D), lambda i: (0, 0))] + [ANY] * nx,
        out_shape=[SDS((T, D), F32), SDS((8, D), F32)] + _chip_shapes(exchange),
        scratch_shapes=_exchange_sems(3 * nx),
        compiler_params=_cparams(1),
    )(*dparts, w_t, x, g1, dx2, *exchange)


def _wgrad(bs, a, name, into=None, row_off=0, total_rows=None):
    bs = list(bs) if isinstance(bs, (list, tuple)) else [bs]
    P = len(bs)
    T, N = bs[0].shape
    M = a.shape[1]
    tn = _tile(N, 1408 if M <= 1024 else 512)
    while row_off % tn:
        tn = _tile(N, tn - LANES)
    tk = _tile(T, 512, 16)
    per_row = 2 * (P * tn * bs[0].dtype.itemsize + M * a.dtype.itemsize)
    while T % (2 * tk) == 0 and 2 * tk * per_row + 2 * tn * M * 4 <= WGRAD_VMEM_BUDGET:
        tk *= 2
    blk0, nj, nk = row_off // tn, N // tn, T // tk

    def body(*refs):
        b_refs, a_ref, o_ref = refs[:P], refs[P], refs[-1]
        j = pl.program_id(0)

        @pl.when(pl.program_id(1) == 0)
        def _():
            o_ref[...] = jnp.zeros_like(o_ref)

        for p, b_ref in enumerate(b_refs):
            @pl.when((j >= p * nj) & (j < (p + 1) * nj))
            def _(b_ref=b_ref):
                o_ref[...] += _dot_tn(b_ref[...].astype(BF16), a_ref[...].astype(BF16))

    def b_spec(p):
        def index(j, k):
            before, mine = j < p * nj, (j >= p * nj) & (j < (p + 1) * nj)
            return (jnp.where(mine, k, jnp.where(before, 0, nk - 1)),
                    jnp.where(mine, j - p * nj, jnp.where(before, 0, nj - 1)))
        return pl.BlockSpec((tk, tn), index)

    in_specs = [b_spec(p) for p in range(P)] + [pl.BlockSpec((tk, M), lambda j, k: (k, 0))]
    args = (*bs, a)
    kwargs = {}
    if into is not None:
        in_specs.append(ANY)
        args += (into,)
        kwargs["input_output_aliases"] = {P + 1: 0}
        total_rows = into.shape[0]
    return pl.pallas_call(
        body, name=name, grid=(P * nj, nk),
        in_specs=in_specs,
        out_specs=pl.BlockSpec((tn, M), lambda j, k: (blk0 + j, 0)),
        out_shape=SDS((P * N if total_rows is None else total_rows, M), F32),
        compiler_params=_cparams(2), **kwargs,
    )(*args)


def _adamw(w, g, m, v, name):
    shape = w.shape
    C = shape[-1]
    w2, g2, m2, v2 = (a.reshape(-1, C) for a in (w, g, m, v))
    R = w2.shape[0]
    tr = R if R <= 512 else _tile(R, 256, 8)
    if tr < 64:
        tr = R

    def body(w_ref, g_ref, m_ref, v_ref, d_ref, nm_ref, nv_ref):
        gv = g_ref[...]
        mv = ADAM_B1 * m_ref[...] + (1.0 - ADAM_B1) * gv
        vv = ADAM_B2 * v_ref[...] + (1.0 - ADAM_B2) * (gv * gv)
        m_hat = mv / (1.0 - ADAM_B1 ** ADAM_STEP)
        v_hat = vv / (1.0 - ADAM_B2 ** ADAM_STEP)
        d_ref[...] = -ADAM_LR * (m_hat / (jnp.sqrt(v_hat) + ADAM_EPS) + ADAM_WD * w_ref[...])
        nm_ref[...] = mv
        nv_ref[...] = vv

    spec = pl.BlockSpec((tr, C), lambda r: (r, 0))
    outs = pl.pallas_call(
        body, name=name, grid=(R // tr,),
        in_specs=[spec] * 4, out_specs=[spec] * 3, out_shape=[SDS((R, C), F32)] * 3,
        compiler_params=_cparams(1),
    )(w2, g2, m2, v2)
    return tuple(o.reshape(shape) for o in outs)


def _rows(a, L):
    lead = a.shape[0]
    flat = a.reshape(lead, -1)
    n = flat.shape[1]
    r = -(-n // L)
    return jnp.pad(flat, ((0, 0), (0, r * L - n))).reshape(lead, r, L)


def _unrows(p, shape):
    return p.reshape(-1)[:int(np.prod(shape))].reshape(shape)


def _from_col_blocks(a):
    n, R, c = a.shape
    return a.transpose(1, 0, 2).reshape(R, n * c)


def _in_windows(n_loc, nqkv, H, NM):
    win = (n_loc + 7 + 7) // 8 * 8
    starts, index = [], np.zeros((N_DEV, n_loc), np.int32)
    for d in range(N_DEV):
        rows = np.arange(n_loc * d, n_loc * (d + 1))
        is_f = (rows >= nqkv) & (rows < nqkv + H)
        kept = np.where(rows < nqkv, rows, rows - H)
        lo = int(kept[~is_f].min())
        start = lo // 8 * 8
        assert int(kept[~is_f].max()) - start < win and start + win <= NM + LANES
        starts.append(start)
        index[d] = np.where(is_f, win + rows - nqkv, kept - start)
    return starts, win, index


def kernel(x, norm_mix_g, w_in, b_f, b_gate, conv_mix_w, w_out_conv, w_out_attn, w_o, norm_ffn_g, w_up, conv_ffn_w, w_down, norm_f_g, loss_target, m_norm_mix_g, m_w_in, m_b_f, m_b_gate, m_conv_mix_w, m_w_out_conv, m_w_out_attn, m_w_o, m_norm_ffn_g, m_w_up, m_conv_ffn_w, m_w_down, m_norm_f_g, v_norm_mix_g, v_w_in, v_b_f, v_b_gate, v_conv_mix_w, v_w_out_conv, v_w_out_attn, v_w_o, v_norm_ffn_g, v_w_up, v_conv_ffn_w, v_w_down, v_norm_f_g):
    Bl, S, D = x.shape
    T = Bl * S
    H = b_f.shape[-1]
    CW = N_DEV * conv_mix_w.shape[-1]
    AW = w_out_attn.shape[1]
    hd = AW // H
    FH = N_DEV * w_down.shape[1]
    n_loc = w_in.shape[-1]
    NIN = N_DEV * n_loc
    NM = 3 * CW + 3 * AW + 2 * D
    nqkv = 3 * CW + 3 * AW
    assert NIN == NM + H and w_out_conv.shape[1] == CW and nqkv % D == 0 and AW % LANES == 0 and LANES % hd == 0
    hpg, ng = LANES // hd, AW // LANES
    assert hpg <= 8
    tm_big = min(512, S // 2)
    tm_ffn = min(256, S // 2)
    tq_fwd = min(512, S // 2)
    tq_bwd = min(256, S // 2)
    px, py, pc = _place()
    me = 4 * px + 2 * py + pc

    bits = lambda a: lax.bitcast_convert_type(a, BF16)
    taps = jnp.concatenate([_rows(bits(conv_ffn_w[0])[None], D)[0], _rows(bits(conv_mix_w[0])[None], D)[0]], axis=0)
    n_ffn_rows = -(-conv_ffn_w[0].size * 2 // D)
    late = [w_up[0].T.astype(BF16), w_down[0].astype(BF16), w_o[0].astype(BF16), w_out_conv[0].T.astype(BF16),
            w_out_attn[0].T.astype(BF16), taps]
    g_in, = _all_gather([w_in[0].T.astype(BF16)], "weights_all_gather")
    W_in_rows = g_in.reshape(NIN, D)
    W_in_t = jnp.concatenate([W_in_rows[:nqkv], W_in_rows[nqkv + H:], W_in_rows[nqkv:nqkv + H],
                              jnp.zeros((LANES - H, D), BF16)], axis=0)
    bf128 = jnp.pad(b_f, ((0, 0), (0, LANES - H)))

    x2d = x.reshape(T, D)
    tgt = loss_target.reshape(T, D)
    pm, fl, h1 = _inproj_fwd(x2d, norm_mix_g, W_in_t, NM, tm_big)
    fcum = _forget_cumsum(fl, bf128, S)
    frow = jnp.pad(fcum[:, :H].reshape(Bl, S, ng, hpg).transpose(0, 2, 3, 1), ((0, 0), (0, 0), (0, 8 - hpg), (0, 0)))
    q_off = 3 * CW // LANES
    o, lse, g_up, g_dn, g_o, g_oc, g_oa, g_taps = _attn_fwd(pm, fcum, frow, S, q_off, AW, hd, tq_fwd, late)
    W_up_t = g_up.reshape(2 * FH, D)
    W_dn = g_dn.reshape(FH, D)
    W_o = g_o.reshape(D, D)
    W_oc_t = g_oc.reshape(D, CW)
    W_oa_t = g_oa.reshape(D, AW)
    tap_bits = g_taps.reshape(N_DEV, -1)
    n_ffn, n_mix = conv_ffn_w[0].size * 2, conv_mix_w[0].size * 2
    wf_full = _from_col_blocks(lax.bitcast_convert_type(
        tap_bits[:, :n_ffn].reshape((N_DEV,) + conv_ffn_w.shape[1:] + (2,)), F32))
    wc_full = _from_col_blocks(lax.bitcast_convert_type(
        tap_bits[:, n_ffn_rows * D:n_ffn_rows * D + n_mix].reshape((N_DEV,) + conv_mix_w.shape[1:] + (2,)), F32))
    wf8 = jnp.pad(wf_full, ((0, 5), (0, 0)))
    wc8 = jnp.pad(wc_full, ((0, 5), (0, 0)))
    x2, u, m, ycin, yc, ya = _merge_fwd(pm, o, x2d, wc8, b_gate, W_oc_t, W_oa_t, W_o, S, tm_big)
    upp, up, act, h2, dx3, loss8, dgf8 = _ffn_fwd(x2, norm_ffn_g, W_up_t, wf8, W_dn, norm_f_g.reshape(1, D), tgt, S, tm_ffn)

    dx2, dupp, dwf8, dg2_8 = _ffn_bwd(dx3, x2, norm_ffn_g, upp, up, wf8, W_dn, W_up_t, S, tm_ffn)
    dW_dn = _wgrad(act, dx3, "wgrad_down")
    dW_up_t = _wgrad(dupp, h2, "wgrad_up")
    ids = jnp.stack([pc, 2 * px + py]).astype(jnp.int32)
    big = [dW_up_t.reshape(4, 2, -1, D), dW_dn.reshape(4, 2, -1, D)]
    dconv, dgl, do, dW_small, dwc8, dbg8, *sib_big = _merge_bwd(
        dx2, pm, u, yc, ya, m, ycin, o, wc8, b_gate, W_o, W_oc_t, W_oa_t, S, tm_big, big)
    big_sums = [_pair_sum(b, r, ids, "grads_pair_sum_%d" % a) for a, (b, r) in enumerate(zip(big, sib_big))]
    dq, dk, dv, dfr, *chips_big = _attn_bwd(pm, do, fcum, frow, lse, S, q_off, AW, hd, tq_bwd, [s[1] for s in big_sums])
    dfc = jnp.pad(dfr[:, :, :hpg, :].transpose(0, 3, 1, 2).reshape(T, H), ((0, 0), (0, LANES - H)))
    dfl, dbf8 = _forget_bwd(dfc, fl, bf128, S)
    dparts = [dconv, dq, dk, dv, dgl, dfl]
    offs = [0, 3 * CW, 3 * CW + AW, 3 * CW + 2 * AW, nqkv, NM]
    dW_in_t = _wgrad(dconv, h1, "wgrad_in_conv", total_rows=NM + LANES)
    dW_in_t = _wgrad([dq, dk, dv], h1, "wgrad_in_qkv", into=dW_in_t, row_off=3 * CW)
    dW_in_t = _wgrad(dgl, h1, "wgrad_in_gates", into=dW_in_t, row_off=nqkv)
    dW_in_t = _wgrad(dfl, h1, "wgrad_in_forget", into=dW_in_t, row_off=NM)

    starts, win, index = _in_windows(n_loc, nqkv, H, NM)
    small = dW_small.reshape(4, 2, -1, D)
    my_starts = [jnp.where(pc == 0, starts[2 * j], starts[2 * j + 1]) for j in range(4)]
    win_ids = jnp.stack([pc, 2 * px + py] + my_starts).astype(jnp.int32)
    sib_win, sib_small = _sibling_exchange(dW_in_t, starts, win, [small], "grads_sibling_exchange")
    sums = [_pair_sum(dW_in_t, sib_win, win_ids, "grads_pair_sum_in", win_rows=win),
            _pair_sum(small, sib_small, ids, "grads_pair_sum_small")]
    grad_x, dg1_8, *from_chips = _inproj_bwd(dparts, offs, W_in_t, x2d, norm_mix_g, dx2, tm_big, [s[1] for s in sums])
    red_win, red_small, red_up, red_dn = [
        _final_sum(s[0], r, "grads_final_sum_%d" % a)
        for a, (s, r) in enumerate(zip(sums + big_sums, list(from_chips) + list(chips_big)))]

    f_rows = dW_in_t[NM:NM + 8]
    wf_rows = _rows(dwf8[:3].reshape(1, -1), D)[0]
    wc_rows = _rows(dwc8[:3].reshape(1, -1), D)[0]
    smalls = [dg1_8[0:1], dg2_8[0:1], dgf8[0:1], dbg8[0:1, :D], dbg8[0:1, D:], jnp.pad(dbf8[0:1], ((0, 0), (0, D - LANES))),
              jnp.pad(loss8[0:1], ((0, 0), (0, D - LANES))), jnp.zeros((1, D), F32), f_rows, wf_rows, wc_rows]
    spack = jnp.concatenate(smalls, axis=0)
    spack = jnp.pad(spack, ((0, -spack.shape[0] % 8), (0, 0)))
    ssum = _sum8(_all_gather([spack], "small_all_gather")[0], "small_sum")
    g_g1, g_g2, g_gf = ssum[0:1], ssum[1:2], ssum[2]
    g_bg = jnp.concatenate([ssum[3:4], ssum[4:5]], axis=1)
    g_bf = ssum[5:6, :H]
    r0 = 16
    r1 = r0 + wf_rows.shape[0]
    wf_sum = _unrows(ssum[r0:r1], (3, 2 * FH))
    wc_sum = _unrows(ssum[r1:r1 + wc_rows.shape[0]], (3, CW))
    g_wf = lax.dynamic_slice_in_dim(wf_sum, me * conv_ffn_w.shape[-1], conv_ffn_w.shape[-1], axis=1)
    g_wc = lax.dynamic_slice_in_dim(wc_sum, me * conv_mix_w.shape[-1], conv_mix_w.shape[-1], axis=1)

    ext = jnp.concatenate([red_win, ssum[8:8 + H]], axis=0)
    my_index = lax.dynamic_index_in_dim(jnp.asarray(index), me, axis=0, keepdims=False)
    g_w_in_t = jnp.take(ext, my_index, axis=0)
    rb = D // N_DEV
    g_w_o = red_small[:rb]
    g_w_oc = red_small[rb:, :CW].T
    g_w_oa = red_small[rb:, CW:].T

    loss = ssum[6, 0]

    names = ["norm_mix_g", "w_in", "b_f", "b_gate", "conv_mix_w", "w_out_conv", "w_out_attn", "w_o", "norm_ffn_g", "w_up",
             "conv_ffn_w", "w_down", "norm_f_g"]
    weights = [norm_mix_g, w_in, b_f, b_gate, conv_mix_w, w_out_conv, w_out_attn, w_o, norm_ffn_g, w_up, conv_ffn_w, w_down, norm_f_g]
    grads = [g_g1, g_w_in_t, g_bf, g_bg, g_wc, g_w_oc, g_w_oa, g_w_o, g_g2, red_up, g_wf, red_dn, g_gf]
    ms = [m_norm_mix_g, m_w_in, m_b_f, m_b_gate, m_conv_mix_w, m_w_out_conv, m_w_out_attn, m_w_o, m_norm_ffn_g, m_w_up,
          m_conv_ffn_w, m_w_down, m_norm_f_g]
    vs = [v_norm_mix_g, v_w_in, v_b_f, v_b_gate, v_conv_mix_w, v_w_out_conv, v_w_out_attn, v_w_o, v_norm_ffn_g, v_w_up,
          v_conv_ffn_w, v_w_down, v_norm_f_g]
    to_view = {"w_in": lambda a: a[0].T.reshape(-1, LANES), "w_up": lambda a: a[0].T}
    from_view = {"w_in": lambda a: a.reshape(n_loc, D).T[None], "w_up": lambda a: a.T[None]}
    out_grads, steps = [], []
    for nm, w, g, mm, vv in zip(names, weights, grads, ms, vs):
        if nm in to_view:
            wv, mv, vw = (to_view[nm](a) for a in (w, mm, vv))
            gv = g.reshape(wv.shape)
            steps.append(tuple(from_view[nm](o) for o in _adamw(wv, gv, mv, vw, "adamw_" + nm)))
            out_grads.append(from_view[nm](gv))
        else:
            gv = g.reshape(w.shape)
            steps.append(_adamw(w, gv, mm, vv, "adamw_" + nm))
            out_grads.append(gv)
    deltas, new_ms, new_vs = zip(*steps)
    return (loss, grad_x.reshape(Bl, S, D), *out_grads, *deltas, *new_ms, *new_vs)
```

```python
import numpy as np

import jax
import jax.numpy as jnp
from jax import lax
from jax.experimental import pallas as pl
from jax.experimental.pallas import tpu as pltpu

F32, BF16 = jnp.float32, jnp.bfloat16
EPS = 1e-6
ADAM_LR, ADAM_B1, ADAM_B2, ADAM_EPS, ADAM_WD, ADAM_STEP = 0.001, 0.9, 0.999, 1e-08, 0.01, 10
N_DEV = 8
LANES = 128
V7X_VMEM_LIMIT = 56 * 1024 * 1024
FFN_CHUNK = 2816
WGRAD_VMEM_BUDGET = 40 * 1024 * 1024
MESH = pl.DeviceIdType.MESH
SDS = jax.ShapeDtypeStruct
ANY = pl.BlockSpec(memory_space=pl.ANY)


def _tile(n, target, mult=LANES):
    best = None
    for t in range(mult, min(n, target) + 1, mult):
        if n % t == 0:
            best = t
    return best if best is not None else n


def _resident(shape):
    return pl.BlockSpec(shape, lambda *_: (0,) * len(shape), pipeline_mode=pl.Buffered(1))


def _cparams(n_axes=1):
    return pltpu.CompilerParams(dimension_semantics=("arbitrary",) * n_axes, vmem_limit_bytes=V7X_VMEM_LIMIT)


def _dot(a, b):
    return jnp.dot(a, b, preferred_element_type=F32)


def _dot_tn(a, b):
    return lax.dot_general(a, b, (((0,), (0,)), ((), ())), preferred_element_type=F32)


def _dot_nt(a, b):
    return lax.dot_general(a, b, (((1,), (1,)), ((), ())), preferred_element_type=F32)


def _sigmoid(x):
    return 0.5 * jnp.tanh(0.5 * x) + 0.5


def _rms(x):
    return lax.rsqrt(jnp.mean(x * x, axis=-1, keepdims=True) + EPS)


def _taps_back(z, r6, r7):
    row = lax.broadcasted_iota(jnp.int32, (8, 1), 0)
    z1, z2 = pltpu.roll(z, 1, 0), pltpu.roll(z, 2, 0)
    z1 = jnp.concatenate([jnp.where(row == 0, r7, z1[0:8]), z1[8:]], axis=0)
    z2 = jnp.concatenate([jnp.where(row == 0, r6, jnp.where(row == 1, r7, z2[0:8])), z2[8:]], axis=0)
    return z1, z2


def _taps_ahead(d, h0, h1):
    tm = d.shape[0]
    row = lax.broadcasted_iota(jnp.int32, (8, 1), 0)
    d1, d2 = pltpu.roll(d, tm - 1, 0), pltpu.roll(d, tm - 2, 0)
    d1 = jnp.concatenate([d1[:tm - 8], jnp.where(row == 7, h0, d1[tm - 8:])], axis=0)
    d2 = jnp.concatenate([d2[:tm - 8], jnp.where(row == 6, h0, jnp.where(row == 7, h1, d2[tm - 8:]))], axis=0)
    return d1, d2


def _tri_dot(tri, x):
    hi = x.astype(BF16)
    r = x - hi.astype(F32)
    mid = r.astype(BF16)
    lo = (r - mid.astype(F32)).astype(BF16)
    return (_dot(tri, lo) + _dot(tri, mid)) + _dot(tri, hi)


def _lane_pick(block, lane):
    lanes = lax.broadcasted_iota(jnp.int32, (1, block.shape[1]), 1)
    return jnp.sum(jnp.where(lanes == lane, block, 0.0), axis=1, keepdims=True)


def _place():
    return lax.axis_index("x"), lax.axis_index("y"), lax.axis_index("c")


def _all_gather(xs, name):
    n = len(xs)

    def body(*refs):
        start, forward, finish = _gather_phases(refs[:n], refs[n:2 * n], *refs[2 * n:])
        start()
        forward()
        finish()

    return pl.pallas_call(
        body, name=name,
        out_shape=_gather_shapes(xs), in_specs=[ANY] * n, out_specs=[ANY] * n, scratch_shapes=_gather_sems(n),
    )(*xs)


def _gather_shapes(xs):
    return [SDS((N_DEV,) + x.shape, x.dtype) for x in xs]


def _gather_sems(n):
    return [pltpu.SemaphoreType.DMA((7 * n,)), pltpu.SemaphoreType.DMA((7 * n,)), pltpu.SemaphoreType.DMA((n,))]


def _gather_phases(x_refs, out_refs, send_sems, recv_sems, local_sems):
    n = len(x_refs)

    def parts():
        px, py, pc = _place()
        me, sibling = (px, py, pc), (px, py, 1 - pc)
        chips = [(1 - px, py), (px, 1 - py), (1 - px, 1 - py)]

        def slot(a, qx, qy, qc):
            return out_refs[a].at[4 * qx + 2 * qy + qc]

        def copy(a, k, block, to, src=None):
            return pltpu.make_async_remote_copy(
                src_ref=slot(a, *block) if src is None else src, dst_ref=slot(a, *block),
                send_sem=send_sems.at[7 * a + k], recv_sem=recv_sems.at[7 * a + k], device_id=to, device_id_type=MESH)

        def mine():
            return [pltpu.make_async_copy(x_refs[a], slot(a, *me), local_sems.at[a]) for a in range(n)]

        def first():
            out = []
            for a in range(n):
                out.append(copy(a, 0, me, sibling, src=x_refs[a]))
                out += [copy(a, 1 + j, me, (*chip, pc), src=x_refs[a]) for j, chip in enumerate(chips)]
            return out

        def landed():
            return [copy(a, 1 + j, (*chip, pc), me) for j, chip in enumerate(chips) for a in range(n)]

        def passed():
            return [copy(a, 4 + j, (*chip, pc), sibling) for j, chip in enumerate(chips) for a in range(n)]

        def late():
            out = [copy(a, 0, sibling, me) for a in range(n)]
            return out + [copy(a, 4 + j, (*chip, 1 - pc), me) for j, chip in enumerate(chips) for a in range(n)]

        return mine, first, landed, passed, late

    def start():
        mine, first, _, _, _ = parts()
        for cp in mine() + first():
            cp.start()

    def forward():
        _, _, landed, passed, _ = parts()
        for got, cp in zip(landed(), passed()):
            got.wait_recv()
            cp.start()

    def finish():
        mine, first, _, passed, late = parts()
        for cp in late():
            cp.wait_recv()
        for cp in first() + passed():
            cp.wait_send()
        for cp in mine():
            cp.wait()

    return start, forward, finish


def _sibling_exchange(win_buf, win_starts, win_rows, blocked, name):
    nb = len(blocked)

    def body(*refs):
        win_ref, blk_refs = refs[0], refs[1:1 + nb]
        rwin_ref, rblk_refs = refs[1 + nb], refs[2 + nb:2 + 2 * nb]
        send_sems, recv_sems, wsend_sems, wrecv_sems = refs[2 + 2 * nb:]
        px, py, pc = _place()
        copies = _sibling_copies(blk_refs, rblk_refs, send_sems, recv_sems)
        for j in range(4):
            theirs = jnp.where(pc == 0, win_starts[2 * j + 1], win_starts[2 * j])
            copies.append(pltpu.make_async_remote_copy(
                src_ref=win_ref.at[pl.ds(pl.multiple_of(theirs, 8), win_rows)], dst_ref=rwin_ref.at[j],
                send_sem=wsend_sems.at[j], recv_sem=wrecv_sems.at[j], device_id=(px, py, 1 - pc), device_id_type=MESH))
        for cp in copies:
            cp.start()
        for cp in copies:
            cp.wait()

    C = win_buf.shape[1]
    return pl.pallas_call(
        body, name=name,
        out_shape=[SDS((4, win_rows, C), F32)] + _sibling_shapes(blocked),
        in_specs=[ANY] * (1 + nb), out_specs=[ANY] * (1 + nb),
        scratch_shapes=_exchange_sems(nb) + _exchange_sems(4),
    )(win_buf, *blocked)


def _sibling_shapes(blocked):
    return [SDS((4,) + b.shape[2:], F32) for b in blocked]


def _exchange_sems(n):
    return [pltpu.SemaphoreType.DMA((n,)), pltpu.SemaphoreType.DMA((n,))]


def _sibling_copies(blk_refs, out_refs, send_sems, recv_sems):
    px, py, pc = _place()
    return [pltpu.make_async_remote_copy(
        src_ref=b.at[:, 1 - pc], dst_ref=o, send_sem=send_sems.at[a], recv_sem=recv_sems.at[a],
        device_id=(px, py, 1 - pc), device_id_type=MESH) for a, (b, o) in enumerate(zip(blk_refs, out_refs))]


def _chip_shapes(ps):
    return [SDS((3,) + p.shape[1:], p.dtype) for p in ps]


def _chip_copies(p_refs, out_refs, send_sems, recv_sems):
    px, py, pc = _place()
    n = len(p_refs)
    chips = [(1 - px, py), (px, 1 - py), (1 - px, 1 - py)]
    return [pltpu.make_async_remote_copy(
        src_ref=p_refs[a].at[2 * qx + qy], dst_ref=out_refs[a].at[k],
        send_sem=send_sems.at[3 * a + k], recv_sem=recv_sems.at[3 * a + k],
        device_id=(qx, qy, pc), device_id_type=MESH) for k, (qx, qy) in enumerate(chips) for a in range(n)]


def _pair_sum(own, recv, ids, name, win_rows=None):
    _, R, C = recv.shape
    tr = _tile(R, 1024, 8)

    def body(ids_ref, g_ref, r_ref, own_ref, pb_ref):
        s = g_ref[...] + r_ref[...]
        pb_ref[...] = s.astype(BF16)

        @pl.when(pl.program_id(1) == ids_ref[1])
        def _():
            own_ref[...] = s

    if win_rows is None:
        own_spec = pl.BlockSpec((None, None, tr, C), lambda r, j, ids: (j, ids[0], r, 0))
    else:
        own_spec = pl.BlockSpec((pl.Element(tr), pl.Element(C)), lambda r, j, ids: (pl.multiple_of(ids[2 + j] + r * tr, 8), 0))
    return pl.pallas_call(
        body, name=name,
        grid_spec=pltpu.PrefetchScalarGridSpec(
            num_scalar_prefetch=1, grid=(R // tr, 4),
            in_specs=[own_spec, pl.BlockSpec((None, tr, C), lambda r, j, ids: (j, r, 0))],
            out_specs=[pl.BlockSpec((tr, C), lambda r, j, ids: (r, 0)),
                       pl.BlockSpec((None, tr, C), lambda r, j, ids: (j, r, 0))]),
        out_shape=[SDS((R, C), F32), SDS((4, R, C), BF16)],
        compiler_params=_cparams(2),
    )(ids, own, recv)


def _final_sum(own, recv, name):
    R, C = own.shape
    tr = _tile(R, 1024, 8)

    def body(o_ref, r_ref, out_ref):
        out_ref[...] = ((o_ref[...] + r_ref[0].astype(F32)) + r_ref[1].astype(F32)) + r_ref[2].astype(F32)

    return pl.pallas_call(
        body, name=name, grid=(R // tr,),
        in_specs=[pl.BlockSpec((tr, C), lambda r: (r, 0)), pl.BlockSpec((3, tr, C), lambda r: (0, r, 0))],
        out_specs=pl.BlockSpec((tr, C), lambda r: (r, 0)),
        out_shape=SDS((R, C), F32),
        compiler_params=_cparams(1),
    )(own, recv)


def _sum8(a, name):
    def body(a_ref, out_ref):
        s = a_ref[0]
        for d in range(1, N_DEV):
            s = s + a_ref[d]
        out_ref[...] = s

    return pl.pallas_call(body, name=name, out_shape=SDS(a.shape[1:], F32))(a)


def _inproj_fwd(x, g1, w_t, NM, tm):
    T, D = x.shape
    NF = w_t.shape[0] - NM
    ch = _tile(NM, 1024)

    def body(x_ref, g_ref, w_ref, pm_ref, fl_ref, h_ref):
        xv = x_ref[...]
        h = (xv * _rms(xv) * g_ref[...]).astype(BF16)
        h_ref[...] = h
        for c in range(0, NM, ch):
            pm_ref[:, c:c + ch] = _dot_nt(h, w_ref[c:c + ch, :]).astype(BF16)
        fl_ref[...] = _dot_nt(h, w_ref[NM:NM + NF, :])

    return pl.pallas_call(
        body, name="inproj_fwd", grid=(T // tm,),
        in_specs=[pl.BlockSpec((tm, D), lambda i: (i, 0)), _resident((1, D)), _resident(w_t.shape)],
        out_specs=[pl.BlockSpec((tm, NM), lambda i: (i, 0)), pl.BlockSpec((tm, NF), lambda i: (i, 0)),
                   pl.BlockSpec((tm, D), lambda i: (i, 0))],
        out_shape=[SDS((T, NM), BF16), SDS((T, NF), F32), SDS((T, D), BF16)],
        compiler_params=_cparams(1),
    )(x, g1, w_t)


def _log_sigmoid(x):
    return jnp.minimum(x, 0.0) - jnp.log(1.0 + jnp.exp(-jnp.abs(x)))


def _forget_cumsum(fl, bf, S):
    T, NF = fl.shape
    ch = _tile(S, 256, 8)

    def body(fl_ref, bf_ref, f_ref):
        row = lax.broadcasted_iota(jnp.int32, (ch, ch), 0)
        col = lax.broadcasted_iota(jnp.int32, (ch, ch), 1)
        tri = (col <= row).astype(BF16)
        carry = jnp.zeros((1, NF), F32)
        for c in range(0, S, ch):
            lf = _log_sigmoid(fl_ref[c:c + ch, :] + bf_ref[...])
            f_ref[c:c + ch, :] = _tri_dot(tri, lf) + carry
            carry = carry + jnp.sum(lf, axis=0, keepdims=True)

    return pl.pallas_call(
        body, name="forget_cumsum", grid=(T // S,),
        in_specs=[pl.BlockSpec((S, NF), lambda b: (b, 0)), _resident((1, NF))],
        out_specs=pl.BlockSpec((S, NF), lambda b: (b, 0)),
        out_shape=SDS((T, NF), F32),
        compiler_params=_cparams(1),
    )(fl, bf)


def _head_mask(e, hd):
    lanes = lax.broadcasted_iota(jnp.int32, (1, LANES), 1)
    return (lanes >= e * hd) & (lanes < (e + 1) * hd)


def _attn_specs(S, q_off, AW):
    ng = AW // LANES
    return [pl.BlockSpec((S, LANES), lambda b, g, o=q_off + w * ng: (b, o + g)) for w in range(3)]


def _attn_fwd(pm, fcum, frow, S, q_off, AW, hd, tq, gather):
    T = pm.shape[0]
    scale = float(hd) ** -0.5
    nq, hpg = S // tq, LANES // hd
    ng, nx = AW // LANES, len(gather)
    steps = (T // S) * ng

    def body(q_ref, k_ref, v_ref, fc_ref, fr_ref, *rest):
        x_refs, (o_ref, lse_ref), out_refs, sems = rest[:nx], rest[nx:nx + 2], rest[nx + 2:2 * nx + 2], rest[2 * nx + 2:]
        g = pl.program_id(1)
        step = pl.program_id(0) * ng + g
        start, forward, finish = _gather_phases(x_refs, out_refs, *sems)
        pl.when(step == 0)(start)

        @pl.when(g == 0)
        def _():
            lse_ref[...] = jnp.zeros_like(lse_ref)

        lanes = lax.broadcasted_iota(jnp.int32, (1, LANES), 1)
        for i in range(nq):
            rs, kend = slice(i * tq, (i + 1) * tq), (i + 1) * tq
            row = i * tq + lax.broadcasted_iota(jnp.int32, (tq, kend), 0)
            col = lax.broadcasted_iota(jnp.int32, (tq, kend), 1)
            o_tile = jnp.zeros((tq, LANES), F32)
            lse_tile = lse_ref[rs, :]
            for e in range(hpg):
                mask = _head_mask(e, hd)
                qs = jnp.where(mask, q_ref[rs, :], 0) * scale
                s = _dot_nt(qs, k_ref[0:kend, :]) + _lane_pick(fc_ref[rs, :], g * hpg + e) - fr_ref[e:e + 1, 0:kend]
                s = jnp.where(col <= row, s, -1e30)
                m = jnp.max(s, axis=1, keepdims=True)
                p = jnp.exp(s - m)
                l = jnp.sum(p, axis=1, keepdims=True)
                o_tile = jnp.where(mask, _dot(p.astype(BF16), v_ref[0:kend, :]) / l, o_tile)
                lse_tile = jnp.where(lanes == g * hpg + e, m + jnp.log(l), lse_tile)
            o_ref[rs, :] = o_tile.astype(BF16)
            lse_ref[rs, :] = lse_tile
        pl.when(step == (steps * 5) // 8)(forward)
        pl.when(step == steps - 1)(finish)

    full = pl.BlockSpec((S, LANES), lambda b, g: (b, 0))
    return pl.pallas_call(
        body, name="attn_fwd", grid=(T // S, ng),
        in_specs=_attn_specs(S, q_off, AW) + [full, pl.BlockSpec((None, None, 8, S), lambda b, g: (b, g, 0, 0))] + [ANY] * nx,
        out_specs=[pl.BlockSpec((S, LANES), lambda b, g: (b, g)), full] + [ANY] * nx,
        out_shape=[SDS((T, AW), BF16), SDS((T, LANES), F32)] + _gather_shapes(gather),
        scratch_shapes=_gather_sems(nx),
        compiler_params=_cparams(2),
    )(pm, pm, pm, fcum, frow, *gather)


def _merge_fwd(pm, o, x, wc, bg, woc_t, woa_t, wo, S, tm):
    T, D = x.shape
    CW, AW = woc_t.shape[1], woa_t.shape[1]
    tps = S // tm

    def body(cb_ref, cc_ref, cin_ref, gc_ref, ga_ref, o_ref, x_ref, wc_ref, bg_ref, woct_ref, woat_ref, wo_ref,
             x2_ref, u_ref, m_ref, ycin_ref, yc_ref, ya_ref, tail_ref):
        @pl.when(pl.program_id(0) % tps == 0)
        def _():
            tail_ref[...] = jnp.zeros_like(tail_ref)

        z = cc_ref[...].astype(F32) * cin_ref[...].astype(F32)
        z1, z2 = _taps_back(z, tail_ref[6:7, :], tail_ref[7:8, :])
        tail_ref[...] = z[tm - 8:tm, :]
        u = (z * wc_ref[2:3, :] + z2 * wc_ref[0:1, :]) + z1 * wc_ref[1:2, :]
        u_ref[...] = u.astype(BF16)
        ycin = (cb_ref[...].astype(F32) * u).astype(BF16)
        ycin_ref[...] = ycin
        yc = _dot_nt(ycin, woct_ref[...])
        ya = _dot_nt(o_ref[...], woat_ref[...])
        yc_ref[...] = yc.astype(BF16)
        ya_ref[...] = ya.astype(BF16)
        gc = _sigmoid(gc_ref[...].astype(F32) + bg_ref[:, 0:D])
        ga = _sigmoid(ga_ref[...].astype(F32) + bg_ref[:, D:2 * D])
        m = (gc * yc + ga * ya).astype(BF16)
        m_ref[...] = m
        x2_ref[...] = x_ref[...] + _dot(m, wo_ref[...])

    g_off = (3 * CW + 3 * AW) // D
    tok = lambda w, j=0: pl.BlockSpec((tm, w), lambda i: (i, j))
    return pl.pallas_call(
        body, name="merge_fwd", grid=(T // tm,),
        in_specs=[tok(CW, 0), tok(CW, 1), tok(CW, 2), tok(D, g_off), tok(D, g_off + 1), tok(AW), tok(D),
                  _resident((8, CW)), _resident((1, 2 * D)), _resident((D, CW)), _resident((D, AW)), _resident((D, D))],
        out_specs=[tok(D), tok(CW), tok(D), tok(CW), tok(D), tok(D)],
        out_shape=[SDS((T, D), F32), SDS((T, CW), BF16), SDS((T, D), BF16), SDS((T, CW), BF16),
                   SDS((T, D), BF16), SDS((T, D), BF16)],
        scratch_shapes=[pltpu.VMEM((8, CW), F32)],
        compiler_params=_cparams(1),
    )(pm, pm, pm, pm, pm, o, x, wc, bg, woc_t, woa_t, wo)


def _ffn_fwd(x2, g2, wup_t, wf, wdn, gf, tgt, S, tm):
    T, D = x2.shape
    FH = wdn.shape[0]
    ch = _tile(FH, FFN_CHUNK)
    tps = S // tm

    def body(x2_ref, g2_ref, wupt_ref, wf_ref, wdn_ref, gf_ref, tgt_ref,
             upp_ref, up_ref, act_ref, h2_ref, dx3_ref, loss_ref, dgf_ref, tail_ref):
        i = pl.program_id(0)

        @pl.when(i % tps == 0)
        def _():
            tail_ref[...] = jnp.zeros_like(tail_ref)

        @pl.when(i == 0)
        def _():
            loss_ref[...] = jnp.zeros_like(loss_ref)
            dgf_ref[...] = jnp.zeros_like(dgf_ref)

        x2v = x2_ref[...]
        h2 = (x2v * _rms(x2v) * g2_ref[...]).astype(BF16)
        h2_ref[...] = h2
        x3 = x2v
        for c in range(0, FH, ch):
            gated = []
            for cols in (slice(c, c + ch), slice(FH + c, FH + c + ch)):
                upp = _dot_nt(h2, wupt_ref[cols, :])
                upp_ref[:, cols] = upp.astype(BF16)
                p1, p2 = _taps_back(upp, tail_ref[6:7, cols], tail_ref[7:8, cols])
                tail_ref[:, cols] = upp[tm - 8:tm, :]
                up = (upp * wf_ref[2:3, cols] + p2 * wf_ref[0:1, cols]) + p1 * wf_ref[1:2, cols]
                up_ref[:, cols] = up.astype(BF16)
                gated.append(up)
            a, b = gated
            act = (a * _sigmoid(a) * b).astype(BF16)
            act_ref[:, c:c + ch] = act
            x3 = x3 + _dot(act, wdn_ref[c:c + ch, :])
        r3 = _rms(x3)
        xn3 = x3 * r3
        e = xn3 * gf_ref[...] - tgt_ref[...]
        loss_ref[...] += 0.5 * jnp.sum(jnp.mean(e * e, axis=-1, keepdims=True), axis=0, keepdims=True)
        dy = e / D
        dgf_ref[0:1, :] += jnp.sum(dy * xn3, axis=0, keepdims=True)
        dxn = dy * gf_ref[...]
        dx3_ref[...] = r3 * (dxn - xn3 * jnp.mean(dxn * xn3, axis=-1, keepdims=True))

    tok = lambda w: pl.BlockSpec((tm, w), lambda i: (i, 0))
    return pl.pallas_call(
        body, name="ffn_fwd", grid=(T // tm,),
        in_specs=[tok(D), _resident((1, D)), _resident((2 * FH, D)), _resident((8, 2 * FH)), _resident((FH, D)),
                  _resident((1, D)), tok(D)],
        out_specs=[tok(2 * FH), tok(2 * FH), tok(FH), tok(D), tok(D), pl.BlockSpec((8, LANES), lambda i: (0, 0)),
                   pl.BlockSpec((8, D), lambda i: (0, 0))],
        out_shape=[SDS((T, 2 * FH), BF16), SDS((T, 2 * FH), BF16), SDS((T, FH), BF16), SDS((T, D), BF16), SDS((T, D), F32),
                   SDS((8, LANES), F32), SDS((8, D), F32)],
        scratch_shapes=[pltpu.VMEM((8, 2 * FH), F32)],
        compiler_params=_cparams(1),
    )(x2, g2, wup_t, wf, wdn, gf, tgt)


def _ffn_bwd(dx3, x2, g2, upp, up, wf, wdn, wup_t, S, tm):
    T, D = x2.shape
    FH = wdn.shape[0]
    ch = _tile(FH, FFN_CHUNK)
    n, tps = T // tm, S // tm

    def body(dx3_ref, x2_ref, g2_ref, upp_ref, up_ref, wf_ref, wdn_ref, wupt_ref,
             dx2_ref, dupp_ref, dwf_ref, dg2_ref, head_ref):
        i = pl.program_id(0)
        t = n - 1 - i

        @pl.when(i == 0)
        def _():
            dwf_ref[...] = jnp.zeros_like(dwf_ref)
            dg2_ref[...] = jnp.zeros_like(dg2_ref)

        @pl.when(t % tps == tps - 1)
        def _():
            head_ref[...] = jnp.zeros_like(head_ref)

        dx3v = dx3_ref[...]
        dx3b = dx3v.astype(BF16)
        x2v = x2_ref[...]
        r2 = _rms(x2v)
        xn2 = x2v * r2
        dh2 = jnp.zeros((tm, D), F32)
        for c in range(0, FH, ch):
            ca, cb = slice(c, c + ch), slice(FH + c, FH + c + ch)
            a, b = up_ref[:, ca].astype(F32), up_ref[:, cb].astype(F32)
            sig = _sigmoid(a)
            sl = a * sig
            dact = _dot_nt(dx3b, wdn_ref[ca, :])
            grads = (dact * b * (sig + sl * (1.0 - sig)), dact * sl)
            for cols, d in zip((ca, cb), grads):
                u0 = upp_ref[:, cols].astype(F32)
                d1, d2 = _taps_ahead(d, head_ref[0:1, cols], head_ref[1:2, cols])
                head_ref[:, cols] = d[0:8, :]
                dwf_ref[2:3, cols] += jnp.sum(u0 * d, axis=0, keepdims=True)
                dwf_ref[1:2, cols] += jnp.sum(u0 * d1, axis=0, keepdims=True)
                dwf_ref[0:1, cols] += jnp.sum(u0 * d2, axis=0, keepdims=True)
                dpre = ((d * wf_ref[2:3, cols] + d1 * wf_ref[1:2, cols]) + d2 * wf_ref[0:1, cols]).astype(BF16)
                dupp_ref[:, cols] = dpre
                dh2 = dh2 + _dot(dpre, wupt_ref[cols, :])
        dg2_ref[0:1, :] += jnp.sum(dh2 * xn2, axis=0, keepdims=True)
        dxn = dh2 * g2_ref[...]
        dx2_ref[...] = dx3v + r2 * (dxn - xn2 * jnp.mean(dxn * xn2, axis=-1, keepdims=True))

    tok = lambda w: pl.BlockSpec((tm, w), lambda i: (n - 1 - i, 0))
    acc = lambda w: pl.BlockSpec((8, w), lambda i: (0, 0))
    return pl.pallas_call(
        body, name="ffn_bwd", grid=(n,),
        in_specs=[tok(D), tok(D), _resident((1, D)), tok(2 * FH), tok(2 * FH), _resident((8, 2 * FH)),
                  _resident((FH, D)), _resident((2 * FH, D))],
        out_specs=[tok(D), tok(2 * FH), acc(2 * FH), acc(D)],
        out_shape=[SDS((T, D), F32), SDS((T, 2 * FH), BF16), SDS((8, 2 * FH), F32), SDS((8, D), F32)],
        scratch_shapes=[pltpu.VMEM((8, 2 * FH), F32)],
        compiler_params=_cparams(1),
    )(dx3, x2, g2, upp, up, wf, wdn, wup_t)


def _merge_bwd(dx2, pm, u, yc, ya, m, ycin, o, wc, bg, wo, woc_t, woa_t, S, tm, exchange):
    T, D = dx2.shape
    CW, AW = woc_t.shape[1], woa_t.shape[1]
    n, tps = T // tm, S // tm
    nx = len(exchange)
    rb = D // N_DEV
    assert CW + AW == D

    def body(dx2_ref, cb_ref, cc_ref, cin_ref, gc_ref, ga_ref, u_ref, yc_ref, ya_ref, m_ref, ycin_ref, o_ref,
             wc_ref, bg_ref, wo_ref, woct_ref, woat_ref, *rest):
        x_refs = rest[:nx]
        dconv_ref, dgl_ref, do_ref, dws_ref, dwc_ref, dbg_ref = rest[nx:nx + 6]
        got_refs, head_ref, sems = rest[nx + 6:2 * nx + 6], rest[2 * nx + 6], rest[2 * nx + 7:]

        def add_blocks(rows, cols, grad):
            for d in range(N_DEV):
                dws_ref[d, rows, cols] += grad[d * rb:(d + 1) * rb, :]

        i = pl.program_id(0)
        t = n - 1 - i

        @pl.when(i == 0)
        def _():
            for cp in _sibling_copies(x_refs, got_refs, *sems):
                cp.start()

        @pl.when(i == 0)
        def _():
            for ref in (dwc_ref, dbg_ref, dws_ref):
                ref[...] = jnp.zeros_like(ref)

        @pl.when(t % tps == tps - 1)
        def _():
            head_ref[...] = jnp.zeros_like(head_ref)

        dx2b = dx2_ref[...].astype(BF16)
        add_blocks(slice(0, rb), slice(0, D), _dot_tn(m_ref[...], dx2b))
        dm = _dot_nt(dx2b, wo_ref[...])
        outs = []
        for g_ref, y_ref, cols in ((gc_ref, yc_ref, slice(0, D)), (ga_ref, ya_ref, slice(D, 2 * D))):
            g = _sigmoid(g_ref[...].astype(F32) + bg_ref[:, cols])
            dgl = dm * y_ref[...].astype(F32) * g * (1.0 - g)
            dgl_ref[:, cols] = dgl.astype(BF16)
            dbg_ref[0:1, cols] += jnp.sum(dgl, axis=0, keepdims=True)
            outs.append((dm * g).astype(BF16))
        dyc, dya = outs
        add_blocks(slice(rb, 2 * rb), slice(0, CW), _dot_tn(dyc, ycin_ref[...]))
        add_blocks(slice(rb, 2 * rb), slice(CW, CW + AW), _dot_tn(dya, o_ref[...]))
        do_ref[...] = _dot(dya, woat_ref[...]).astype(BF16)
        dycin = _dot(dyc, woct_ref[...])
        cc, cin = cc_ref[...].astype(F32), cin_ref[...].astype(F32)
        z = cc * cin
        du = dycin * cb_ref[...].astype(F32)
        du1, du2 = _taps_ahead(du, head_ref[0:1, :], head_ref[1:2, :])
        head_ref[...] = du[0:8, :]
        dwc_ref[2:3, :] += jnp.sum(z * du, axis=0, keepdims=True)
        dwc_ref[1:2, :] += jnp.sum(z * du1, axis=0, keepdims=True)
        dwc_ref[0:1, :] += jnp.sum(z * du2, axis=0, keepdims=True)
        dz = (du * wc_ref[2:3, :] + du1 * wc_ref[1:2, :]) + du2 * wc_ref[0:1, :]
        dconv_ref[:, 0:CW] = (dycin * u_ref[...].astype(F32)).astype(BF16)
        dconv_ref[:, CW:2 * CW] = (dz * cin).astype(BF16)
        dconv_ref[:, 2 * CW:3 * CW] = (dz * cc).astype(BF16)

        @pl.when(i == n - 1)
        def _():
            for cp in _sibling_copies(x_refs, got_refs, *sems):
                cp.wait()

    g_off = (3 * CW + 3 * AW) // D
    tok = lambda w, j=0: pl.BlockSpec((tm, w), lambda i: (n - 1 - i, j))
    acc = lambda w: pl.BlockSpec((8, w), lambda i: (0, 0))
    return pl.pallas_call(
        body, name="merge_bwd", grid=(n,),
        in_specs=[tok(D), tok(CW, 0), tok(CW, 1), tok(CW, 2), tok(D, g_off), tok(D, g_off + 1), tok(CW), tok(D), tok(D),
                  tok(D), tok(CW), tok(AW),
                  _resident((8, CW)), _resident((1, 2 * D)), _resident((D, D)), _resident((D, CW)), _resident((D, AW))]
                 + [ANY] * nx,
        out_specs=[tok(3 * CW), tok(2 * D), tok(AW), pl.BlockSpec((N_DEV, 2 * rb, D), lambda i: (0, 0, 0)), acc(CW),
                   acc(2 * D)] + [ANY] * nx,
        out_shape=[SDS((T, 3 * CW), BF16), SDS((T, 2 * D), BF16), SDS((T, AW), BF16), SDS((N_DEV, 2 * rb, D), F32),
                   SDS((8, CW), F32), SDS((8, 2 * D), F32)] + _sibling_shapes(exchange),
        scratch_shapes=[pltpu.VMEM((8, CW), F32)] + _exchange_sems(nx),
        compiler_params=_cparams(1),
    )(dx2, pm, pm, pm, pm, pm, u, yc, ya, m, ycin, o, wc, bg, wo, woc_t, woa_t, *exchange)


def _attn_bwd(pm, do, fcum, frow, lse, S, q_off, AW, hd, tq, exchange):
    T = pm.shape[0]
    scale = float(hd) ** -0.5
    nq, hpg, ng = S // tq, LANES // hd, AW // LANES
    nx = len(exchange)
    steps = (T // S) * ng

    def body(q_ref, k_ref, v_ref, do_ref, fc_ref, fr_ref, lse_ref, *rest):
        x_refs, (dq_ref, dk_ref, dv_ref, dfr_ref) = rest[:nx], rest[nx:nx + 4]
        got_refs, (dk_acc, dv_acc), sems = rest[nx + 4:2 * nx + 4], rest[2 * nx + 4:2 * nx + 6], rest[2 * nx + 6:]
        g = pl.program_id(1)
        step = pl.program_id(0) * ng + g

        @pl.when(step == 0)
        def _():
            for cp in _chip_copies(x_refs, got_refs, *sems):
                cp.start()

        dk_acc[...] = jnp.zeros_like(dk_acc)
        dv_acc[...] = jnp.zeros_like(dv_acc)
        dfr_ref[...] = jnp.zeros_like(dfr_ref)
        for i in range(nq):
            rs, kend = slice(i * tq, (i + 1) * tq), (i + 1) * tq
            row = i * tq + lax.broadcasted_iota(jnp.int32, (tq, kend), 0)
            col = lax.broadcasted_iota(jnp.int32, (tq, kend), 1)
            kk, vv = k_ref[0:kend, :], v_ref[0:kend, :]
            dq_tile = jnp.zeros((tq, LANES), F32)
            for e in range(hpg):
                mask = _head_mask(e, hd)
                qs = jnp.where(mask, q_ref[rs, :], 0) * scale
                doi = jnp.where(mask, do_ref[rs, :], 0)
                s = _dot_nt(qs, kk) + _lane_pick(fc_ref[rs, :], g * hpg + e) - fr_ref[e:e + 1, 0:kend]
                p = jnp.where(col <= row, jnp.exp(s - _lane_pick(lse_ref[rs, :], g * hpg + e)), 0.0)
                dp = _dot_nt(doi, vv)
                ds = p * (dp - jnp.sum(p * dp, axis=1, keepdims=True))
                pb, dsb = p.astype(BF16), ds.astype(BF16)
                dq_tile = jnp.where(mask, _dot(dsb, kk) * scale, dq_tile)
                dv_acc[0:kend, :] += _dot_tn(pb, doi)
                dk_acc[0:kend, :] += _dot_tn(dsb, qs)
                dfr_ref[e:e + 1, 0:kend] -= jnp.sum(ds, axis=0, keepdims=True)
            dq_ref[rs, :] = dq_tile.astype(BF16)
        dk_ref[...] = dk_acc[...].astype(BF16)
        dv_ref[...] = dv_acc[...].astype(BF16)

        @pl.when(step == steps - 1)
        def _():
            for cp in _chip_copies(x_refs, got_refs, *sems):
                cp.wait()

    full = pl.BlockSpec((S, LANES), lambda b, g: (b, 0))
    grp = pl.BlockSpec((S, LANES), lambda b, g: (b, g))
    rows = pl.BlockSpec((None, None, 8, S), lambda b, g: (b, g, 0, 0))
    return pl.pallas_call(
        body, name="attn_bwd", grid=(T // S, ng),
        in_specs=_attn_specs(S, q_off, AW) + [grp, full, rows, full] + [ANY] * nx,
        out_specs=[grp, grp, grp, rows] + [ANY] * nx,
        out_shape=[SDS((T, AW), BF16), SDS((T, AW), BF16), SDS((T, AW), BF16), SDS((T // S, ng, 8, S), F32)]
                  + _chip_shapes(exchange),
        scratch_shapes=[pltpu.VMEM((S, LANES), F32), pltpu.VMEM((S, LANES), F32)] + _exchange_sems(3 * nx),
        compiler_params=_cparams(2),
    )(pm, pm, pm, do, fcum, frow, lse, *exchange)


def _forget_bwd(dfc, fl, bf, S):
    T, NF = fl.shape
    ch = _tile(S, 256, 8)

    def body(df_ref, fl_ref, bf_ref, dfl_ref, dbf_ref):
        @pl.when(pl.program_id(0) == 0)
        def _():
            dbf_ref[...] = jnp.zeros_like(dbf_ref)

        row = lax.broadcasted_iota(jnp.int32, (ch, ch), 0)
        col = lax.broadcasted_iota(jnp.int32, (ch, ch), 1)
        tri = (col >= row).astype(BF16)
        carry = jnp.zeros((1, NF), F32)
        for c in range(S - ch, -1, -ch):
            d = df_ref[c:c + ch, :]
            dlf = _tri_dot(tri, d) + carry
            carry = carry + jnp.sum(d, axis=0, keepdims=True)
            dfl = dlf * _sigmoid(-(fl_ref[c:c + ch, :] + bf_ref[...]))
            dfl_ref[c:c + ch, :] = dfl.astype(BF16)
            dbf_ref[0:1, :] += jnp.sum(dfl, axis=0, keepdims=True)

    return pl.pallas_call(
        body, name="forget_bwd", grid=(T // S,),
        in_specs=[pl.BlockSpec((S, NF), lambda b: (b, 0)), pl.BlockSpec((S, NF), lambda b: (b, 0)), _resident((1, NF))],
        out_specs=[pl.BlockSpec((S, NF), lambda b: (b, 0)), pl.BlockSpec((8, NF), lambda b: (0, 0))],
        out_shape=[SDS((T, NF), BF16), SDS((8, NF), F32)],
        compiler_params=_cparams(1),
    )(dfc, fl, bf)


def _inproj_bwd(dparts, offs, w_t, x, g1, dx2, tm, exchange):
    T, D = x.shape
    npart, nx = len(dparts), len(exchange)
    n = T // tm

    def body(*refs):
        d_refs = refs[:npart]
        w_ref, x_ref, g_ref, dx2_ref = refs[npart:npart + 4]
        x_refs = refs[npart + 4:npart + 4 + nx]
        dx_ref, dg_ref = refs[npart + 4 + nx:npart + 6 + nx]
        got_refs, sems = refs[npart + 6 + nx:npart + 6 + 2 * nx], refs[npart + 6 + 2 * nx:]

        @pl.when(pl.program_id(0) == 0)
        def _():
            for cp in _chip_copies(x_refs, got_refs, *sems):
                cp.start()

        @pl.when(pl.program_id(0) == 0)
        def _():
            dg_ref[...] = jnp.zeros_like(dg_ref)

        dh = None
        for d_ref, off in zip(d_refs, offs):
            term = _dot(d_ref[...], w_ref[off:off + d_ref.shape[1], :])
            dh = term if dh is None else dh + term
        xv = x_ref[...]
        r = _rms(xv)
        xn = xv * r
        dg_ref[0:1, :] += jnp.sum(dh * xn, axis=0, keepdims=True)
        dxn = dh * g_ref[...]
        dx_ref[...] = dx2_ref[...] + r * (dxn - xn * jnp.mean(dxn * xn, axis=-1, keepdims=True))

        @pl.when(pl.program_id(0) == n - 1)
        def _():
            for cp in _chip_copies(x_refs, got_refs, *sems):
                cp.wait()

    tok = lambda w: pl.BlockSpec((tm, w), lambda i: (i, 0))
    return pl.pallas_call(
        body, name="inproj_bwd", grid=(n,),
        in_specs=[tok(d.shape[1]) for d in dparts] + [_resident(w_t.shape), tok(D), _resident((1, D)), tok(D)] + [ANY] * nx,
        out_specs=[tok(D), pl.BlockSpec((8, D), lambda i: (0, 0))] + [ANY] * nx,
        out_shape=[SDS((T, D), F32), SDS((8, D), F32)] + _chip_shapes(exchange),
        scratch_shapes=_exchange_sems(3 * nx),
        compiler_params=_cparams(1),
    )(*dparts, w_t, x, g1, dx2, *exchange)


def _wgrad(bs, a, name, into=None, row_off=0, total_rows=None):
    bs = list(bs) if isinstance(bs, (list, tuple)) else [bs]
    P = len(bs)
    T, N = bs[0].shape
    M = a.shape[1]
    tn = _tile(N, 1408 if M <= 1024 else 512)
    while row_off % tn:
        tn = _tile(N, tn - LANES)
    tk = _tile(T, 512, 16)
    per_row = 2 * (P * tn * bs[0].dtype.itemsize + M * a.dtype.itemsize)
    while T % (2 * tk) == 0 and 2 * tk * per_row + 2 * tn * M * 4 <= WGRAD_VMEM_BUDGET:
        tk *= 2
    blk0, nj, nk = row_off // tn, N // tn, T // tk

    def body(*refs):
        b_refs, a_ref, o_ref = refs[:P], refs[P], refs[-1]
        j = pl.program_id(0)

        @pl.when(pl.program_id(1) == 0)
        def _():
            o_ref[...] = jnp.zeros_like(o_ref)

        for p, b_ref in enumerate(b_refs):
            @pl.when((j >= p * nj) & (j < (p + 1) * nj))
            def _(b_ref=b_ref):
                o_ref[...] += _dot_tn(b_ref[...].astype(BF16), a_ref[...].astype(BF16))

    def b_spec(p):
        def index(j, k):
            before, mine = j < p * nj, (j >= p * nj) & (j < (p + 1) * nj)
            return (jnp.where(mine, k, jnp.where(before, 0, nk - 1)),
                    jnp.where(mine, j - p * nj, jnp.where(before, 0, nj - 1)))
        return pl.BlockSpec((tk, tn), index)

    in_specs = [b_spec(p) for p in range(P)] + [pl.BlockSpec((tk, M), lambda j, k: (k, 0))]
    args = (*bs, a)
    kwargs = {}
    if into is not None:
        in_specs.append(ANY)
        args += (into,)
        kwargs["input_output_aliases"] = {P + 1: 0}
        total_rows = into.shape[0]
    return pl.pallas_call(
        body, name=name, grid=(P * nj, nk),
        in_specs=in_specs,
        out_specs=pl.BlockSpec((tn, M), lambda j, k: (blk0 + j, 0)),
        out_shape=SDS((P * N if total_rows is None else total_rows, M), F32),
        compiler_params=_cparams(2), **kwargs,
    )(*args)


def _adamw(w, g, m, v, name):
    shape = w.shape
    C = shape[-1]
    w2, g2, m2, v2 = (a.reshape(-1, C) for a in (w, g, m, v))
    R = w2.shape[0]
    tr = R if R <= 512 else _tile(R, 512, 8)
    if tr < 64:
        tr = R

    def body(w_ref, g_ref, m_ref, v_ref, d_ref, nm_ref, nv_ref):
        gv = g_ref[...]
        mv = ADAM_B1 * m_ref[...] + (1.0 - ADAM_B1) * gv
        vv = ADAM_B2 * v_ref[...] + (1.0 - ADAM_B2) * (gv * gv)
        m_hat = mv / (1.0 - ADAM_B1 ** ADAM_STEP)
        v_hat = vv / (1.0 - ADAM_B2 ** ADAM_STEP)
        d_ref[...] = -ADAM_LR * (m_hat / (jnp.sqrt(v_hat) + ADAM_EPS) + ADAM_WD * w_ref[...])
        nm_ref[...] = mv
        nv_ref[...] = vv

    spec = pl.BlockSpec((tr, C), lambda r: (r, 0))
    outs = pl.pallas_call(
        body, name=name, grid=(R // tr,),
        in_specs=[spec] * 4, out_specs=[spec] * 3, out_shape=[SDS((R, C), F32)] * 3,
        compiler_params=_cparams(1),
    )(w2, g2, m2, v2)
    return tuple(o.reshape(shape) for o in outs)


def _rows(a, L):
    lead = a.shape[0]
    flat = a.reshape(lead, -1)
    n = flat.shape[1]
    r = -(-n // L)
    return jnp.pad(flat, ((0, 0), (0, r * L - n))).reshape(lead, r, L)


def _unrows(p, shape):
    return p.reshape(-1)[:int(np.prod(shape))].reshape(shape)


def _from_col_blocks(a):
    n, R, c = a.shape
    return a.transpose(1, 0, 2).reshape(R, n * c)


def _in_windows(n_loc, nqkv, H, NM):
    win = (n_loc + 7 + 7) // 8 * 8
    starts, index = [], np.zeros((N_DEV, n_loc), np.int32)
    for d in range(N_DEV):
        rows = np.arange(n_loc * d, n_loc * (d + 1))
        is_f = (rows >= nqkv) & (rows < nqkv + H)
        kept = np.where(rows < nqkv, rows, rows - H)
        lo = int(kept[~is_f].min())
        start = lo // 8 * 8
        assert int(kept[~is_f].max()) - start < win and start + win <= NM + LANES
        starts.append(start)
        index[d] = np.where(is_f, win + rows - nqkv, kept - start)
    return starts, win, index


def kernel(x, norm_mix_g, w_in, b_f, b_gate, conv_mix_w, w_out_conv, w_out_attn, w_o, norm_ffn_g, w_up, conv_ffn_w, w_down, norm_f_g, loss_target, m_norm_mix_g, m_w_in, m_b_f, m_b_gate, m_conv_mix_w, m_w_out_conv, m_w_out_attn, m_w_o, m_norm_ffn_g, m_w_up, m_conv_ffn_w, m_w_down, m_norm_f_g, v_norm_mix_g, v_w_in, v_b_f, v_b_gate, v_conv_mix_w, v_w_out_conv, v_w_out_attn, v_w_o, v_norm_ffn_g, v_w_up, v_conv_ffn_w, v_w_down, v_norm_f_g):
    Bl, S, D = x.shape
    T = Bl * S
    H = b_f.shape[-1]
    CW = N_DEV * conv_mix_w.shape[-1]
    AW = w_out_attn.shape[1]
    hd = AW // H
    FH = N_DEV * w_down.shape[1]
    n_loc = w_in.shape[-1]
    NIN = N_DEV * n_loc
    NM = 3 * CW + 3 * AW + 2 * D
    nqkv = 3 * CW + 3 * AW
    assert NIN == NM + H and w_out_conv.shape[1] == CW and nqkv % D == 0 and AW % LANES == 0 and LANES % hd == 0
    hpg, ng = LANES // hd, AW // LANES
    assert hpg <= 8
    tm_big = min(512, S // 2)
    tm_ffn = min(256, S // 2)
    tq_fwd = min(512, S // 2)
    tq_bwd = min(256, S // 2)
    px, py, pc = _place()
    me = 4 * px + 2 * py + pc

    bits = lambda a: lax.bitcast_convert_type(a, BF16)
    taps = jnp.concatenate([_rows(bits(conv_ffn_w[0])[None], D)[0], _rows(bits(conv_mix_w[0])[None], D)[0]], axis=0)
    n_ffn_rows = -(-conv_ffn_w[0].size * 2 // D)
    late = [w_up[0].T.astype(BF16), w_down[0].astype(BF16), w_o[0].astype(BF16), w_out_conv[0].T.astype(BF16),
            w_out_attn[0].T.astype(BF16), taps]
    g_in, = _all_gather([w_in[0].T.astype(BF16)], "weights_all_gather")
    W_in_rows = g_in.reshape(NIN, D)
    W_in_t = jnp.concatenate([W_in_rows[:nqkv], W_in_rows[nqkv + H:], W_in_rows[nqkv:nqkv + H],
                              jnp.zeros((LANES - H, D), BF16)], axis=0)
    bf128 = jnp.pad(b_f, ((0, 0), (0, LANES - H)))

    x2d = x.reshape(T, D)
    tgt = loss_target.reshape(T, D)
    pm, fl, h1 = _inproj_fwd(x2d, norm_mix_g, W_in_t, NM, tm_big)
    fcum = _forget_cumsum(fl, bf128, S)
    frow = jnp.pad(fcum[:, :H].reshape(Bl, S, ng, hpg).transpose(0, 2, 3, 1), ((0, 0), (0, 0), (0, 8 - hpg), (0, 0)))
    q_off = 3 * CW // LANES
    o, lse, g_up, g_dn, g_o, g_oc, g_oa, g_taps = _attn_fwd(pm, fcum, frow, S, q_off, AW, hd, tq_fwd, late)
    W_up_t = g_up.reshape(2 * FH, D)
    W_dn = g_dn.reshape(FH, D)
    W_o = g_o.reshape(D, D)
    W_oc_t = g_oc.reshape(D, CW)
    W_oa_t = g_oa.reshape(D, AW)
    tap_bits = g_taps.reshape(N_DEV, -1)
    n_ffn, n_mix = conv_ffn_w[0].size * 2, conv_mix_w[0].size * 2
    wf_full = _from_col_blocks(lax.bitcast_convert_type(
        tap_bits[:, :n_ffn].reshape((N_DEV,) + conv_ffn_w.shape[1:] + (2,)), F32))
    wc_full = _from_col_blocks(lax.bitcast_convert_type(
        tap_bits[:, n_ffn_rows * D:n_ffn_rows * D + n_mix].reshape((N_DEV,) + conv_mix_w.shape[1:] + (2,)), F32))
    wf8 = jnp.pad(wf_full, ((0, 5), (0, 0)))
    wc8 = jnp.pad(wc_full, ((0, 5), (0, 0)))
    x2, u, m, ycin, yc, ya = _merge_fwd(pm, o, x2d, wc8, b_gate, W_oc_t, W_oa_t, W_o, S, tm_big)
    upp, up, act, h2, dx3, loss8, dgf8 = _ffn_fwd(x2, norm_ffn_g, W_up_t, wf8, W_dn, norm_f_g.reshape(1, D), tgt, S, tm_ffn)

    dx2, dupp, dwf8, dg2_8 = _ffn_bwd(dx3, x2, norm_ffn_g, upp, up, wf8, W_dn, W_up_t, S, tm_ffn)
    dW_dn = _wgrad(act, dx3, "wgrad_down")
    dW_up_t = _wgrad(dupp, h2, "wgrad_up")
    ids = jnp.stack([pc, 2 * px + py]).astype(jnp.int32)
    big = [dW_up_t.reshape(4, 2, -1, D), dW_dn.reshape(4, 2, -1, D)]
    dconv, dgl, do, dW_small, dwc8, dbg8, *sib_big = _merge_bwd(
        dx2, pm, u, yc, ya, m, ycin, o, wc8, b_gate, W_o, W_oc_t, W_oa_t, S, tm_big, big)
    big_sums = [_pair_sum(b, r, ids, "grads_pair_sum_%d" % a) for a, (b, r) in enumerate(zip(big, sib_big))]
    dq, dk, dv, dfr, *chips_big = _attn_bwd(pm, do, fcum, frow, lse, S, q_off, AW, hd, tq_bwd, [s[1] for s in big_sums])
    dfc = jnp.pad(dfr[:, :, :hpg, :].transpose(0, 3, 1, 2).reshape(T, H), ((0, 0), (0, LANES - H)))
    dfl, dbf8 = _forget_bwd(dfc, fl, bf128, S)
    dparts = [dconv, dq, dk, dv, dgl, dfl]
    offs = [0, 3 * CW, 3 * CW + AW, 3 * CW + 2 * AW, nqkv, NM]
    dW_in_t = _wgrad(dconv, h1, "wgrad_in_conv", total_rows=NM + LANES)
    dW_in_t = _wgrad([dq, dk, dv], h1, "wgrad_in_qkv", into=dW_in_t, row_off=3 * CW)
    dW_in_t = _wgrad(dgl, h1, "wgrad_in_gates", into=dW_in_t, row_off=nqkv)
    dW_in_t = _wgrad(dfl, h1, "wgrad_in_forget", into=dW_in_t, row_off=NM)

    starts, win, index = _in_windows(n_loc, nqkv, H, NM)
    small = dW_small.reshape(4, 2, -1, D)
    my_starts = [jnp.where(pc == 0, starts[2 * j], starts[2 * j + 1]) for j in range(4)]
    win_ids = jnp.stack([pc, 2 * px + py] + my_starts).astype(jnp.int32)
    sib_win, sib_small = _sibling_exchange(dW_in_t, starts, win, [small], "grads_sibling_exchange")
    sums = [_pair_sum(dW_in_t, sib_win, win_ids, "grads_pair_sum_in", win_rows=win),
            _pair_sum(small, sib_small, ids, "grads_pair_sum_small")]
    grad_x, dg1_8, *from_chips = _inproj_bwd(dparts, offs, W_in_t, x2d, norm_mix_g, dx2, tm_big, [s[1] for s in sums])
    red_win, red_small, red_up, red_dn = [
        _final_sum(s[0], r, "grads_final_sum_%d" % a)
        for a, (s, r) in enumerate(zip(sums + big_sums, list(from_chips) + list(chips_big)))]

    f_rows = dW_in_t[NM:NM + 8]
    wf_rows = _rows(dwf8[:3].reshape(1, -1), D)[0]
    wc_rows = _rows(dwc8[:3].reshape(1, -1), D)[0]
    smalls = [dg1_8[0:1], dg2_8[0:1], dgf8[0:1], dbg8[0:1, :D], dbg8[0:1, D:], jnp.pad(dbf8[0:1], ((0, 0), (0, D - LANES))),
              jnp.pad(loss8[0:1], ((0, 0), (0, D - LANES))), jnp.zeros((1, D), F32), f_rows, wf_rows, wc_rows]
    spack = jnp.concatenate(smalls, axis=0)
    spack = jnp.pad(spack, ((0, -spack.shape[0] % 8), (0, 0)))
    ssum = _sum8(_all_gather([spack], "small_all_gather")[0], "small_sum")
    g_g1, g_g2, g_gf = ssum[0:1], ssum[1:2], ssum[2]
    g_bg = jnp.concatenate([ssum[3:4], ssum[4:5]], axis=1)
    g_bf = ssum[5:6, :H]
    r0 = 16
    r1 = r0 + wf_rows.shape[0]
    wf_sum = _unrows(ssum[r0:r1], (3, 2 * FH))
    wc_sum = _unrows(ssum[r1:r1 + wc_rows.shape[0]], (3, CW))
    g_wf = lax.dynamic_slice_in_dim(wf_sum, me * conv_ffn_w.shape[-1], conv_ffn_w.shape[-1], axis=1)
    g_wc = lax.dynamic_slice_in_dim(wc_sum, me * conv_mix_w.shape[-1], conv_mix_w.shape[-1], axis=1)

    ext = jnp.concatenate([red_win, ssum[8:8 + H]], axis=0)
    my_index = lax.dynamic_index_in_dim(jnp.asarray(index), me, axis=0, keepdims=False)
    g_w_in_t = jnp.take(ext, my_index, axis=0)
    rb = D // N_DEV
    g_w_o = red_small[:rb]
    g_w_oc = red_small[rb:, :CW].T
    g_w_oa = red_small[rb:, CW:].T

    loss = ssum[6, 0]

    names = ["norm_mix_g", "w_in", "b_f", "b_gate", "conv_mix_w", "w_out_conv", "w_out_attn", "w_o", "norm_ffn_g", "w_up",
             "conv_ffn_w", "w_down", "norm_f_g"]
    weights = [norm_mix_g, w_in, b_f, b_gate, conv_mix_w, w_out_conv, w_out_attn, w_o, norm_ffn_g, w_up, conv_ffn_w, w_down, norm_f_g]
    grads = [g_g1, g_w_in_t, g_bf, g_bg, g_wc, g_w_oc, g_w_oa, g_w_o, g_g2, red_up, g_wf, red_dn, g_gf]
    ms = [m_norm_mix_g, m_w_in, m_b_f, m_b_gate, m_conv_mix_w, m_w_out_conv, m_w_out_attn, m_w_o, m_norm_ffn_g, m_w_up,
          m_conv_ffn_w, m_w_down, m_norm_f_g]
    vs = [v_norm_mix_g, v_w_in, v_b_f, v_b_gate, v_conv_mix_w, v_w_out_conv, v_w_out_attn, v_w_o, v_norm_ffn_g, v_w_up,
          v_conv_ffn_w, v_w_down, v_norm_f_g]
    to_view = {"w_in": lambda a: a[0].T.reshape(-1, LANES), "w_up": lambda a: a[0].T}
    from_view = {"w_in": lambda a: a.reshape(n_loc, D).T[None], "w_up": lambda a: a.T[None]}
    out_grads, steps = [], []
    for nm, w, g, mm, vv in zip(names, weights, grads, ms, vs):
        if nm in to_view:
            wv, mv, vw = (to_view[nm](a) for a in (w, mm, vv))
            gv = g.reshape(wv.shape)
            steps.append(tuple(from_view[nm](o) for o in _adamw(wv, gv, mv, vw, "adamw_" + nm)))
            out_grads.append(from_view[nm](gv))
        else:
            gv = g.reshape(w.shape)
            steps.append(_adamw(w, gv, mm, vv, "adamw_" + nm))
            out_grads.append(gv)
    deltas, new_ms, new_vs = zip(*steps)
    return (loss, grad_x.reshape(Bl, S, D), *out_grads, *deltas, *new_ms, *new_vs)
```

```python
import numpy as np

import jax
import jax.numpy as jnp
from jax import lax
from jax.experimental import pallas as pl
from jax.experimental.pallas import tpu as pltpu

F32, BF16 = jnp.float32, jnp.bfloat16
EPS = 1e-6
ADAM_LR, ADAM_B1, ADAM_B2, ADAM_EPS, ADAM_WD, ADAM_STEP = 0.001, 0.9, 0.999, 1e-08, 0.01, 10
N_DEV = 8
LANES = 128
V7X_VMEM_LIMIT = 56 * 1024 * 1024
FFN_CHUNK = 2816
WGRAD_VMEM_BUDGET = 40 * 1024 * 1024
MESH = pl.DeviceIdType.MESH
SDS = jax.ShapeDtypeStruct
ANY = pl.BlockSpec(memory_space=pl.ANY)


def _tile(n, target, mult=LANES):
    best = None
    for t in range(mult, min(n, target) + 1, mult):
        if n % t == 0:
            best = t
    return best if best is not None else n


def _resident(shape):
    return pl.BlockSpec(shape, lambda *_: (0,) * len(shape), pipeline_mode=pl.Buffered(1))


def _cparams(n_axes=1):
    return pltpu.CompilerParams(dimension_semantics=("arbitrary",) * n_axes, vmem_limit_bytes=V7X_VMEM_LIMIT)


def _dot(a, b):
    return jnp.dot(a, b, preferred_element_type=F32)


def _dot_tn(a, b):
    return lax.dot_general(a, b, (((0,), (0,)), ((), ())), preferred_element_type=F32)


def _dot_nt(a, b):
    return lax.dot_general(a, b, (((1,), (1,)), ((), ())), preferred_element_type=F32)


def _sigmoid(x):
    return 0.5 * jnp.tanh(0.5 * x) + 0.5


def _rms(x):
    return lax.rsqrt(jnp.mean(x * x, axis=-1, keepdims=True) + EPS)


def _taps_back(z, r6, r7):
    row = lax.broadcasted_iota(jnp.int32, (8, 1), 0)
    z1, z2 = pltpu.roll(z, 1, 0), pltpu.roll(z, 2, 0)
    z1 = jnp.concatenate([jnp.where(row == 0, r7, z1[0:8]), z1[8:]], axis=0)
    z2 = jnp.concatenate([jnp.where(row == 0, r6, jnp.where(row == 1, r7, z2[0:8])), z2[8:]], axis=0)
    return z1, z2


def _taps_ahead(d, h0, h1):
    tm = d.shape[0]
    row = lax.broadcasted_iota(jnp.int32, (8, 1), 0)
    d1, d2 = pltpu.roll(d, tm - 1, 0), pltpu.roll(d, tm - 2, 0)
    d1 = jnp.concatenate([d1[:tm - 8], jnp.where(row == 7, h0, d1[tm - 8:])], axis=0)
    d2 = jnp.concatenate([d2[:tm - 8], jnp.where(row == 6, h0, jnp.where(row == 7, h1, d2[tm - 8:]))], axis=0)
    return d1, d2


def _tri_dot(tri, x):
    hi = x.astype(BF16)
    r = x - hi.astype(F32)
    mid = r.astype(BF16)
    lo = (r - mid.astype(F32)).astype(BF16)
    return (_dot(tri, lo) + _dot(tri, mid)) + _dot(tri, hi)


def _lane_pick(block, lane):
    lanes = lax.broadcasted_iota(jnp.int32, (1, block.shape[1]), 1)
    return jnp.sum(jnp.where(lanes == lane, block, 0.0), axis=1, keepdims=True)


def _place():
    return lax.axis_index("x"), lax.axis_index("y"), lax.axis_index("c")


def _all_gather(xs, name):
    n = len(xs)

    def body(*refs):
        start, forward, finish = _gather_phases(refs[:n], refs[n:2 * n], *refs[2 * n:])
        start()
        forward()
        finish()

    return pl.pallas_call(
        body, name=name,
        out_shape=_gather_shapes(xs), in_specs=[ANY] * n, out_specs=[ANY] * n, scratch_shapes=_gather_sems(n),
    )(*xs)


def _all_gather_direct(x, name):
    def body(x_ref, out_ref, send_sems, recv_sems, local_sem):
        px, py, pc = _place()
        me = 4 * px + 2 * py + pc
        flips = [(dx, dy, dc) for dx in (0, 1) for dy in (0, 1) for dc in (0, 1) if dx or dy or dc]

        def peer(dx, dy, dc):
            return (1 - px if dx else px, 1 - py if dy else py, 1 - pc if dc else pc)

        copies = [pltpu.make_async_remote_copy(
            src_ref=x_ref, dst_ref=out_ref.at[me], send_sem=send_sems.at[k], recv_sem=recv_sems.at[k],
            device_id=peer(*f), device_id_type=MESH) for k, f in enumerate(flips)]
        mine = pltpu.make_async_copy(x_ref, out_ref.at[me], local_sem)
        mine.start()
        for cp in copies:
            cp.start()
        for k, f in enumerate(flips):
            qx, qy, qc = peer(*f)
            pltpu.make_async_remote_copy(
                src_ref=x_ref, dst_ref=out_ref.at[4 * qx + 2 * qy + qc], send_sem=send_sems.at[k], recv_sem=recv_sems.at[k],
                device_id=peer(*f), device_id_type=MESH).wait_recv()
        for cp in copies:
            cp.wait_send()
        mine.wait()

    return pl.pallas_call(
        body, name=name,
        out_shape=SDS((N_DEV,) + x.shape, x.dtype), in_specs=[ANY], out_specs=ANY,
        scratch_shapes=_exchange_sems(7) + [pltpu.SemaphoreType.DMA],
    )(x)


def _gather_shapes(xs):
    return [SDS((N_DEV,) + x.shape, x.dtype) for x in xs]


def _gather_sems(n):
    return [pltpu.SemaphoreType.DMA((7 * n,)), pltpu.SemaphoreType.DMA((7 * n,)), pltpu.SemaphoreType.DMA((n,))]


def _gather_phases(x_refs, out_refs, send_sems, recv_sems, local_sems):
    n = len(x_refs)

    def parts():
        px, py, pc = _place()
        me, sibling = (px, py, pc), (px, py, 1 - pc)
        chips = [(1 - px, py), (px, 1 - py), (1 - px, 1 - py)]

        def slot(a, qx, qy, qc):
            return out_refs[a].at[4 * qx + 2 * qy + qc]

        def copy(a, k, block, to, src=None):
            return pltpu.make_async_remote_copy(
                src_ref=slot(a, *block) if src is None else src, dst_ref=slot(a, *block),
                send_sem=send_sems.at[7 * a + k], recv_sem=recv_sems.at[7 * a + k], device_id=to, device_id_type=MESH)

        def mine():
            return [pltpu.make_async_copy(x_refs[a], slot(a, *me), local_sems.at[a]) for a in range(n)]

        def first():
            out = []
            for a in range(n):
                out.append(copy(a, 0, me, sibling, src=x_refs[a]))
                out += [copy(a, 1 + j, me, (*chip, pc), src=x_refs[a]) for j, chip in enumerate(chips)]
            return out

        def landed():
            return [copy(a, 1 + j, (*chip, pc), me) for j, chip in enumerate(chips) for a in range(n)]

        def passed():
            return [copy(a, 4 + j, (*chip, pc), sibling) for j, chip in enumerate(chips) for a in range(n)]

        def late():
            out = [copy(a, 0, sibling, me) for a in range(n)]
            return out + [copy(a, 4 + j, (*chip, 1 - pc), me) for j, chip in enumerate(chips) for a in range(n)]

        return mine, first, landed, passed, late

    def start():
        mine, first, _, _, _ = parts()
        for cp in mine() + first():
            cp.start()

    def forward():
        _, _, landed, passed, _ = parts()
        for got, cp in zip(landed(), passed()):
            got.wait_recv()
            cp.start()

    def finish():
        mine, first, _, passed, late = parts()
        for cp in late():
            cp.wait_recv()
        for cp in first() + passed():
            cp.wait_send()
        for cp in mine():
            cp.wait()

    return start, forward, finish


def _sibling_exchange(win_buf, win_starts, win_rows, blocked, name):
    nb = len(blocked)

    def body(*refs):
        win_ref, blk_refs = refs[0], refs[1:1 + nb]
        rwin_ref, rblk_refs = refs[1 + nb], refs[2 + nb:2 + 2 * nb]
        send_sems, recv_sems, wsend_sems, wrecv_sems = refs[2 + 2 * nb:]
        px, py, pc = _place()
        copies = _sibling_copies(blk_refs, rblk_refs, send_sems, recv_sems)
        for j in range(4):
            theirs = jnp.where(pc == 0, win_starts[2 * j + 1], win_starts[2 * j])
            copies.append(pltpu.make_async_remote_copy(
                src_ref=win_ref.at[pl.ds(pl.multiple_of(theirs, 8), win_rows)], dst_ref=rwin_ref.at[j],
                send_sem=wsend_sems.at[j], recv_sem=wrecv_sems.at[j], device_id=(px, py, 1 - pc), device_id_type=MESH))
        for cp in copies:
            cp.start()
        for cp in copies:
            cp.wait()

    C = win_buf.shape[1]
    return pl.pallas_call(
        body, name=name,
        out_shape=[SDS((4, win_rows, C), F32)] + _sibling_shapes(blocked),
        in_specs=[ANY] * (1 + nb), out_specs=[ANY] * (1 + nb),
        scratch_shapes=_exchange_sems(nb) + _exchange_sems(4),
    )(win_buf, *blocked)


def _sibling_shapes(blocked):
    return [SDS((4,) + b.shape[2:], F32) for b in blocked]


def _exchange_sems(n):
    return [pltpu.SemaphoreType.DMA((n,)), pltpu.SemaphoreType.DMA((n,))]


def _sibling_copies(blk_refs, out_refs, send_sems, recv_sems):
    px, py, pc = _place()
    return [pltpu.make_async_remote_copy(
        src_ref=b.at[:, 1 - pc], dst_ref=o, send_sem=send_sems.at[a], recv_sem=recv_sems.at[a],
        device_id=(px, py, 1 - pc), device_id_type=MESH) for a, (b, o) in enumerate(zip(blk_refs, out_refs))]


def _chip_shapes(ps):
    return [SDS((3,) + p.shape[1:], p.dtype) for p in ps]


def _chip_copies(p_refs, out_refs, send_sems, recv_sems):
    px, py, pc = _place()
    n = len(p_refs)
    chips = [(1 - px, py), (px, 1 - py), (1 - px, 1 - py)]
    return [pltpu.make_async_remote_copy(
        src_ref=p_refs[a].at[2 * qx + qy], dst_ref=out_refs[a].at[k],
        send_sem=send_sems.at[3 * a + k], recv_sem=recv_sems.at[3 * a + k],
        device_id=(qx, qy, pc), device_id_type=MESH) for k, (qx, qy) in enumerate(chips) for a in range(n)]


def _pair_sum(own, recv, ids, name, win_rows=None):
    _, R, C = recv.shape
    tr = _tile(R, 512, 8)

    def body(ids_ref, g_ref, r_ref, own_ref, pb_ref):
        s = g_ref[...] + r_ref[...]
        pb_ref[...] = s.astype(BF16)

        @pl.when(pl.program_id(1) == ids_ref[1])
        def _():
            own_ref[...] = s

    if win_rows is None:
        own_spec = pl.BlockSpec((None, None, tr, C), lambda r, j, ids: (j, ids[0], r, 0))
    else:
        own_spec = pl.BlockSpec((pl.Element(tr), pl.Element(C)), lambda r, j, ids: (pl.multiple_of(ids[2 + j] + r * tr, 8), 0))
    return pl.pallas_call(
        body, name=name,
        grid_spec=pltpu.PrefetchScalarGridSpec(
            num_scalar_prefetch=1, grid=(R // tr, 4),
            in_specs=[own_spec, pl.BlockSpec((None, tr, C), lambda r, j, ids: (j, r, 0))],
            out_specs=[pl.BlockSpec((tr, C), lambda r, j, ids: (r, 0)),
                       pl.BlockSpec((None, tr, C), lambda r, j, ids: (j, r, 0))]),
        out_shape=[SDS((R, C), F32), SDS((4, R, C), BF16)],
        compiler_params=_cparams(2),
    )(ids, own, recv)


def _final_sum(own, recv, name):
    R, C = own.shape
    tr = _tile(R, 512, 8)

    def body(o_ref, r_ref, out_ref):
        out_ref[...] = ((o_ref[...] + r_ref[0].astype(F32)) + r_ref[1].astype(F32)) + r_ref[2].astype(F32)

    return pl.pallas_call(
        body, name=name, grid=(R // tr,),
        in_specs=[pl.BlockSpec((tr, C), lambda r: (r, 0)), pl.BlockSpec((3, tr, C), lambda r: (0, r, 0))],
        out_specs=pl.BlockSpec((tr, C), lambda r: (r, 0)),
        out_shape=SDS((R, C), F32),
        compiler_params=_cparams(1),
    )(own, recv)


def _sum8(a, name):
    def body(a_ref, out_ref):
        s = a_ref[0]
        for d in range(1, N_DEV):
            s = s + a_ref[d]
        out_ref[...] = s

    return pl.pallas_call(body, name=name, out_shape=SDS(a.shape[1:], F32))(a)


def _inproj_fwd(x, g1, w_t, NM, tm):
    T, D = x.shape
    NF = w_t.shape[0] - NM
    ch = _tile(NM, 1024)

    def body(x_ref, g_ref, w_ref, pm_ref, fl_ref, h_ref):
        xv = x_ref[...]
        h = (xv * _rms(xv) * g_ref[...]).astype(BF16)
        h_ref[...] = h
        for c in range(0, NM, ch):
            pm_ref[:, c:c + ch] = _dot_nt(h, w_ref[c:c + ch, :]).astype(BF16)
        fl_ref[...] = _dot_nt(h, w_ref[NM:NM + NF, :])

    return pl.pallas_call(
        body, name="inproj_fwd", grid=(T // tm,),
        in_specs=[pl.BlockSpec((tm, D), lambda i: (i, 0)), _resident((1, D)), _resident(w_t.shape)],
        out_specs=[pl.BlockSpec((tm, NM), lambda i: (i, 0)), pl.BlockSpec((tm, NF), lambda i: (i, 0)),
                   pl.BlockSpec((tm, D), lambda i: (i, 0))],
        out_shape=[SDS((T, NM), BF16), SDS((T, NF), F32), SDS((T, D), BF16)],
        compiler_params=_cparams(1),
    )(x, g1, w_t)


def _log_sigmoid(x):
    return jnp.minimum(x, 0.0) - jnp.log(1.0 + jnp.exp(-jnp.abs(x)))


def _forget_cumsum(fl, bf, S):
    T, NF = fl.shape
    ch = _tile(S, 256, 8)

    def body(fl_ref, bf_ref, f_ref):
        row = lax.broadcasted_iota(jnp.int32, (ch, ch), 0)
        col = lax.broadcasted_iota(jnp.int32, (ch, ch), 1)
        tri = (col <= row).astype(BF16)
        carry = jnp.zeros((1, NF), F32)
        for c in range(0, S, ch):
            lf = _log_sigmoid(fl_ref[c:c + ch, :] + bf_ref[...])
            f_ref[c:c + ch, :] = _tri_dot(tri, lf) + carry
            carry = carry + jnp.sum(lf, axis=0, keepdims=True)

    return pl.pallas_call(
        body, name="forget_cumsum", grid=(T // S,),
        in_specs=[pl.BlockSpec((S, NF), lambda b: (b, 0)), _resident((1, NF))],
        out_specs=pl.BlockSpec((S, NF), lambda b: (b, 0)),
        out_shape=SDS((T, NF), F32),
        compiler_params=_cparams(1),
    )(fl, bf)


def _head_mask(e, hd):
    lanes = lax.broadcasted_iota(jnp.int32, (1, LANES), 1)
    return (lanes >= e * hd) & (lanes < (e + 1) * hd)


def _attn_specs(S, q_off, AW):
    ng = AW // LANES
    return [pl.BlockSpec((S, LANES), lambda b, g, o=q_off + w * ng: (b, o + g)) for w in range(3)]


def _attn_fwd(pm, fcum, frow, S, q_off, AW, hd, tq, gather):
    T = pm.shape[0]
    scale = float(hd) ** -0.5
    nq, hpg = S // tq, LANES // hd
    ng, nx = AW // LANES, len(gather)
    steps = (T // S) * ng

    def body(q_ref, k_ref, v_ref, fc_ref, fr_ref, *rest):
        x_refs, (o_ref, lse_ref), out_refs, sems = rest[:nx], rest[nx:nx + 2], rest[nx + 2:2 * nx + 2], rest[2 * nx + 2:]
        g = pl.program_id(1)
        step = pl.program_id(0) * ng + g
        start, forward, finish = _gather_phases(x_refs, out_refs, *sems)
        pl.when(step == 0)(start)

        @pl.when(g == 0)
        def _():
            lse_ref[...] = jnp.zeros_like(lse_ref)

        lanes = lax.broadcasted_iota(jnp.int32, (1, LANES), 1)
        for i in range(nq):
            rs, kend = slice(i * tq, (i + 1) * tq), (i + 1) * tq
            row = i * tq + lax.broadcasted_iota(jnp.int32, (tq, kend), 0)
            col = lax.broadcasted_iota(jnp.int32, (tq, kend), 1)
            o_tile = jnp.zeros((tq, LANES), F32)
            lse_tile = lse_ref[rs, :]
            for e in range(hpg):
                mask = _head_mask(e, hd)
                qs = jnp.where(mask, q_ref[rs, :], 0) * scale
                s = _dot_nt(qs, k_ref[0:kend, :]) + _lane_pick(fc_ref[rs, :], g * hpg + e) - fr_ref[e:e + 1, 0:kend]
                s = jnp.where(col <= row, s, -1e30)
                m = jnp.max(s, axis=1, keepdims=True)
                p = jnp.exp(s - m)
                l = jnp.sum(p, axis=1, keepdims=True)
                o_tile = jnp.where(mask, _dot(p.astype(BF16), v_ref[0:kend, :]) / l, o_tile)
                lse_tile = jnp.where(lanes == g * hpg + e, m + jnp.log(l), lse_tile)
            o_ref[rs, :] = o_tile.astype(BF16)
            lse_ref[rs, :] = lse_tile
        pl.when(step == (steps * 5) // 8)(forward)
        pl.when(step == steps - 1)(finish)

    full = pl.BlockSpec((S, LANES), lambda b, g: (b, 0))
    return pl.pallas_call(
        body, name="attn_fwd", grid=(T // S, ng),
        in_specs=_attn_specs(S, q_off, AW) + [full, pl.BlockSpec((None, None, 8, S), lambda b, g: (b, g, 0, 0))] + [ANY] * nx,
        out_specs=[pl.BlockSpec((S, LANES), lambda b, g: (b, g)), full] + [ANY] * nx,
        out_shape=[SDS((T, AW), BF16), SDS((T, LANES), F32)] + _gather_shapes(gather),
        scratch_shapes=_gather_sems(nx),
        compiler_params=_cparams(2),
    )(pm, pm, pm, fcum, frow, *gather)


def _merge_fwd(pm, o, x, wc, bg, woc_t, woa_t, wo, S, tm):
    T, D = x.shape
    CW, AW = woc_t.shape[1], woa_t.shape[1]
    tps = S // tm

    def body(cb_ref, cc_ref, cin_ref, gc_ref, ga_ref, o_ref, x_ref, wc_ref, bg_ref, woct_ref, woat_ref, wo_ref,
             x2_ref, u_ref, m_ref, ycin_ref, yc_ref, ya_ref, tail_ref):
        @pl.when(pl.program_id(0) % tps == 0)
        def _():
            tail_ref[...] = jnp.zeros_like(tail_ref)

        z = cc_ref[...].astype(F32) * cin_ref[...].astype(F32)
        z1, z2 = _taps_back(z, tail_ref[6:7, :], tail_ref[7:8, :])
        tail_ref[...] = z[tm - 8:tm, :]
        u = (z * wc_ref[2:3, :] + z2 * wc_ref[0:1, :]) + z1 * wc_ref[1:2, :]
        u_ref[...] = u.astype(BF16)
        ycin = (cb_ref[...].astype(F32) * u).astype(BF16)
        ycin_ref[...] = ycin
        yc = _dot_nt(ycin, woct_ref[...])
        ya = _dot_nt(o_ref[...], woat_ref[...])
        yc_ref[...] = yc.astype(BF16)
        ya_ref[...] = ya.astype(BF16)
        gc = _sigmoid(gc_ref[...].astype(F32) + bg_ref[:, 0:D])
        ga = _sigmoid(ga_ref[...].astype(F32) + bg_ref[:, D:2 * D])
        m = (gc * yc + ga * ya).astype(BF16)
        m_ref[...] = m
        x2_ref[...] = x_ref[...] + _dot(m, wo_ref[...])

    g_off = (3 * CW + 3 * AW) // D
    tok = lambda w, j=0: pl.BlockSpec((tm, w), lambda i: (i, j))
    return pl.pallas_call(
        body, name="merge_fwd", grid=(T // tm,),
        in_specs=[tok(CW, 0), tok(CW, 1), tok(CW, 2), tok(D, g_off), tok(D, g_off + 1), tok(AW), tok(D),
                  _resident((8, CW)), _resident((1, 2 * D)), _resident((D, CW)), _resident((D, AW)), _resident((D, D))],
        out_specs=[tok(D), tok(CW), tok(D), tok(CW), tok(D), tok(D)],
        out_shape=[SDS((T, D), F32), SDS((T, CW), BF16), SDS((T, D), BF16), SDS((T, CW), BF16),
                   SDS((T, D), BF16), SDS((T, D), BF16)],
        scratch_shapes=[pltpu.VMEM((8, CW), F32)],
        compiler_params=_cparams(1),
    )(pm, pm, pm, pm, pm, o, x, wc, bg, woc_t, woa_t, wo)


def _ffn_fwd(x2, g2, wup_t, wf, wdn, gf, tgt, S, tm):
    T, D = x2.shape
    FH = wdn.shape[0]
    ch = _tile(FH, FFN_CHUNK)
    tps = S // tm

    def body(x2_ref, g2_ref, wupt_ref, wf_ref, wdn_ref, gf_ref, tgt_ref,
             upp_ref, up_ref, act_ref, h2_ref, dx3_ref, loss_ref, dgf_ref, tail_ref):
        i = pl.program_id(0)

        @pl.when(i % tps == 0)
        def _():
            tail_ref[...] = jnp.zeros_like(tail_ref)

        @pl.when(i == 0)
        def _():
            loss_ref[...] = jnp.zeros_like(loss_ref)
            dgf_ref[...] = jnp.zeros_like(dgf_ref)

        x2v = x2_ref[...]
        h2 = (x2v * _rms(x2v) * g2_ref[...]).astype(BF16)
        h2_ref[...] = h2
        x3 = x2v
        for c in range(0, FH, ch):
            gated = []
            for cols in (slice(c, c + ch), slice(FH + c, FH + c + ch)):
                upp = _dot_nt(h2, wupt_ref[cols, :])
                upp_ref[:, cols] = upp.astype(BF16)
                p1, p2 = _taps_back(upp, tail_ref[6:7, cols], tail_ref[7:8, cols])
                tail_ref[:, cols] = upp[tm - 8:tm, :]
                up = (upp * wf_ref[2:3, cols] + p2 * wf_ref[0:1, cols]) + p1 * wf_ref[1:2, cols]
                up_ref[:, cols] = up.astype(BF16)
                gated.append(up)
            a, b = gated
            act = (a * _sigmoid(a) * b).astype(BF16)
            act_ref[:, c:c + ch] = act
            x3 = x3 + _dot(act, wdn_ref[c:c + ch, :])
        r3 = _rms(x3)
        xn3 = x3 * r3
        e = xn3 * gf_ref[...] - tgt_ref[...]
        loss_ref[...] += 0.5 * jnp.sum(jnp.mean(e * e, axis=-1, keepdims=True), axis=0, keepdims=True)
        dy = e / D
        dgf_ref[0:1, :] += jnp.sum(dy * xn3, axis=0, keepdims=True)
        dxn = dy * gf_ref[...]
        dx3_ref[...] = r3 * (dxn - xn3 * jnp.mean(dxn * xn3, axis=-1, keepdims=True))

    tok = lambda w: pl.BlockSpec((tm, w), lambda i: (i, 0))
    return pl.pallas_call(
        body, name="ffn_fwd", grid=(T // tm,),
        in_specs=[tok(D), _resident((1, D)), _resident((2 * FH, D)), _resident((8, 2 * FH)), _resident((FH, D)),
                  _resident((1, D)), tok(D)],
        out_specs=[tok(2 * FH), tok(2 * FH), tok(FH), tok(D), tok(D), pl.BlockSpec((8, LANES), lambda i: (0, 0)),
                   pl.BlockSpec((8, D), lambda i: (0, 0))],
        out_shape=[SDS((T, 2 * FH), BF16), SDS((T, 2 * FH), BF16), SDS((T, FH), BF16), SDS((T, D), BF16), SDS((T, D), F32),
                   SDS((8, LANES), F32), SDS((8, D), F32)],
        scratch_shapes=[pltpu.VMEM((8, 2 * FH), F32)],
        compiler_params=_cparams(1),
    )(x2, g2, wup_t, wf, wdn, gf, tgt)


def _ffn_bwd(dx3, x2, g2, upp, up, wf, wdn, wup_t, S, tm):
    T, D = x2.shape
    FH = wdn.shape[0]
    ch = _tile(FH, FFN_CHUNK)
    n, tps = T // tm, S // tm

    def body(dx3_ref, x2_ref, g2_ref, upp_ref, up_ref, wf_ref, wdn_ref, wupt_ref,
             dx2_ref, dupp_ref, dwf_ref, dg2_ref, head_ref):
        i = pl.program_id(0)
        t = n - 1 - i

        @pl.when(i == 0)
        def _():
            dwf_ref[...] = jnp.zeros_like(dwf_ref)
            dg2_ref[...] = jnp.zeros_like(dg2_ref)

        @pl.when(t % tps == tps - 1)
        def _():
            head_ref[...] = jnp.zeros_like(head_ref)

        dx3v = dx3_ref[...]
        dx3b = dx3v.astype(BF16)
        x2v = x2_ref[...]
        r2 = _rms(x2v)
        xn2 = x2v * r2
        dh2 = jnp.zeros((tm, D), F32)
        for c in range(0, FH, ch):
            ca, cb = slice(c, c + ch), slice(FH + c, FH + c + ch)
            a, b = up_ref[:, ca].astype(F32), up_ref[:, cb].astype(F32)
            sig = _sigmoid(a)
            sl = a * sig
            dact = _dot_nt(dx3b, wdn_ref[ca, :])
            grads = (dact * b * (sig * (1.0 + a * (1.0 - sig))), dact * sl)
            for cols, d in zip((ca, cb), grads):
                u0 = upp_ref[:, cols].astype(F32)
                d1, d2 = _taps_ahead(d, head_ref[0:1, cols], head_ref[1:2, cols])
                head_ref[:, cols] = d[0:8, :]
                dwf_ref[2:3, cols] += jnp.sum(u0 * d, axis=0, keepdims=True)
                dwf_ref[1:2, cols] += jnp.sum(u0 * d1, axis=0, keepdims=True)
                dwf_ref[0:1, cols] += jnp.sum(u0 * d2, axis=0, keepdims=True)
                dpre = ((d * wf_ref[2:3, cols] + d1 * wf_ref[1:2, cols]) + d2 * wf_ref[0:1, cols]).astype(BF16)
                dupp_ref[:, cols] = dpre
                dh2 = dh2 + _dot(dpre, wupt_ref[cols, :])
        dg2_ref[0:1, :] += jnp.sum(dh2 * xn2, axis=0, keepdims=True)
        dxn = dh2 * g2_ref[...]
        dx2_ref[...] = dx3v + r2 * (dxn - xn2 * jnp.mean(dxn * xn2, axis=-1, keepdims=True))

    tok = lambda w: pl.BlockSpec((tm, w), lambda i: (n - 1 - i, 0))
    acc = lambda w: pl.BlockSpec((8, w), lambda i: (0, 0))
    return pl.pallas_call(
        body, name="ffn_bwd", grid=(n,),
        in_specs=[tok(D), tok(D), _resident((1, D)), tok(2 * FH), tok(2 * FH), _resident((8, 2 * FH)),
                  _resident((FH, D)), _resident((2 * FH, D))],
        out_specs=[tok(D), tok(2 * FH), acc(2 * FH), acc(D)],
        out_shape=[SDS((T, D), F32), SDS((T, 2 * FH), BF16), SDS((8, 2 * FH), F32), SDS((8, D), F32)],
        scratch_shapes=[pltpu.VMEM((8, 2 * FH), F32)],
        compiler_params=_cparams(1),
    )(dx3, x2, g2, upp, up, wf, wdn, wup_t)


def _merge_bwd(dx2, pm, u, yc, ya, m, ycin, o, wc, bg, wo, woc_t, woa_t, S, tm, exchange):
    T, D = dx2.shape
    CW, AW = woc_t.shape[1], woa_t.shape[1]
    n, tps = T // tm, S // tm
    nx = len(exchange)
    rb = D // N_DEV
    assert CW + AW == D

    def body(dx2_ref, cb_ref, cc_ref, cin_ref, gc_ref, ga_ref, u_ref, yc_ref, ya_ref, m_ref, ycin_ref, o_ref,
             wc_ref, bg_ref, wo_ref, woct_ref, woat_ref, *rest):
        x_refs = rest[:nx]
        dconv_ref, dgl_ref, do_ref, dws_ref, dwc_ref, dbg_ref = rest[nx:nx + 6]
        got_refs, head_ref, sems = rest[nx + 6:2 * nx + 6], rest[2 * nx + 6], rest[2 * nx + 7:]

        def add_blocks(rows, cols, grad):
            for d in range(N_DEV):
                dws_ref[d, rows, cols] += grad[d * rb:(d + 1) * rb, :]

        i = pl.program_id(0)
        t = n - 1 - i

        @pl.when(i == 0)
        def _():
            for cp in _sibling_copies(x_refs, got_refs, *sems):
                cp.start()

        @pl.when(i == 0)
        def _():
            for ref in (dwc_ref, dbg_ref, dws_ref):
                ref[...] = jnp.zeros_like(ref)

        @pl.when(t % tps == tps - 1)
        def _():
            head_ref[...] = jnp.zeros_like(head_ref)

        dx2b = dx2_ref[...].astype(BF16)
        add_blocks(slice(0, rb), slice(0, D), _dot_tn(m_ref[...], dx2b))
        dm = _dot_nt(dx2b, wo_ref[...])
        outs = []
        for g_ref, y_ref, cols in ((gc_ref, yc_ref, slice(0, D)), (ga_ref, ya_ref, slice(D, 2 * D))):
            g = _sigmoid(g_ref[...].astype(F32) + bg_ref[:, cols])
            dgl = dm * y_ref[...].astype(F32) * g * (1.0 - g)
            dgl_ref[:, cols] = dgl.astype(BF16)
            dbg_ref[0:1, cols] += jnp.sum(dgl, axis=0, keepdims=True)
            outs.append((dm * g).astype(BF16))
        dyc, dya = outs
        add_blocks(slice(rb, 2 * rb), slice(0, CW), _dot_tn(dyc, ycin_ref[...]))
        add_blocks(slice(rb, 2 * rb), slice(CW, CW + AW), _dot_tn(dya, o_ref[...]))
        do_ref[...] = _dot(dya, woat_ref[...]).astype(BF16)
        dycin = _dot(dyc, woct_ref[...])
        cc, cin = cc_ref[...].astype(F32), cin_ref[...].astype(F32)
        z = cc * cin
        du = dycin * cb_ref[...].astype(F32)
        du1, du2 = _taps_ahead(du, head_ref[0:1, :], head_ref[1:2, :])
        head_ref[...] = du[0:8, :]
        dwc_ref[2:3, :] += jnp.sum(z * du, axis=0, keepdims=True)
        dwc_ref[1:2, :] += jnp.sum(z * du1, axis=0, keepdims=True)
        dwc_ref[0:1, :] += jnp.sum(z * du2, axis=0, keepdims=True)
        dz = (du * wc_ref[2:3, :] + du1 * wc_ref[1:2, :]) + du2 * wc_ref[0:1, :]
        dconv_ref[:, 0:CW] = (dycin * u_ref[...].astype(F32)).astype(BF16)
        dconv_ref[:, CW:2 * CW] = (dz * cin).astype(BF16)
        dconv_ref[:, 2 * CW:3 * CW] = (dz * cc).astype(BF16)

        @pl.when(i == n - 1)
        def _():
            for cp in _sibling_copies(x_refs, got_refs, *sems):
                cp.wait()

    g_off = (3 * CW + 3 * AW) // D
    tok = lambda w, j=0: pl.BlockSpec((tm, w), lambda i: (n - 1 - i, j))
    acc = lambda w: pl.BlockSpec((8, w), lambda i: (0, 0))
    return pl.pallas_call(
        body, name="merge_bwd", grid=(n,),
        in_specs=[tok(D), tok(CW, 0), tok(CW, 1), tok(CW, 2), tok(D, g_off), tok(D, g_off + 1), tok(CW), tok(D), tok(D),
                  tok(D), tok(CW), tok(AW),
                  _resident((8, CW)), _resident((1, 2 * D)), _resident((D, D)), _resident((D, CW)), _resident((D, AW))]
                 + [ANY] * nx,
        out_specs=[tok(3 * CW), tok(2 * D), tok(AW), pl.BlockSpec((N_DEV, 2 * rb, D), lambda i: (0, 0, 0)), acc(CW),
                   acc(2 * D)] + [ANY] * nx,
        out_shape=[SDS((T, 3 * CW), BF16), SDS((T, 2 * D), BF16), SDS((T, AW), BF16), SDS((N_DEV, 2 * rb, D), F32),
                   SDS((8, CW), F32), SDS((8, 2 * D), F32)] + _sibling_shapes(exchange),
        scratch_shapes=[pltpu.VMEM((8, CW), F32)] + _exchange_sems(nx),
        compiler_params=_cparams(1),
    )(dx2, pm, pm, pm, pm, pm, u, yc, ya, m, ycin, o, wc, bg, wo, woc_t, woa_t, *exchange)


def _attn_bwd(pm, do, fcum, frow, lse, S, q_off, AW, hd, tq, exchange):
    T = pm.shape[0]
    scale = float(hd) ** -0.5
    nq, hpg, ng = S // tq, LANES // hd, AW // LANES
    nx = len(exchange)
    steps = (T // S) * ng

    def body(q_ref, k_ref, v_ref, do_ref, fc_ref, fr_ref, lse_ref, *rest):
        x_refs, (dq_ref, dk_ref, dv_ref, dfr_ref) = rest[:nx], rest[nx:nx + 4]
        got_refs, (dk_acc, dv_acc), sems = rest[nx + 4:2 * nx + 4], rest[2 * nx + 4:2 * nx + 6], rest[2 * nx + 6:]
        g = pl.program_id(1)
        step = pl.program_id(0) * ng + g

        @pl.when(step == 0)
        def _():
            for cp in _chip_copies(x_refs, got_refs, *sems):
                cp.start()

        dk_acc[...] = jnp.zeros_like(dk_acc)
        dv_acc[...] = jnp.zeros_like(dv_acc)
        dfr_ref[...] = jnp.zeros_like(dfr_ref)
        for i in range(nq):
            rs, kend = slice(i * tq, (i + 1) * tq), (i + 1) * tq
            row = i * tq + lax.broadcasted_iota(jnp.int32, (tq, kend), 0)
            col = lax.broadcasted_iota(jnp.int32, (tq, kend), 1)
            kk, vv = k_ref[0:kend, :], v_ref[0:kend, :]
            dq_tile = jnp.zeros((tq, LANES), F32)
            for e in range(hpg):
                mask = _head_mask(e, hd)
                qs = jnp.where(mask, q_ref[rs, :], 0) * scale
                doi = jnp.where(mask, do_ref[rs, :], 0)
                s = _dot_nt(qs, kk) + _lane_pick(fc_ref[rs, :], g * hpg + e) - fr_ref[e:e + 1, 0:kend]
                p = jnp.where(col <= row, jnp.exp(s - _lane_pick(lse_ref[rs, :], g * hpg + e)), 0.0)
                dp = _dot_nt(doi, vv)
                ds = p * (dp - jnp.sum(p * dp, axis=1, keepdims=True))
                pb, dsb = p.astype(BF16), ds.astype(BF16)
                dq_tile = jnp.where(mask, _dot(dsb, kk) * scale, dq_tile)
                dv_acc[0:kend, :] += _dot_tn(pb, doi)
                dk_acc[0:kend, :] += _dot_tn(dsb, qs)
                dfr_ref[e:e + 1, 0:kend] -= jnp.sum(ds, axis=0, keepdims=True)
            dq_ref[rs, :] = dq_tile.astype(BF16)
        dk_ref[...] = dk_acc[...].astype(BF16)
        dv_ref[...] = dv_acc[...].astype(BF16)

        @pl.when(step == steps - 1)
        def _():
            for cp in _chip_copies(x_refs, got_refs, *sems):
                cp.wait()

    full = pl.BlockSpec((S, LANES), lambda b, g: (b, 0))
    grp = pl.BlockSpec((S, LANES), lambda b, g: (b, g))
    rows = pl.BlockSpec((None, None, 8, S), lambda b, g: (b, g, 0, 0))
    return pl.pallas_call(
        body, name="attn_bwd", grid=(T // S, ng),
        in_specs=_attn_specs(S, q_off, AW) + [grp, full, rows, full] + [ANY] * nx,
        out_specs=[grp, grp, grp, rows] + [ANY] * nx,
        out_shape=[SDS((T, AW), BF16), SDS((T, AW), BF16), SDS((T, AW), BF16), SDS((T // S, ng, 8, S), F32)]
                  + _chip_shapes(exchange),
        scratch_shapes=[pltpu.VMEM((S, LANES), F32), pltpu.VMEM((S, LANES), F32)] + _exchange_sems(3 * nx),
        compiler_params=_cparams(2),
    )(pm, pm, pm, do, fcum, frow, lse, *exchange)


def _forget_bwd(dfc, fl, bf, S):
    T, NF = fl.shape
    ch = _tile(S, 256, 8)

    def body(df_ref, fl_ref, bf_ref, dfl_ref, dbf_ref):
        @pl.when(pl.program_id(0) == 0)
        def _():
            dbf_ref[...] = jnp.zeros_like(dbf_ref)

        row = lax.broadcasted_iota(jnp.int32, (ch, ch), 0)
        col = lax.broadcasted_iota(jnp.int32, (ch, ch), 1)
        tri = (col >= row).astype(BF16)
        carry = jnp.zeros((1, NF), F32)
        for c in range(S - ch, -1, -ch):
            d = df_ref[c:c + ch, :]
            dlf = _tri_dot(tri, d) + carry
            carry = carry + jnp.sum(d, axis=0, keepdims=True)
            dfl = dlf * _sigmoid(-(fl_ref[c:c + ch, :] + bf_ref[...]))
            dfl_ref[c:c + ch, :] = dfl.astype(BF16)
            dbf_ref[0:1, :] += jnp.sum(dfl, axis=0, keepdims=True)

    return pl.pallas_call(
        body, name="forget_bwd", grid=(T // S,),
        in_specs=[pl.BlockSpec((S, NF), lambda b: (b, 0)), pl.BlockSpec((S, NF), lambda b: (b, 0)), _resident((1, NF))],
        out_specs=[pl.BlockSpec((S, NF), lambda b: (b, 0)), pl.BlockSpec((8, NF), lambda b: (0, 0))],
        out_shape=[SDS((T, NF), BF16), SDS((8, NF), F32)],
        compiler_params=_cparams(1),
    )(dfc, fl, bf)


def _inproj_bwd(dparts, offs, w_t, x, g1, dx2, tm, exchange):
    T, D = x.shape
    npart, nx = len(dparts), len(exchange)
    n = T // tm

    def body(*refs):
        d_refs = refs[:npart]
        w_ref, x_ref, g_ref, dx2_ref = refs[npart:npart + 4]
        x_refs = refs[npart + 4:npart + 4 + nx]
        dx_ref, dg_ref = refs[npart + 4 + nx:npart + 6 + nx]
        got_refs, sems = refs[npart + 6 + nx:npart + 6 + 2 * nx], refs[npart + 6 + 2 * nx:]

        @pl.when(pl.program_id(0) == 0)
        def _():
            for cp in _chip_copies(x_refs, got_refs, *sems):
                cp.start()

        @pl.when(pl.program_id(0) == 0)
        def _():
            dg_ref[...] = jnp.zeros_like(dg_ref)

        dh = None
        for d_ref, off in zip(d_refs, offs):
            term = _dot(d_ref[...], w_ref[off:off + d_ref.shape[1], :])
            dh = term if dh is None else dh + term
        xv = x_ref[...]
        r = _rms(xv)
        xn = xv * r
        dg_ref[0:1, :] += jnp.sum(dh * xn, axis=0, keepdims=True)
        dxn = dh * g_ref[...]
        dx_ref[...] = dx2_ref[...] + r * (dxn - xn * jnp.mean(dxn * xn, axis=-1, keepdims=True))

        @pl.when(pl.program_id(0) == n - 1)
        def _():
            for cp in _chip_copies(x_refs, got_refs, *sems):
                cp.wait()

    tok = lambda w: pl.BlockSpec((tm, w), lambda i: (i, 0))
    return pl.pallas_call(
        body, name="inproj_bwd", grid=(n,),
        in_specs=[tok(d.shape[1]) for d in dparts] + [_resident(w_t.shape), tok(D), _resident((1, D)), tok(D)] + [ANY] * nx,
        out_specs=[tok(D), pl.BlockSpec((8, D), lambda i: (0, 0))] + [ANY] * nx,
        out_shape=[SDS((T, D), F32), SDS((8, D), F32)] + _chip_shapes(exchange),
        scratch_shapes=_exchange_sems(3 * nx),
        compiler_params=_cparams(1),
    )(*dparts, w_t, x, g1, dx2, *exchange)


def _wgrad(bs, a, name, into=None, row_off=0, total_rows=None):
    bs = list(bs) if isinstance(bs, (list, tuple)) else [bs]
    P = len(bs)
    T, N = bs[0].shape
    M = a.shape[1]
    tn = _tile(N, 1408 if M <= 1024 else 512)
    while row_off % tn:
        tn = _tile(N, tn - LANES)
    tk = _tile(T, 512, 16)
    per_row = 2 * (P * tn * bs[0].dtype.itemsize + M * a.dtype.itemsize)
    while T % (2 * tk) == 0 and 2 * tk * per_row + 2 * tn * M * 4 <= WGRAD_VMEM_BUDGET:
        tk *= 2
    blk0, nj, nk = row_off // tn, N // tn, T // tk

    def body(*refs):
        b_refs, a_ref, o_ref = refs[:P], refs[P], refs[-1]
        j = pl.program_id(0)

        @pl.when(pl.program_id(1) == 0)
        def _():
            o_ref[...] = jnp.zeros_like(o_ref)

        for p, b_ref in enumerate(b_refs):
            @pl.when((j >= p * nj) & (j < (p + 1) * nj))
            def _(b_ref=b_ref):
                o_ref[...] += _dot_tn(b_ref[...].astype(BF16), a_ref[...].astype(BF16))

    def b_spec(p):
        def index(j, k):
            before, mine = j < p * nj, (j >= p * nj) & (j < (p + 1) * nj)
            return (jnp.where(mine, k, jnp.where(before, 0, nk - 1)),
                    jnp.where(mine, j - p * nj, jnp.where(before, 0, nj - 1)))
        return pl.BlockSpec((tk, tn), index)

    in_specs = [b_spec(p) for p in range(P)] + [pl.BlockSpec((tk, M), lambda j, k: (k, 0))]
    args = (*bs, a)
    kwargs = {}
    if into is not None:
        in_specs.append(ANY)
        args += (into,)
        kwargs["input_output_aliases"] = {P + 1: 0}
        total_rows = into.shape[0]
    return pl.pallas_call(
        body, name=name, grid=(P * nj, nk),
        in_specs=in_specs,
        out_specs=pl.BlockSpec((tn, M), lambda j, k: (blk0 + j, 0)),
        out_shape=SDS((P * N if total_rows is None else total_rows, M), F32),
        compiler_params=_cparams(2), **kwargs,
    )(*args)


def _adamw(w, g, m, v, name):
    shape = w.shape
    C = shape[-1]
    w2, g2, m2, v2 = (a.reshape(-1, C) for a in (w, g, m, v))
    R = w2.shape[0]
    tr = R if R <= 512 else _tile(R, 256, 8)
    if tr < 64:
        tr = R

    def body(w_ref, g_ref, m_ref, v_ref, d_ref, nm_ref, nv_ref):
        gv = g_ref[...]
        mv = ADAM_B1 * m_ref[...] + (1.0 - ADAM_B1) * gv
        vv = ADAM_B2 * v_ref[...] + (1.0 - ADAM_B2) * (gv * gv)
        m_hat = mv / (1.0 - ADAM_B1 ** ADAM_STEP)
        v_hat = vv / (1.0 - ADAM_B2 ** ADAM_STEP)
        d_ref[...] = -ADAM_LR * (m_hat / (jnp.sqrt(v_hat) + ADAM_EPS) + ADAM_WD * w_ref[...])
        nm_ref[...] = mv
        nv_ref[...] = vv

    spec = pl.BlockSpec((tr, C), lambda r: (r, 0))
    outs = pl.pallas_call(
        body, name=name, grid=(R // tr,),
        in_specs=[spec] * 4, out_specs=[spec] * 3, out_shape=[SDS((R, C), F32)] * 3,
        compiler_params=_cparams(1),
    )(w2, g2, m2, v2)
    return tuple(o.reshape(shape) for o in outs)


def _rows(a, L):
    lead = a.shape[0]
    flat = a.reshape(lead, -1)
    n = flat.shape[1]
    r = -(-n // L)
    return jnp.pad(flat, ((0, 0), (0, r * L - n))).reshape(lead, r, L)


def _unrows(p, shape):
    return p.reshape(-1)[:int(np.prod(shape))].reshape(shape)


def _from_col_blocks(a):
    n, R, c = a.shape
    return a.transpose(1, 0, 2).reshape(R, n * c)


def _in_windows(n_loc, nqkv, H, NM):
    win = (n_loc + 7 + 7) // 8 * 8
    starts, index = [], np.zeros((N_DEV, n_loc), np.int32)
    for d in range(N_DEV):
        rows = np.arange(n_loc * d, n_loc * (d + 1))
        is_f = (rows >= nqkv) & (rows < nqkv + H)
        kept = np.where(rows < nqkv, rows, rows - H)
        lo = int(kept[~is_f].min())
        start = lo // 8 * 8
        assert int(kept[~is_f].max()) - start < win and start + win <= NM + LANES
        starts.append(start)
        index[d] = np.where(is_f, win + rows - nqkv, kept - start)
    return starts, win, index


def kernel(x, norm_mix_g, w_in, b_f, b_gate, conv_mix_w, w_out_conv, w_out_attn, w_o, norm_ffn_g, w_up, conv_ffn_w, w_down, norm_f_g, loss_target, m_norm_mix_g, m_w_in, m_b_f, m_b_gate, m_conv_mix_w, m_w_out_conv, m_w_out_attn, m_w_o, m_norm_ffn_g, m_w_up, m_conv_ffn_w, m_w_down, m_norm_f_g, v_norm_mix_g, v_w_in, v_b_f, v_b_gate, v_conv_mix_w, v_w_out_conv, v_w_out_attn, v_w_o, v_norm_ffn_g, v_w_up, v_conv_ffn_w, v_w_down, v_norm_f_g):
    Bl, S, D = x.shape
    T = Bl * S
    H = b_f.shape[-1]
    CW = N_DEV * conv_mix_w.shape[-1]
    AW = w_out_attn.shape[1]
    hd = AW // H
    FH = N_DEV * w_down.shape[1]
    n_loc = w_in.shape[-1]
    NIN = N_DEV * n_loc
    NM = 3 * CW + 3 * AW + 2 * D
    nqkv = 3 * CW + 3 * AW
    assert NIN == NM + H and w_out_conv.shape[1] == CW and nqkv % D == 0 and AW % LANES == 0 and LANES % hd == 0
    hpg, ng = LANES // hd, AW // LANES
    assert hpg <= 8
    tm_big = min(512, S // 2)
    tm_ffn = min(256, S // 2)
    tq_fwd = min(512, S // 2)
    tq_bwd = min(256, S // 2)
    px, py, pc = _place()
    me = 4 * px + 2 * py + pc

    bits = lambda a: lax.bitcast_convert_type(a, BF16)
    taps = jnp.concatenate([_rows(bits(conv_ffn_w[0])[None], D)[0], _rows(bits(conv_mix_w[0])[None], D)[0]], axis=0)
    n_ffn_rows = -(-conv_ffn_w[0].size * 2 // D)
    late = [w_up[0].T.astype(BF16), w_down[0].astype(BF16), w_o[0].astype(BF16), w_out_conv[0].T.astype(BF16),
            w_out_attn[0].T.astype(BF16), taps]
    g_in, = _all_gather([w_in[0].T.astype(BF16)], "weights_all_gather")
    W_in_rows = g_in.reshape(NIN, D)
    W_in_t = jnp.concatenate([W_in_rows[:nqkv], W_in_rows[nqkv + H:], W_in_rows[nqkv:nqkv + H],
                              jnp.zeros((LANES - H, D), BF16)], axis=0)
    bf128 = jnp.pad(b_f, ((0, 0), (0, LANES - H)))

    x2d = x.reshape(T, D)
    tgt = loss_target.reshape(T, D)
    pm, fl, h1 = _inproj_fwd(x2d, norm_mix_g, W_in_t, NM, tm_big)
    fcum = _forget_cumsum(fl, bf128, S)
    frow = jnp.pad(fcum[:, :H].reshape(Bl, S, ng, hpg).transpose(0, 2, 3, 1), ((0, 0), (0, 0), (0, 8 - hpg), (0, 0)))
    q_off = 3 * CW // LANES
    o, lse, g_up, g_dn, g_o, g_oc, g_oa, g_taps = _attn_fwd(pm, fcum, frow, S, q_off, AW, hd, tq_fwd, late)
    W_up_t = g_up.reshape(2 * FH, D)
    W_dn = g_dn.reshape(FH, D)
    W_o = g_o.reshape(D, D)
    W_oc_t = g_oc.reshape(D, CW)
    W_oa_t = g_oa.reshape(D, AW)
    tap_bits = g_taps.reshape(N_DEV, -1)
    n_ffn, n_mix = conv_ffn_w[0].size * 2, conv_mix_w[0].size * 2
    wf_full = _from_col_blocks(lax.bitcast_convert_type(
        tap_bits[:, :n_ffn].reshape((N_DEV,) + conv_ffn_w.shape[1:] + (2,)), F32))
    wc_full = _from_col_blocks(lax.bitcast_convert_type(
        tap_bits[:, n_ffn_rows * D:n_ffn_rows * D + n_mix].reshape((N_DEV,) + conv_mix_w.shape[1:] + (2,)), F32))
    wf8 = jnp.pad(wf_full, ((0, 5), (0, 0)))
    wc8 = jnp.pad(wc_full, ((0, 5), (0, 0)))
    x2, u, m, ycin, yc, ya = _merge_fwd(pm, o, x2d, wc8, b_gate, W_oc_t, W_oa_t, W_o, S, tm_big)
    upp, up, act, h2, dx3, loss8, dgf8 = _ffn_fwd(x2, norm_ffn_g, W_up_t, wf8, W_dn, norm_f_g.reshape(1, D), tgt, S, tm_ffn)

    dx2, dupp, dwf8, dg2_8 = _ffn_bwd(dx3, x2, norm_ffn_g, upp, up, wf8, W_dn, W_up_t, S, tm_ffn)
    dW_dn = _wgrad(act, dx3, "wgrad_down")
    dW_up_t = _wgrad(dupp, h2, "wgrad_up")
    ids = jnp.stack([pc, 2 * px + py]).astype(jnp.int32)
    big = [dW_up_t.reshape(4, 2, -1, D), dW_dn.reshape(4, 2, -1, D)]
    dconv, dgl, do, dW_small, dwc8, dbg8, *sib_big = _merge_bwd(
        dx2, pm, u, yc, ya, m, ycin, o, wc8, b_gate, W_o, W_oc_t, W_oa_t, S, tm_big, big)
    big_sums = [_pair_sum(b, r, ids, "grads_pair_sum_%d" % a) for a, (b, r) in enumerate(zip(big, sib_big))]
    dq, dk, dv, dfr, *chips_big = _attn_bwd(pm, do, fcum, frow, lse, S, q_off, AW, hd, tq_bwd, [s[1] for s in big_sums])
    dfc = jnp.pad(dfr[:, :, :hpg, :].transpose(0, 3, 1, 2).reshape(T, H), ((0, 0), (0, LANES - H)))
    dfl, dbf8 = _forget_bwd(dfc, fl, bf128, S)
    dparts = [dconv, dq, dk, dv, dgl, dfl]
    offs = [0, 3 * CW, 3 * CW + AW, 3 * CW + 2 * AW, nqkv, NM]
    dW_in_t = _wgrad(dconv, h1, "wgrad_in_conv", total_rows=NM + LANES)
    dW_in_t = _wgrad([dq, dk, dv], h1, "wgrad_in_qkv", into=dW_in_t, row_off=3 * CW)
    dW_in_t = _wgrad(dgl, h1, "wgrad_in_gates", into=dW_in_t, row_off=nqkv)
    dW_in_t = _wgrad(dfl, h1, "wgrad_in_forget", into=dW_in_t, row_off=NM)

    starts, win, index = _in_windows(n_loc, nqkv, H, NM)
    small = dW_small.reshape(4, 2, -1, D)
    my_starts = [jnp.where(pc == 0, starts[2 * j], starts[2 * j + 1]) for j in range(4)]
    win_ids = jnp.stack([pc, 2 * px + py] + my_starts).astype(jnp.int32)
    sib_win, sib_small = _sibling_exchange(dW_in_t, starts, win, [small], "grads_sibling_exchange")
    sums = [_pair_sum(dW_in_t, sib_win, win_ids, "grads_pair_sum_in", win_rows=win),
            _pair_sum(small, sib_small, ids, "grads_pair_sum_small")]
    grad_x, dg1_8, *from_chips = _inproj_bwd(dparts, offs, W_in_t, x2d, norm_mix_g, dx2, tm_big, [s[1] for s in sums])
    red_win, red_small, red_up, red_dn = [
        _final_sum(s[0], r, "grads_final_sum_%d" % a)
        for a, (s, r) in enumerate(zip(sums + big_sums, list(from_chips) + list(chips_big)))]

    f_rows = dW_in_t[NM:NM + 8]
    wf_rows = _rows(dwf8[:3].reshape(1, -1), D)[0]
    wc_rows = _rows(dwc8[:3].reshape(1, -1), D)[0]
    smalls = [dg1_8[0:1], dg2_8[0:1], dgf8[0:1], dbg8[0:1, :D], dbg8[0:1, D:], jnp.pad(dbf8[0:1], ((0, 0), (0, D - LANES))),
              jnp.pad(loss8[0:1], ((0, 0), (0, D - LANES))), jnp.zeros((1, D), F32), f_rows, wf_rows, wc_rows]
    spack = jnp.concatenate(smalls, axis=0)
    spack = jnp.pad(spack, ((0, -spack.shape[0] % 8), (0, 0)))
    ssum = _sum8(_all_gather_direct(spack, "small_all_gather"), "small_sum")
    g_g1, g_g2, g_gf = ssum[0:1], ssum[1:2], ssum[2]
    g_bg = jnp.concatenate([ssum[3:4], ssum[4:5]], axis=1)
    g_bf = ssum[5:6, :H]
    r0 = 16
    r1 = r0 + wf_rows.shape[0]
    wf_sum = _unrows(ssum[r0:r1], (3, 2 * FH))
    wc_sum = _unrows(ssum[r1:r1 + wc_rows.shape[0]], (3, CW))
    g_wf = lax.dynamic_slice_in_dim(wf_sum, me * conv_ffn_w.shape[-1], conv_ffn_w.shape[-1], axis=1)
    g_wc = lax.dynamic_slice_in_dim(wc_sum, me * conv_mix_w.shape[-1], conv_mix_w.shape[-1], axis=1)

    ext = jnp.concatenate([red_win, ssum[8:8 + H]], axis=0)
    my_index = lax.dynamic_index_in_dim(jnp.asarray(index), me, axis=0, keepdims=False)
    g_w_in_t = jnp.take(ext, my_index, axis=0)
    rb = D // N_DEV
    g_w_o = red_small[:rb]
    g_w_oc = red_small[rb:, :CW].T
    g_w_oa = red_small[rb:, CW:].T

    loss = ssum[6, 0]

    names = ["norm_mix_g", "w_in", "b_f", "b_gate", "conv_mix_w", "w_out_conv", "w_out_attn", "w_o", "norm_ffn_g", "w_up",
             "conv_ffn_w", "w_down", "norm_f_g"]
    weights = [norm_mix_g, w_in, b_f, b_gate, conv_mix_w, w_out_conv, w_out_attn, w_o, norm_ffn_g, w_up, conv_ffn_w, w_down, norm_f_g]
    grads = [g_g1, g_w_in_t, g_bf, g_bg, g_wc, g_w_oc, g_w_oa, g_w_o, g_g2, red_up, g_wf, red_dn, g_gf]
    ms = [m_norm_mix_g, m_w_in, m_b_f, m_b_gate, m_conv_mix_w, m_w_out_conv, m_w_out_attn, m_w_o, m_norm_ffn_g, m_w_up,
          m_conv_ffn_w, m_w_down, m_norm_f_g]
    vs = [v_norm_mix_g, v_w_in, v_b_f, v_b_gate, v_conv_mix_w, v_w_out_conv, v_w_out_attn, v_w_o, v_norm_ffn_g, v_w_up,
          v_conv_ffn_w, v_w_down, v_norm_f_g]
    to_view = {"w_in": lambda a: a[0].T.reshape(-1, LANES), "w_up": lambda a: a[0].T}
    from_view = {"w_in": lambda a: a.reshape(n_loc, D).T[None], "w_up": lambda a: a.T[None]}
    out_grads, steps = [], []
    for nm, w, g, mm, vv in zip(names, weights, grads, ms, vs):
        if nm in to_view:
            wv, mv, vw = (to_view[nm](a) for a in (w, mm, vv))
            gv = g.reshape(wv.shape)
            steps.append(tuple(from_view[nm](o) for o in _adamw(wv, gv, mv, vw, "adamw_" + nm)))
            out_grads.append(from_view[nm](gv))
        else:
            gv = g.reshape(w.shape)
            steps.append(_adamw(w, gv, mm, vv, "adamw_" + nm))
            out_grads.append(gv)
    deltas, new_ms, new_vs = zip(*steps)
    return (loss, grad_x.reshape(Bl, S, D), *out_grads, *deltas, *new_ms, *new_vs)
```

```python
import numpy as np

import jax
import jax.numpy as jnp
from jax import lax
from jax.experimental import pallas as pl
from jax.experimental.pallas import tpu as pltpu

F32, BF16 = jnp.float32, jnp.bfloat16
EPS = 1e-6
ADAM_LR, ADAM_B1, ADAM_B2, ADAM_EPS, ADAM_WD, ADAM_STEP = 0.001, 0.9, 0.999, 1e-08, 0.01, 10
N_DEV = 8
LANES = 128
V7X_VMEM_LIMIT = 56 * 1024 * 1024
FFN_CHUNK = 2816
WGRAD_VMEM_BUDGET = 40 * 1024 * 1024
MESH = pl.DeviceIdType.MESH
SDS = jax.ShapeDtypeStruct
ANY = pl.BlockSpec(memory_space=pl.ANY)


def _tile(n, target, mult=LANES):
    best = None
    for t in range(mult, min(n, target) + 1, mult):
        if n % t == 0:
            best = t
    return best if best is not None else n


def _resident(shape):
    return pl.BlockSpec(shape, lambda *_: (0,) * len(shape), pipeline_mode=pl.Buffered(1))


def _cparams(n_axes=1):
    return pltpu.CompilerParams(dimension_semantics=("arbitrary",) * n_axes, vmem_limit_bytes=V7X_VMEM_LIMIT)


def _dot(a, b):
    return jnp.dot(a, b, preferred_element_type=F32)


def _dot_tn(a, b):
    return lax.dot_general(a, b, (((0,), (0,)), ((), ())), preferred_element_type=F32)


def _dot_nt(a, b):
    return lax.dot_general(a, b, (((1,), (1,)), ((), ())), preferred_element_type=F32)


def _sigmoid(x):
    return 0.5 * jnp.tanh(0.5 * x) + 0.5


def _rms(x):
    return lax.rsqrt(jnp.mean(x * x, axis=-1, keepdims=True) + EPS)


def _taps_back(z, r6, r7):
    row = lax.broadcasted_iota(jnp.int32, (8, 1), 0)
    z1, z2 = pltpu.roll(z, 1, 0), pltpu.roll(z, 2, 0)
    z1 = jnp.concatenate([jnp.where(row == 0, r7, z1[0:8]), z1[8:]], axis=0)
    z2 = jnp.concatenate([jnp.where(row == 0, r6, jnp.where(row == 1, r7, z2[0:8])), z2[8:]], axis=0)
    return z1, z2


def _taps_ahead(d, h0, h1):
    tm = d.shape[0]
    row = lax.broadcasted_iota(jnp.int32, (8, 1), 0)
    d1, d2 = pltpu.roll(d, tm - 1, 0), pltpu.roll(d, tm - 2, 0)
    d1 = jnp.concatenate([d1[:tm - 8], jnp.where(row == 7, h0, d1[tm - 8:])], axis=0)
    d2 = jnp.concatenate([d2[:tm - 8], jnp.where(row == 6, h0, jnp.where(row == 7, h1, d2[tm - 8:]))], axis=0)
    return d1, d2


def _tri_dot(tri, x):
    hi = x.astype(BF16)
    r = x - hi.astype(F32)
    mid = r.astype(BF16)
    lo = (r - mid.astype(F32)).astype(BF16)
    return (_dot(tri, lo) + _dot(tri, mid)) + _dot(tri, hi)


def _lane_pick(block, lane):
    lanes = lax.broadcasted_iota(jnp.int32, (1, block.shape[1]), 1)
    return jnp.sum(jnp.where(lanes == lane, block, 0.0), axis=1, keepdims=True)


def _place():
    return lax.axis_index("x"), lax.axis_index("y"), lax.axis_index("c")


def _all_gather(xs, name):
    n = len(xs)

    def body(*refs):
        start, forward, finish = _gather_phases(refs[:n], refs[n:2 * n], *refs[2 * n:])
        start()
        forward()
        finish()

    return pl.pallas_call(
        body, name=name,
        out_shape=_gather_shapes(xs), in_specs=[ANY] * n, out_specs=[ANY] * n, scratch_shapes=_gather_sems(n),
    )(*xs)


def _gather_shapes(xs):
    return [SDS((N_DEV,) + x.shape, x.dtype) for x in xs]


def _gather_sems(n):
    return [pltpu.SemaphoreType.DMA((7 * n,)), pltpu.SemaphoreType.DMA((7 * n,)), pltpu.SemaphoreType.DMA((n,))]


def _gather_phases(x_refs, out_refs, send_sems, recv_sems, local_sems):
    n = len(x_refs)

    def parts():
        px, py, pc = _place()
        me, sibling = (px, py, pc), (px, py, 1 - pc)
        chips = [(1 - px, py), (px, 1 - py), (1 - px, 1 - py)]

        def slot(a, qx, qy, qc):
            return out_refs[a].at[4 * qx + 2 * qy + qc]

        def copy(a, k, block, to, src=None):
            return pltpu.make_async_remote_copy(
                src_ref=slot(a, *block) if src is None else src, dst_ref=slot(a, *block),
                send_sem=send_sems.at[7 * a + k], recv_sem=recv_sems.at[7 * a + k], device_id=to, device_id_type=MESH)

        def mine():
            return [pltpu.make_async_copy(x_refs[a], slot(a, *me), local_sems.at[a]) for a in range(n)]

        def first():
            out = []
            for a in range(n):
                out.append(copy(a, 0, me, sibling, src=x_refs[a]))
                out += [copy(a, 1 + j, me, (*chip, pc), src=x_refs[a]) for j, chip in enumerate(chips)]
            return out

        def landed():
            return [copy(a, 1 + j, (*chip, pc), me) for j, chip in enumerate(chips) for a in range(n)]

        def passed():
            return [copy(a, 4 + j, (*chip, pc), sibling) for j, chip in enumerate(chips) for a in range(n)]

        def late():
            out = [copy(a, 0, sibling, me) for a in range(n)]
            return out + [copy(a, 4 + j, (*chip, 1 - pc), me) for j, chip in enumerate(chips) for a in range(n)]

        return mine, first, landed, passed, late

    def start():
        mine, first, _, _, _ = parts()
        for cp in mine() + first():
            cp.start()

    def forward():
        _, _, landed, passed, _ = parts()
        for got, cp in zip(landed(), passed()):
            got.wait_recv()
            cp.start()

    def finish():
        mine, first, _, passed, late = parts()
        for cp in late():
            cp.wait_recv()
        for cp in first() + passed():
            cp.wait_send()
        for cp in mine():
            cp.wait()

    return start, forward, finish


def _sibling_exchange(win_buf, win_starts, win_rows, blocked, name):
    nb = len(blocked)

    def body(*refs):
        win_ref, blk_refs = refs[0], refs[1:1 + nb]
        rwin_ref, rblk_refs = refs[1 + nb], refs[2 + nb:2 + 2 * nb]
        send_sems, recv_sems, wsend_sems, wrecv_sems = refs[2 + 2 * nb:]
        px, py, pc = _place()
        copies = _sibling_copies(blk_refs, rblk_refs, send_sems, recv_sems)
        for j in range(4):
            theirs = jnp.where(pc == 0, win_starts[2 * j + 1], win_starts[2 * j])
            copies.append(pltpu.make_async_remote_copy(
                src_ref=win_ref.at[pl.ds(pl.multiple_of(theirs, 8), win_rows)], dst_ref=rwin_ref.at[j],
                send_sem=wsend_sems.at[j], recv_sem=wrecv_sems.at[j], device_id=(px, py, 1 - pc), device_id_type=MESH))
        for cp in copies:
            cp.start()
        for cp in copies:
            cp.wait()

    C = win_buf.shape[1]
    return pl.pallas_call(
        body, name=name,
        out_shape=[SDS((4, win_rows, C), F32)] + _sibling_shapes(blocked),
        in_specs=[ANY] * (1 + nb), out_specs=[ANY] * (1 + nb),
        scratch_shapes=_exchange_sems(nb) + _exchange_sems(4),
    )(win_buf, *blocked)


def _sibling_shapes(blocked):
    return [SDS((4,) + b.shape[2:], F32) for b in blocked]


def _exchange_sems(n):
    return [pltpu.SemaphoreType.DMA((n,)), pltpu.SemaphoreType.DMA((n,))]


def _sibling_copies(blk_refs, out_refs, send_sems, recv_sems):
    px, py, pc = _place()
    return [pltpu.make_async_remote_copy(
        src_ref=b.at[:, 1 - pc], dst_ref=o, send_sem=send_sems.at[a], recv_sem=recv_sems.at[a],
        device_id=(px, py, 1 - pc), device_id_type=MESH) for a, (b, o) in enumerate(zip(blk_refs, out_refs))]


def _chip_shapes(ps):
    return [SDS((3,) + p.shape[1:], p.dtype) for p in ps]


def _chip_copies(p_refs, out_refs, send_sems, recv_sems):
    px, py, pc = _place()
    n = len(p_refs)
    chips = [(1 - px, py), (px, 1 - py), (1 - px, 1 - py)]
    return [pltpu.make_async_remote_copy(
        src_ref=p_refs[a].at[2 * qx + qy], dst_ref=out_refs[a].at[k],
        send_sem=send_sems.at[3 * a + k], recv_sem=recv_sems.at[3 * a + k],
        device_id=(qx, qy, pc), device_id_type=MESH) for k, (qx, qy) in enumerate(chips) for a in range(n)]


def _pair_sum(own, recv, ids, name, win_rows=None):
    _, R, C = recv.shape
    tr = _tile(R, 512, 8)

    def body(ids_ref, g_ref, r_ref, own_ref, pb_ref):
        s = g_ref[...] + r_ref[...]
        pb_ref[...] = s.astype(BF16)

        @pl.when(pl.program_id(1) == ids_ref[1])
        def _():
            own_ref[...] = s

    if win_rows is None:
        own_spec = pl.BlockSpec((None, None, tr, C), lambda r, j, ids: (j, ids[0], r, 0))
    else:
        own_spec = pl.BlockSpec((pl.Element(tr), pl.Element(C)), lambda r, j, ids: (pl.multiple_of(ids[2 + j] + r * tr, 8), 0))
    return pl.pallas_call(
        body, name=name,
        grid_spec=pltpu.PrefetchScalarGridSpec(
            num_scalar_prefetch=1, grid=(R // tr, 4),
            in_specs=[own_spec, pl.BlockSpec((None, tr, C), lambda r, j, ids: (j, r, 0))],
            out_specs=[pl.BlockSpec((tr, C), lambda r, j, ids: (r, 0)),
                       pl.BlockSpec((None, tr, C), lambda r, j, ids: (j, r, 0))]),
        out_shape=[SDS((R, C), F32), SDS((4, R, C), BF16)],
        compiler_params=_cparams(2),
    )(ids, own, recv)


def _final_sum(own, recv, name):
    R, C = own.shape
    tr = _tile(R, 512, 8)

    def body(o_ref, r_ref, out_ref):
        out_ref[...] = ((o_ref[...] + r_ref[0].astype(F32)) + r_ref[1].astype(F32)) + r_ref[2].astype(F32)

    return pl.pallas_call(
        body, name=name, grid=(R // tr,),
        in_specs=[pl.BlockSpec((tr, C), lambda r: (r, 0)), pl.BlockSpec((3, tr, C), lambda r: (0, r, 0))],
        out_specs=pl.BlockSpec((tr, C), lambda r: (r, 0)),
        out_shape=SDS((R, C), F32),
        compiler_params=_cparams(1),
    )(own, recv)


def _sum8(a, name):
    def body(a_ref, out_ref):
        s = a_ref[0]
        for d in range(1, N_DEV):
            s = s + a_ref[d]
        out_ref[...] = s

    return pl.pallas_call(body, name=name, out_shape=SDS(a.shape[1:], F32))(a)


def _inproj_fwd(x, g1, w_t, NM, tm):
    T, D = x.shape
    NF = w_t.shape[0] - NM
    ch = _tile(NM, 1024)

    def body(x_ref, g_ref, w_ref, pm_ref, fl_ref, h_ref):
        xv = x_ref[...]
        h = (xv * _rms(xv) * g_ref[...]).astype(BF16)
        h_ref[...] = h
        for c in range(0, NM, ch):
            pm_ref[:, c:c + ch] = _dot_nt(h, w_ref[c:c + ch, :]).astype(BF16)
        fl_ref[...] = _dot_nt(h, w_ref[NM:NM + NF, :])

    return pl.pallas_call(
        body, name="inproj_fwd", grid=(T // tm,),
        in_specs=[pl.BlockSpec((tm, D), lambda i: (i, 0)), _resident((1, D)), _resident(w_t.shape)],
        out_specs=[pl.BlockSpec((tm, NM), lambda i: (i, 0)), pl.BlockSpec((tm, NF), lambda i: (i, 0)),
                   pl.BlockSpec((tm, D), lambda i: (i, 0))],
        out_shape=[SDS((T, NM), BF16), SDS((T, NF), F32), SDS((T, D), BF16)],
        compiler_params=_cparams(1),
    )(x, g1, w_t)


def _log_sigmoid(x):
    return jnp.minimum(x, 0.0) - jnp.log(1.0 + jnp.exp(-jnp.abs(x)))


def _forget_cumsum(fl, bf, S):
    T, NF = fl.shape
    ch = _tile(S, 256, 8)

    def body(fl_ref, bf_ref, f_ref):
        row = lax.broadcasted_iota(jnp.int32, (ch, ch), 0)
        col = lax.broadcasted_iota(jnp.int32, (ch, ch), 1)
        tri = (col <= row).astype(BF16)
        carry = jnp.zeros((1, NF), F32)
        for c in range(0, S, ch):
            lf = _log_sigmoid(fl_ref[c:c + ch, :] + bf_ref[...])
            f_ref[c:c + ch, :] = _tri_dot(tri, lf) + carry
            carry = carry + jnp.sum(lf, axis=0, keepdims=True)

    return pl.pallas_call(
        body, name="forget_cumsum", grid=(T // S,),
        in_specs=[pl.BlockSpec((S, NF), lambda b: (b, 0)), _resident((1, NF))],
        out_specs=pl.BlockSpec((S, NF), lambda b: (b, 0)),
        out_shape=SDS((T, NF), F32),
        compiler_params=_cparams(1),
    )(fl, bf)


def _head_mask(e, hd):
    lanes = lax.broadcasted_iota(jnp.int32, (1, LANES), 1)
    return (lanes >= e * hd) & (lanes < (e + 1) * hd)


def _attn_specs(S, q_off, AW):
    ng = AW // LANES
    return [pl.BlockSpec((S, LANES), lambda b, g, o=q_off + w * ng: (b, o + g)) for w in range(3)]


def _attn_fwd(pm, fcum, frow, S, q_off, AW, hd, tq, gather):
    T = pm.shape[0]
    scale = float(hd) ** -0.5
    nq, hpg = S // tq, LANES // hd
    ng, nx = AW // LANES, len(gather)
    steps = (T // S) * ng

    def body(q_ref, k_ref, v_ref, fc_ref, fr_ref, *rest):
        x_refs, (o_ref, lse_ref), out_refs, sems = rest[:nx], rest[nx:nx + 2], rest[nx + 2:2 * nx + 2], rest[2 * nx + 2:]
        g = pl.program_id(1)
        step = pl.program_id(0) * ng + g
        start, forward, finish = _gather_phases(x_refs, out_refs, *sems)
        pl.when(step == 0)(start)

        @pl.when(g == 0)
        def _():
            lse_ref[...] = jnp.zeros_like(lse_ref)

        lanes = lax.broadcasted_iota(jnp.int32, (1, LANES), 1)
        for i in range(nq):
            rs, kend = slice(i * tq, (i + 1) * tq), (i + 1) * tq
            row = i * tq + lax.broadcasted_iota(jnp.int32, (tq, kend), 0)
            col = lax.broadcasted_iota(jnp.int32, (tq, kend), 1)
            o_tile = jnp.zeros((tq, LANES), F32)
            lse_tile = lse_ref[rs, :]
            for e in range(hpg):
                mask = _head_mask(e, hd)
                qs = jnp.where(mask, q_ref[rs, :], 0) * scale
                s = _dot_nt(qs, k_ref[0:kend, :]) + _lane_pick(fc_ref[rs, :], g * hpg + e) - fr_ref[e:e + 1, 0:kend]
                s = jnp.where(col <= row, s, -1e30)
                m = jnp.max(s, axis=1, keepdims=True)
                p = jnp.exp(s - m)
                l = jnp.sum(p, axis=1, keepdims=True)
                o_tile = jnp.where(mask, _dot(p.astype(BF16), v_ref[0:kend, :]) / l, o_tile)
                lse_tile = jnp.where(lanes == g * hpg + e, m + jnp.log(l), lse_tile)
            o_ref[rs, :] = o_tile.astype(BF16)
            lse_ref[rs, :] = lse_tile
        pl.when(step == (steps * 5) // 8)(forward)
        pl.when(step == steps - 1)(finish)

    full = pl.BlockSpec((S, LANES), lambda b, g: (b, 0))
    return pl.pallas_call(
        body, name="attn_fwd", grid=(T // S, ng),
        in_specs=_attn_specs(S, q_off, AW) + [full, pl.BlockSpec((None, None, 8, S), lambda b, g: (b, g, 0, 0))] + [ANY] * nx,
        out_specs=[pl.BlockSpec((S, LANES), lambda b, g: (b, g)), full] + [ANY] * nx,
        out_shape=[SDS((T, AW), BF16), SDS((T, LANES), F32)] + _gather_shapes(gather),
        scratch_shapes=_gather_sems(nx),
        compiler_params=_cparams(2),
    )(pm, pm, pm, fcum, frow, *gather)


def _merge_fwd(pm, o, x, wc, bg, woc_t, woa_t, wo, S, tm):
    T, D = x.shape
    CW, AW = woc_t.shape[1], woa_t.shape[1]
    tps = S // tm

    def body(cb_ref, cc_ref, cin_ref, gc_ref, ga_ref, o_ref, x_ref, wc_ref, bg_ref, woct_ref, woat_ref, wo_ref,
             x2_ref, u_ref, m_ref, ycin_ref, yc_ref, ya_ref, tail_ref):
        @pl.when(pl.program_id(0) % tps == 0)
        def _():
            tail_ref[...] = jnp.zeros_like(tail_ref)

        z = cc_ref[...].astype(F32) * cin_ref[...].astype(F32)
        z1, z2 = _taps_back(z, tail_ref[6:7, :], tail_ref[7:8, :])
        tail_ref[...] = z[tm - 8:tm, :]
        u = (z * wc_ref[2:3, :] + z2 * wc_ref[0:1, :]) + z1 * wc_ref[1:2, :]
        u_ref[...] = u.astype(BF16)
        ycin = (cb_ref[...].astype(F32) * u).astype(BF16)
        ycin_ref[...] = ycin
        yc = _dot_nt(ycin, woct_ref[...])
        ya = _dot_nt(o_ref[...], woat_ref[...])
        yc_ref[...] = yc.astype(BF16)
        ya_ref[...] = ya.astype(BF16)
        gc = _sigmoid(gc_ref[...].astype(F32) + bg_ref[:, 0:D])
        ga = _sigmoid(ga_ref[...].astype(F32) + bg_ref[:, D:2 * D])
        m = (gc * yc + ga * ya).astype(BF16)
        m_ref[...] = m
        x2_ref[...] = x_ref[...] + _dot(m, wo_ref[...])

    g_off = (3 * CW + 3 * AW) // D
    tok = lambda w, j=0: pl.BlockSpec((tm, w), lambda i: (i, j))
    return pl.pallas_call(
        body, name="merge_fwd", grid=(T // tm,),
        in_specs=[tok(CW, 0), tok(CW, 1), tok(CW, 2), tok(D, g_off), tok(D, g_off + 1), tok(AW), tok(D),
                  _resident((8, CW)), _resident((1, 2 * D)), _resident((D, CW)), _resident((D, AW)), _resident((D, D))],
        out_specs=[tok(D), tok(CW), tok(D), tok(CW), tok(D), tok(D)],
        out_shape=[SDS((T, D), F32), SDS((T, CW), BF16), SDS((T, D), BF16), SDS((T, CW), BF16),
                   SDS((T, D), BF16), SDS((T, D), BF16)],
        scratch_shapes=[pltpu.VMEM((8, CW), F32)],
        compiler_params=_cparams(1),
    )(pm, pm, pm, pm, pm, o, x, wc, bg, woc_t, woa_t, wo)


def _ffn_fwd(x2, g2, wup_t, wf, wdn, gf, tgt, S, tm):
    T, D = x2.shape
    FH = wdn.shape[0]
    ch = _tile(FH, FFN_CHUNK)
    tps = S // tm

    def body(x2_ref, g2_ref, wupt_ref, wf_ref, wdn_ref, gf_ref, tgt_ref,
             upp_ref, up_ref, act_ref, h2_ref, dx3_ref, loss_ref, dgf_ref, tail_ref):
        i = pl.program_id(0)

        @pl.when(i % tps == 0)
        def _():
            tail_ref[...] = jnp.zeros_like(tail_ref)

        @pl.when(i == 0)
        def _():
            loss_ref[...] = jnp.zeros_like(loss_ref)
            dgf_ref[...] = jnp.zeros_like(dgf_ref)

        x2v = x2_ref[...]
        h2 = (x2v * _rms(x2v) * g2_ref[...]).astype(BF16)
        h2_ref[...] = h2
        x3 = x2v
        for c in range(0, FH, ch):
            gated = []
            for cols in (slice(c, c + ch), slice(FH + c, FH + c + ch)):
                upp = _dot_nt(h2, wupt_ref[cols, :])
                upp_ref[:, cols] = upp.astype(BF16)
                p1, p2 = _taps_back(upp, tail_ref[6:7, cols], tail_ref[7:8, cols])
                tail_ref[:, cols] = upp[tm - 8:tm, :]
                up = (upp * wf_ref[2:3, cols] + p2 * wf_ref[0:1, cols]) + p1 * wf_ref[1:2, cols]
                up_ref[:, cols] = up.astype(BF16)
                gated.append(up)
            a, b = gated
            act = (a * _sigmoid(a) * b).astype(BF16)
            act_ref[:, c:c + ch] = act
            x3 = x3 + _dot(act, wdn_ref[c:c + ch, :])
        r3 = _rms(x3)
        xn3 = x3 * r3
        e = xn3 * gf_ref[...] - tgt_ref[...]
        loss_ref[...] += 0.5 * jnp.sum(jnp.mean(e * e, axis=-1, keepdims=True), axis=0, keepdims=True)
        dy = e / D
        dgf_ref[0:1, :] += jnp.sum(dy * xn3, axis=0, keepdims=True)
        dxn = dy * gf_ref[...]
        dx3_ref[...] = r3 * (dxn - xn3 * jnp.mean(dxn * xn3, axis=-1, keepdims=True))

    tok = lambda w: pl.BlockSpec((tm, w), lambda i: (i, 0))
    return pl.pallas_call(
        body, name="ffn_fwd", grid=(T // tm,),
        in_specs=[tok(D), _resident((1, D)), _resident((2 * FH, D)), _resident((8, 2 * FH)), _resident((FH, D)),
                  _resident((1, D)), tok(D)],
        out_specs=[tok(2 * FH), tok(2 * FH), tok(FH), tok(D), tok(D), pl.BlockSpec((8, LANES), lambda i: (0, 0)),
                   pl.BlockSpec((8, D), lambda i: (0, 0))],
        out_shape=[SDS((T, 2 * FH), BF16), SDS((T, 2 * FH), BF16), SDS((T, FH), BF16), SDS((T, D), BF16), SDS((T, D), F32),
                   SDS((8, LANES), F32), SDS((8, D), F32)],
        scratch_shapes=[pltpu.VMEM((8, 2 * FH), F32)],
        compiler_params=_cparams(1),
    )(x2, g2, wup_t, wf, wdn, gf, tgt)


def _ffn_bwd(dx3, x2, g2, upp, up, wf, wdn, wup_t, S, tm):
    T, D = x2.shape
    FH = wdn.shape[0]
    ch = _tile(FH, FFN_CHUNK)
    n, tps = T // tm, S // tm

    def body(dx3_ref, x2_ref, g2_ref, upp_ref, up_ref, wf_ref, wdn_ref, wupt_ref,
             dx2_ref, dupp_ref, dwf_ref, dg2_ref, head_ref):
        i = pl.program_id(0)
        t = n - 1 - i

        @pl.when(i == 0)
        def _():
            dwf_ref[...] = jnp.zeros_like(dwf_ref)
            dg2_ref[...] = jnp.zeros_like(dg2_ref)

        @pl.when(t % tps == tps - 1)
        def _():
            head_ref[...] = jnp.zeros_like(head_ref)

        dx3v = dx3_ref[...]
        dx3b = dx3v.astype(BF16)
        x2v = x2_ref[...]
        r2 = _rms(x2v)
        xn2 = x2v * r2
        dh2 = jnp.zeros((tm, D), F32)
        for c in range(0, FH, ch):
            ca, cb = slice(c, c + ch), slice(FH + c, FH + c + ch)
            a, b = up_ref[:, ca].astype(F32), up_ref[:, cb].astype(F32)
            sig = _sigmoid(a)
            sl = a * sig
            dact = _dot_nt(dx3b, wdn_ref[ca, :])
            grads = (dact * b * (sig * (1.0 + a * (1.0 - sig))), dact * sl)
            for cols, d in zip((ca, cb), grads):
                u0 = upp_ref[:, cols].astype(F32)
                d1, d2 = _taps_ahead(d, head_ref[0:1, cols], head_ref[1:2, cols])
                head_ref[:, cols] = d[0:8, :]
                dwf_ref[2:3, cols] += jnp.sum(u0 * d, axis=0, keepdims=True)
                dwf_ref[1:2, cols] += jnp.sum(u0 * d1, axis=0, keepdims=True)
                dwf_ref[0:1, cols] += jnp.sum(u0 * d2, axis=0, keepdims=True)
                dpre = ((d * wf_ref[2:3, cols] + d1 * wf_ref[1:2, cols]) + d2 * wf_ref[0:1, cols]).astype(BF16)
                dupp_ref[:, cols] = dpre
                dh2 = dh2 + _dot(dpre, wupt_ref[cols, :])
        dg2_ref[0:1, :] += jnp.sum(dh2 * xn2, axis=0, keepdims=True)
        dxn = dh2 * g2_ref[...]
        dx2_ref[...] = dx3v + r2 * (dxn - xn2 * jnp.mean(dxn * xn2, axis=-1, keepdims=True))

    tok = lambda w: pl.BlockSpec((tm, w), lambda i: (n - 1 - i, 0))
    acc = lambda w: pl.BlockSpec((8, w), lambda i: (0, 0))
    return pl.pallas_call(
        body, name="ffn_bwd", grid=(n,),
        in_specs=[tok(D), tok(D), _resident((1, D)), tok(2 * FH), tok(2 * FH), _resident((8, 2 * FH)),
                  _resident((FH, D)), _resident((2 * FH, D))],
        out_specs=[tok(D), tok(2 * FH), acc(2 * FH), acc(D)],
        out_shape=[SDS((T, D), F32), SDS((T, 2 * FH), BF16), SDS((8, 2 * FH), F32), SDS((8, D), F32)],
        scratch_shapes=[pltpu.VMEM((8, 2 * FH), F32)],
        compiler_params=_cparams(1),
    )(dx3, x2, g2, upp, up, wf, wdn, wup_t)


def _merge_bwd(dx2, pm, u, yc, ya, m, ycin, o, wc, bg, wo, woc_t, woa_t, S, tm, exchange):
    T, D = dx2.shape
    CW, AW = woc_t.shape[1], woa_t.shape[1]
    n, tps = T // tm, S // tm
    nx = len(exchange)
    rb = D // N_DEV
    assert CW + AW == D

    def body(dx2_ref, cb_ref, cc_ref, cin_ref, gc_ref, ga_ref, u_ref, yc_ref, ya_ref, m_ref, ycin_ref, o_ref,
             wc_ref, bg_ref, wo_ref, woct_ref, woat_ref, *rest):
        x_refs = rest[:nx]
        dconv_ref, dgl_ref, do_ref, dws_ref, dwc_ref, dbg_ref = rest[nx:nx + 6]
        got_refs, head_ref, sems = rest[nx + 6:2 * nx + 6], rest[2 * nx + 6], rest[2 * nx + 7:]

        def add_blocks(rows, cols, grad):
            for d in range(N_DEV):
                dws_ref[d, rows, cols] += grad[d * rb:(d + 1) * rb, :]

        i = pl.program_id(0)
        t = n - 1 - i

        @pl.when(i == 0)
        def _():
            for cp in _sibling_copies(x_refs, got_refs, *sems):
                cp.start()

        @pl.when(i == 0)
        def _():
            for ref in (dwc_ref, dbg_ref, dws_ref):
                ref[...] = jnp.zeros_like(ref)

        @pl.when(t % tps == tps - 1)
        def _():
            head_ref[...] = jnp.zeros_like(head_ref)

        dx2b = dx2_ref[...].astype(BF16)
        add_blocks(slice(0, rb), slice(0, D), _dot_tn(m_ref[...], dx2b))
        dm = _dot_nt(dx2b, wo_ref[...])
        outs = []
        for g_ref, y_ref, cols in ((gc_ref, yc_ref, slice(0, D)), (ga_ref, ya_ref, slice(D, 2 * D))):
            g = _sigmoid(g_ref[...].astype(F32) + bg_ref[:, cols])
            dgl = dm * y_ref[...].astype(F32) * g * (1.0 - g)
            dgl_ref[:, cols] = dgl.astype(BF16)
            dbg_ref[0:1, cols] += jnp.sum(dgl, axis=0, keepdims=True)
            outs.append((dm * g).astype(BF16))
        dyc, dya = outs
        add_blocks(slice(rb, 2 * rb), slice(0, CW), _dot_tn(dyc, ycin_ref[...]))
        add_blocks(slice(rb, 2 * rb), slice(CW, CW + AW), _dot_tn(dya, o_ref[...]))
        do_ref[...] = _dot(dya, woat_ref[...]).astype(BF16)
        dycin = _dot(dyc, woct_ref[...])
        cc, cin = cc_ref[...].astype(F32), cin_ref[...].astype(F32)
        z = cc * cin
        du = dycin * cb_ref[...].astype(F32)
        du1, du2 = _taps_ahead(du, head_ref[0:1, :], head_ref[1:2, :])
        head_ref[...] = du[0:8, :]
        dwc_ref[2:3, :] += jnp.sum(z * du, axis=0, keepdims=True)
        dwc_ref[1:2, :] += jnp.sum(z * du1, axis=0, keepdims=True)
        dwc_ref[0:1, :] += jnp.sum(z * du2, axis=0, keepdims=True)
        dz = (du * wc_ref[2:3, :] + du1 * wc_ref[1:2, :]) + du2 * wc_ref[0:1, :]
        dconv_ref[:, 0:CW] = (dycin * u_ref[...].astype(F32)).astype(BF16)
        dconv_ref[:, CW:2 * CW] = (dz * cin).astype(BF16)
        dconv_ref[:, 2 * CW:3 * CW] = (dz * cc).astype(BF16)

        @pl.when(i == n - 1)
        def _():
            for cp in _sibling_copies(x_refs, got_refs, *sems):
                cp.wait()

    g_off = (3 * CW + 3 * AW) // D
    tok = lambda w, j=0: pl.BlockSpec((tm, w), lambda i: (n - 1 - i, j))
    acc = lambda w: pl.BlockSpec((8, w), lambda i: (0, 0))
    return pl.pallas_call(
        body, name="merge_bwd", grid=(n,),
        in_specs=[tok(D), tok(CW, 0), tok(CW, 1), tok(CW, 2), tok(D, g_off), tok(D, g_off + 1), tok(CW), tok(D), tok(D),
                  tok(D), tok(CW), tok(AW),
                  _resident((8, CW)), _resident((1, 2 * D)), _resident((D, D)), _resident((D, CW)), _resident((D, AW))]
                 + [ANY] * nx,
        out_specs=[tok(3 * CW), tok(2 * D), tok(AW), pl.BlockSpec((N_DEV, 2 * rb, D), lambda i: (0, 0, 0)), acc(CW),
                   acc(2 * D)] + [ANY] * nx,
        out_shape=[SDS((T, 3 * CW), BF16), SDS((T, 2 * D), BF16), SDS((T, AW), BF16), SDS((N_DEV, 2 * rb, D), F32),
                   SDS((8, CW), F32), SDS((8, 2 * D), F32)] + _sibling_shapes(exchange),
        scratch_shapes=[pltpu.VMEM((8, CW), F32)] + _exchange_sems(nx),
        compiler_params=_cparams(1),
    )(dx2, pm, pm, pm, pm, pm, u, yc, ya, m, ycin, o, wc, bg, wo, woc_t, woa_t, *exchange)


def _attn_bwd(pm, do, fcum, frow, lse, S, q_off, AW, hd, tq, exchange):
    T = pm.shape[0]
    scale = float(hd) ** -0.5
    nq, hpg, ng = S // tq, LANES // hd, AW // LANES
    nx = len(exchange)
    steps = (T // S) * ng

    def body(q_ref, k_ref, v_ref, do_ref, fc_ref, fr_ref, lse_ref, *rest):
        x_refs, (dq_ref, dk_ref, dv_ref, dfr_ref) = rest[:nx], rest[nx:nx + 4]
        got_refs, (dk_acc, dv_acc), sems = rest[nx + 4:2 * nx + 4], rest[2 * nx + 4:2 * nx + 6], rest[2 * nx + 6:]
        g = pl.program_id(1)
        step = pl.program_id(0) * ng + g

        @pl.when(step == 0)
        def _():
            for cp in _chip_copies(x_refs, got_refs, *sems):
                cp.start()

        dk_acc[...] = jnp.zeros_like(dk_acc)
        dv_acc[...] = jnp.zeros_like(dv_acc)
        dfr_ref[...] = jnp.zeros_like(dfr_ref)
        for i in range(nq):
            rs, kend = slice(i * tq, (i + 1) * tq), (i + 1) * tq
            row = i * tq + lax.broadcasted_iota(jnp.int32, (tq, kend), 0)
            col = lax.broadcasted_iota(jnp.int32, (tq, kend), 1)
            kk, vv = k_ref[0:kend, :], v_ref[0:kend, :]
            dq_tile = jnp.zeros((tq, LANES), F32)
            for e in range(hpg):
                mask = _head_mask(e, hd)
                qs = jnp.where(mask, q_ref[rs, :], 0) * scale
                doi = jnp.where(mask, do_ref[rs, :], 0)
                s = _dot_nt(qs, kk) + _lane_pick(fc_ref[rs, :], g * hpg + e) - fr_ref[e:e + 1, 0:kend]
                p = jnp.where(col <= row, jnp.exp(s - _lane_pick(lse_ref[rs, :], g * hpg + e)), 0.0)
                dp = _dot_nt(doi, vv)
                ds = p * (dp - jnp.sum(p * dp, axis=1, keepdims=True))
                pb, dsb = p.astype(BF16), ds.astype(BF16)
                dq_tile = jnp.where(mask, _dot(dsb, kk) * scale, dq_tile)
                dv_acc[0:kend, :] += _dot_tn(pb, doi)
                dk_acc[0:kend, :] += _dot_tn(dsb, qs)
                dfr_ref[e:e + 1, 0:kend] -= jnp.sum(ds, axis=0, keepdims=True)
            dq_ref[rs, :] = dq_tile.astype(BF16)
        dk_ref[...] = dk_acc[...].astype(BF16)
        dv_ref[...] = dv_acc[...].astype(BF16)

        @pl.when(step == steps - 1)
        def _():
            for cp in _chip_copies(x_refs, got_refs, *sems):
                cp.wait()

    full = pl.BlockSpec((S, LANES), lambda b, g: (b, 0))
    grp = pl.BlockSpec((S, LANES), lambda b, g: (b, g))
    rows = pl.BlockSpec((None, None, 8, S), lambda b, g: (b, g, 0, 0))
    return pl.pallas_call(
        body, name="attn_bwd", grid=(T // S, ng),
        in_specs=_attn_specs(S, q_off, AW) + [grp, full, rows, full] + [ANY] * nx,
        out_specs=[grp, grp, grp, rows] + [ANY] * nx,
        out_shape=[SDS((T, AW), BF16), SDS((T, AW), BF16), SDS((T, AW), BF16), SDS((T // S, ng, 8, S), F32)]
                  + _chip_shapes(exchange),
        scratch_shapes=[pltpu.VMEM((S, LANES), F32), pltpu.VMEM((S, LANES), F32)] + _exchange_sems(3 * nx),
        compiler_params=_cparams(2),
    )(pm, pm, pm, do, fcum, frow, lse, *exchange)


def _forget_bwd(dfc, fl, bf, S):
    T, NF = fl.shape
    ch = _tile(S, 256, 8)

    def body(df_ref, fl_ref, bf_ref, dfl_ref, dbf_ref):
        @pl.when(pl.program_id(0) == 0)
        def _():
            dbf_ref[...] = jnp.zeros_like(dbf_ref)

        row = lax.broadcasted_iota(jnp.int32, (ch, ch), 0)
        col = lax.broadcasted_iota(jnp.int32, (ch, ch), 1)
        tri = (col >= row).astype(BF16)
        carry = jnp.zeros((1, NF), F32)
        for c in range(S - ch, -1, -ch):
            d = df_ref[c:c + ch, :]
            dlf = _tri_dot(tri, d) + carry
            carry = carry + jnp.sum(d, axis=0, keepdims=True)
            dfl = dlf * _sigmoid(-(fl_ref[c:c + ch, :] + bf_ref[...]))
            dfl_ref[c:c + ch, :] = dfl.astype(BF16)
            dbf_ref[0:1, :] += jnp.sum(dfl, axis=0, keepdims=True)

    return pl.pallas_call(
        body, name="forget_bwd", grid=(T // S,),
        in_specs=[pl.BlockSpec((S, NF), lambda b: (b, 0)), pl.BlockSpec((S, NF), lambda b: (b, 0)), _resident((1, NF))],
        out_specs=[pl.BlockSpec((S, NF), lambda b: (b, 0)), pl.BlockSpec((8, NF), lambda b: (0, 0))],
        out_shape=[SDS((T, NF), BF16), SDS((8, NF), F32)],
        compiler_params=_cparams(1),
    )(dfc, fl, bf)


def _inproj_bwd(dparts, offs, w_t, x, g1, dx2, tm, exchange):
    T, D = x.shape
    npart, nx = len(dparts), len(exchange)
    n = T // tm

    def body(*refs):
        d_refs = refs[:npart]
        w_ref, x_ref, g_ref, dx2_ref = refs[npart:npart + 4]
        x_refs = refs[npart + 4:npart + 4 + nx]
        dx_ref, dg_ref, dwl_ref = refs[npart + 4 + nx:npart + 7 + nx]
        got_refs, sems = refs[npart + 7 + nx:npart + 7 + 2 * nx], refs[npart + 7 + 2 * nx:]

        @pl.when(pl.program_id(0) == 0)
        def _():
            for cp in _chip_copies(x_refs, got_refs, *sems):
                cp.start()

        @pl.when(pl.program_id(0) == 0)
        def _():
            dg_ref[...] = jnp.zeros_like(dg_ref)
            dwl_ref[...] = jnp.zeros_like(dwl_ref)

        dh = None
        for d_ref, off in zip(d_refs, offs):
            term = _dot(d_ref[...], w_ref[off:off + d_ref.shape[1], :])
            dh = term if dh is None else dh + term
        xv = x_ref[...]
        r = _rms(xv)
        xn = xv * r
        dg_ref[0:1, :] += jnp.sum(dh * xn, axis=0, keepdims=True)
        dwl_ref[...] += _dot_tn(d_refs[-1][...], (xn * g_ref[...]).astype(BF16))
        dxn = dh * g_ref[...]
        dx_ref[...] = dx2_ref[...] + r * (dxn - xn * jnp.mean(dxn * xn, axis=-1, keepdims=True))

        @pl.when(pl.program_id(0) == n - 1)
        def _():
            for cp in _chip_copies(x_refs, got_refs, *sems):
                cp.wait()

    tok = lambda w: pl.BlockSpec((tm, w), lambda i: (i, 0))
    nl = dparts[-1].shape[1]
    return pl.pallas_call(
        body, name="inproj_bwd", grid=(n,),
        in_specs=[tok(d.shape[1]) for d in dparts] + [_resident(w_t.shape), tok(D), _resident((1, D)), tok(D)] + [ANY] * nx,
        out_specs=[tok(D), pl.BlockSpec((8, D), lambda i: (0, 0)), pl.BlockSpec((nl, D), lambda i: (0, 0))] + [ANY] * nx,
        out_shape=[SDS((T, D), F32), SDS((8, D), F32), SDS((nl, D), F32)] + _chip_shapes(exchange),
        scratch_shapes=_exchange_sems(3 * nx),
        compiler_params=_cparams(1),
    )(*dparts, w_t, x, g1, dx2, *exchange)


def _wgrad(bs, a, name, into=None, row_off=0, total_rows=None):
    bs = list(bs) if isinstance(bs, (list, tuple)) else [bs]
    P = len(bs)
    T, N = bs[0].shape
    M = a.shape[1]
    tn = _tile(N, 1408 if M <= 1024 else 512)
    while row_off % tn:
        tn = _tile(N, tn - LANES)
    tk = _tile(T, 512, 16)
    per_row = 2 * (P * tn * bs[0].dtype.itemsize + M * a.dtype.itemsize)
    while T % (2 * tk) == 0 and 2 * tk * per_row + 2 * tn * M * 4 <= WGRAD_VMEM_BUDGET:
        tk *= 2
    blk0, nj, nk = row_off // tn, N // tn, T // tk

    def body(*refs):
        b_refs, a_ref, o_ref = refs[:P], refs[P], refs[-1]
        j = pl.program_id(0)

        @pl.when(pl.program_id(1) == 0)
        def _():
            o_ref[...] = jnp.zeros_like(o_ref)

        for p, b_ref in enumerate(b_refs):
            @pl.when((j >= p * nj) & (j < (p + 1) * nj))
            def _(b_ref=b_ref):
                o_ref[...] += _dot_tn(b_ref[...].astype(BF16), a_ref[...].astype(BF16))

    def b_spec(p):
        def index(j, k):
            before, mine = j < p * nj, (j >= p * nj) & (j < (p + 1) * nj)
            return (jnp.where(mine, k, jnp.where(before, 0, nk - 1)),
                    jnp.where(mine, j - p * nj, jnp.where(before, 0, nj - 1)))
        return pl.BlockSpec((tk, tn), index)

    in_specs = [b_spec(p) for p in range(P)] + [pl.BlockSpec((tk, M), lambda j, k: (k, 0))]
    args = (*bs, a)
    kwargs = {}
    if into is not None:
        in_specs.append(ANY)
        args += (into,)
        kwargs["input_output_aliases"] = {P + 1: 0}
        total_rows = into.shape[0]
    return pl.pallas_call(
        body, name=name, grid=(P * nj, nk),
        in_specs=in_specs,
        out_specs=pl.BlockSpec((tn, M), lambda j, k: (blk0 + j, 0)),
        out_shape=SDS((P * N if total_rows is None else total_rows, M), F32),
        compiler_params=_cparams(2), **kwargs,
    )(*args)


def _adamw(w, g, m, v, name):
    shape = w.shape
    C = shape[-1]
    w2, g2, m2, v2 = (a.reshape(-1, C) for a in (w, g, m, v))
    R = w2.shape[0]
    tr = R if R <= 512 else _tile(R, 256, 8)
    if tr < 64:
        tr = R

    def body(w_ref, g_ref, m_ref, v_ref, d_ref, nm_ref, nv_ref):
        gv = g_ref[...]
        mv = ADAM_B1 * m_ref[...] + (1.0 - ADAM_B1) * gv
        vv = ADAM_B2 * v_ref[...] + (1.0 - ADAM_B2) * (gv * gv)
        m_hat = mv / (1.0 - ADAM_B1 ** ADAM_STEP)
        v_hat = vv / (1.0 - ADAM_B2 ** ADAM_STEP)
        d_ref[...] = -ADAM_LR * (m_hat / (jnp.sqrt(v_hat) + ADAM_EPS) + ADAM_WD * w_ref[...])
        nm_ref[...] = mv
        nv_ref[...] = vv

    spec = pl.BlockSpec((tr, C), lambda r: (r, 0))
    outs = pl.pallas_call(
        body, name=name, grid=(R // tr,),
        in_specs=[spec] * 4, out_specs=[spec] * 3, out_shape=[SDS((R, C), F32)] * 3,
        compiler_params=_cparams(1),
    )(w2, g2, m2, v2)
    return tuple(o.reshape(shape) for o in outs)


def _rows(a, L):
    lead = a.shape[0]
    flat = a.reshape(lead, -1)
    n = flat.shape[1]
    r = -(-n // L)
    return jnp.pad(flat, ((0, 0), (0, r * L - n))).reshape(lead, r, L)


def _unrows(p, shape):
    return p.reshape(-1)[:int(np.prod(shape))].reshape(shape)


def _from_col_blocks(a):
    n, R, c = a.shape
    return a.transpose(1, 0, 2).reshape(R, n * c)


def _in_windows(n_loc, nqkv, H, NM):
    win = (n_loc + 7 + 7) // 8 * 8
    starts, index = [], np.zeros((N_DEV, n_loc), np.int32)
    for d in range(N_DEV):
        rows = np.arange(n_loc * d, n_loc * (d + 1))
        is_f = (rows >= nqkv) & (rows < nqkv + H)
        kept = np.where(rows < nqkv, rows, rows - H)
        lo = int(kept[~is_f].min())
        start = lo // 8 * 8
        assert int(kept[~is_f].max()) - start < win and start + win <= NM + LANES
        starts.append(start)
        index[d] = np.where(is_f, win + rows - nqkv, kept - start)
    return starts, win, index


def kernel(x, norm_mix_g, w_in, b_f, b_gate, conv_mix_w, w_out_conv, w_out_attn, w_o, norm_ffn_g, w_up, conv_ffn_w, w_down, norm_f_g, loss_target, m_norm_mix_g, m_w_in, m_b_f, m_b_gate, m_conv_mix_w, m_w_out_conv, m_w_out_attn, m_w_o, m_norm_ffn_g, m_w_up, m_conv_ffn_w, m_w_down, m_norm_f_g, v_norm_mix_g, v_w_in, v_b_f, v_b_gate, v_conv_mix_w, v_w_out_conv, v_w_out_attn, v_w_o, v_norm_ffn_g, v_w_up, v_conv_ffn_w, v_w_down, v_norm_f_g):
    Bl, S, D = x.shape
    T = Bl * S
    H = b_f.shape[-1]
    CW = N_DEV * conv_mix_w.shape[-1]
    AW = w_out_attn.shape[1]
    hd = AW // H
    FH = N_DEV * w_down.shape[1]
    n_loc = w_in.shape[-1]
    NIN = N_DEV * n_loc
    NM = 3 * CW + 3 * AW + 2 * D
    nqkv = 3 * CW + 3 * AW
    assert NIN == NM + H and w_out_conv.shape[1] == CW and nqkv % D == 0 and AW % LANES == 0 and LANES % hd == 0
    hpg, ng = LANES // hd, AW // LANES
    assert hpg <= 8
    tm_big = min(512, S // 2)
    tm_ffn = min(256, S // 2)
    tq_fwd = min(512, S // 2)
    tq_bwd = min(256, S // 2)
    px, py, pc = _place()
    me = 4 * px + 2 * py + pc

    bits = lambda a: lax.bitcast_convert_type(a, BF16)
    taps = jnp.concatenate([_rows(bits(conv_ffn_w[0])[None], D)[0], _rows(bits(conv_mix_w[0])[None], D)[0]], axis=0)
    n_ffn_rows = -(-conv_ffn_w[0].size * 2 // D)
    late = [w_up[0].T.astype(BF16), w_down[0].astype(BF16), w_o[0].astype(BF16), w_out_conv[0].T.astype(BF16),
            w_out_attn[0].T.astype(BF16), taps]
    g_in, = _all_gather([w_in[0].T.astype(BF16)], "weights_all_gather")
    W_in_rows = g_in.reshape(NIN, D)
    W_in_t = jnp.concatenate([W_in_rows[:nqkv], W_in_rows[nqkv + H:], W_in_rows[nqkv:nqkv + H],
                              jnp.zeros((LANES - H, D), BF16)], axis=0)
    bf128 = jnp.pad(b_f, ((0, 0), (0, LANES - H)))

    x2d = x.reshape(T, D)
    tgt = loss_target.reshape(T, D)
    pm, fl, h1 = _inproj_fwd(x2d, norm_mix_g, W_in_t, NM, tm_big)
    fcum = _forget_cumsum(fl, bf128, S)
    frow = jnp.pad(fcum[:, :H].reshape(Bl, S, ng, hpg).transpose(0, 2, 3, 1), ((0, 0), (0, 0), (0, 8 - hpg), (0, 0)))
    q_off = 3 * CW // LANES
    o, lse, g_up, g_dn, g_o, g_oc, g_oa, g_taps = _attn_fwd(pm, fcum, frow, S, q_off, AW, hd, tq_fwd, late)
    W_up_t = g_up.reshape(2 * FH, D)
    W_dn = g_dn.reshape(FH, D)
    W_o = g_o.reshape(D, D)
    W_oc_t = g_oc.reshape(D, CW)
    W_oa_t = g_oa.reshape(D, AW)
    tap_bits = g_taps.reshape(N_DEV, -1)
    n_ffn, n_mix = conv_ffn_w[0].size * 2, conv_mix_w[0].size * 2
    wf_full = _from_col_blocks(lax.bitcast_convert_type(
        tap_bits[:, :n_ffn].reshape((N_DEV,) + conv_ffn_w.shape[1:] + (2,)), F32))
    wc_full = _from_col_blocks(lax.bitcast_convert_type(
        tap_bits[:, n_ffn_rows * D:n_ffn_rows * D + n_mix].reshape((N_DEV,) + conv_mix_w.shape[1:] + (2,)), F32))
    wf8 = jnp.pad(wf_full, ((0, 5), (0, 0)))
    wc8 = jnp.pad(wc_full, ((0, 5), (0, 0)))
    x2, u, m, ycin, yc, ya = _merge_fwd(pm, o, x2d, wc8, b_gate, W_oc_t, W_oa_t, W_o, S, tm_big)
    upp, up, act, h2, dx3, loss8, dgf8 = _ffn_fwd(x2, norm_ffn_g, W_up_t, wf8, W_dn, norm_f_g.reshape(1, D), tgt, S, tm_ffn)

    dx2, dupp, dwf8, dg2_8 = _ffn_bwd(dx3, x2, norm_ffn_g, upp, up, wf8, W_dn, W_up_t, S, tm_ffn)
    dW_dn = _wgrad(act, dx3, "wgrad_down")
    dW_up_t = _wgrad(dupp, h2, "wgrad_up")
    ids = jnp.stack([pc, 2 * px + py]).astype(jnp.int32)
    big = [dW_up_t.reshape(4, 2, -1, D), dW_dn.reshape(4, 2, -1, D)]
    dconv, dgl, do, dW_small, dwc8, dbg8, *sib_big = _merge_bwd(
        dx2, pm, u, yc, ya, m, ycin, o, wc8, b_gate, W_o, W_oc_t, W_oa_t, S, tm_big, big)
    big_sums = [_pair_sum(b, r, ids, "grads_pair_sum_%d" % a) for a, (b, r) in enumerate(zip(big, sib_big))]
    dq, dk, dv, dfr, *chips_big = _attn_bwd(pm, do, fcum, frow, lse, S, q_off, AW, hd, tq_bwd, [s[1] for s in big_sums])
    dfc = jnp.pad(dfr[:, :, :hpg, :].transpose(0, 3, 1, 2).reshape(T, H), ((0, 0), (0, LANES - H)))
    dfl, dbf8 = _forget_bwd(dfc, fl, bf128, S)
    dparts = [dconv, dq, dk, dv, dgl, dfl]
    offs = [0, 3 * CW, 3 * CW + AW, 3 * CW + 2 * AW, nqkv, NM]
    dW_in_t = _wgrad(dconv, h1, "wgrad_in_conv", total_rows=NM + LANES)
    dW_in_t = _wgrad([dq, dk, dv], h1, "wgrad_in_qkv", into=dW_in_t, row_off=3 * CW)
    dW_in_t = _wgrad(dgl, h1, "wgrad_in_gates", into=dW_in_t, row_off=nqkv)

    starts, win, index = _in_windows(n_loc, nqkv, H, NM)
    small = dW_small.reshape(4, 2, -1, D)
    my_starts = [jnp.where(pc == 0, starts[2 * j], starts[2 * j + 1]) for j in range(4)]
    win_ids = jnp.stack([pc, 2 * px + py] + my_starts).astype(jnp.int32)
    sib_win, sib_small = _sibling_exchange(dW_in_t, starts, win, [small], "grads_sibling_exchange")
    sums = [_pair_sum(dW_in_t, sib_win, win_ids, "grads_pair_sum_in", win_rows=win),
            _pair_sum(small, sib_small, ids, "grads_pair_sum_small")]
    grad_x, dg1_8, dW_forget, *from_chips = _inproj_bwd(dparts, offs, W_in_t, x2d, norm_mix_g, dx2, tm_big, [s[1] for s in sums])
    red_win, red_small, red_up, red_dn = [
        _final_sum(s[0], r, "grads_final_sum_%d" % a)
        for a, (s, r) in enumerate(zip(sums + big_sums, list(from_chips) + list(chips_big)))]

    f_rows = dW_forget[0:8]
    wf_rows = _rows(dwf8[:3].reshape(1, -1), D)[0]
    wc_rows = _rows(dwc8[:3].reshape(1, -1), D)[0]
    smalls = [dg1_8[0:1], dg2_8[0:1], dgf8[0:1], dbg8[0:1, :D], dbg8[0:1, D:], jnp.pad(dbf8[0:1], ((0, 0), (0, D - LANES))),
              jnp.pad(loss8[0:1], ((0, 0), (0, D - LANES))), jnp.zeros((1, D), F32), f_rows, wf_rows, wc_rows]
    spack = jnp.concatenate(smalls, axis=0)
    spack = jnp.pad(spack, ((0, -spack.shape[0] % 8), (0, 0)))
    ssum = _sum8(_all_gather([spack], "small_all_gather")[0], "small_sum")
    g_g1, g_g2, g_gf = ssum[0:1], ssum[1:2], ssum[2]
    g_bg = jnp.concatenate([ssum[3:4], ssum[4:5]], axis=1)
    g_bf = ssum[5:6, :H]
    r0 = 16
    r1 = r0 + wf_rows.shape[0]
    wf_sum = _unrows(ssum[r0:r1], (3, 2 * FH))
    wc_sum = _unrows(ssum[r1:r1 + wc_rows.shape[0]], (3, CW))
    g_wf = lax.dynamic_slice_in_dim(wf_sum, me * conv_ffn_w.shape[-1], conv_ffn_w.shape[-1], axis=1)
    g_wc = lax.dynamic_slice_in_dim(wc_sum, me * conv_mix_w.shape[-1], conv_mix_w.shape[-1], axis=1)

    ext = jnp.concatenate([red_win, ssum[8:8 + H]], axis=0)
    my_index = lax.dynamic_index_in_dim(jnp.asarray(index), me, axis=0, keepdims=False)
    g_w_in_t = jnp.take(ext, my_index, axis=0)
    rb = D // N_DEV
    g_w_o = red_small[:rb]
    g_w_oc = red_small[rb:, :CW].T
    g_w_oa = red_small[rb:, CW:].T

    loss = ssum[6, 0]

    names = ["norm_mix_g", "w_in", "b_f", "b_gate", "conv_mix_w", "w_out_conv", "w_out_attn", "w_o", "norm_ffn_g", "w_up",
             "conv_ffn_w", "w_down", "norm_f_g"]
    weights = [norm_mix_g, w_in, b_f, b_gate, conv_mix_w, w_out_conv, w_out_attn, w_o, norm_ffn_g, w_up, conv_ffn_w, w_down, norm_f_g]
    grads = [g_g1, g_w_in_t, g_bf, g_bg, g_wc, g_w_oc, g_w_oa, g_w_o, g_g2, red_up, g_wf, red_dn, g_gf]
    ms = [m_norm_mix_g, m_w_in, m_b_f, m_b_gate, m_conv_mix_w, m_w_out_conv, m_w_out_attn, m_w_o, m_norm_ffn_g, m_w_up,
          m_conv_ffn_w, m_w_down, m_norm_f_g]
    vs = [v_norm_mix_g, v_w_in, v_b_f, v_b_gate, v_conv_mix_w, v_w_out_conv, v_w_out_attn, v_w_o, v_norm_ffn_g, v_w_up,
          v_conv_ffn_w, v_w_down, v_norm_f_g]
    to_view = {"w_in": lambda a: a[0].T.reshape(-1, LANES), "w_up": lambda a: a[0].T}
    from_view = {"w_in": lambda a: a.reshape(n_loc, D).T[None], "w_up": lambda a: a.T[None]}
    out_grads, steps = [], []
    for nm, w, g, mm, vv in zip(names, weights, grads, ms, vs):
        if nm in to_view:
            wv, mv, vw = (to_view[nm](a) for a in (w, mm, vv))
            gv = g.reshape(wv.shape)
            steps.append(tuple(from_view[nm](o) for o in _adamw(wv, gv, mv, vw, "adamw_" + nm)))
            out_grads.append(from_view[nm](gv))
        else:
            gv = g.reshape(w.shape)
            steps.append(_adamw(w, gv, mm, vv, "adamw_" + nm))
            out_grads.append(gv)
    deltas, new_ms, new_vs = zip(*steps)
    return (loss, grad_x.reshape(Bl, S, D), *out_grads, *deltas, *new_ms, *new_vs)
```

```python
import numpy as np

import jax
import jax.numpy as jnp
from jax import lax
from jax.experimental import pallas as pl
from jax.experimental.pallas import tpu as pltpu

F32, BF16 = jnp.float32, jnp.bfloat16
EPS = 1e-6
ADAM_LR, ADAM_B1, ADAM_B2, ADAM_EPS, ADAM_WD, ADAM_STEP = 0.001, 0.9, 0.999, 1e-08, 0.01, 10
N_DEV = 8
LANES = 128
V7X_VMEM_LIMIT = 56 * 1024 * 1024
FFN_CHUNK = 2816
WGRAD_VMEM_BUDGET = 40 * 1024 * 1024
MESH = pl.DeviceIdType.MESH
SDS = jax.ShapeDtypeStruct
ANY = pl.BlockSpec(memory_space=pl.ANY)


def _tile(n, target, mult=LANES):
    best = None
    for t in range(mult, min(n, target) + 1, mult):
        if n % t == 0:
            best = t
    return best if best is not None else n


def _resident(shape):
    return pl.BlockSpec(shape, lambda *_: (0,) * len(shape), pipeline_mode=pl.Buffered(1))


def _cparams(n_axes=1):
    return pltpu.CompilerParams(dimension_semantics=("arbitrary",) * n_axes, vmem_limit_bytes=V7X_VMEM_LIMIT)


def _dot(a, b):
    return jnp.dot(a, b, preferred_element_type=F32)


def _dot_tn(a, b):
    return lax.dot_general(a, b, (((0,), (0,)), ((), ())), preferred_element_type=F32)


def _dot_nt(a, b):
    return lax.dot_general(a, b, (((1,), (1,)), ((), ())), preferred_element_type=F32)


def _sigmoid(x):
    return 0.5 * jnp.tanh(0.5 * x) + 0.5


def _rms(x):
    return lax.rsqrt(jnp.mean(x * x, axis=-1, keepdims=True) + EPS)


def _taps_back(z, r6, r7):
    row = lax.broadcasted_iota(jnp.int32, (8, 1), 0)
    z1, z2 = pltpu.roll(z, 1, 0), pltpu.roll(z, 2, 0)
    z1 = jnp.concatenate([jnp.where(row == 0, r7, z1[0:8]), z1[8:]], axis=0)
    z2 = jnp.concatenate([jnp.where(row == 0, r6, jnp.where(row == 1, r7, z2[0:8])), z2[8:]], axis=0)
    return z1, z2


def _taps_ahead(d, h0, h1):
    tm = d.shape[0]
    row = lax.broadcasted_iota(jnp.int32, (8, 1), 0)
    d1, d2 = pltpu.roll(d, tm - 1, 0), pltpu.roll(d, tm - 2, 0)
    d1 = jnp.concatenate([d1[:tm - 8], jnp.where(row == 7, h0, d1[tm - 8:])], axis=0)
    d2 = jnp.concatenate([d2[:tm - 8], jnp.where(row == 6, h0, jnp.where(row == 7, h1, d2[tm - 8:]))], axis=0)
    return d1, d2


def _tri_dot(tri, x):
    hi = x.astype(BF16)
    r = x - hi.astype(F32)
    mid = r.astype(BF16)
    lo = (r - mid.astype(F32)).astype(BF16)
    return (_dot(tri, lo) + _dot(tri, mid)) + _dot(tri, hi)


def _lane_pick(block, lane):
    lanes = lax.broadcasted_iota(jnp.int32, (1, block.shape[1]), 1)
    return jnp.sum(jnp.where(lanes == lane, block, 0.0), axis=1, keepdims=True)


def _place():
    return lax.axis_index("x"), lax.axis_index("y"), lax.axis_index("c")


def _all_gather(xs, name):
    n = len(xs)

    def body(*refs):
        start, forward, finish = _gather_phases(refs[:n], refs[n:2 * n], *refs[2 * n:])
        start()
        forward()
        finish()

    return pl.pallas_call(
        body, name=name,
        out_shape=_gather_shapes(xs), in_specs=[ANY] * n, out_specs=[ANY] * n, scratch_shapes=_gather_sems(n),
    )(*xs)


def _gather_shapes(xs):
    return [SDS((N_DEV,) + x.shape, x.dtype) for x in xs]


def _gather_sems(n):
    return [pltpu.SemaphoreType.DMA((7 * n,)), pltpu.SemaphoreType.DMA((7 * n,)), pltpu.SemaphoreType.DMA((n,))]


def _gather_phases(x_refs, out_refs, send_sems, recv_sems, local_sems):
    n = len(x_refs)

    def parts():
        px, py, pc = _place()
        me, sibling = (px, py, pc), (px, py, 1 - pc)
        chips = [(1 - px, py), (px, 1 - py), (1 - px, 1 - py)]

        def slot(a, qx, qy, qc):
            return out_refs[a].at[4 * qx + 2 * qy + qc]

        def copy(a, k, block, to, src=None):
            return pltpu.make_async_remote_copy(
                src_ref=slot(a, *block) if src is None else src, dst_ref=slot(a, *block),
                send_sem=send_sems.at[7 * a + k], recv_sem=recv_sems.at[7 * a + k], device_id=to, device_id_type=MESH)

        def mine():
            return [pltpu.make_async_copy(x_refs[a], slot(a, *me), local_sems.at[a]) for a in range(n)]

        def first():
            out = []
            for a in range(n):
                out.append(copy(a, 0, me, sibling, src=x_refs[a]))
                out += [copy(a, 1 + j, me, (*chip, pc), src=x_refs[a]) for j, chip in enumerate(chips)]
            return out

        def landed():
            return [copy(a, 1 + j, (*chip, pc), me) for j, chip in enumerate(chips) for a in range(n)]

        def passed():
            return [copy(a, 4 + j, (*chip, pc), sibling) for j, chip in enumerate(chips) for a in range(n)]

        def late():
            out = [copy(a, 0, sibling, me) for a in range(n)]
            return out + [copy(a, 4 + j, (*chip, 1 - pc), me) for j, chip in enumerate(chips) for a in range(n)]

        return mine, first, landed, passed, late

    def start():
        mine, first, _, _, _ = parts()
        for cp in mine() + first():
            cp.start()

    def forward():
        _, _, landed, passed, _ = parts()
        for got, cp in zip(landed(), passed()):
            got.wait_recv()
            cp.start()

    def finish():
        mine, first, _, passed, late = parts()
        for cp in late():
            cp.wait_recv()
        for cp in first() + passed():
            cp.wait_send()
        for cp in mine():
            cp.wait()

    return start, forward, finish


def _sibling_exchange(win_buf, win_starts, win_rows, blocked, name):
    nb = len(blocked)

    def body(*refs):
        win_ref, blk_refs = refs[0], refs[1:1 + nb]
        rwin_ref, rblk_refs = refs[1 + nb], refs[2 + nb:2 + 2 * nb]
        send_sems, recv_sems, wsend_sems, wrecv_sems = refs[2 + 2 * nb:]
        px, py, pc = _place()
        copies = _sibling_copies(blk_refs, rblk_refs, send_sems, recv_sems)
        for j in range(4):
            theirs = jnp.where(pc == 0, win_starts[2 * j + 1], win_starts[2 * j])
            copies.append(pltpu.make_async_remote_copy(
                src_ref=win_ref.at[pl.ds(pl.multiple_of(theirs, 8), win_rows)], dst_ref=rwin_ref.at[j],
                send_sem=wsend_sems.at[j], recv_sem=wrecv_sems.at[j], device_id=(px, py, 1 - pc), device_id_type=MESH))
        for cp in copies:
            cp.start()
        for cp in copies:
            cp.wait()

    C = win_buf.shape[1]
    return pl.pallas_call(
        body, name=name,
        out_shape=[SDS((4, win_rows, C), F32)] + _sibling_shapes(blocked),
        in_specs=[ANY] * (1 + nb), out_specs=[ANY] * (1 + nb),
        scratch_shapes=_exchange_sems(nb) + _exchange_sems(4),
    )(win_buf, *blocked)


def _sibling_shapes(blocked):
    return [SDS((4,) + b.shape[2:], F32) for b in blocked]


def _exchange_sems(n):
    return [pltpu.SemaphoreType.DMA((n,)), pltpu.SemaphoreType.DMA((n,))]


def _sibling_copies(blk_refs, out_refs, send_sems, recv_sems):
    px, py, pc = _place()
    return [pltpu.make_async_remote_copy(
        src_ref=b.at[:, 1 - pc], dst_ref=o, send_sem=send_sems.at[a], recv_sem=recv_sems.at[a],
        device_id=(px, py, 1 - pc), device_id_type=MESH) for a, (b, o) in enumerate(zip(blk_refs, out_refs))]


def _chip_shapes(ps):
    return [SDS((3,) + p.shape[1:], p.dtype) for p in ps]


def _chip_copies(p_refs, out_refs, send_sems, recv_sems):
    px, py, pc = _place()
    n = len(p_refs)
    chips = [(1 - px, py), (px, 1 - py), (1 - px, 1 - py)]
    return [pltpu.make_async_remote_copy(
        src_ref=p_refs[a].at[2 * qx + qy], dst_ref=out_refs[a].at[k],
        send_sem=send_sems.at[3 * a + k], recv_sem=recv_sems.at[3 * a + k],
        device_id=(qx, qy, pc), device_id_type=MESH) for k, (qx, qy) in enumerate(chips) for a in range(n)]


def _pair_sum(own, recv, ids, name, win_rows=None):
    _, R, C = recv.shape
    tr = _tile(R, 512, 8)

    def body(ids_ref, g_ref, r_ref, own_ref, pb_ref):
        s = g_ref[...] + r_ref[...]
        pb_ref[...] = s.astype(BF16)

        @pl.when(pl.program_id(1) == ids_ref[1])
        def _():
            own_ref[...] = s

    if win_rows is None:
        own_spec = pl.BlockSpec((None, None, tr, C), lambda r, j, ids: (j, ids[0], r, 0))
    else:
        own_spec = pl.BlockSpec((pl.Element(tr), pl.Element(C)), lambda r, j, ids: (pl.multiple_of(ids[2 + j] + r * tr, 8), 0))
    return pl.pallas_call(
        body, name=name,
        grid_spec=pltpu.PrefetchScalarGridSpec(
            num_scalar_prefetch=1, grid=(R // tr, 4),
            in_specs=[own_spec, pl.BlockSpec((None, tr, C), lambda r, j, ids: (j, r, 0))],
            out_specs=[pl.BlockSpec((tr, C), lambda r, j, ids: (r, 0)),
                       pl.BlockSpec((None, tr, C), lambda r, j, ids: (j, r, 0))]),
        out_shape=[SDS((R, C), F32), SDS((4, R, C), BF16)],
        compiler_params=_cparams(2),
    )(ids, own, recv)


def _final_sum(own, recv, name):
    R, C = own.shape
    tr = _tile(R, 512, 8)

    def body(o_ref, r_ref, out_ref):
        out_ref[...] = ((o_ref[...] + r_ref[0].astype(F32)) + r_ref[1].astype(F32)) + r_ref[2].astype(F32)

    return pl.pallas_call(
        body, name=name, grid=(R // tr,),
        in_specs=[pl.BlockSpec((tr, C), lambda r: (r, 0)), pl.BlockSpec((3, tr, C), lambda r: (0, r, 0))],
        out_specs=pl.BlockSpec((tr, C), lambda r: (r, 0)),
        out_shape=SDS((R, C), F32),
        compiler_params=_cparams(1),
    )(own, recv)


def _sum8(a, name):
    def body(a_ref, out_ref):
        s = a_ref[0]
        for d in range(1, N_DEV):
            s = s + a_ref[d]
        out_ref[...] = s

    return pl.pallas_call(body, name=name, out_shape=SDS(a.shape[1:], F32))(a)


def _inproj_fwd(x, g1, w_t, NM, tm):
    T, D = x.shape
    NF = w_t.shape[0] - NM
    ch = _tile(NM, 1024)

    def body(x_ref, g_ref, w_ref, pm_ref, fl_ref, h_ref):
        xv = x_ref[...]
        h = (xv * _rms(xv) * g_ref[...]).astype(BF16)
        h_ref[...] = h
        for c in range(0, NM, ch):
            pm_ref[:, c:c + ch] = _dot_nt(h, w_ref[c:c + ch, :]).astype(BF16)
        fl_ref[...] = _dot_nt(h, w_ref[NM:NM + NF, :])

    return pl.pallas_call(
        body, name="inproj_fwd", grid=(T // tm,),
        in_specs=[pl.BlockSpec((tm, D), lambda i: (i, 0)), _resident((1, D)), _resident(w_t.shape)],
        out_specs=[pl.BlockSpec((tm, NM), lambda i: (i, 0)), pl.BlockSpec((tm, NF), lambda i: (i, 0)),
                   pl.BlockSpec((tm, D), lambda i: (i, 0))],
        out_shape=[SDS((T, NM), BF16), SDS((T, NF), F32), SDS((T, D), BF16)],
        compiler_params=_cparams(1),
    )(x, g1, w_t)


def _log_sigmoid(x):
    return jnp.minimum(x, 0.0) - jnp.log(1.0 + jnp.exp(-jnp.abs(x)))


def _forget_cumsum(fl, bf, S):
    T, NF = fl.shape
    ch = _tile(S, 256, 8)

    def body(fl_ref, bf_ref, f_ref):
        row = lax.broadcasted_iota(jnp.int32, (ch, ch), 0)
        col = lax.broadcasted_iota(jnp.int32, (ch, ch), 1)
        tri = (col <= row).astype(BF16)
        carry = jnp.zeros((1, NF), F32)
        for c in range(0, S, ch):
            lf = _log_sigmoid(fl_ref[c:c + ch, :] + bf_ref[...])
            f_ref[c:c + ch, :] = _tri_dot(tri, lf) + carry
            carry = carry + jnp.sum(lf, axis=0, keepdims=True)

    return pl.pallas_call(
        body, name="forget_cumsum", grid=(T // S,),
        in_specs=[pl.BlockSpec((S, NF), lambda b: (b, 0)), _resident((1, NF))],
        out_specs=pl.BlockSpec((S, NF), lambda b: (b, 0)),
        out_shape=SDS((T, NF), F32),
        compiler_params=_cparams(1),
    )(fl, bf)


def _head_mask(e, hd):
    lanes = lax.broadcasted_iota(jnp.int32, (1, LANES), 1)
    return (lanes >= e * hd) & (lanes < (e + 1) * hd)


def _attn_specs(S, q_off, AW):
    ng = AW // LANES
    return [pl.BlockSpec((S, LANES), lambda b, g, o=q_off + w * ng: (b, o + g)) for w in range(3)]


def _attn_fwd(pm, fcum, frow, S, q_off, AW, hd, tq, gather):
    T = pm.shape[0]
    scale = float(hd) ** -0.5
    nq, hpg = S // tq, LANES // hd
    ng, nx = AW // LANES, len(gather)
    steps = (T // S) * ng

    def body(q_ref, k_ref, v_ref, fc_ref, fr_ref, *rest):
        x_refs, (o_ref, lse_ref), out_refs, sems = rest[:nx], rest[nx:nx + 2], rest[nx + 2:2 * nx + 2], rest[2 * nx + 2:]
        g = pl.program_id(1)
        step = pl.program_id(0) * ng + g
        start, forward, finish = _gather_phases(x_refs, out_refs, *sems)
        pl.when(step == 0)(start)

        @pl.when(g == 0)
        def _():
            lse_ref[...] = jnp.zeros_like(lse_ref)

        lanes = lax.broadcasted_iota(jnp.int32, (1, LANES), 1)
        for i in range(nq):
            rs, kend = slice(i * tq, (i + 1) * tq), (i + 1) * tq
            row = i * tq + lax.broadcasted_iota(jnp.int32, (tq, kend), 0)
            col = lax.broadcasted_iota(jnp.int32, (tq, kend), 1)
            o_tile = jnp.zeros((tq, LANES), F32)
            lse_tile = lse_ref[rs, :]
            for e in range(hpg):
                mask = _head_mask(e, hd)
                qs = jnp.where(mask, q_ref[rs, :], 0) * scale
                s = _dot_nt(qs, k_ref[0:kend, :]) + _lane_pick(fc_ref[rs, :], g * hpg + e) - fr_ref[e:e + 1, 0:kend]
                s = jnp.where(col <= row, s, -1e30)
                m = jnp.max(s, axis=1, keepdims=True)
                p = jnp.exp(s - m)
                l = jnp.sum(p, axis=1, keepdims=True)
                o_tile = jnp.where(mask, _dot(p.astype(BF16), v_ref[0:kend, :]) / l, o_tile)
                lse_tile = jnp.where(lanes == g * hpg + e, m + jnp.log(l), lse_tile)
            o_ref[rs, :] = o_tile.astype(BF16)
            lse_ref[rs, :] = lse_tile
        pl.when(step == (steps * 5) // 8)(forward)
        pl.when(step == steps - 1)(finish)

    full = pl.BlockSpec((S, LANES), lambda b, g: (b, 0))
    return pl.pallas_call(
        body, name="attn_fwd", grid=(T // S, ng),
        in_specs=_attn_specs(S, q_off, AW) + [full, pl.BlockSpec((None, None, 8, S), lambda b, g: (b, g, 0, 0))] + [ANY] * nx,
        out_specs=[pl.BlockSpec((S, LANES), lambda b, g: (b, g)), full] + [ANY] * nx,
        out_shape=[SDS((T, AW), BF16), SDS((T, LANES), F32)] + _gather_shapes(gather),
        scratch_shapes=_gather_sems(nx),
        compiler_params=_cparams(2),
    )(pm, pm, pm, fcum, frow, *gather)


def _merge_fwd(pm, o, x, wc, bg, woc_t, woa_t, wo, S, tm):
    T, D = x.shape
    CW, AW = woc_t.shape[1], woa_t.shape[1]
    tps = S // tm

    def body(cb_ref, cc_ref, cin_ref, gc_ref, ga_ref, o_ref, x_ref, wc_ref, bg_ref, woct_ref, woat_ref, wo_ref,
             x2_ref, u_ref, m_ref, ycin_ref, yc_ref, ya_ref, tail_ref):
        @pl.when(pl.program_id(0) % tps == 0)
        def _():
            tail_ref[...] = jnp.zeros_like(tail_ref)

        z = cc_ref[...].astype(F32) * cin_ref[...].astype(F32)
        z1, z2 = _taps_back(z, tail_ref[6:7, :], tail_ref[7:8, :])
        tail_ref[...] = z[tm - 8:tm, :]
        u = (z * wc_ref[2:3, :] + z2 * wc_ref[0:1, :]) + z1 * wc_ref[1:2, :]
        u_ref[...] = u.astype(BF16)
        ycin = (cb_ref[...].astype(F32) * u).astype(BF16)
        ycin_ref[...] = ycin
        yc = _dot_nt(ycin, woct_ref[...])
        ya = _dot_nt(o_ref[...], woat_ref[...])
        yc_ref[...] = yc.astype(BF16)
        ya_ref[...] = ya.astype(BF16)
        gc = _sigmoid(gc_ref[...].astype(F32) + bg_ref[:, 0:D])
        ga = _sigmoid(ga_ref[...].astype(F32) + bg_ref[:, D:2 * D])
        m = (gc * yc + ga * ya).astype(BF16)
        m_ref[...] = m
        x2_ref[...] = x_ref[...] + _dot(m, wo_ref[...])

    g_off = (3 * CW + 3 * AW) // D
    tok = lambda w, j=0: pl.BlockSpec((tm, w), lambda i: (i, j))
    return pl.pallas_call(
        body, name="merge_fwd", grid=(T // tm,),
        in_specs=[tok(CW, 0), tok(CW, 1), tok(CW, 2), tok(D, g_off), tok(D, g_off + 1), tok(AW), tok(D),
                  _resident((8, CW)), _resident((1, 2 * D)), _resident((D, CW)), _resident((D, AW)), _resident((D, D))],
        out_specs=[tok(D), tok(CW), tok(D), tok(CW), tok(D), tok(D)],
        out_shape=[SDS((T, D), F32), SDS((T, CW), BF16), SDS((T, D), BF16), SDS((T, CW), BF16),
                   SDS((T, D), BF16), SDS((T, D), BF16)],
        scratch_shapes=[pltpu.VMEM((8, CW), F32)],
        compiler_params=_cparams(1),
    )(pm, pm, pm, pm, pm, o, x, wc, bg, woc_t, woa_t, wo)


def _ffn_fwd(x2, g2, wup_t, wf, wdn, gf, tgt, S, tm):
    T, D = x2.shape
    FH = wdn.shape[0]
    ch = _tile(FH, FFN_CHUNK)
    tps = S // tm

    def body(x2_ref, g2_ref, wupt_ref, wf_ref, wdn_ref, gf_ref, tgt_ref,
             upp_ref, up_ref, act_ref, h2_ref, dx3_ref, loss_ref, dgf_ref, tail_ref):
        i = pl.program_id(0)

        @pl.when(i % tps == 0)
        def _():
            tail_ref[...] = jnp.zeros_like(tail_ref)

        @pl.when(i == 0)
        def _():
            loss_ref[...] = jnp.zeros_like(loss_ref)
            dgf_ref[...] = jnp.zeros_like(dgf_ref)

        x2v = x2_ref[...]
        h2 = (x2v * _rms(x2v) * g2_ref[...]).astype(BF16)
        h2_ref[...] = h2
        x3 = x2v
        for c in range(0, FH, ch):
            gated = []
            for cols in (slice(c, c + ch), slice(FH + c, FH + c + ch)):
                upp = _dot_nt(h2, wupt_ref[cols, :])
                upp_ref[:, cols] = upp.astype(BF16)
                p1, p2 = _taps_back(upp, tail_ref[6:7, cols], tail_ref[7:8, cols])
                tail_ref[:, cols] = upp[tm - 8:tm, :]
                up = (upp * wf_ref[2:3, cols] + p2 * wf_ref[0:1, cols]) + p1 * wf_ref[1:2, cols]
                up_ref[:, cols] = up.astype(BF16)
                gated.append(up)
            a, b = gated
            act = (a * _sigmoid(a) * b).astype(BF16)
            act_ref[:, c:c + ch] = act
            x3 = x3 + _dot(act, wdn_ref[c:c + ch, :])
        r3 = _rms(x3)
        xn3 = x3 * r3
        e = xn3 * gf_ref[...] - tgt_ref[...]
        loss_ref[...] += 0.5 * jnp.sum(jnp.mean(e * e, axis=-1, keepdims=True), axis=0, keepdims=True)
        dy = e / D
        dgf_ref[0:1, :] += jnp.sum(dy * xn3, axis=0, keepdims=True)
        dxn = dy * gf_ref[...]
        dx3_ref[...] = r3 * (dxn - xn3 * jnp.mean(dxn * xn3, axis=-1, keepdims=True))

    tok = lambda w: pl.BlockSpec((tm, w), lambda i: (i, 0))
    return pl.pallas_call(
        body, name="ffn_fwd", grid=(T // tm,),
        in_specs=[tok(D), _resident((1, D)), _resident((2 * FH, D)), _resident((8, 2 * FH)), _resident((FH, D)),
                  _resident((1, D)), tok(D)],
        out_specs=[tok(2 * FH), tok(2 * FH), tok(FH), tok(D), tok(D), pl.BlockSpec((8, LANES), lambda i: (0, 0)),
                   pl.BlockSpec((8, D), lambda i: (0, 0))],
        out_shape=[SDS((T, 2 * FH), BF16), SDS((T, 2 * FH), BF16), SDS((T, FH), BF16), SDS((T, D), BF16), SDS((T, D), F32),
                   SDS((8, LANES), F32), SDS((8, D), F32)],
        scratch_shapes=[pltpu.VMEM((8, 2 * FH), F32)],
        compiler_params=_cparams(1),
    )(x2, g2, wup_t, wf, wdn, gf, tgt)


def _ffn_bwd(dx3, x2, g2, upp, up, wf, wdn, wup_t, S, tm):
    T, D = x2.shape
    FH = wdn.shape[0]
    ch = _tile(FH, FFN_CHUNK)
    n, tps = T // tm, S // tm

    def body(dx3_ref, x2_ref, g2_ref, upp_ref, up_ref, wf_ref, wdn_ref, wupt_ref,
             dx2_ref, dupp_ref, dwf_ref, dg2_ref, head_ref):
        i = pl.program_id(0)
        t = n - 1 - i

        @pl.when(i == 0)
        def _():
            dwf_ref[...] = jnp.zeros_like(dwf_ref)
            dg2_ref[...] = jnp.zeros_like(dg2_ref)

        @pl.when(t % tps == tps - 1)
        def _():
            head_ref[...] = jnp.zeros_like(head_ref)

        dx3v = dx3_ref[...]
        dx3b = dx3v.astype(BF16)
        x2v = x2_ref[...]
        r2 = _rms(x2v)
        xn2 = x2v * r2
        dh2 = jnp.zeros((tm, D), F32)
        for c in range(0, FH, ch):
            ca, cb = slice(c, c + ch), slice(FH + c, FH + c + ch)
            a, b = up_ref[:, ca].astype(F32), up_ref[:, cb].astype(F32)
            sig = _sigmoid(a)
            sl = a * sig
            dact = _dot_nt(dx3b, wdn_ref[ca, :])
            grads = (dact * b * (sig * (1.0 + a * (1.0 - sig))), dact * sl)
            for cols, d in zip((ca, cb), grads):
                u0 = upp_ref[:, cols].astype(F32)
                d1, d2 = _taps_ahead(d, head_ref[0:1, cols], head_ref[1:2, cols])
                head_ref[:, cols] = d[0:8, :]
                dwf_ref[2:3, cols] += jnp.sum(u0 * d, axis=0, keepdims=True)
                dwf_ref[1:2, cols] += jnp.sum(u0 * d1, axis=0, keepdims=True)
                dwf_ref[0:1, cols] += jnp.sum(u0 * d2, axis=0, keepdims=True)
                dpre = ((d * wf_ref[2:3, cols] + d1 * wf_ref[1:2, cols]) + d2 * wf_ref[0:1, cols]).astype(BF16)
                dupp_ref[:, cols] = dpre
                dh2 = dh2 + _dot(dpre, wupt_ref[cols, :])
        dg2_ref[0:1, :] += jnp.sum(dh2 * xn2, axis=0, keepdims=True)
        dxn = dh2 * g2_ref[...]
        dx2_ref[...] = dx3v + r2 * (dxn - xn2 * jnp.mean(dxn * xn2, axis=-1, keepdims=True))

    tok = lambda w: pl.BlockSpec((tm, w), lambda i: (n - 1 - i, 0))
    acc = lambda w: pl.BlockSpec((8, w), lambda i: (0, 0))
    return pl.pallas_call(
        body, name="ffn_bwd", grid=(n,),
        in_specs=[tok(D), tok(D), _resident((1, D)), tok(2 * FH), tok(2 * FH), _resident((8, 2 * FH)),
                  _resident((FH, D)), _resident((2 * FH, D))],
        out_specs=[tok(D), tok(2 * FH), acc(2 * FH), acc(D)],
        out_shape=[SDS((T, D), F32), SDS((T, 2 * FH), BF16), SDS((8, 2 * FH), F32), SDS((8, D), F32)],
        scratch_shapes=[pltpu.VMEM((8, 2 * FH), F32)],
        compiler_params=_cparams(1),
    )(dx3, x2, g2, upp, up, wf, wdn, wup_t)


def _merge_bwd(dx2, pm, u, yc, ya, m, ycin, o, wc, bg, wo, woc_t, woa_t, S, tm, exchange):
    T, D = dx2.shape
    CW, AW = woc_t.shape[1], woa_t.shape[1]
    n, tps = T // tm, S // tm
    nx = len(exchange)
    rb = D // N_DEV
    assert CW + AW == D

    def body(dx2_ref, cb_ref, cc_ref, cin_ref, gc_ref, ga_ref, u_ref, yc_ref, ya_ref, m_ref, ycin_ref, o_ref,
             wc_ref, bg_ref, wo_ref, woct_ref, woat_ref, *rest):
        x_refs = rest[:nx]
        dconv_ref, dgl_ref, do_ref, dws_ref, dwc_ref, dbg_ref = rest[nx:nx + 6]
        got_refs, head_ref, sems = rest[nx + 6:2 * nx + 6], rest[2 * nx + 6], rest[2 * nx + 7:]

        def add_blocks(rows, cols, grad):
            for d in range(N_DEV):
                dws_ref[d, rows, cols] += grad[d * rb:(d + 1) * rb, :]

        i = pl.program_id(0)
        t = n - 1 - i

        @pl.when(i == 0)
        def _():
            for cp in _sibling_copies(x_refs, got_refs, *sems):
                cp.start()

        @pl.when(i == 0)
        def _():
            for ref in (dwc_ref, dbg_ref, dws_ref):
                ref[...] = jnp.zeros_like(ref)

        @pl.when(t % tps == tps - 1)
        def _():
            head_ref[...] = jnp.zeros_like(head_ref)

        dx2b = dx2_ref[...].astype(BF16)
        add_blocks(slice(0, rb), slice(0, D), _dot_tn(m_ref[...], dx2b))
        dm = _dot_nt(dx2b, wo_ref[...])
        outs = []
        for g_ref, y_ref, cols in ((gc_ref, yc_ref, slice(0, D)), (ga_ref, ya_ref, slice(D, 2 * D))):
            g = _sigmoid(g_ref[...].astype(F32) + bg_ref[:, cols])
            dgl = dm * y_ref[...].astype(F32) * g * (1.0 - g)
            dgl_ref[:, cols] = dgl.astype(BF16)
            dbg_ref[0:1, cols] += jnp.sum(dgl, axis=0, keepdims=True)
            outs.append((dm * g).astype(BF16))
        dyc, dya = outs
        add_blocks(slice(rb, 2 * rb), slice(0, CW), _dot_tn(dyc, ycin_ref[...]))
        add_blocks(slice(rb, 2 * rb), slice(CW, CW + AW), _dot_tn(dya, o_ref[...]))
        do_ref[...] = _dot(dya, woat_ref[...]).astype(BF16)
        dycin = _dot(dyc, woct_ref[...])
        cc, cin = cc_ref[...].astype(F32), cin_ref[...].astype(F32)
        z = cc * cin
        du = dycin * cb_ref[...].astype(F32)
        du1, du2 = _taps_ahead(du, head_ref[0:1, :], head_ref[1:2, :])
        head_ref[...] = du[0:8, :]
        dwc_ref[2:3, :] += jnp.sum(z * du, axis=0, keepdims=True)
        dwc_ref[1:2, :] += jnp.sum(z * du1, axis=0, keepdims=True)
        dwc_ref[0:1, :] += jnp.sum(z * du2, axis=0, keepdims=True)
        dz = (du * wc_ref[2:3, :] + du1 * wc_ref[1:2, :]) + du2 * wc_ref[0:1, :]
        dconv_ref[:, 0:CW] = (dycin * u_ref[...].astype(F32)).astype(BF16)
        dconv_ref[:, CW:2 * CW] = (dz * cin).astype(BF16)
        dconv_ref[:, 2 * CW:3 * CW] = (dz * cc).astype(BF16)

        @pl.when(i == n - 1)
        def _():
            for cp in _sibling_copies(x_refs, got_refs, *sems):
                cp.wait()

    g_off = (3 * CW + 3 * AW) // D
    tok = lambda w, j=0: pl.BlockSpec((tm, w), lambda i: (n - 1 - i, j))
    acc = lambda w: pl.BlockSpec((8, w), lambda i: (0, 0))
    return pl.pallas_call(
        body, name="merge_bwd", grid=(n,),
        in_specs=[tok(D), tok(CW, 0), tok(CW, 1), tok(CW, 2), tok(D, g_off), tok(D, g_off + 1), tok(CW), tok(D), tok(D),
                  tok(D), tok(CW), tok(AW),
                  _resident((8, CW)), _resident((1, 2 * D)), _resident((D, D)), _resident((D, CW)), _resident((D, AW))]
                 + [ANY] * nx,
        out_specs=[tok(3 * CW), tok(2 * D), tok(AW), pl.BlockSpec((N_DEV, 2 * rb, D), lambda i: (0, 0, 0)), acc(CW),
                   acc(2 * D)] + [ANY] * nx,
        out_shape=[SDS((T, 3 * CW), BF16), SDS((T, 2 * D), BF16), SDS((T, AW), BF16), SDS((N_DEV, 2 * rb, D), F32),
                   SDS((8, CW), F32), SDS((8, 2 * D), F32)] + _sibling_shapes(exchange),
        scratch_shapes=[pltpu.VMEM((8, CW), F32)] + _exchange_sems(nx),
        compiler_params=_cparams(1),
    )(dx2, pm, pm, pm, pm, pm, u, yc, ya, m, ycin, o, wc, bg, wo, woc_t, woa_t, *exchange)


def _attn_bwd(pm, do, fcum, frow, lse, S, q_off, AW, hd, tq, exchange):
    T = pm.shape[0]
    scale = float(hd) ** -0.5
    nq, hpg, ng = S // tq, LANES // hd, AW // LANES
    nx = len(exchange)
    steps = (T // S) * ng

    def body(q_ref, k_ref, v_ref, do_ref, fc_ref, fr_ref, lse_ref, *rest):
        x_refs, (dq_ref, dk_ref, dv_ref, dfr_ref) = rest[:nx], rest[nx:nx + 4]
        got_refs, (dk_acc, dv_acc), sems = rest[nx + 4:2 * nx + 4], rest[2 * nx + 4:2 * nx + 6], rest[2 * nx + 6:]
        g = pl.program_id(1)
        step = pl.program_id(0) * ng + g

        @pl.when(step == 0)
        def _():
            for cp in _chip_copies(x_refs, got_refs, *sems):
                cp.start()

        dk_acc[...] = jnp.zeros_like(dk_acc)
        dv_acc[...] = jnp.zeros_like(dv_acc)
        dfr_ref[...] = jnp.zeros_like(dfr_ref)
        for i in range(nq):
            rs, kend = slice(i * tq, (i + 1) * tq), (i + 1) * tq
            row = i * tq + lax.broadcasted_iota(jnp.int32, (tq, kend), 0)
            col = lax.broadcasted_iota(jnp.int32, (tq, kend), 1)
            kk, vv = k_ref[0:kend, :], v_ref[0:kend, :]
            dq_tile = jnp.zeros((tq, LANES), F32)
            for e in range(hpg):
                mask = _head_mask(e, hd)
                qs = jnp.where(mask, q_ref[rs, :], 0) * scale
                doi = jnp.where(mask, do_ref[rs, :], 0)
                s = _dot_nt(qs, kk) + _lane_pick(fc_ref[rs, :], g * hpg + e) - fr_ref[e:e + 1, 0:kend]
                p = jnp.where(col <= row, jnp.exp(s - _lane_pick(lse_ref[rs, :], g * hpg + e)), 0.0)
                dp = _dot_nt(doi, vv)
                ds = p * (dp - jnp.sum(p * dp, axis=1, keepdims=True))
                pb, dsb = p.astype(BF16), ds.astype(BF16)
                dq_tile = jnp.where(mask, _dot(dsb, kk) * scale, dq_tile)
                dv_acc[0:kend, :] += _dot_tn(pb, doi)
                dk_acc[0:kend, :] += _dot_tn(dsb, qs)
                dfr_ref[e:e + 1, 0:kend] -= jnp.sum(ds, axis=0, keepdims=True)
            dq_ref[rs, :] = dq_tile.astype(BF16)
        dk_ref[...] = dk_acc[...].astype(BF16)
        dv_ref[...] = dv_acc[...].astype(BF16)

        @pl.when(step == steps - 1)
        def _():
            for cp in _chip_copies(x_refs, got_refs, *sems):
                cp.wait()

    full = pl.BlockSpec((S, LANES), lambda b, g: (b, 0))
    grp = pl.BlockSpec((S, LANES), lambda b, g: (b, g))
    rows = pl.BlockSpec((None, None, 8, S), lambda b, g: (b, g, 0, 0))
    return pl.pallas_call(
        body, name="attn_bwd", grid=(T // S, ng),
        in_specs=_attn_specs(S, q_off, AW) + [grp, full, rows, full] + [ANY] * nx,
        out_specs=[grp, grp, grp, rows] + [ANY] * nx,
        out_shape=[SDS((T, AW), BF16), SDS((T, AW), BF16), SDS((T, AW), BF16), SDS((T // S, ng, 8, S), F32)]
                  + _chip_shapes(exchange),
        scratch_shapes=[pltpu.VMEM((S, LANES), F32), pltpu.VMEM((S, LANES), F32)] + _exchange_sems(3 * nx),
        compiler_params=_cparams(2),
    )(pm, pm, pm, do, fcum, frow, lse, *exchange)


def _forget_bwd(dfc, fl, bf, S):
    T, NF = fl.shape
    ch = _tile(S, 256, 8)

    def body(df_ref, fl_ref, bf_ref, dfl_ref, dbf_ref):
        @pl.when(pl.program_id(0) == 0)
        def _():
            dbf_ref[...] = jnp.zeros_like(dbf_ref)

        row = lax.broadcasted_iota(jnp.int32, (ch, ch), 0)
        col = lax.broadcasted_iota(jnp.int32, (ch, ch), 1)
        tri = (col >= row).astype(BF16)
        carry = jnp.zeros((1, NF), F32)
        for c in range(S - ch, -1, -ch):
            d = df_ref[c:c + ch, :]
            dlf = _tri_dot(tri, d) + carry
            carry = carry + jnp.sum(d, axis=0, keepdims=True)
            dfl = dlf * _sigmoid(-(fl_ref[c:c + ch, :] + bf_ref[...]))
            dfl_ref[c:c + ch, :] = dfl.astype(BF16)
            dbf_ref[0:1, :] += jnp.sum(dfl, axis=0, keepdims=True)

    return pl.pallas_call(
        body, name="forget_bwd", grid=(T // S,),
        in_specs=[pl.BlockSpec((S, NF), lambda b: (b, 0)), pl.BlockSpec((S, NF), lambda b: (b, 0)), _resident((1, NF))],
        out_specs=[pl.BlockSpec((S, NF), lambda b: (b, 0)), pl.BlockSpec((8, NF), lambda b: (0, 0))],
        out_shape=[SDS((T, NF), BF16), SDS((8, NF), F32)],
        compiler_params=_cparams(1),
    )(dfc, fl, bf)


def _inproj_bwd(dparts, offs, w_t, x, g1, dx2, tm, exchange):
    T, D = x.shape
    npart, nx = len(dparts), len(exchange)
    n = T // tm

    def body(*refs):
        d_refs = refs[:npart]
        w_ref, x_ref, g_ref, dx2_ref = refs[npart:npart + 4]
        x_refs = refs[npart + 4:npart + 4 + nx]
        dx_ref, dg_ref, dwl_ref = refs[npart + 4 + nx:npart + 7 + nx]
        got_refs, sems = refs[npart + 7 + nx:npart + 7 + 2 * nx], refs[npart + 7 + 2 * nx:]

        @pl.when(pl.program_id(0) == 0)
        def _():
            for cp in _chip_copies(x_refs, got_refs, *sems):
                cp.start()

        @pl.when(pl.program_id(0) == 0)
        def _():
            dg_ref[...] = jnp.zeros_like(dg_ref)
            dwl_ref[...] = jnp.zeros_like(dwl_ref)

        dh = None
        for d_ref, off in zip(d_refs, offs):
            term = _dot(d_ref[...], w_ref[off:off + d_ref.shape[1], :])
            dh = term if dh is None else dh + term
        xv = x_ref[...]
        r = _rms(xv)
        xn = xv * r
        dg_ref[0:1, :] += jnp.sum(dh * xn, axis=0, keepdims=True)
        dwl_ref[...] += _dot_tn(d_refs[-1][...], (xn * g_ref[...]).astype(BF16))
        dxn = dh * g_ref[...]
        dx_ref[...] = dx2_ref[...] + r * (dxn - xn * jnp.mean(dxn * xn, axis=-1, keepdims=True))

        @pl.when(pl.program_id(0) == n - 1)
        def _():
            for cp in _chip_copies(x_refs, got_refs, *sems):
                cp.wait()

    tok = lambda w: pl.BlockSpec((tm, w), lambda i: (i, 0))
    nl = dparts[-1].shape[1]
    return pl.pallas_call(
        body, name="inproj_bwd", grid=(n,),
        in_specs=[tok(d.shape[1]) for d in dparts] + [_resident(w_t.shape), tok(D), _resident((1, D)), tok(D)] + [ANY] * nx,
        out_specs=[tok(D), pl.BlockSpec((8, D), lambda i: (0, 0)), pl.BlockSpec((nl, D), lambda i: (0, 0))] + [ANY] * nx,
        out_shape=[SDS((T, D), F32), SDS((8, D), F32), SDS((nl, D), F32)] + _chip_shapes(exchange),
        scratch_shapes=_exchange_sems(3 * nx),
        compiler_params=_cparams(1),
    )(*dparts, w_t, x, g1, dx2, *exchange)


def _wgrad(bs, a, name, into=None, row_off=0, total_rows=None):
    bs = list(bs) if isinstance(bs, (list, tuple)) else [bs]
    P = len(bs)
    T, N = bs[0].shape
    M = a.shape[1]
    tn = _tile(N, 1408 if M <= 1024 else 512)
    while row_off % tn:
        tn = _tile(N, tn - LANES)
    tk = _tile(T, 512, 16)
    per_row = 2 * (P * tn * bs[0].dtype.itemsize + M * a.dtype.itemsize)
    while T % (2 * tk) == 0 and 2 * tk * per_row + 2 * tn * M * 4 <= WGRAD_VMEM_BUDGET:
        tk *= 2
    blk0, nj, nk = row_off // tn, N // tn, T // tk

    def body(*refs):
        b_refs, a_ref, o_ref = refs[:P], refs[P], refs[-1]
        j = pl.program_id(0)

        @pl.when(pl.program_id(1) == 0)
        def _():
            o_ref[...] = jnp.zeros_like(o_ref)

        for p, b_ref in enumerate(b_refs):
            @pl.when((j >= p * nj) & (j < (p + 1) * nj))
            def _(b_ref=b_ref):
                o_ref[...] += _dot_tn(b_ref[...].astype(BF16), a_ref[...].astype(BF16))

    def b_spec(p):
        def index(j, k):
            before, mine = j < p * nj, (j >= p * nj) & (j < (p + 1) * nj)
            return (jnp.where(mine, k, jnp.where(before, 0, nk - 1)),
                    jnp.where(mine, j - p * nj, jnp.where(before, 0, nj - 1)))
        return pl.BlockSpec((tk, tn), index)

    in_specs = [b_spec(p) for p in range(P)] + [pl.BlockSpec((tk, M), lambda j, k: (k, 0))]
    args = (*bs, a)
    kwargs = {}
    if into is not None:
        in_specs.append(ANY)
        args += (into,)
        kwargs["input_output_aliases"] = {P + 1: 0}
        total_rows = into.shape[0]
    return pl.pallas_call(
        body, name=name, grid=(P * nj, nk),
        in_specs=in_specs,
        out_specs=pl.BlockSpec((tn, M), lambda j, k: (blk0 + j, 0)),
        out_shape=SDS((P * N if total_rows is None else total_rows, M), F32),
        compiler_params=_cparams(2), **kwargs,
    )(*args)


def _adamw(w, g, m, v, name, recv=None):
    shape = w.shape
    C = shape[-1]
    w2, g2, m2, v2 = (a.reshape(-1, C) for a in (w, g, m, v))
    R = w2.shape[0]
    tr = R if R <= 512 else _tile(R, 256, 8)
    if tr < 64:
        tr = R

    def body(w_ref, g_ref, m_ref, v_ref, *rest):
        d_ref, nm_ref, nv_ref = rest[-3:] if recv is None else rest[1:4]
        gv = g_ref[...]
        if recv is not None:
            r_ref, gs_ref = rest[0], rest[4]
            gv = ((gv + r_ref[0].astype(F32)) + r_ref[1].astype(F32)) + r_ref[2].astype(F32)
            gs_ref[...] = gv
        mv = ADAM_B1 * m_ref[...] + (1.0 - ADAM_B1) * gv
        vv = ADAM_B2 * v_ref[...] + (1.0 - ADAM_B2) * (gv * gv)
        m_hat = mv / (1.0 - ADAM_B1 ** ADAM_STEP)
        v_hat = vv / (1.0 - ADAM_B2 ** ADAM_STEP)
        d_ref[...] = -ADAM_LR * (m_hat / (jnp.sqrt(v_hat) + ADAM_EPS) + ADAM_WD * w_ref[...])
        nm_ref[...] = mv
        nv_ref[...] = vv

    spec = pl.BlockSpec((tr, C), lambda r: (r, 0))
    extra = [] if recv is None else [pl.BlockSpec((3, tr, C), lambda r: (0, r, 0))]
    n_out = 3 if recv is None else 4
    outs = pl.pallas_call(
        body, name=name, grid=(R // tr,),
        in_specs=[spec] * 4 + extra, out_specs=[spec] * n_out, out_shape=[SDS((R, C), F32)] * n_out,
        compiler_params=_cparams(1),
    )(w2, g2, m2, v2, *([] if recv is None else [recv]))
    return tuple(o.reshape(shape) for o in outs)


def _rows(a, L):
    lead = a.shape[0]
    flat = a.reshape(lead, -1)
    n = flat.shape[1]
    r = -(-n // L)
    return jnp.pad(flat, ((0, 0), (0, r * L - n))).reshape(lead, r, L)


def _unrows(p, shape):
    return p.reshape(-1)[:int(np.prod(shape))].reshape(shape)


def _from_col_blocks(a):
    n, R, c = a.shape
    return a.transpose(1, 0, 2).reshape(R, n * c)


def _in_windows(n_loc, nqkv, H, NM):
    win = (n_loc + 7 + 7) // 8 * 8
    starts, index = [], np.zeros((N_DEV, n_loc), np.int32)
    for d in range(N_DEV):
        rows = np.arange(n_loc * d, n_loc * (d + 1))
        is_f = (rows >= nqkv) & (rows < nqkv + H)
        kept = np.where(rows < nqkv, rows, rows - H)
        lo = int(kept[~is_f].min())
        start = lo // 8 * 8
        assert int(kept[~is_f].max()) - start < win and start + win <= NM + LANES
        starts.append(start)
        index[d] = np.where(is_f, win + rows - nqkv, kept - start)
    return starts, win, index


def kernel(x, norm_mix_g, w_in, b_f, b_gate, conv_mix_w, w_out_conv, w_out_attn, w_o, norm_ffn_g, w_up, conv_ffn_w, w_down, norm_f_g, loss_target, m_norm_mix_g, m_w_in, m_b_f, m_b_gate, m_conv_mix_w, m_w_out_conv, m_w_out_attn, m_w_o, m_norm_ffn_g, m_w_up, m_conv_ffn_w, m_w_down, m_norm_f_g, v_norm_mix_g, v_w_in, v_b_f, v_b_gate, v_conv_mix_w, v_w_out_conv, v_w_out_attn, v_w_o, v_norm_ffn_g, v_w_up, v_conv_ffn_w, v_w_down, v_norm_f_g):
    Bl, S, D = x.shape
    T = Bl * S
    H = b_f.shape[-1]
    CW = N_DEV * conv_mix_w.shape[-1]
    AW = w_out_attn.shape[1]
    hd = AW // H
    FH = N_DEV * w_down.shape[1]
    n_loc = w_in.shape[-1]
    NIN = N_DEV * n_loc
    NM = 3 * CW + 3 * AW + 2 * D
    nqkv = 3 * CW + 3 * AW
    assert NIN == NM + H and w_out_conv.shape[1] == CW and nqkv % D == 0 and AW % LANES == 0 and LANES % hd == 0
    hpg, ng = LANES // hd, AW // LANES
    assert hpg <= 8
    tm_big = min(512, S // 2)
    tm_ffn = min(256, S // 2)
    tq_fwd = min(512, S // 2)
    tq_bwd = min(256, S // 2)
    px, py, pc = _place()
    me = 4 * px + 2 * py + pc

    bits = lambda a: lax.bitcast_convert_type(a, BF16)
    taps = jnp.concatenate([_rows(bits(conv_ffn_w[0])[None], D)[0], _rows(bits(conv_mix_w[0])[None], D)[0]], axis=0)
    n_ffn_rows = -(-conv_ffn_w[0].size * 2 // D)
    late = [w_up[0].T.astype(BF16), w_down[0].astype(BF16), w_o[0].astype(BF16), w_out_conv[0].T.astype(BF16),
            w_out_attn[0].T.astype(BF16), taps]
    g_in, = _all_gather([w_in[0].T.astype(BF16)], "weights_all_gather")
    W_in_rows = g_in.reshape(NIN, D)
    W_in_t = jnp.concatenate([W_in_rows[:nqkv], W_in_rows[nqkv + H:], W_in_rows[nqkv:nqkv + H],
                              jnp.zeros((LANES - H, D), BF16)], axis=0)
    bf128 = jnp.pad(b_f, ((0, 0), (0, LANES - H)))

    x2d = x.reshape(T, D)
    tgt = loss_target.reshape(T, D)
    pm, fl, h1 = _inproj_fwd(x2d, norm_mix_g, W_in_t, NM, tm_big)
    fcum = _forget_cumsum(fl, bf128, S)
    frow = jnp.pad(fcum[:, :H].reshape(Bl, S, ng, hpg).transpose(0, 2, 3, 1), ((0, 0), (0, 0), (0, 8 - hpg), (0, 0)))
    q_off = 3 * CW // LANES
    o, lse, g_up, g_dn, g_o, g_oc, g_oa, g_taps = _attn_fwd(pm, fcum, frow, S, q_off, AW, hd, tq_fwd, late)
    W_up_t = g_up.reshape(2 * FH, D)
    W_dn = g_dn.reshape(FH, D)
    W_o = g_o.reshape(D, D)
    W_oc_t = g_oc.reshape(D, CW)
    W_oa_t = g_oa.reshape(D, AW)
    tap_bits = g_taps.reshape(N_DEV, -1)
    n_ffn, n_mix = conv_ffn_w[0].size * 2, conv_mix_w[0].size * 2
    wf_full = _from_col_blocks(lax.bitcast_convert_type(
        tap_bits[:, :n_ffn].reshape((N_DEV,) + conv_ffn_w.shape[1:] + (2,)), F32))
    wc_full = _from_col_blocks(lax.bitcast_convert_type(
        tap_bits[:, n_ffn_rows * D:n_ffn_rows * D + n_mix].reshape((N_DEV,) + conv_mix_w.shape[1:] + (2,)), F32))
    wf8 = jnp.pad(wf_full, ((0, 5), (0, 0)))
    wc8 = jnp.pad(wc_full, ((0, 5), (0, 0)))
    x2, u, m, ycin, yc, ya = _merge_fwd(pm, o, x2d, wc8, b_gate, W_oc_t, W_oa_t, W_o, S, tm_big)
    upp, up, act, h2, dx3, loss8, dgf8 = _ffn_fwd(x2, norm_ffn_g, W_up_t, wf8, W_dn, norm_f_g.reshape(1, D), tgt, S, tm_ffn)

    dx2, dupp, dwf8, dg2_8 = _ffn_bwd(dx3, x2, norm_ffn_g, upp, up, wf8, W_dn, W_up_t, S, tm_ffn)
    dW_dn = _wgrad(act, dx3, "wgrad_down")
    dW_up_t = _wgrad(dupp, h2, "wgrad_up")
    ids = jnp.stack([pc, 2 * px + py]).astype(jnp.int32)
    big = [dW_up_t.reshape(4, 2, -1, D), dW_dn.reshape(4, 2, -1, D)]
    dconv, dgl, do, dW_small, dwc8, dbg8, *sib_big = _merge_bwd(
        dx2, pm, u, yc, ya, m, ycin, o, wc8, b_gate, W_o, W_oc_t, W_oa_t, S, tm_big, big)
    big_sums = [_pair_sum(b, r, ids, "grads_pair_sum_%d" % a) for a, (b, r) in enumerate(zip(big, sib_big))]
    dq, dk, dv, dfr, *chips_big = _attn_bwd(pm, do, fcum, frow, lse, S, q_off, AW, hd, tq_bwd, [s[1] for s in big_sums])
    dfc = jnp.pad(dfr[:, :, :hpg, :].transpose(0, 3, 1, 2).reshape(T, H), ((0, 0), (0, LANES - H)))
    dfl, dbf8 = _forget_bwd(dfc, fl, bf128, S)
    dparts = [dconv, dq, dk, dv, dgl, dfl]
    offs = [0, 3 * CW, 3 * CW + AW, 3 * CW + 2 * AW, nqkv, NM]
    dW_in_t = _wgrad(dconv, h1, "wgrad_in_conv", total_rows=NM + LANES)
    dW_in_t = _wgrad([dq, dk, dv], h1, "wgrad_in_qkv", into=dW_in_t, row_off=3 * CW)
    dW_in_t = _wgrad(dgl, h1, "wgrad_in_gates", into=dW_in_t, row_off=nqkv)

    starts, win, index = _in_windows(n_loc, nqkv, H, NM)
    small = dW_small.reshape(4, 2, -1, D)
    my_starts = [jnp.where(pc == 0, starts[2 * j], starts[2 * j + 1]) for j in range(4)]
    win_ids = jnp.stack([pc, 2 * px + py] + my_starts).astype(jnp.int32)
    sib_win, sib_small = _sibling_exchange(dW_in_t, starts, win, [small], "grads_sibling_exchange")
    sums = [_pair_sum(dW_in_t, sib_win, win_ids, "grads_pair_sum_in", win_rows=win),
            _pair_sum(small, sib_small, ids, "grads_pair_sum_small")]
    grad_x, dg1_8, dW_forget, *from_chips = _inproj_bwd(dparts, offs, W_in_t, x2d, norm_mix_g, dx2, tm_big, [s[1] for s in sums])
    red_win, red_small = [_final_sum(s[0], r, "grads_final_sum_%d" % a) for a, (s, r) in enumerate(zip(sums, from_chips))]
    (own_up, _), (own_dn, _) = big_sums
    from_up, from_dn = chips_big

    f_rows = dW_forget[0:8]
    wf_rows = _rows(dwf8[:3].reshape(1, -1), D)[0]
    wc_rows = _rows(dwc8[:3].reshape(1, -1), D)[0]
    smalls = [dg1_8[0:1], dg2_8[0:1], dgf8[0:1], dbg8[0:1, :D], dbg8[0:1, D:], jnp.pad(dbf8[0:1], ((0, 0), (0, D - LANES))),
              jnp.pad(loss8[0:1], ((0, 0), (0, D - LANES))), jnp.zeros((1, D), F32), f_rows, wf_rows, wc_rows]
    spack = jnp.concatenate(smalls, axis=0)
    spack = jnp.pad(spack, ((0, -spack.shape[0] % 8), (0, 0)))
    ssum = _sum8(_all_gather([spack], "small_all_gather")[0], "small_sum")
    g_g1, g_g2, g_gf = ssum[0:1], ssum[1:2], ssum[2]
    g_bg = jnp.concatenate([ssum[3:4], ssum[4:5]], axis=1)
    g_bf = ssum[5:6, :H]
    r0 = 16
    r1 = r0 + wf_rows.shape[0]
    wf_sum = _unrows(ssum[r0:r1], (3, 2 * FH))
    wc_sum = _unrows(ssum[r1:r1 + wc_rows.shape[0]], (3, CW))
    g_wf = lax.dynamic_slice_in_dim(wf_sum, me * conv_ffn_w.shape[-1], conv_ffn_w.shape[-1], axis=1)
    g_wc = lax.dynamic_slice_in_dim(wc_sum, me * conv_mix_w.shape[-1], conv_mix_w.shape[-1], axis=1)

    ext = jnp.concatenate([red_win, ssum[8:8 + H]], axis=0)
    my_index = lax.dynamic_index_in_dim(jnp.asarray(index), me, axis=0, keepdims=False)
    g_w_in_t = jnp.take(ext, my_index, axis=0)
    rb = D // N_DEV
    g_w_o = red_small[:rb]
    g_w_oc = red_small[rb:, :CW].T
    g_w_oa = red_small[rb:, CW:].T

    loss = ssum[6, 0]

    names = ["norm_mix_g", "w_in", "b_f", "b_gate", "conv_mix_w", "w_out_conv", "w_out_attn", "w_o", "norm_ffn_g", "w_up",
             "conv_ffn_w", "w_down", "norm_f_g"]
    weights = [norm_mix_g, w_in, b_f, b_gate, conv_mix_w, w_out_conv, w_out_attn, w_o, norm_ffn_g, w_up, conv_ffn_w, w_down, norm_f_g]
    grads = [g_g1, g_w_in_t, g_bf, g_bg, g_wc, g_w_oc, g_w_oa, g_w_o, g_g2, own_up, g_wf, own_dn, g_gf]
    ms = [m_norm_mix_g, m_w_in, m_b_f, m_b_gate, m_conv_mix_w, m_w_out_conv, m_w_out_attn, m_w_o, m_norm_ffn_g, m_w_up,
          m_conv_ffn_w, m_w_down, m_norm_f_g]
    vs = [v_norm_mix_g, v_w_in, v_b_f, v_b_gate, v_conv_mix_w, v_w_out_conv, v_w_out_attn, v_w_o, v_norm_ffn_g, v_w_up,
          v_conv_ffn_w, v_w_down, v_norm_f_g]
    to_view = {"w_in": lambda a: a[0].T.reshape(-1, LANES), "w_up": lambda a: a[0].T}
    from_view = {"w_in": lambda a: a.reshape(n_loc, D).T[None], "w_up": lambda a: a.T[None]}
    received = {"w_up": from_up, "w_down": from_dn}
    out_grads, steps = [], []
    for nm, w, g, mm, vv in zip(names, weights, grads, ms, vs):
        view, back = to_view.get(nm, lambda a: a), from_view.get(nm, lambda a: a)
        wv, mv, vw = (view(a) for a in (w, mm, vv))
        gv = g.reshape(wv.shape)
        res = _adamw(wv, gv, mv, vw, "adamw_" + nm, recv=received.get(nm))
        if nm in received:
            gv = res[3]
        steps.append(tuple(back(o) for o in res[:3]))
        out_grads.append(back(gv))
    deltas, new_ms, new_vs = zip(*steps)
    return (loss, grad_x.reshape(Bl, S, D), *out_grads, *deltas, *new_ms, *new_vs)
```

```python
import numpy as np

import jax
import jax.numpy as jnp
from jax import lax
from jax.experimental import pallas as pl
from jax.experimental.pallas import tpu as pltpu

F32, BF16 = jnp.float32, jnp.bfloat16
EPS = 1e-6
ADAM_LR, ADAM_B1, ADAM_B2, ADAM_EPS, ADAM_WD, ADAM_STEP = 0.001, 0.9, 0.999, 1e-08, 0.01, 10
N_DEV = 8
LANES = 128
V7X_VMEM_LIMIT = 56 * 1024 * 1024
FFN_CHUNK = 2816
WGRAD_VMEM_BUDGET = 40 * 1024 * 1024
MESH = pl.DeviceIdType.MESH
SDS = jax.ShapeDtypeStruct
ANY = pl.BlockSpec(memory_space=pl.ANY)


def _tile(n, target, mult=LANES):
    best = None
    for t in range(mult, min(n, target) + 1, mult):
        if n % t == 0:
            best = t
    return best if best is not None else n


def _resident(shape):
    return pl.BlockSpec(shape, lambda *_: (0,) * len(shape), pipeline_mode=pl.Buffered(1))


def _cparams(n_axes=1):
    return pltpu.CompilerParams(dimension_semantics=("arbitrary",) * n_axes, vmem_limit_bytes=V7X_VMEM_LIMIT)


def _dot(a, b):
    return jnp.dot(a, b, preferred_element_type=F32)


def _dot_tn(a, b):
    return lax.dot_general(a, b, (((0,), (0,)), ((), ())), preferred_element_type=F32)


def _dot_nt(a, b):
    return lax.dot_general(a, b, (((1,), (1,)), ((), ())), preferred_element_type=F32)


def _sigmoid(x):
    return 0.5 * jnp.tanh(0.5 * x) + 0.5


def _rms(x):
    return lax.rsqrt(jnp.mean(x * x, axis=-1, keepdims=True) + EPS)


def _taps_back(z, r6, r7):
    row = lax.broadcasted_iota(jnp.int32, (8, 1), 0)
    z1, z2 = pltpu.roll(z, 1, 0), pltpu.roll(z, 2, 0)
    z1 = jnp.concatenate([jnp.where(row == 0, r7, z1[0:8]), z1[8:]], axis=0)
    z2 = jnp.concatenate([jnp.where(row == 0, r6, jnp.where(row == 1, r7, z2[0:8])), z2[8:]], axis=0)
    return z1, z2


def _taps_ahead(d, h0, h1):
    tm = d.shape[0]
    row = lax.broadcasted_iota(jnp.int32, (8, 1), 0)
    d1, d2 = pltpu.roll(d, tm - 1, 0), pltpu.roll(d, tm - 2, 0)
    d1 = jnp.concatenate([d1[:tm - 8], jnp.where(row == 7, h0, d1[tm - 8:])], axis=0)
    d2 = jnp.concatenate([d2[:tm - 8], jnp.where(row == 6, h0, jnp.where(row == 7, h1, d2[tm - 8:]))], axis=0)
    return d1, d2


def _tri_dot(tri, x):
    hi = x.astype(BF16)
    r = x - hi.astype(F32)
    mid = r.astype(BF16)
    lo = (r - mid.astype(F32)).astype(BF16)
    return (_dot(tri, lo) + _dot(tri, mid)) + _dot(tri, hi)


def _lane_pick(block, lane):
    lanes = lax.broadcasted_iota(jnp.int32, (1, block.shape[1]), 1)
    return jnp.sum(jnp.where(lanes == lane, block, 0.0), axis=1, keepdims=True)


def _place():
    return lax.axis_index("x"), lax.axis_index("y"), lax.axis_index("c")


def _all_gather(xs, name):
    n = len(xs)

    def body(*refs):
        start, forward, finish = _gather_phases(refs[:n], refs[n:2 * n], *refs[2 * n:])
        start()
        forward()
        finish()

    return pl.pallas_call(
        body, name=name,
        out_shape=_gather_shapes(xs), in_specs=[ANY] * n, out_specs=[ANY] * n, scratch_shapes=_gather_sems(n),
    )(*xs)


def _gather_shapes(xs):
    return [SDS((N_DEV,) + x.shape, x.dtype) for x in xs]


def _gather_sems(n):
    return [pltpu.SemaphoreType.DMA((7 * n,)), pltpu.SemaphoreType.DMA((7 * n,)), pltpu.SemaphoreType.DMA((n,))]


def _gather_phases(x_refs, out_refs, send_sems, recv_sems, local_sems):
    n = len(x_refs)

    def parts():
        px, py, pc = _place()
        me, sibling = (px, py, pc), (px, py, 1 - pc)
        chips = [(1 - px, py), (px, 1 - py), (1 - px, 1 - py)]

        def slot(a, qx, qy, qc):
            return out_refs[a].at[4 * qx + 2 * qy + qc]

        def copy(a, k, block, to, src=None):
            return pltpu.make_async_remote_copy(
                src_ref=slot(a, *block) if src is None else src, dst_ref=slot(a, *block),
                send_sem=send_sems.at[7 * a + k], recv_sem=recv_sems.at[7 * a + k], device_id=to, device_id_type=MESH)

        def mine():
            return [pltpu.make_async_copy(x_refs[a], slot(a, *me), local_sems.at[a]) for a in range(n)]

        def first():
            out = []
            for a in range(n):
                out.append(copy(a, 0, me, sibling, src=x_refs[a]))
                out += [copy(a, 1 + j, me, (*chip, pc), src=x_refs[a]) for j, chip in enumerate(chips)]
            return out

        def landed():
            return [copy(a, 1 + j, (*chip, pc), me) for j, chip in enumerate(chips) for a in range(n)]

        def passed():
            return [copy(a, 4 + j, (*chip, pc), sibling) for j, chip in enumerate(chips) for a in range(n)]

        def late():
            out = [copy(a, 0, sibling, me) for a in range(n)]
            return out + [copy(a, 4 + j, (*chip, 1 - pc), me) for j, chip in enumerate(chips) for a in range(n)]

        return mine, first, landed, passed, late

    def start():
        mine, first, _, _, _ = parts()
        for cp in first() + mine():
            cp.start()

    def forward():
        _, _, landed, passed, _ = parts()
        for got, cp in zip(landed(), passed()):
            got.wait_recv()
            cp.start()

    def finish():
        mine, first, _, passed, late = parts()
        for cp in late():
            cp.wait_recv()
        for cp in first() + passed():
            cp.wait_send()
        for cp in mine():
            cp.wait()

    return start, forward, finish


def _sibling_exchange(win_buf, win_starts, win_rows, blocked, name):
    nb = len(blocked)

    def body(*refs):
        win_ref, blk_refs = refs[0], refs[1:1 + nb]
        rwin_ref, rblk_refs = refs[1 + nb], refs[2 + nb:2 + 2 * nb]
        send_sems, recv_sems, wsend_sems, wrecv_sems = refs[2 + 2 * nb:]
        px, py, pc = _place()
        copies = _sibling_copies(blk_refs, rblk_refs, send_sems, recv_sems)
        for j in range(4):
            theirs = jnp.where(pc == 0, win_starts[2 * j + 1], win_starts[2 * j])
            copies.append(pltpu.make_async_remote_copy(
                src_ref=win_ref.at[pl.ds(pl.multiple_of(theirs, 8), win_rows)], dst_ref=rwin_ref.at[j],
                send_sem=wsend_sems.at[j], recv_sem=wrecv_sems.at[j], device_id=(px, py, 1 - pc), device_id_type=MESH))
        for cp in copies:
            cp.start()
        for cp in copies:
            cp.wait()

    C = win_buf.shape[1]
    return pl.pallas_call(
        body, name=name,
        out_shape=[SDS((4, win_rows, C), F32)] + _sibling_shapes(blocked),
        in_specs=[ANY] * (1 + nb), out_specs=[ANY] * (1 + nb),
        scratch_shapes=_exchange_sems(nb) + _exchange_sems(4),
    )(win_buf, *blocked)


def _sibling_shapes(blocked):
    return [SDS((4,) + b.shape[2:], F32) for b in blocked]


def _exchange_sems(n):
    return [pltpu.SemaphoreType.DMA((n,)), pltpu.SemaphoreType.DMA((n,))]


def _sibling_copies(blk_refs, out_refs, send_sems, recv_sems):
    px, py, pc = _place()
    return [pltpu.make_async_remote_copy(
        src_ref=b.at[:, 1 - pc], dst_ref=o, send_sem=send_sems.at[a], recv_sem=recv_sems.at[a],
        device_id=(px, py, 1 - pc), device_id_type=MESH) for a, (b, o) in enumerate(zip(blk_refs, out_refs))]


def _chip_shapes(ps):
    return [SDS((3,) + p.shape[1:], p.dtype) for p in ps]


def _chip_copies(p_refs, out_refs, send_sems, recv_sems):
    px, py, pc = _place()
    n = len(p_refs)
    chips = [(1 - px, py), (px, 1 - py), (1 - px, 1 - py)]
    return [pltpu.make_async_remote_copy(
        src_ref=p_refs[a].at[2 * qx + qy], dst_ref=out_refs[a].at[k],
        send_sem=send_sems.at[3 * a + k], recv_sem=recv_sems.at[3 * a + k],
        device_id=(qx, qy, pc), device_id_type=MESH) for k, (qx, qy) in enumerate(chips) for a in range(n)]


def _pair_sum(own, recv, ids, name, win_rows=None):
    _, R, C = recv.shape
    tr = _tile(R, 512, 8)

    def body(ids_ref, g_ref, r_ref, own_ref, pb_ref):
        s = g_ref[...] + r_ref[...]
        pb_ref[...] = s.astype(BF16)

        @pl.when(pl.program_id(1) == ids_ref[1])
        def _():
            own_ref[...] = s

    if win_rows is None:
        own_spec = pl.BlockSpec((None, None, tr, C), lambda r, j, ids: (j, ids[0], r, 0))
    else:
        own_spec = pl.BlockSpec((pl.Element(tr), pl.Element(C)), lambda r, j, ids: (pl.multiple_of(ids[2 + j] + r * tr, 8), 0))
    return pl.pallas_call(
        body, name=name,
        grid_spec=pltpu.PrefetchScalarGridSpec(
            num_scalar_prefetch=1, grid=(R // tr, 4),
            in_specs=[own_spec, pl.BlockSpec((None, tr, C), lambda r, j, ids: (j, r, 0))],
            out_specs=[pl.BlockSpec((tr, C), lambda r, j, ids: (r, 0)),
                       pl.BlockSpec((None, tr, C), lambda r, j, ids: (j, r, 0))]),
        out_shape=[SDS((R, C), F32), SDS((4, R, C), BF16)],
        compiler_params=_cparams(2),
    )(ids, own, recv)


def _final_sum(own, recv, name):
    R, C = own.shape
    tr = _tile(R, 512, 8)

    def body(o_ref, r_ref, out_ref):
        out_ref[...] = ((o_ref[...] + r_ref[0].astype(F32)) + r_ref[1].astype(F32)) + r_ref[2].astype(F32)

    return pl.pallas_call(
        body, name=name, grid=(R // tr,),
        in_specs=[pl.BlockSpec((tr, C), lambda r: (r, 0)), pl.BlockSpec((3, tr, C), lambda r: (0, r, 0))],
        out_specs=pl.BlockSpec((tr, C), lambda r: (r, 0)),
        out_shape=SDS((R, C), F32),
        compiler_params=_cparams(1),
    )(own, recv)


def _sum8(a, name):
    def body(a_ref, out_ref):
        s = a_ref[0]
        for d in range(1, N_DEV):
            s = s + a_ref[d]
        out_ref[...] = s

    return pl.pallas_call(body, name=name, out_shape=SDS(a.shape[1:], F32))(a)


def _inproj_fwd(x, g1, w_t, NM, tm):
    T, D = x.shape
    NF = w_t.shape[0] - NM
    ch = _tile(NM, 1024)

    def body(x_ref, g_ref, w_ref, pm_ref, fl_ref, h_ref):
        xv = x_ref[...]
        h = (xv * _rms(xv) * g_ref[...]).astype(BF16)
        h_ref[...] = h
        for c in range(0, NM, ch):
            pm_ref[:, c:c + ch] = _dot_nt(h, w_ref[c:c + ch, :]).astype(BF16)
        fl_ref[...] = _dot_nt(h, w_ref[NM:NM + NF, :])

    return pl.pallas_call(
        body, name="inproj_fwd", grid=(T // tm,),
        in_specs=[pl.BlockSpec((tm, D), lambda i: (i, 0)), _resident((1, D)), _resident(w_t.shape)],
        out_specs=[pl.BlockSpec((tm, NM), lambda i: (i, 0)), pl.BlockSpec((tm, NF), lambda i: (i, 0)),
                   pl.BlockSpec((tm, D), lambda i: (i, 0))],
        out_shape=[SDS((T, NM), BF16), SDS((T, NF), F32), SDS((T, D), BF16)],
        compiler_params=_cparams(1),
    )(x, g1, w_t)


def _log_sigmoid(x):
    return jnp.minimum(x, 0.0) - jnp.log(1.0 + jnp.exp(-jnp.abs(x)))


def _forget_cumsum(fl, bf, S):
    T, NF = fl.shape
    ch = _tile(S, 256, 8)

    def body(fl_ref, bf_ref, f_ref):
        row = lax.broadcasted_iota(jnp.int32, (ch, ch), 0)
        col = lax.broadcasted_iota(jnp.int32, (ch, ch), 1)
        tri = (col <= row).astype(BF16)
        carry = jnp.zeros((1, NF), F32)
        for c in range(0, S, ch):
            lf = _log_sigmoid(fl_ref[c:c + ch, :] + bf_ref[...])
            f_ref[c:c + ch, :] = _tri_dot(tri, lf) + carry
            carry = carry + jnp.sum(lf, axis=0, keepdims=True)

    return pl.pallas_call(
        body, name="forget_cumsum", grid=(T // S,),
        in_specs=[pl.BlockSpec((S, NF), lambda b: (b, 0)), _resident((1, NF))],
        out_specs=pl.BlockSpec((S, NF), lambda b: (b, 0)),
        out_shape=SDS((T, NF), F32),
        compiler_params=_cparams(1),
    )(fl, bf)


def _head_mask(e, hd):
    lanes = lax.broadcasted_iota(jnp.int32, (1, LANES), 1)
    return (lanes >= e * hd) & (lanes < (e + 1) * hd)


def _attn_specs(S, q_off, AW):
    ng = AW // LANES
    return [pl.BlockSpec((S, LANES), lambda b, g, o=q_off + w * ng: (b, o + g)) for w in range(3)]


def _attn_fwd(pm, fcum, frow, S, q_off, AW, hd, tq, gather):
    T = pm.shape[0]
    scale = float(hd) ** -0.5
    nq, hpg = S // tq, LANES // hd
    ng, nx = AW // LANES, len(gather)
    steps = (T // S) * ng

    def body(q_ref, k_ref, v_ref, fc_ref, fr_ref, *rest):
        x_refs, (o_ref, lse_ref), out_refs, sems = rest[:nx], rest[nx:nx + 2], rest[nx + 2:2 * nx + 2], rest[2 * nx + 2:]
        g = pl.program_id(1)
        step = pl.program_id(0) * ng + g
        start, forward, finish = _gather_phases(x_refs, out_refs, *sems)
        pl.when(step == 0)(start)

        @pl.when(g == 0)
        def _():
            lse_ref[...] = jnp.zeros_like(lse_ref)

        lanes = lax.broadcasted_iota(jnp.int32, (1, LANES), 1)
        for i in range(nq):
            rs, kend = slice(i * tq, (i + 1) * tq), (i + 1) * tq
            row = i * tq + lax.broadcasted_iota(jnp.int32, (tq, kend), 0)
            col = lax.broadcasted_iota(jnp.int32, (tq, kend), 1)
            o_tile = jnp.zeros((tq, LANES), F32)
            lse_tile = lse_ref[rs, :]
            for e in range(hpg):
                mask = _head_mask(e, hd)
                qs = jnp.where(mask, q_ref[rs, :], 0) * scale
                s = _dot_nt(qs, k_ref[0:kend, :]) + _lane_pick(fc_ref[rs, :], g * hpg + e) - fr_ref[e:e + 1, 0:kend]
                s = jnp.where(col <= row, s, -1e30)
                m = jnp.max(s, axis=1, keepdims=True)
                p = jnp.exp(s - m)
                l = jnp.sum(p, axis=1, keepdims=True)
                o_tile = jnp.where(mask, _dot(p.astype(BF16), v_ref[0:kend, :]) / l, o_tile)
                lse_tile = jnp.where(lanes == g * hpg + e, m + jnp.log(l), lse_tile)
            o_ref[rs, :] = o_tile.astype(BF16)
            lse_ref[rs, :] = lse_tile
        pl.when(step == (steps * 5) // 8)(forward)
        pl.when(step == steps - 1)(finish)

    full = pl.BlockSpec((S, LANES), lambda b, g: (b, 0))
    return pl.pallas_call(
        body, name="attn_fwd", grid=(T // S, ng),
        in_specs=_attn_specs(S, q_off, AW) + [full, pl.BlockSpec((None, None, 8, S), lambda b, g: (b, g, 0, 0))] + [ANY] * nx,
        out_specs=[pl.BlockSpec((S, LANES), lambda b, g: (b, g)), full] + [ANY] * nx,
        out_shape=[SDS((T, AW), BF16), SDS((T, LANES), F32)] + _gather_shapes(gather),
        scratch_shapes=_gather_sems(nx),
        compiler_params=_cparams(2),
    )(pm, pm, pm, fcum, frow, *gather)


def _merge_fwd(pm, o, x, wc, bg, woc_t, woa_t, wo, S, tm):
    T, D = x.shape
    CW, AW = woc_t.shape[1], woa_t.shape[1]
    tps = S // tm

    def body(cb_ref, cc_ref, cin_ref, gc_ref, ga_ref, o_ref, x_ref, wc_ref, bg_ref, woct_ref, woat_ref, wo_ref,
             x2_ref, u_ref, m_ref, ycin_ref, yc_ref, ya_ref, tail_ref):
        @pl.when(pl.program_id(0) % tps == 0)
        def _():
            tail_ref[...] = jnp.zeros_like(tail_ref)

        z = cc_ref[...].astype(F32) * cin_ref[...].astype(F32)
        z1, z2 = _taps_back(z, tail_ref[6:7, :], tail_ref[7:8, :])
        tail_ref[...] = z[tm - 8:tm, :]
        u = (z * wc_ref[2:3, :] + z2 * wc_ref[0:1, :]) + z1 * wc_ref[1:2, :]
        u_ref[...] = u.astype(BF16)
        ycin = (cb_ref[...].astype(F32) * u).astype(BF16)
        ycin_ref[...] = ycin
        yc = _dot_nt(ycin, woct_ref[...])
        ya = _dot_nt(o_ref[...], woat_ref[...])
        yc_ref[...] = yc.astype(BF16)
        ya_ref[...] = ya.astype(BF16)
        gc = _sigmoid(gc_ref[...].astype(F32) + bg_ref[:, 0:D])
        ga = _sigmoid(ga_ref[...].astype(F32) + bg_ref[:, D:2 * D])
        m = (gc * yc + ga * ya).astype(BF16)
        m_ref[...] = m
        x2_ref[...] = x_ref[...] + _dot(m, wo_ref[...])

    g_off = (3 * CW + 3 * AW) // D
    tok = lambda w, j=0: pl.BlockSpec((tm, w), lambda i: (i, j))
    return pl.pallas_call(
        body, name="merge_fwd", grid=(T // tm,),
        in_specs=[tok(CW, 0), tok(CW, 1), tok(CW, 2), tok(D, g_off), tok(D, g_off + 1), tok(AW), tok(D),
                  _resident((8, CW)), _resident((1, 2 * D)), _resident((D, CW)), _resident((D, AW)), _resident((D, D))],
        out_specs=[tok(D), tok(CW), tok(D), tok(CW), tok(D), tok(D)],
        out_shape=[SDS((T, D), F32), SDS((T, CW), BF16), SDS((T, D), BF16), SDS((T, CW), BF16),
                   SDS((T, D), BF16), SDS((T, D), BF16)],
        scratch_shapes=[pltpu.VMEM((8, CW), F32)],
        compiler_params=_cparams(1),
    )(pm, pm, pm, pm, pm, o, x, wc, bg, woc_t, woa_t, wo)


def _ffn_fwd(x2, g2, wup_t, wf, wdn, gf, tgt, S, tm):
    T, D = x2.shape
    FH = wdn.shape[0]
    ch = _tile(FH, FFN_CHUNK)
    tps = S // tm

    def body(x2_ref, g2_ref, wupt_ref, wf_ref, wdn_ref, gf_ref, tgt_ref,
             upp_ref, up_ref, act_ref, h2_ref, dx3_ref, loss_ref, dgf_ref, tail_ref):
        i = pl.program_id(0)

        @pl.when(i % tps == 0)
        def _():
            tail_ref[...] = jnp.zeros_like(tail_ref)

        @pl.when(i == 0)
        def _():
            loss_ref[...] = jnp.zeros_like(loss_ref)
            dgf_ref[...] = jnp.zeros_like(dgf_ref)

        x2v = x2_ref[...]
        h2 = (x2v * _rms(x2v) * g2_ref[...]).astype(BF16)
        h2_ref[...] = h2
        x3 = x2v
        for c in range(0, FH, ch):
            gated = []
            for cols in (slice(c, c + ch), slice(FH + c, FH + c + ch)):
                upp = _dot_nt(h2, wupt_ref[cols, :])
                upp_ref[:, cols] = upp.astype(BF16)
                p1, p2 = _taps_back(upp, tail_ref[6:7, cols], tail_ref[7:8, cols])
                tail_ref[:, cols] = upp[tm - 8:tm, :]
                up = (upp * wf_ref[2:3, cols] + p2 * wf_ref[0:1, cols]) + p1 * wf_ref[1:2, cols]
                up_ref[:, cols] = up.astype(BF16)
                gated.append(up)
            a, b = gated
            act = (a * _sigmoid(a) * b).astype(BF16)
            act_ref[:, c:c + ch] = act
            x3 = x3 + _dot(act, wdn_ref[c:c + ch, :])
        r3 = _rms(x3)
        xn3 = x3 * r3
        e = xn3 * gf_ref[...] - tgt_ref[...]
        loss_ref[...] += 0.5 * jnp.sum(jnp.mean(e * e, axis=-1, keepdims=True), axis=0, keepdims=True)
        dy = e / D
        dgf_ref[0:1, :] += jnp.sum(dy * xn3, axis=0, keepdims=True)
        dxn = dy * gf_ref[...]
        dx3_ref[...] = r3 * (dxn - xn3 * jnp.mean(dxn * xn3, axis=-1, keepdims=True))

    tok = lambda w: pl.BlockSpec((tm, w), lambda i: (i, 0))
    return pl.pallas_call(
        body, name="ffn_fwd", grid=(T // tm,),
        in_specs=[tok(D), _resident((1, D)), _resident((2 * FH, D)), _resident((8, 2 * FH)), _resident((FH, D)),
                  _resident((1, D)), tok(D)],
        out_specs=[tok(2 * FH), tok(2 * FH), tok(FH), tok(D), tok(D), pl.BlockSpec((8, LANES), lambda i: (0, 0)),
                   pl.BlockSpec((8, D), lambda i: (0, 0))],
        out_shape=[SDS((T, 2 * FH), BF16), SDS((T, 2 * FH), BF16), SDS((T, FH), BF16), SDS((T, D), BF16), SDS((T, D), F32),
                   SDS((8, LANES), F32), SDS((8, D), F32)],
        scratch_shapes=[pltpu.VMEM((8, 2 * FH), F32)],
        compiler_params=_cparams(1),
    )(x2, g2, wup_t, wf, wdn, gf, tgt)


def _ffn_bwd(dx3, x2, g2, upp, up, wf, wdn, wup_t, S, tm):
    T, D = x2.shape
    FH = wdn.shape[0]
    ch = _tile(FH, FFN_CHUNK)
    n, tps = T // tm, S // tm

    def body(dx3_ref, x2_ref, g2_ref, upp_ref, up_ref, wf_ref, wdn_ref, wupt_ref,
             dx2_ref, dupp_ref, dwf_ref, dg2_ref, head_ref):
        i = pl.program_id(0)
        t = n - 1 - i

        @pl.when(i == 0)
        def _():
            dwf_ref[...] = jnp.zeros_like(dwf_ref)
            dg2_ref[...] = jnp.zeros_like(dg2_ref)

        @pl.when(t % tps == tps - 1)
        def _():
            head_ref[...] = jnp.zeros_like(head_ref)

        dx3v = dx3_ref[...]
        dx3b = dx3v.astype(BF16)
        x2v = x2_ref[...]
        r2 = _rms(x2v)
        xn2 = x2v * r2
        dh2 = jnp.zeros((tm, D), F32)
        for c in range(0, FH, ch):
            ca, cb = slice(c, c + ch), slice(FH + c, FH + c + ch)
            a, b = up_ref[:, ca].astype(F32), up_ref[:, cb].astype(F32)
            sig = _sigmoid(a)
            sl = a * sig
            dact = _dot_nt(dx3b, wdn_ref[ca, :])
            grads = (dact * b * (sig * (1.0 + a * (1.0 - sig))), dact * sl)
            for cols, d in zip((ca, cb), grads):
                u0 = upp_ref[:, cols].astype(F32)
                d1, d2 = _taps_ahead(d, head_ref[0:1, cols], head_ref[1:2, cols])
                head_ref[:, cols] = d[0:8, :]
                dwf_ref[2:3, cols] += jnp.sum(u0 * d, axis=0, keepdims=True)
                dwf_ref[1:2, cols] += jnp.sum(u0 * d1, axis=0, keepdims=True)
                dwf_ref[0:1, cols] += jnp.sum(u0 * d2, axis=0, keepdims=True)
                dpre = ((d * wf_ref[2:3, cols] + d1 * wf_ref[1:2, cols]) + d2 * wf_ref[0:1, cols]).astype(BF16)
                dupp_ref[:, cols] = dpre
                dh2 = dh2 + _dot(dpre, wupt_ref[cols, :])
        dg2_ref[0:1, :] += jnp.sum(dh2 * xn2, axis=0, keepdims=True)
        dxn = dh2 * g2_ref[...]
        dx2_ref[...] = dx3v + r2 * (dxn - xn2 * jnp.mean(dxn * xn2, axis=-1, keepdims=True))

    tok = lambda w: pl.BlockSpec((tm, w), lambda i: (n - 1 - i, 0))
    acc = lambda w: pl.BlockSpec((8, w), lambda i: (0, 0))
    return pl.pallas_call(
        body, name="ffn_bwd", grid=(n,),
        in_specs=[tok(D), tok(D), _resident((1, D)), tok(2 * FH), tok(2 * FH), _resident((8, 2 * FH)),
                  _resident((FH, D)), _resident((2 * FH, D))],
        out_specs=[tok(D), tok(2 * FH), acc(2 * FH), acc(D)],
        out_shape=[SDS((T, D), F32), SDS((T, 2 * FH), BF16), SDS((8, 2 * FH), F32), SDS((8, D), F32)],
        scratch_shapes=[pltpu.VMEM((8, 2 * FH), F32)],
        compiler_params=_cparams(1),
    )(dx3, x2, g2, upp, up, wf, wdn, wup_t)


def _merge_bwd(dx2, pm, u, yc, ya, m, ycin, o, wc, bg, wo, woc_t, woa_t, S, tm, exchange):
    T, D = dx2.shape
    CW, AW = woc_t.shape[1], woa_t.shape[1]
    n, tps = T // tm, S // tm
    nx = len(exchange)
    rb = D // N_DEV
    assert CW + AW == D

    def body(dx2_ref, cb_ref, cc_ref, cin_ref, gc_ref, ga_ref, u_ref, yc_ref, ya_ref, m_ref, ycin_ref, o_ref,
             wc_ref, bg_ref, wo_ref, woct_ref, woat_ref, *rest):
        x_refs = rest[:nx]
        dconv_ref, dgl_ref, do_ref, dws_ref, dwc_ref, dbg_ref = rest[nx:nx + 6]
        got_refs, head_ref, sems = rest[nx + 6:2 * nx + 6], rest[2 * nx + 6], rest[2 * nx + 7:]

        def add_blocks(rows, cols, grad):
            for d in range(N_DEV):
                dws_ref[d, rows, cols] += grad[d * rb:(d + 1) * rb, :]

        i = pl.program_id(0)
        t = n - 1 - i

        @pl.when(i == 0)
        def _():
            for cp in _sibling_copies(x_refs, got_refs, *sems):
                cp.start()

        @pl.when(i == 0)
        def _():
            for ref in (dwc_ref, dbg_ref, dws_ref):
                ref[...] = jnp.zeros_like(ref)

        @pl.when(t % tps == tps - 1)
        def _():
            head_ref[...] = jnp.zeros_like(head_ref)

        dx2b = dx2_ref[...].astype(BF16)
        add_blocks(slice(0, rb), slice(0, D), _dot_tn(m_ref[...], dx2b))
        dm = _dot_nt(dx2b, wo_ref[...])
        outs = []
        for g_ref, y_ref, cols in ((gc_ref, yc_ref, slice(0, D)), (ga_ref, ya_ref, slice(D, 2 * D))):
            g = _sigmoid(g_ref[...].astype(F32) + bg_ref[:, cols])
            dgl = dm * y_ref[...].astype(F32) * g * (1.0 - g)
            dgl_ref[:, cols] = dgl.astype(BF16)
            dbg_ref[0:1, cols] += jnp.sum(dgl, axis=0, keepdims=True)
            outs.append((dm * g).astype(BF16))
        dyc, dya = outs
        add_blocks(slice(rb, 2 * rb), slice(0, CW), _dot_tn(dyc, ycin_ref[...]))
        add_blocks(slice(rb, 2 * rb), slice(CW, CW + AW), _dot_tn(dya, o_ref[...]))
        do_ref[...] = _dot(dya, woat_ref[...]).astype(BF16)
        dycin = _dot(dyc, woct_ref[...])
        cc, cin = cc_ref[...].astype(F32), cin_ref[...].astype(F32)
        z = cc * cin
        du = dycin * cb_ref[...].astype(F32)
        du1, du2 = _taps_ahead(du, head_ref[0:1, :], head_ref[1:2, :])
        head_ref[...] = du[0:8, :]
        dwc_ref[2:3, :] += jnp.sum(z * du, axis=0, keepdims=True)
        dwc_ref[1:2, :] += jnp.sum(z * du1, axis=0, keepdims=True)
        dwc_ref[0:1, :] += jnp.sum(z * du2, axis=0, keepdims=True)
        dz = (du * wc_ref[2:3, :] + du1 * wc_ref[1:2, :]) + du2 * wc_ref[0:1, :]
        dconv_ref[:, 0:CW] = (dycin * u_ref[...].astype(F32)).astype(BF16)
        dconv_ref[:, CW:2 * CW] = (dz * cin).astype(BF16)
        dconv_ref[:, 2 * CW:3 * CW] = (dz * cc).astype(BF16)

        @pl.when(i == n - 1)
        def _():
            for cp in _sibling_copies(x_refs, got_refs, *sems):
                cp.wait()

    g_off = (3 * CW + 3 * AW) // D
    tok = lambda w, j=0: pl.BlockSpec((tm, w), lambda i: (n - 1 - i, j))
    acc = lambda w: pl.BlockSpec((8, w), lambda i: (0, 0))
    return pl.pallas_call(
        body, name="merge_bwd", grid=(n,),
        in_specs=[tok(D), tok(CW, 0), tok(CW, 1), tok(CW, 2), tok(D, g_off), tok(D, g_off + 1), tok(CW), tok(D), tok(D),
                  tok(D), tok(CW), tok(AW),
                  _resident((8, CW)), _resident((1, 2 * D)), _resident((D, D)), _resident((D, CW)), _resident((D, AW))]
                 + [ANY] * nx,
        out_specs=[tok(3 * CW), tok(2 * D), tok(AW), pl.BlockSpec((N_DEV, 2 * rb, D), lambda i: (0, 0, 0)), acc(CW),
                   acc(2 * D)] + [ANY] * nx,
        out_shape=[SDS((T, 3 * CW), BF16), SDS((T, 2 * D), BF16), SDS((T, AW), BF16), SDS((N_DEV, 2 * rb, D), F32),
                   SDS((8, CW), F32), SDS((8, 2 * D), F32)] + _sibling_shapes(exchange),
        scratch_shapes=[pltpu.VMEM((8, CW), F32)] + _exchange_sems(nx),
        compiler_params=_cparams(1),
    )(dx2, pm, pm, pm, pm, pm, u, yc, ya, m, ycin, o, wc, bg, wo, woc_t, woa_t, *exchange)


def _attn_bwd(pm, do, fcum, frow, lse, S, q_off, AW, hd, tq, exchange):
    T = pm.shape[0]
    scale = float(hd) ** -0.5
    nq, hpg, ng = S // tq, LANES // hd, AW // LANES
    nx = len(exchange)
    steps = (T // S) * ng

    def body(q_ref, k_ref, v_ref, do_ref, fc_ref, fr_ref, lse_ref, *rest):
        x_refs, (dq_ref, dk_ref, dv_ref, dfr_ref) = rest[:nx], rest[nx:nx + 4]
        got_refs, (dk_acc, dv_acc), sems = rest[nx + 4:2 * nx + 4], rest[2 * nx + 4:2 * nx + 6], rest[2 * nx + 6:]
        g = pl.program_id(1)
        step = pl.program_id(0) * ng + g

        @pl.when(step == 0)
        def _():
            for cp in _chip_copies(x_refs, got_refs, *sems):
                cp.start()

        dk_acc[...] = jnp.zeros_like(dk_acc)
        dv_acc[...] = jnp.zeros_like(dv_acc)
        dfr_ref[...] = jnp.zeros_like(dfr_ref)
        for i in range(nq):
            rs, kend = slice(i * tq, (i + 1) * tq), (i + 1) * tq
            row = i * tq + lax.broadcasted_iota(jnp.int32, (tq, kend), 0)
            col = lax.broadcasted_iota(jnp.int32, (tq, kend), 1)
            kk, vv = k_ref[0:kend, :], v_ref[0:kend, :]
            dq_tile = jnp.zeros((tq, LANES), F32)
            for e in range(hpg):
                mask = _head_mask(e, hd)
                qs = jnp.where(mask, q_ref[rs, :], 0) * scale
                doi = jnp.where(mask, do_ref[rs, :], 0)
                s = _dot_nt(qs, kk) + _lane_pick(fc_ref[rs, :], g * hpg + e) - fr_ref[e:e + 1, 0:kend]
                p = jnp.where(col <= row, jnp.exp(s - _lane_pick(lse_ref[rs, :], g * hpg + e)), 0.0)
                dp = _dot_nt(doi, vv)
                ds = p * (dp - jnp.sum(p * dp, axis=1, keepdims=True))
                pb, dsb = p.astype(BF16), ds.astype(BF16)
                dq_tile = jnp.where(mask, _dot(dsb, kk) * scale, dq_tile)
                dv_acc[0:kend, :] += _dot_tn(pb, doi)
                dk_acc[0:kend, :] += _dot_tn(dsb, qs)
                dfr_ref[e:e + 1, 0:kend] -= jnp.sum(ds, axis=0, keepdims=True)
            dq_ref[rs, :] = dq_tile.astype(BF16)
        dk_ref[...] = dk_acc[...].astype(BF16)
        dv_ref[...] = dv_acc[...].astype(BF16)

        @pl.when(step == steps - 1)
        def _():
            for cp in _chip_copies(x_refs, got_refs, *sems):
                cp.wait()

    full = pl.BlockSpec((S, LANES), lambda b, g: (b, 0))
    grp = pl.BlockSpec((S, LANES), lambda b, g: (b, g))
    rows = pl.BlockSpec((None, None, 8, S), lambda b, g: (b, g, 0, 0))
    return pl.pallas_call(
        body, name="attn_bwd", grid=(T // S, ng),
        in_specs=_attn_specs(S, q_off, AW) + [grp, full, rows, full] + [ANY] * nx,
        out_specs=[grp, grp, grp, rows] + [ANY] * nx,
        out_shape=[SDS((T, AW), BF16), SDS((T, AW), BF16), SDS((T, AW), BF16), SDS((T // S, ng, 8, S), F32)]
                  + _chip_shapes(exchange),
        scratch_shapes=[pltpu.VMEM((S, LANES), F32), pltpu.VMEM((S, LANES), F32)] + _exchange_sems(3 * nx),
        compiler_params=_cparams(2),
    )(pm, pm, pm, do, fcum, frow, lse, *exchange)


def _forget_bwd(dfc, fl, bf, S):
    T, NF = fl.shape
    ch = _tile(S, 256, 8)

    def body(df_ref, fl_ref, bf_ref, dfl_ref, dbf_ref):
        @pl.when(pl.program_id(0) == 0)
        def _():
            dbf_ref[...] = jnp.zeros_like(dbf_ref)

        row = lax.broadcasted_iota(jnp.int32, (ch, ch), 0)
        col = lax.broadcasted_iota(jnp.int32, (ch, ch), 1)
        tri = (col >= row).astype(BF16)
        carry = jnp.zeros((1, NF), F32)
        for c in range(S - ch, -1, -ch):
            d = df_ref[c:c + ch, :]
            dlf = _tri_dot(tri, d) + carry
            carry = carry + jnp.sum(d, axis=0, keepdims=True)
            dfl = dlf * _sigmoid(-(fl_ref[c:c + ch, :] + bf_ref[...]))
            dfl_ref[c:c + ch, :] = dfl.astype(BF16)
            dbf_ref[0:1, :] += jnp.sum(dfl, axis=0, keepdims=True)

    return pl.pallas_call(
        body, name="forget_bwd", grid=(T // S,),
        in_specs=[pl.BlockSpec((S, NF), lambda b: (b, 0)), pl.BlockSpec((S, NF), lambda b: (b, 0)), _resident((1, NF))],
        out_specs=[pl.BlockSpec((S, NF), lambda b: (b, 0)), pl.BlockSpec((8, NF), lambda b: (0, 0))],
        out_shape=[SDS((T, NF), BF16), SDS((8, NF), F32)],
        compiler_params=_cparams(1),
    )(dfc, fl, bf)


def _inproj_bwd(dparts, offs, w_t, x, g1, dx2, tm, exchange):
    T, D = x.shape
    npart, nx = len(dparts), len(exchange)
    n = T // tm

    def body(*refs):
        d_refs = refs[:npart]
        w_ref, x_ref, g_ref, dx2_ref = refs[npart:npart + 4]
        x_refs = refs[npart + 4:npart + 4 + nx]
        dx_ref, dg_ref, dwl_ref = refs[npart + 4 + nx:npart + 7 + nx]
        got_refs, sems = refs[npart + 7 + nx:npart + 7 + 2 * nx], refs[npart + 7 + 2 * nx:]

        @pl.when(pl.program_id(0) == 0)
        def _():
            for cp in _chip_copies(x_refs, got_refs, *sems):
                cp.start()

        @pl.when(pl.program_id(0) == 0)
        def _():
            dg_ref[...] = jnp.zeros_like(dg_ref)
            dwl_ref[...] = jnp.zeros_like(dwl_ref)

        dh = None
        for d_ref, off in zip(d_refs, offs):
            term = _dot(d_ref[...], w_ref[off:off + d_ref.shape[1], :])
            dh = term if dh is None else dh + term
        xv = x_ref[...]
        r = _rms(xv)
        xn = xv * r
        dg_ref[0:1, :] += jnp.sum(dh * xn, axis=0, keepdims=True)
        dwl_ref[...] += _dot_tn(d_refs[-1][...], (xn * g_ref[...]).astype(BF16))
        dxn = dh * g_ref[...]
        dx_ref[...] = dx2_ref[...] + r * (dxn - xn * jnp.mean(dxn * xn, axis=-1, keepdims=True))

        @pl.when(pl.program_id(0) == n - 1)
        def _():
            for cp in _chip_copies(x_refs, got_refs, *sems):
                cp.wait()

    tok = lambda w: pl.BlockSpec((tm, w), lambda i: (i, 0))
    nl = dparts[-1].shape[1]
    return pl.pallas_call(
        body, name="inproj_bwd", grid=(n,),
        in_specs=[tok(d.shape[1]) for d in dparts] + [_resident(w_t.shape), tok(D), _resident((1, D)), tok(D)] + [ANY] * nx,
        out_specs=[tok(D), pl.BlockSpec((8, D), lambda i: (0, 0)), pl.BlockSpec((nl, D), lambda i: (0, 0))] + [ANY] * nx,
        out_shape=[SDS((T, D), F32), SDS((8, D), F32), SDS((nl, D), F32)] + _chip_shapes(exchange),
        scratch_shapes=_exchange_sems(3 * nx),
        compiler_params=_cparams(1),
    )(*dparts, w_t, x, g1, dx2, *exchange)


def _wgrad(bs, a, name, into=None, row_off=0, total_rows=None):
    bs = list(bs) if isinstance(bs, (list, tuple)) else [bs]
    P = len(bs)
    T, N = bs[0].shape
    M = a.shape[1]
    tn = _tile(N, 1408 if M <= 1024 else 512)
    while row_off % tn:
        tn = _tile(N, tn - LANES)
    tk = _tile(T, 512, 16)
    per_row = 2 * (P * tn * bs[0].dtype.itemsize + M * a.dtype.itemsize)
    while T % (2 * tk) == 0 and 2 * tk * per_row + 2 * tn * M * 4 <= WGRAD_VMEM_BUDGET:
        tk *= 2
    blk0, nj, nk = row_off // tn, N // tn, T // tk

    def body(*refs):
        b_refs, a_ref, o_ref = refs[:P], refs[P], refs[-1]
        j = pl.program_id(0)

        @pl.when(pl.program_id(1) == 0)
        def _():
            o_ref[...] = jnp.zeros_like(o_ref)

        for p, b_ref in enumerate(b_refs):
            @pl.when((j >= p * nj) & (j < (p + 1) * nj))
            def _(b_ref=b_ref):
                o_ref[...] += _dot_tn(b_ref[...].astype(BF16), a_ref[...].astype(BF16))

    def b_spec(p):
        def index(j, k):
            before, mine = j < p * nj, (j >= p * nj) & (j < (p + 1) * nj)
            return (jnp.where(mine, k, jnp.where(before, 0, nk - 1)),
                    jnp.where(mine, j - p * nj, jnp.where(before, 0, nj - 1)))
        return pl.BlockSpec((tk, tn), index)

    in_specs = [b_spec(p) for p in range(P)] + [pl.BlockSpec((tk, M), lambda j, k: (k, 0))]
    args = (*bs, a)
    kwargs = {}
    if into is not None:
        in_specs.append(ANY)
        args += (into,)
        kwargs["input_output_aliases"] = {P + 1: 0}
        total_rows = into.shape[0]
    return pl.pallas_call(
        body, name=name, grid=(P * nj, nk),
        in_specs=in_specs,
        out_specs=pl.BlockSpec((tn, M), lambda j, k: (blk0 + j, 0)),
        out_shape=SDS((P * N if total_rows is None else total_rows, M), F32),
        compiler_params=_cparams(2), **kwargs,
    )(*args)


def _adamw(w, g, m, v, name, recv=None):
    shape = w.shape
    C = shape[-1]
    w2, g2, m2, v2 = (a.reshape(-1, C) for a in (w, g, m, v))
    R = w2.shape[0]
    tr = R if R <= 512 else _tile(R, 256, 8)
    if tr < 64:
        tr = R

    def body(w_ref, g_ref, m_ref, v_ref, *rest):
        d_ref, nm_ref, nv_ref = rest[-3:] if recv is None else rest[1:4]
        gv = g_ref[...]
        if recv is not None:
            r_ref, gs_ref = rest[0], rest[4]
            gv = ((gv + r_ref[0].astype(F32)) + r_ref[1].astype(F32)) + r_ref[2].astype(F32)
            gs_ref[...] = gv
        mv = ADAM_B1 * m_ref[...] + (1.0 - ADAM_B1) * gv
        vv = ADAM_B2 * v_ref[...] + (1.0 - ADAM_B2) * (gv * gv)
        m_hat = mv / (1.0 - ADAM_B1 ** ADAM_STEP)
        v_hat = vv / (1.0 - ADAM_B2 ** ADAM_STEP)
        d_ref[...] = -ADAM_LR * (m_hat / (jnp.sqrt(v_hat) + ADAM_EPS) + ADAM_WD * w_ref[...])
        nm_ref[...] = mv
        nv_ref[...] = vv

    spec = pl.BlockSpec((tr, C), lambda r: (r, 0))
    extra = [] if recv is None else [pl.BlockSpec((3, tr, C), lambda r: (0, r, 0))]
    n_out = 3 if recv is None else 4
    outs = pl.pallas_call(
        body, name=name, grid=(R // tr,),
        in_specs=[spec] * 4 + extra, out_specs=[spec] * n_out, out_shape=[SDS((R, C), F32)] * n_out,
        compiler_params=_cparams(1),
    )(w2, g2, m2, v2, *([] if recv is None else [recv]))
    return tuple(o.reshape(shape) for o in outs)


def _rows(a, L):
    lead = a.shape[0]
    flat = a.reshape(lead, -1)
    n = flat.shape[1]
    r = -(-n // L)
    return jnp.pad(flat, ((0, 0), (0, r * L - n))).reshape(lead, r, L)


def _unrows(p, shape):
    return p.reshape(-1)[:int(np.prod(shape))].reshape(shape)


def _from_col_blocks(a):
    n, R, c = a.shape
    return a.transpose(1, 0, 2).reshape(R, n * c)


def _in_windows(n_loc, nqkv, H, NM):
    win = (n_loc + 7 + 7) // 8 * 8
    starts, index = [], np.zeros((N_DEV, n_loc), np.int32)
    for d in range(N_DEV):
        rows = np.arange(n_loc * d, n_loc * (d + 1))
        is_f = (rows >= nqkv) & (rows < nqkv + H)
        kept = np.where(rows < nqkv, rows, rows - H)
        lo = int(kept[~is_f].min())
        start = lo // 8 * 8
        assert int(kept[~is_f].max()) - start < win and start + win <= NM + LANES
        starts.append(start)
        index[d] = np.where(is_f, win + rows - nqkv, kept - start)
    return starts, win, index


def kernel(x, norm_mix_g, w_in, b_f, b_gate, conv_mix_w, w_out_conv, w_out_attn, w_o, norm_ffn_g, w_up, conv_ffn_w, w_down, norm_f_g, loss_target, m_norm_mix_g, m_w_in, m_b_f, m_b_gate, m_conv_mix_w, m_w_out_conv, m_w_out_attn, m_w_o, m_norm_ffn_g, m_w_up, m_conv_ffn_w, m_w_down, m_norm_f_g, v_norm_mix_g, v_w_in, v_b_f, v_b_gate, v_conv_mix_w, v_w_out_conv, v_w_out_attn, v_w_o, v_norm_ffn_g, v_w_up, v_conv_ffn_w, v_w_down, v_norm_f_g):
    Bl, S, D = x.shape
    T = Bl * S
    H = b_f.shape[-1]
    CW = N_DEV * conv_mix_w.shape[-1]
    AW = w_out_attn.shape[1]
    hd = AW // H
    FH = N_DEV * w_down.shape[1]
    n_loc = w_in.shape[-1]
    NIN = N_DEV * n_loc
    NM = 3 * CW + 3 * AW + 2 * D
    nqkv = 3 * CW + 3 * AW
    assert NIN == NM + H and w_out_conv.shape[1] == CW and nqkv % D == 0 and AW % LANES == 0 and LANES % hd == 0
    hpg, ng = LANES // hd, AW // LANES
    assert hpg <= 8
    tm_big = min(512, S // 2)
    tm_ffn = min(256, S // 2)
    tq_fwd = min(512, S // 2)
    tq_bwd = min(256, S // 2)
    px, py, pc = _place()
    me = 4 * px + 2 * py + pc

    bits = lambda a: lax.bitcast_convert_type(a, BF16)
    taps = jnp.concatenate([_rows(bits(conv_ffn_w[0])[None], D)[0], _rows(bits(conv_mix_w[0])[None], D)[0]], axis=0)
    n_ffn_rows = -(-conv_ffn_w[0].size * 2 // D)
    late = [w_up[0].T.astype(BF16), w_down[0].astype(BF16), w_o[0].astype(BF16), w_out_conv[0].T.astype(BF16),
            w_out_attn[0].T.astype(BF16), taps]
    g_in, = _all_gather([w_in[0].T.astype(BF16)], "weights_all_gather")
    W_in_rows = g_in.reshape(NIN, D)
    W_in_t = jnp.concatenate([W_in_rows[:nqkv], W_in_rows[nqkv + H:], W_in_rows[nqkv:nqkv + H],
                              jnp.zeros((LANES - H, D), BF16)], axis=0)
    bf128 = jnp.pad(b_f, ((0, 0), (0, LANES - H)))

    x2d = x.reshape(T, D)
    tgt = loss_target.reshape(T, D)
    pm, fl, h1 = _inproj_fwd(x2d, norm_mix_g, W_in_t, NM, tm_big)
    fcum = _forget_cumsum(fl, bf128, S)
    frow = jnp.pad(fcum[:, :H].reshape(Bl, S, ng, hpg).transpose(0, 2, 3, 1), ((0, 0), (0, 0), (0, 8 - hpg), (0, 0)))
    q_off = 3 * CW // LANES
    o, lse, g_up, g_dn, g_o, g_oc, g_oa, g_taps = _attn_fwd(pm, fcum, frow, S, q_off, AW, hd, tq_fwd, late)
    W_up_t = g_up.reshape(2 * FH, D)
    W_dn = g_dn.reshape(FH, D)
    W_o = g_o.reshape(D, D)
    W_oc_t = g_oc.reshape(D, CW)
    W_oa_t = g_oa.reshape(D, AW)
    tap_bits = g_taps.reshape(N_DEV, -1)
    n_ffn, n_mix = conv_ffn_w[0].size * 2, conv_mix_w[0].size * 2
    wf_full = _from_col_blocks(lax.bitcast_convert_type(
        tap_bits[:, :n_ffn].reshape((N_DEV,) + conv_ffn_w.shape[1:] + (2,)), F32))
    wc_full = _from_col_blocks(lax.bitcast_convert_type(
        tap_bits[:, n_ffn_rows * D:n_ffn_rows * D + n_mix].reshape((N_DEV,) + conv_mix_w.shape[1:] + (2,)), F32))
    wf8 = jnp.pad(wf_full, ((0, 5), (0, 0)))
    wc8 = jnp.pad(wc_full, ((0, 5), (0, 0)))
    x2, u, m, ycin, yc, ya = _merge_fwd(pm, o, x2d, wc8, b_gate, W_oc_t, W_oa_t, W_o, S, tm_big)
    upp, up, act, h2, dx3, loss8, dgf8 = _ffn_fwd(x2, norm_ffn_g, W_up_t, wf8, W_dn, norm_f_g.reshape(1, D), tgt, S, tm_ffn)

    dx2, dupp, dwf8, dg2_8 = _ffn_bwd(dx3, x2, norm_ffn_g, upp, up, wf8, W_dn, W_up_t, S, tm_ffn)
    dW_dn = _wgrad(act, dx3, "wgrad_down")
    dW_up_t = _wgrad(dupp, h2, "wgrad_up")
    ids = jnp.stack([pc, 2 * px + py]).astype(jnp.int32)
    big = [dW_up_t.reshape(4, 2, -1, D), dW_dn.reshape(4, 2, -1, D)]
    dconv, dgl, do, dW_small, dwc8, dbg8, *sib_big = _merge_bwd(
        dx2, pm, u, yc, ya, m, ycin, o, wc8, b_gate, W_o, W_oc_t, W_oa_t, S, tm_big, big)
    big_sums = [_pair_sum(b, r, ids, "grads_pair_sum_%d" % a) for a, (b, r) in enumerate(zip(big, sib_big))]
    dq, dk, dv, dfr, *chips_big = _attn_bwd(pm, do, fcum, frow, lse, S, q_off, AW, hd, tq_bwd, [s[1] for s in big_sums])
    dfc = jnp.pad(dfr[:, :, :hpg, :].transpose(0, 3, 1, 2).reshape(T, H), ((0, 0), (0, LANES - H)))
    dfl, dbf8 = _forget_bwd(dfc, fl, bf128, S)
    dparts = [dconv, dq, dk, dv, dgl, dfl]
    offs = [0, 3 * CW, 3 * CW + AW, 3 * CW + 2 * AW, nqkv, NM]
    dW_in_t = _wgrad(dconv, h1, "wgrad_in_conv", total_rows=NM + LANES)
    dW_in_t = _wgrad([dq, dk, dv], h1, "wgrad_in_qkv", into=dW_in_t, row_off=3 * CW)
    dW_in_t = _wgrad(dgl, h1, "wgrad_in_gates", into=dW_in_t, row_off=nqkv)

    starts, win, index = _in_windows(n_loc, nqkv, H, NM)
    small = dW_small.reshape(4, 2, -1, D)
    my_starts = [jnp.where(pc == 0, starts[2 * j], starts[2 * j + 1]) for j in range(4)]
    win_ids = jnp.stack([pc, 2 * px + py] + my_starts).astype(jnp.int32)
    sib_win, sib_small = _sibling_exchange(dW_in_t, starts, win, [small], "grads_sibling_exchange")
    sums = [_pair_sum(dW_in_t, sib_win, win_ids, "grads_pair_sum_in", win_rows=win),
            _pair_sum(small, sib_small, ids, "grads_pair_sum_small")]
    grad_x, dg1_8, dW_forget, *from_chips = _inproj_bwd(dparts, offs, W_in_t, x2d, norm_mix_g, dx2, tm_big, [s[1] for s in sums])
    red_win, red_small = [_final_sum(s[0], r, "grads_final_sum_%d" % a) for a, (s, r) in enumerate(zip(sums, from_chips))]
    (own_up, _), (own_dn, _) = big_sums
    from_up, from_dn = chips_big

    f_rows = dW_forget[0:8]
    wf_rows = _rows(dwf8[:3].reshape(1, -1), D)[0]
    wc_rows = _rows(dwc8[:3].reshape(1, -1), D)[0]
    smalls = [dg1_8[0:1], dg2_8[0:1], dgf8[0:1], dbg8[0:1, :D], dbg8[0:1, D:], jnp.pad(dbf8[0:1], ((0, 0), (0, D - LANES))),
              jnp.pad(loss8[0:1], ((0, 0), (0, D - LANES))), jnp.zeros((1, D), F32), f_rows, wf_rows, wc_rows]
    spack = jnp.concatenate(smalls, axis=0)
    spack = jnp.pad(spack, ((0, -spack.shape[0] % 8), (0, 0)))
    ssum = _sum8(_all_gather([spack], "small_all_gather")[0], "small_sum")
    g_g1, g_g2, g_gf = ssum[0:1], ssum[1:2], ssum[2]
    g_bg = jnp.concatenate([ssum[3:4], ssum[4:5]], axis=1)
    g_bf = ssum[5:6, :H]
    r0 = 16
    r1 = r0 + wf_rows.shape[0]
    wf_sum = _unrows(ssum[r0:r1], (3, 2 * FH))
    wc_sum = _unrows(ssum[r1:r1 + wc_rows.shape[0]], (3, CW))
    g_wf = lax.dynamic_slice_in_dim(wf_sum, me * conv_ffn_w.shape[-1], conv_ffn_w.shape[-1], axis=1)
    g_wc = lax.dynamic_slice_in_dim(wc_sum, me * conv_mix_w.shape[-1], conv_mix_w.shape[-1], axis=1)

    ext = jnp.concatenate([red_win, ssum[8:8 + H]], axis=0)
    my_index = lax.dynamic_index_in_dim(jnp.asarray(index), me, axis=0, keepdims=False)
    g_w_in_t = jnp.take(ext, my_index, axis=0)
    rb = D // N_DEV
    g_w_o = red_small[:rb]
    g_w_oc = red_small[rb:, :CW].T
    g_w_oa = red_small[rb:, CW:].T

    loss = ssum[6, 0]

    names = ["norm_mix_g", "w_in", "b_f", "b_gate", "conv_mix_w", "w_out_conv", "w_out_attn", "w_o", "norm_ffn_g", "w_up",
             "conv_ffn_w", "w_down", "norm_f_g"]
    weights = [norm_mix_g, w_in, b_f, b_gate, conv_mix_w, w_out_conv, w_out_attn, w_o, norm_ffn_g, w_up, conv_ffn_w, w_down, norm_f_g]
    grads = [g_g1, g_w_in_t, g_bf, g_bg, g_wc, g_w_oc, g_w_oa, g_w_o, g_g2, own_up, g_wf, own_dn, g_gf]
    ms = [m_norm_mix_g, m_w_in, m_b_f, m_b_gate, m_conv_mix_w, m_w_out_conv, m_w_out_attn, m_w_o, m_norm_ffn_g, m_w_up,
          m_conv_ffn_w, m_w_down, m_norm_f_g]
    vs = [v_norm_mix_g, v_w_in, v_b_f, v_b_gate, v_conv_mix_w, v_w_out_conv, v_w_out_attn, v_w_o, v_norm_ffn_g, v_w_up,
          v_conv_ffn_w, v_w_down, v_norm_f_g]
    to_view = {"w_in": lambda a: a[0].T.reshape(-1, LANES), "w_up": lambda a: a[0].T}
    from_view = {"w_in": lambda a: a.reshape(n_loc, D).T[None], "w_up": lambda a: a.T[None]}
    received = {"w_up": from_up, "w_down": from_dn}
    out_grads, steps = [], []
    for nm, w, g, mm, vv in zip(names, weights, grads, ms, vs):
        view, back = to_view.get(nm, lambda a: a), from_view.get(nm, lambda a: a)
        wv, mv, vw = (view(a) for a in (w, mm, vv))
        gv = g.reshape(wv.shape)
        res = _adamw(wv, gv, mv, vw, "adamw_" + nm, recv=received.get(nm))
        if nm in received:
            gv = res[3]
        steps.append(tuple(back(o) for o in res[:3]))
        out_grads.append(back(gv))
    deltas, new_ms, new_vs = zip(*steps)
    return (loss, grad_x.reshape(Bl, S, D), *out_grads, *deltas, *new_ms, *new_vs)
```

```python
import numpy as np

import jax
import jax.numpy as jnp
from jax import lax
from jax.experimental import pallas as pl
from jax.experimental.pallas import tpu as pltpu

F32, BF16 = jnp.float32, jnp.bfloat16
EPS = 1e-6
ADAM_LR, ADAM_B1, ADAM_B2, ADAM_EPS, ADAM_WD, ADAM_STEP = 0.001, 0.9, 0.999, 1e-08, 0.01, 10
N_DEV = 8
LANES = 128
V7X_VMEM_LIMIT = 56 * 1024 * 1024
FFN_CHUNK = 2816
WGRAD_VMEM_BUDGET = 40 * 1024 * 1024
MESH = pl.DeviceIdType.MESH
SDS = jax.ShapeDtypeStruct
ANY = pl.BlockSpec(memory_space=pl.ANY)


def _tile(n, target, mult=LANES):
    best = None
    for t in range(mult, min(n, target) + 1, mult):
        if n % t == 0:
            best = t
    return best if best is not None else n


def _resident(shape):
    return pl.BlockSpec(shape, lambda *_: (0,) * len(shape), pipeline_mode=pl.Buffered(1))


def _cparams(n_axes=1):
    return pltpu.CompilerParams(dimension_semantics=("arbitrary",) * n_axes, vmem_limit_bytes=V7X_VMEM_LIMIT)


def _dot(a, b):
    return jnp.dot(a, b, preferred_element_type=F32)


def _dot_tn(a, b):
    return lax.dot_general(a, b, (((0,), (0,)), ((), ())), preferred_element_type=F32)


def _dot_nt(a, b):
    return lax.dot_general(a, b, (((1,), (1,)), ((), ())), preferred_element_type=F32)


def _sigmoid(x):
    return 0.5 * jnp.tanh(0.5 * x) + 0.5


def _rms(x):
    return lax.rsqrt(jnp.mean(x * x, axis=-1, keepdims=True) + EPS)


def _taps_back(z, r6, r7):
    row = lax.broadcasted_iota(jnp.int32, (8, 1), 0)
    z1, z2 = pltpu.roll(z, 1, 0), pltpu.roll(z, 2, 0)
    z1 = jnp.concatenate([jnp.where(row == 0, r7, z1[0:8]), z1[8:]], axis=0)
    z2 = jnp.concatenate([jnp.where(row == 0, r6, jnp.where(row == 1, r7, z2[0:8])), z2[8:]], axis=0)
    return z1, z2


def _taps_ahead(d, h0, h1):
    tm = d.shape[0]
    row = lax.broadcasted_iota(jnp.int32, (8, 1), 0)
    d1, d2 = pltpu.roll(d, tm - 1, 0), pltpu.roll(d, tm - 2, 0)
    d1 = jnp.concatenate([d1[:tm - 8], jnp.where(row == 7, h0, d1[tm - 8:])], axis=0)
    d2 = jnp.concatenate([d2[:tm - 8], jnp.where(row == 6, h0, jnp.where(row == 7, h1, d2[tm - 8:]))], axis=0)
    return d1, d2


def _tri_dot(tri, x):
    hi = x.astype(BF16)
    r = x - hi.astype(F32)
    mid = r.astype(BF16)
    lo = (r - mid.astype(F32)).astype(BF16)
    return (_dot(tri, lo) + _dot(tri, mid)) + _dot(tri, hi)


def _lane_pick(block, lane):
    lanes = lax.broadcasted_iota(jnp.int32, (1, block.shape[1]), 1)
    return jnp.sum(jnp.where(lanes == lane, block, 0.0), axis=1, keepdims=True)


def _place():
    return lax.axis_index("x"), lax.axis_index("y"), lax.axis_index("c")


def _all_gather(xs, name):
    n = len(xs)

    def body(*refs):
        start, forward, finish = _gather_phases(refs[:n], refs[n:2 * n], *refs[2 * n:])
        start()
        forward()
        finish()

    return pl.pallas_call(
        body, name=name,
        out_shape=_gather_shapes(xs), in_specs=[ANY] * n, out_specs=[ANY] * n, scratch_shapes=_gather_sems(n),
    )(*xs)


def _gather_shapes(xs):
    return [SDS((N_DEV,) + x.shape, x.dtype) for x in xs]


def _gather_sems(n):
    return [pltpu.SemaphoreType.DMA((7 * n,)), pltpu.SemaphoreType.DMA((7 * n,)), pltpu.SemaphoreType.DMA((n,))]


def _gather_phases(x_refs, out_refs, send_sems, recv_sems, local_sems):
    n = len(x_refs)

    def parts():
        px, py, pc = _place()
        me, sibling = (px, py, pc), (px, py, 1 - pc)
        chips = [(1 - px, py), (px, 1 - py), (1 - px, 1 - py)]

        def slot(a, qx, qy, qc):
            return out_refs[a].at[4 * qx + 2 * qy + qc]

        def copy(a, k, block, to, src=None):
            return pltpu.make_async_remote_copy(
                src_ref=slot(a, *block) if src is None else src, dst_ref=slot(a, *block),
                send_sem=send_sems.at[7 * a + k], recv_sem=recv_sems.at[7 * a + k], device_id=to, device_id_type=MESH)

        def mine():
            return [pltpu.make_async_copy(x_refs[a], slot(a, *me), local_sems.at[a]) for a in range(n)]

        def first():
            out = []
            for a in range(n):
                out.append(copy(a, 0, me, sibling, src=x_refs[a]))
                out += [copy(a, 1 + j, me, (*chip, pc), src=x_refs[a]) for j, chip in enumerate(chips)]
            return out

        def landed():
            return [copy(a, 1 + j, (*chip, pc), me) for j, chip in enumerate(chips) for a in range(n)]

        def passed():
            return [copy(a, 4 + j, (*chip, pc), sibling) for j, chip in enumerate(chips) for a in range(n)]

        def late():
            out = [copy(a, 0, sibling, me) for a in range(n)]
            return out + [copy(a, 4 + j, (*chip, 1 - pc), me) for j, chip in enumerate(chips) for a in range(n)]

        return mine, first, landed, passed, late

    def start():
        mine, first, _, _, _ = parts()
        for cp in mine() + first():
            cp.start()

    def forward():
        _, _, landed, passed, _ = parts()
        for got, cp in zip(landed(), passed()):
            got.wait_recv()
            cp.start()

    def finish():
        mine, first, _, passed, late = parts()
        for cp in late():
            cp.wait_recv()
        for cp in first() + passed():
            cp.wait_send()
        for cp in mine():
            cp.wait()

    return start, forward, finish


def _sibling_exchange(win_buf, win_starts, win_rows, blocked, name):
    nb = len(blocked)

    def body(*refs):
        win_ref, blk_refs = refs[0], refs[1:1 + nb]
        rwin_ref, rblk_refs = refs[1 + nb], refs[2 + nb:2 + 2 * nb]
        send_sems, recv_sems, wsend_sems, wrecv_sems = refs[2 + 2 * nb:]
        px, py, pc = _place()
        copies = _sibling_copies(blk_refs, rblk_refs, send_sems, recv_sems)
        for j in range(4):
            theirs = jnp.where(pc == 0, win_starts[2 * j + 1], win_starts[2 * j])
            copies.append(pltpu.make_async_remote_copy(
                src_ref=win_ref.at[pl.ds(pl.multiple_of(theirs, 8), win_rows)], dst_ref=rwin_ref.at[j],
                send_sem=wsend_sems.at[j], recv_sem=wrecv_sems.at[j], device_id=(px, py, 1 - pc), device_id_type=MESH))
        for cp in copies:
            cp.start()
        for cp in copies:
            cp.wait()

    C = win_buf.shape[1]
    return pl.pallas_call(
        body, name=name,
        out_shape=[SDS((4, win_rows, C), F32)] + _sibling_shapes(blocked),
        in_specs=[ANY] * (1 + nb), out_specs=[ANY] * (1 + nb),
        scratch_shapes=_exchange_sems(nb) + _exchange_sems(4),
    )(win_buf, *blocked)


def _sibling_shapes(blocked):
    return [SDS((4,) + b.shape[2:], F32) for b in blocked]


def _exchange_sems(n):
    return [pltpu.SemaphoreType.DMA((n,)), pltpu.SemaphoreType.DMA((n,))]


def _sibling_copies(blk_refs, out_refs, send_sems, recv_sems):
    px, py, pc = _place()
    return [pltpu.make_async_remote_copy(
        src_ref=b.at[:, 1 - pc], dst_ref=o, send_sem=send_sems.at[a], recv_sem=recv_sems.at[a],
        device_id=(px, py, 1 - pc), device_id_type=MESH) for a, (b, o) in enumerate(zip(blk_refs, out_refs))]


def _chip_shapes(ps):
    return [SDS((3,) + p.shape[1:], p.dtype) for p in ps]


def _chip_copies(p_refs, out_refs, send_sems, recv_sems):
    px, py, pc = _place()
    n = len(p_refs)
    chips = [(1 - px, py), (px, 1 - py), (1 - px, 1 - py)]
    return [pltpu.make_async_remote_copy(
        src_ref=p_refs[a].at[2 * qx + qy], dst_ref=out_refs[a].at[k],
        send_sem=send_sems.at[3 * a + k], recv_sem=recv_sems.at[3 * a + k],
        device_id=(qx, qy, pc), device_id_type=MESH) for k, (qx, qy) in enumerate(chips) for a in range(n)]


def _pair_sum(own, recv, ids, name, win_rows=None):
    _, R, C = recv.shape
    tr = _tile(R, 512, 8)

    def body(ids_ref, g_ref, r_ref, own_ref, pb_ref):
        s = g_ref[...] + r_ref[...]
        pb_ref[...] = s.astype(BF16)

        @pl.when(pl.program_id(1) == ids_ref[1])
        def _():
            own_ref[...] = s

    if win_rows is None:
        own_spec = pl.BlockSpec((None, None, tr, C), lambda r, j, ids: (j, ids[0], r, 0))
    else:
        own_spec = pl.BlockSpec((pl.Element(tr), pl.Element(C)), lambda r, j, ids: (pl.multiple_of(ids[2 + j] + r * tr, 8), 0))
    return pl.pallas_call(
        body, name=name,
        grid_spec=pltpu.PrefetchScalarGridSpec(
            num_scalar_prefetch=1, grid=(R // tr, 4),
            in_specs=[own_spec, pl.BlockSpec((None, tr, C), lambda r, j, ids: (j, r, 0))],
            out_specs=[pl.BlockSpec((tr, C), lambda r, j, ids: (r, 0)),
                       pl.BlockSpec((None, tr, C), lambda r, j, ids: (j, r, 0))]),
        out_shape=[SDS((R, C), F32), SDS((4, R, C), BF16)],
        compiler_params=_cparams(2),
    )(ids, own, recv)


def _final_sum(own, recv, name):
    R, C = own.shape
    tr = _tile(R, 512, 8)

    def body(o_ref, r_ref, out_ref):
        out_ref[...] = ((o_ref[...] + r_ref[0].astype(F32)) + r_ref[1].astype(F32)) + r_ref[2].astype(F32)

    return pl.pallas_call(
        body, name=name, grid=(R // tr,),
        in_specs=[pl.BlockSpec((tr, C), lambda r: (r, 0)), pl.BlockSpec((3, tr, C), lambda r: (0, r, 0))],
        out_specs=pl.BlockSpec((tr, C), lambda r: (r, 0)),
        out_shape=SDS((R, C), F32),
        compiler_params=_cparams(1),
    )(own, recv)


def _sum8(a, name):
    def body(a_ref, out_ref):
        s = a_ref[0]
        for d in range(1, N_DEV):
            s = s + a_ref[d]
        out_ref[...] = s

    return pl.pallas_call(body, name=name, out_shape=SDS(a.shape[1:], F32))(a)


def _inproj_fwd(x, g1, w_t, NM, tm):
    T, D = x.shape
    NF = w_t.shape[0] - NM
    ch = _tile(NM, 1024)

    def body(x_ref, g_ref, w_ref, pm_ref, fl_ref, h_ref):
        xv = x_ref[...]
        h = (xv * _rms(xv) * g_ref[...]).astype(BF16)
        h_ref[...] = h
        for c in range(0, NM, ch):
            pm_ref[:, c:c + ch] = _dot_nt(h, w_ref[c:c + ch, :]).astype(BF16)
        fl_ref[...] = _dot_nt(h, w_ref[NM:NM + NF, :])

    return pl.pallas_call(
        body, name="inproj_fwd", grid=(T // tm,),
        in_specs=[pl.BlockSpec((tm, D), lambda i: (i, 0)), _resident((1, D)), _resident(w_t.shape)],
        out_specs=[pl.BlockSpec((tm, NM), lambda i: (i, 0)), pl.BlockSpec((tm, NF), lambda i: (i, 0)),
                   pl.BlockSpec((tm, D), lambda i: (i, 0))],
        out_shape=[SDS((T, NM), BF16), SDS((T, NF), F32), SDS((T, D), BF16)],
        compiler_params=_cparams(1),
    )(x, g1, w_t)


def _log_sigmoid(x):
    return jnp.minimum(x, 0.0) - jnp.log(1.0 + jnp.exp(-jnp.abs(x)))


def _forget_cumsum(fl, bf, S):
    T, NF = fl.shape
    ch = _tile(S, 256, 8)

    def body(fl_ref, bf_ref, f_ref):
        row = lax.broadcasted_iota(jnp.int32, (ch, ch), 0)
        col = lax.broadcasted_iota(jnp.int32, (ch, ch), 1)
        tri = (col <= row).astype(BF16)
        carry = jnp.zeros((1, NF), F32)
        for c in range(0, S, ch):
            lf = _log_sigmoid(fl_ref[c:c + ch, :] + bf_ref[...])
            f_ref[c:c + ch, :] = _tri_dot(tri, lf) + carry
            carry = carry + jnp.sum(lf, axis=0, keepdims=True)

    return pl.pallas_call(
        body, name="forget_cumsum", grid=(T // S,),
        in_specs=[pl.BlockSpec((S, NF), lambda b: (b, 0)), _resident((1, NF))],
        out_specs=pl.BlockSpec((S, NF), lambda b: (b, 0)),
        out_shape=SDS((T, NF), F32),
        compiler_params=_cparams(1),
    )(fl, bf)


def _head_mask(e, hd):
    lanes = lax.broadcasted_iota(jnp.int32, (1, LANES), 1)
    return (lanes >= e * hd) & (lanes < (e + 1) * hd)


def _attn_specs(S, q_off, AW):
    ng = AW // LANES
    return [pl.BlockSpec((S, LANES), lambda b, g, o=q_off + w * ng: (b, o + g)) for w in range(3)]


def _attn_fwd(pm, fcum, frow, S, q_off, AW, hd, tq, gather):
    T = pm.shape[0]
    scale = float(hd) ** -0.5
    nq, hpg = S // tq, LANES // hd
    ng, nx = AW // LANES, len(gather)
    steps = (T // S) * ng

    def body(q_ref, k_ref, v_ref, fc_ref, fr_ref, *rest):
        x_refs, (o_ref, lse_ref), out_refs, sems = rest[:nx], rest[nx:nx + 2], rest[nx + 2:2 * nx + 2], rest[2 * nx + 2:]
        g = pl.program_id(1)
        step = pl.program_id(0) * ng + g
        start, forward, finish = _gather_phases(x_refs, out_refs, *sems)
        pl.when(step == 0)(start)

        @pl.when(g == 0)
        def _():
            lse_ref[...] = jnp.zeros_like(lse_ref)

        lanes = lax.broadcasted_iota(jnp.int32, (1, LANES), 1)
        for i in range(nq):
            rs, kend = slice(i * tq, (i + 1) * tq), (i + 1) * tq
            row = i * tq + lax.broadcasted_iota(jnp.int32, (tq, kend), 0)
            col = lax.broadcasted_iota(jnp.int32, (tq, kend), 1)
            o_tile = jnp.zeros((tq, LANES), F32)
            lse_tile = lse_ref[rs, :]
            for e in range(hpg):
                mask = _head_mask(e, hd)
                qs = jnp.where(mask, q_ref[rs, :], 0) * scale
                s = _dot_nt(qs, k_ref[0:kend, :]) + _lane_pick(fc_ref[rs, :], g * hpg + e) - fr_ref[e:e + 1, 0:kend]
                s = jnp.where(col <= row, s, -1e30)
                m = jnp.max(s, axis=1, keepdims=True)
                p = jnp.exp(s - m)
                l = jnp.sum(p, axis=1, keepdims=True)
                o_tile = jnp.where(mask, _dot(p.astype(BF16), v_ref[0:kend, :]) / l, o_tile)
                lse_tile = jnp.where(lanes == g * hpg + e, m + jnp.log(l), lse_tile)
            o_ref[rs, :] = o_tile.astype(BF16)
            lse_ref[rs, :] = lse_tile
        pl.when(step == (steps * 5) // 8)(forward)
        pl.when(step == steps - 1)(finish)

    full = pl.BlockSpec((S, LANES), lambda b, g: (b, 0))
    return pl.pallas_call(
        body, name="attn_fwd", grid=(T // S, ng),
        in_specs=_attn_specs(S, q_off, AW) + [full, pl.BlockSpec((None, None, 8, S), lambda b, g: (b, g, 0, 0))] + [ANY] * nx,
        out_specs=[pl.BlockSpec((S, LANES), lambda b, g: (b, g)), full] + [ANY] * nx,
        out_shape=[SDS((T, AW), BF16), SDS((T, LANES), F32)] + _gather_shapes(gather),
        scratch_shapes=_gather_sems(nx),
        compiler_params=_cparams(2),
    )(pm, pm, pm, fcum, frow, *gather)


def _merge_fwd(pm, o, x, wc, bg, woc_t, woa_t, wo, S, tm):
    T, D = x.shape
    CW, AW = woc_t.shape[1], woa_t.shape[1]
    tps = S // tm

    def body(cb_ref, cc_ref, cin_ref, gc_ref, ga_ref, o_ref, x_ref, wc_ref, bg_ref, woct_ref, woat_ref, wo_ref,
             x2_ref, u_ref, m_ref, ycin_ref, tail_ref):
        @pl.when(pl.program_id(0) % tps == 0)
        def _():
            tail_ref[...] = jnp.zeros_like(tail_ref)

        z = cc_ref[...].astype(F32) * cin_ref[...].astype(F32)
        z1, z2 = _taps_back(z, tail_ref[6:7, :], tail_ref[7:8, :])
        tail_ref[...] = z[tm - 8:tm, :]
        u = (z * wc_ref[2:3, :] + z2 * wc_ref[0:1, :]) + z1 * wc_ref[1:2, :]
        u_ref[...] = u.astype(BF16)
        ycin = (cb_ref[...].astype(F32) * u).astype(BF16)
        ycin_ref[...] = ycin
        yc = _dot_nt(ycin, woct_ref[...])
        ya = _dot_nt(o_ref[...], woat_ref[...])
        gc = _sigmoid(gc_ref[...].astype(F32) + bg_ref[:, 0:D])
        ga = _sigmoid(ga_ref[...].astype(F32) + bg_ref[:, D:2 * D])
        m = (gc * yc + ga * ya).astype(BF16)
        m_ref[...] = m
        x2_ref[...] = x_ref[...] + _dot(m, wo_ref[...])

    g_off = (3 * CW + 3 * AW) // D
    tok = lambda w, j=0: pl.BlockSpec((tm, w), lambda i: (i, j))
    return pl.pallas_call(
        body, name="merge_fwd", grid=(T // tm,),
        in_specs=[tok(CW, 0), tok(CW, 1), tok(CW, 2), tok(D, g_off), tok(D, g_off + 1), tok(AW), tok(D),
                  _resident((8, CW)), _resident((1, 2 * D)), _resident((D, CW)), _resident((D, AW)), _resident((D, D))],
        out_specs=[tok(D), tok(CW), tok(D), tok(CW)],
        out_shape=[SDS((T, D), F32), SDS((T, CW), BF16), SDS((T, D), BF16), SDS((T, CW), BF16)],
        scratch_shapes=[pltpu.VMEM((8, CW), F32)],
        compiler_params=_cparams(1),
    )(pm, pm, pm, pm, pm, o, x, wc, bg, woc_t, woa_t, wo)


def _ffn_fwd(x2, g2, wup_t, wf, wdn, gf, tgt, S, tm):
    T, D = x2.shape
    FH = wdn.shape[0]
    ch = _tile(FH, FFN_CHUNK)
    tps = S // tm

    def body(x2_ref, g2_ref, wupt_ref, wf_ref, wdn_ref, gf_ref, tgt_ref,
             upp_ref, up_ref, act_ref, h2_ref, dx3_ref, loss_ref, dgf_ref, tail_ref):
        i = pl.program_id(0)

        @pl.when(i % tps == 0)
        def _():
            tail_ref[...] = jnp.zeros_like(tail_ref)

        @pl.when(i == 0)
        def _():
            loss_ref[...] = jnp.zeros_like(loss_ref)
            dgf_ref[...] = jnp.zeros_like(dgf_ref)

        x2v = x2_ref[...]
        h2 = (x2v * _rms(x2v) * g2_ref[...]).astype(BF16)
        h2_ref[...] = h2
        x3 = x2v
        for c in range(0, FH, ch):
            gated = []
            for cols in (slice(c, c + ch), slice(FH + c, FH + c + ch)):
                upp = _dot_nt(h2, wupt_ref[cols, :])
                upp_ref[:, cols] = upp.astype(BF16)
                p1, p2 = _taps_back(upp, tail_ref[6:7, cols], tail_ref[7:8, cols])
                tail_ref[:, cols] = upp[tm - 8:tm, :]
                up = (upp * wf_ref[2:3, cols] + p2 * wf_ref[0:1, cols]) + p1 * wf_ref[1:2, cols]
                up_ref[:, cols] = up.astype(BF16)
                gated.append(up)
            a, b = gated
            act = (a * _sigmoid(a) * b).astype(BF16)
            act_ref[:, c:c + ch] = act
            x3 = x3 + _dot(act, wdn_ref[c:c + ch, :])
        r3 = _rms(x3)
        xn3 = x3 * r3
        e = xn3 * gf_ref[...] - tgt_ref[...]
        loss_ref[...] += 0.5 * jnp.sum(jnp.mean(e * e, axis=-1, keepdims=True), axis=0, keepdims=True)
        dy = e / D
        dgf_ref[0:1, :] += jnp.sum(dy * xn3, axis=0, keepdims=True)
        dxn = dy * gf_ref[...]
        dx3_ref[...] = r3 * (dxn - xn3 * jnp.mean(dxn * xn3, axis=-1, keepdims=True))

    tok = lambda w: pl.BlockSpec((tm, w), lambda i: (i, 0))
    return pl.pallas_call(
        body, name="ffn_fwd", grid=(T // tm,),
        in_specs=[tok(D), _resident((1, D)), _resident((2 * FH, D)), _resident((8, 2 * FH)), _resident((FH, D)),
                  _resident((1, D)), tok(D)],
        out_specs=[tok(2 * FH), tok(2 * FH), tok(FH), tok(D), tok(D), pl.BlockSpec((8, LANES), lambda i: (0, 0)),
                   pl.BlockSpec((8, D), lambda i: (0, 0))],
        out_shape=[SDS((T, 2 * FH), BF16), SDS((T, 2 * FH), BF16), SDS((T, FH), BF16), SDS((T, D), BF16), SDS((T, D), F32),
                   SDS((8, LANES), F32), SDS((8, D), F32)],
        scratch_shapes=[pltpu.VMEM((8, 2 * FH), F32)],
        compiler_params=_cparams(1),
    )(x2, g2, wup_t, wf, wdn, gf, tgt)


def _ffn_bwd(dx3, x2, g2, upp, up, wf, wdn, wup_t, S, tm):
    T, D = x2.shape
    FH = wdn.shape[0]
    ch = _tile(FH, FFN_CHUNK)
    n, tps = T // tm, S // tm

    def body(dx3_ref, x2_ref, g2_ref, upp_ref, up_ref, wf_ref, wdn_ref, wupt_ref,
             dx2_ref, dupp_ref, dwf_ref, dg2_ref, head_ref):
        i = pl.program_id(0)
        t = n - 1 - i

        @pl.when(i == 0)
        def _():
            dwf_ref[...] = jnp.zeros_like(dwf_ref)
            dg2_ref[...] = jnp.zeros_like(dg2_ref)

        @pl.when(t % tps == tps - 1)
        def _():
            head_ref[...] = jnp.zeros_like(head_ref)

        dx3v = dx3_ref[...]
        dx3b = dx3v.astype(BF16)
        x2v = x2_ref[...]
        r2 = _rms(x2v)
        xn2 = x2v * r2
        dh2 = jnp.zeros((tm, D), F32)
        for c in range(0, FH, ch):
            ca, cb = slice(c, c + ch), slice(FH + c, FH + c + ch)
            a, b = up_ref[:, ca].astype(F32), up_ref[:, cb].astype(F32)
            sig = _sigmoid(a)
            sl = a * sig
            dact = _dot_nt(dx3b, wdn_ref[ca, :])
            grads = (dact * b * (sig * (1.0 + a * (1.0 - sig))), dact * sl)
            for cols, d in zip((ca, cb), grads):
                u0 = upp_ref[:, cols].astype(F32)
                d1, d2 = _taps_ahead(d, head_ref[0:1, cols], head_ref[1:2, cols])
                head_ref[:, cols] = d[0:8, :]
                dwf_ref[2:3, cols] += jnp.sum(u0 * d, axis=0, keepdims=True)
                dwf_ref[1:2, cols] += jnp.sum(u0 * d1, axis=0, keepdims=True)
                dwf_ref[0:1, cols] += jnp.sum(u0 * d2, axis=0, keepdims=True)
                dpre = ((d * wf_ref[2:3, cols] + d1 * wf_ref[1:2, cols]) + d2 * wf_ref[0:1, cols]).astype(BF16)
                dupp_ref[:, cols] = dpre
                dh2 = dh2 + _dot(dpre, wupt_ref[cols, :])
        dg2_ref[0:1, :] += jnp.sum(dh2 * xn2, axis=0, keepdims=True)
        dxn = dh2 * g2_ref[...]
        dx2_ref[...] = dx3v + r2 * (dxn - xn2 * jnp.mean(dxn * xn2, axis=-1, keepdims=True))

    tok = lambda w: pl.BlockSpec((tm, w), lambda i: (n - 1 - i, 0))
    acc = lambda w: pl.BlockSpec((8, w), lambda i: (0, 0))
    return pl.pallas_call(
        body, name="ffn_bwd", grid=(n,),
        in_specs=[tok(D), tok(D), _resident((1, D)), tok(2 * FH), tok(2 * FH), _resident((8, 2 * FH)),
                  _resident((FH, D)), _resident((2 * FH, D))],
        out_specs=[tok(D), tok(2 * FH), acc(2 * FH), acc(D)],
        out_shape=[SDS((T, D), F32), SDS((T, 2 * FH), BF16), SDS((8, 2 * FH), F32), SDS((8, D), F32)],
        scratch_shapes=[pltpu.VMEM((8, 2 * FH), F32)],
        compiler_params=_cparams(1),
    )(dx3, x2, g2, upp, up, wf, wdn, wup_t)


def _merge_bwd(dx2, pm, u, m, ycin, o, wc, bg, wo, woc_t, woa_t, S, tm, exchange):
    T, D = dx2.shape
    CW, AW = woc_t.shape[1], woa_t.shape[1]
    n, tps = T // tm, S // tm
    nx = len(exchange)
    rb = D // N_DEV
    assert CW + AW == D

    def body(dx2_ref, cb_ref, cc_ref, cin_ref, gc_ref, ga_ref, u_ref, m_ref, ycin_ref, o_ref,
             wc_ref, bg_ref, wo_ref, woct_ref, woat_ref, *rest):
        x_refs = rest[:nx]
        dconv_ref, dgl_ref, do_ref, dws_ref, dwc_ref, dbg_ref = rest[nx:nx + 6]
        got_refs, head_ref, sems = rest[nx + 6:2 * nx + 6], rest[2 * nx + 6], rest[2 * nx + 7:]

        def add_blocks(rows, cols, grad):
            for d in range(N_DEV):
                dws_ref[d, rows, cols] += grad[d * rb:(d + 1) * rb, :]

        i = pl.program_id(0)
        t = n - 1 - i

        @pl.when(i == 0)
        def _():
            for cp in _sibling_copies(x_refs, got_refs, *sems):
                cp.start()

        @pl.when(i == 0)
        def _():
            for ref in (dwc_ref, dbg_ref, dws_ref):
                ref[...] = jnp.zeros_like(ref)

        @pl.when(t % tps == tps - 1)
        def _():
            head_ref[...] = jnp.zeros_like(head_ref)

        dx2b = dx2_ref[...].astype(BF16)
        add_blocks(slice(0, rb), slice(0, D), _dot_tn(m_ref[...], dx2b))
        dm = _dot_nt(dx2b, wo_ref[...])
        outs = []
        branches = ((gc_ref, _dot_nt(ycin_ref[...], woct_ref[...]), slice(0, D)),
                    (ga_ref, _dot_nt(o_ref[...], woat_ref[...]), slice(D, 2 * D)))
        for g_ref, y, cols in branches:
            g = _sigmoid(g_ref[...].astype(F32) + bg_ref[:, cols])
            dgl = dm * y * g * (1.0 - g)
            dgl_ref[:, cols] = dgl.astype(BF16)
            dbg_ref[0:1, cols] += jnp.sum(dgl, axis=0, keepdims=True)
            outs.append((dm * g).astype(BF16))
        dyc, dya = outs
        add_blocks(slice(rb, 2 * rb), slice(0, CW), _dot_tn(dyc, ycin_ref[...]))
        add_blocks(slice(rb, 2 * rb), slice(CW, CW + AW), _dot_tn(dya, o_ref[...]))
        do_ref[...] = _dot(dya, woat_ref[...]).astype(BF16)
        dycin = _dot(dyc, woct_ref[...])
        cc, cin = cc_ref[...].astype(F32), cin_ref[...].astype(F32)
        z = cc * cin
        du = dycin * cb_ref[...].astype(F32)
        du1, du2 = _taps_ahead(du, head_ref[0:1, :], head_ref[1:2, :])
        head_ref[...] = du[0:8, :]
        dwc_ref[2:3, :] += jnp.sum(z * du, axis=0, keepdims=True)
        dwc_ref[1:2, :] += jnp.sum(z * du1, axis=0, keepdims=True)
        dwc_ref[0:1, :] += jnp.sum(z * du2, axis=0, keepdims=True)
        dz = (du * wc_ref[2:3, :] + du1 * wc_ref[1:2, :]) + du2 * wc_ref[0:1, :]
        dconv_ref[:, 0:CW] = (dycin * u_ref[...].astype(F32)).astype(BF16)
        dconv_ref[:, CW:2 * CW] = (dz * cin).astype(BF16)
        dconv_ref[:, 2 * CW:3 * CW] = (dz * cc).astype(BF16)

        @pl.when(i == n - 1)
        def _():
            for cp in _sibling_copies(x_refs, got_refs, *sems):
                cp.wait()

    g_off = (3 * CW + 3 * AW) // D
    tok = lambda w, j=0: pl.BlockSpec((tm, w), lambda i: (n - 1 - i, j))
    acc = lambda w: pl.BlockSpec((8, w), lambda i: (0, 0))
    return pl.pallas_call(
        body, name="merge_bwd", grid=(n,),
        in_specs=[tok(D), tok(CW, 0), tok(CW, 1), tok(CW, 2), tok(D, g_off), tok(D, g_off + 1), tok(CW),
                  tok(D), tok(CW), tok(AW),
                  _resident((8, CW)), _resident((1, 2 * D)), _resident((D, D)), _resident((D, CW)), _resident((D, AW))]
                 + [ANY] * nx,
        out_specs=[tok(3 * CW), tok(2 * D), tok(AW), pl.BlockSpec((N_DEV, 2 * rb, D), lambda i: (0, 0, 0)), acc(CW),
                   acc(2 * D)] + [ANY] * nx,
        out_shape=[SDS((T, 3 * CW), BF16), SDS((T, 2 * D), BF16), SDS((T, AW), BF16), SDS((N_DEV, 2 * rb, D), F32),
                   SDS((8, CW), F32), SDS((8, 2 * D), F32)] + _sibling_shapes(exchange),
        scratch_shapes=[pltpu.VMEM((8, CW), F32)] + _exchange_sems(nx),
        compiler_params=_cparams(1),
    )(dx2, pm, pm, pm, pm, pm, u, m, ycin, o, wc, bg, wo, woc_t, woa_t, *exchange)


def _attn_bwd(pm, do, fcum, frow, lse, S, q_off, AW, hd, tq, exchange):
    T = pm.shape[0]
    scale = float(hd) ** -0.5
    nq, hpg, ng = S // tq, LANES // hd, AW // LANES
    nx = len(exchange)
    steps = (T // S) * ng

    def body(q_ref, k_ref, v_ref, do_ref, fc_ref, fr_ref, lse_ref, *rest):
        x_refs, (dq_ref, dk_ref, dv_ref, dfr_ref) = rest[:nx], rest[nx:nx + 4]
        got_refs, (dk_acc, dv_acc), sems = rest[nx + 4:2 * nx + 4], rest[2 * nx + 4:2 * nx + 6], rest[2 * nx + 6:]
        g = pl.program_id(1)
        step = pl.program_id(0) * ng + g

        @pl.when(step == 0)
        def _():
            for cp in _chip_copies(x_refs, got_refs, *sems):
                cp.start()

        dk_acc[...] = jnp.zeros_like(dk_acc)
        dv_acc[...] = jnp.zeros_like(dv_acc)
        dfr_ref[...] = jnp.zeros_like(dfr_ref)
        for i in range(nq):
            rs, kend = slice(i * tq, (i + 1) * tq), (i + 1) * tq
            row = i * tq + lax.broadcasted_iota(jnp.int32, (tq, kend), 0)
            col = lax.broadcasted_iota(jnp.int32, (tq, kend), 1)
            kk, vv = k_ref[0:kend, :], v_ref[0:kend, :]
            dq_tile = jnp.zeros((tq, LANES), F32)
            for e in range(hpg):
                mask = _head_mask(e, hd)
                qs = jnp.where(mask, q_ref[rs, :], 0) * scale
                doi = jnp.where(mask, do_ref[rs, :], 0)
                s = _dot_nt(qs, kk) + _lane_pick(fc_ref[rs, :], g * hpg + e) - fr_ref[e:e + 1, 0:kend]
                p = jnp.where(col <= row, jnp.exp(s - _lane_pick(lse_ref[rs, :], g * hpg + e)), 0.0)
                dp = _dot_nt(doi, vv)
                ds = p * (dp - jnp.sum(p * dp, axis=1, keepdims=True))
                pb, dsb = p.astype(BF16), ds.astype(BF16)
                dq_tile = jnp.where(mask, _dot(dsb, kk) * scale, dq_tile)
                dv_acc[0:kend, :] += _dot_tn(pb, doi)
                dk_acc[0:kend, :] += _dot_tn(dsb, qs)
                dfr_ref[e:e + 1, 0:kend] -= jnp.sum(ds, axis=0, keepdims=True)
            dq_ref[rs, :] = dq_tile.astype(BF16)
        dk_ref[...] = dk_acc[...].astype(BF16)
        dv_ref[...] = dv_acc[...].astype(BF16)

        @pl.when(step == steps - 1)
        def _():
            for cp in _chip_copies(x_refs, got_refs, *sems):
                cp.wait()

    full = pl.BlockSpec((S, LANES), lambda b, g: (b, 0))
    grp = pl.BlockSpec((S, LANES), lambda b, g: (b, g))
    rows = pl.BlockSpec((None, None, 8, S), lambda b, g: (b, g, 0, 0))
    return pl.pallas_call(
        body, name="attn_bwd", grid=(T // S, ng),
        in_specs=_attn_specs(S, q_off, AW) + [grp, full, rows, full] + [ANY] * nx,
        out_specs=[grp, grp, grp, rows] + [ANY] * nx,
        out_shape=[SDS((T, AW), BF16), SDS((T, AW), BF16), SDS((T, AW), BF16), SDS((T // S, ng, 8, S), F32)]
                  + _chip_shapes(exchange),
        scratch_shapes=[pltpu.VMEM((S, LANES), F32), pltpu.VMEM((S, LANES), F32)] + _exchange_sems(3 * nx),
        compiler_params=_cparams(2),
    )(pm, pm, pm, do, fcum, frow, lse, *exchange)


def _forget_bwd(dfc, fl, bf, S):
    T, NF = fl.shape
    ch = _tile(S, 256, 8)

    def body(df_ref, fl_ref, bf_ref, dfl_ref, dbf_ref):
        @pl.when(pl.program_id(0) == 0)
        def _():
            dbf_ref[...] = jnp.zeros_like(dbf_ref)

        row = lax.broadcasted_iota(jnp.int32, (ch, ch), 0)
        col = lax.broadcasted_iota(jnp.int32, (ch, ch), 1)
        tri = (col >= row).astype(BF16)
        carry = jnp.zeros((1, NF), F32)
        for c in range(S - ch, -1, -ch):
            d = df_ref[c:c + ch, :]
            dlf = _tri_dot(tri, d) + carry
            carry = carry + jnp.sum(d, axis=0, keepdims=True)
            dfl = dlf * _sigmoid(-(fl_ref[c:c + ch, :] + bf_ref[...]))
            dfl_ref[c:c + ch, :] = dfl.astype(BF16)
            dbf_ref[0:1, :] += jnp.sum(dfl, axis=0, keepdims=True)

    return pl.pallas_call(
        body, name="forget_bwd", grid=(T // S,),
        in_specs=[pl.BlockSpec((S, NF), lambda b: (b, 0)), pl.BlockSpec((S, NF), lambda b: (b, 0)), _resident((1, NF))],
        out_specs=[pl.BlockSpec((S, NF), lambda b: (b, 0)), pl.BlockSpec((8, NF), lambda b: (0, 0))],
        out_shape=[SDS((T, NF), BF16), SDS((8, NF), F32)],
        compiler_params=_cparams(1),
    )(dfc, fl, bf)


def _inproj_bwd(dparts, offs, w_t, x, g1, dx2, tm, exchange):
    T, D = x.shape
    npart, nx = len(dparts), len(exchange)
    n = T // tm

    def body(*refs):
        d_refs = refs[:npart]
        w_ref, x_ref, g_ref, dx2_ref = refs[npart:npart + 4]
        x_refs = refs[npart + 4:npart + 4 + nx]
        dx_ref, dg_ref, dwl_ref = refs[npart + 4 + nx:npart + 7 + nx]
        got_refs, sems = refs[npart + 7 + nx:npart + 7 + 2 * nx], refs[npart + 7 + 2 * nx:]

        @pl.when(pl.program_id(0) == 0)
        def _():
            for cp in _chip_copies(x_refs, got_refs, *sems):
                cp.start()

        @pl.when(pl.program_id(0) == 0)
        def _():
            dg_ref[...] = jnp.zeros_like(dg_ref)
            dwl_ref[...] = jnp.zeros_like(dwl_ref)

        dh = None
        for d_ref, off in zip(d_refs, offs):
            term = _dot(d_ref[...], w_ref[off:off + d_ref.shape[1], :])
            dh = term if dh is None else dh + term
        xv = x_ref[...]
        r = _rms(xv)
        xn = xv * r
        dg_ref[0:1, :] += jnp.sum(dh * xn, axis=0, keepdims=True)
        dwl_ref[...] += _dot_tn(d_refs[-1][...], (xn * g_ref[...]).astype(BF16))
        dxn = dh * g_ref[...]
        dx_ref[...] = dx2_ref[...] + r * (dxn - xn * jnp.mean(dxn * xn, axis=-1, keepdims=True))

        @pl.when(pl.program_id(0) == n - 1)
        def _():
            for cp in _chip_copies(x_refs, got_refs, *sems):
                cp.wait()

    tok = lambda w: pl.BlockSpec((tm, w), lambda i: (i, 0))
    nl = dparts[-1].shape[1]
    return pl.pallas_call(
        body, name="inproj_bwd", grid=(n,),
        in_specs=[tok(d.shape[1]) for d in dparts] + [_resident(w_t.shape), tok(D), _resident((1, D)), tok(D)] + [ANY] * nx,
        out_specs=[tok(D), pl.BlockSpec((8, D), lambda i: (0, 0)), pl.BlockSpec((nl, D), lambda i: (0, 0))] + [ANY] * nx,
        out_shape=[SDS((T, D), F32), SDS((8, D), F32), SDS((nl, D), F32)] + _chip_shapes(exchange),
        scratch_shapes=_exchange_sems(3 * nx),
        compiler_params=_cparams(1),
    )(*dparts, w_t, x, g1, dx2, *exchange)


def _wgrad(bs, a, name, into=None, row_off=0, total_rows=None):
    bs = list(bs) if isinstance(bs, (list, tuple)) else [bs]
    P = len(bs)
    T, N = bs[0].shape
    M = a.shape[1]
    tn = _tile(N, 1408 if M <= 1024 else 512)
    while row_off % tn:
        tn = _tile(N, tn - LANES)
    tk = _tile(T, 512, 16)
    per_row = 2 * (P * tn * bs[0].dtype.itemsize + M * a.dtype.itemsize)
    while T % (2 * tk) == 0 and 2 * tk * per_row + 2 * tn * M * 4 <= WGRAD_VMEM_BUDGET:
        tk *= 2
    blk0, nj, nk = row_off // tn, N // tn, T // tk

    def body(*refs):
        b_refs, a_ref, o_ref = refs[:P], refs[P], refs[-1]
        j = pl.program_id(0)

        @pl.when(pl.program_id(1) == 0)
        def _():
            o_ref[...] = jnp.zeros_like(o_ref)

        for p, b_ref in enumerate(b_refs):
            @pl.when((j >= p * nj) & (j < (p + 1) * nj))
            def _(b_ref=b_ref):
                o_ref[...] += _dot_tn(b_ref[...].astype(BF16), a_ref[...].astype(BF16))

    def b_spec(p):
        def index(j, k):
            before, mine = j < p * nj, (j >= p * nj) & (j < (p + 1) * nj)
            return (jnp.where(mine, k, jnp.where(before, 0, nk - 1)),
                    jnp.where(mine, j - p * nj, jnp.where(before, 0, nj - 1)))
        return pl.BlockSpec((tk, tn), index)

    in_specs = [b_spec(p) for p in range(P)] + [pl.BlockSpec((tk, M), lambda j, k: (k, 0))]
    args = (*bs, a)
    kwargs = {}
    if into is not None:
        in_specs.append(ANY)
        args += (into,)
        kwargs["input_output_aliases"] = {P + 1: 0}
        total_rows = into.shape[0]
    return pl.pallas_call(
        body, name=name, grid=(P * nj, nk),
        in_specs=in_specs,
        out_specs=pl.BlockSpec((tn, M), lambda j, k: (blk0 + j, 0)),
        out_shape=SDS((P * N if total_rows is None else total_rows, M), F32),
        compiler_params=_cparams(2), **kwargs,
    )(*args)


def _adamw(w, g, m, v, name, recv=None):
    shape = w.shape
    C = shape[-1]
    w2, g2, m2, v2 = (a.reshape(-1, C) for a in (w, g, m, v))
    R = w2.shape[0]
    tr = R if R <= 512 else _tile(R, 256, 8)
    if tr < 64:
        tr = R

    def body(w_ref, g_ref, m_ref, v_ref, *rest):
        d_ref, nm_ref, nv_ref = rest[-3:] if recv is None else rest[1:4]
        gv = g_ref[...]
        if recv is not None:
            r_ref, gs_ref = rest[0], rest[4]
            gv = ((gv + r_ref[0].astype(F32)) + r_ref[1].astype(F32)) + r_ref[2].astype(F32)
            gs_ref[...] = gv
        mv = ADAM_B1 * m_ref[...] + (1.0 - ADAM_B1) * gv
        vv = ADAM_B2 * v_ref[...] + (1.0 - ADAM_B2) * (gv * gv)
        m_hat = mv / (1.0 - ADAM_B1 ** ADAM_STEP)
        v_hat = vv / (1.0 - ADAM_B2 ** ADAM_STEP)
        d_ref[...] = -ADAM_LR * (m_hat / (jnp.sqrt(v_hat) + ADAM_EPS) + ADAM_WD * w_ref[...])
        nm_ref[...] = mv
        nv_ref[...] = vv

    spec = pl.BlockSpec((tr, C), lambda r: (r, 0))
    extra = [] if recv is None else [pl.BlockSpec((3, tr, C), lambda r: (0, r, 0))]
    n_out = 3 if recv is None else 4
    outs = pl.pallas_call(
        body, name=name, grid=(R // tr,),
        in_specs=[spec] * 4 + extra, out_specs=[spec] * n_out, out_shape=[SDS((R, C), F32)] * n_out,
        compiler_params=_cparams(1),
    )(w2, g2, m2, v2, *([] if recv is None else [recv]))
    return tuple(o.reshape(shape) for o in outs)


def _rows(a, L):
    lead = a.shape[0]
    flat = a.reshape(lead, -1)
    n = flat.shape[1]
    r = -(-n // L)
    return jnp.pad(flat, ((0, 0), (0, r * L - n))).reshape(lead, r, L)


def _unrows(p, shape):
    return p.reshape(-1)[:int(np.prod(shape))].reshape(shape)


def _from_col_blocks(a):
    n, R, c = a.shape
    return a.transpose(1, 0, 2).reshape(R, n * c)


def _in_windows(n_loc, nqkv, H, NM):
    win = (n_loc + 7 + 7) // 8 * 8
    starts, index = [], np.zeros((N_DEV, n_loc), np.int32)
    for d in range(N_DEV):
        rows = np.arange(n_loc * d, n_loc * (d + 1))
        is_f = (rows >= nqkv) & (rows < nqkv + H)
        kept = np.where(rows < nqkv, rows, rows - H)
        lo = int(kept[~is_f].min())
        start = lo // 8 * 8
        assert int(kept[~is_f].max()) - start < win and start + win <= NM + LANES
        starts.append(start)
        index[d] = np.where(is_f, win + rows - nqkv, kept - start)
    return starts, win, index


def kernel(x, norm_mix_g, w_in, b_f, b_gate, conv_mix_w, w_out_conv, w_out_attn, w_o, norm_ffn_g, w_up, conv_ffn_w, w_down, norm_f_g, loss_target, m_norm_mix_g, m_w_in, m_b_f, m_b_gate, m_conv_mix_w, m_w_out_conv, m_w_out_attn, m_w_o, m_norm_ffn_g, m_w_up, m_conv_ffn_w, m_w_down, m_norm_f_g, v_norm_mix_g, v_w_in, v_b_f, v_b_gate, v_conv_mix_w, v_w_out_conv, v_w_out_attn, v_w_o, v_norm_ffn_g, v_w_up, v_conv_ffn_w, v_w_down, v_norm_f_g):
    Bl, S, D = x.shape
    T = Bl * S
    H = b_f.shape[-1]
    CW = N_DEV * conv_mix_w.shape[-1]
    AW = w_out_attn.shape[1]
    hd = AW // H
    FH = N_DEV * w_down.shape[1]
    n_loc = w_in.shape[-1]
    NIN = N_DEV * n_loc
    NM = 3 * CW + 3 * AW + 2 * D
    nqkv = 3 * CW + 3 * AW
    assert NIN == NM + H and w_out_conv.shape[1] == CW and nqkv % D == 0 and AW % LANES == 0 and LANES % hd == 0
    hpg, ng = LANES // hd, AW // LANES
    assert hpg <= 8
    tm_big = min(512, S // 2)
    tm_ffn = min(256, S // 2)
    tq_fwd = min(512, S // 2)
    tq_bwd = min(256, S // 2)
    px, py, pc = _place()
    me = 4 * px + 2 * py + pc

    bits = lambda a: lax.bitcast_convert_type(a, BF16)
    taps = jnp.concatenate([_rows(bits(conv_ffn_w[0])[None], D)[0], _rows(bits(conv_mix_w[0])[None], D)[0]], axis=0)
    n_ffn_rows = -(-conv_ffn_w[0].size * 2 // D)
    late = [w_up[0].T.astype(BF16), w_down[0].astype(BF16), w_o[0].astype(BF16), w_out_conv[0].T.astype(BF16),
            w_out_attn[0].T.astype(BF16), taps]
    g_in, = _all_gather([w_in[0].T.astype(BF16)], "weights_all_gather")
    W_in_rows = g_in.reshape(NIN, D)
    W_in_t = jnp.concatenate([W_in_rows[:nqkv], W_in_rows[nqkv + H:], W_in_rows[nqkv:nqkv + H],
                              jnp.zeros((LANES - H, D), BF16)], axis=0)
    bf128 = jnp.pad(b_f, ((0, 0), (0, LANES - H)))

    x2d = x.reshape(T, D)
    tgt = loss_target.reshape(T, D)
    pm, fl, h1 = _inproj_fwd(x2d, norm_mix_g, W_in_t, NM, tm_big)
    fcum = _forget_cumsum(fl, bf128, S)
    frow = jnp.pad(fcum[:, :H].reshape(Bl, S, ng, hpg).transpose(0, 2, 3, 1), ((0, 0), (0, 0), (0, 8 - hpg), (0, 0)))
    q_off = 3 * CW // LANES
    o, lse, g_up, g_dn, g_o, g_oc, g_oa, g_taps = _attn_fwd(pm, fcum, frow, S, q_off, AW, hd, tq_fwd, late)
    W_up_t = g_up.reshape(2 * FH, D)
    W_dn = g_dn.reshape(FH, D)
    W_o = g_o.reshape(D, D)
    W_oc_t = g_oc.reshape(D, CW)
    W_oa_t = g_oa.reshape(D, AW)
    tap_bits = g_taps.reshape(N_DEV, -1)
    n_ffn, n_mix = conv_ffn_w[0].size * 2, conv_mix_w[0].size * 2
    wf_full = _from_col_blocks(lax.bitcast_convert_type(
        tap_bits[:, :n_ffn].reshape((N_DEV,) + conv_ffn_w.shape[1:] + (2,)), F32))
    wc_full = _from_col_blocks(lax.bitcast_convert_type(
        tap_bits[:, n_ffn_rows * D:n_ffn_rows * D + n_mix].reshape((N_DEV,) + conv_mix_w.shape[1:] + (2,)), F32))
    wf8 = jnp.pad(wf_full, ((0, 5), (0, 0)))
    wc8 = jnp.pad(wc_full, ((0, 5), (0, 0)))
    x2, u, m, ycin = _merge_fwd(pm, o, x2d, wc8, b_gate, W_oc_t, W_oa_t, W_o, S, tm_big)
    upp, up, act, h2, dx3, loss8, dgf8 = _ffn_fwd(x2, norm_ffn_g, W_up_t, wf8, W_dn, norm_f_g.reshape(1, D), tgt, S, tm_ffn)

    dx2, dupp, dwf8, dg2_8 = _ffn_bwd(dx3, x2, norm_ffn_g, upp, up, wf8, W_dn, W_up_t, S, tm_ffn)
    dW_dn = _wgrad(act, dx3, "wgrad_down")
    dW_up_t = _wgrad(dupp, h2, "wgrad_up")
    ids = jnp.stack([pc, 2 * px + py]).astype(jnp.int32)
    big = [dW_up_t.reshape(4, 2, -1, D), dW_dn.reshape(4, 2, -1, D)]
    dconv, dgl, do, dW_small, dwc8, dbg8, *sib_big = _merge_bwd(
        dx2, pm, u, m, ycin, o, wc8, b_gate, W_o, W_oc_t, W_oa_t, S, tm_big, big)
    big_sums = [_pair_sum(b, r, ids, "grads_pair_sum_%d" % a) for a, (b, r) in enumerate(zip(big, sib_big))]
    dq, dk, dv, dfr, *chips_big = _attn_bwd(pm, do, fcum, frow, lse, S, q_off, AW, hd, tq_bwd, [s[1] for s in big_sums])
    dfc = jnp.pad(dfr[:, :, :hpg, :].transpose(0, 3, 1, 2).reshape(T, H), ((0, 0), (0, LANES - H)))
    dfl, dbf8 = _forget_bwd(dfc, fl, bf128, S)
    dparts = [dconv, dq, dk, dv, dgl, dfl]
    offs = [0, 3 * CW, 3 * CW + AW, 3 * CW + 2 * AW, nqkv, NM]
    dW_in_t = _wgrad(dconv, h1, "wgrad_in_conv", total_rows=NM + LANES)
    dW_in_t = _wgrad([dq, dk, dv], h1, "wgrad_in_qkv", into=dW_in_t, row_off=3 * CW)
    dW_in_t = _wgrad(dgl, h1, "wgrad_in_gates", into=dW_in_t, row_off=nqkv)

    starts, win, index = _in_windows(n_loc, nqkv, H, NM)
    small = dW_small.reshape(4, 2, -1, D)
    my_starts = [jnp.where(pc == 0, starts[2 * j], starts[2 * j + 1]) for j in range(4)]
    win_ids = jnp.stack([pc, 2 * px + py] + my_starts).astype(jnp.int32)
    sib_win, sib_small = _sibling_exchange(dW_in_t, starts, win, [small], "grads_sibling_exchange")
    sums = [_pair_sum(dW_in_t, sib_win, win_ids, "grads_pair_sum_in", win_rows=win),
            _pair_sum(small, sib_small, ids, "grads_pair_sum_small")]
    grad_x, dg1_8, dW_forget, *from_chips = _inproj_bwd(dparts, offs, W_in_t, x2d, norm_mix_g, dx2, tm_big, [s[1] for s in sums])
    red_win, red_small = [_final_sum(s[0], r, "grads_final_sum_%d" % a) for a, (s, r) in enumerate(zip(sums, from_chips))]
    (own_up, _), (own_dn, _) = big_sums
    from_up, from_dn = chips_big

    f_rows = dW_forget[0:8]
    wf_rows = _rows(dwf8[:3].reshape(1, -1), D)[0]
    wc_rows = _rows(dwc8[:3].reshape(1, -1), D)[0]
    smalls = [dg1_8[0:1], dg2_8[0:1], dgf8[0:1], dbg8[0:1, :D], dbg8[0:1, D:], jnp.pad(dbf8[0:1], ((0, 0), (0, D - LANES))),
              jnp.pad(loss8[0:1], ((0, 0), (0, D - LANES))), jnp.zeros((1, D), F32), f_rows, wf_rows, wc_rows]
    spack = jnp.concatenate(smalls, axis=0)
    spack = jnp.pad(spack, ((0, -spack.shape[0] % 8), (0, 0)))
    ssum = _sum8(_all_gather([spack], "small_all_gather")[0], "small_sum")
    g_g1, g_g2, g_gf = ssum[0:1], ssum[1:2], ssum[2]
    g_bg = jnp.concatenate([ssum[3:4], ssum[4:5]], axis=1)
    g_bf = ssum[5:6, :H]
    r0 = 16
    r1 = r0 + wf_rows.shape[0]
    wf_sum = _unrows(ssum[r0:r1], (3, 2 * FH))
    wc_sum = _unrows(ssum[r1:r1 + wc_rows.shape[0]], (3, CW))
    g_wf = lax.dynamic_slice_in_dim(wf_sum, me * conv_ffn_w.shape[-1], conv_ffn_w.shape[-1], axis=1)
    g_wc = lax.dynamic_slice_in_dim(wc_sum, me * conv_mix_w.shape[-1], conv_mix_w.shape[-1], axis=1)

    ext = jnp.concatenate([red_win, ssum[8:8 + H]], axis=0)
    my_index = lax.dynamic_index_in_dim(jnp.asarray(index), me, axis=0, keepdims=False)
    g_w_in_t = jnp.take(ext, my_index, axis=0)
    rb = D // N_DEV
    g_w_o = red_small[:rb]
    g_w_oc = red_small[rb:, :CW].T
    g_w_oa = red_small[rb:, CW:].T

    loss = ssum[6, 0]

    names = ["norm_mix_g", "w_in", "b_f", "b_gate", "conv_mix_w", "w_out_conv", "w_out_attn", "w_o", "norm_ffn_g", "w_up",
             "conv_ffn_w", "w_down", "norm_f_g"]
    weights = [norm_mix_g, w_in, b_f, b_gate, conv_mix_w, w_out_conv, w_out_attn, w_o, norm_ffn_g, w_up, conv_ffn_w, w_down, norm_f_g]
    grads = [g_g1, g_w_in_t, g_bf, g_bg, g_wc, g_w_oc, g_w_oa, g_w_o, g_g2, own_up, g_wf, own_dn, g_gf]
    ms = [m_norm_mix_g, m_w_in, m_b_f, m_b_gate, m_conv_mix_w, m_w_out_conv, m_w_out_attn, m_w_o, m_norm_ffn_g, m_w_up,
          m_conv_ffn_w, m_w_down, m_norm_f_g]
    vs = [v_norm_mix_g, v_w_in, v_b_f, v_b_gate, v_conv_mix_w, v_w_out_conv, v_w_out_attn, v_w_o, v_norm_ffn_g, v_w_up,
          v_conv_ffn_w, v_w_down, v_norm_f_g]
    to_view = {"w_in": lambda a: a[0].T.reshape(-1, LANES), "w_up": lambda a: a[0].T}
    from_view = {"w_in": lambda a: a.reshape(n_loc, D).T[None], "w_up": lambda a: a.T[None]}
    received = {"w_up": from_up, "w_down": from_dn}
    out_grads, steps = [], []
    for nm, w, g, mm, vv in zip(names, weights, grads, ms, vs):
        view, back = to_view.get(nm, lambda a: a), from_view.get(nm, lambda a: a)
        wv, mv, vw = (view(a) for a in (w, mm, vv))
        gv = g.reshape(wv.shape)
        res = _adamw(wv, gv, mv, vw, "adamw_" + nm, recv=received.get(nm))
        if nm in received:
            gv = res[3]
        steps.append(tuple(back(o) for o in res[:3]))
        out_grads.append(back(gv))
    deltas, new_ms, new_vs = zip(*steps)
    return (loss, grad_x.reshape(Bl, S, D), *out_grads, *deltas, *new_ms, *new_vs)
```

```python
import numpy as np

import jax
import jax.numpy as jnp
from jax import lax
from jax.experimental import pallas as pl
from jax.experimental.pallas import tpu as pltpu

F32, BF16 = jnp.float32, jnp.bfloat16
EPS = 1e-6
ADAM_LR, ADAM_B1, ADAM_B2, ADAM_EPS, ADAM_WD, ADAM_STEP = 0.001, 0.9, 0.999, 1e-08, 0.01, 10
N_DEV = 8
LANES = 128
V7X_VMEM_LIMIT = 56 * 1024 * 1024
FFN_CHUNK = 2816
WGRAD_VMEM_BUDGET = 40 * 1024 * 1024
MESH = pl.DeviceIdType.MESH
SDS = jax.ShapeDtypeStruct
ANY = pl.BlockSpec(memory_space=pl.ANY)


def _tile(n, target, mult=LANES):
    best = None
    for t in range(mult, min(n, target) + 1, mult):
        if n % t == 0:
            best = t
    return best if best is not None else n


def _resident(shape):
    return pl.BlockSpec(shape, lambda *_: (0,) * len(shape), pipeline_mode=pl.Buffered(1))


def _cparams(n_axes=1):
    return pltpu.CompilerParams(dimension_semantics=("arbitrary",) * n_axes, vmem_limit_bytes=V7X_VMEM_LIMIT)


def _dot(a, b):
    return jnp.dot(a, b, preferred_element_type=F32)


def _dot_tn(a, b):
    return lax.dot_general(a, b, (((0,), (0,)), ((), ())), preferred_element_type=F32)


def _dot_nt(a, b):
    return lax.dot_general(a, b, (((1,), (1,)), ((), ())), preferred_element_type=F32)


def _sigmoid(x):
    return 0.5 * jnp.tanh(0.5 * x) + 0.5


def _rms(x):
    return lax.rsqrt(jnp.mean(x * x, axis=-1, keepdims=True) + EPS)


def _taps_back(z, r6, r7):
    row = lax.broadcasted_iota(jnp.int32, (8, 1), 0)
    z1, z2 = pltpu.roll(z, 1, 0), pltpu.roll(z, 2, 0)
    z1 = jnp.concatenate([jnp.where(row == 0, r7, z1[0:8]), z1[8:]], axis=0)
    z2 = jnp.concatenate([jnp.where(row == 0, r6, jnp.where(row == 1, r7, z2[0:8])), z2[8:]], axis=0)
    return z1, z2


def _taps_ahead(d, h0, h1):
    tm = d.shape[0]
    row = lax.broadcasted_iota(jnp.int32, (8, 1), 0)
    d1, d2 = pltpu.roll(d, tm - 1, 0), pltpu.roll(d, tm - 2, 0)
    d1 = jnp.concatenate([d1[:tm - 8], jnp.where(row == 7, h0, d1[tm - 8:])], axis=0)
    d2 = jnp.concatenate([d2[:tm - 8], jnp.where(row == 6, h0, jnp.where(row == 7, h1, d2[tm - 8:]))], axis=0)
    return d1, d2


def _tri_dot(tri, x):
    hi = x.astype(BF16)
    r = x - hi.astype(F32)
    mid = r.astype(BF16)
    lo = (r - mid.astype(F32)).astype(BF16)
    return (_dot(tri, lo) + _dot(tri, mid)) + _dot(tri, hi)


def _lane_pick(block, lane):
    lanes = lax.broadcasted_iota(jnp.int32, (1, block.shape[1]), 1)
    return jnp.sum(jnp.where(lanes == lane, block, 0.0), axis=1, keepdims=True)


def _place():
    return lax.axis_index("x"), lax.axis_index("y"), lax.axis_index("c")


def _all_gather(xs, name):
    n = len(xs)

    def body(*refs):
        start, forward, finish = _gather_phases(refs[:n], refs[n:2 * n], *refs[2 * n:])
        start()
        forward()
        finish()

    return pl.pallas_call(
        body, name=name,
        out_shape=_gather_shapes(xs), in_specs=[ANY] * n, out_specs=[ANY] * n, scratch_shapes=_gather_sems(n),
    )(*xs)


def _gather_shapes(xs):
    return [SDS((N_DEV,) + x.shape, x.dtype) for x in xs]


def _gather_sems(n):
    return [pltpu.SemaphoreType.DMA((7 * n,)), pltpu.SemaphoreType.DMA((7 * n,)), pltpu.SemaphoreType.DMA((n,))]


def _gather_phases(x_refs, out_refs, send_sems, recv_sems, local_sems):
    n = len(x_refs)

    def parts():
        px, py, pc = _place()
        me, sibling = (px, py, pc), (px, py, 1 - pc)
        chips = [(1 - px, py), (px, 1 - py), (1 - px, 1 - py)]

        def slot(a, qx, qy, qc):
            return out_refs[a].at[4 * qx + 2 * qy + qc]

        def copy(a, k, block, to, src=None):
            return pltpu.make_async_remote_copy(
                src_ref=slot(a, *block) if src is None else src, dst_ref=slot(a, *block),
                send_sem=send_sems.at[7 * a + k], recv_sem=recv_sems.at[7 * a + k], device_id=to, device_id_type=MESH)

        def mine():
            return [pltpu.make_async_copy(x_refs[a], slot(a, *me), local_sems.at[a]) for a in range(n)]

        def first():
            out = []
            for a in range(n):
                out.append(copy(a, 0, me, sibling, src=x_refs[a]))
                out += [copy(a, 1 + j, me, (*chip, pc), src=x_refs[a]) for j, chip in enumerate(chips)]
            return out

        def landed():
            return [copy(a, 1 + j, (*chip, pc), me) for j, chip in enumerate(chips) for a in range(n)]

        def passed():
            return [copy(a, 4 + j, (*chip, pc), sibling) for j, chip in enumerate(chips) for a in range(n)]

        def late():
            out = [copy(a, 0, sibling, me) for a in range(n)]
            return out + [copy(a, 4 + j, (*chip, 1 - pc), me) for j, chip in enumerate(chips) for a in range(n)]

        return mine, first, landed, passed, late

    def start():
        mine, first, _, _, _ = parts()
        for cp in mine() + first():
            cp.start()

    def forward():
        _, _, landed, passed, _ = parts()
        for got, cp in zip(landed(), passed()):
            got.wait_recv()
            cp.start()

    def finish():
        mine, first, _, passed, late = parts()
        for cp in late():
            cp.wait_recv()
        for cp in first() + passed():
            cp.wait_send()
        for cp in mine():
            cp.wait()

    return start, forward, finish


def _sibling_exchange(win_buf, win_starts, win_rows, blocked, name):
    nb = len(blocked)

    def body(*refs):
        win_ref, blk_refs = refs[0], refs[1:1 + nb]
        rwin_ref, rblk_refs = refs[1 + nb], refs[2 + nb:2 + 2 * nb]
        send_sems, recv_sems, wsend_sems, wrecv_sems = refs[2 + 2 * nb:]
        px, py, pc = _place()
        copies = _sibling_copies(blk_refs, rblk_refs, send_sems, recv_sems)
        for j in range(4):
            theirs = jnp.where(pc == 0, win_starts[2 * j + 1], win_starts[2 * j])
            copies.append(pltpu.make_async_remote_copy(
                src_ref=win_ref.at[pl.ds(pl.multiple_of(theirs, 8), win_rows)], dst_ref=rwin_ref.at[j],
                send_sem=wsend_sems.at[j], recv_sem=wrecv_sems.at[j], device_id=(px, py, 1 - pc), device_id_type=MESH))
        for cp in copies:
            cp.start()
        for cp in copies:
            cp.wait()

    C = win_buf.shape[1]
    return pl.pallas_call(
        body, name=name,
        out_shape=[SDS((4, win_rows, C), F32)] + _sibling_shapes(blocked),
        in_specs=[ANY] * (1 + nb), out_specs=[ANY] * (1 + nb),
        scratch_shapes=_exchange_sems(nb) + _exchange_sems(4),
    )(win_buf, *blocked)


def _sibling_shapes(blocked):
    return [SDS((4,) + b.shape[2:], F32) for b in blocked]


def _exchange_sems(n):
    return [pltpu.SemaphoreType.DMA((n,)), pltpu.SemaphoreType.DMA((n,))]


def _sibling_copies(blk_refs, out_refs, send_sems, recv_sems):
    px, py, pc = _place()
    return [pltpu.make_async_remote_copy(
        src_ref=b.at[:, 1 - pc], dst_ref=o, send_sem=send_sems.at[a], recv_sem=recv_sems.at[a],
        device_id=(px, py, 1 - pc), device_id_type=MESH) for a, (b, o) in enumerate(zip(blk_refs, out_refs))]


def _chip_shapes(ps):
    return [SDS((3,) + p.shape[1:], p.dtype) for p in ps]


def _chip_copies(p_refs, out_refs, send_sems, recv_sems):
    px, py, pc = _place()
    n = len(p_refs)
    chips = [(1 - px, py), (px, 1 - py), (1 - px, 1 - py)]
    return [pltpu.make_async_remote_copy(
        src_ref=p_refs[a].at[2 * qx + qy], dst_ref=out_refs[a].at[k],
        send_sem=send_sems.at[3 * a + k], recv_sem=recv_sems.at[3 * a + k],
        device_id=(qx, qy, pc), device_id_type=MESH) for k, (qx, qy) in enumerate(chips) for a in range(n)]


def _pair_sum(own, recv, ids, name, win_rows=None):
    _, R, C = recv.shape
    tr = _tile(R, 512, 8)

    def body(ids_ref, g_ref, r_ref, own_ref, pb_ref):
        s = g_ref[...] + r_ref[...]
        pb_ref[...] = s.astype(BF16)

        @pl.when(pl.program_id(1) == ids_ref[1])
        def _():
            own_ref[...] = s

    if win_rows is None:
        own_spec = pl.BlockSpec((None, None, tr, C), lambda r, j, ids: (j, ids[0], r, 0))
    else:
        own_spec = pl.BlockSpec((pl.Element(tr), pl.Element(C)), lambda r, j, ids: (pl.multiple_of(ids[2 + j] + r * tr, 8), 0))
    return pl.pallas_call(
        body, name=name,
        grid_spec=pltpu.PrefetchScalarGridSpec(
            num_scalar_prefetch=1, grid=(R // tr, 4),
            in_specs=[own_spec, pl.BlockSpec((None, tr, C), lambda r, j, ids: (j, r, 0))],
            out_specs=[pl.BlockSpec((tr, C), lambda r, j, ids: (r, 0)),
                       pl.BlockSpec((None, tr, C), lambda r, j, ids: (j, r, 0))]),
        out_shape=[SDS((R, C), F32), SDS((4, R, C), BF16)],
        compiler_params=_cparams(2),
    )(ids, own, recv)


def _final_sum(own, recv, name):
    R, C = own.shape
    tr = _tile(R, 512, 8)

    def body(o_ref, r_ref, out_ref):
        out_ref[...] = ((o_ref[...] + r_ref[0].astype(F32)) + r_ref[1].astype(F32)) + r_ref[2].astype(F32)

    return pl.pallas_call(
        body, name=name, grid=(R // tr,),
        in_specs=[pl.BlockSpec((tr, C), lambda r: (r, 0)), pl.BlockSpec((3, tr, C), lambda r: (0, r, 0))],
        out_specs=pl.BlockSpec((tr, C), lambda r: (r, 0)),
        out_shape=SDS((R, C), F32),
        compiler_params=_cparams(1),
    )(own, recv)


def _sum8(a, name):
    def body(a_ref, out_ref):
        s = a_ref[0]
        for d in range(1, N_DEV):
            s = s + a_ref[d]
        out_ref[...] = s

    return pl.pallas_call(body, name=name, out_shape=SDS(a.shape[1:], F32))(a)


def _inproj_fwd(x, g1, w_t, NM, tm):
    T, D = x.shape
    NF = w_t.shape[0] - NM
    ch = _tile(NM, 1024)

    def body(x_ref, g_ref, w_ref, pm_ref, fl_ref, h_ref):
        xv = x_ref[...]
        h = (xv * _rms(xv) * g_ref[...]).astype(BF16)
        h_ref[...] = h
        for c in range(0, NM, ch):
            pm_ref[:, c:c + ch] = _dot_nt(h, w_ref[c:c + ch, :]).astype(BF16)
        fl_ref[...] = _dot_nt(h, w_ref[NM:NM + NF, :])

    return pl.pallas_call(
        body, name="inproj_fwd", grid=(T // tm,),
        in_specs=[pl.BlockSpec((tm, D), lambda i: (i, 0)), _resident((1, D)), _resident(w_t.shape)],
        out_specs=[pl.BlockSpec((tm, NM), lambda i: (i, 0)), pl.BlockSpec((tm, NF), lambda i: (i, 0)),
                   pl.BlockSpec((tm, D), lambda i: (i, 0))],
        out_shape=[SDS((T, NM), BF16), SDS((T, NF), F32), SDS((T, D), BF16)],
        compiler_params=_cparams(1),
    )(x, g1, w_t)


def _log_sigmoid(x):
    return jnp.minimum(x, 0.0) - jnp.log(1.0 + jnp.exp(-jnp.abs(x)))


def _forget_cumsum(fl, bf, S):
    T, NF = fl.shape
    ch = _tile(S, 256, 8)

    def body(fl_ref, bf_ref, f_ref):
        row = lax.broadcasted_iota(jnp.int32, (ch, ch), 0)
        col = lax.broadcasted_iota(jnp.int32, (ch, ch), 1)
        tri = (col <= row).astype(BF16)
        carry = jnp.zeros((1, NF), F32)
        for c in range(0, S, ch):
            lf = _log_sigmoid(fl_ref[c:c + ch, :] + bf_ref[...])
            f_ref[c:c + ch, :] = _tri_dot(tri, lf) + carry
            carry = carry + jnp.sum(lf, axis=0, keepdims=True)

    return pl.pallas_call(
        body, name="forget_cumsum", grid=(T // S,),
        in_specs=[pl.BlockSpec((S, NF), lambda b: (b, 0)), _resident((1, NF))],
        out_specs=pl.BlockSpec((S, NF), lambda b: (b, 0)),
        out_shape=SDS((T, NF), F32),
        compiler_params=_cparams(1),
    )(fl, bf)


def _head_mask(e, hd):
    lanes = lax.broadcasted_iota(jnp.int32, (1, LANES), 1)
    return (lanes >= e * hd) & (lanes < (e + 1) * hd)


def _attn_specs(S, q_off, AW):
    ng = AW // LANES
    return [pl.BlockSpec((S, LANES), lambda b, g, o=q_off + w * ng: (b, o + g)) for w in range(3)]


def _attn_fwd(pm, fcum, frow, S, q_off, AW, hd, tq, gather):
    T = pm.shape[0]
    scale = float(hd) ** -0.5
    nq, hpg = S // tq, LANES // hd
    ng, nx = AW // LANES, len(gather)
    steps = (T // S) * ng

    def body(q_ref, k_ref, v_ref, fc_ref, fr_ref, *rest):
        x_refs, (o_ref, lse_ref), out_refs, sems = rest[:nx], rest[nx:nx + 2], rest[nx + 2:2 * nx + 2], rest[2 * nx + 2:]
        g = pl.program_id(1)
        step = pl.program_id(0) * ng + g
        start, forward, finish = _gather_phases(x_refs, out_refs, *sems)
        pl.when(step == 0)(start)

        @pl.when(g == 0)
        def _():
            lse_ref[...] = jnp.zeros_like(lse_ref)

        lanes = lax.broadcasted_iota(jnp.int32, (1, LANES), 1)
        for i in range(nq):
            rs, kend = slice(i * tq, (i + 1) * tq), (i + 1) * tq
            row = i * tq + lax.broadcasted_iota(jnp.int32, (tq, kend), 0)
            col = lax.broadcasted_iota(jnp.int32, (tq, kend), 1)
            o_tile = jnp.zeros((tq, LANES), F32)
            lse_tile = lse_ref[rs, :]
            for e in range(hpg):
                mask = _head_mask(e, hd)
                qs = jnp.where(mask, q_ref[rs, :], 0) * scale
                s = _dot_nt(qs, k_ref[0:kend, :]) + _lane_pick(fc_ref[rs, :], g * hpg + e) - fr_ref[e:e + 1, 0:kend]
                s = jnp.where(col <= row, s, -1e30)
                m = jnp.max(s, axis=1, keepdims=True)
                p = jnp.exp(s - m)
                l = jnp.sum(p, axis=1, keepdims=True)
                o_tile = jnp.where(mask, _dot(p.astype(BF16), v_ref[0:kend, :]) / l, o_tile)
                lse_tile = jnp.where(lanes == g * hpg + e, m + jnp.log(l), lse_tile)
            o_ref[rs, :] = o_tile.astype(BF16)
            lse_ref[rs, :] = lse_tile
        pl.when(step == (steps * 5) // 8)(forward)
        pl.when(step == steps - 1)(finish)

    full = pl.BlockSpec((S, LANES), lambda b, g: (b, 0))
    return pl.pallas_call(
        body, name="attn_fwd", grid=(T // S, ng),
        in_specs=_attn_specs(S, q_off, AW) + [full, pl.BlockSpec((None, None, 8, S), lambda b, g: (b, g, 0, 0))] + [ANY] * nx,
        out_specs=[pl.BlockSpec((S, LANES), lambda b, g: (b, g)), full] + [ANY] * nx,
        out_shape=[SDS((T, AW), BF16), SDS((T, LANES), F32)] + _gather_shapes(gather),
        scratch_shapes=_gather_sems(nx),
        compiler_params=_cparams(2),
    )(pm, pm, pm, fcum, frow, *gather)


def _merge_fwd(pm, o, x, wc, bg, woc_t, woa_t, wo, S, tm):
    T, D = x.shape
    CW, AW = woc_t.shape[1], woa_t.shape[1]
    tps = S // tm

    def body(cb_ref, cc_ref, cin_ref, gc_ref, ga_ref, o_ref, x_ref, wc_ref, bg_ref, woct_ref, woat_ref, wo_ref,
             x2_ref, u_ref, ycin_ref, tail_ref):
        @pl.when(pl.program_id(0) % tps == 0)
        def _():
            tail_ref[...] = jnp.zeros_like(tail_ref)

        z = cc_ref[...].astype(F32) * cin_ref[...].astype(F32)
        z1, z2 = _taps_back(z, tail_ref[6:7, :], tail_ref[7:8, :])
        tail_ref[...] = z[tm - 8:tm, :]
        u = (z * wc_ref[2:3, :] + z2 * wc_ref[0:1, :]) + z1 * wc_ref[1:2, :]
        u_ref[...] = u.astype(BF16)
        ycin = (cb_ref[...].astype(F32) * u).astype(BF16)
        ycin_ref[...] = ycin
        yc = _dot_nt(ycin, woct_ref[...])
        ya = _dot_nt(o_ref[...], woat_ref[...])
        gc = _sigmoid(gc_ref[...].astype(F32) + bg_ref[:, 0:D])
        ga = _sigmoid(ga_ref[...].astype(F32) + bg_ref[:, D:2 * D])
        m = (gc * yc + ga * ya).astype(BF16)
        x2_ref[...] = x_ref[...] + _dot(m, wo_ref[...])

    g_off = (3 * CW + 3 * AW) // D
    tok = lambda w, j=0: pl.BlockSpec((tm, w), lambda i: (i, j))
    return pl.pallas_call(
        body, name="merge_fwd", grid=(T // tm,),
        in_specs=[tok(CW, 0), tok(CW, 1), tok(CW, 2), tok(D, g_off), tok(D, g_off + 1), tok(AW), tok(D),
                  _resident((8, CW)), _resident((1, 2 * D)), _resident((D, CW)), _resident((D, AW)), _resident((D, D))],
        out_specs=[tok(D), tok(CW), tok(CW)],
        out_shape=[SDS((T, D), F32), SDS((T, CW), BF16), SDS((T, CW), BF16)],
        scratch_shapes=[pltpu.VMEM((8, CW), F32)],
        compiler_params=_cparams(1),
    )(pm, pm, pm, pm, pm, o, x, wc, bg, woc_t, woa_t, wo)


def _ffn_fwd(x2, g2, wup_t, wf, wdn, gf, tgt, S, tm):
    T, D = x2.shape
    FH = wdn.shape[0]
    ch = _tile(FH, FFN_CHUNK)
    tps = S // tm

    def body(x2_ref, g2_ref, wupt_ref, wf_ref, wdn_ref, gf_ref, tgt_ref,
             upp_ref, up_ref, act_ref, h2_ref, dx3_ref, loss_ref, dgf_ref, tail_ref):
        i = pl.program_id(0)

        @pl.when(i % tps == 0)
        def _():
            tail_ref[...] = jnp.zeros_like(tail_ref)

        @pl.when(i == 0)
        def _():
            loss_ref[...] = jnp.zeros_like(loss_ref)
            dgf_ref[...] = jnp.zeros_like(dgf_ref)

        x2v = x2_ref[...]
        h2 = (x2v * _rms(x2v) * g2_ref[...]).astype(BF16)
        h2_ref[...] = h2
        x3 = x2v
        for c in range(0, FH, ch):
            gated = []
            for cols in (slice(c, c + ch), slice(FH + c, FH + c + ch)):
                upp = _dot_nt(h2, wupt_ref[cols, :])
                upp_ref[:, cols] = upp.astype(BF16)
                p1, p2 = _taps_back(upp, tail_ref[6:7, cols], tail_ref[7:8, cols])
                tail_ref[:, cols] = upp[tm - 8:tm, :]
                up = (upp * wf_ref[2:3, cols] + p2 * wf_ref[0:1, cols]) + p1 * wf_ref[1:2, cols]
                up_ref[:, cols] = up.astype(BF16)
                gated.append(up)
            a, b = gated
            act = (a * _sigmoid(a) * b).astype(BF16)
            act_ref[:, c:c + ch] = act
            x3 = x3 + _dot(act, wdn_ref[c:c + ch, :])
        r3 = _rms(x3)
        xn3 = x3 * r3
        e = xn3 * gf_ref[...] - tgt_ref[...]
        loss_ref[...] += 0.5 * jnp.sum(jnp.mean(e * e, axis=-1, keepdims=True), axis=0, keepdims=True)
        dy = e / D
        dgf_ref[0:1, :] += jnp.sum(dy * xn3, axis=0, keepdims=True)
        dxn = dy * gf_ref[...]
        dx3_ref[...] = r3 * (dxn - xn3 * jnp.mean(dxn * xn3, axis=-1, keepdims=True))

    tok = lambda w: pl.BlockSpec((tm, w), lambda i: (i, 0))
    return pl.pallas_call(
        body, name="ffn_fwd", grid=(T // tm,),
        in_specs=[tok(D), _resident((1, D)), _resident((2 * FH, D)), _resident((8, 2 * FH)), _resident((FH, D)),
                  _resident((1, D)), tok(D)],
        out_specs=[tok(2 * FH), tok(2 * FH), tok(FH), tok(D), tok(D), pl.BlockSpec((8, LANES), lambda i: (0, 0)),
                   pl.BlockSpec((8, D), lambda i: (0, 0))],
        out_shape=[SDS((T, 2 * FH), BF16), SDS((T, 2 * FH), BF16), SDS((T, FH), BF16), SDS((T, D), BF16), SDS((T, D), F32),
                   SDS((8, LANES), F32), SDS((8, D), F32)],
        scratch_shapes=[pltpu.VMEM((8, 2 * FH), F32)],
        compiler_params=_cparams(1),
    )(x2, g2, wup_t, wf, wdn, gf, tgt)


def _ffn_bwd(dx3, x2, g2, upp, up, wf, wdn, wup_t, S, tm):
    T, D = x2.shape
    FH = wdn.shape[0]
    ch = _tile(FH, FFN_CHUNK)
    n, tps = T // tm, S // tm

    def body(dx3_ref, x2_ref, g2_ref, upp_ref, up_ref, wf_ref, wdn_ref, wupt_ref,
             dx2_ref, dupp_ref, dwf_ref, dg2_ref, head_ref):
        i = pl.program_id(0)
        t = n - 1 - i

        @pl.when(i == 0)
        def _():
            dwf_ref[...] = jnp.zeros_like(dwf_ref)
            dg2_ref[...] = jnp.zeros_like(dg2_ref)

        @pl.when(t % tps == tps - 1)
        def _():
            head_ref[...] = jnp.zeros_like(head_ref)

        dx3v = dx3_ref[...]
        dx3b = dx3v.astype(BF16)
        x2v = x2_ref[...]
        r2 = _rms(x2v)
        xn2 = x2v * r2
        dh2 = jnp.zeros((tm, D), F32)
        for c in range(0, FH, ch):
            ca, cb = slice(c, c + ch), slice(FH + c, FH + c + ch)
            a, b = up_ref[:, ca].astype(F32), up_ref[:, cb].astype(F32)
            sig = _sigmoid(a)
            sl = a * sig
            dact = _dot_nt(dx3b, wdn_ref[ca, :])
            grads = (dact * b * (sig * (1.0 + a * (1.0 - sig))), dact * sl)
            for cols, d in zip((ca, cb), grads):
                u0 = upp_ref[:, cols].astype(F32)
                d1, d2 = _taps_ahead(d, head_ref[0:1, cols], head_ref[1:2, cols])
                head_ref[:, cols] = d[0:8, :]
                dwf_ref[2:3, cols] += jnp.sum(u0 * d, axis=0, keepdims=True)
                dwf_ref[1:2, cols] += jnp.sum(u0 * d1, axis=0, keepdims=True)
                dwf_ref[0:1, cols] += jnp.sum(u0 * d2, axis=0, keepdims=True)
                dpre = ((d * wf_ref[2:3, cols] + d1 * wf_ref[1:2, cols]) + d2 * wf_ref[0:1, cols]).astype(BF16)
                dupp_ref[:, cols] = dpre
                dh2 = dh2 + _dot(dpre, wupt_ref[cols, :])
        dg2_ref[0:1, :] += jnp.sum(dh2 * xn2, axis=0, keepdims=True)
        dxn = dh2 * g2_ref[...]
        dx2_ref[...] = dx3v + r2 * (dxn - xn2 * jnp.mean(dxn * xn2, axis=-1, keepdims=True))

    tok = lambda w: pl.BlockSpec((tm, w), lambda i: (n - 1 - i, 0))
    acc = lambda w: pl.BlockSpec((8, w), lambda i: (0, 0))
    return pl.pallas_call(
        body, name="ffn_bwd", grid=(n,),
        in_specs=[tok(D), tok(D), _resident((1, D)), tok(2 * FH), tok(2 * FH), _resident((8, 2 * FH)),
                  _resident((FH, D)), _resident((2 * FH, D))],
        out_specs=[tok(D), tok(2 * FH), acc(2 * FH), acc(D)],
        out_shape=[SDS((T, D), F32), SDS((T, 2 * FH), BF16), SDS((8, 2 * FH), F32), SDS((8, D), F32)],
        scratch_shapes=[pltpu.VMEM((8, 2 * FH), F32)],
        compiler_params=_cparams(1),
    )(dx3, x2, g2, upp, up, wf, wdn, wup_t)


def _merge_bwd(dx2, pm, u, ycin, o, wc, bg, wo, woc_t, woa_t, S, tm, exchange):
    T, D = dx2.shape
    CW, AW = woc_t.shape[1], woa_t.shape[1]
    n, tps = T // tm, S // tm
    nx = len(exchange)
    rb = D // N_DEV
    assert CW + AW == D

    def body(dx2_ref, cb_ref, cc_ref, cin_ref, gc_ref, ga_ref, u_ref, ycin_ref, o_ref,
             wc_ref, bg_ref, wo_ref, woct_ref, woat_ref, *rest):
        x_refs = rest[:nx]
        dconv_ref, dgl_ref, do_ref, dws_ref, dwc_ref, dbg_ref = rest[nx:nx + 6]
        got_refs, head_ref, sems = rest[nx + 6:2 * nx + 6], rest[2 * nx + 6], rest[2 * nx + 7:]

        def add_blocks(rows, cols, grad):
            for d in range(N_DEV):
                dws_ref[d, rows, cols] += grad[d * rb:(d + 1) * rb, :]

        i = pl.program_id(0)
        t = n - 1 - i

        @pl.when(i == 0)
        def _():
            for cp in _sibling_copies(x_refs, got_refs, *sems):
                cp.start()

        @pl.when(i == 0)
        def _():
            for ref in (dwc_ref, dbg_ref, dws_ref):
                ref[...] = jnp.zeros_like(ref)

        @pl.when(t % tps == tps - 1)
        def _():
            head_ref[...] = jnp.zeros_like(head_ref)

        dx2b = dx2_ref[...].astype(BF16)
        dm = _dot_nt(dx2b, wo_ref[...])
        outs, merged = [], 0.0
        branches = ((gc_ref, _dot_nt(ycin_ref[...], woct_ref[...]), slice(0, D)),
                    (ga_ref, _dot_nt(o_ref[...], woat_ref[...]), slice(D, 2 * D)))
        for g_ref, y, cols in branches:
            g = _sigmoid(g_ref[...].astype(F32) + bg_ref[:, cols])
            dgl = dm * y * g * (1.0 - g)
            dgl_ref[:, cols] = dgl.astype(BF16)
            dbg_ref[0:1, cols] += jnp.sum(dgl, axis=0, keepdims=True)
            outs.append((dm * g).astype(BF16))
            merged = merged + g * y
        dyc, dya = outs
        add_blocks(slice(0, rb), slice(0, D), _dot_tn(merged.astype(BF16), dx2b))
        add_blocks(slice(rb, 2 * rb), slice(0, CW), _dot_tn(dyc, ycin_ref[...]))
        add_blocks(slice(rb, 2 * rb), slice(CW, CW + AW), _dot_tn(dya, o_ref[...]))
        do_ref[...] = _dot(dya, woat_ref[...]).astype(BF16)
        dycin = _dot(dyc, woct_ref[...])
        cc, cin = cc_ref[...].astype(F32), cin_ref[...].astype(F32)
        z = cc * cin
        du = dycin * cb_ref[...].astype(F32)
        du1, du2 = _taps_ahead(du, head_ref[0:1, :], head_ref[1:2, :])
        head_ref[...] = du[0:8, :]
        dwc_ref[2:3, :] += jnp.sum(z * du, axis=0, keepdims=True)
        dwc_ref[1:2, :] += jnp.sum(z * du1, axis=0, keepdims=True)
        dwc_ref[0:1, :] += jnp.sum(z * du2, axis=0, keepdims=True)
        dz = (du * wc_ref[2:3, :] + du1 * wc_ref[1:2, :]) + du2 * wc_ref[0:1, :]
        dconv_ref[:, 0:CW] = (dycin * u_ref[...].astype(F32)).astype(BF16)
        dconv_ref[:, CW:2 * CW] = (dz * cin).astype(BF16)
        dconv_ref[:, 2 * CW:3 * CW] = (dz * cc).astype(BF16)

        @pl.when(i == n - 1)
        def _():
            for cp in _sibling_copies(x_refs, got_refs, *sems):
                cp.wait()

    g_off = (3 * CW + 3 * AW) // D
    tok = lambda w, j=0: pl.BlockSpec((tm, w), lambda i: (n - 1 - i, j))
    acc = lambda w: pl.BlockSpec((8, w), lambda i: (0, 0))
    return pl.pallas_call(
        body, name="merge_bwd", grid=(n,),
        in_specs=[tok(D), tok(CW, 0), tok(CW, 1), tok(CW, 2), tok(D, g_off), tok(D, g_off + 1), tok(CW),
                  tok(CW), tok(AW),
                  _resident((8, CW)), _resident((1, 2 * D)), _resident((D, D)), _resident((D, CW)), _resident((D, AW))]
                 + [ANY] * nx,
        out_specs=[tok(3 * CW), tok(2 * D), tok(AW), pl.BlockSpec((N_DEV, 2 * rb, D), lambda i: (0, 0, 0)), acc(CW),
                   acc(2 * D)] + [ANY] * nx,
        out_shape=[SDS((T, 3 * CW), BF16), SDS((T, 2 * D), BF16), SDS((T, AW), BF16), SDS((N_DEV, 2 * rb, D), F32),
                   SDS((8, CW), F32), SDS((8, 2 * D), F32)] + _sibling_shapes(exchange),
        scratch_shapes=[pltpu.VMEM((8, CW), F32)] + _exchange_sems(nx),
        compiler_params=_cparams(1),
    )(dx2, pm, pm, pm, pm, pm, u, ycin, o, wc, bg, wo, woc_t, woa_t, *exchange)


def _attn_bwd(pm, do, fcum, frow, lse, S, q_off, AW, hd, tq, exchange):
    T = pm.shape[0]
    scale = float(hd) ** -0.5
    nq, hpg, ng = S // tq, LANES // hd, AW // LANES
    nx = len(exchange)
    steps = (T // S) * ng

    def body(q_ref, k_ref, v_ref, do_ref, fc_ref, fr_ref, lse_ref, *rest):
        x_refs, (dq_ref, dk_ref, dv_ref, dfr_ref) = rest[:nx], rest[nx:nx + 4]
        got_refs, (dk_acc, dv_acc), sems = rest[nx + 4:2 * nx + 4], rest[2 * nx + 4:2 * nx + 6], rest[2 * nx + 6:]
        g = pl.program_id(1)
        step = pl.program_id(0) * ng + g

        @pl.when(step == 0)
        def _():
            for cp in _chip_copies(x_refs, got_refs, *sems):
                cp.start()

        dk_acc[...] = jnp.zeros_like(dk_acc)
        dv_acc[...] = jnp.zeros_like(dv_acc)
        dfr_ref[...] = jnp.zeros_like(dfr_ref)
        for i in range(nq):
            rs, kend = slice(i * tq, (i + 1) * tq), (i + 1) * tq
            row = i * tq + lax.broadcasted_iota(jnp.int32, (tq, kend), 0)
            col = lax.broadcasted_iota(jnp.int32, (tq, kend), 1)
            kk, vv = k_ref[0:kend, :], v_ref[0:kend, :]
            dq_tile = jnp.zeros((tq, LANES), F32)
            for e in range(hpg):
                mask = _head_mask(e, hd)
                qs = jnp.where(mask, q_ref[rs, :], 0) * scale
                doi = jnp.where(mask, do_ref[rs, :], 0)
                s = _dot_nt(qs, kk) + _lane_pick(fc_ref[rs, :], g * hpg + e) - fr_ref[e:e + 1, 0:kend]
                p = jnp.where(col <= row, jnp.exp(s - _lane_pick(lse_ref[rs, :], g * hpg + e)), 0.0)
                dp = _dot_nt(doi, vv)
                ds = p * (dp - jnp.sum(p * dp, axis=1, keepdims=True))
                pb, dsb = p.astype(BF16), ds.astype(BF16)
                dq_tile = jnp.where(mask, _dot(dsb, kk) * scale, dq_tile)
                dv_acc[0:kend, :] += _dot_tn(pb, doi)
                dk_acc[0:kend, :] += _dot_tn(dsb, qs)
                dfr_ref[e:e + 1, 0:kend] -= jnp.sum(ds, axis=0, keepdims=True)
            dq_ref[rs, :] = dq_tile.astype(BF16)
        dk_ref[...] = dk_acc[...].astype(BF16)
        dv_ref[...] = dv_acc[...].astype(BF16)

        @pl.when(step == steps - 1)
        def _():
            for cp in _chip_copies(x_refs, got_refs, *sems):
                cp.wait()

    full = pl.BlockSpec((S, LANES), lambda b, g: (b, 0))
    grp = pl.BlockSpec((S, LANES), lambda b, g: (b, g))
    rows = pl.BlockSpec((None, None, 8, S), lambda b, g: (b, g, 0, 0))
    return pl.pallas_call(
        body, name="attn_bwd", grid=(T // S, ng),
        in_specs=_attn_specs(S, q_off, AW) + [grp, full, rows, full] + [ANY] * nx,
        out_specs=[grp, grp, grp, rows] + [ANY] * nx,
        out_shape=[SDS((T, AW), BF16), SDS((T, AW), BF16), SDS((T, AW), BF16), SDS((T // S, ng, 8, S), F32)]
                  + _chip_shapes(exchange),
        scratch_shapes=[pltpu.VMEM((S, LANES), F32), pltpu.VMEM((S, LANES), F32)] + _exchange_sems(3 * nx),
        compiler_params=_cparams(2),
    )(pm, pm, pm, do, fcum, frow, lse, *exchange)


def _forget_bwd(dfc, fl, bf, S):
    T, NF = fl.shape
    ch = _tile(S, 256, 8)

    def body(df_ref, fl_ref, bf_ref, dfl_ref, dbf_ref):
        @pl.when(pl.program_id(0) == 0)
        def _():
            dbf_ref[...] = jnp.zeros_like(dbf_ref)

        row = lax.broadcasted_iota(jnp.int32, (ch, ch), 0)
        col = lax.broadcasted_iota(jnp.int32, (ch, ch), 1)
        tri = (col >= row).astype(BF16)
        carry = jnp.zeros((1, NF), F32)
        for c in range(S - ch, -1, -ch):
            d = df_ref[c:c + ch, :]
            dlf = _tri_dot(tri, d) + carry
            carry = carry + jnp.sum(d, axis=0, keepdims=True)
            dfl = dlf * _sigmoid(-(fl_ref[c:c + ch, :] + bf_ref[...]))
            dfl_ref[c:c + ch, :] = dfl.astype(BF16)
            dbf_ref[0:1, :] += jnp.sum(dfl, axis=0, keepdims=True)

    return pl.pallas_call(
        body, name="forget_bwd", grid=(T // S,),
        in_specs=[pl.BlockSpec((S, NF), lambda b: (b, 0)), pl.BlockSpec((S, NF), lambda b: (b, 0)), _resident((1, NF))],
        out_specs=[pl.BlockSpec((S, NF), lambda b: (b, 0)), pl.BlockSpec((8, NF), lambda b: (0, 0))],
        out_shape=[SDS((T, NF), BF16), SDS((8, NF), F32)],
        compiler_params=_cparams(1),
    )(dfc, fl, bf)


def _inproj_bwd(dparts, offs, w_t, x, g1, dx2, tm, exchange):
    T, D = x.shape
    npart, nx = len(dparts), len(exchange)
    n = T // tm

    def body(*refs):
        d_refs = refs[:npart]
        w_ref, x_ref, g_ref, dx2_ref = refs[npart:npart + 4]
        x_refs = refs[npart + 4:npart + 4 + nx]
        dx_ref, dg_ref, dwl_ref = refs[npart + 4 + nx:npart + 7 + nx]
        got_refs, sems = refs[npart + 7 + nx:npart + 7 + 2 * nx], refs[npart + 7 + 2 * nx:]

        @pl.when(pl.program_id(0) == 0)
        def _():
            for cp in _chip_copies(x_refs, got_refs, *sems):
                cp.start()

        @pl.when(pl.program_id(0) == 0)
        def _():
            dg_ref[...] = jnp.zeros_like(dg_ref)
            dwl_ref[...] = jnp.zeros_like(dwl_ref)

        dh = None
        for d_ref, off in zip(d_refs, offs):
            term = _dot(d_ref[...], w_ref[off:off + d_ref.shape[1], :])
            dh = term if dh is None else dh + term
        xv = x_ref[...]
        r = _rms(xv)
        xn = xv * r
        dg_ref[0:1, :] += jnp.sum(dh * xn, axis=0, keepdims=True)
        dwl_ref[...] += _dot_tn(d_refs[-1][...], (xn * g_ref[...]).astype(BF16))
        dxn = dh * g_ref[...]
        dx_ref[...] = dx2_ref[...] + r * (dxn - xn * jnp.mean(dxn * xn, axis=-1, keepdims=True))

        @pl.when(pl.program_id(0) == n - 1)
        def _():
            for cp in _chip_copies(x_refs, got_refs, *sems):
                cp.wait()

    tok = lambda w: pl.BlockSpec((tm, w), lambda i: (i, 0))
    nl = dparts[-1].shape[1]
    return pl.pallas_call(
        body, name="inproj_bwd", grid=(n,),
        in_specs=[tok(d.shape[1]) for d in dparts] + [_resident(w_t.shape), tok(D), _resident((1, D)), tok(D)] + [ANY] * nx,
        out_specs=[tok(D), pl.BlockSpec((8, D), lambda i: (0, 0)), pl.BlockSpec((nl, D), lambda i: (0, 0))] + [ANY] * nx,
        out_shape=[SDS((T, D), F32), SDS((8, D), F32), SDS((nl, D), F32)] + _chip_shapes(exchange),
        scratch_shapes=_exchange_sems(3 * nx),
        compiler_params=_cparams(1),
    )(*dparts, w_t, x, g1, dx2, *exchange)


def _wgrad(bs, a, name, into=None, row_off=0, total_rows=None):
    bs = list(bs) if isinstance(bs, (list, tuple)) else [bs]
    P = len(bs)
    T, N = bs[0].shape
    M = a.shape[1]
    tn = _tile(N, 1408 if M <= 1024 else 512)
    while row_off % tn:
        tn = _tile(N, tn - LANES)
    tk = _tile(T, 512, 16)
    per_row = 2 * (P * tn * bs[0].dtype.itemsize + M * a.dtype.itemsize)
    while T % (2 * tk) == 0 and 2 * tk * per_row + 2 * tn * M * 4 <= WGRAD_VMEM_BUDGET:
        tk *= 2
    blk0, nj, nk = row_off // tn, N // tn, T // tk

    def body(*refs):
        b_refs, a_ref, o_ref = refs[:P], refs[P], refs[-1]
        j = pl.program_id(0)

        @pl.when(pl.program_id(1) == 0)
        def _():
            o_ref[...] = jnp.zeros_like(o_ref)

        for p, b_ref in enumerate(b_refs):
            @pl.when((j >= p * nj) & (j < (p + 1) * nj))
            def _(b_ref=b_ref):
                o_ref[...] += _dot_tn(b_ref[...].astype(BF16), a_ref[...].astype(BF16))

    def b_spec(p):
        def index(j, k):
            before, mine = j < p * nj, (j >= p * nj) & (j < (p + 1) * nj)
            return (jnp.where(mine, k, jnp.where(before, 0, nk - 1)),
                    jnp.where(mine, j - p * nj, jnp.where(before, 0, nj - 1)))
        return pl.BlockSpec((tk, tn), index)

    in_specs = [b_spec(p) for p in range(P)] + [pl.BlockSpec((tk, M), lambda j, k: (k, 0))]
    args = (*bs, a)
    kwargs = {}
    if into is not None:
        in_specs.append(ANY)
        args += (into,)
        kwargs["input_output_aliases"] = {P + 1: 0}
        total_rows = into.shape[0]
    return pl.pallas_call(
        body, name=name, grid=(P * nj, nk),
        in_specs=in_specs,
        out_specs=pl.BlockSpec((tn, M), lambda j, k: (blk0 + j, 0)),
        out_shape=SDS((P * N if total_rows is None else total_rows, M), F32),
        compiler_params=_cparams(2), **kwargs,
    )(*args)


def _adamw(w, g, m, v, name, recv=None):
    shape = w.shape
    C = shape[-1]
    w2, g2, m2, v2 = (a.reshape(-1, C) for a in (w, g, m, v))
    R = w2.shape[0]
    tr = R if R <= 512 else _tile(R, 256, 8)
    if tr < 64:
        tr = R

    def body(w_ref, g_ref, m_ref, v_ref, *rest):
        d_ref, nm_ref, nv_ref = rest[-3:] if recv is None else rest[1:4]
        gv = g_ref[...]
        if recv is not None:
            r_ref, gs_ref = rest[0], rest[4]
            gv = ((gv + r_ref[0].astype(F32)) + r_ref[1].astype(F32)) + r_ref[2].astype(F32)
            gs_ref[...] = gv
        mv = ADAM_B1 * m_ref[...] + (1.0 - ADAM_B1) * gv
        vv = ADAM_B2 * v_ref[...] + (1.0 - ADAM_B2) * (gv * gv)
        m_hat = mv / (1.0 - ADAM_B1 ** ADAM_STEP)
        v_hat = vv / (1.0 - ADAM_B2 ** ADAM_STEP)
        d_ref[...] = -ADAM_LR * (m_hat / (jnp.sqrt(v_hat) + ADAM_EPS) + ADAM_WD * w_ref[...])
        nm_ref[...] = mv
        nv_ref[...] = vv

    spec = pl.BlockSpec((tr, C), lambda r: (r, 0))
    extra = [] if recv is None else [pl.BlockSpec((3, tr, C), lambda r: (0, r, 0))]
    n_out = 3 if recv is None else 4
    outs = pl.pallas_call(
        body, name=name, grid=(R // tr,),
        in_specs=[spec] * 4 + extra, out_specs=[spec] * n_out, out_shape=[SDS((R, C), F32)] * n_out,
        compiler_params=_cparams(1),
    )(w2, g2, m2, v2, *([] if recv is None else [recv]))
    return tuple(o.reshape(shape) for o in outs)


def _rows(a, L):
    lead = a.shape[0]
    flat = a.reshape(lead, -1)
    n = flat.shape[1]
    r = -(-n // L)
    return jnp.pad(flat, ((0, 0), (0, r * L - n))).reshape(lead, r, L)


def _unrows(p, shape):
    return p.reshape(-1)[:int(np.prod(shape))].reshape(shape)


def _from_col_blocks(a):
    n, R, c = a.shape
    return a.transpose(1, 0, 2).reshape(R, n * c)


def _in_windows(n_loc, nqkv, H, NM):
    win = (n_loc + 7 + 7) // 8 * 8
    starts, index = [], np.zeros((N_DEV, n_loc), np.int32)
    for d in range(N_DEV):
        rows = np.arange(n_loc * d, n_loc * (d + 1))
        is_f = (rows >= nqkv) & (rows < nqkv + H)
        kept = np.where(rows < nqkv, rows, rows - H)
        lo = int(kept[~is_f].min())
        start = lo // 8 * 8
        assert int(kept[~is_f].max()) - start < win and start + win <= NM + LANES
        starts.append(start)
        index[d] = np.where(is_f, win + rows - nqkv, kept - start)
    return starts, win, index


def kernel(x, norm_mix_g, w_in, b_f, b_gate, conv_mix_w, w_out_conv, w_out_attn, w_o, norm_ffn_g, w_up, conv_ffn_w, w_down, norm_f_g, loss_target, m_norm_mix_g, m_w_in, m_b_f, m_b_gate, m_conv_mix_w, m_w_out_conv, m_w_out_attn, m_w_o, m_norm_ffn_g, m_w_up, m_conv_ffn_w, m_w_down, m_norm_f_g, v_norm_mix_g, v_w_in, v_b_f, v_b_gate, v_conv_mix_w, v_w_out_conv, v_w_out_attn, v_w_o, v_norm_ffn_g, v_w_up, v_conv_ffn_w, v_w_down, v_norm_f_g):
    Bl, S, D = x.shape
    T = Bl * S
    H = b_f.shape[-1]
    CW = N_DEV * conv_mix_w.shape[-1]
    AW = w_out_attn.shape[1]
    hd = AW // H
    FH = N_DEV * w_down.shape[1]
    n_loc = w_in.shape[-1]
    NIN = N_DEV * n_loc
    NM = 3 * CW + 3 * AW + 2 * D
    nqkv = 3 * CW + 3 * AW
    assert NIN == NM + H and w_out_conv.shape[1] == CW and nqkv % D == 0 and AW % LANES == 0 and LANES % hd == 0
    hpg, ng = LANES // hd, AW // LANES
    assert hpg <= 8
    tm_big = min(512, S // 2)
    tm_ffn = min(256, S // 2)
    tq_fwd = min(512, S // 2)
    tq_bwd = min(256, S // 2)
    px, py, pc = _place()
    me = 4 * px + 2 * py + pc

    bits = lambda a: lax.bitcast_convert_type(a, BF16)
    taps = jnp.concatenate([_rows(bits(conv_ffn_w[0])[None], D)[0], _rows(bits(conv_mix_w[0])[None], D)[0]], axis=0)
    n_ffn_rows = -(-conv_ffn_w[0].size * 2 // D)
    late = [w_up[0].T.astype(BF16), w_down[0].astype(BF16), w_o[0].astype(BF16), w_out_conv[0].T.astype(BF16),
            w_out_attn[0].T.astype(BF16), taps]
    g_in, = _all_gather([w_in[0].T.astype(BF16)], "weights_all_gather")
    W_in_rows = g_in.reshape(NIN, D)
    W_in_t = jnp.concatenate([W_in_rows[:nqkv], W_in_rows[nqkv + H:], W_in_rows[nqkv:nqkv + H],
                              jnp.zeros((LANES - H, D), BF16)], axis=0)
    bf128 = jnp.pad(b_f, ((0, 0), (0, LANES - H)))

    x2d = x.reshape(T, D)
    tgt = loss_target.reshape(T, D)
    pm, fl, h1 = _inproj_fwd(x2d, norm_mix_g, W_in_t, NM, tm_big)
    fcum = _forget_cumsum(fl, bf128, S)
    frow = jnp.pad(fcum[:, :H].reshape(Bl, S, ng, hpg).transpose(0, 2, 3, 1), ((0, 0), (0, 0), (0, 8 - hpg), (0, 0)))
    q_off = 3 * CW // LANES
    o, lse, g_up, g_dn, g_o, g_oc, g_oa, g_taps = _attn_fwd(pm, fcum, frow, S, q_off, AW, hd, tq_fwd, late)
    W_up_t = g_up.reshape(2 * FH, D)
    W_dn = g_dn.reshape(FH, D)
    W_o = g_o.reshape(D, D)
    W_oc_t = g_oc.reshape(D, CW)
    W_oa_t = g_oa.reshape(D, AW)
    tap_bits = g_taps.reshape(N_DEV, -1)
    n_ffn, n_mix = conv_ffn_w[0].size * 2, conv_mix_w[0].size * 2
    wf_full = _from_col_blocks(lax.bitcast_convert_type(
        tap_bits[:, :n_ffn].reshape((N_DEV,) + conv_ffn_w.shape[1:] + (2,)), F32))
    wc_full = _from_col_blocks(lax.bitcast_convert_type(
        tap_bits[:, n_ffn_rows * D:n_ffn_rows * D + n_mix].reshape((N_DEV,) + conv_mix_w.shape[1:] + (2,)), F32))
    wf8 = jnp.pad(wf_full, ((0, 5), (0, 0)))
    wc8 = jnp.pad(wc_full, ((0, 5), (0, 0)))
    x2, u, ycin = _merge_fwd(pm, o, x2d, wc8, b_gate, W_oc_t, W_oa_t, W_o, S, tm_big)
    upp, up, act, h2, dx3, loss8, dgf8 = _ffn_fwd(x2, norm_ffn_g, W_up_t, wf8, W_dn, norm_f_g.reshape(1, D), tgt, S, tm_ffn)

    dx2, dupp, dwf8, dg2_8 = _ffn_bwd(dx3, x2, norm_ffn_g, upp, up, wf8, W_dn, W_up_t, S, tm_ffn)
    dW_dn = _wgrad(act, dx3, "wgrad_down")
    dW_up_t = _wgrad(dupp, h2, "wgrad_up")
    ids = jnp.stack([pc, 2 * px + py]).astype(jnp.int32)
    big = [dW_up_t.reshape(4, 2, -1, D), dW_dn.reshape(4, 2, -1, D)]
    dconv, dgl, do, dW_small, dwc8, dbg8, *sib_big = _merge_bwd(
        dx2, pm, u, ycin, o, wc8, b_gate, W_o, W_oc_t, W_oa_t, S, tm_big, big)
    big_sums = [_pair_sum(b, r, ids, "grads_pair_sum_%d" % a) for a, (b, r) in enumerate(zip(big, sib_big))]
    dq, dk, dv, dfr, *chips_big = _attn_bwd(pm, do, fcum, frow, lse, S, q_off, AW, hd, tq_bwd, [s[1] for s in big_sums])
    dfc = jnp.pad(dfr[:, :, :hpg, :].transpose(0, 3, 1, 2).reshape(T, H), ((0, 0), (0, LANES - H)))
    dfl, dbf8 = _forget_bwd(dfc, fl, bf128, S)
    dparts = [dconv, dq, dk, dv, dgl, dfl]
    offs = [0, 3 * CW, 3 * CW + AW, 3 * CW + 2 * AW, nqkv, NM]
    dW_in_t = _wgrad(dconv, h1, "wgrad_in_conv", total_rows=NM + LANES)
    dW_in_t = _wgrad([dq, dk, dv], h1, "wgrad_in_qkv", into=dW_in_t, row_off=3 * CW)
    dW_in_t = _wgrad(dgl, h1, "wgrad_in_gates", into=dW_in_t, row_off=nqkv)

    starts, win, index = _in_windows(n_loc, nqkv, H, NM)
    small = dW_small.reshape(4, 2, -1, D)
    my_starts = [jnp.where(pc == 0, starts[2 * j], starts[2 * j + 1]) for j in range(4)]
    win_ids = jnp.stack([pc, 2 * px + py] + my_starts).astype(jnp.int32)
    sib_win, sib_small = _sibling_exchange(dW_in_t, starts, win, [small], "grads_sibling_exchange")
    sums = [_pair_sum(dW_in_t, sib_win, win_ids, "grads_pair_sum_in", win_rows=win),
            _pair_sum(small, sib_small, ids, "grads_pair_sum_small")]
    grad_x, dg1_8, dW_forget, *from_chips = _inproj_bwd(dparts, offs, W_in_t, x2d, norm_mix_g, dx2, tm_big, [s[1] for s in sums])
    red_win, red_small = [_final_sum(s[0], r, "grads_final_sum_%d" % a) for a, (s, r) in enumerate(zip(sums, from_chips))]
    (own_up, _), (own_dn, _) = big_sums
    from_up, from_dn = chips_big

    f_rows = dW_forget[0:8]
    wf_rows = _rows(dwf8[:3].reshape(1, -1), D)[0]
    wc_rows = _rows(dwc8[:3].reshape(1, -1), D)[0]
    smalls = [dg1_8[0:1], dg2_8[0:1], dgf8[0:1], dbg8[0:1, :D], dbg8[0:1, D:], jnp.pad(dbf8[0:1], ((0, 0), (0, D - LANES))),
              jnp.pad(loss8[0:1], ((0, 0), (0, D - LANES))), jnp.zeros((1, D), F32), f_rows, wf_rows, wc_rows]
    spack = jnp.concatenate(smalls, axis=0)
    spack = jnp.pad(spack, ((0, -spack.shape[0] % 8), (0, 0)))
    ssum = _sum8(_all_gather([spack], "small_all_gather")[0], "small_sum")
    g_g1, g_g2, g_gf = ssum[0:1], ssum[1:2], ssum[2]
    g_bg = jnp.concatenate([ssum[3:4], ssum[4:5]], axis=1)
    g_bf = ssum[5:6, :H]
    r0 = 16
    r1 = r0 + wf_rows.shape[0]
    wf_sum = _unrows(ssum[r0:r1], (3, 2 * FH))
    wc_sum = _unrows(ssum[r1:r1 + wc_rows.shape[0]], (3, CW))
    g_wf = lax.dynamic_slice_in_dim(wf_sum, me * conv_ffn_w.shape[-1], conv_ffn_w.shape[-1], axis=1)
    g_wc = lax.dynamic_slice_in_dim(wc_sum, me * conv_mix_w.shape[-1], conv_mix_w.shape[-1], axis=1)

    ext = jnp.concatenate([red_win, ssum[8:8 + H]], axis=0)
    my_index = lax.dynamic_index_in_dim(jnp.asarray(index), me, axis=0, keepdims=False)
    g_w_in_t = jnp.take(ext, my_index, axis=0)
    rb = D // N_DEV
    g_w_o = red_small[:rb]
    g_w_oc = red_small[rb:, :CW].T
    g_w_oa = red_small[rb:, CW:].T

    loss = ssum[6, 0]

    names = ["norm_mix_g", "w_in", "b_f", "b_gate", "conv_mix_w", "w_out_conv", "w_out_attn", "w_o", "norm_ffn_g", "w_up",
             "conv_ffn_w", "w_down", "norm_f_g"]
    weights = [norm_mix_g, w_in, b_f, b_gate, conv_mix_w, w_out_conv, w_out_attn, w_o, norm_ffn_g, w_up, conv_ffn_w, w_down, norm_f_g]
    grads = [g_g1, g_w_in_t, g_bf, g_bg, g_wc, g_w_oc, g_w_oa, g_w_o, g_g2, own_up, g_wf, own_dn, g_gf]
    ms = [m_norm_mix_g, m_w_in, m_b_f, m_b_gate, m_conv_mix_w, m_w_out_conv, m_w_out_attn, m_w_o, m_norm_ffn_g, m_w_up,
          m_conv_ffn_w, m_w_down, m_norm_f_g]
    vs = [v_norm_mix_g, v_w_in, v_b_f, v_b_gate, v_conv_mix_w, v_w_out_conv, v_w_out_attn, v_w_o, v_norm_ffn_g, v_w_up,
          v_conv_ffn_w, v_w_down, v_norm_f_g]
    to_view = {"w_in": lambda a: a[0].T.reshape(-1, LANES), "w_up": lambda a: a[0].T}
    from_view = {"w_in": lambda a: a.reshape(n_loc, D).T[None], "w_up": lambda a: a.T[None]}
    received = {"w_up": from_up, "w_down": from_dn}
    out_grads, steps = [], []
    for nm, w, g, mm, vv in zip(names, weights, grads, ms, vs):
        view, back = to_view.get(nm, lambda a: a), from_view.get(nm, lambda a: a)
        wv, mv, vw = (view(a) for a in (w, mm, vv))
        gv = g.reshape(wv.shape)
        res = _adamw(wv, gv, mv, vw, "adamw_" + nm, recv=received.get(nm))
        if nm in received:
            gv = res[3]
        steps.append(tuple(back(o) for o in res[:3]))
        out_grads.append(back(gv))
    deltas, new_ms, new_vs = zip(*steps)
    return (loss, grad_x.reshape(Bl, S, D), *out_grads, *deltas, *new_ms, *new_vs)
```
